```python
import jax, jax.numpy as jnp
from jax import lax
import numpy as np

D_MODEL = 1024
BATCH = 2
SEQ = 8192
DEPTH = 1

D_MIX = D_MODEL
ATT_HEADS = 8
ATT_HEAD_DIM = 64
ATT_WIDTH = ATT_HEADS * ATT_HEAD_DIM
DILATED_PATTERNS = ((128, 1), (512, 4), (2048, 16))
Q_BLOCK = 128
ROPE_THETA = 10000.0
GLA_HEADS = 4
GLA_KEY_DIM = 64
GLA_VAL_DIM = 128
GLA_KEY_WIDTH = GLA_HEADS * GLA_KEY_DIM
GLA_VAL_WIDTH = GLA_HEADS * GLA_VAL_DIM
GLA_GATE_RANK = 16
GLA_GATE_NORMALIZER = 16.0
GLA_CHUNK = 64
IN_SPLITS = (ATT_WIDTH, ATT_WIDTH, ATT_WIDTH, GLA_KEY_WIDTH, GLA_KEY_WIDTH,
             GLA_VAL_WIDTH, GLA_VAL_WIDTH, GLA_GATE_RANK)
D_IN = sum(IN_SPLITS)
N_EXPERTS = 32
TOP_K = 4
D_FF = D_MODEL
SWIGLU_LIMIT = 7.0
SWIGLU_ALPHA = 1.702
MOE_BLOCK = 128
NORM_EPS = 1e-6
N_MOD = 6

kernel_name = "hybrid_dilated_gla_moe_block"


def rms_norm(x, g):
    xf = x.astype(jnp.float32)
    y = xf * lax.rsqrt(jnp.mean(xf * xf, axis=-1, keepdims=True) + NORM_EPS)
    return (y * g.astype(jnp.float32)).astype(x.dtype)


def rope(x, positions):
    half = x.shape[-1] // 2
    inv_freq = ROPE_THETA ** (-jnp.arange(half, dtype=jnp.float32) / half)
    ang = positions.astype(jnp.float32)[:, None] * inv_freq[None, :]
    cos, sin = jnp.cos(ang), jnp.sin(ang)
    xf = x.astype(jnp.float32)
    x1, x2 = xf[..., :half], xf[..., half:]
    return jnp.concatenate([x1 * cos - x2 * sin, x2 * cos + x1 * sin], axis=-1).astype(x.dtype)


def dilated_attention(q, k, v):
    B, H, S, hd = q.shape
    n_blk = S // Q_BLOCK

    def block(i):
        q0 = i * Q_BLOCK
        qb = lax.dynamic_slice_in_dim(q, q0, Q_BLOCK, axis=2).astype(jnp.float32)
        t = q0 + jnp.arange(Q_BLOCK)
        outs, lses = [], []
        for window, dil in DILATED_PATTERNS:
            offs = jnp.arange(window // dil + 1) * dil
            idx = t[:, None] - offs[None, :]
            valid = idx >= 0
            idx = jnp.maximum(idx, 0)
            kg = jnp.take(k, idx, axis=2).astype(jnp.float32)
            vg = jnp.take(v, idx, axis=2).astype(jnp.float32)
            s = jnp.einsum('bhqd,bhqjd->bhqj', qb, kg)
            s = jnp.where(valid, s, -jnp.inf)
            m = jnp.max(s, axis=-1, keepdims=True)
            p = jnp.exp(s - m)
            l = jnp.sum(p, axis=-1, keepdims=True)
            outs.append(jnp.einsum('bhqj,bhqjd->bhqd', p, vg) / l)
            lses.append(m + jnp.log(l))
        w = jax.nn.softmax(jnp.stack(lses, axis=0), axis=0)
        return jnp.sum(w * jnp.stack(outs, axis=0), axis=0)

    o = lax.map(block, jnp.arange(n_blk))
    o = o.transpose(1, 0, 3, 2, 4).reshape(B, S, H * hd)
    return o.astype(q.dtype)


def gated_linear_attention(q, k, v, log_a):
    B, H, S, dk = q.shape
    dv = v.shape[-1]
    C = GLA_CHUNK
    N = S // C
    qf = q.astype(jnp.float32).reshape(B, H, N, C, dk) * (dk ** -0.5)
    kf = k.astype(jnp.float32).reshape(B, H, N, C, dk)
    vf = v.astype(jnp.float32).reshape(B, H, N, C, dv)
    b = jnp.cumsum(log_a.reshape(B, H, N, C, dk), axis=3)
    b_last = b[:, :, :, -1:, :]
    q_dec = qf * jnp.exp(b)
    att = jnp.einsum('bhncd,bhnjd->bhncj', q_dec, kf * jnp.exp(-b))
    causal = jnp.tril(jnp.ones((C, C), dtype=bool))
    att = jnp.where(causal, att, 0.0)
    o_intra = jnp.einsum('bhncj,bhnje->bhnce', att, vf)
    u = jnp.einsum('bhncd,bhnce->nbhde', kf * jnp.exp(b_last - b), vf)
    a = jnp.exp(b_last[:, :, :, 0, :]).transpose(2, 0, 1, 3)

    def step(state, inp):
        a_n, u_n = inp
        return a_n[..., None] * state + u_n, state

    _, s_prev = lax.scan(step, jnp.zeros((B, H, dk, dv), jnp.float32), (a, u))
    o_inter = jnp.einsum('bhncd,nbhde->bhnce', q_dec, s_prev)
    return (o_intra + o_inter).reshape(B, H, S, dv)


def hybrid_mixer(h, w_in, w_gate_lr, b_gate, g_gla, w_out):
    B, S, _ = h.shape
    proj = h @ w_in
    split_points = [int(s) for s in np.cumsum(IN_SPLITS)[:-1]]
    q_a, k_a, v_a, q_g, k_g, v_g, r_g, g_lr = jnp.split(proj, split_points, axis=-1)

    def heads(t, n):
        return t.reshape(B, S, n, -1).transpose(0, 2, 1, 3)

    pos = jnp.arange(S)
    qa = rope(heads(q_a, ATT_HEADS), pos) * (ATT_HEAD_DIM ** -0.5)
    ka = rope(heads(k_a, ATT_HEADS), pos)
    o_att = dilated_attention(qa, ka, heads(v_a, ATT_HEADS))
    gate_logit = (g_lr @ w_gate_lr + b_gate).astype(jnp.float32)
    log_a = jax.nn.log_sigmoid(gate_logit) / GLA_GATE_NORMALIZER
    o_g = gated_linear_attention(heads(q_g, GLA_HEADS), heads(k_g, GLA_HEADS),
                                 heads(v_g, GLA_HEADS), heads(log_a, GLA_HEADS))
    o_g = rms_norm(o_g, g_gla)
    o_g = o_g.transpose(0, 2, 1, 3).reshape(B, S, GLA_VAL_WIDTH).astype(h.dtype) * jax.nn.silu(r_g)
    return jnp.concatenate([o_att, o_g], axis=-1) @ w_out


def moe_ffn(h, router_w, router_b, w_gate_up, b_gate_up, w_down, b_down):
    B, S, D = h.shape
    T = B * S
    xt = h.reshape(T, D)
    logits = (xt @ router_w + router_b).astype(jnp.float32)
    top_v, top_i = lax.top_k(logits, TOP_K)
    gates = jax.nn.softmax(top_v, axis=-1)
    A = T * TOP_K
    e_flat = top_i.reshape(A)
    tok_flat = jnp.arange(A, dtype=jnp.int32) // TOP_K
    g_flat = gates.reshape(A)
    order = jnp.argsort(e_flat)
    e_sorted = e_flat[order]
    counts = jnp.bincount(e_flat, length=N_EXPERTS)
    start = jnp.cumsum(counts) - counts
    padded = (counts + MOE_BLOCK - 1) // MOE_BLOCK * MOE_BLOCK
    pend = jnp.cumsum(padded)
    pstart = pend - padded
    dest = pstart[e_sorted] + jnp.arange(A, dtype=jnp.int32) - start[e_sorted]
    n_blocks = -(-(A + N_EXPERTS * (MOE_BLOCK - 1)) // MOE_BLOCK)
    P = n_blocks * MOE_BLOCK
    tok_buf = jnp.full((P,), T, dtype=jnp.int32).at[dest].set(tok_flat[order])
    gate_buf = jnp.zeros((P,), jnp.float32).at[dest].set(g_flat[order])
    blk_e = jnp.minimum(jnp.searchsorted(pend, jnp.arange(n_blocks) * MOE_BLOCK, side='right'),
                        N_EXPERTS - 1)
    xpad = jnp.concatenate([xt, jnp.zeros((1, D), xt.dtype)], axis=0)

    def expert_block(args):
        idx, e = args
        xb = xpad[idx]
        gu = xb @ w_gate_up[e] + b_gate_up[e]
        x_glu = jnp.minimum(gu[:, ::2], SWIGLU_LIMIT)
        x_lin = jnp.clip(gu[:, 1::2], -SWIGLU_LIMIT, SWIGLU_LIMIT)
        act = x_glu * jax.nn.sigmoid(SWIGLU_ALPHA * x_glu) * (x_lin + 1.0)
        return act @ w_down[e] + b_down[e]

    out_buf = lax.map(expert_block, (tok_buf.reshape(n_blocks, MOE_BLOCK), blk_e))
    out_buf = out_buf.reshape(P, D) * gate_buf[:, None].astype(out_buf.dtype)
    out = jnp.zeros((T + 1, D), out_buf.dtype).at[tok_buf].add(out_buf)
    return out[:T].reshape(B, S, D)


def setup_inputs(seed: int = 0) -> dict:
    key = jax.random.key(seed)
    ks = jax.random.split(key, 20)
    L = DEPTH

    def nrm(k, shape, scale):
        return jax.random.normal(k, shape, jnp.float32) * scale

    return {
        "x": nrm(ks[0], (BATCH, SEQ, D_MODEL), 1.0),
        "c": nrm(ks[1], (BATCH, D_MODEL), 1.0),
        "w_mod": nrm(ks[2], (L, D_MODEL, N_MOD * D_MODEL), 0.5 * D_MODEL ** -0.5),
        "b_mod": nrm(ks[3], (L, N_MOD * D_MODEL), 0.02),
        "g_pre_mix": 1.0 + nrm(ks[4], (L, D_MODEL), 0.05),
        "w_in": nrm(ks[5], (L, D_MODEL, D_IN), D_MODEL ** -0.5),
        "w_gate_lr": nrm(ks[6], (L, GLA_GATE_RANK, GLA_KEY_WIDTH), GLA_GATE_RANK ** -0.5),
        "b_gate": nrm(ks[7], (L, GLA_KEY_WIDTH), 0.1),
        "g_gla": 1.0 + nrm(ks[8], (L, GLA_VAL_DIM), 0.05),
        "w_out": nrm(ks[9], (L, D_MIX, D_MODEL), D_MIX ** -0.5),
        "g_post_mix": 1.0 + nrm(ks[10], (L, D_MODEL), 0.05),
        "g_pre_ffn": 1.0 + nrm(ks[11], (L, D_MODEL), 0.05),
        "router_w": nrm(ks[12], (L, D_MODEL, N_EXPERTS), D_MODEL ** -0.5),
        "router_b": nrm(ks[13], (L, N_EXPERTS), 0.01),
        "w_gate_up": nrm(ks[14], (L, N_EXPERTS, D_MODEL, 2 * D_FF), D_MODEL ** -0.5),
        "b_gate_up": nrm(ks[15], (L, N_EXPERTS, 2 * D_FF), 0.02),
        "w_down": nrm(ks[16], (L, N_EXPERTS, D_FF, D_MODEL), D_FF ** -0.5),
        "b_down": nrm(ks[17], (L, N_EXPERTS, D_MODEL), 0.02),
        "g_post_ffn": 1.0 + nrm(ks[18], (L, D_MODEL), 0.05),
    }


def reference(x, c, w_mod, b_mod, g_pre_mix, w_in, w_gate_lr, b_gate, g_gla, w_out,
              g_post_mix, g_pre_ffn, router_w, router_b, w_gate_up, b_gate_up,
              w_down, b_down, g_post_ffn):
    for l in range(DEPTH):
        mod = jax.nn.silu(c) @ w_mod[l] + b_mod[l]
        shift1, scale1, gate1, shift2, scale2, gate2 = jnp.split(mod[:, None, :], N_MOD, axis=-1)
        h = rms_norm(x, g_pre_mix[l]) * (1.0 + scale1) + shift1
        y = hybrid_mixer(h, w_in[l], w_gate_lr[l], b_gate[l], g_gla[l], w_out[l])
        x = x + gate1 * rms_norm(y, g_post_mix[l])
        h = rms_norm(x, g_pre_ffn[l]) * (1.0 + scale2) + shift2
        y = moe_ffn(h, router_w[l], router_b[l], w_gate_up[l], b_gate_up[l], w_down[l], b_down[l])
        x = x + gate2 * rms_norm(y, g_post_ffn[l])
    return x
```

```python
import functools

import numpy as np
import jax
import jax.numpy as jnp
from jax import lax
from jax.experimental import pallas as pl
from jax.experimental.pallas import tpu as pltpu

D_MODEL = 1024
ATT_HEADS = 8
ATT_HEAD_DIM = 64
ATT_WIDTH = ATT_HEADS * ATT_HEAD_DIM
DILATIONS = (1, 4, 16)
ATT_SPAN = 128
ROPE_THETA = 10000.0
GLA_HEADS = 4
GLA_KEY_DIM = 64
GLA_VAL_DIM = 128
GLA_KEY_WIDTH = GLA_HEADS * GLA_KEY_DIM
GLA_VAL_WIDTH = GLA_HEADS * GLA_VAL_DIM
GLA_GATE_RANK = 16
GLA_GATE_NORMALIZER = 16.0
GLA_CHUNK = 64
N_EXPERTS = 32
TOP_K = 4
D_FF = D_MODEL
SWIGLU_LIMIT = 7.0
SWIGLU_ALPHA = 1.702
NORM_EPS = 1e-6
N_MOD = 6

LANES = 128
NEG_BIG = -1e30

TM_PROJ = 512
TQ_ATT = 128
TG_GLA = 512
TM_OUT = 512
TM_MOE = 256
TM_FIN = 256
TN_MOD = 512
VMEM_LIMIT = 48 * 1024 * 1024

_C_QA, _C_KA, _C_VA = 0, 512, 1024
_C_QG, _C_KG, _C_VG, _C_RG, _C_LR = 1536, 1792, 2048, 2560, 3072
D_IN_PACKED = 3200


def _rms(x, g):
    return x * lax.rsqrt(jnp.mean(x * x, axis=-1, keepdims=True) + NORM_EPS) * g


def _mod_kernel(ct_ref, w_ref, b_ref, o_ref):
    ct = ct_ref[...]
    s = ct * (1.0 / (1.0 + jnp.exp(-ct)))
    w = w_ref[...]
    rows = [jnp.sum(s[:, b:b + 1] * w, axis=0, keepdims=True) for b in range(ct.shape[1])]
    o_ref[...] = jnp.concatenate(rows, axis=0) + b_ref[...]


def _mod(c, w_mod, b_mod):
    B = c.shape[0]
    n = w_mod.shape[1]
    return pl.pallas_call(
        _mod_kernel,
        name="mod",
        grid=(n // TN_MOD,),
        in_specs=[pl.BlockSpec((D_MODEL, B), lambda j: (0, 0)),
                  pl.BlockSpec((D_MODEL, TN_MOD), lambda j: (0, j)),
                  pl.BlockSpec((1, TN_MOD), lambda j: (0, j))],
        out_specs=pl.BlockSpec((B, TN_MOD), lambda j: (0, j)),
        out_shape=jax.ShapeDtypeStruct((B, n), jnp.float32),
        compiler_params=pltpu.CompilerParams(dimension_semantics=("arbitrary",),
                                             vmem_limit_bytes=VMEM_LIMIT),
    )(c.T, w_mod, b_mod.reshape(1, n))


def _in_proj_kernel(x_ref, mod_ref, g_ref, w_ref, cos_ref, sin_ref, wlr_ref, bg_ref,
                    qa_ref, ka_ref, va_ref, qg_ref, kg_ref, vg_ref, sr_ref, la_ref):
    x = x_ref[0]
    shift = mod_ref[0, 0:1, :]
    scale = mod_ref[0, 1:2, :]
    h = (_rms(x, g_ref[...]) * (1.0 + scale) + shift).astype(jnp.bfloat16)

    def proj(c0, width):
        return jnp.dot(h, w_ref[:, c0:c0 + width], preferred_element_type=jnp.float32)

    cos = cos_ref[...]
    sin = sin_ref[...]

    def rope_store(c0, out_ref, mult):
        p = proj(c0, ATT_WIDTH)
        for s in range(ATT_WIDTH // LANES):
            t = p[:, s * LANES:(s + 1) * LANES]
            r = t * cos + pltpu.roll(t, LANES // 2, axis=1) * sin
            if mult != 1.0:
                r = r * mult
            out_ref[0, :, s * LANES:(s + 1) * LANES] = r.astype(out_ref.dtype)

    rope_store(_C_QA, qa_ref, ATT_HEAD_DIM ** -0.5)
    rope_store(_C_KA, ka_ref, 1.0)
    va_ref[0] = proj(_C_VA, ATT_WIDTH).astype(va_ref.dtype)
    qg_ref[0] = proj(_C_QG, GLA_KEY_WIDTH).astype(qg_ref.dtype)
    kg_ref[0] = proj(_C_KG, GLA_KEY_WIDTH).astype(kg_ref.dtype)
    vg_ref[0] = proj(_C_VG, GLA_VAL_WIDTH).astype(vg_ref.dtype)
    r = proj(_C_RG, GLA_VAL_WIDTH)
    sr_ref[0] = (r * (1.0 / (1.0 + jnp.exp(-r)))).astype(sr_ref.dtype)
    lr = proj(_C_LR, LANES).astype(jnp.bfloat16)
    z = jnp.dot(lr, wlr_ref[...], preferred_element_type=jnp.float32) + bg_ref[...]
    log_sig = jnp.minimum(z, 0.0) - jnp.log(1.0 + jnp.exp(-jnp.abs(z)))
    la_ref[0] = log_sig * (1.0 / GLA_GATE_NORMALIZER)


def _in_proj(x, mod3, g_pre, w_packed, cos_t, sin_t, wlr, bg):
    B, S, D = x.shape
    nb = S // TM_PROJ
    tok = lambda w: pl.BlockSpec((1, TM_PROJ, w), lambda b, i: (b, i, 0))
    const = lambda shape: pl.BlockSpec(shape, lambda b, i: tuple(0 for _ in shape))
    bf = jnp.bfloat16
    out_shapes = [jax.ShapeDtypeStruct((B, S, w), dt) for w, dt in
                  [(ATT_WIDTH, bf), (ATT_WIDTH, bf), (ATT_WIDTH, bf),
                   (GLA_KEY_WIDTH, bf), (GLA_KEY_WIDTH, bf),
                   (GLA_VAL_WIDTH, bf), (GLA_VAL_WIDTH, bf), (GLA_KEY_WIDTH, jnp.float32)]]
    return pl.pallas_call(
        _in_proj_kernel,
        name="in_proj",
        grid=(B, nb),
        in_specs=[tok(D),
                  pl.BlockSpec((1, N_MOD, D), lambda b, i: (b, 0, 0)),
                  const((1, D)),
                  const((D, D_IN_PACKED)),
                  pl.BlockSpec((TM_PROJ, LANES), lambda b, i: (i, 0)),
                  pl.BlockSpec((TM_PROJ, LANES), lambda b, i: (i, 0)),
                  const((LANES, GLA_KEY_WIDTH)),
                  const((1, GLA_KEY_WIDTH))],
        out_specs=[tok(s.shape[-1]) for s in out_shapes],
        out_shape=out_shapes,
        compiler_params=pltpu.CompilerParams(dimension_semantics=("arbitrary", "arbitrary"),
                                             vmem_limit_bytes=VMEM_LIMIT),
    )(x, mod3, g_pre, w_packed, cos_t, sin_t, wlr, bg)


def _attn_kernel(q_ref, kp_ref, kc_ref, vp_ref, vc_ref, o_ref, st_ref):
    i = pl.program_id(2)
    tq = q_ref.shape[1]
    row = lax.broadcasted_iota(jnp.int32, (tq, 2 * tq), 0)
    col = lax.broadcasted_iota(jnp.int32, (tq, 2 * tq), 1)
    valid = (col >= row) & (col <= row + ATT_SPAN) & ((col >= tq) | (i > 0))
    bias = jnp.where(valid, 0.0, NEG_BIG)
    lane = lax.broadcasted_iota(jnp.int32, (tq, LANES), 1)
    q_head = (lane >> 5) & 1
    stats = jnp.zeros((tq, LANES), jnp.float32)
    for hp in range(ATT_HEADS // 2):
        sl = slice(hp * LANES, (hp + 1) * LANES)
        q2 = q_ref[0, :, sl]
        kcat = jnp.concatenate([kp_ref[0, :, sl], kc_ref[0, :, sl]], axis=0)
        vcat = jnp.concatenate([vp_ref[0, :, sl], vc_ref[0, :, sl]], axis=0)
        outs = []
        for hh in range(2):
            qm = jnp.where(q_head == hh, q2, jnp.zeros_like(q2))
            s = lax.dot_general(qm, kcat, (((1,), (1,)), ((), ())),
                                preferred_element_type=jnp.float32) + bias
            m = jnp.max(s, axis=1, keepdims=True)
            p = jnp.exp(s - m)
            l = jnp.sum(p, axis=1, keepdims=True)
            o = jnp.dot(p.astype(vcat.dtype), vcat, preferred_element_type=jnp.float32)
            outs.append(o / l)
            stats = jnp.where(lane == 2 * hp + hh, m + jnp.log(l), stats)
        o_ref[0, :, sl] = jnp.where(lane < ATT_HEAD_DIM, outs[0], outs[1]).astype(o_ref.dtype)
    st_ref[0] = stats


def _attn_pattern(q, k, v, dil):
    B, S, W = q.shape
    L = S // dil
    nq = L // TQ_ATT
    qv, kv, vv = (t.reshape(B, L, dil * W) for t in (q, k, v))
    cur = pl.BlockSpec((1, TQ_ATT, W), lambda b, r, i: (b, i, r))
    prev = pl.BlockSpec((1, TQ_ATT, W), lambda b, r, i: (b, jnp.maximum(i - 1, 0), r))
    o, st = pl.pallas_call(
        _attn_kernel,
        name=f"attn_d{dil}",
        grid=(B, dil, nq),
        in_specs=[cur, prev, cur, prev, cur],
        out_specs=[cur, pl.BlockSpec((1, TQ_ATT, LANES), lambda b, r, i: (b, i, r))],
        out_shape=[jax.ShapeDtypeStruct((B, L, dil * W), jnp.bfloat16),
                   jax.ShapeDtypeStruct((B, L, dil * LANES), jnp.float32)],
        compiler_params=pltpu.CompilerParams(
            dimension_semantics=("arbitrary", "arbitrary", "arbitrary"),
            vmem_limit_bytes=VMEM_LIMIT),
    )(qv, kv, kv, vv, vv)
    return o.reshape(B, S, W), st.reshape(B, S, LANES)


def _cumsum_rows(x):
    n = x.shape[0]
    row = lax.broadcasted_iota(jnp.int32, x.shape, 0)
    s = 1
    while s < n:
        x = x + jnp.where(row >= s, pltpu.roll(x, s, axis=0), 0.0)
        s *= 2
    return x


def _gla_kernel(q_ref, k_ref, v_ref, la_ref, sr_ref, g_ref, o_ref, st_ref):
    @pl.when(pl.program_id(1) == 0)
    def _():
        st_ref[...] = jnp.zeros_like(st_ref)

    C = GLA_CHUNK
    n_chunks = q_ref.shape[1] // C
    lane = lax.broadcasted_iota(jnp.int32, (C, LANES), 1)
    lane_sq = lax.broadcasted_iota(jnp.int32, (LANES, LANES), 1)
    tril = (lax.broadcasted_iota(jnp.int32, (C, C), 0)
            >= lax.broadcasted_iota(jnp.int32, (C, C), 1))
    g = g_ref[...]
    bf = jnp.bfloat16

    def chunk(c, carry):
        r0 = pl.multiple_of(c * C, C)
        rows = pl.ds(r0, C)
        for p in range(GLA_HEADS // 2):
            ksl = slice(p * LANES, (p + 1) * LANES)
            b = _cumsum_rows(la_ref[0, rows, ksl])
            b_last = b[C - 1:C, :]
            q = q_ref[0, rows, ksl].astype(jnp.float32) * (GLA_KEY_DIM ** -0.5)
            k = k_ref[0, rows, ksl].astype(jnp.float32)
            q_dec = q * jnp.exp(b)
            k_inv = (k * jnp.exp(-b)).astype(bf)
            k_dec = (k * jnp.exp(b_last - b)).astype(bf)
            st = st_ref[p]
            st_b = st.astype(bf)
            ut = []
            for hh in range(2):
                h = 2 * p + hh
                vsl = slice(h * GLA_VAL_DIM, (h + 1) * GLA_VAL_DIM)
                v = v_ref[0, rows, vsl]
                own = (lane >= GLA_KEY_DIM) if hh else (lane < GLA_KEY_DIM)
                qm = jnp.where(own, q_dec, 0.0).astype(bf)
                att = lax.dot_general(qm, k_inv, (((1,), (1,)), ((), ())),
                                      preferred_element_type=jnp.float32)
                att = jnp.where(tril, att, 0.0).astype(bf)
                o = jnp.dot(att, v, preferred_element_type=jnp.float32)
                o = o + lax.dot_general(qm, st_b, (((1,), (1,)), ((), ())),
                                        preferred_element_type=jnp.float32)
                o = _rms(o, g) * sr_ref[0, rows, vsl].astype(jnp.float32)
                o_ref[0, rows, vsl] = o.astype(o_ref.dtype)
                ut.append(lax.dot_general(v, k_dec, (((0,), (0,)), ((), ())),
                                          preferred_element_type=jnp.float32))
            st_ref[p] = st * jnp.exp(b_last) + jnp.where(lane_sq < GLA_KEY_DIM, ut[0], ut[1])
        return carry

    lax.fori_loop(0, n_chunks, chunk, 0)


def _gla(qg, kg, vg, la, sr, g_gla):
    B, S, _ = qg.shape
    tok = lambda w: pl.BlockSpec((1, TG_GLA, w), lambda b, i: (b, i, 0))
    return pl.pallas_call(
        _gla_kernel,
        name="gla",
        grid=(B, S // TG_GLA),
        in_specs=[tok(GLA_KEY_WIDTH), tok(GLA_KEY_WIDTH), tok(GLA_VAL_WIDTH),
                  tok(GLA_KEY_WIDTH), tok(GLA_VAL_WIDTH),
                  pl.BlockSpec((1, GLA_VAL_DIM), lambda b, i: (0, 0))],
        out_specs=tok(GLA_VAL_WIDTH),
        out_shape=jax.ShapeDtypeStruct((B, S, GLA_VAL_WIDTH), jnp.bfloat16),
        scratch_shapes=[pltpu.VMEM((GLA_HEADS // 2, GLA_VAL_DIM, LANES), jnp.float32)],
        compiler_params=pltpu.CompilerParams(dimension_semantics=("arbitrary", "arbitrary"),
                                             vmem_limit_bytes=VMEM_LIMIT),
    )(qg, kg, vg, la, sr, g_gla)


def _out_proj_kernel(o1_ref, o2_ref, o3_ref, s1_ref, s2_ref, s3_ref, og_ref, x_ref, mod_ref,
                     wout_ref, gpost_ref, gpre_ref, rw_ref, rb_ref,
                     x1_ref, h2_ref, ti_ref, tg_ref):
    tm = x_ref.shape[1]
    lane = lax.broadcasted_iota(jnp.int32, (tm, LANES), 1)
    lses = [s1_ref[0], s2_ref[0], s3_ref[0]]
    m = jnp.maximum(jnp.maximum(lses[0], lses[1]), lses[2])
    es = [jnp.exp(t - m) for t in lses]
    inv = 1.0 / (es[0] + es[1] + es[2])
    ws = [e * inv for e in es]
    o_refs = [o1_ref, o2_ref, o3_ref]
    pairs = []
    for hp in range(ATT_HEADS // 2):
        sl = slice(hp * LANES, (hp + 1) * LANES)
        acc = jnp.zeros((tm, LANES), jnp.float32)
        for w, o_ref in zip(ws, o_refs):
            wp = jnp.where(lane < ATT_HEAD_DIM, w[:, 2 * hp:2 * hp + 1], w[:, 2 * hp + 1:2 * hp + 2])
            acc = acc + wp * o_ref[0, :, sl].astype(jnp.float32)
        pairs.append(acc.astype(jnp.bfloat16))
    mixed = jnp.concatenate(pairs + [og_ref[0]], axis=1)
    y = jnp.dot(mixed, wout_ref[...], preferred_element_type=jnp.float32)
    gate1 = mod_ref[0, 2:3, :]
    shift2 = mod_ref[0, 3:4, :]
    scale2 = mod_ref[0, 4:5, :]
    x1 = x_ref[0] + gate1 * _rms(y, gpost_ref[...])
    x1_ref[0] = x1
    h2 = _rms(x1, gpre_ref[...]) * (1.0 + scale2) + shift2
    h2_ref[0] = h2.astype(h2_ref.dtype)
    logits = jnp.dot(h2, rw_ref[...], preferred_element_type=jnp.float32,
                     precision=lax.Precision.HIGHEST) + rb_ref[...]
    lane_f = lane.astype(jnp.float32)
    vals, idxs = [], []
    for _ in range(TOP_K):
        mk = jnp.max(logits, axis=1, keepdims=True)
        ik = jnp.min(jnp.where(logits == mk, lane_f, float(LANES)), axis=1, keepdims=True)
        logits = jnp.where(lane_f == ik, -jnp.inf, logits)
        vals.append(mk)
        idxs.append(ik)
    ex = [jnp.exp(v - vals[0]) for v in vals]
    den = ex[0] + ex[1] + ex[2] + ex[3]
    ti = jnp.zeros((tm, LANES), jnp.float32)
    tg = jnp.zeros((tm, LANES), jnp.float32)
    for kk in range(TOP_K):
        ti = jnp.where(lane == kk, idxs[kk], ti)
        tg = jnp.where(lane == kk, ex[kk] / den, tg)
    ti_ref[0] = ti.astype(jnp.int32)
    tg_ref[0] = tg


def _out_proj(o_pats, st_pats, og, x, mod3, w_out, g_post, g_pre, rw, rb):
    B, S, D = x.shape
    tok = lambda w: pl.BlockSpec((1, TM_OUT, w), lambda b, i: (b, i, 0))
    const = lambda shape: pl.BlockSpec(shape, lambda b, i: tuple(0 for _ in shape))
    out_shapes = [jax.ShapeDtypeStruct((B, S, D), jnp.float32),
                  jax.ShapeDtypeStruct((B, S, D), jnp.bfloat16),
                  jax.ShapeDtypeStruct((B, S, LANES), jnp.int32),
                  jax.ShapeDtypeStruct((B, S, LANES), jnp.float32)]
    return pl.pallas_call(
        _out_proj_kernel,
        name="out_proj",
        grid=(B, S // TM_OUT),
        in_specs=[tok(ATT_WIDTH)] * 3 + [tok(LANES)] * 3 + [tok(GLA_VAL_WIDTH), tok(D),
                  pl.BlockSpec((1, N_MOD, D), lambda b, i: (b, 0, 0)),
                  const((D, D)), const((1, D)), const((1, D)),
                  const((D, LANES)), const((1, LANES))],
        out_specs=[tok(D), tok(D), tok(LANES), tok(LANES)],
        out_shape=out_shapes,
        compiler_params=pltpu.CompilerParams(dimension_semantics=("arbitrary", "arbitrary"),
                                             vmem_limit_bytes=VMEM_LIMIT),
    )(*o_pats, *st_pats, og, x, mod3, w_out, g_post, g_pre, rw, rb)


def _moe_kernel(be_ref, nu_ref, x_ref, wg_ref, wl_ref, bg_ref, bl_ref, wd_ref, bd_ref, o_ref):
    i = pl.program_id(0)

    @pl.when(i < nu_ref[0])
    def _():
        x = x_ref[...]
        xg = jnp.dot(x, wg_ref[...], preferred_element_type=jnp.float32) + bg_ref[...]
        xl = jnp.dot(x, wl_ref[...], preferred_element_type=jnp.float32) + bl_ref[...]
        xg = jnp.minimum(xg, SWIGLU_LIMIT)
        xl = jnp.clip(xl, -SWIGLU_LIMIT, SWIGLU_LIMIT)
        act = xg * (1.0 / (1.0 + jnp.exp(-SWIGLU_ALPHA * xg))) * (xl + 1.0)
        out = jnp.dot(act.astype(jnp.bfloat16), wd_ref[...],
                      preferred_element_type=jnp.float32) + bd_ref[...]
        o_ref[...] = out.astype(o_ref.dtype)

    @pl.when(i >= nu_ref[0])
    def _():
        o_ref[...] = jnp.zeros_like(o_ref)


def _moe(xs, blk_e, n_used, wg, wl, bg, bl, wd, bd):
    P, D = xs.shape
    n_blocks = P // TM_MOE
    wspec = lambda k, n: pl.BlockSpec((None, k, n), lambda i, be, nu: (be[i], 0, 0))
    grid_spec = pltpu.PrefetchScalarGridSpec(
        num_scalar_prefetch=2,
        grid=(n_blocks,),
        in_specs=[pl.BlockSpec((TM_MOE, D), lambda i, be, nu: (i, 0)),
                  wspec(D, D_FF), wspec(D, D_FF), wspec(1, D_FF), wspec(1, D_FF),
                  wspec(D_FF, D), wspec(1, D)],
        out_specs=pl.BlockSpec((TM_MOE, D), lambda i, be, nu: (i, 0)),
    )
    return pl.pallas_call(
        _moe_kernel,
        name="moe",
        grid_spec=grid_spec,
        out_shape=jax.ShapeDtypeStruct((P, D), jnp.bfloat16),
        compiler_params=pltpu.CompilerParams(dimension_semantics=("arbitrary",),
                                             vmem_limit_bytes=VMEM_LIMIT),
    )(blk_e, n_used, xs, wg, wl, bg, bl, wd, bd)


def _final_kernel(x1_ref, yk_ref, tg_ref, mod_ref, g_ref, o_ref):
    gates = tg_ref[...]
    y = jnp.zeros(x1_ref.shape, jnp.float32)
    for kk in range(TOP_K):
        y = y + gates[:, kk:kk + 1] * yk_ref[:, kk * D_MODEL:(kk + 1) * D_MODEL].astype(jnp.float32)
    gate2 = mod_ref[0, 5:6, :]
    o_ref[...] = x1_ref[...] + gate2 * _rms(y, g_ref[...])


def _final(x1, yk, tg, mod3, g_post, seq_len):
    T, D = x1.shape
    per_batch = seq_len // TM_FIN
    return pl.pallas_call(
        _final_kernel,
        name="final",
        grid=(T // TM_FIN,),
        in_specs=[pl.BlockSpec((TM_FIN, D), lambda i: (i, 0)),
                  pl.BlockSpec((TM_FIN, TOP_K * D), lambda i: (i, 0)),
                  pl.BlockSpec((TM_FIN, LANES), lambda i: (i, 0)),
                  pl.BlockSpec((1, N_MOD, D), lambda i: (i // per_batch, 0, 0)),
                  pl.BlockSpec((1, D), lambda i: (0, 0))],
        out_specs=pl.BlockSpec((TM_FIN, D), lambda i: (i, 0)),
        out_shape=jax.ShapeDtypeStruct((T, D), jnp.float32),
        compiler_params=pltpu.CompilerParams(dimension_semantics=("arbitrary",),
                                             vmem_limit_bytes=VMEM_LIMIT),
    )(x1, yk, tg, mod3, g_post)


def _pack_w_in(w_in):
    n = np.arange(LANES)
    src = np.where(n < 32, n, np.where(n < 64, n + 32, np.where(n < 96, n - 32, n)))
    perm = (np.arange(ATT_WIDTH // LANES)[:, None] * LANES + src[None, :]).reshape(-1)
    lr = jnp.pad(w_in[:, 3072:3072 + GLA_GATE_RANK], ((0, 0), (0, LANES - GLA_GATE_RANK)))
    packed = jnp.concatenate([w_in[:, perm], w_in[:, ATT_WIDTH + perm], w_in[:, 1024:3072], lr],
                             axis=1)
    return packed.astype(jnp.bfloat16)


def _rope_tables(seq_len):
    half = ATT_HEAD_DIM // 2
    inv_freq = ROPE_THETA ** (-jnp.arange(half, dtype=jnp.float32) / half)
    ang = jnp.arange(seq_len, dtype=jnp.float32)[:, None] * inv_freq[None, :]
    cos = jnp.tile(jnp.cos(ang), (1, LANES // half))
    sin = jnp.tile(jnp.sin(ang), (1, LANES // half))
    sign = jnp.where(jnp.arange(LANES) < LANES // 2, -1.0, 1.0)
    return cos, sin * sign


def _route(top_i, n_tokens):
    A = n_tokens * TOP_K
    e_flat = top_i.reshape(A)
    onehot = (e_flat[:, None] == jnp.arange(N_EXPERTS, dtype=jnp.int32)[None, :]).astype(jnp.int32)
    csum = jnp.cumsum(onehot, axis=0)
    rank = jnp.sum(onehot * csum, axis=1) - 1
    counts = csum[-1]
    padded = (counts + TM_MOE - 1) // TM_MOE * TM_MOE
    pend = jnp.cumsum(padded)
    pstart = pend - padded
    pos = pstart[e_flat] + rank
    n_blocks = -(-(A + N_EXPERTS * (TM_MOE - 1)) // TM_MOE)
    tok_buf = jnp.zeros((n_blocks * TM_MOE,), jnp.int32).at[pos].set(
        jnp.arange(A, dtype=jnp.int32) // TOP_K)
    blk_start = jnp.arange(n_blocks, dtype=jnp.int32) * TM_MOE
    blk_e = jnp.minimum(jnp.sum((pend[None, :] <= blk_start[:, None]).astype(jnp.int32), axis=1),
                        N_EXPERTS - 1)
    n_used = (pend[-1:] // TM_MOE).astype(jnp.int32)
    return pos, tok_buf, blk_e, n_used


def kernel(x, c, w_mod, b_mod, g_pre_mix, w_in, w_gate_lr, b_gate, g_gla, w_out, g_post_mix,
           g_pre_ffn, router_w, router_b, w_gate_up, b_gate_up, w_down, b_down, g_post_ffn):
    B, S, D = x.shape
    T = B * S
    bf = jnp.bfloat16
    cos_t, sin_t = _rope_tables(S)
    for l in range(w_mod.shape[0]):
        mod3 = _mod(c, w_mod[l], b_mod[l]).reshape(B, N_MOD, D)
        wlr = jnp.pad(w_gate_lr[l], ((0, LANES - GLA_GATE_RANK), (0, 0))).astype(bf)
        qa, ka, va, qg, kg, vg, sr, la = _in_proj(
            x, mod3, g_pre_mix[l][None], _pack_w_in(w_in[l]), cos_t, sin_t, wlr, b_gate[l][None])
        pats = [_attn_pattern(qa, ka, va, d) for d in DILATIONS]
        og = _gla(qg, kg, vg, la, sr, g_gla[l][None])
        rw = jnp.pad(router_w[l], ((0, 0), (0, LANES - N_EXPERTS)))
        rb = jnp.pad(router_b[l], (0, LANES - N_EXPERTS), constant_values=NEG_BIG)[None]
        x1, h2, ti, tg = _out_proj([p[0] for p in pats], [p[1] for p in pats], og, x, mod3,
                                   w_out[l].astype(bf), g_post_mix[l][None], g_pre_ffn[l][None],
                                   rw, rb)
        top_i = ti.reshape(T, LANES)[:, :TOP_K]
        pos, tok_buf, blk_e, n_used = _route(top_i, T)
        xs = jnp.take(h2.reshape(T, D), tok_buf, axis=0)
        wgu = w_gate_up[l]
        out_buf = _moe(xs, blk_e, n_used,
                       wgu[:, :, 0::2].astype(bf), wgu[:, :, 1::2].astype(bf),
                       b_gate_up[l][:, None, 0::2], b_gate_up[l][:, None, 1::2],
                       w_down[l].astype(bf), b_down[l][:, None, :])
        yk = jnp.take(out_buf, pos, axis=0).reshape(T, TOP_K * D)
        x = _final(x1.reshape(T, D), yk, tg.reshape(T, LANES), mod3, g_post_ffn[l][None],
                   S).reshape(B, S, D)
    return x
```

```python
import functools

import numpy as np
import jax
import jax.numpy as jnp
from jax import lax
from jax.experimental import pallas as pl
from jax.experimental.pallas import tpu as pltpu

D_MODEL = 1024
ATT_HEADS = 8
ATT_HEAD_DIM = 64
ATT_WIDTH = ATT_HEADS * ATT_HEAD_DIM
DILATIONS = (1, 4, 16)
ATT_SPAN = 128
ROPE_THETA = 10000.0
GLA_HEADS = 4
GLA_KEY_DIM = 64
GLA_VAL_DIM = 128
GLA_KEY_WIDTH = GLA_HEADS * GLA_KEY_DIM
GLA_VAL_WIDTH = GLA_HEADS * GLA_VAL_DIM
GLA_GATE_RANK = 16
GLA_GATE_NORMALIZER = 16.0
GLA_CHUNK = 64
N_EXPERTS = 32
TOP_K = 4
D_FF = D_MODEL
SWIGLU_LIMIT = 7.0
SWIGLU_ALPHA = 1.702
NORM_EPS = 1e-6
N_MOD = 6

LANES = 128
NEG_BIG = -1e30

TM_PROJ = 512
TQ_ATT = 128
TG_GLA = 512
TM_OUT = 512
TM_MOE = 256
TM_FIN = 256
TN_MOD = 512
TK_SPLIT = 512
VMEM_LIMIT = 48 * 1024 * 1024

_C_QA, _C_KA, _C_VA = 0, 512, 1024
_C_QG, _C_KG, _C_VG, _C_RG, _C_LR = 1536, 1792, 2048, 2560, 3072
D_IN_PACKED = 3200


def _rms(x, g):
    return x * lax.rsqrt(jnp.mean(x * x, axis=-1, keepdims=True) + NORM_EPS) * g


def _mod_kernel(ct_ref, w_ref, b_ref, o_ref):
    ct = ct_ref[...]
    s = ct * (1.0 / (1.0 + jnp.exp(-ct)))
    w = w_ref[...]
    rows = [jnp.sum(s[:, b:b + 1] * w, axis=0, keepdims=True) for b in range(ct.shape[1])]
    o_ref[...] = jnp.concatenate(rows, axis=0) + b_ref[...]


def _mod(c, w_mod, b_mod):
    B = c.shape[0]
    n = w_mod.shape[1]
    return pl.pallas_call(
        _mod_kernel,
        name="mod",
        grid=(n // TN_MOD,),
        in_specs=[pl.BlockSpec((D_MODEL, B), lambda j: (0, 0)),
                  pl.BlockSpec((D_MODEL, TN_MOD), lambda j: (0, j)),
                  pl.BlockSpec((1, TN_MOD), lambda j: (0, j))],
        out_specs=pl.BlockSpec((B, TN_MOD), lambda j: (0, j)),
        out_shape=jax.ShapeDtypeStruct((B, n), jnp.float32),
        compiler_params=pltpu.CompilerParams(dimension_semantics=("arbitrary",),
                                             vmem_limit_bytes=VMEM_LIMIT),
    )(c.T, w_mod, b_mod.reshape(1, n))


def _store_residue_views(scr_ref, out_refs):
    ns, tm, _ = scr_ref.shape
    w = ns * LANES
    for d, ref in zip(DILATIONS, out_refs):
        for r in range(d):
            for s in range(ns):
                c0 = r * w + s * LANES
                ref[0, :, c0:c0 + LANES] = scr_ref[s, pl.ds(r, tm // d, stride=d), :].astype(ref.dtype)


def _in_proj_kernel(x_ref, mod_ref, g_ref, w_ref, cos_ref, sin_ref, wlr_ref, bg_ref,
                    qa1_ref, qa4_ref, qa16_ref, ka1_ref, ka4_ref, ka16_ref,
                    va1_ref, va4_ref, va16_ref, qg_ref, kg_ref, vg_ref, sr_ref, la_ref,
                    qs_ref, ks_ref, vs_ref):
    x = x_ref[0]
    shift = mod_ref[0, 0:1, :]
    scale = mod_ref[0, 1:2, :]
    h = (_rms(x, g_ref[...]) * (1.0 + scale) + shift).astype(jnp.bfloat16)

    def proj(c0, width):
        return jnp.dot(h, w_ref[:, c0:c0 + width], preferred_element_type=jnp.float32)

    cos = cos_ref[...]
    sin = sin_ref[...]

    def rope(c0, scr_ref, mult):
        p = proj(c0, ATT_WIDTH)
        for s in range(ATT_WIDTH // LANES):
            t = p[:, s * LANES:(s + 1) * LANES]
            r = t * cos + pltpu.roll(t, LANES // 2, axis=1) * sin
            if mult != 1.0:
                r = r * mult
            scr_ref[s] = r

    rope(_C_QA, qs_ref, ATT_HEAD_DIM ** -0.5)
    _store_residue_views(qs_ref, (qa1_ref, qa4_ref, qa16_ref))
    rope(_C_KA, ks_ref, 1.0)
    _store_residue_views(ks_ref, (ka1_ref, ka4_ref, ka16_ref))
    pv = proj(_C_VA, ATT_WIDTH)
    for s in range(ATT_WIDTH // LANES):
        vs_ref[s] = pv[:, s * LANES:(s + 1) * LANES]
    _store_residue_views(vs_ref, (va1_ref, va4_ref, va16_ref))
    qg_ref[0] = proj(_C_QG, GLA_KEY_WIDTH).astype(qg_ref.dtype)
    kg_ref[0] = proj(_C_KG, GLA_KEY_WIDTH).astype(kg_ref.dtype)
    vg_ref[0] = proj(_C_VG, GLA_VAL_WIDTH).astype(vg_ref.dtype)
    r = proj(_C_RG, GLA_VAL_WIDTH)
    sr_ref[0] = (r * (1.0 / (1.0 + jnp.exp(-r)))).astype(sr_ref.dtype)
    lr = proj(_C_LR, LANES).astype(jnp.bfloat16)
    z = jnp.dot(lr, wlr_ref[...], preferred_element_type=jnp.float32) + bg_ref[...]
    log_sig = jnp.minimum(z, 0.0) - jnp.log(1.0 + jnp.exp(-jnp.abs(z)))
    la_ref[0] = log_sig * (1.0 / GLA_GATE_NORMALIZER)


def _in_proj(x, mod3, g_pre, w_packed, cos_t, sin_t, wlr, bg):
    B, S, D = x.shape
    nb = S // TM_PROJ
    tok = lambda w: pl.BlockSpec((1, TM_PROJ, w), lambda b, i: (b, i, 0))
    const = lambda shape: pl.BlockSpec(shape, lambda b, i: tuple(0 for _ in shape))
    bf = jnp.bfloat16
    att_shapes = [jax.ShapeDtypeStruct((B, S // d, d * ATT_WIDTH), bf) for d in DILATIONS] * 3
    att_specs = [pl.BlockSpec((1, TM_PROJ // d, d * ATT_WIDTH), lambda b, i: (b, i, 0))
                 for d in DILATIONS] * 3
    gla_shapes = [jax.ShapeDtypeStruct((B, S, w), dt) for w, dt in
                  [(GLA_KEY_WIDTH, bf), (GLA_KEY_WIDTH, bf),
                   (GLA_VAL_WIDTH, bf), (GLA_VAL_WIDTH, bf), (GLA_KEY_WIDTH, jnp.float32)]]
    return pl.pallas_call(
        _in_proj_kernel,
        name="in_proj",
        grid=(B, nb),
        in_specs=[tok(D),
                  pl.BlockSpec((1, N_MOD, D), lambda b, i: (b, 0, 0)),
                  const((1, D)),
                  const((D, D_IN_PACKED)),
                  pl.BlockSpec((TM_PROJ, LANES), lambda b, i: (i, 0)),
                  pl.BlockSpec((TM_PROJ, LANES), lambda b, i: (i, 0)),
                  const((LANES, GLA_KEY_WIDTH)),
                  const((1, GLA_KEY_WIDTH))],
        out_specs=att_specs + [tok(s.shape[-1]) for s in gla_shapes],
        out_shape=att_shapes + gla_shapes,
        scratch_shapes=[pltpu.VMEM((ATT_WIDTH // LANES, TM_PROJ, LANES), jnp.float32)] * 3,
        compiler_params=pltpu.CompilerParams(dimension_semantics=("arbitrary", "arbitrary"),
                                             vmem_limit_bytes=VMEM_LIMIT),
    )(x, mod3, g_pre, w_packed, cos_t, sin_t, wlr, bg)


def _attn_kernel(q_ref, kp_ref, kc_ref, vp_ref, vc_ref, o_ref, st_ref):
    i = pl.program_id(2)
    tq = q_ref.shape[1]
    row = lax.broadcasted_iota(jnp.int32, (tq, 2 * tq), 0)
    col = lax.broadcasted_iota(jnp.int32, (tq, 2 * tq), 1)
    valid = (col >= row) & (col <= row + ATT_SPAN) & ((col >= tq) | (i > 0))
    bias = jnp.where(valid, 0.0, NEG_BIG)
    lane = lax.broadcasted_iota(jnp.int32, (tq, LANES), 1)
    q_head = (lane >> 5) & 1
    stats = jnp.zeros((tq, LANES), jnp.float32)
    for hp in range(ATT_HEADS // 2):
        sl = slice(hp * LANES, (hp + 1) * LANES)
        q2 = q_ref[0, :, sl]
        kcat = jnp.concatenate([kp_ref[0, :, sl], kc_ref[0, :, sl]], axis=0)
        vcat = jnp.concatenate([vp_ref[0, :, sl], vc_ref[0, :, sl]], axis=0)
        outs = []
        for hh in range(2):
            qm = jnp.where(q_head == hh, q2, jnp.zeros_like(q2))
            s = lax.dot_general(qm, kcat, (((1,), (1,)), ((), ())),
                                preferred_element_type=jnp.float32) + bias
            m = jnp.max(s, axis=1, keepdims=True)
            p = jnp.exp(s - m)
            l = jnp.sum(p, axis=1, keepdims=True)
            o = jnp.dot(p.astype(vcat.dtype), vcat, preferred_element_type=jnp.float32)
            outs.append(o / l)
            stats = jnp.where(lane == 2 * hp + hh, m + jnp.log(l), stats)
        o_ref[0, :, sl] = jnp.where(lane < ATT_HEAD_DIM, outs[0], outs[1]).astype(o_ref.dtype)
    st_ref[0] = stats


def _attn_pattern(qv, kv, vv, dil):
    B, L, _ = qv.shape
    W = ATT_WIDTH
    nq = L // TQ_ATT
    cur = pl.BlockSpec((1, TQ_ATT, W), lambda b, r, i: (b, i, r))
    prev = pl.BlockSpec((1, TQ_ATT, W), lambda b, r, i: (b, jnp.maximum(i - 1, 0), r))
    o, st = pl.pallas_call(
        _attn_kernel,
        name=f"attn_d{dil}",
        grid=(B, dil, nq),
        in_specs=[cur, prev, cur, prev, cur],
        out_specs=[cur, pl.BlockSpec((1, TQ_ATT, LANES), lambda b, r, i: (b, i, r))],
        out_shape=[jax.ShapeDtypeStruct((B, L, dil * W), jnp.bfloat16),
                   jax.ShapeDtypeStruct((B, L, dil * LANES), jnp.float32)],
        compiler_params=pltpu.CompilerParams(
            dimension_semantics=("arbitrary", "arbitrary", "arbitrary"),
            vmem_limit_bytes=VMEM_LIMIT),
    )(qv, kv, kv, vv, vv)
    return o, st


def _cumsum_rows(x):
    n = x.shape[0]
    row = lax.broadcasted_iota(jnp.int32, x.shape, 0)
    s = 1
    while s < n:
        x = x + jnp.where(row >= s, pltpu.roll(x, s, axis=0), 0.0)
        s *= 2
    return x


def _gla_kernel(q_ref, k_ref, v_ref, la_ref, sr_ref, g_ref, o_ref, st_ref):
    @pl.when(pl.program_id(1) == 0)
    def _():
        st_ref[...] = jnp.zeros_like(st_ref)

    C = GLA_CHUNK
    n_chunks = q_ref.shape[1] // C
    lane = lax.broadcasted_iota(jnp.int32, (C, LANES), 1)
    lane_sq = lax.broadcasted_iota(jnp.int32, (LANES, LANES), 1)
    tril = (lax.broadcasted_iota(jnp.int32, (C, C), 0)
            >= lax.broadcasted_iota(jnp.int32, (C, C), 1))
    g = g_ref[...]
    bf = jnp.bfloat16

    def chunk(c, carry):
        r0 = pl.multiple_of(c * C, C)
        rows = pl.ds(r0, C)
        for p in range(GLA_HEADS // 2):
            ksl = slice(p * LANES, (p + 1) * LANES)
            b = _cumsum_rows(la_ref[0, rows, ksl])
            b_last = b[C - 1:C, :]
            q = q_ref[0, rows, ksl].astype(jnp.float32) * (GLA_KEY_DIM ** -0.5)
            k = k_ref[0, rows, ksl].astype(jnp.float32)
            q_dec = q * jnp.exp(b)
            k_inv = (k * jnp.exp(-b)).astype(bf)
            k_dec = (k * jnp.exp(b_last - b)).astype(bf)
            st = st_ref[p]
            st_b = st.astype(bf)
            ut = []
            for hh in range(2):
                h = 2 * p + hh
                vsl = slice(h * GLA_VAL_DIM, (h + 1) * GLA_VAL_DIM)
                v = v_ref[0, rows, vsl]
                own = (lane >= GLA_KEY_DIM) if hh else (lane < GLA_KEY_DIM)
                qm = jnp.where(own, q_dec, 0.0).astype(bf)
                att = lax.dot_general(qm, k_inv, (((1,), (1,)), ((), ())),
                                      preferred_element_type=jnp.float32)
                att = jnp.where(tril, att, 0.0).astype(bf)
                o = jnp.dot(att, v, preferred_element_type=jnp.float32)
                o = o + lax.dot_general(qm, st_b, (((1,), (1,)), ((), ())),
                                        preferred_element_type=jnp.float32)
                o = _rms(o, g) * sr_ref[0, rows, vsl].astype(jnp.float32)
                o_ref[0, rows, vsl] = o.astype(o_ref.dtype)
                ut.append(lax.dot_general(v, k_dec, (((0,), (0,)), ((), ())),
                                          preferred_element_type=jnp.float32))
            st_ref[p] = st * jnp.exp(b_last) + jnp.where(lane_sq < GLA_KEY_DIM, ut[0], ut[1])
        return carry

    lax.fori_loop(0, n_chunks, chunk, 0)


def _gla(qg, kg, vg, la, sr, g_gla):
    B, S, _ = qg.shape
    tok = lambda w: pl.BlockSpec((1, TG_GLA, w), lambda b, i: (b, i, 0))
    return pl.pallas_call(
        _gla_kernel,
        name="gla",
        grid=(B, S // TG_GLA),
        in_specs=[tok(GLA_KEY_WIDTH), tok(GLA_KEY_WIDTH), tok(GLA_VAL_WIDTH),
                  tok(GLA_KEY_WIDTH), tok(GLA_VAL_WIDTH),
                  pl.BlockSpec((1, GLA_VAL_DIM), lambda b, i: (0, 0))],
        out_specs=tok(GLA_VAL_WIDTH),
        out_shape=jax.ShapeDtypeStruct((B, S, GLA_VAL_WIDTH), jnp.bfloat16),
        scratch_shapes=[pltpu.VMEM((GLA_HEADS // 2, GLA_VAL_DIM, LANES), jnp.float32)],
        compiler_params=pltpu.CompilerParams(dimension_semantics=("arbitrary", "arbitrary"),
                                             vmem_limit_bytes=VMEM_LIMIT),
    )(qg, kg, vg, la, sr, g_gla)


def _out_proj_kernel(o1_ref, o2_ref, o3_ref, s1_ref, s2_ref, s3_ref, og_ref, x_ref, mod_ref,
                     wout_ref, gpost_ref, gpre_ref, rw_ref, rb_ref,
                     x1_ref, h2_ref, ti_ref, tg_ref, oscr_ref, sscr_ref):
    tm = x_ref.shape[1]
    lane = lax.broadcasted_iota(jnp.int32, (tm, LANES), 1)
    for j, (d, o_ref, s_ref) in enumerate(((DILATIONS[1], o2_ref, s2_ref),
                                           (DILATIONS[2], o3_ref, s3_ref))):
        for r in range(d):
            rows = pl.ds(r, tm // d, stride=d)
            for s in range(ATT_WIDTH // LANES):
                c0 = r * ATT_WIDTH + s * LANES
                oscr_ref[j, s, rows, :] = o_ref[0, :, c0:c0 + LANES].astype(jnp.float32)
            sscr_ref[j, rows, :] = s_ref[0, :, r * LANES:(r + 1) * LANES]
    lses = [s1_ref[0], sscr_ref[0], sscr_ref[1]]
    m = jnp.maximum(jnp.maximum(lses[0], lses[1]), lses[2])
    es = [jnp.exp(t - m) for t in lses]
    inv = 1.0 / (es[0] + es[1] + es[2])
    ws = [e * inv for e in es]
    pairs = []
    for hp in range(ATT_HEADS // 2):
        sl = slice(hp * LANES, (hp + 1) * LANES)
        o_pats = [o1_ref[0, :, sl].astype(jnp.float32), oscr_ref[0, hp], oscr_ref[1, hp]]
        acc = jnp.zeros((tm, LANES), jnp.float32)
        for w, o in zip(ws, o_pats):
            wp = jnp.where(lane < ATT_HEAD_DIM, w[:, 2 * hp:2 * hp + 1], w[:, 2 * hp + 1:2 * hp + 2])
            acc = acc + wp * o
        pairs.append(acc.astype(jnp.bfloat16))
    mixed = jnp.concatenate(pairs + [og_ref[0]], axis=1)
    y = jnp.dot(mixed, wout_ref[...], preferred_element_type=jnp.float32)
    gate1 = mod_ref[0, 2:3, :]
    shift2 = mod_ref[0, 3:4, :]
    scale2 = mod_ref[0, 4:5, :]
    x1 = x_ref[0] + gate1 * _rms(y, gpost_ref[...])
    x1_ref[0] = x1
    h2 = _rms(x1, gpre_ref[...]) * (1.0 + scale2) + shift2
    h2_ref[0] = h2.astype(h2_ref.dtype)
    logits = jnp.dot(h2, rw_ref[...], preferred_element_type=jnp.float32,
                     precision=lax.Precision.HIGHEST) + rb_ref[...]
    lane_f = lane.astype(jnp.float32)
    vals, idxs = [], []
    for _ in range(TOP_K):
        mk = jnp.max(logits, axis=1, keepdims=True)
        ik = jnp.min(jnp.where(logits == mk, lane_f, float(LANES)), axis=1, keepdims=True)
        logits = jnp.where(lane_f == ik, -jnp.inf, logits)
        vals.append(mk)
        idxs.append(ik)
    ex = [jnp.exp(v - vals[0]) for v in vals]
    den = ex[0] + ex[1] + ex[2] + ex[3]
    ti = jnp.zeros((tm, LANES), jnp.float32)
    tg = jnp.zeros((tm, LANES), jnp.float32)
    for kk in range(TOP_K):
        ti = jnp.where(lane == kk, idxs[kk], ti)
        tg = jnp.where(lane == kk, ex[kk] / den, tg)
    ti_ref[0] = ti.astype(jnp.int32)
    tg_ref[0] = tg


def _out_proj(o_pats, st_pats, og, x, mod3, w_out, g_post, g_pre, rw, rb):
    B, S, D = x.shape
    tok = lambda w: pl.BlockSpec((1, TM_OUT, w), lambda b, i: (b, i, 0))
    const = lambda shape: pl.BlockSpec(shape, lambda b, i: tuple(0 for _ in shape))
    out_shapes = [jax.ShapeDtypeStruct((B, S, D), jnp.float32),
                  jax.ShapeDtypeStruct((B, S, D), jnp.bfloat16),
                  jax.ShapeDtypeStruct((B, S, LANES), jnp.int32),
                  jax.ShapeDtypeStruct((B, S, LANES), jnp.float32)]
    return pl.pallas_call(
        _out_proj_kernel,
        name="out_proj",
        grid=(B, S // TM_OUT),
        in_specs=[pl.BlockSpec((1, TM_OUT // d, d * ATT_WIDTH), lambda b, i: (b, i, 0))
                  for d in DILATIONS]
                 + [pl.BlockSpec((1, TM_OUT // d, d * LANES), lambda b, i: (b, i, 0))
                    for d in DILATIONS]
                 + [tok(GLA_VAL_WIDTH), tok(D),
                  pl.BlockSpec((1, N_MOD, D), lambda b, i: (b, 0, 0)),
                  const((D, D)), const((1, D)), const((1, D)),
                  const((D, LANES)), const((1, LANES))],
        out_specs=[tok(D), tok(D), tok(LANES), tok(LANES)],
        out_shape=out_shapes,
        scratch_shapes=[pltpu.VMEM((2, ATT_WIDTH // LANES, TM_OUT, LANES), jnp.float32),
                        pltpu.VMEM((2, TM_OUT, LANES), jnp.float32)],
        compiler_params=pltpu.CompilerParams(dimension_semantics=("arbitrary", "arbitrary"),
                                             vmem_limit_bytes=VMEM_LIMIT),
    )(*o_pats, *st_pats, og, x, mod3, w_out, g_post, g_pre, rw, rb)


def _split_gate_up_kernel(w_ref, wg_ref, wl_ref):
    group = 2 * LANES
    src = lax.broadcasted_iota(jnp.int32, (group, group), 0)
    dst = lax.broadcasted_iota(jnp.int32, (group, group), 1)
    want = jnp.where(dst < LANES, 2 * dst, 2 * (dst - LANES) + 1)
    perm = jnp.where(src == want, 1.0, 0.0).astype(jnp.bfloat16)
    for j in range(w_ref.shape[1] // group):
        t = jnp.dot(w_ref[:, j * group:(j + 1) * group].astype(jnp.bfloat16), perm,
                    preferred_element_type=jnp.float32)
        wg_ref[:, j * LANES:(j + 1) * LANES] = t[:, :LANES].astype(wg_ref.dtype)
        wl_ref[:, j * LANES:(j + 1) * LANES] = t[:, LANES:].astype(wl_ref.dtype)


def _split_gate_up(w_gate_up):
    E, K, N2 = w_gate_up.shape
    out = jax.ShapeDtypeStruct((E, K, N2 // 2), jnp.bfloat16)
    ospec = pl.BlockSpec((None, TK_SPLIT, N2 // 2), lambda e, i: (e, i, 0))
    return pl.pallas_call(
        _split_gate_up_kernel,
        name="split_gate_up",
        grid=(E, K // TK_SPLIT),
        in_specs=[pl.BlockSpec((None, TK_SPLIT, N2), lambda e, i: (e, i, 0))],
        out_specs=[ospec, ospec],
        out_shape=[out, out],
        compiler_params=pltpu.CompilerParams(dimension_semantics=("arbitrary", "arbitrary"),
                                             vmem_limit_bytes=VMEM_LIMIT),
    )(w_gate_up)


def _moe_kernel(be_ref, nu_ref, x_ref, wg_ref, wl_ref, bg_ref, bl_ref, wd_ref, bd_ref, o_ref):
    i = pl.program_id(0)

    @pl.when(i < nu_ref[0])
    def _():
        x = x_ref[...]
        xg = jnp.dot(x, wg_ref[...], preferred_element_type=jnp.float32) + bg_ref[...]
        xl = jnp.dot(x, wl_ref[...], preferred_element_type=jnp.float32) + bl_ref[...]
        xg = jnp.minimum(xg, SWIGLU_LIMIT)
        xl = jnp.clip(xl, -SWIGLU_LIMIT, SWIGLU_LIMIT)
        act = xg * (1.0 / (1.0 + jnp.exp(-SWIGLU_ALPHA * xg))) * (xl + 1.0)
        out = jnp.dot(act.astype(jnp.bfloat16), wd_ref[...],
                      preferred_element_type=jnp.float32) + bd_ref[...]
        o_ref[...] = out.astype(o_ref.dtype)

    @pl.when(i >= nu_ref[0])
    def _():
        o_ref[...] = jnp.zeros_like(o_ref)


def _moe(xs, blk_e, n_used, wg, wl, bg, bl, wd, bd):
    P, D = xs.shape
    n_blocks = P // TM_MOE
    wspec = lambda k, n: pl.BlockSpec((None, k, n), lambda i, be, nu: (be[i], 0, 0))
    grid_spec = pltpu.PrefetchScalarGridSpec(
        num_scalar_prefetch=2,
        grid=(n_blocks,),
        in_specs=[pl.BlockSpec((TM_MOE, D), lambda i, be, nu: (i, 0)),
                  wspec(D, D_FF), wspec(D, D_FF), wspec(1, D_FF), wspec(1, D_FF),
                  wspec(D_FF, D), wspec(1, D)],
        out_specs=pl.BlockSpec((TM_MOE, D), lambda i, be, nu: (i, 0)),
    )
    return pl.pallas_call(
        _moe_kernel,
        name="moe",
        grid_spec=grid_spec,
        out_shape=jax.ShapeDtypeStruct((P, D), jnp.bfloat16),
        compiler_params=pltpu.CompilerParams(dimension_semantics=("arbitrary",),
                                             vmem_limit_bytes=VMEM_LIMIT),
    )(blk_e, n_used, xs, wg, wl, bg, bl, wd, bd)


def _final_kernel(x1_ref, y0_ref, y1_ref, y2_ref, y3_ref, tg_ref, mod_ref, g_ref, o_ref):
    gates = tg_ref[...]
    y = jnp.zeros(x1_ref.shape, jnp.float32)
    for kk, yk_ref in enumerate((y0_ref, y1_ref, y2_ref, y3_ref)):
        y = y + gates[:, kk:kk + 1] * yk_ref[...].astype(jnp.float32)
    gate2 = mod_ref[0, 5:6, :]
    o_ref[...] = x1_ref[...] + gate2 * _rms(y, g_ref[...])


def _final(x1, yk, tg, mod3, g_post, seq_len):
    T, D = x1.shape
    per_batch = seq_len // TM_FIN
    nt = T // TM_FIN
    yspec = lambda kk: pl.BlockSpec((TM_FIN, D), lambda i: (kk * nt + i, 0))
    return pl.pallas_call(
        _final_kernel,
        name="final",
        grid=(nt,),
        in_specs=[pl.BlockSpec((TM_FIN, D), lambda i: (i, 0))]
                 + [yspec(kk) for kk in range(TOP_K)]
                 + [pl.BlockSpec((TM_FIN, LANES), lambda i: (i, 0)),
                  pl.BlockSpec((1, N_MOD, D), lambda i: (i // per_batch, 0, 0)),
                  pl.BlockSpec((1, D), lambda i: (0, 0))],
        out_specs=pl.BlockSpec((TM_FIN, D), lambda i: (i, 0)),
        out_shape=jax.ShapeDtypeStruct((T, D), jnp.float32),
        compiler_params=pltpu.CompilerParams(dimension_semantics=("arbitrary",),
                                             vmem_limit_bytes=VMEM_LIMIT),
    )(x1, yk, yk, yk, yk, tg, mod3, g_post)


def _pack_w_in(w_in):
    n = np.arange(LANES)
    src = np.where(n < 32, n, np.where(n < 64, n + 32, np.where(n < 96, n - 32, n)))
    perm = (np.arange(ATT_WIDTH // LANES)[:, None] * LANES + src[None, :]).reshape(-1)
    lr = jnp.pad(w_in[:, 3072:3072 + GLA_GATE_RANK], ((0, 0), (0, LANES - GLA_GATE_RANK)))
    packed = jnp.concatenate([w_in[:, perm], w_in[:, ATT_WIDTH + perm], w_in[:, 1024:3072], lr],
                             axis=1)
    return packed.astype(jnp.bfloat16)


def _rope_tables(seq_len):
    half = ATT_HEAD_DIM // 2
    inv_freq = ROPE_THETA ** (-jnp.arange(half, dtype=jnp.float32) / half)
    ang = jnp.arange(seq_len, dtype=jnp.float32)[:, None] * inv_freq[None, :]
    cos = jnp.tile(jnp.cos(ang), (1, LANES // half))
    sin = jnp.tile(jnp.sin(ang), (1, LANES // half))
    sign = jnp.where(jnp.arange(LANES) < LANES // 2, -1.0, 1.0)
    return cos, sin * sign


def _route(top_i, n_tokens):
    A = n_tokens * TOP_K
    e_flat = top_i.reshape(A)
    onehot = (e_flat[:, None] == jnp.arange(N_EXPERTS, dtype=jnp.int32)[None, :]).astype(jnp.int32)
    csum = jnp.cumsum(onehot, axis=0)
    rank = jnp.sum(onehot * csum, axis=1) - 1
    counts = csum[-1]
    padded = (counts + TM_MOE - 1) // TM_MOE * TM_MOE
    pend = jnp.cumsum(padded)
    pstart = pend - padded
    pos = pstart[e_flat] + rank
    n_blocks = -(-(A + N_EXPERTS * (TM_MOE - 1)) // TM_MOE)
    tok_buf = jnp.zeros((n_blocks * TM_MOE,), jnp.int32).at[pos].set(
        jnp.arange(A, dtype=jnp.int32) // TOP_K, unique_indices=True, mode="promise_in_bounds")
    blk_start = jnp.arange(n_blocks, dtype=jnp.int32) * TM_MOE
    blk_e = jnp.minimum(jnp.sum((pend[None, :] <= blk_start[:, None]).astype(jnp.int32), axis=1),
                        N_EXPERTS - 1)
    n_used = (pend[-1:] // TM_MOE).astype(jnp.int32)
    return pos, tok_buf, blk_e, n_used


def kernel(x, c, w_mod, b_mod, g_pre_mix, w_in, w_gate_lr, b_gate, g_gla, w_out, g_post_mix,
           g_pre_ffn, router_w, router_b, w_gate_up, b_gate_up, w_down, b_down, g_post_ffn):
    B, S, D = x.shape
    T = B * S
    bf = jnp.bfloat16
    cos_t, sin_t = _rope_tables(S)
    for l in range(w_mod.shape[0]):
        mod3 = _mod(c, w_mod[l], b_mod[l]).reshape(B, N_MOD, D)
        wlr = jnp.pad(w_gate_lr[l], ((0, LANES - GLA_GATE_RANK), (0, 0))).astype(bf)
        proj = _in_proj(x, mod3, g_pre_mix[l][None], _pack_w_in(w_in[l]), cos_t, sin_t, wlr,
                        b_gate[l][None])
        n_pat = len(DILATIONS)
        qa, ka, va = proj[:n_pat], proj[n_pat:2 * n_pat], proj[2 * n_pat:3 * n_pat]
        qg, kg, vg, sr, la = proj[3 * n_pat:]
        pats = [_attn_pattern(qa[j], ka[j], va[j], d) for j, d in enumerate(DILATIONS)]
        og = _gla(qg, kg, vg, la, sr, g_gla[l][None])
        rw = jnp.pad(router_w[l], ((0, 0), (0, LANES - N_EXPERTS)))
        rb = jnp.pad(router_b[l], (0, LANES - N_EXPERTS), constant_values=NEG_BIG)[None]
        x1, h2, ti, tg = _out_proj([p[0] for p in pats], [p[1] for p in pats], og, x, mod3,
                                   w_out[l].astype(bf), g_post_mix[l][None], g_pre_ffn[l][None],
                                   rw, rb)
        top_i = ti.reshape(T, LANES)[:, :TOP_K]
        pos, tok_buf, blk_e, n_used = _route(top_i, T)
        xs = h2.reshape(T, D).at[tok_buf].get(mode="promise_in_bounds")
        wg, wl = _split_gate_up(w_gate_up[l])
        out_buf = _moe(xs, blk_e, n_used, wg, wl,
                       b_gate_up[l][:, None, 0::2], b_gate_up[l][:, None, 1::2],
                       w_down[l].astype(bf), b_down[l][:, None, :])
        pos_kmajor = pos.reshape(T, TOP_K).T.reshape(-1)
        yk = out_buf.at[pos_kmajor].get(mode="promise_in_bounds")
        x = _final(x1.reshape(T, D), yk, tg.reshape(T, LANES), mod3, g_post_ffn[l][None],
                   S).reshape(B, S, D)
    return x
```

```python
import functools

import numpy as np
import jax
import jax.numpy as jnp
from jax import lax
from jax.experimental import pallas as pl
from jax.experimental.pallas import tpu as pltpu
from jax.experimental.pallas import tpu_sc as plsc

D_MODEL = 1024
ATT_HEADS = 8
ATT_HEAD_DIM = 64
ATT_WIDTH = ATT_HEADS * ATT_HEAD_DIM
DILATIONS = (1, 4, 16)
ATT_SPAN = 128
ROPE_THETA = 10000.0
GLA_HEADS = 4
GLA_KEY_DIM = 64
GLA_VAL_DIM = 128
GLA_KEY_WIDTH = GLA_HEADS * GLA_KEY_DIM
GLA_VAL_WIDTH = GLA_HEADS * GLA_VAL_DIM
GLA_GATE_RANK = 16
GLA_GATE_NORMALIZER = 16.0
GLA_CHUNK = 64
N_EXPERTS = 32
TOP_K = 4
D_FF = D_MODEL
SWIGLU_LIMIT = 7.0
SWIGLU_ALPHA = 1.702
NORM_EPS = 1e-6
N_MOD = 6

LANES = 128
NEG_BIG = -1e30

TM_PROJ = 512
TQ_ATT = 128
TG_GLA = 512
TM_OUT = 512
TM_MOE = 256
TM_FIN = 256
TN_MOD = 512
TK_SPLIT = 512
SC_INDEX_WINDOW = 128
SC_GATHER_WINDOW = 64
VMEM_LIMIT = 48 * 1024 * 1024

_C_QA, _C_KA, _C_VA = 0, 512, 1024
_C_QG, _C_KG, _C_VG, _C_RG, _C_LR = 1536, 1792, 2048, 2560, 3072
D_IN_PACKED = 3200


def _rms(x, g):
    return x * lax.rsqrt(jnp.mean(x * x, axis=-1, keepdims=True) + NORM_EPS) * g


def _pack_bf16_pairs(x):
    n = x.shape[1] // 2
    u = lax.bitcast_convert_type(x.astype(jnp.bfloat16).astype(jnp.float32), jnp.uint32)
    return (u[:, :n] >> 16) | (u[:, n:] & jnp.uint32(0xFFFF0000))


def _unpack_bf16_pairs(w):
    lo = lax.bitcast_convert_type(w << 16, jnp.float32)
    hi = lax.bitcast_convert_type(w & jnp.uint32(0xFFFF0000), jnp.float32)
    return lo, hi


def _gather_rows(data, idx):
    n_rows = idx.shape[0]
    width = data.shape[1]
    mesh = plsc.VectorSubcoreMesh(core_axis_name="core", subcore_axis_name="subcore")
    n_workers = mesh.num_cores * mesh.num_subcores
    per_worker = n_rows // n_workers
    assert per_worker * n_workers == n_rows and per_worker % SC_INDEX_WINDOW == 0
    halves = SC_INDEX_WINDOW // SC_GATHER_WINDOW

    @pl.kernel(out_type=jax.ShapeDtypeStruct((n_rows, width), data.dtype), mesh=mesh,
               name="gather_rows",
               scratch_types=[pltpu.VMEM((1, SC_INDEX_WINDOW), jnp.int32),
                              pltpu.VMEM((SC_GATHER_WINDOW, width), data.dtype)])
    def gather(x_hbm, i_hbm, o_hbm, idx_vmem, rows_vmem):
        worker = lax.axis_index("core") * mesh.num_subcores + lax.axis_index("subcore")
        base = worker * per_worker

        @pl.loop(0, per_worker // SC_INDEX_WINDOW)
        def _(j):
            off = base + j * SC_INDEX_WINDOW
            pltpu.sync_copy(i_hbm.at[:, pl.ds(off, SC_INDEX_WINDOW)], idx_vmem)
            for h in range(halves):
                part = idx_vmem.at[0, pl.ds(h * SC_GATHER_WINDOW, SC_GATHER_WINDOW)]
                pltpu.sync_copy(x_hbm.at[part], rows_vmem)
                pltpu.sync_copy(rows_vmem,
                                o_hbm.at[pl.ds(off + h * SC_GATHER_WINDOW, SC_GATHER_WINDOW)])

    return gather(data, idx.reshape(1, n_rows))


def _mod_kernel(ct_ref, w_ref, b_ref, o_ref):
    ct = ct_ref[...]
    s = ct * (1.0 / (1.0 + jnp.exp(-ct)))
    w = w_ref[...]
    rows = [jnp.sum(s[:, b:b + 1] * w, axis=0, keepdims=True) for b in range(ct.shape[1])]
    o_ref[...] = jnp.concatenate(rows, axis=0) + b_ref[...]


def _mod(c, w_mod, b_mod):
    B = c.shape[0]
    n = w_mod.shape[1]
    return pl.pallas_call(
        _mod_kernel,
        name="mod",
        grid=(n // TN_MOD,),
        in_specs=[pl.BlockSpec((D_MODEL, B), lambda j: (0, 0)),
                  pl.BlockSpec((D_MODEL, TN_MOD), lambda j: (0, j)),
                  pl.BlockSpec((1, TN_MOD), lambda j: (0, j))],
        out_specs=pl.BlockSpec((B, TN_MOD), lambda j: (0, j)),
        out_shape=jax.ShapeDtypeStruct((B, n), jnp.float32),
        compiler_params=pltpu.CompilerParams(dimension_semantics=("arbitrary",),
                                             vmem_limit_bytes=VMEM_LIMIT),
    )(c.T, w_mod, b_mod.reshape(1, n))


def _store_residue_views(scr_ref, out_refs):
    ns, tm, _ = scr_ref.shape
    w = ns * LANES
    for d, ref in zip(DILATIONS, out_refs):
        for r in range(d):
            for s in range(ns):
                c0 = r * w + s * LANES
                ref[0, :, c0:c0 + LANES] = scr_ref[s, pl.ds(r, tm // d, stride=d), :].astype(ref.dtype)


def _in_proj_kernel(x_ref, mod_ref, g_ref, w_ref, cos_ref, sin_ref, wlr_ref, bg_ref,
                    qa1_ref, qa4_ref, qa16_ref, ka1_ref, ka4_ref, ka16_ref,
                    va1_ref, va4_ref, va16_ref, qg_ref, kg_ref, vg_ref, sr_ref, la_ref,
                    qs_ref, ks_ref, vs_ref):
    x = x_ref[0]
    shift = mod_ref[0, 0:1, :]
    scale = mod_ref[0, 1:2, :]
    h = (_rms(x, g_ref[...]) * (1.0 + scale) + shift).astype(jnp.bfloat16)

    def proj(c0, width):
        return jnp.dot(h, w_ref[:, c0:c0 + width], preferred_element_type=jnp.float32)

    cos = cos_ref[...]
    sin = sin_ref[...]

    def rope(c0, scr_ref, mult):
        p = proj(c0, ATT_WIDTH)
        for s in range(ATT_WIDTH // LANES):
            t = p[:, s * LANES:(s + 1) * LANES]
            r = t * cos + pltpu.roll(t, LANES // 2, axis=1) * sin
            if mult != 1.0:
                r = r * mult
            scr_ref[s] = r

    rope(_C_QA, qs_ref, ATT_HEAD_DIM ** -0.5)
    _store_residue_views(qs_ref, (qa1_ref, qa4_ref, qa16_ref))
    rope(_C_KA, ks_ref, 1.0)
    _store_residue_views(ks_ref, (ka1_ref, ka4_ref, ka16_ref))
    pv = proj(_C_VA, ATT_WIDTH)
    for s in range(ATT_WIDTH // LANES):
        vs_ref[s] = pv[:, s * LANES:(s + 1) * LANES]
    _store_residue_views(vs_ref, (va1_ref, va4_ref, va16_ref))
    qg_ref[0] = proj(_C_QG, GLA_KEY_WIDTH).astype(qg_ref.dtype)
    kg_ref[0] = proj(_C_KG, GLA_KEY_WIDTH).astype(kg_ref.dtype)
    vg_ref[0] = proj(_C_VG, GLA_VAL_WIDTH).astype(vg_ref.dtype)
    r = proj(_C_RG, GLA_VAL_WIDTH)
    sr_ref[0] = (r * (1.0 / (1.0 + jnp.exp(-r)))).astype(sr_ref.dtype)
    lr = proj(_C_LR, LANES).astype(jnp.bfloat16)
    z = jnp.dot(lr, wlr_ref[...], preferred_element_type=jnp.float32) + bg_ref[...]
    log_sig = jnp.minimum(z, 0.0) - jnp.log(1.0 + jnp.exp(-jnp.abs(z)))
    la_ref[0] = log_sig * (1.0 / GLA_GATE_NORMALIZER)


def _in_proj(x, mod3, g_pre, w_packed, cos_t, sin_t, wlr, bg):
    B, S, D = x.shape
    nb = S // TM_PROJ
    tok = lambda w: pl.BlockSpec((1, TM_PROJ, w), lambda b, i: (b, i, 0))
    const = lambda shape: pl.BlockSpec(shape, lambda b, i: tuple(0 for _ in shape))
    bf = jnp.bfloat16
    att_shapes = [jax.ShapeDtypeStruct((B, S // d, d * ATT_WIDTH), bf) for d in DILATIONS] * 3
    att_specs = [pl.BlockSpec((1, TM_PROJ // d, d * ATT_WIDTH), lambda b, i: (b, i, 0))
                 for d in DILATIONS] * 3
    gla_shapes = [jax.ShapeDtypeStruct((B, S, w), dt) for w, dt in
                  [(GLA_KEY_WIDTH, bf), (GLA_KEY_WIDTH, bf),
                   (GLA_VAL_WIDTH, bf), (GLA_VAL_WIDTH, bf), (GLA_KEY_WIDTH, jnp.float32)]]
    return pl.pallas_call(
        _in_proj_kernel,
        name="in_proj",
        grid=(B, nb),
        in_specs=[tok(D),
                  pl.BlockSpec((1, N_MOD, D), lambda b, i: (b, 0, 0)),
                  const((1, D)),
                  const((D, D_IN_PACKED)),
                  pl.BlockSpec((TM_PROJ, LANES), lambda b, i: (i, 0)),
                  pl.BlockSpec((TM_PROJ, LANES), lambda b, i: (i, 0)),
                  const((LANES, GLA_KEY_WIDTH)),
                  const((1, GLA_KEY_WIDTH))],
        out_specs=att_specs + [tok(s.shape[-1]) for s in gla_shapes],
        out_shape=att_shapes + gla_shapes,
        scratch_shapes=[pltpu.VMEM((ATT_WIDTH // LANES, TM_PROJ, LANES), jnp.float32)] * 3,
        compiler_params=pltpu.CompilerParams(dimension_semantics=("arbitrary", "arbitrary"),
                                             vmem_limit_bytes=VMEM_LIMIT),
    )(x, mod3, g_pre, w_packed, cos_t, sin_t, wlr, bg)


def _attn_kernel(q_ref, kp_ref, kc_ref, vp_ref, vc_ref, o_ref, st_ref):
    i = pl.program_id(2)
    tq = q_ref.shape[1]
    row = lax.broadcasted_iota(jnp.int32, (tq, 2 * tq), 0)
    col = lax.broadcasted_iota(jnp.int32, (tq, 2 * tq), 1)
    valid = (col >= row) & (col <= row + ATT_SPAN) & ((col >= tq) | (i > 0))
    bias = jnp.where(valid, 0.0, NEG_BIG)
    lane = lax.broadcasted_iota(jnp.int32, (tq, LANES), 1)
    q_head = (lane >> 5) & 1
    stats = jnp.zeros((tq, LANES), jnp.float32)
    for hp in range(ATT_HEADS // 2):
        sl = slice(hp * LANES, (hp + 1) * LANES)
        q2 = q_ref[0, :, sl]
        kcat = jnp.concatenate([kp_ref[0, :, sl], kc_ref[0, :, sl]], axis=0)
        vcat = jnp.concatenate([vp_ref[0, :, sl], vc_ref[0, :, sl]], axis=0)
        outs = []
        for hh in range(2):
            qm = jnp.where(q_head == hh, q2, jnp.zeros_like(q2))
            s = lax.dot_general(qm, kcat, (((1,), (1,)), ((), ())),
                                preferred_element_type=jnp.float32) + bias
            m = jnp.max(s, axis=1, keepdims=True)
            p = jnp.exp(s - m)
            l = jnp.sum(p, axis=1, keepdims=True)
            o = jnp.dot(p.astype(vcat.dtype), vcat, preferred_element_type=jnp.float32)
            outs.append(o / l)
            stats = jnp.where(lane == 2 * hp + hh, m + jnp.log(l), stats)
        o_ref[0, :, sl] = jnp.where(lane < ATT_HEAD_DIM, outs[0], outs[1]).astype(o_ref.dtype)
    st_ref[0] = stats


def _attn_pattern(qv, kv, vv, dil):
    B, L, _ = qv.shape
    W = ATT_WIDTH
    nq = L // TQ_ATT
    cur = pl.BlockSpec((1, TQ_ATT, W), lambda b, r, i: (b, i, r))
    prev = pl.BlockSpec((1, TQ_ATT, W), lambda b, r, i: (b, jnp.maximum(i - 1, 0), r))
    o, st = pl.pallas_call(
        _attn_kernel,
        name=f"attn_d{dil}",
        grid=(B, dil, nq),
        in_specs=[cur, prev, cur, prev, cur],
        out_specs=[cur, pl.BlockSpec((1, TQ_ATT, LANES), lambda b, r, i: (b, i, r))],
        out_shape=[jax.ShapeDtypeStruct((B, L, dil * W), jnp.bfloat16),
                   jax.ShapeDtypeStruct((B, L, dil * LANES), jnp.float32)],
        compiler_params=pltpu.CompilerParams(
            dimension_semantics=("arbitrary", "arbitrary", "arbitrary"),
            vmem_limit_bytes=VMEM_LIMIT),
    )(qv, kv, kv, vv, vv)
    return o, st


def _cumsum_rows(x):
    n = x.shape[0]
    row = lax.broadcasted_iota(jnp.int32, x.shape, 0)
    s = 1
    while s < n:
        x = x + jnp.where(row >= s, pltpu.roll(x, s, axis=0), 0.0)
        s *= 2
    return x


def _gla_kernel(q_ref, k_ref, v_ref, la_ref, sr_ref, g_ref, o_ref, st_ref):
    @pl.when(pl.program_id(1) == 0)
    def _():
        st_ref[...] = jnp.zeros_like(st_ref)

    C = GLA_CHUNK
    n_chunks = q_ref.shape[1] // C
    lane = lax.broadcasted_iota(jnp.int32, (C, LANES), 1)
    lane_sq = lax.broadcasted_iota(jnp.int32, (LANES, LANES), 1)
    tril = (lax.broadcasted_iota(jnp.int32, (C, C), 0)
            >= lax.broadcasted_iota(jnp.int32, (C, C), 1))
    g = g_ref[...]
    bf = jnp.bfloat16

    def chunk(c, carry):
        r0 = pl.multiple_of(c * C, C)
        rows = pl.ds(r0, C)
        for p in range(GLA_HEADS // 2):
            ksl = slice(p * LANES, (p + 1) * LANES)
            b = _cumsum_rows(la_ref[0, rows, ksl])
            b_last = b[C - 1:C, :]
            q = q_ref[0, rows, ksl].astype(jnp.float32) * (GLA_KEY_DIM ** -0.5)
            k = k_ref[0, rows, ksl].astype(jnp.float32)
            q_dec = q * jnp.exp(b)
            k_inv = (k * jnp.exp(-b)).astype(bf)
            k_dec = (k * jnp.exp(b_last - b)).astype(bf)
            st = st_ref[p]
            st_b = st.astype(bf)
            ut = []
            for hh in range(2):
                h = 2 * p + hh
                vsl = slice(h * GLA_VAL_DIM, (h + 1) * GLA_VAL_DIM)
                v = v_ref[0, rows, vsl]
                own = (lane >= GLA_KEY_DIM) if hh else (lane < GLA_KEY_DIM)
                qm = jnp.where(own, q_dec, 0.0).astype(bf)
                att = lax.dot_general(qm, k_inv, (((1,), (1,)), ((), ())),
                                      preferred_element_type=jnp.float32)
                att = jnp.where(tril, att, 0.0).astype(bf)
                o = jnp.dot(att, v, preferred_element_type=jnp.float32)
                o = o + lax.dot_general(qm, st_b, (((1,), (1,)), ((), ())),
                                        preferred_element_type=jnp.float32)
                o = _rms(o, g) * sr_ref[0, rows, vsl].astype(jnp.float32)
                o_ref[0, rows, vsl] = o.astype(o_ref.dtype)
                ut.append(lax.dot_general(v, k_dec, (((0,), (0,)), ((), ())),
                                          preferred_element_type=jnp.float32))
            st_ref[p] = st * jnp.exp(b_last) + jnp.where(lane_sq < GLA_KEY_DIM, ut[0], ut[1])
        return carry

    lax.fori_loop(0, n_chunks, chunk, 0)


def _gla(qg, kg, vg, la, sr, g_gla):
    B, S, _ = qg.shape
    tok = lambda w: pl.BlockSpec((1, TG_GLA, w), lambda b, i: (b, i, 0))
    return pl.pallas_call(
        _gla_kernel,
        name="gla",
        grid=(B, S // TG_GLA),
        in_specs=[tok(GLA_KEY_WIDTH), tok(GLA_KEY_WIDTH), tok(GLA_VAL_WIDTH),
                  tok(GLA_KEY_WIDTH), tok(GLA_VAL_WIDTH),
                  pl.BlockSpec((1, GLA_VAL_DIM), lambda b, i: (0, 0))],
        out_specs=tok(GLA_VAL_WIDTH),
        out_shape=jax.ShapeDtypeStruct((B, S, GLA_VAL_WIDTH), jnp.bfloat16),
        scratch_shapes=[pltpu.VMEM((GLA_HEADS // 2, GLA_VAL_DIM, LANES), jnp.float32)],
        compiler_params=pltpu.CompilerParams(dimension_semantics=("arbitrary", "arbitrary"),
                                             vmem_limit_bytes=VMEM_LIMIT),
    )(qg, kg, vg, la, sr, g_gla)


def _out_proj_kernel(o1_ref, o2_ref, o3_ref, s1_ref, s2_ref, s3_ref, og_ref, x_ref, mod_ref,
                     wout_ref, gpost_ref, gpre_ref, rw_ref, rb_ref,
                     x1_ref, h2_ref, ti_ref, tg_ref, oscr_ref, sscr_ref):
    tm = x_ref.shape[1]
    lane = lax.broadcasted_iota(jnp.int32, (tm, LANES), 1)
    for j, (d, o_ref, s_ref) in enumerate(((DILATIONS[1], o2_ref, s2_ref),
                                           (DILATIONS[2], o3_ref, s3_ref))):
        for r in range(d):
            rows = pl.ds(r, tm // d, stride=d)
            for s in range(ATT_WIDTH // LANES):
                c0 = r * ATT_WIDTH + s * LANES
                oscr_ref[j, s, rows, :] = o_ref[0, :, c0:c0 + LANES].astype(jnp.float32)
            sscr_ref[j, rows, :] = s_ref[0, :, r * LANES:(r + 1) * LANES]
    lses = [s1_ref[0], sscr_ref[0], sscr_ref[1]]
    m = jnp.maximum(jnp.maximum(lses[0], lses[1]), lses[2])
    es = [jnp.exp(t - m) for t in lses]
    inv = 1.0 / (es[0] + es[1] + es[2])
    ws = [e * inv for e in es]
    pairs = []
    for hp in range(ATT_HEADS // 2):
        sl = slice(hp * LANES, (hp + 1) * LANES)
        o_pats = [o1_ref[0, :, sl].astype(jnp.float32), oscr_ref[0, hp], oscr_ref[1, hp]]
        acc = jnp.zeros((tm, LANES), jnp.float32)
        for w, o in zip(ws, o_pats):
            wp = jnp.where(lane < ATT_HEAD_DIM, w[:, 2 * hp:2 * hp + 1], w[:, 2 * hp + 1:2 * hp + 2])
            acc = acc + wp * o
        pairs.append(acc.astype(jnp.bfloat16))
    mixed = jnp.concatenate(pairs + [og_ref[0]], axis=1)
    y = jnp.dot(mixed, wout_ref[...], preferred_element_type=jnp.float32)
    gate1 = mod_ref[0, 2:3, :]
    shift2 = mod_ref[0, 3:4, :]
    scale2 = mod_ref[0, 4:5, :]
    x1 = x_ref[0] + gate1 * _rms(y, gpost_ref[...])
    x1_ref[0] = x1
    h2 = _rms(x1, gpre_ref[...]) * (1.0 + scale2) + shift2
    h2_ref[0] = _pack_bf16_pairs(h2)
    logits = jnp.dot(h2, rw_ref[...], preferred_element_type=jnp.float32,
                     precision=lax.Precision.HIGHEST) + rb_ref[...]
    lane_f = lane.astype(jnp.float32)
    vals, idxs = [], []
    for _ in range(TOP_K):
        mk = jnp.max(logits, axis=1, keepdims=True)
        ik = jnp.min(jnp.where(logits == mk, lane_f, float(LANES)), axis=1, keepdims=True)
        logits = jnp.where(lane_f == ik, -jnp.inf, logits)
        vals.append(mk)
        idxs.append(ik)
    ex = [jnp.exp(v - vals[0]) for v in vals]
    den = ex[0] + ex[1] + ex[2] + ex[3]
    ti = jnp.zeros((tm, LANES), jnp.float32)
    tg = jnp.zeros((tm, LANES), jnp.float32)
    for kk in range(TOP_K):
        ti = jnp.where(lane == kk, idxs[kk], ti)
        tg = jnp.where(lane == kk, ex[kk] / den, tg)
    ti_ref[0] = ti.astype(jnp.int32)
    tg_ref[0] = tg


def _out_proj(o_pats, st_pats, og, x, mod3, w_out, g_post, g_pre, rw, rb):
    B, S, D = x.shape
    tok = lambda w: pl.BlockSpec((1, TM_OUT, w), lambda b, i: (b, i, 0))
    const = lambda shape: pl.BlockSpec(shape, lambda b, i: tuple(0 for _ in shape))
    out_shapes = [jax.ShapeDtypeStruct((B, S, D), jnp.float32),
                  jax.ShapeDtypeStruct((B, S, D // 2), jnp.uint32),
                  jax.ShapeDtypeStruct((B, S, LANES), jnp.int32),
                  jax.ShapeDtypeStruct((B, S, LANES), jnp.float32)]
    return pl.pallas_call(
        _out_proj_kernel,
        name="out_proj",
        grid=(B, S // TM_OUT),
        in_specs=[pl.BlockSpec((1, TM_OUT // d, d * ATT_WIDTH), lambda b, i: (b, i, 0))
                  for d in DILATIONS]
                 + [pl.BlockSpec((1, TM_OUT // d, d * LANES), lambda b, i: (b, i, 0))
                    for d in DILATIONS]
                 + [tok(GLA_VAL_WIDTH), tok(D),
                  pl.BlockSpec((1, N_MOD, D), lambda b, i: (b, 0, 0)),
                  const((D, D)), const((1, D)), const((1, D)),
                  const((D, LANES)), const((1, LANES))],
        out_specs=[tok(D), tok(D // 2), tok(LANES), tok(LANES)],
        out_shape=out_shapes,
        scratch_shapes=[pltpu.VMEM((2, ATT_WIDTH // LANES, TM_OUT, LANES), jnp.float32),
                        pltpu.VMEM((2, TM_OUT, LANES), jnp.float32)],
        compiler_params=pltpu.CompilerParams(dimension_semantics=("arbitrary", "arbitrary"),
                                             vmem_limit_bytes=VMEM_LIMIT),
    )(*o_pats, *st_pats, og, x, mod3, w_out, g_post, g_pre, rw, rb)


def _split_gate_up_kernel(w_ref, wg_ref, wl_ref):
    group = 2 * LANES
    src = lax.broadcasted_iota(jnp.int32, (group, group), 0)
    dst = lax.broadcasted_iota(jnp.int32, (group, group), 1)
    want = jnp.where(dst < LANES, 2 * dst, 2 * (dst - LANES) + 1)
    perm = jnp.where(src == want, 1.0, 0.0).astype(jnp.bfloat16)
    for j in range(w_ref.shape[1] // group):
        t = jnp.dot(w_ref[:, j * group:(j + 1) * group].astype(jnp.bfloat16), perm,
                    preferred_element_type=jnp.float32)
        wg_ref[:, j * LANES:(j + 1) * LANES] = t[:, :LANES].astype(wg_ref.dtype)
        wl_ref[:, j * LANES:(j + 1) * LANES] = t[:, LANES:].astype(wl_ref.dtype)


def _split_gate_up(w_gate_up):
    E, K, N2 = w_gate_up.shape
    out = jax.ShapeDtypeStruct((E, K, N2 // 2), jnp.bfloat16)
    ospec = pl.BlockSpec((None, TK_SPLIT, N2 // 2), lambda e, i: (e, i, 0))
    return pl.pallas_call(
        _split_gate_up_kernel,
        name="split_gate_up",
        grid=(E, K // TK_SPLIT),
        in_specs=[pl.BlockSpec((None, TK_SPLIT, N2), lambda e, i: (e, i, 0))],
        out_specs=[ospec, ospec],
        out_shape=[out, out],
        compiler_params=pltpu.CompilerParams(dimension_semantics=("arbitrary", "arbitrary"),
                                             vmem_limit_bytes=VMEM_LIMIT),
    )(w_gate_up)


def _moe_kernel(be_ref, nu_ref, x_ref, wg_ref, wl_ref, bg_ref, bl_ref, wd_ref, bd_ref, o_ref):
    i = pl.program_id(0)

    @pl.when(i < nu_ref[0])
    def _():
        lo, hi = _unpack_bf16_pairs(x_ref[...])
        x = jnp.concatenate([lo, hi], axis=1).astype(jnp.bfloat16)
        xg = jnp.dot(x, wg_ref[...], preferred_element_type=jnp.float32) + bg_ref[...]
        xl = jnp.dot(x, wl_ref[...], preferred_element_type=jnp.float32) + bl_ref[...]
        xg = jnp.minimum(xg, SWIGLU_LIMIT)
        xl = jnp.clip(xl, -SWIGLU_LIMIT, SWIGLU_LIMIT)
        act = xg * (1.0 / (1.0 + jnp.exp(-SWIGLU_ALPHA * xg))) * (xl + 1.0)
        out = jnp.dot(act.astype(jnp.bfloat16), wd_ref[...],
                      preferred_element_type=jnp.float32) + bd_ref[...]
        o_ref[...] = _pack_bf16_pairs(out)

    @pl.when(i >= nu_ref[0])
    def _():
        o_ref[...] = jnp.zeros_like(o_ref)


def _moe(xs, blk_e, n_used, wg, wl, bg, bl, wd, bd):
    P, half = xs.shape
    D = 2 * half
    n_blocks = P // TM_MOE
    wspec = lambda k, n: pl.BlockSpec((None, k, n), lambda i, be, nu: (be[i], 0, 0))
    grid_spec = pltpu.PrefetchScalarGridSpec(
        num_scalar_prefetch=2,
        grid=(n_blocks,),
        in_specs=[pl.BlockSpec((TM_MOE, half), lambda i, be, nu: (i, 0)),
                  wspec(D, D_FF), wspec(D, D_FF), wspec(1, D_FF), wspec(1, D_FF),
                  wspec(D_FF, D), wspec(1, D)],
        out_specs=pl.BlockSpec((TM_MOE, half), lambda i, be, nu: (i, 0)),
    )
    return pl.pallas_call(
        _moe_kernel,
        name="moe",
        grid_spec=grid_spec,
        out_shape=jax.ShapeDtypeStruct((P, half), jnp.uint32),
        compiler_params=pltpu.CompilerParams(dimension_semantics=("arbitrary",),
                                             vmem_limit_bytes=VMEM_LIMIT),
    )(blk_e, n_used, xs, wg, wl, bg, bl, wd, bd)


def _final_kernel(x1_ref, y0_ref, y1_ref, y2_ref, y3_ref, tg_ref, mod_ref, g_ref, o_ref):
    gates = tg_ref[...]
    half = y0_ref.shape[1]
    y_lo = jnp.zeros((x1_ref.shape[0], half), jnp.float32)
    y_hi = jnp.zeros((x1_ref.shape[0], half), jnp.float32)
    for kk, yk_ref in enumerate((y0_ref, y1_ref, y2_ref, y3_ref)):
        lo, hi = _unpack_bf16_pairs(yk_ref[...])
        y_lo = y_lo + gates[:, kk:kk + 1] * lo
        y_hi = y_hi + gates[:, kk:kk + 1] * hi
    y = jnp.concatenate([y_lo, y_hi], axis=1)
    gate2 = mod_ref[0, 5:6, :]
    o_ref[...] = x1_ref[...] + gate2 * _rms(y, g_ref[...])


def _final(x1, yk, tg, mod3, g_post, seq_len):
    T, D = x1.shape
    per_batch = seq_len // TM_FIN
    nt = T // TM_FIN
    yspec = lambda kk: pl.BlockSpec((TM_FIN, D // 2), lambda i: (kk * nt + i, 0))
    return pl.pallas_call(
        _final_kernel,
        name="final",
        grid=(nt,),
        in_specs=[pl.BlockSpec((TM_FIN, D), lambda i: (i, 0))]
                 + [yspec(kk) for kk in range(TOP_K)]
                 + [pl.BlockSpec((TM_FIN, LANES), lambda i: (i, 0)),
                  pl.BlockSpec((1, N_MOD, D), lambda i: (i // per_batch, 0, 0)),
                  pl.BlockSpec((1, D), lambda i: (0, 0))],
        out_specs=pl.BlockSpec((TM_FIN, D), lambda i: (i, 0)),
        out_shape=jax.ShapeDtypeStruct((T, D), jnp.float32),
        compiler_params=pltpu.CompilerParams(dimension_semantics=("arbitrary",),
                                             vmem_limit_bytes=VMEM_LIMIT),
    )(x1, yk, yk, yk, yk, tg, mod3, g_post)


def _pack_w_in(w_in):
    n = np.arange(LANES)
    src = np.where(n < 32, n, np.where(n < 64, n + 32, np.where(n < 96, n - 32, n)))
    perm = (np.arange(ATT_WIDTH // LANES)[:, None] * LANES + src[None, :]).reshape(-1)
    lr = jnp.pad(w_in[:, 3072:3072 + GLA_GATE_RANK], ((0, 0), (0, LANES - GLA_GATE_RANK)))
    packed = jnp.concatenate([w_in[:, perm], w_in[:, ATT_WIDTH + perm], w_in[:, 1024:3072], lr],
                             axis=1)
    return packed.astype(jnp.bfloat16)


def _rope_tables(seq_len):
    half = ATT_HEAD_DIM // 2
    inv_freq = ROPE_THETA ** (-jnp.arange(half, dtype=jnp.float32) / half)
    ang = jnp.arange(seq_len, dtype=jnp.float32)[:, None] * inv_freq[None, :]
    cos = jnp.tile(jnp.cos(ang), (1, LANES // half))
    sin = jnp.tile(jnp.sin(ang), (1, LANES // half))
    sign = jnp.where(jnp.arange(LANES) < LANES // 2, -1.0, 1.0)
    return cos, sin * sign


def _route(top_i, n_tokens):
    A = n_tokens * TOP_K
    e_flat = top_i.reshape(A)
    onehot = (e_flat[:, None] == jnp.arange(N_EXPERTS, dtype=jnp.int32)[None, :]).astype(jnp.int32)
    csum = jnp.cumsum(onehot, axis=0)
    rank = jnp.sum(onehot * csum, axis=1) - 1
    counts = csum[-1]
    padded = (counts + TM_MOE - 1) // TM_MOE * TM_MOE
    pend = jnp.cumsum(padded)
    pstart = pend - padded
    pos = pstart[e_flat] + rank
    n_blocks = -(-(A + N_EXPERTS * (TM_MOE - 1)) // TM_MOE)
    tok_buf = jnp.zeros((n_blocks * TM_MOE,), jnp.int32).at[pos].set(
        jnp.arange(A, dtype=jnp.int32) // TOP_K, unique_indices=True, mode="promise_in_bounds")
    blk_start = jnp.arange(n_blocks, dtype=jnp.int32) * TM_MOE
    blk_e = jnp.minimum(jnp.sum((pend[None, :] <= blk_start[:, None]).astype(jnp.int32), axis=1),
                        N_EXPERTS - 1)
    n_used = (pend[-1:] // TM_MOE).astype(jnp.int32)
    return pos, tok_buf, blk_e, n_used


def kernel(x, c, w_mod, b_mod, g_pre_mix, w_in, w_gate_lr, b_gate, g_gla, w_out, g_post_mix,
           g_pre_ffn, router_w, router_b, w_gate_up, b_gate_up, w_down, b_down, g_post_ffn):
    B, S, D = x.shape
    T = B * S
    bf = jnp.bfloat16
    cos_t, sin_t = _rope_tables(S)
    for l in range(w_mod.shape[0]):
        mod3 = _mod(c, w_mod[l], b_mod[l]).reshape(B, N_MOD, D)
        wlr = jnp.pad(w_gate_lr[l], ((0, LANES - GLA_GATE_RANK), (0, 0))).astype(bf)
        proj = _in_proj(x, mod3, g_pre_mix[l][None], _pack_w_in(w_in[l]), cos_t, sin_t, wlr,
                        b_gate[l][None])
        n_pat = len(DILATIONS)
        qa, ka, va = proj[:n_pat], proj[n_pat:2 * n_pat], proj[2 * n_pat:3 * n_pat]
        qg, kg, vg, sr, la = proj[3 * n_pat:]
        pats = [_attn_pattern(qa[j], ka[j], va[j], d) for j, d in enumerate(DILATIONS)]
        og = _gla(qg, kg, vg, la, sr, g_gla[l][None])
        rw = jnp.pad(router_w[l], ((0, 0), (0, LANES - N_EXPERTS)))
        rb = jnp.pad(router_b[l], (0, LANES - N_EXPERTS), constant_values=NEG_BIG)[None]
        x1, h2, ti, tg = _out_proj([p[0] for p in pats], [p[1] for p in pats], og, x, mod3,
                                   w_out[l].astype(bf), g_post_mix[l][None], g_pre_ffn[l][None],
                                   rw, rb)
        top_i = ti.reshape(T, LANES)[:, :TOP_K]
        pos, tok_buf, blk_e, n_used = _route(top_i, T)
        xs = _gather_rows(h2.reshape(T, D // 2), tok_buf)
        wg, wl = _split_gate_up(w_gate_up[l])
        out_buf = _moe(xs, blk_e, n_used, wg, wl,
                       b_gate_up[l][:, None, 0::2], b_gate_up[l][:, None, 1::2],
                       w_down[l].astype(bf), b_down[l][:, None, :])
        pos_kmajor = pos.reshape(T, TOP_K).T.reshape(-1)
        yk = _gather_rows(out_buf, pos_kmajor)
        x = _final(x1.reshape(T, D), yk, tg.reshape(T, LANES), mod3, g_post_ffn[l][None],
                   S).reshape(B, S, D)
    return x
```

```python
import functools

import numpy as np
import jax
import jax.numpy as jnp
from jax import lax
from jax.experimental import pallas as pl
from jax.experimental.pallas import tpu as pltpu
from jax.experimental.pallas import tpu_sc as plsc

D_MODEL = 1024
ATT_HEADS = 8
ATT_HEAD_DIM = 64
ATT_WIDTH = ATT_HEADS * ATT_HEAD_DIM
DILATIONS = (1, 4, 16)
ATT_SPAN = 128
ROPE_THETA = 10000.0
GLA_HEADS = 4
GLA_KEY_DIM = 64
GLA_VAL_DIM = 128
GLA_KEY_WIDTH = GLA_HEADS * GLA_KEY_DIM
GLA_VAL_WIDTH = GLA_HEADS * GLA_VAL_DIM
GLA_GATE_RANK = 16
GLA_GATE_NORMALIZER = 16.0
GLA_CHUNK = 64
N_EXPERTS = 32
TOP_K = 4
D_FF = D_MODEL
SWIGLU_LIMIT = 7.0
SWIGLU_ALPHA = 1.702
NORM_EPS = 1e-6
N_MOD = 6

LANES = 128
NEG_BIG = -1e30

TM_PROJ = 512
TQ_ATT = 128
TG_GLA = 512
TM_OUT = 512
TM_MOE = 256
TM_FIN = 256
TN_MOD = 512
TK_SPLIT = 512
SC_INDEX_WINDOW = 128
SC_GATHER_WINDOW = 64
VMEM_LIMIT = 48 * 1024 * 1024

_C_QA, _C_KA, _C_VA = 0, 512, 1024
_C_QG, _C_KG, _C_VG, _C_RG, _C_LR = 1536, 1792, 2048, 2560, 3072
D_IN_PACKED = 3200


def _rms(x, g):
    return x * lax.rsqrt(jnp.mean(x * x, axis=-1, keepdims=True) + NORM_EPS) * g


def _pack_bf16_pairs(x):
    n = x.shape[1] // 2
    u = lax.bitcast_convert_type(x.astype(jnp.bfloat16).astype(jnp.float32), jnp.uint32)
    return (u[:, :n] >> 16) | (u[:, n:] & jnp.uint32(0xFFFF0000))


def _unpack_bf16_pairs(w):
    lo = lax.bitcast_convert_type(w << 16, jnp.float32)
    hi = lax.bitcast_convert_type(w & jnp.uint32(0xFFFF0000), jnp.float32)
    return lo, hi


def _gather_rows(data, idx):
    n_rows = idx.shape[0]
    width = data.shape[1]
    mesh = plsc.VectorSubcoreMesh(core_axis_name="core", subcore_axis_name="subcore")
    n_workers = mesh.num_cores * mesh.num_subcores
    per_worker = n_rows // n_workers
    assert per_worker * n_workers == n_rows and per_worker % SC_INDEX_WINDOW == 0
    halves = SC_INDEX_WINDOW // SC_GATHER_WINDOW

    @pl.kernel(out_type=jax.ShapeDtypeStruct((n_rows, width), data.dtype), mesh=mesh,
               name="gather_rows",
               scratch_types=[pltpu.VMEM((1, SC_INDEX_WINDOW), jnp.int32),
                              pltpu.VMEM((SC_GATHER_WINDOW, width), data.dtype)])
    def gather(x_hbm, i_hbm, o_hbm, idx_vmem, rows_vmem):
        worker = lax.axis_index("core") * mesh.num_subcores + lax.axis_index("subcore")
        base = worker * per_worker

        @pl.loop(0, per_worker // SC_INDEX_WINDOW)
        def _(j):
            off = base + j * SC_INDEX_WINDOW
            pltpu.sync_copy(i_hbm.at[:, pl.ds(off, SC_INDEX_WINDOW)], idx_vmem)
            for h in range(halves):
                part = idx_vmem.at[0, pl.ds(h * SC_GATHER_WINDOW, SC_GATHER_WINDOW)]
                pltpu.sync_copy(x_hbm.at[part], rows_vmem)
                pltpu.sync_copy(rows_vmem,
                                o_hbm.at[pl.ds(off + h * SC_GATHER_WINDOW, SC_GATHER_WINDOW)])

    return gather(data, idx.reshape(1, n_rows))


def _scatter_rows(data, idx, n_out):
    n_copies, n_rows = idx.shape
    width = data.shape[1]
    mesh = plsc.VectorSubcoreMesh(core_axis_name="core", subcore_axis_name="subcore")
    n_workers = mesh.num_cores * mesh.num_subcores
    per_worker = n_rows // n_workers
    assert per_worker * n_workers == n_rows and per_worker % SC_INDEX_WINDOW == 0
    halves = SC_INDEX_WINDOW // SC_GATHER_WINDOW

    @pl.kernel(out_type=jax.ShapeDtypeStruct((n_out, width), data.dtype), mesh=mesh,
               name="scatter_rows",
               scratch_types=[pltpu.VMEM((n_copies, SC_INDEX_WINDOW), jnp.int32),
                              pltpu.VMEM((SC_GATHER_WINDOW, width), data.dtype)])
    def scatter(x_hbm, i_hbm, o_hbm, idx_vmem, rows_vmem):
        worker = lax.axis_index("core") * mesh.num_subcores + lax.axis_index("subcore")
        base = worker * per_worker

        @pl.loop(0, per_worker // SC_INDEX_WINDOW)
        def _(j):
            off = base + j * SC_INDEX_WINDOW
            pltpu.sync_copy(i_hbm.at[:, pl.ds(off, SC_INDEX_WINDOW)], idx_vmem)
            for h in range(halves):
                pltpu.sync_copy(x_hbm.at[pl.ds(off + h * SC_GATHER_WINDOW, SC_GATHER_WINDOW)],
                                rows_vmem)
                for k in range(n_copies):
                    part = idx_vmem.at[k, pl.ds(h * SC_GATHER_WINDOW, SC_GATHER_WINDOW)]
                    pltpu.sync_copy(rows_vmem, o_hbm.at[part])

    return scatter(data, idx)


def _mod_kernel(ct_ref, w_ref, b_ref, o_ref):
    ct = ct_ref[...]
    s = ct * (1.0 / (1.0 + jnp.exp(-ct)))
    w = w_ref[...]
    rows = [jnp.sum(s[:, b:b + 1] * w, axis=0, keepdims=True) for b in range(ct.shape[1])]
    o_ref[...] = jnp.concatenate(rows, axis=0) + b_ref[...]


def _mod(c, w_mod, b_mod):
    B = c.shape[0]
    n = w_mod.shape[1]
    return pl.pallas_call(
        _mod_kernel,
        name="mod",
        grid=(n // TN_MOD,),
        in_specs=[pl.BlockSpec((D_MODEL, B), lambda j: (0, 0)),
                  pl.BlockSpec((D_MODEL, TN_MOD), lambda j: (0, j)),
                  pl.BlockSpec((1, TN_MOD), lambda j: (0, j))],
        out_specs=pl.BlockSpec((B, TN_MOD), lambda j: (0, j)),
        out_shape=jax.ShapeDtypeStruct((B, n), jnp.float32),
        compiler_params=pltpu.CompilerParams(dimension_semantics=("arbitrary",),
                                             vmem_limit_bytes=VMEM_LIMIT),
    )(c.T, w_mod, b_mod.reshape(1, n))


def _store_residue_views(scr_ref, out_refs):
    ns, tm, _ = scr_ref.shape
    w = ns * LANES
    for d, ref in zip(DILATIONS, out_refs):
        for r in range(d):
            for s in range(ns):
                c0 = r * w + s * LANES
                ref[0, :, c0:c0 + LANES] = scr_ref[s, pl.ds(r, tm // d, stride=d), :].astype(ref.dtype)


def _in_proj_kernel(x_ref, mod_ref, g_ref, w_ref, cos_ref, sin_ref, wlr_ref, bg_ref,
                    qa1_ref, qa4_ref, qa16_ref, ka1_ref, ka4_ref, ka16_ref,
                    va1_ref, va4_ref, va16_ref, qg_ref, kg_ref, vg_ref, sr_ref, la_ref,
                    qs_ref, ks_ref, vs_ref):
    x = x_ref[0]
    shift = mod_ref[0, 0:1, :]
    scale = mod_ref[0, 1:2, :]
    h = (_rms(x, g_ref[...]) * (1.0 + scale) + shift).astype(jnp.bfloat16)

    def proj(c0, width):
        return jnp.dot(h, w_ref[:, c0:c0 + width], preferred_element_type=jnp.float32)

    cos = cos_ref[...]
    sin = sin_ref[...]

    def rope(c0, scr_ref, mult):
        p = proj(c0, ATT_WIDTH)
        for s in range(ATT_WIDTH // LANES):
            t = p[:, s * LANES:(s + 1) * LANES]
            r = t * cos + pltpu.roll(t, LANES // 2, axis=1) * sin
            if mult != 1.0:
                r = r * mult
            scr_ref[s] = r

    rope(_C_QA, qs_ref, ATT_HEAD_DIM ** -0.5)
    _store_residue_views(qs_ref, (qa1_ref, qa4_ref, qa16_ref))
    rope(_C_KA, ks_ref, 1.0)
    _store_residue_views(ks_ref, (ka1_ref, ka4_ref, ka16_ref))
    pv = proj(_C_VA, ATT_WIDTH)
    for s in range(ATT_WIDTH // LANES):
        vs_ref[s] = pv[:, s * LANES:(s + 1) * LANES]
    _store_residue_views(vs_ref, (va1_ref, va4_ref, va16_ref))
    qg_ref[0] = proj(_C_QG, GLA_KEY_WIDTH).astype(qg_ref.dtype)
    kg_ref[0] = proj(_C_KG, GLA_KEY_WIDTH).astype(kg_ref.dtype)
    vg_ref[0] = proj(_C_VG, GLA_VAL_WIDTH).astype(vg_ref.dtype)
    r = proj(_C_RG, GLA_VAL_WIDTH)
    sr_ref[0] = (r * (1.0 / (1.0 + jnp.exp(-r)))).astype(sr_ref.dtype)
    lr = proj(_C_LR, LANES).astype(jnp.bfloat16)
    z = jnp.dot(lr, wlr_ref[...], preferred_element_type=jnp.float32) + bg_ref[...]
    log_sig = jnp.minimum(z, 0.0) - jnp.log(1.0 + jnp.exp(-jnp.abs(z)))
    la_ref[0] = log_sig * (1.0 / GLA_GATE_NORMALIZER)


def _in_proj(x, mod3, g_pre, w_packed, cos_t, sin_t, wlr, bg):
    B, S, D = x.shape
    nb = S // TM_PROJ
    tok = lambda w: pl.BlockSpec((1, TM_PROJ, w), lambda b, i: (b, i, 0))
    const = lambda shape: pl.BlockSpec(shape, lambda b, i: tuple(0 for _ in shape))
    bf = jnp.bfloat16
    att_shapes = [jax.ShapeDtypeStruct((B, S // d, d * ATT_WIDTH), bf) for d in DILATIONS] * 3
    att_specs = [pl.BlockSpec((1, TM_PROJ // d, d * ATT_WIDTH), lambda b, i: (b, i, 0))
                 for d in DILATIONS] * 3
    gla_shapes = [jax.ShapeDtypeStruct((B, S, w), dt) for w, dt in
                  [(GLA_KEY_WIDTH, bf), (GLA_KEY_WIDTH, bf),
                   (GLA_VAL_WIDTH, bf), (GLA_VAL_WIDTH, bf), (GLA_KEY_WIDTH, jnp.float32)]]
    return pl.pallas_call(
        _in_proj_kernel,
        name="in_proj",
        grid=(B, nb),
        in_specs=[tok(D),
                  pl.BlockSpec((1, N_MOD, D), lambda b, i: (b, 0, 0)),
                  const((1, D)),
                  const((D, D_IN_PACKED)),
                  pl.BlockSpec((TM_PROJ, LANES), lambda b, i: (i, 0)),
                  pl.BlockSpec((TM_PROJ, LANES), lambda b, i: (i, 0)),
                  const((LANES, GLA_KEY_WIDTH)),
                  const((1, GLA_KEY_WIDTH))],
        out_specs=att_specs + [tok(s.shape[-1]) for s in gla_shapes],
        out_shape=att_shapes + gla_shapes,
        scratch_shapes=[pltpu.VMEM((ATT_WIDTH // LANES, TM_PROJ, LANES), jnp.float32)] * 3,
        compiler_params=pltpu.CompilerParams(dimension_semantics=("arbitrary", "arbitrary"),
                                             vmem_limit_bytes=VMEM_LIMIT),
    )(x, mod3, g_pre, w_packed, cos_t, sin_t, wlr, bg)


def _attn_kernel(q_ref, kp_ref, kc_ref, vp_ref, vc_ref, o_ref, st_ref):
    i = pl.program_id(2)
    tq = q_ref.shape[1]
    row = lax.broadcasted_iota(jnp.int32, (tq, 2 * tq), 0)
    col = lax.broadcasted_iota(jnp.int32, (tq, 2 * tq), 1)
    valid = (col >= row) & (col <= row + ATT_SPAN) & ((col >= tq) | (i > 0))
    bias = jnp.where(valid, 0.0, NEG_BIG)
    lane = lax.broadcasted_iota(jnp.int32, (tq, LANES), 1)
    q_head = (lane >> 5) & 1
    stats = jnp.zeros((tq, LANES), jnp.float32)
    for hp in range(ATT_HEADS // 2):
        sl = slice(hp * LANES, (hp + 1) * LANES)
        q2 = q_ref[0, :, sl]
        kcat = jnp.concatenate([kp_ref[0, :, sl], kc_ref[0, :, sl]], axis=0)
        vcat = jnp.concatenate([vp_ref[0, :, sl], vc_ref[0, :, sl]], axis=0)
        outs = []
        for hh in range(2):
            qm = jnp.where(q_head == hh, q2, jnp.zeros_like(q2))
            s = lax.dot_general(qm, kcat, (((1,), (1,)), ((), ())),
                                preferred_element_type=jnp.float32) + bias
            m = jnp.max(s, axis=1, keepdims=True)
            p = jnp.exp(s - m)
            l = jnp.sum(p, axis=1, keepdims=True)
            o = jnp.dot(p.astype(vcat.dtype), vcat, preferred_element_type=jnp.float32)
            outs.append(o / l)
            stats = jnp.where(lane == 2 * hp + hh, m + jnp.log(l), stats)
        o_ref[0, :, sl] = jnp.where(lane < ATT_HEAD_DIM, outs[0], outs[1]).astype(o_ref.dtype)
    st_ref[0] = stats


def _attn_pattern(qv, kv, vv, dil):
    B, L, _ = qv.shape
    W = ATT_WIDTH
    nq = L // TQ_ATT
    cur = pl.BlockSpec((1, TQ_ATT, W), lambda b, r, i: (b, i, r))
    prev = pl.BlockSpec((1, TQ_ATT, W), lambda b, r, i: (b, jnp.maximum(i - 1, 0), r))
    o, st = pl.pallas_call(
        _attn_kernel,
        name=f"attn_d{dil}",
        grid=(B, dil, nq),
        in_specs=[cur, prev, cur, prev, cur],
        out_specs=[cur, pl.BlockSpec((1, TQ_ATT, LANES), lambda b, r, i: (b, i, r))],
        out_shape=[jax.ShapeDtypeStruct((B, L, dil * W), jnp.bfloat16),
                   jax.ShapeDtypeStruct((B, L, dil * LANES), jnp.float32)],
        compiler_params=pltpu.CompilerParams(
            dimension_semantics=("arbitrary", "arbitrary", "arbitrary"),
            vmem_limit_bytes=VMEM_LIMIT),
    )(qv, kv, kv, vv, vv)
    return o, st


def _cumsum_rows(x):
    n = x.shape[0]
    row = lax.broadcasted_iota(jnp.int32, x.shape, 0)
    s = 1
    while s < n:
        x = x + jnp.where(row >= s, pltpu.roll(x, s, axis=0), 0.0)
        s *= 2
    return x


def _gla_kernel(q_ref, k_ref, v_ref, la_ref, sr_ref, g_ref, o_ref, st_ref):
    @pl.when(pl.program_id(1) == 0)
    def _():
        st_ref[...] = jnp.zeros_like(st_ref)

    C = GLA_CHUNK
    n_chunks = q_ref.shape[1] // C
    lane = lax.broadcasted_iota(jnp.int32, (C, LANES), 1)
    lane_sq = lax.broadcasted_iota(jnp.int32, (LANES, LANES), 1)
    tril = (lax.broadcasted_iota(jnp.int32, (C, C), 0)
            >= lax.broadcasted_iota(jnp.int32, (C, C), 1))
    g = g_ref[...]
    bf = jnp.bfloat16

    def chunk(c, carry):
        r0 = pl.multiple_of(c * C, C)
        rows = pl.ds(r0, C)
        for p in range(GLA_HEADS // 2):
            ksl = slice(p * LANES, (p + 1) * LANES)
            b = _cumsum_rows(la_ref[0, rows, ksl])
            b_last = b[C - 1:C, :]
            q = q_ref[0, rows, ksl].astype(jnp.float32) * (GLA_KEY_DIM ** -0.5)
            k = k_ref[0, rows, ksl].astype(jnp.float32)
            q_dec = q * jnp.exp(b)
            k_inv = (k * jnp.exp(-b)).astype(bf)
            k_dec = (k * jnp.exp(b_last - b)).astype(bf)
            st = st_ref[p]
            st_b = st.astype(bf)
            ut = []
            for hh in range(2):
                h = 2 * p + hh
                vsl = slice(h * GLA_VAL_DIM, (h + 1) * GLA_VAL_DIM)
                v = v_ref[0, rows, vsl]
                own = (lane >= GLA_KEY_DIM) if hh else (lane < GLA_KEY_DIM)
                qm = jnp.where(own, q_dec, 0.0).astype(bf)
                att = lax.dot_general(qm, k_inv, (((1,), (1,)), ((), ())),
                                      preferred_element_type=jnp.float32)
                att = jnp.where(tril, att, 0.0).astype(bf)
                o = jnp.dot(att, v, preferred_element_type=jnp.float32)
                o = o + lax.dot_general(qm, st_b, (((1,), (1,)), ((), ())),
                                        preferred_element_type=jnp.float32)
                o = _rms(o, g) * sr_ref[0, rows, vsl].astype(jnp.float32)
                o_ref[0, rows, vsl] = o.astype(o_ref.dtype)
                ut.append(lax.dot_general(v, k_dec, (((0,), (0,)), ((), ())),
                                          preferred_element_type=jnp.float32))
            st_ref[p] = st * jnp.exp(b_last) + jnp.where(lane_sq < GLA_KEY_DIM, ut[0], ut[1])
        return carry

    lax.fori_loop(0, n_chunks, chunk, 0)


def _gla(qg, kg, vg, la, sr, g_gla):
    B, S, _ = qg.shape
    tok = lambda w: pl.BlockSpec((1, TG_GLA, w), lambda b, i: (b, i, 0))
    return pl.pallas_call(
        _gla_kernel,
        name="gla",
        grid=(B, S // TG_GLA),
        in_specs=[tok(GLA_KEY_WIDTH), tok(GLA_KEY_WIDTH), tok(GLA_VAL_WIDTH),
                  tok(GLA_KEY_WIDTH), tok(GLA_VAL_WIDTH),
                  pl.BlockSpec((1, GLA_VAL_DIM), lambda b, i: (0, 0))],
        out_specs=tok(GLA_VAL_WIDTH),
        out_shape=jax.ShapeDtypeStruct((B, S, GLA_VAL_WIDTH), jnp.bfloat16),
        scratch_shapes=[pltpu.VMEM((GLA_HEADS // 2, GLA_VAL_DIM, LANES), jnp.float32)],
        compiler_params=pltpu.CompilerParams(dimension_semantics=("arbitrary", "arbitrary"),
                                             vmem_limit_bytes=VMEM_LIMIT),
    )(qg, kg, vg, la, sr, g_gla)


def _out_proj_kernel(o1_ref, o2_ref, o3_ref, s1_ref, s2_ref, s3_ref, og_ref, x_ref, mod_ref,
                     wout_ref, gpost_ref, gpre_ref, rw_ref, rb_ref,
                     x1_ref, h2_ref, ti_ref, tg_ref, oscr_ref, sscr_ref):
    tm = x_ref.shape[1]
    lane = lax.broadcasted_iota(jnp.int32, (tm, LANES), 1)
    for j, (d, o_ref, s_ref) in enumerate(((DILATIONS[1], o2_ref, s2_ref),
                                           (DILATIONS[2], o3_ref, s3_ref))):
        for r in range(d):
            rows = pl.ds(r, tm // d, stride=d)
            for s in range(ATT_WIDTH // LANES):
                c0 = r * ATT_WIDTH + s * LANES
                oscr_ref[j, s, rows, :] = o_ref[0, :, c0:c0 + LANES].astype(jnp.float32)
            sscr_ref[j, rows, :] = s_ref[0, :, r * LANES:(r + 1) * LANES]
    lses = [s1_ref[0], sscr_ref[0], sscr_ref[1]]
    m = jnp.maximum(jnp.maximum(lses[0], lses[1]), lses[2])
    es = [jnp.exp(t - m) for t in lses]
    inv = 1.0 / (es[0] + es[1] + es[2])
    ws = [e * inv for e in es]
    pairs = []
    for hp in range(ATT_HEADS // 2):
        sl = slice(hp * LANES, (hp + 1) * LANES)
        o_pats = [o1_ref[0, :, sl].astype(jnp.float32), oscr_ref[0, hp], oscr_ref[1, hp]]
        acc = jnp.zeros((tm, LANES), jnp.float32)
        for w, o in zip(ws, o_pats):
            wp = jnp.where(lane < ATT_HEAD_DIM, w[:, 2 * hp:2 * hp + 1], w[:, 2 * hp + 1:2 * hp + 2])
            acc = acc + wp * o
        pairs.append(acc.astype(jnp.bfloat16))
    mixed = jnp.concatenate(pairs + [og_ref[0]], axis=1)
    y = jnp.dot(mixed, wout_ref[...], preferred_element_type=jnp.float32)
    gate1 = mod_ref[0, 2:3, :]
    shift2 = mod_ref[0, 3:4, :]
    scale2 = mod_ref[0, 4:5, :]
    x1 = x_ref[0] + gate1 * _rms(y, gpost_ref[...])
    x1_ref[0] = x1
    h2 = _rms(x1, gpre_ref[...]) * (1.0 + scale2) + shift2
    h2_ref[0] = _pack_bf16_pairs(h2)
    logits = jnp.dot(h2, rw_ref[...], preferred_element_type=jnp.float32,
                     precision=lax.Precision.HIGHEST) + rb_ref[...]
    lane_f = lane.astype(jnp.float32)
    vals, idxs = [], []
    for _ in range(TOP_K):
        mk = jnp.max(logits, axis=1, keepdims=True)
        ik = jnp.min(jnp.where(logits == mk, lane_f, float(LANES)), axis=1, keepdims=True)
        logits = jnp.where(lane_f == ik, -jnp.inf, logits)
        vals.append(mk)
        idxs.append(ik)
    ex = [jnp.exp(v - vals[0]) for v in vals]
    den = ex[0] + ex[1] + ex[2] + ex[3]
    ti = jnp.zeros((tm, LANES), jnp.float32)
    tg = jnp.zeros((tm, LANES), jnp.float32)
    for kk in range(TOP_K):
        ti = jnp.where(lane == kk, idxs[kk], ti)
        tg = jnp.where(lane == kk, ex[kk] / den, tg)
    ti_ref[0] = ti.astype(jnp.int32)
    tg_ref[0] = tg


def _out_proj(o_pats, st_pats, og, x, mod3, w_out, g_post, g_pre, rw, rb):
    B, S, D = x.shape
    tok = lambda w: pl.BlockSpec((1, TM_OUT, w), lambda b, i: (b, i, 0))
    const = lambda shape: pl.BlockSpec(shape, lambda b, i: tuple(0 for _ in shape))
    out_shapes = [jax.ShapeDtypeStruct((B, S, D), jnp.float32),
                  jax.ShapeDtypeStruct((B, S, D // 2), jnp.uint32),
                  jax.ShapeDtypeStruct((B, S, LANES), jnp.int32),
                  jax.ShapeDtypeStruct((B, S, LANES), jnp.float32)]
    return pl.pallas_call(
        _out_proj_kernel,
        name="out_proj",
        grid=(B, S // TM_OUT),
        in_specs=[pl.BlockSpec((1, TM_OUT // d, d * ATT_WIDTH), lambda b, i: (b, i, 0))
                  for d in DILATIONS]
                 + [pl.BlockSpec((1, TM_OUT // d, d * LANES), lambda b, i: (b, i, 0))
                    for d in DILATIONS]
                 + [tok(GLA_VAL_WIDTH), tok(D),
                  pl.BlockSpec((1, N_MOD, D), lambda b, i: (b, 0, 0)),
                  const((D, D)), const((1, D)), const((1, D)),
                  const((D, LANES)), const((1, LANES))],
        out_specs=[tok(D), tok(D // 2), tok(LANES), tok(LANES)],
        out_shape=out_shapes,
        scratch_shapes=[pltpu.VMEM((2, ATT_WIDTH // LANES, TM_OUT, LANES), jnp.float32),
                        pltpu.VMEM((2, TM_OUT, LANES), jnp.float32)],
        compiler_params=pltpu.CompilerParams(dimension_semantics=("arbitrary", "arbitrary"),
                                             vmem_limit_bytes=VMEM_LIMIT),
    )(*o_pats, *st_pats, og, x, mod3, w_out, g_post, g_pre, rw, rb)


def _split_gate_up_kernel(w_ref, wg_ref, wl_ref):
    group = 2 * LANES
    src = lax.broadcasted_iota(jnp.int32, (group, group), 0)
    dst = lax.broadcasted_iota(jnp.int32, (group, group), 1)
    want = jnp.where(dst < LANES, 2 * dst, 2 * (dst - LANES) + 1)
    perm = jnp.where(src == want, 1.0, 0.0).astype(jnp.bfloat16)
    for j in range(w_ref.shape[1] // group):
        t = jnp.dot(w_ref[:, j * group:(j + 1) * group].astype(jnp.bfloat16), perm,
                    preferred_element_type=jnp.float32)
        wg_ref[:, j * LANES:(j + 1) * LANES] = t[:, :LANES].astype(wg_ref.dtype)
        wl_ref[:, j * LANES:(j + 1) * LANES] = t[:, LANES:].astype(wl_ref.dtype)


def _split_gate_up(w_gate_up):
    E, K, N2 = w_gate_up.shape
    out = jax.ShapeDtypeStruct((E, K, N2 // 2), jnp.bfloat16)
    ospec = pl.BlockSpec((None, TK_SPLIT, N2 // 2), lambda e, i: (e, i, 0))
    return pl.pallas_call(
        _split_gate_up_kernel,
        name="split_gate_up",
        grid=(E, K // TK_SPLIT),
        in_specs=[pl.BlockSpec((None, TK_SPLIT, N2), lambda e, i: (e, i, 0))],
        out_specs=[ospec, ospec],
        out_shape=[out, out],
        compiler_params=pltpu.CompilerParams(dimension_semantics=("arbitrary", "arbitrary"),
                                             vmem_limit_bytes=VMEM_LIMIT),
    )(w_gate_up)


def _moe_kernel(be_ref, nu_ref, x_ref, wg_ref, wl_ref, bg_ref, bl_ref, wd_ref, bd_ref, o_ref):
    i = pl.program_id(0)

    @pl.when(i < nu_ref[0])
    def _():
        lo, hi = _unpack_bf16_pairs(x_ref[...])
        x = jnp.concatenate([lo, hi], axis=1).astype(jnp.bfloat16)
        xg = jnp.dot(x, wg_ref[...], preferred_element_type=jnp.float32) + bg_ref[...]
        xl = jnp.dot(x, wl_ref[...], preferred_element_type=jnp.float32) + bl_ref[...]
        xg = jnp.minimum(xg, SWIGLU_LIMIT)
        xl = jnp.clip(xl, -SWIGLU_LIMIT, SWIGLU_LIMIT)
        act = xg * (1.0 / (1.0 + jnp.exp(-SWIGLU_ALPHA * xg))) * (xl + 1.0)
        out = jnp.dot(act.astype(jnp.bfloat16), wd_ref[...],
                      preferred_element_type=jnp.float32) + bd_ref[...]
        o_ref[...] = _pack_bf16_pairs(out)

    @pl.when(i >= nu_ref[0])
    def _():
        o_ref[...] = jnp.zeros_like(o_ref)


def _moe(xs, blk_e, n_used, wg, wl, bg, bl, wd, bd):
    P, half = xs.shape
    D = 2 * half
    n_blocks = P // TM_MOE
    wspec = lambda k, n: pl.BlockSpec((None, k, n), lambda i, be, nu: (be[i], 0, 0))
    grid_spec = pltpu.PrefetchScalarGridSpec(
        num_scalar_prefetch=2,
        grid=(n_blocks,),
        in_specs=[pl.BlockSpec((TM_MOE, half), lambda i, be, nu: (i, 0)),
                  wspec(D, D_FF), wspec(D, D_FF), wspec(1, D_FF), wspec(1, D_FF),
                  wspec(D_FF, D), wspec(1, D)],
        out_specs=pl.BlockSpec((TM_MOE, half), lambda i, be, nu: (i, 0)),
    )
    return pl.pallas_call(
        _moe_kernel,
        name="moe",
        grid_spec=grid_spec,
        out_shape=jax.ShapeDtypeStruct((P, half), jnp.uint32),
        compiler_params=pltpu.CompilerParams(dimension_semantics=("arbitrary",),
                                             vmem_limit_bytes=VMEM_LIMIT),
    )(blk_e, n_used, xs, wg, wl, bg, bl, wd, bd)


def _final_kernel(x1_ref, y0_ref, y1_ref, y2_ref, y3_ref, tg_ref, mod_ref, g_ref, o_ref):
    gates = tg_ref[...]
    half = y0_ref.shape[1]
    y_lo = jnp.zeros((x1_ref.shape[0], half), jnp.float32)
    y_hi = jnp.zeros((x1_ref.shape[0], half), jnp.float32)
    for kk, yk_ref in enumerate((y0_ref, y1_ref, y2_ref, y3_ref)):
        lo, hi = _unpack_bf16_pairs(yk_ref[...])
        y_lo = y_lo + gates[:, kk:kk + 1] * lo
        y_hi = y_hi + gates[:, kk:kk + 1] * hi
    y = jnp.concatenate([y_lo, y_hi], axis=1)
    gate2 = mod_ref[0, 5:6, :]
    o_ref[...] = x1_ref[...] + gate2 * _rms(y, g_ref[...])


def _final(x1, yk, tg, mod3, g_post, seq_len):
    T, D = x1.shape
    per_batch = seq_len // TM_FIN
    nt = T // TM_FIN
    yspec = lambda kk: pl.BlockSpec((TM_FIN, D // 2), lambda i: (kk * nt + i, 0))
    return pl.pallas_call(
        _final_kernel,
        name="final",
        grid=(nt,),
        in_specs=[pl.BlockSpec((TM_FIN, D), lambda i: (i, 0))]
                 + [yspec(kk) for kk in range(TOP_K)]
                 + [pl.BlockSpec((TM_FIN, LANES), lambda i: (i, 0)),
                  pl.BlockSpec((1, N_MOD, D), lambda i: (i // per_batch, 0, 0)),
                  pl.BlockSpec((1, D), lambda i: (0, 0))],
        out_specs=pl.BlockSpec((TM_FIN, D), lambda i: (i, 0)),
        out_shape=jax.ShapeDtypeStruct((T, D), jnp.float32),
        compiler_params=pltpu.CompilerParams(dimension_semantics=("arbitrary",),
                                             vmem_limit_bytes=VMEM_LIMIT),
    )(x1, yk, yk, yk, yk, tg, mod3, g_post)


def _pack_w_in(w_in):
    n = np.arange(LANES)
    src = np.where(n < 32, n, np.where(n < 64, n + 32, np.where(n < 96, n - 32, n)))
    perm = (np.arange(ATT_WIDTH // LANES)[:, None] * LANES + src[None, :]).reshape(-1)
    lr = jnp.pad(w_in[:, 3072:3072 + GLA_GATE_RANK], ((0, 0), (0, LANES - GLA_GATE_RANK)))
    packed = jnp.concatenate([w_in[:, perm], w_in[:, ATT_WIDTH + perm], w_in[:, 1024:3072], lr],
                             axis=1)
    return packed.astype(jnp.bfloat16)


def _rope_tables(seq_len):
    half = ATT_HEAD_DIM // 2
    inv_freq = ROPE_THETA ** (-jnp.arange(half, dtype=jnp.float32) / half)
    ang = jnp.arange(seq_len, dtype=jnp.float32)[:, None] * inv_freq[None, :]
    cos = jnp.tile(jnp.cos(ang), (1, LANES // half))
    sin = jnp.tile(jnp.sin(ang), (1, LANES // half))
    sign = jnp.where(jnp.arange(LANES) < LANES // 2, -1.0, 1.0)
    return cos, sin * sign


def _route(top_i, n_tokens):
    A = n_tokens * TOP_K
    e_flat = top_i.reshape(A)
    onehot = (e_flat[:, None] == jnp.arange(N_EXPERTS, dtype=jnp.int32)[None, :]).astype(jnp.int32)
    csum = jnp.cumsum(onehot, axis=0)
    rank = jnp.sum(onehot * csum, axis=1) - 1
    counts = csum[-1]
    padded = (counts + TM_MOE - 1) // TM_MOE * TM_MOE
    pend = jnp.cumsum(padded)
    pstart = pend - padded
    pos = pstart[e_flat] + rank
    n_blocks = -(-(A + N_EXPERTS * (TM_MOE - 1)) // TM_MOE)
    blk_start = jnp.arange(n_blocks, dtype=jnp.int32) * TM_MOE
    blk_e = jnp.minimum(jnp.sum((pend[None, :] <= blk_start[:, None]).astype(jnp.int32), axis=1),
                        N_EXPERTS - 1)
    n_used = (pend[-1:] // TM_MOE).astype(jnp.int32)
    pos_kmajor = pos.reshape(n_tokens, TOP_K).T
    return pos_kmajor, n_blocks * TM_MOE, blk_e, n_used


def kernel(x, c, w_mod, b_mod, g_pre_mix, w_in, w_gate_lr, b_gate, g_gla, w_out, g_post_mix,
           g_pre_ffn, router_w, router_b, w_gate_up, b_gate_up, w_down, b_down, g_post_ffn):
    B, S, D = x.shape
    T = B * S
    bf = jnp.bfloat16
    cos_t, sin_t = _rope_tables(S)
    for l in range(w_mod.shape[0]):
        wg, wl = _split_gate_up(w_gate_up[l])
        wd = w_down[l].astype(bf)
        mod3 = _mod(c, w_mod[l], b_mod[l]).reshape(B, N_MOD, D)
        wlr = jnp.pad(w_gate_lr[l], ((0, LANES - GLA_GATE_RANK), (0, 0))).astype(bf)
        proj = _in_proj(x, mod3, g_pre_mix[l][None], _pack_w_in(w_in[l]), cos_t, sin_t, wlr,
                        b_gate[l][None])
        n_pat = len(DILATIONS)
        qa, ka, va = proj[:n_pat], proj[n_pat:2 * n_pat], proj[2 * n_pat:3 * n_pat]
        qg, kg, vg, sr, la = proj[3 * n_pat:]
        pats = [_attn_pattern(qa[j], ka[j], va[j], d) for j, d in enumerate(DILATIONS)]
        og = _gla(qg, kg, vg, la, sr, g_gla[l][None])
        rw = jnp.pad(router_w[l], ((0, 0), (0, LANES - N_EXPERTS)))
        rb = jnp.pad(router_b[l], (0, LANES - N_EXPERTS), constant_values=NEG_BIG)[None]
        x1, h2, ti, tg = _out_proj([p[0] for p in pats], [p[1] for p in pats], og, x, mod3,
                                   w_out[l].astype(bf), g_post_mix[l][None], g_pre_ffn[l][None],
                                   rw, rb)
        top_i = ti.reshape(T, LANES)[:, :TOP_K]
        pos_kmajor, n_slots, blk_e, n_used = _route(top_i, T)
        xs = _scatter_rows(h2.reshape(T, D // 2), pos_kmajor, n_slots)
        out_buf = _moe(xs, blk_e, n_used, wg, wl,
                       b_gate_up[l][:, None, 0::2], b_gate_up[l][:, None, 1::2],
                       wd, b_down[l][:, None, :])
        yk = _gather_rows(out_buf, pos_kmajor.reshape(-1))
        x = _final(x1.reshape(T, D), yk, tg.reshape(T, LANES), mod3, g_post_ffn[l][None],
                   S).reshape(B, S, D)
    return x
```

```python
import functools

import numpy as np
import jax
import jax.numpy as jnp
from jax import lax
from jax.experimental import pallas as pl
from jax.experimental.pallas import tpu as pltpu
from jax.experimental.pallas import tpu_sc as plsc

D_MODEL = 1024
ATT_HEADS = 8
ATT_HEAD_DIM = 64
ATT_WIDTH = ATT_HEADS * ATT_HEAD_DIM
DILATIONS = (1, 4, 16)
ATT_SPAN = 128
ROPE_THETA = 10000.0
GLA_HEADS = 4
GLA_KEY_DIM = 64
GLA_VAL_DIM = 128
GLA_KEY_WIDTH = GLA_HEADS * GLA_KEY_DIM
GLA_VAL_WIDTH = GLA_HEADS * GLA_VAL_DIM
GLA_GATE_RANK = 16
GLA_GATE_NORMALIZER = 16.0
GLA_CHUNK = 64
N_EXPERTS = 32
TOP_K = 4
D_FF = D_MODEL
SWIGLU_LIMIT = 7.0
SWIGLU_ALPHA = 1.702
NORM_EPS = 1e-6
N_MOD = 6

LANES = 128
NEG_BIG = -1e30

TM_PROJ = 512
TQ_ATT = 128
TG_GLA = 512
TM_OUT = 512
TM_MOE = 512
TM_FIN = 256
TN_MOD = 512
TK_SPLIT = 512
SC_INDEX_WINDOW = 128
SC_GATHER_WINDOW = 64
VMEM_LIMIT = 48 * 1024 * 1024

_C_QA, _C_KA, _C_VA = 0, 512, 1024
_C_QG, _C_KG, _C_VG, _C_RG, _C_LR = 1536, 1792, 2048, 2560, 3072
D_IN_PACKED = 3200


def _rms(x, g):
    return x * lax.rsqrt(jnp.mean(x * x, axis=-1, keepdims=True) + NORM_EPS) * g


def _pack_bf16_pairs(x):
    n = x.shape[1] // 2
    u = lax.bitcast_convert_type(x.astype(jnp.bfloat16).astype(jnp.float32), jnp.uint32)
    return (u[:, :n] >> 16) | (u[:, n:] & jnp.uint32(0xFFFF0000))


def _unpack_bf16_pairs(w):
    lo = lax.bitcast_convert_type(w << 16, jnp.float32)
    hi = lax.bitcast_convert_type(w & jnp.uint32(0xFFFF0000), jnp.float32)
    return lo, hi


def _gather_rows(data, idx):
    n_rows = idx.shape[0]
    width = data.shape[1]
    mesh = plsc.VectorSubcoreMesh(core_axis_name="core", subcore_axis_name="subcore")
    n_workers = mesh.num_cores * mesh.num_subcores
    per_worker = n_rows // n_workers
    assert per_worker * n_workers == n_rows and per_worker % SC_INDEX_WINDOW == 0
    halves = SC_INDEX_WINDOW // SC_GATHER_WINDOW

    @pl.kernel(out_type=jax.ShapeDtypeStruct((n_rows, width), data.dtype), mesh=mesh,
               name="gather_rows",
               scratch_types=[pltpu.VMEM((1, SC_INDEX_WINDOW), jnp.int32),
                              pltpu.VMEM((SC_GATHER_WINDOW, width), data.dtype)])
    def gather(x_hbm, i_hbm, o_hbm, idx_vmem, rows_vmem):
        worker = lax.axis_index("core") * mesh.num_subcores + lax.axis_index("subcore")
        base = worker * per_worker

        @pl.loop(0, per_worker // SC_INDEX_WINDOW)
        def _(j):
            off = base + j * SC_INDEX_WINDOW
            pltpu.sync_copy(i_hbm.at[:, pl.ds(off, SC_INDEX_WINDOW)], idx_vmem)
            for h in range(halves):
                part = idx_vmem.at[0, pl.ds(h * SC_GATHER_WINDOW, SC_GATHER_WINDOW)]
                pltpu.sync_copy(x_hbm.at[part], rows_vmem)
                pltpu.sync_copy(rows_vmem,
                                o_hbm.at[pl.ds(off + h * SC_GATHER_WINDOW, SC_GATHER_WINDOW)])

    return gather(data, idx.reshape(1, n_rows))


def _scatter_rows(data, idx, n_out):
    n_copies, n_rows = idx.shape
    width = data.shape[1]
    mesh = plsc.VectorSubcoreMesh(core_axis_name="core", subcore_axis_name="subcore")
    n_workers = mesh.num_cores * mesh.num_subcores
    per_worker = n_rows // n_workers
    assert per_worker * n_workers == n_rows and per_worker % SC_INDEX_WINDOW == 0
    halves = SC_INDEX_WINDOW // SC_GATHER_WINDOW

    @pl.kernel(out_type=jax.ShapeDtypeStruct((n_out, width), data.dtype), mesh=mesh,
               name="scatter_rows",
               scratch_types=[pltpu.VMEM((n_copies, SC_INDEX_WINDOW), jnp.int32),
                              pltpu.VMEM((SC_GATHER_WINDOW, width), data.dtype)])
    def scatter(x_hbm, i_hbm, o_hbm, idx_vmem, rows_vmem):
        worker = lax.axis_index("core") * mesh.num_subcores + lax.axis_index("subcore")
        base = worker * per_worker

        @pl.loop(0, per_worker // SC_INDEX_WINDOW)
        def _(j):
            off = base + j * SC_INDEX_WINDOW
            pltpu.sync_copy(i_hbm.at[:, pl.ds(off, SC_INDEX_WINDOW)], idx_vmem)
            for h in range(halves):
                pltpu.sync_copy(x_hbm.at[pl.ds(off + h * SC_GATHER_WINDOW, SC_GATHER_WINDOW)],
                                rows_vmem)
                for k in range(n_copies):
                    part = idx_vmem.at[k, pl.ds(h * SC_GATHER_WINDOW, SC_GATHER_WINDOW)]
                    pltpu.sync_copy(rows_vmem, o_hbm.at[part])

    return scatter(data, idx)


def _mod_kernel(ct_ref, w_ref, b_ref, o_ref):
    ct = ct_ref[...]
    s = ct * (1.0 / (1.0 + jnp.exp(-ct)))
    w = w_ref[...]
    rows = [jnp.sum(s[:, b:b + 1] * w, axis=0, keepdims=True) for b in range(ct.shape[1])]
    o_ref[...] = jnp.concatenate(rows, axis=0) + b_ref[...]


def _mod(c, w_mod, b_mod):
    B = c.shape[0]
    n = w_mod.shape[1]
    return pl.pallas_call(
        _mod_kernel,
        name="mod",
        grid=(n // TN_MOD,),
        in_specs=[pl.BlockSpec((D_MODEL, B), lambda j: (0, 0)),
                  pl.BlockSpec((D_MODEL, TN_MOD), lambda j: (0, j)),
                  pl.BlockSpec((1, TN_MOD), lambda j: (0, j))],
        out_specs=pl.BlockSpec((B, TN_MOD), lambda j: (0, j)),
        out_shape=jax.ShapeDtypeStruct((B, n), jnp.float32),
        compiler_params=pltpu.CompilerParams(dimension_semantics=("arbitrary",),
                                             vmem_limit_bytes=VMEM_LIMIT),
    )(c.T, w_mod, b_mod.reshape(1, n))


def _store_residue_views(scr_ref, out_refs):
    ns, tm, _ = scr_ref.shape
    w = ns * LANES
    for d, ref in zip(DILATIONS, out_refs):
        for r in range(d):
            for s in range(ns):
                c0 = r * w + s * LANES
                ref[0, :, c0:c0 + LANES] = scr_ref[s, pl.ds(r, tm // d, stride=d), :].astype(ref.dtype)


def _in_proj_kernel(x_ref, mod_ref, g_ref, w_ref, cos_ref, sin_ref, wlr_ref, bg_ref,
                    qa1_ref, qa4_ref, qa16_ref, ka1_ref, ka4_ref, ka16_ref,
                    va1_ref, va4_ref, va16_ref, qg_ref, kg_ref, vg_ref, sr_ref, la_ref,
                    qs_ref, ks_ref, vs_ref):
    x = x_ref[0]
    shift = mod_ref[0, 0:1, :]
    scale = mod_ref[0, 1:2, :]
    h = (_rms(x, g_ref[...]) * (1.0 + scale) + shift).astype(jnp.bfloat16)

    def proj(c0, width):
        return jnp.dot(h, w_ref[:, c0:c0 + width], preferred_element_type=jnp.float32)

    cos = cos_ref[...]
    sin = sin_ref[...]

    def rope(c0, scr_ref, mult):
        p = proj(c0, ATT_WIDTH)
        for s in range(ATT_WIDTH // LANES):
            t = p[:, s * LANES:(s + 1) * LANES]
            r = t * cos + pltpu.roll(t, LANES // 2, axis=1) * sin
            if mult != 1.0:
                r = r * mult
            scr_ref[s] = r

    rope(_C_QA, qs_ref, ATT_HEAD_DIM ** -0.5)
    _store_residue_views(qs_ref, (qa1_ref, qa4_ref, qa16_ref))
    rope(_C_KA, ks_ref, 1.0)
    _store_residue_views(ks_ref, (ka1_ref, ka4_ref, ka16_ref))
    pv = proj(_C_VA, ATT_WIDTH)
    for s in range(ATT_WIDTH // LANES):
        vs_ref[s] = pv[:, s * LANES:(s + 1) * LANES]
    _store_residue_views(vs_ref, (va1_ref, va4_ref, va16_ref))
    qg_ref[0] = proj(_C_QG, GLA_KEY_WIDTH).astype(qg_ref.dtype)
    kg_ref[0] = proj(_C_KG, GLA_KEY_WIDTH).astype(kg_ref.dtype)
    vg_ref[0] = proj(_C_VG, GLA_VAL_WIDTH).astype(vg_ref.dtype)
    r = proj(_C_RG, GLA_VAL_WIDTH)
    sr_ref[0] = (r * (1.0 / (1.0 + jnp.exp(-r)))).astype(sr_ref.dtype)
    lr = proj(_C_LR, LANES).astype(jnp.bfloat16)
    z = jnp.dot(lr, wlr_ref[...], preferred_element_type=jnp.float32) + bg_ref[...]
    log_sig = jnp.minimum(z, 0.0) - jnp.log(1.0 + jnp.exp(-jnp.abs(z)))
    la_ref[0] = log_sig * (1.0 / GLA_GATE_NORMALIZER)


def _in_proj(x, mod3, g_pre, w_packed, cos_t, sin_t, wlr, bg):
    B, S, D = x.shape
    nb = S // TM_PROJ
    tok = lambda w: pl.BlockSpec((1, TM_PROJ, w), lambda b, i: (b, i, 0))
    const = lambda shape: pl.BlockSpec(shape, lambda b, i: tuple(0 for _ in shape))
    bf = jnp.bfloat16
    att_shapes = [jax.ShapeDtypeStruct((B, S // d, d * ATT_WIDTH), bf) for d in DILATIONS] * 3
    att_specs = [pl.BlockSpec((1, TM_PROJ // d, d * ATT_WIDTH), lambda b, i: (b, i, 0))
                 for d in DILATIONS] * 3
    gla_shapes = [jax.ShapeDtypeStruct((B, S, w), dt) for w, dt in
                  [(GLA_KEY_WIDTH, bf), (GLA_KEY_WIDTH, bf),
                   (GLA_VAL_WIDTH, bf), (GLA_VAL_WIDTH, bf), (GLA_KEY_WIDTH, jnp.float32)]]
    return pl.pallas_call(
        _in_proj_kernel,
        name="in_proj",
        grid=(B, nb),
        in_specs=[tok(D),
                  pl.BlockSpec((1, N_MOD, D), lambda b, i: (b, 0, 0)),
                  const((1, D)),
                  const((D, D_IN_PACKED)),
                  pl.BlockSpec((TM_PROJ, LANES), lambda b, i: (i, 0)),
                  pl.BlockSpec((TM_PROJ, LANES), lambda b, i: (i, 0)),
                  const((LANES, GLA_KEY_WIDTH)),
                  const((1, GLA_KEY_WIDTH))],
        out_specs=att_specs + [tok(s.shape[-1]) for s in gla_shapes],
        out_shape=att_shapes + gla_shapes,
        scratch_shapes=[pltpu.VMEM((ATT_WIDTH // LANES, TM_PROJ, LANES), jnp.float32)] * 3,
        compiler_params=pltpu.CompilerParams(dimension_semantics=("arbitrary", "arbitrary"),
                                             vmem_limit_bytes=VMEM_LIMIT),
    )(x, mod3, g_pre, w_packed, cos_t, sin_t, wlr, bg)


def _attn_kernel(q_ref, kp_ref, kc_ref, vp_ref, vc_ref, o_ref, st_ref):
    i = pl.program_id(2)
    tq = q_ref.shape[1]
    row = lax.broadcasted_iota(jnp.int32, (tq, 2 * tq), 0)
    col = lax.broadcasted_iota(jnp.int32, (tq, 2 * tq), 1)
    valid = (col >= row) & (col <= row + ATT_SPAN) & ((col >= tq) | (i > 0))
    bias = jnp.where(valid, 0.0, NEG_BIG)
    lane = lax.broadcasted_iota(jnp.int32, (tq, LANES), 1)
    q_head = (lane >> 5) & 1
    stats = jnp.zeros((tq, LANES), jnp.float32)
    for hp in range(ATT_HEADS // 2):
        sl = slice(hp * LANES, (hp + 1) * LANES)
        q2 = q_ref[0, :, sl]
        kcat = jnp.concatenate([kp_ref[0, :, sl], kc_ref[0, :, sl]], axis=0)
        vcat = jnp.concatenate([vp_ref[0, :, sl], vc_ref[0, :, sl]], axis=0)
        outs = []
        for hh in range(2):
            qm = jnp.where(q_head == hh, q2, jnp.zeros_like(q2))
            s = lax.dot_general(qm, kcat, (((1,), (1,)), ((), ())),
                                preferred_element_type=jnp.float32) + bias
            m = jnp.max(s, axis=1, keepdims=True)
            p = jnp.exp(s - m)
            l = jnp.sum(p, axis=1, keepdims=True)
            o = jnp.dot(p.astype(vcat.dtype), vcat, preferred_element_type=jnp.float32)
            outs.append(o / l)
            stats = jnp.where(lane == 2 * hp + hh, m + jnp.log(l), stats)
        o_ref[0, :, sl] = jnp.where(lane < ATT_HEAD_DIM, outs[0], outs[1]).astype(o_ref.dtype)
    st_ref[0] = stats


def _attn_pattern(qv, kv, vv, dil):
    B, L, _ = qv.shape
    W = ATT_WIDTH
    nq = L // TQ_ATT
    cur = pl.BlockSpec((1, TQ_ATT, W), lambda b, r, i: (b, i, r))
    prev = pl.BlockSpec((1, TQ_ATT, W), lambda b, r, i: (b, jnp.maximum(i - 1, 0), r))
    o, st = pl.pallas_call(
        _attn_kernel,
        name=f"attn_d{dil}",
        grid=(B, dil, nq),
        in_specs=[cur, prev, cur, prev, cur],
        out_specs=[cur, pl.BlockSpec((1, TQ_ATT, LANES), lambda b, r, i: (b, i, r))],
        out_shape=[jax.ShapeDtypeStruct((B, L, dil * W), jnp.bfloat16),
                   jax.ShapeDtypeStruct((B, L, dil * LANES), jnp.float32)],
        compiler_params=pltpu.CompilerParams(
            dimension_semantics=("arbitrary", "arbitrary", "arbitrary"),
            vmem_limit_bytes=VMEM_LIMIT),
    )(qv, kv, kv, vv, vv)
    return o, st


def _cumsum_rows(x):
    n = x.shape[0]
    row = lax.broadcasted_iota(jnp.int32, x.shape, 0)
    s = 1
    while s < n:
        x = x + jnp.where(row >= s, pltpu.roll(x, s, axis=0), 0.0)
        s *= 2
    return x


def _gla_kernel(q_ref, k_ref, v_ref, la_ref, sr_ref, g_ref, o_ref, st_ref):
    @pl.when(pl.program_id(1) == 0)
    def _():
        st_ref[...] = jnp.zeros_like(st_ref)

    C = GLA_CHUNK
    n_chunks = q_ref.shape[1] // C
    lane = lax.broadcasted_iota(jnp.int32, (C, LANES), 1)
    lane_sq = lax.broadcasted_iota(jnp.int32, (LANES, LANES), 1)
    tril = (lax.broadcasted_iota(jnp.int32, (C, C), 0)
            >= lax.broadcasted_iota(jnp.int32, (C, C), 1))
    g = g_ref[...]
    bf = jnp.bfloat16

    def chunk(c, carry):
        r0 = pl.multiple_of(c * C, C)
        rows = pl.ds(r0, C)
        for p in range(GLA_HEADS // 2):
            ksl = slice(p * LANES, (p + 1) * LANES)
            b = _cumsum_rows(la_ref[0, rows, ksl])
            b_last = b[C - 1:C, :]
            q = q_ref[0, rows, ksl].astype(jnp.float32) * (GLA_KEY_DIM ** -0.5)
            k = k_ref[0, rows, ksl].astype(jnp.float32)
            q_dec = q * jnp.exp(b)
            k_inv = (k * jnp.exp(-b)).astype(bf)
            k_dec = (k * jnp.exp(b_last - b)).astype(bf)
            st = st_ref[p]
            st_b = st.astype(bf)
            ut = []
            for hh in range(2):
                h = 2 * p + hh
                vsl = slice(h * GLA_VAL_DIM, (h + 1) * GLA_VAL_DIM)
                v = v_ref[0, rows, vsl]
                own = (lane >= GLA_KEY_DIM) if hh else (lane < GLA_KEY_DIM)
                qm = jnp.where(own, q_dec, 0.0).astype(bf)
                att = lax.dot_general(qm, k_inv, (((1,), (1,)), ((), ())),
                                      preferred_element_type=jnp.float32)
                att = jnp.where(tril, att, 0.0).astype(bf)
                o = jnp.dot(att, v, preferred_element_type=jnp.float32)
                o = o + lax.dot_general(qm, st_b, (((1,), (1,)), ((), ())),
                                        preferred_element_type=jnp.float32)
                o = _rms(o, g) * sr_ref[0, rows, vsl].astype(jnp.float32)
                o_ref[0, rows, vsl] = o.astype(o_ref.dtype)
                ut.append(lax.dot_general(v, k_dec, (((0,), (0,)), ((), ())),
                                          preferred_element_type=jnp.float32))
            st_ref[p] = st * jnp.exp(b_last) + jnp.where(lane_sq < GLA_KEY_DIM, ut[0], ut[1])
        return carry

    lax.fori_loop(0, n_chunks, chunk, 0)


def _gla(qg, kg, vg, la, sr, g_gla):
    B, S, _ = qg.shape
    tok = lambda w: pl.BlockSpec((1, TG_GLA, w), lambda b, i: (b, i, 0))
    return pl.pallas_call(
        _gla_kernel,
        name="gla",
        grid=(B, S // TG_GLA),
        in_specs=[tok(GLA_KEY_WIDTH), tok(GLA_KEY_WIDTH), tok(GLA_VAL_WIDTH),
                  tok(GLA_KEY_WIDTH), tok(GLA_VAL_WIDTH),
                  pl.BlockSpec((1, GLA_VAL_DIM), lambda b, i: (0, 0))],
        out_specs=tok(GLA_VAL_WIDTH),
        out_shape=jax.ShapeDtypeStruct((B, S, GLA_VAL_WIDTH), jnp.bfloat16),
        scratch_shapes=[pltpu.VMEM((GLA_HEADS // 2, GLA_VAL_DIM, LANES), jnp.float32)],
        compiler_params=pltpu.CompilerParams(dimension_semantics=("arbitrary", "arbitrary"),
                                             vmem_limit_bytes=VMEM_LIMIT),
    )(qg, kg, vg, la, sr, g_gla)


def _out_proj_kernel(o1_ref, o2_ref, o3_ref, s1_ref, s2_ref, s3_ref, og_ref, x_ref, mod_ref,
                     wout_ref, gpost_ref, gpre_ref, rw_ref, rb_ref,
                     x1_ref, h2_ref, ti_ref, tg_ref, oscr_ref, sscr_ref):
    tm = x_ref.shape[1]
    lane = lax.broadcasted_iota(jnp.int32, (tm, LANES), 1)
    for j, (d, o_ref, s_ref) in enumerate(((DILATIONS[1], o2_ref, s2_ref),
                                           (DILATIONS[2], o3_ref, s3_ref))):
        for r in range(d):
            rows = pl.ds(r, tm // d, stride=d)
            for s in range(ATT_WIDTH // LANES):
                c0 = r * ATT_WIDTH + s * LANES
                oscr_ref[j, s, rows, :] = o_ref[0, :, c0:c0 + LANES].astype(jnp.float32)
            sscr_ref[j, rows, :] = s_ref[0, :, r * LANES:(r + 1) * LANES]
    lses = [s1_ref[0], sscr_ref[0], sscr_ref[1]]
    m = jnp.maximum(jnp.maximum(lses[0], lses[1]), lses[2])
    es = [jnp.exp(t - m) for t in lses]
    inv = 1.0 / (es[0] + es[1] + es[2])
    ws = [e * inv for e in es]
    pairs = []
    for hp in range(ATT_HEADS // 2):
        sl = slice(hp * LANES, (hp + 1) * LANES)
        o_pats = [o1_ref[0, :, sl].astype(jnp.float32), oscr_ref[0, hp], oscr_ref[1, hp]]
        acc = jnp.zeros((tm, LANES), jnp.float32)
        for w, o in zip(ws, o_pats):
            wp = jnp.where(lane < ATT_HEAD_DIM, w[:, 2 * hp:2 * hp + 1], w[:, 2 * hp + 1:2 * hp + 2])
            acc = acc + wp * o
        pairs.append(acc.astype(jnp.bfloat16))
    mixed = jnp.concatenate(pairs + [og_ref[0]], axis=1)
    y = jnp.dot(mixed, wout_ref[...], preferred_element_type=jnp.float32)
    gate1 = mod_ref[0, 2:3, :]
    shift2 = mod_ref[0, 3:4, :]
    scale2 = mod_ref[0, 4:5, :]
    x1 = x_ref[0] + gate1 * _rms(y, gpost_ref[...])
    x1_ref[0] = x1
    h2 = _rms(x1, gpre_ref[...]) * (1.0 + scale2) + shift2
    h2_ref[0] = _pack_bf16_pairs(h2)
    logits = jnp.dot(h2, rw_ref[...], preferred_element_type=jnp.float32,
                     precision=lax.Precision.HIGHEST) + rb_ref[...]
    lane_f = lane.astype(jnp.float32)
    vals, idxs = [], []
    for _ in range(TOP_K):
        mk = jnp.max(logits, axis=1, keepdims=True)
        ik = jnp.min(jnp.where(logits == mk, lane_f, float(LANES)), axis=1, keepdims=True)
        logits = jnp.where(lane_f == ik, -jnp.inf, logits)
        vals.append(mk)
        idxs.append(ik)
    ex = [jnp.exp(v - vals[0]) for v in vals]
    den = ex[0] + ex[1] + ex[2] + ex[3]
    ti = jnp.zeros((tm, LANES), jnp.float32)
    tg = jnp.zeros((tm, LANES), jnp.float32)
    for kk in range(TOP_K):
        ti = jnp.where(lane == kk, idxs[kk], ti)
        tg = jnp.where(lane == kk, ex[kk] / den, tg)
    ti_ref[0] = ti.astype(jnp.int32)
    tg_ref[0] = tg


def _out_proj(o_pats, st_pats, og, x, mod3, w_out, g_post, g_pre, rw, rb):
    B, S, D = x.shape
    tok = lambda w: pl.BlockSpec((1, TM_OUT, w), lambda b, i: (b, i, 0))
    const = lambda shape: pl.BlockSpec(shape, lambda b, i: tuple(0 for _ in shape))
    out_shapes = [jax.ShapeDtypeStruct((B, S, D), jnp.float32),
                  jax.ShapeDtypeStruct((B, S, D // 2), jnp.uint32),
                  jax.ShapeDtypeStruct((B, S, LANES), jnp.int32),
                  jax.ShapeDtypeStruct((B, S, LANES), jnp.float32)]
    return pl.pallas_call(
        _out_proj_kernel,
        name="out_proj",
        grid=(B, S // TM_OUT),
        in_specs=[pl.BlockSpec((1, TM_OUT // d, d * ATT_WIDTH), lambda b, i: (b, i, 0))
                  for d in DILATIONS]
                 + [pl.BlockSpec((1, TM_OUT // d, d * LANES), lambda b, i: (b, i, 0))
                    for d in DILATIONS]
                 + [tok(GLA_VAL_WIDTH), tok(D),
                  pl.BlockSpec((1, N_MOD, D), lambda b, i: (b, 0, 0)),
                  const((D, D)), const((1, D)), const((1, D)),
                  const((D, LANES)), const((1, LANES))],
        out_specs=[tok(D), tok(D // 2), tok(LANES), tok(LANES)],
        out_shape=out_shapes,
        scratch_shapes=[pltpu.VMEM((2, ATT_WIDTH // LANES, TM_OUT, LANES), jnp.float32),
                        pltpu.VMEM((2, TM_OUT, LANES), jnp.float32)],
        compiler_params=pltpu.CompilerParams(dimension_semantics=("arbitrary", "arbitrary"),
                                             vmem_limit_bytes=VMEM_LIMIT),
    )(*o_pats, *st_pats, og, x, mod3, w_out, g_post, g_pre, rw, rb)


def _split_gate_up_kernel(w_ref, wg_ref, wl_ref):
    group = 2 * LANES
    src = lax.broadcasted_iota(jnp.int32, (group, group), 0)
    dst = lax.broadcasted_iota(jnp.int32, (group, group), 1)
    want = jnp.where(dst < LANES, 2 * dst, 2 * (dst - LANES) + 1)
    perm = jnp.where(src == want, 1.0, 0.0).astype(jnp.bfloat16)
    for j in range(w_ref.shape[1] // group):
        t = jnp.dot(w_ref[:, j * group:(j + 1) * group].astype(jnp.bfloat16), perm,
                    preferred_element_type=jnp.float32)
        wg_ref[:, j * LANES:(j + 1) * LANES] = t[:, :LANES].astype(wg_ref.dtype)
        wl_ref[:, j * LANES:(j + 1) * LANES] = t[:, LANES:].astype(wl_ref.dtype)


def _split_gate_up(w_gate_up):
    E, K, N2 = w_gate_up.shape
    out = jax.ShapeDtypeStruct((E, K, N2 // 2), jnp.bfloat16)
    ospec = pl.BlockSpec((None, TK_SPLIT, N2 // 2), lambda e, i: (e, i, 0))
    return pl.pallas_call(
        _split_gate_up_kernel,
        name="split_gate_up",
        grid=(E, K // TK_SPLIT),
        in_specs=[pl.BlockSpec((None, TK_SPLIT, N2), lambda e, i: (e, i, 0))],
        out_specs=[ospec, ospec],
        out_shape=[out, out],
        compiler_params=pltpu.CompilerParams(dimension_semantics=("arbitrary", "arbitrary"),
                                             vmem_limit_bytes=VMEM_LIMIT),
    )(w_gate_up)


def _moe_kernel(be_ref, nu_ref, x_ref, wg_ref, wl_ref, bg_ref, bl_ref, wd_ref, bd_ref, o_ref):
    i = pl.program_id(0)

    @pl.when(i < nu_ref[0])
    def _():
        lo, hi = _unpack_bf16_pairs(x_ref[...])
        x = jnp.concatenate([lo, hi], axis=1).astype(jnp.bfloat16)
        xg = jnp.dot(x, wg_ref[...], preferred_element_type=jnp.float32) + bg_ref[...]
        xl = jnp.dot(x, wl_ref[...], preferred_element_type=jnp.float32) + bl_ref[...]
        xg = jnp.minimum(xg, SWIGLU_LIMIT)
        xl = jnp.clip(xl, -SWIGLU_LIMIT, SWIGLU_LIMIT)
        act = xg * (1.0 / (1.0 + jnp.exp(-SWIGLU_ALPHA * xg))) * (xl + 1.0)
        out = jnp.dot(act.astype(jnp.bfloat16), wd_ref[...],
                      preferred_element_type=jnp.float32) + bd_ref[...]
        o_ref[...] = _pack_bf16_pairs(out)

    @pl.when(i >= nu_ref[0])
    def _():
        o_ref[...] = jnp.zeros_like(o_ref)


def _moe(xs, blk_e, n_used, wg, wl, bg, bl, wd, bd):
    P, half = xs.shape
    D = 2 * half
    n_blocks = P // TM_MOE
    wspec = lambda k, n: pl.BlockSpec((None, k, n), lambda i, be, nu: (be[i], 0, 0))
    grid_spec = pltpu.PrefetchScalarGridSpec(
        num_scalar_prefetch=2,
        grid=(n_blocks,),
        in_specs=[pl.BlockSpec((TM_MOE, half), lambda i, be, nu: (i, 0)),
                  wspec(D, D_FF), wspec(D, D_FF), wspec(1, D_FF), wspec(1, D_FF),
                  wspec(D_FF, D), wspec(1, D)],
        out_specs=pl.BlockSpec((TM_MOE, half), lambda i, be, nu: (i, 0)),
    )
    return pl.pallas_call(
        _moe_kernel,
        name="moe",
        grid_spec=grid_spec,
        out_shape=jax.ShapeDtypeStruct((P, half), jnp.uint32),
        compiler_params=pltpu.CompilerParams(dimension_semantics=("arbitrary",),
                                             vmem_limit_bytes=VMEM_LIMIT),
    )(blk_e, n_used, xs, wg, wl, bg, bl, wd, bd)


def _final_kernel(x1_ref, y0_ref, y1_ref, y2_ref, y3_ref, tg_ref, mod_ref, g_ref, o_ref):
    gates = tg_ref[...]
    half = y0_ref.shape[1]
    y_lo = jnp.zeros((x1_ref.shape[0], half), jnp.float32)
    y_hi = jnp.zeros((x1_ref.shape[0], half), jnp.float32)
    for kk, yk_ref in enumerate((y0_ref, y1_ref, y2_ref, y3_ref)):
        lo, hi = _unpack_bf16_pairs(yk_ref[...])
        y_lo = y_lo + gates[:, kk:kk + 1] * lo
        y_hi = y_hi + gates[:, kk:kk + 1] * hi
    y = jnp.concatenate([y_lo, y_hi], axis=1)
    gate2 = mod_ref[0, 5:6, :]
    o_ref[...] = x1_ref[...] + gate2 * _rms(y, g_ref[...])


def _final(x1, yk, tg, mod3, g_post, seq_len):
    T, D = x1.shape
    per_batch = seq_len // TM_FIN
    nt = T // TM_FIN
    yspec = lambda kk: pl.BlockSpec((TM_FIN, D // 2), lambda i: (kk * nt + i, 0))
    return pl.pallas_call(
        _final_kernel,
        name="final",
        grid=(nt,),
        in_specs=[pl.BlockSpec((TM_FIN, D), lambda i: (i, 0))]
                 + [yspec(kk) for kk in range(TOP_K)]
                 + [pl.BlockSpec((TM_FIN, LANES), lambda i: (i, 0)),
                  pl.BlockSpec((1, N_MOD, D), lambda i: (i // per_batch, 0, 0)),
                  pl.BlockSpec((1, D), lambda i: (0, 0))],
        out_specs=pl.BlockSpec((TM_FIN, D), lambda i: (i, 0)),
        out_shape=jax.ShapeDtypeStruct((T, D), jnp.float32),
        compiler_params=pltpu.CompilerParams(dimension_semantics=("arbitrary",),
                                             vmem_limit_bytes=VMEM_LIMIT),
    )(x1, yk, yk, yk, yk, tg, mod3, g_post)


def _pack_w_in(w_in):
    n = np.arange(LANES)
    src = np.where(n < 32, n, np.where(n < 64, n + 32, np.where(n < 96, n - 32, n)))
    perm = (np.arange(ATT_WIDTH // LANES)[:, None] * LANES + src[None, :]).reshape(-1)
    lr = jnp.pad(w_in[:, 3072:3072 + GLA_GATE_RANK], ((0, 0), (0, LANES - GLA_GATE_RANK)))
    packed = jnp.concatenate([w_in[:, perm], w_in[:, ATT_WIDTH + perm], w_in[:, 1024:3072], lr],
                             axis=1)
    return packed.astype(jnp.bfloat16)


def _rope_tables(seq_len):
    half = ATT_HEAD_DIM // 2
    inv_freq = ROPE_THETA ** (-jnp.arange(half, dtype=jnp.float32) / half)
    ang = jnp.arange(seq_len, dtype=jnp.float32)[:, None] * inv_freq[None, :]
    cos = jnp.tile(jnp.cos(ang), (1, LANES // half))
    sin = jnp.tile(jnp.sin(ang), (1, LANES // half))
    sign = jnp.where(jnp.arange(LANES) < LANES // 2, -1.0, 1.0)
    return cos, sin * sign


def _route(top_i, n_tokens):
    A = n_tokens * TOP_K
    e_flat = top_i.reshape(A)
    onehot = (e_flat[:, None] == jnp.arange(N_EXPERTS, dtype=jnp.int32)[None, :]).astype(jnp.int32)
    csum = jnp.cumsum(onehot, axis=0)
    rank = jnp.sum(onehot * csum, axis=1) - 1
    counts = csum[-1]
    padded = (counts + TM_MOE - 1) // TM_MOE * TM_MOE
    pend = jnp.cumsum(padded)
    pstart = pend - padded
    pos = pstart[e_flat] + rank
    n_blocks = -(-(A + N_EXPERTS * (TM_MOE - 1)) // TM_MOE)
    blk_start = jnp.arange(n_blocks, dtype=jnp.int32) * TM_MOE
    blk_e = jnp.minimum(jnp.sum((pend[None, :] <= blk_start[:, None]).astype(jnp.int32), axis=1),
                        N_EXPERTS - 1)
    n_used = (pend[-1:] // TM_MOE).astype(jnp.int32)
    pos_kmajor = pos.reshape(n_tokens, TOP_K).T
    return pos_kmajor, n_blocks * TM_MOE, blk_e, n_used


def kernel(x, c, w_mod, b_mod, g_pre_mix, w_in, w_gate_lr, b_gate, g_gla, w_out, g_post_mix,
           g_pre_ffn, router_w, router_b, w_gate_up, b_gate_up, w_down, b_down, g_post_ffn):
    B, S, D = x.shape
    T = B * S
    bf = jnp.bfloat16
    cos_t, sin_t = _rope_tables(S)
    for l in range(w_mod.shape[0]):
        wg, wl = _split_gate_up(w_gate_up[l])
        wd = w_down[l].astype(bf)
        mod3 = _mod(c, w_mod[l], b_mod[l]).reshape(B, N_MOD, D)
        wlr = jnp.pad(w_gate_lr[l], ((0, LANES - GLA_GATE_RANK), (0, 0))).astype(bf)
        proj = _in_proj(x, mod3, g_pre_mix[l][None], _pack_w_in(w_in[l]), cos_t, sin_t, wlr,
                        b_gate[l][None])
        n_pat = len(DILATIONS)
        qa, ka, va = proj[:n_pat], proj[n_pat:2 * n_pat], proj[2 * n_pat:3 * n_pat]
        qg, kg, vg, sr, la = proj[3 * n_pat:]
        pats = [_attn_pattern(qa[j], ka[j], va[j], d) for j, d in enumerate(DILATIONS)]
        og = _gla(qg, kg, vg, la, sr, g_gla[l][None])
        rw = jnp.pad(router_w[l], ((0, 0), (0, LANES - N_EXPERTS)))
        rb = jnp.pad(router_b[l], (0, LANES - N_EXPERTS), constant_values=NEG_BIG)[None]
        x1, h2, ti, tg = _out_proj([p[0] for p in pats], [p[1] for p in pats], og, x, mod3,
                                   w_out[l].astype(bf), g_post_mix[l][None], g_pre_ffn[l][None],
                                   rw, rb)
        top_i = ti.reshape(T, LANES)[:, :TOP_K]
        pos_kmajor, n_slots, blk_e, n_used = _route(top_i, T)
        xs = _scatter_rows(h2.reshape(T, D // 2), pos_kmajor, n_slots)
        out_buf = _moe(xs, blk_e, n_used, wg, wl,
                       b_gate_up[l][:, None, 0::2], b_gate_up[l][:, None, 1::2],
                       wd, b_down[l][:, None, :])
        yk = _gather_rows(out_buf, pos_kmajor.reshape(-1))
        x = _final(x1.reshape(T, D), yk, tg.reshape(T, LANES), mod3, g_post_ffn[l][None],
                   S).reshape(B, S, D)
    return x
```

```python
import functools

import numpy as np
import jax
import jax.numpy as jnp
from jax import lax
from jax.experimental import pallas as pl
from jax.experimental.pallas import tpu as pltpu
from jax.experimental.pallas import tpu_sc as plsc

D_MODEL = 1024
ATT_HEADS = 8
ATT_HEAD_DIM = 64
ATT_WIDTH = ATT_HEADS * ATT_HEAD_DIM
DILATIONS = (1, 4, 16)
ATT_SPAN = 128
ROPE_THETA = 10000.0
GLA_HEADS = 4
GLA_KEY_DIM = 64
GLA_VAL_DIM = 128
GLA_KEY_WIDTH = GLA_HEADS * GLA_KEY_DIM
GLA_VAL_WIDTH = GLA_HEADS * GLA_VAL_DIM
GLA_GATE_RANK = 16
GLA_GATE_NORMALIZER = 16.0
GLA_CHUNK = 64
N_EXPERTS = 32
TOP_K = 4
D_FF = D_MODEL
SWIGLU_LIMIT = 7.0
SWIGLU_ALPHA = 1.702
NORM_EPS = 1e-6
N_MOD = 6

LANES = 128
NEG_BIG = -1e30

TM_PROJ = 512
TQ_ATT = 128
TG_GLA = 512
TM_OUT = 512
TM_MOE = 512
TM_FIN = 256
TN_MOD = 512
TK_SPLIT = 512
SC_INDEX_WINDOW = 128
SC_GATHER_WINDOW = 64
VMEM_LIMIT = 48 * 1024 * 1024

_C_QA, _C_KA, _C_VA = 0, 512, 1024
_C_QG, _C_KG, _C_VG, _C_RG, _C_LR = 1536, 1792, 2048, 2560, 3072
D_IN_PACKED = 3200


def _rms(x, g):
    return x * lax.rsqrt(jnp.mean(x * x, axis=-1, keepdims=True) + NORM_EPS) * g


def _pack_bf16_pairs(x):
    n = x.shape[1] // 2
    u = lax.bitcast_convert_type(x.astype(jnp.bfloat16).astype(jnp.float32), jnp.uint32)
    return (u[:, :n] >> 16) | (u[:, n:] & jnp.uint32(0xFFFF0000))


def _unpack_bf16_pairs(w):
    lo = lax.bitcast_convert_type(w << 16, jnp.float32)
    hi = lax.bitcast_convert_type(w & jnp.uint32(0xFFFF0000), jnp.float32)
    return lo, hi


def _gather_rows(data, idx):
    n_rows = idx.shape[0]
    width = data.shape[1]
    mesh = plsc.VectorSubcoreMesh(core_axis_name="core", subcore_axis_name="subcore")
    n_workers = mesh.num_cores * mesh.num_subcores
    per_worker = n_rows // n_workers
    assert per_worker * n_workers == n_rows and per_worker % SC_INDEX_WINDOW == 0
    halves = SC_INDEX_WINDOW // SC_GATHER_WINDOW

    @pl.kernel(out_type=jax.ShapeDtypeStruct((n_rows, width), data.dtype), mesh=mesh,
               name="gather_rows",
               scratch_types=[pltpu.VMEM((1, SC_INDEX_WINDOW), jnp.int32),
                              pltpu.VMEM((SC_GATHER_WINDOW, width), data.dtype)])
    def gather(x_hbm, i_hbm, o_hbm, idx_vmem, rows_vmem):
        worker = lax.axis_index("core") * mesh.num_subcores + lax.axis_index("subcore")
        base = worker * per_worker

        @pl.loop(0, per_worker // SC_INDEX_WINDOW)
        def _(j):
            off = base + j * SC_INDEX_WINDOW
            pltpu.sync_copy(i_hbm.at[:, pl.ds(off, SC_INDEX_WINDOW)], idx_vmem)
            for h in range(halves):
                part = idx_vmem.at[0, pl.ds(h * SC_GATHER_WINDOW, SC_GATHER_WINDOW)]
                pltpu.sync_copy(x_hbm.at[part], rows_vmem)
                pltpu.sync_copy(rows_vmem,
                                o_hbm.at[pl.ds(off + h * SC_GATHER_WINDOW, SC_GATHER_WINDOW)])

    return gather(data, idx.reshape(1, n_rows))


def _scatter_rows(data, idx, n_out):
    n_copies, n_rows = idx.shape
    width = data.shape[1]
    mesh = plsc.VectorSubcoreMesh(core_axis_name="core", subcore_axis_name="subcore")
    n_workers = mesh.num_cores * mesh.num_subcores
    per_worker = n_rows // n_workers
    assert per_worker * n_workers == n_rows and per_worker % SC_INDEX_WINDOW == 0
    halves = SC_INDEX_WINDOW // SC_GATHER_WINDOW

    @pl.kernel(out_type=jax.ShapeDtypeStruct((n_out, width), data.dtype), mesh=mesh,
               name="scatter_rows",
               scratch_types=[pltpu.VMEM((n_copies, SC_INDEX_WINDOW), jnp.int32),
                              pltpu.VMEM((SC_GATHER_WINDOW, width), data.dtype)])
    def scatter(x_hbm, i_hbm, o_hbm, idx_vmem, rows_vmem):
        worker = lax.axis_index("core") * mesh.num_subcores + lax.axis_index("subcore")
        base = worker * per_worker

        @pl.loop(0, per_worker // SC_INDEX_WINDOW)
        def _(j):
            off = base + j * SC_INDEX_WINDOW
            pltpu.sync_copy(i_hbm.at[:, pl.ds(off, SC_INDEX_WINDOW)], idx_vmem)
            for h in range(halves):
                pltpu.sync_copy(x_hbm.at[pl.ds(off + h * SC_GATHER_WINDOW, SC_GATHER_WINDOW)],
                                rows_vmem)
                for k in range(n_copies):
                    part = idx_vmem.at[k, pl.ds(h * SC_GATHER_WINDOW, SC_GATHER_WINDOW)]
                    pltpu.sync_copy(rows_vmem, o_hbm.at[part])

    return scatter(data, idx)


def _mod_kernel(ct_ref, w_ref, b_ref, o_ref):
    ct = ct_ref[...]
    s = ct * (1.0 / (1.0 + jnp.exp(-ct)))
    w = w_ref[...]
    rows = [jnp.sum(s[:, b:b + 1] * w, axis=0, keepdims=True) for b in range(ct.shape[1])]
    o_ref[...] = jnp.concatenate(rows, axis=0) + b_ref[...]


def _mod(c, w_mod, b_mod):
    B = c.shape[0]
    n = w_mod.shape[1]
    return pl.pallas_call(
        _mod_kernel,
        name="mod",
        grid=(n // TN_MOD,),
        in_specs=[pl.BlockSpec((D_MODEL, B), lambda j: (0, 0)),
                  pl.BlockSpec((D_MODEL, TN_MOD), lambda j: (0, j)),
                  pl.BlockSpec((1, TN_MOD), lambda j: (0, j))],
        out_specs=pl.BlockSpec((B, TN_MOD), lambda j: (0, j)),
        out_shape=jax.ShapeDtypeStruct((B, n), jnp.float32),
        compiler_params=pltpu.CompilerParams(dimension_semantics=("arbitrary",),
                                             vmem_limit_bytes=VMEM_LIMIT),
    )(c.T, w_mod, b_mod.reshape(1, n))


def _store_residue_views(scr_ref, out_refs):
    ns, tm, _ = scr_ref.shape
    w = ns * LANES
    for d, ref in zip(DILATIONS, out_refs):
        for r in range(d):
            for s in range(ns):
                c0 = r * w + s * LANES
                ref[0, :, c0:c0 + LANES] = scr_ref[s, pl.ds(r, tm // d, stride=d), :].astype(ref.dtype)


def _in_proj_kernel(x_ref, mod_ref, g_ref, w_ref, cos_ref, sin_ref, wlr_ref, bg_ref,
                    qa1_ref, qa4_ref, qa16_ref, ka1_ref, ka4_ref, ka16_ref,
                    va1_ref, va4_ref, va16_ref, qg_ref, kg_ref, vg_ref, sr_ref, la_ref,
                    qs_ref, ks_ref, vs_ref):
    x = x_ref[0]
    shift = mod_ref[0, 0:1, :]
    scale = mod_ref[0, 1:2, :]
    h = (_rms(x, g_ref[...]) * (1.0 + scale) + shift).astype(jnp.bfloat16)

    def proj(c0, width):
        return jnp.dot(h, w_ref[:, c0:c0 + width], preferred_element_type=jnp.float32)

    cos = cos_ref[...]
    sin = sin_ref[...]

    def rope(c0, scr_ref, mult):
        p = proj(c0, ATT_WIDTH)
        for s in range(ATT_WIDTH // LANES):
            t = p[:, s * LANES:(s + 1) * LANES]
            r = t * cos + pltpu.roll(t, LANES // 2, axis=1) * sin
            if mult != 1.0:
                r = r * mult
            scr_ref[s] = r

    rope(_C_QA, qs_ref, ATT_HEAD_DIM ** -0.5)
    _store_residue_views(qs_ref, (qa1_ref, qa4_ref, qa16_ref))
    rope(_C_KA, ks_ref, 1.0)
    _store_residue_views(ks_ref, (ka1_ref, ka4_ref, ka16_ref))
    pv = proj(_C_VA, ATT_WIDTH)
    for s in range(ATT_WIDTH // LANES):
        vs_ref[s] = pv[:, s * LANES:(s + 1) * LANES]
    _store_residue_views(vs_ref, (va1_ref, va4_ref, va16_ref))
    qg_ref[0] = proj(_C_QG, GLA_KEY_WIDTH).astype(qg_ref.dtype)
    kg_ref[0] = proj(_C_KG, GLA_KEY_WIDTH).astype(kg_ref.dtype)
    vg_ref[0] = proj(_C_VG, GLA_VAL_WIDTH).astype(vg_ref.dtype)
    r = proj(_C_RG, GLA_VAL_WIDTH)
    sr_ref[0] = (r * (1.0 / (1.0 + jnp.exp(-r)))).astype(sr_ref.dtype)
    lr = proj(_C_LR, LANES).astype(jnp.bfloat16)
    z = jnp.dot(lr, wlr_ref[...], preferred_element_type=jnp.float32) + bg_ref[...]
    log_sig = jnp.minimum(z, 0.0) - jnp.log(1.0 + jnp.exp(-jnp.abs(z)))
    la_ref[0] = log_sig * (1.0 / GLA_GATE_NORMALIZER)


def _in_proj(x, mod3, g_pre, w_packed, cos_t, sin_t, wlr, bg):
    B, S, D = x.shape
    nb = S // TM_PROJ
    tok = lambda w: pl.BlockSpec((1, TM_PROJ, w), lambda b, i: (b, i, 0))
    const = lambda shape: pl.BlockSpec(shape, lambda b, i: tuple(0 for _ in shape))
    bf = jnp.bfloat16
    att_shapes = [jax.ShapeDtypeStruct((B, S // d, d * ATT_WIDTH), bf) for d in DILATIONS] * 3
    att_specs = [pl.BlockSpec((1, TM_PROJ // d, d * ATT_WIDTH), lambda b, i: (b, i, 0))
                 for d in DILATIONS] * 3
    gla_shapes = [jax.ShapeDtypeStruct((B, S, w), dt) for w, dt in
                  [(GLA_KEY_WIDTH, bf), (GLA_KEY_WIDTH, bf),
                   (GLA_VAL_WIDTH, bf), (GLA_VAL_WIDTH, bf), (GLA_KEY_WIDTH, jnp.float32)]]
    return pl.pallas_call(
        _in_proj_kernel,
        name="in_proj",
        grid=(B, nb),
        in_specs=[tok(D),
                  pl.BlockSpec((1, N_MOD, D), lambda b, i: (b, 0, 0)),
                  const((1, D)),
                  const((D, D_IN_PACKED)),
                  pl.BlockSpec((TM_PROJ, LANES), lambda b, i: (i, 0)),
                  pl.BlockSpec((TM_PROJ, LANES), lambda b, i: (i, 0)),
                  const((LANES, GLA_KEY_WIDTH)),
                  const((1, GLA_KEY_WIDTH))],
        out_specs=att_specs + [tok(s.shape[-1]) for s in gla_shapes],
        out_shape=att_shapes + gla_shapes,
        scratch_shapes=[pltpu.VMEM((ATT_WIDTH // LANES, TM_PROJ, LANES), jnp.float32)] * 3,
        compiler_params=pltpu.CompilerParams(dimension_semantics=("arbitrary", "arbitrary"),
                                             vmem_limit_bytes=VMEM_LIMIT),
    )(x, mod3, g_pre, w_packed, cos_t, sin_t, wlr, bg)


def _attn_kernel(q_ref, kp_ref, kc_ref, vp_ref, vc_ref, o_ref, st_ref):
    i = pl.program_id(2)
    tq = q_ref.shape[1]
    tp = kp_ref.shape[1]
    row = lax.broadcasted_iota(jnp.int32, (tq, tp + tq), 0)
    col = lax.broadcasted_iota(jnp.int32, (tq, tp + tq), 1)
    valid = (col >= row) & (col <= row + ATT_SPAN) & ((col >= tp) | (i > 0))
    bias = jnp.where(valid, 0.0, NEG_BIG)
    lane = lax.broadcasted_iota(jnp.int32, (tq, LANES), 1)
    q_head = (lane >> 5) & 1
    stats = jnp.zeros((tq, LANES), jnp.float32)
    for hp in range(ATT_HEADS // 2):
        sl = slice(hp * LANES, (hp + 1) * LANES)
        q2 = q_ref[0, :, sl]
        kcat = jnp.concatenate([kp_ref[0, :, sl], kc_ref[0, :, sl]], axis=0)
        vcat = jnp.concatenate([vp_ref[0, :, sl], vc_ref[0, :, sl]], axis=0)
        outs = []
        for hh in range(2):
            qm = jnp.where(q_head == hh, q2, jnp.zeros_like(q2))
            s = lax.dot_general(qm, kcat, (((1,), (1,)), ((), ())),
                                preferred_element_type=jnp.float32) + bias
            m = jnp.max(s, axis=1, keepdims=True)
            p = jnp.exp(s - m)
            l = jnp.sum(p, axis=1, keepdims=True)
            o = jnp.dot(p.astype(vcat.dtype), vcat, preferred_element_type=jnp.float32)
            outs.append(o / l)
            stats = jnp.where(lane == 2 * hp + hh, m + jnp.log(l), stats)
        o_ref[0, :, sl] = jnp.where(lane < ATT_HEAD_DIM, outs[0], outs[1]).astype(o_ref.dtype)
    st_ref[0] = stats


def _attn_pattern(qv, kv, vv, dil):
    B, L, _ = qv.shape
    W = ATT_WIDTH
    nq = L // TQ_ATT
    cur = pl.BlockSpec((1, TQ_ATT, W), lambda b, r, i: (b, i, r))
    back = TQ_ATT // ATT_SPAN
    prev = pl.BlockSpec((1, ATT_SPAN, W), lambda b, r, i: (b, jnp.maximum(i * back - 1, 0), r))
    o, st = pl.pallas_call(
        _attn_kernel,
        name=f"attn_d{dil}",
        grid=(B, dil, nq),
        in_specs=[cur, prev, cur, prev, cur],
        out_specs=[cur, pl.BlockSpec((1, TQ_ATT, LANES), lambda b, r, i: (b, i, r))],
        out_shape=[jax.ShapeDtypeStruct((B, L, dil * W), jnp.bfloat16),
                   jax.ShapeDtypeStruct((B, L, dil * LANES), jnp.float32)],
        compiler_params=pltpu.CompilerParams(
            dimension_semantics=("arbitrary", "arbitrary", "arbitrary"),
            vmem_limit_bytes=VMEM_LIMIT),
    )(qv, kv, kv, vv, vv)
    return o, st


def _cumsum_rows(x):
    n = x.shape[0]
    row = lax.broadcasted_iota(jnp.int32, x.shape, 0)
    s = 1
    while s < n:
        x = x + jnp.where(row >= s, pltpu.roll(x, s, axis=0), 0.0)
        s *= 2
    return x


def _gla_kernel(q_ref, k_ref, v_ref, la_ref, sr_ref, g_ref, o_ref, st_ref):
    @pl.when(pl.program_id(1) == 0)
    def _():
        st_ref[...] = jnp.zeros_like(st_ref)

    C = GLA_CHUNK
    n_chunks = q_ref.shape[1] // C
    lane = lax.broadcasted_iota(jnp.int32, (C, LANES), 1)
    lane_sq = lax.broadcasted_iota(jnp.int32, (LANES, LANES), 1)
    tril = (lax.broadcasted_iota(jnp.int32, (C, C), 0)
            >= lax.broadcasted_iota(jnp.int32, (C, C), 1))
    g = g_ref[...]
    bf = jnp.bfloat16

    def chunk(c, carry):
        r0 = pl.multiple_of(c * C, C)
        rows = pl.ds(r0, C)
        for p in range(GLA_HEADS // 2):
            ksl = slice(p * LANES, (p + 1) * LANES)
            b = _cumsum_rows(la_ref[0, rows, ksl])
            b_last = b[C - 1:C, :]
            q = q_ref[0, rows, ksl].astype(jnp.float32) * (GLA_KEY_DIM ** -0.5)
            k = k_ref[0, rows, ksl].astype(jnp.float32)
            q_dec = q * jnp.exp(b)
            k_inv = (k * jnp.exp(-b)).astype(bf)
            k_dec = (k * jnp.exp(b_last - b)).astype(bf)
            st = st_ref[p]
            st_b = st.astype(bf)
            ut = []
            for hh in range(2):
                h = 2 * p + hh
                vsl = slice(h * GLA_VAL_DIM, (h + 1) * GLA_VAL_DIM)
                v = v_ref[0, rows, vsl]
                own = (lane >= GLA_KEY_DIM) if hh else (lane < GLA_KEY_DIM)
                qm = jnp.where(own, q_dec, 0.0).astype(bf)
                att = lax.dot_general(qm, k_inv, (((1,), (1,)), ((), ())),
                                      preferred_element_type=jnp.float32)
                att = jnp.where(tril, att, 0.0).astype(bf)
                o = jnp.dot(att, v, preferred_element_type=jnp.float32)
                o = o + lax.dot_general(qm, st_b, (((1,), (1,)), ((), ())),
                                        preferred_element_type=jnp.float32)
                o = _rms(o, g) * sr_ref[0, rows, vsl].astype(jnp.float32)
                o_ref[0, rows, vsl] = o.astype(o_ref.dtype)
                ut.append(lax.dot_general(v, k_dec, (((0,), (0,)), ((), ())),
                                          preferred_element_type=jnp.float32))
            st_ref[p] = st * jnp.exp(b_last) + jnp.where(lane_sq < GLA_KEY_DIM, ut[0], ut[1])
        return carry

    lax.fori_loop(0, n_chunks, chunk, 0, unroll=8)


def _gla(qg, kg, vg, la, sr, g_gla):
    B, S, _ = qg.shape
    tok = lambda w: pl.BlockSpec((1, TG_GLA, w), lambda b, i: (b, i, 0))
    return pl.pallas_call(
        _gla_kernel,
        name="gla",
        grid=(B, S // TG_GLA),
        in_specs=[tok(GLA_KEY_WIDTH), tok(GLA_KEY_WIDTH), tok(GLA_VAL_WIDTH),
                  tok(GLA_KEY_WIDTH), tok(GLA_VAL_WIDTH),
                  pl.BlockSpec((1, GLA_VAL_DIM), lambda b, i: (0, 0))],
        out_specs=tok(GLA_VAL_WIDTH),
        out_shape=jax.ShapeDtypeStruct((B, S, GLA_VAL_WIDTH), jnp.bfloat16),
        scratch_shapes=[pltpu.VMEM((GLA_HEADS // 2, GLA_VAL_DIM, LANES), jnp.float32)],
        compiler_params=pltpu.CompilerParams(dimension_semantics=("arbitrary", "arbitrary"),
                                             vmem_limit_bytes=VMEM_LIMIT),
    )(qg, kg, vg, la, sr, g_gla)


def _out_proj_kernel(o1_ref, o2_ref, o3_ref, s1_ref, s2_ref, s3_ref, og_ref, x_ref, mod_ref,
                     wout_ref, gpost_ref, gpre_ref, rw_ref, rb_ref,
                     x1_ref, h2_ref, ti_ref, tg_ref, oscr_ref, sscr_ref):
    tm = x_ref.shape[1]
    lane = lax.broadcasted_iota(jnp.int32, (tm, LANES), 1)
    for j, (d, o_ref, s_ref) in enumerate(((DILATIONS[1], o2_ref, s2_ref),
                                           (DILATIONS[2], o3_ref, s3_ref))):
        for r in range(d):
            rows = pl.ds(r, tm // d, stride=d)
            for s in range(ATT_WIDTH // LANES):
                c0 = r * ATT_WIDTH + s * LANES
                oscr_ref[j, s, rows, :] = o_ref[0, :, c0:c0 + LANES].astype(jnp.float32)
            sscr_ref[j, rows, :] = s_ref[0, :, r * LANES:(r + 1) * LANES]
    lses = [s1_ref[0], sscr_ref[0], sscr_ref[1]]
    m = jnp.maximum(jnp.maximum(lses[0], lses[1]), lses[2])
    es = [jnp.exp(t - m) for t in lses]
    inv = 1.0 / (es[0] + es[1] + es[2])
    ws = [e * inv for e in es]
    pairs = []
    for hp in range(ATT_HEADS // 2):
        sl = slice(hp * LANES, (hp + 1) * LANES)
        o_pats = [o1_ref[0, :, sl].astype(jnp.float32), oscr_ref[0, hp], oscr_ref[1, hp]]
        acc = jnp.zeros((tm, LANES), jnp.float32)
        head_of_lane = 2 * hp + (lane >> 6)
        for w, o in zip(ws, o_pats):
            acc = acc + jnp.take_along_axis(w, head_of_lane, axis=1) * o
        pairs.append(acc.astype(jnp.bfloat16))
    mixed = jnp.concatenate(pairs + [og_ref[0]], axis=1)
    y = jnp.dot(mixed, wout_ref[...], preferred_element_type=jnp.float32)
    gate1 = mod_ref[0, 2:3, :]
    shift2 = mod_ref[0, 3:4, :]
    scale2 = mod_ref[0, 4:5, :]
    x1 = x_ref[0] + gate1 * _rms(y, gpost_ref[...])
    x1_ref[0] = x1
    h2 = _rms(x1, gpre_ref[...]) * (1.0 + scale2) + shift2
    h2_ref[0] = _pack_bf16_pairs(h2)
    bf = jnp.bfloat16
    h_hi = h2.astype(bf)
    h_lo = (h2 - h_hi.astype(jnp.float32)).astype(bf)
    rw = rw_ref[...]
    w_hi = rw.astype(bf)
    w_lo = (rw - w_hi.astype(jnp.float32)).astype(bf)
    logits = (jnp.dot(h_hi, w_hi, preferred_element_type=jnp.float32)
              + jnp.dot(h_lo, w_hi, preferred_element_type=jnp.float32)
              + jnp.dot(h_hi, w_lo, preferred_element_type=jnp.float32)) + rb_ref[...]
    lane_f = lane.astype(jnp.float32)
    vals, idxs = [], []
    for _ in range(TOP_K):
        mk = jnp.max(logits, axis=1, keepdims=True)
        ik = jnp.min(jnp.where(logits == mk, lane_f, float(LANES)), axis=1, keepdims=True)
        logits = jnp.where(lane_f == ik, -jnp.inf, logits)
        vals.append(mk)
        idxs.append(ik)
    ex = [jnp.exp(v - vals[0]) for v in vals]
    den = ex[0] + ex[1] + ex[2] + ex[3]
    ti = jnp.zeros((tm, LANES), jnp.float32)
    tg = jnp.zeros((tm, LANES), jnp.float32)
    for kk in range(TOP_K):
        ti = jnp.where(lane == kk, idxs[kk], ti)
        tg = jnp.where(lane == kk, ex[kk] / den, tg)
    ti_ref[0] = ti.astype(jnp.int32)
    tg_ref[0] = tg


def _out_proj(o_pats, st_pats, og, x, mod3, w_out, g_post, g_pre, rw, rb):
    B, S, D = x.shape
    tok = lambda w: pl.BlockSpec((1, TM_OUT, w), lambda b, i: (b, i, 0))
    const = lambda shape: pl.BlockSpec(shape, lambda b, i: tuple(0 for _ in shape))
    out_shapes = [jax.ShapeDtypeStruct((B, S, D), jnp.float32),
                  jax.ShapeDtypeStruct((B, S, D // 2), jnp.uint32),
                  jax.ShapeDtypeStruct((B, S, LANES), jnp.int32),
                  jax.ShapeDtypeStruct((B, S, LANES), jnp.float32)]
    return pl.pallas_call(
        _out_proj_kernel,
        name="out_proj",
        grid=(B, S // TM_OUT),
        in_specs=[pl.BlockSpec((1, TM_OUT // d, d * ATT_WIDTH), lambda b, i: (b, i, 0))
                  for d in DILATIONS]
                 + [pl.BlockSpec((1, TM_OUT // d, d * LANES), lambda b, i: (b, i, 0))
                    for d in DILATIONS]
                 + [tok(GLA_VAL_WIDTH), tok(D),
                  pl.BlockSpec((1, N_MOD, D), lambda b, i: (b, 0, 0)),
                  const((D, D)), const((1, D)), const((1, D)),
                  const((D, LANES)), const((1, LANES))],
        out_specs=[tok(D), tok(D // 2), tok(LANES), tok(LANES)],
        out_shape=out_shapes,
        scratch_shapes=[pltpu.VMEM((2, ATT_WIDTH // LANES, TM_OUT, LANES), jnp.float32),
                        pltpu.VMEM((2, TM_OUT, LANES), jnp.float32)],
        compiler_params=pltpu.CompilerParams(dimension_semantics=("arbitrary", "arbitrary"),
                                             vmem_limit_bytes=VMEM_LIMIT),
    )(*o_pats, *st_pats, og, x, mod3, w_out, g_post, g_pre, rw, rb)


def _split_gate_up_kernel(w_ref, wg_ref, wl_ref):
    group = 2 * LANES
    src = lax.broadcasted_iota(jnp.int32, (group, group), 0)
    dst = lax.broadcasted_iota(jnp.int32, (group, group), 1)
    want = jnp.where(dst < LANES, 2 * dst, 2 * (dst - LANES) + 1)
    perm = jnp.where(src == want, 1.0, 0.0).astype(jnp.bfloat16)
    for j in range(w_ref.shape[1] // group):
        t = jnp.dot(w_ref[:, j * group:(j + 1) * group].astype(jnp.bfloat16), perm,
                    preferred_element_type=jnp.float32)
        wg_ref[:, j * LANES:(j + 1) * LANES] = t[:, :LANES].astype(wg_ref.dtype)
        wl_ref[:, j * LANES:(j + 1) * LANES] = t[:, LANES:].astype(wl_ref.dtype)


def _split_gate_up(w_gate_up):
    E, K, N2 = w_gate_up.shape
    out = jax.ShapeDtypeStruct((E, K, N2 // 2), jnp.bfloat16)
    ospec = pl.BlockSpec((None, TK_SPLIT, N2 // 2), lambda e, i: (e, i, 0))
    return pl.pallas_call(
        _split_gate_up_kernel,
        name="split_gate_up",
        grid=(E, K // TK_SPLIT),
        in_specs=[pl.BlockSpec((None, TK_SPLIT, N2), lambda e, i: (e, i, 0))],
        out_specs=[ospec, ospec],
        out_shape=[out, out],
        compiler_params=pltpu.CompilerParams(dimension_semantics=("arbitrary", "arbitrary"),
                                             vmem_limit_bytes=VMEM_LIMIT),
    )(w_gate_up)


def _moe_kernel(be_ref, nu_ref, x_ref, wg_ref, wl_ref, bg_ref, bl_ref, wd_ref, bd_ref, o_ref):
    i = pl.program_id(0)

    @pl.when(i < nu_ref[0])
    def _():
        lo, hi = _unpack_bf16_pairs(x_ref[...])
        x = jnp.concatenate([lo, hi], axis=1).astype(jnp.bfloat16)
        xg = jnp.dot(x, wg_ref[...], preferred_element_type=jnp.float32) + bg_ref[...]
        xl = jnp.dot(x, wl_ref[...], preferred_element_type=jnp.float32) + bl_ref[...]
        xg = jnp.minimum(xg, SWIGLU_LIMIT)
        xl = jnp.clip(xl, -SWIGLU_LIMIT, SWIGLU_LIMIT)
        act = xg * (1.0 / (1.0 + jnp.exp(-SWIGLU_ALPHA * xg))) * (xl + 1.0)
        out = jnp.dot(act.astype(jnp.bfloat16), wd_ref[...],
                      preferred_element_type=jnp.float32) + bd_ref[...]
        o_ref[...] = _pack_bf16_pairs(out)

    @pl.when(i >= nu_ref[0])
    def _():
        o_ref[...] = jnp.zeros_like(o_ref)


def _moe(xs, blk_e, n_used, wg, wl, bg, bl, wd, bd):
    P, half = xs.shape
    D = 2 * half
    n_blocks = P // TM_MOE
    wspec = lambda k, n: pl.BlockSpec((None, k, n), lambda i, be, nu: (be[i], 0, 0))
    grid_spec = pltpu.PrefetchScalarGridSpec(
        num_scalar_prefetch=2,
        grid=(n_blocks,),
        in_specs=[pl.BlockSpec((TM_MOE, half), lambda i, be, nu: (i, 0)),
                  wspec(D, D_FF), wspec(D, D_FF), wspec(1, D_FF), wspec(1, D_FF),
                  wspec(D_FF, D), wspec(1, D)],
        out_specs=pl.BlockSpec((TM_MOE, half), lambda i, be, nu: (i, 0)),
    )
    return pl.pallas_call(
        _moe_kernel,
        name="moe",
        grid_spec=grid_spec,
        out_shape=jax.ShapeDtypeStruct((P, half), jnp.uint32),
        compiler_params=pltpu.CompilerParams(dimension_semantics=("arbitrary",),
                                             vmem_limit_bytes=VMEM_LIMIT),
    )(blk_e, n_used, xs, wg, wl, bg, bl, wd, bd)


def _final_kernel(x1_ref, y0_ref, y1_ref, y2_ref, y3_ref, tg_ref, mod_ref, g_ref, o_ref):
    gates = tg_ref[...]
    half = y0_ref.shape[1]
    y_lo = jnp.zeros((x1_ref.shape[0], half), jnp.float32)
    y_hi = jnp.zeros((x1_ref.shape[0], half), jnp.float32)
    for kk, yk_ref in enumerate((y0_ref, y1_ref, y2_ref, y3_ref)):
        lo, hi = _unpack_bf16_pairs(yk_ref[...])
        y_lo = y_lo + gates[:, kk:kk + 1] * lo
        y_hi = y_hi + gates[:, kk:kk + 1] * hi
    y = jnp.concatenate([y_lo, y_hi], axis=1)
    gate2 = mod_ref[0, 5:6, :]
    o_ref[...] = x1_ref[...] + gate2 * _rms(y, g_ref[...])


def _final(x1, yk, tg, mod3, g_post, seq_len):
    T, D = x1.shape
    per_batch = seq_len // TM_FIN
    nt = T // TM_FIN
    yspec = lambda kk: pl.BlockSpec((TM_FIN, D // 2), lambda i: (kk * nt + i, 0))
    return pl.pallas_call(
        _final_kernel,
        name="final",
        grid=(nt,),
        in_specs=[pl.BlockSpec((TM_FIN, D), lambda i: (i, 0))]
                 + [yspec(kk) for kk in range(TOP_K)]
                 + [pl.BlockSpec((TM_FIN, LANES), lambda i: (i, 0)),
                  pl.BlockSpec((1, N_MOD, D), lambda i: (i // per_batch, 0, 0)),
                  pl.BlockSpec((1, D), lambda i: (0, 0))],
        out_specs=pl.BlockSpec((TM_FIN, D), lambda i: (i, 0)),
        out_shape=jax.ShapeDtypeStruct((T, D), jnp.float32),
        compiler_params=pltpu.CompilerParams(dimension_semantics=("arbitrary",),
                                             vmem_limit_bytes=VMEM_LIMIT),
    )(x1, yk, yk, yk, yk, tg, mod3, g_post)


def _pack_w_in(w_in):
    n = np.arange(LANES)
    src = np.where(n < 32, n, np.where(n < 64, n + 32, np.where(n < 96, n - 32, n)))
    perm = (np.arange(ATT_WIDTH // LANES)[:, None] * LANES + src[None, :]).reshape(-1)
    lr = jnp.pad(w_in[:, 3072:3072 + GLA_GATE_RANK], ((0, 0), (0, LANES - GLA_GATE_RANK)))
    packed = jnp.concatenate([w_in[:, perm], w_in[:, ATT_WIDTH + perm], w_in[:, 1024:3072], lr],
                             axis=1)
    return packed.astype(jnp.bfloat16)


def _rope_tables(seq_len):
    half = ATT_HEAD_DIM // 2
    inv_freq = ROPE_THETA ** (-jnp.arange(half, dtype=jnp.float32) / half)
    ang = jnp.arange(seq_len, dtype=jnp.float32)[:, None] * inv_freq[None, :]
    cos = jnp.tile(jnp.cos(ang), (1, LANES // half))
    sin = jnp.tile(jnp.sin(ang), (1, LANES // half))
    sign = jnp.where(jnp.arange(LANES) < LANES // 2, -1.0, 1.0)
    return cos, sin * sign


def _route(top_i, n_tokens):
    A = n_tokens * TOP_K
    e_flat = top_i.reshape(A)
    onehot = (e_flat[:, None] == jnp.arange(N_EXPERTS, dtype=jnp.int32)[None, :]).astype(jnp.int32)
    csum = jnp.cumsum(onehot, axis=0)
    rank = jnp.sum(onehot * csum, axis=1) - 1
    counts = csum[-1]
    padded = (counts + TM_MOE - 1) // TM_MOE * TM_MOE
    pend = jnp.cumsum(padded)
    pstart = pend - padded
    pos = pstart[e_flat] + rank
    n_blocks = -(-(A + N_EXPERTS * (TM_MOE - 1)) // TM_MOE)
    blk_start = jnp.arange(n_blocks, dtype=jnp.int32) * TM_MOE
    blk_e = jnp.minimum(jnp.sum((pend[None, :] <= blk_start[:, None]).astype(jnp.int32), axis=1),
                        N_EXPERTS - 1)
    n_used = (pend[-1:] // TM_MOE).astype(jnp.int32)
    pos_kmajor = pos.reshape(n_tokens, TOP_K).T
    return pos_kmajor, n_blocks * TM_MOE, blk_e, n_used


def kernel(x, c, w_mod, b_mod, g_pre_mix, w_in, w_gate_lr, b_gate, g_gla, w_out, g_post_mix,
           g_pre_ffn, router_w, router_b, w_gate_up, b_gate_up, w_down, b_down, g_post_ffn):
    B, S, D = x.shape
    T = B * S
    bf = jnp.bfloat16
    cos_t, sin_t = _rope_tables(S)
    for l in range(w_mod.shape[0]):
        wg, wl = _split_gate_up(w_gate_up[l])
        wd = w_down[l].astype(bf)
        mod3 = _mod(c, w_mod[l], b_mod[l]).reshape(B, N_MOD, D)
        wlr = jnp.pad(w_gate_lr[l], ((0, LANES - GLA_GATE_RANK), (0, 0))).astype(bf)
        proj = _in_proj(x, mod3, g_pre_mix[l][None], _pack_w_in(w_in[l]), cos_t, sin_t, wlr,
                        b_gate[l][None])
        n_pat = len(DILATIONS)
        qa, ka, va = proj[:n_pat], proj[n_pat:2 * n_pat], proj[2 * n_pat:3 * n_pat]
        qg, kg, vg, sr, la = proj[3 * n_pat:]
        pats = [_attn_pattern(qa[j], ka[j], va[j], d) for j, d in enumerate(DILATIONS)]
        og = _gla(qg, kg, vg, la, sr, g_gla[l][None])
        rw = jnp.pad(router_w[l], ((0, 0), (0, LANES - N_EXPERTS)))
        rb = jnp.pad(router_b[l], (0, LANES - N_EXPERTS), constant_values=NEG_BIG)[None]
        x1, h2, ti, tg = _out_proj([p[0] for p in pats], [p[1] for p in pats], og, x, mod3,
                                   w_out[l].astype(bf), g_post_mix[l][None], g_pre_ffn[l][None],
                                   rw, rb)
        top_i = ti.reshape(T, LANES)[:, :TOP_K]
        pos_kmajor, n_slots, blk_e, n_used = _route(top_i, T)
        xs = _scatter_rows(h2.reshape(T, D // 2), pos_kmajor, n_slots)
        out_buf = _moe(xs, blk_e, n_used, wg, wl,
                       b_gate_up[l][:, None, 0::2], b_gate_up[l][:, None, 1::2],
                       wd, b_down[l][:, None, :])
        yk = _gather_rows(out_buf, pos_kmajor.reshape(-1))
        x = _final(x1.reshape(T, D), yk, tg.reshape(T, LANES), mod3, g_post_ffn[l][None],
                   S).reshape(B, S, D)
    return x
```

```python
import functools

import numpy as np
import jax
import jax.numpy as jnp
from jax import lax
from jax.experimental import pallas as pl
from jax.experimental.pallas import tpu as pltpu
from jax.experimental.pallas import tpu_sc as plsc

D_MODEL = 1024
ATT_HEADS = 8
ATT_HEAD_DIM = 64
ATT_WIDTH = ATT_HEADS * ATT_HEAD_DIM
DILATIONS = (1, 4, 16)
ATT_SPAN = 128
ROPE_THETA = 10000.0
GLA_HEADS = 4
GLA_KEY_DIM = 64
GLA_VAL_DIM = 128
GLA_KEY_WIDTH = GLA_HEADS * GLA_KEY_DIM
GLA_VAL_WIDTH = GLA_HEADS * GLA_VAL_DIM
GLA_GATE_RANK = 16
GLA_GATE_NORMALIZER = 16.0
GLA_CHUNK = 64
N_EXPERTS = 32
TOP_K = 4
D_FF = D_MODEL
SWIGLU_LIMIT = 7.0
SWIGLU_ALPHA = 1.702
NORM_EPS = 1e-6
N_MOD = 6

LANES = 128
NEG_BIG = -1e30

TM_PROJ = 512
TQ_ATT = 128
TG_GLA = 512
TM_OUT = 512
TM_MOE = 512
TM_FIN = 256
TN_MOD = 512
TK_SPLIT = 512
SC_INDEX_WINDOW = 128
SC_GATHER_WINDOW = 64
VMEM_LIMIT = 48 * 1024 * 1024

_C_QA, _C_KA, _C_VA = 0, 512, 1024
_C_QG, _C_KG, _C_VG, _C_RG, _C_LR = 1536, 1792, 2048, 2560, 3072
D_IN_PACKED = 3200


def _rms(x, g):
    return x * lax.rsqrt(jnp.mean(x * x, axis=-1, keepdims=True) + NORM_EPS) * g


def _pack_bf16_pairs(x):
    n = x.shape[1] // 2
    u = lax.bitcast_convert_type(x.astype(jnp.bfloat16).astype(jnp.float32), jnp.uint32)
    return (u[:, :n] >> 16) | (u[:, n:] & jnp.uint32(0xFFFF0000))


def _unpack_bf16_pairs(w):
    lo = lax.bitcast_convert_type(w << 16, jnp.float32)
    hi = lax.bitcast_convert_type(w & jnp.uint32(0xFFFF0000), jnp.float32)
    return lo, hi


def _gather_rows(data, idx):
    n_rows = idx.shape[0]
    width = data.shape[1]
    mesh = plsc.VectorSubcoreMesh(core_axis_name="core", subcore_axis_name="subcore")
    n_workers = mesh.num_cores * mesh.num_subcores
    per_worker = n_rows // n_workers
    assert per_worker * n_workers == n_rows and per_worker % SC_INDEX_WINDOW == 0
    halves = SC_INDEX_WINDOW // SC_GATHER_WINDOW

    @pl.kernel(out_type=jax.ShapeDtypeStruct((n_rows, width), data.dtype), mesh=mesh,
               name="gather_rows",
               scratch_types=[pltpu.VMEM((1, SC_INDEX_WINDOW), jnp.int32),
                              pltpu.VMEM((SC_GATHER_WINDOW, width), data.dtype)])
    def gather(x_hbm, i_hbm, o_hbm, idx_vmem, rows_vmem):
        worker = lax.axis_index("core") * mesh.num_subcores + lax.axis_index("subcore")
        base = worker * per_worker

        @pl.loop(0, per_worker // SC_INDEX_WINDOW)
        def _(j):
            off = base + j * SC_INDEX_WINDOW
            pltpu.sync_copy(i_hbm.at[:, pl.ds(off, SC_INDEX_WINDOW)], idx_vmem)
            for h in range(halves):
                part = idx_vmem.at[0, pl.ds(h * SC_GATHER_WINDOW, SC_GATHER_WINDOW)]
                pltpu.sync_copy(x_hbm.at[part], rows_vmem)
                pltpu.sync_copy(rows_vmem,
                                o_hbm.at[pl.ds(off + h * SC_GATHER_WINDOW, SC_GATHER_WINDOW)])

    return gather(data, idx.reshape(1, n_rows))


def _scatter_rows(data, idx, n_out):
    n_copies, n_rows = idx.shape
    width = data.shape[1]
    mesh = plsc.VectorSubcoreMesh(core_axis_name="core", subcore_axis_name="subcore")
    n_workers = mesh.num_cores * mesh.num_subcores
    per_worker = n_rows // n_workers
    assert per_worker * n_workers == n_rows and per_worker % SC_INDEX_WINDOW == 0
    halves = SC_INDEX_WINDOW // SC_GATHER_WINDOW

    @pl.kernel(out_type=jax.ShapeDtypeStruct((n_out, width), data.dtype), mesh=mesh,
               name="scatter_rows",
               scratch_types=[pltpu.VMEM((n_copies, SC_INDEX_WINDOW), jnp.int32),
                              pltpu.VMEM((SC_GATHER_WINDOW, width), data.dtype)])
    def scatter(x_hbm, i_hbm, o_hbm, idx_vmem, rows_vmem):
        worker = lax.axis_index("core") * mesh.num_subcores + lax.axis_index("subcore")
        base = worker * per_worker

        @pl.loop(0, per_worker // SC_INDEX_WINDOW)
        def _(j):
            off = base + j * SC_INDEX_WINDOW
            pltpu.sync_copy(i_hbm.at[:, pl.ds(off, SC_INDEX_WINDOW)], idx_vmem)
            for h in range(halves):
                pltpu.sync_copy(x_hbm.at[pl.ds(off + h * SC_GATHER_WINDOW, SC_GATHER_WINDOW)],
                                rows_vmem)
                for k in range(n_copies):
                    part = idx_vmem.at[k, pl.ds(h * SC_GATHER_WINDOW, SC_GATHER_WINDOW)]
                    pltpu.sync_copy(rows_vmem, o_hbm.at[part])

    return scatter(data, idx)


def _mod_kernel(ct_ref, w_ref, b_ref, o_ref):
    ct = ct_ref[...]
    s = ct * (1.0 / (1.0 + jnp.exp(-ct)))
    w = w_ref[...]
    rows = [jnp.sum(s[:, b:b + 1] * w, axis=0, keepdims=True) for b in range(ct.shape[1])]
    o_ref[...] = jnp.concatenate(rows, axis=0) + b_ref[...]


def _mod(c, w_mod, b_mod):
    B = c.shape[0]
    n = w_mod.shape[1]
    return pl.pallas_call(
        _mod_kernel,
        name="mod",
        grid=(n // TN_MOD,),
        in_specs=[pl.BlockSpec((D_MODEL, B), lambda j: (0, 0)),
                  pl.BlockSpec((D_MODEL, TN_MOD), lambda j: (0, j)),
                  pl.BlockSpec((1, TN_MOD), lambda j: (0, j))],
        out_specs=pl.BlockSpec((B, TN_MOD), lambda j: (0, j)),
        out_shape=jax.ShapeDtypeStruct((B, n), jnp.float32),
        compiler_params=pltpu.CompilerParams(dimension_semantics=("arbitrary",),
                                             vmem_limit_bytes=VMEM_LIMIT),
    )(c.T, w_mod, b_mod.reshape(1, n))


def _store_residue_views(scr_ref, out_refs):
    ns, tm, _ = scr_ref.shape
    w = ns * LANES
    for d, ref in zip(DILATIONS, out_refs):
        for r in range(d):
            for s in range(ns):
                c0 = r * w + s * LANES
                ref[0, :, c0:c0 + LANES] = scr_ref[s, pl.ds(r, tm // d, stride=d), :].astype(ref.dtype)


def _in_proj_kernel(x_ref, mod_ref, g_ref, w_ref, cos_ref, sin_ref, wlr_ref, bg_ref,
                    qa1_ref, qa4_ref, qa16_ref, ka1_ref, ka4_ref, ka16_ref,
                    va1_ref, va4_ref, va16_ref, qg_ref, kg_ref, vg_ref, sr_ref, la_ref,
                    qs_ref, ks_ref, vs_ref):
    x = x_ref[0]
    shift = mod_ref[0, 0:1, :]
    scale = mod_ref[0, 1:2, :]
    h = (_rms(x, g_ref[...]) * (1.0 + scale) + shift).astype(jnp.bfloat16)

    def proj(c0, width):
        return jnp.dot(h, w_ref[:, c0:c0 + width], preferred_element_type=jnp.float32)

    cos = cos_ref[...]
    sin = sin_ref[...]

    def rope(c0, scr_ref, mult):
        p = proj(c0, ATT_WIDTH)
        for s in range(ATT_WIDTH // LANES):
            t = p[:, s * LANES:(s + 1) * LANES]
            r = t * cos + pltpu.roll(t, LANES // 2, axis=1) * sin
            if mult != 1.0:
                r = r * mult
            scr_ref[s] = r

    rope(_C_QA, qs_ref, ATT_HEAD_DIM ** -0.5)
    _store_residue_views(qs_ref, (qa1_ref, qa4_ref, qa16_ref))
    rope(_C_KA, ks_ref, 1.0)
    _store_residue_views(ks_ref, (ka1_ref, ka4_ref, ka16_ref))
    pv = proj(_C_VA, ATT_WIDTH)
    for s in range(ATT_WIDTH // LANES):
        vs_ref[s] = pv[:, s * LANES:(s + 1) * LANES]
    _store_residue_views(vs_ref, (va1_ref, va4_ref, va16_ref))
    qg_ref[0] = proj(_C_QG, GLA_KEY_WIDTH).astype(qg_ref.dtype)
    kg_ref[0] = proj(_C_KG, GLA_KEY_WIDTH).astype(kg_ref.dtype)
    vg_ref[0] = proj(_C_VG, GLA_VAL_WIDTH).astype(vg_ref.dtype)
    r = proj(_C_RG, GLA_VAL_WIDTH)
    sr_ref[0] = (r * (1.0 / (1.0 + jnp.exp(-r)))).astype(sr_ref.dtype)
    lr = proj(_C_LR, LANES).astype(jnp.bfloat16)
    z = jnp.dot(lr, wlr_ref[...], preferred_element_type=jnp.float32) + bg_ref[...]
    log_sig = jnp.minimum(z, 0.0) - jnp.log(1.0 + jnp.exp(-jnp.abs(z)))
    la_ref[0] = log_sig * (1.0 / GLA_GATE_NORMALIZER)


def _in_proj(x, mod3, g_pre, w_packed, cos_t, sin_t, wlr, bg):
    B, S, D = x.shape
    nb = S // TM_PROJ
    tok = lambda w: pl.BlockSpec((1, TM_PROJ, w), lambda b, i: (b, i, 0))
    const = lambda shape: pl.BlockSpec(shape, lambda b, i: tuple(0 for _ in shape))
    bf = jnp.bfloat16
    att_shapes = [jax.ShapeDtypeStruct((B, S // d, d * ATT_WIDTH), bf) for d in DILATIONS] * 3
    att_specs = [pl.BlockSpec((1, TM_PROJ // d, d * ATT_WIDTH), lambda b, i: (b, i, 0))
                 for d in DILATIONS] * 3
    gla_shapes = [jax.ShapeDtypeStruct((B, S, w), dt) for w, dt in
                  [(GLA_KEY_WIDTH, bf), (GLA_KEY_WIDTH, bf),
                   (GLA_VAL_WIDTH, bf), (GLA_VAL_WIDTH, bf), (GLA_KEY_WIDTH, jnp.float32)]]
    return pl.pallas_call(
        _in_proj_kernel,
        name="in_proj",
        grid=(B, nb),
        in_specs=[tok(D),
                  pl.BlockSpec((1, N_MOD, D), lambda b, i: (b, 0, 0)),
                  const((1, D)),
                  const((D, D_IN_PACKED)),
                  pl.BlockSpec((TM_PROJ, LANES), lambda b, i: (i, 0)),
                  pl.BlockSpec((TM_PROJ, LANES), lambda b, i: (i, 0)),
                  const((LANES, GLA_KEY_WIDTH)),
                  const((1, GLA_KEY_WIDTH))],
        out_specs=att_specs + [tok(s.shape[-1]) for s in gla_shapes],
        out_shape=att_shapes + gla_shapes,
        scratch_shapes=[pltpu.VMEM((ATT_WIDTH // LANES, TM_PROJ, LANES), jnp.float32)] * 3,
        compiler_params=pltpu.CompilerParams(dimension_semantics=("arbitrary", "arbitrary"),
                                             vmem_limit_bytes=VMEM_LIMIT),
    )(x, mod3, g_pre, w_packed, cos_t, sin_t, wlr, bg)


def _attn_kernel(q_ref, kp_ref, kc_ref, vp_ref, vc_ref, o_ref, st_ref):
    i = pl.program_id(2)
    tq = q_ref.shape[1]
    tp = kp_ref.shape[1]
    row = lax.broadcasted_iota(jnp.int32, (tq, tp + tq), 0)
    col = lax.broadcasted_iota(jnp.int32, (tq, tp + tq), 1)
    valid = (col >= row) & (col <= row + ATT_SPAN) & ((col >= tp) | (i > 0))
    bias = jnp.where(valid, 0.0, NEG_BIG)
    lane = lax.broadcasted_iota(jnp.int32, (tq, LANES), 1)
    q_head = (lane >> 5) & 1
    stats = jnp.zeros((tq, LANES), jnp.float32)
    for hp in range(ATT_HEADS // 2):
        sl = slice(hp * LANES, (hp + 1) * LANES)
        q2 = q_ref[0, :, sl]
        kcat = jnp.concatenate([kp_ref[0, :, sl], kc_ref[0, :, sl]], axis=0)
        vcat = jnp.concatenate([vp_ref[0, :, sl], vc_ref[0, :, sl]], axis=0)
        outs = []
        for hh in range(2):
            qm = jnp.where(q_head == hh, q2, jnp.zeros_like(q2))
            s = lax.dot_general(qm, kcat, (((1,), (1,)), ((), ())),
                                preferred_element_type=jnp.float32) + bias
            m = jnp.max(s, axis=1, keepdims=True)
            p = jnp.exp(s - m)
            l = jnp.sum(p, axis=1, keepdims=True)
            o = jnp.dot(p.astype(vcat.dtype), vcat, preferred_element_type=jnp.float32)
            outs.append(o / l)
            stats = jnp.where(lane == 2 * hp + hh, m + jnp.log(l), stats)
        o_ref[0, :, sl] = jnp.where(lane < ATT_HEAD_DIM, outs[0], outs[1]).astype(o_ref.dtype)
    st_ref[0] = stats


def _attn_pattern(qv, kv, vv, dil):
    B, L, _ = qv.shape
    W = ATT_WIDTH
    nq = L // TQ_ATT
    cur = pl.BlockSpec((1, TQ_ATT, W), lambda b, r, i: (b, i, r))
    back = TQ_ATT // ATT_SPAN
    prev = pl.BlockSpec((1, ATT_SPAN, W), lambda b, r, i: (b, jnp.maximum(i * back - 1, 0), r))
    o, st = pl.pallas_call(
        _attn_kernel,
        name=f"attn_d{dil}",
        grid=(B, dil, nq),
        in_specs=[cur, prev, cur, prev, cur],
        out_specs=[cur, pl.BlockSpec((1, TQ_ATT, LANES), lambda b, r, i: (b, i, r))],
        out_shape=[jax.ShapeDtypeStruct((B, L, dil * W), jnp.bfloat16),
                   jax.ShapeDtypeStruct((B, L, dil * LANES), jnp.float32)],
        compiler_params=pltpu.CompilerParams(
            dimension_semantics=("arbitrary", "arbitrary", "arbitrary"),
            vmem_limit_bytes=VMEM_LIMIT),
    )(qv, kv, kv, vv, vv)
    return o, st


def _cumsum_rows(x):
    n = x.shape[0]
    row = lax.broadcasted_iota(jnp.int32, x.shape, 0)
    s = 1
    while s < n:
        x = x + jnp.where(row >= s, pltpu.roll(x, s, axis=0), 0.0)
        s *= 2
    return x


def _gla_kernel(q_ref, k_ref, v_ref, la_ref, sr_ref, g_ref, o_ref, st_ref):
    @pl.when(pl.program_id(1) == 0)
    def _():
        st_ref[...] = jnp.zeros_like(st_ref)

    C = GLA_CHUNK
    n_chunks = q_ref.shape[1] // C
    lane = lax.broadcasted_iota(jnp.int32, (C, LANES), 1)
    lane_sq = lax.broadcasted_iota(jnp.int32, (LANES, LANES), 1)
    tril = (lax.broadcasted_iota(jnp.int32, (C, C), 0)
            >= lax.broadcasted_iota(jnp.int32, (C, C), 1))
    g = g_ref[...]
    bf = jnp.bfloat16

    def chunk(c, carry):
        r0 = pl.multiple_of(c * C, C)
        rows = pl.ds(r0, C)
        for p in range(GLA_HEADS // 2):
            ksl = slice(p * LANES, (p + 1) * LANES)
            b = _cumsum_rows(la_ref[0, rows, ksl])
            b_last = b[C - 1:C, :]
            q = q_ref[0, rows, ksl].astype(jnp.float32) * (GLA_KEY_DIM ** -0.5)
            k = k_ref[0, rows, ksl].astype(jnp.float32)
            q_dec = q * jnp.exp(b)
            k_inv = (k * jnp.exp(-b)).astype(bf)
            k_dec = (k * jnp.exp(b_last - b)).astype(bf)
            st = st_ref[p]
            st_b = st.astype(bf)
            ut = []
            for hh in range(2):
                h = 2 * p + hh
                vsl = slice(h * GLA_VAL_DIM, (h + 1) * GLA_VAL_DIM)
                v = v_ref[0, rows, vsl]
                own = (lane >= GLA_KEY_DIM) if hh else (lane < GLA_KEY_DIM)
                qm = jnp.where(own, q_dec, 0.0).astype(bf)
                att = lax.dot_general(qm, k_inv, (((1,), (1,)), ((), ())),
                                      preferred_element_type=jnp.float32)
                att = jnp.where(tril, att, 0.0).astype(bf)
                o = jnp.dot(att, v, preferred_element_type=jnp.float32)
                o = o + lax.dot_general(qm, st_b, (((1,), (1,)), ((), ())),
                                        preferred_element_type=jnp.float32)
                o = _rms(o, g) * sr_ref[0, rows, vsl].astype(jnp.float32)
                o_ref[0, rows, vsl] = o.astype(o_ref.dtype)
                ut.append(lax.dot_general(v, k_dec, (((0,), (0,)), ((), ())),
                                          preferred_element_type=jnp.float32))
            st_ref[p] = st * jnp.exp(b_last) + jnp.where(lane_sq < GLA_KEY_DIM, ut[0], ut[1])
        return carry

    lax.fori_loop(0, n_chunks, chunk, 0, unroll=8)


def _gla(qg, kg, vg, la, sr, g_gla):
    B, S, _ = qg.shape
    tok = lambda w: pl.BlockSpec((1, TG_GLA, w), lambda b, i: (b, i, 0))
    return pl.pallas_call(
        _gla_kernel,
        name="gla",
        grid=(B, S // TG_GLA),
        in_specs=[tok(GLA_KEY_WIDTH), tok(GLA_KEY_WIDTH), tok(GLA_VAL_WIDTH),
                  tok(GLA_KEY_WIDTH), tok(GLA_VAL_WIDTH),
                  pl.BlockSpec((1, GLA_VAL_DIM), lambda b, i: (0, 0))],
        out_specs=tok(GLA_VAL_WIDTH),
        out_shape=jax.ShapeDtypeStruct((B, S, GLA_VAL_WIDTH), jnp.bfloat16),
        scratch_shapes=[pltpu.VMEM((GLA_HEADS // 2, GLA_VAL_DIM, LANES), jnp.float32)],
        compiler_params=pltpu.CompilerParams(dimension_semantics=("arbitrary", "arbitrary"),
                                             vmem_limit_bytes=VMEM_LIMIT),
    )(qg, kg, vg, la, sr, g_gla)


def _out_proj_kernel(o1_ref, o2_ref, o3_ref, s1_ref, s2_ref, s3_ref, og_ref, x_ref, mod_ref,
                     wout_ref, gpost_ref, gpre_ref, rw_ref, rb_ref,
                     x1_ref, h2_ref, ti_ref, tg_ref, oscr_ref, sscr_ref):
    tm = x_ref.shape[1]
    lane = lax.broadcasted_iota(jnp.int32, (tm, LANES), 1)
    for j, (d, o_ref, s_ref) in enumerate(((DILATIONS[1], o2_ref, s2_ref),
                                           (DILATIONS[2], o3_ref, s3_ref))):
        for r in range(d):
            rows = pl.ds(r, tm // d, stride=d)
            for s in range(ATT_WIDTH // LANES):
                c0 = r * ATT_WIDTH + s * LANES
                oscr_ref[j, s, rows, :] = o_ref[0, :, c0:c0 + LANES].astype(jnp.float32)
            sscr_ref[j, rows, :] = s_ref[0, :, r * LANES:(r + 1) * LANES]
    lses = [s1_ref[0], sscr_ref[0], sscr_ref[1]]
    m = jnp.maximum(jnp.maximum(lses[0], lses[1]), lses[2])
    es = [jnp.exp(t - m) for t in lses]
    inv = 1.0 / (es[0] + es[1] + es[2])
    ws = [e * inv for e in es]
    pairs = []
    for hp in range(ATT_HEADS // 2):
        sl = slice(hp * LANES, (hp + 1) * LANES)
        o_pats = [o1_ref[0, :, sl].astype(jnp.float32), oscr_ref[0, hp], oscr_ref[1, hp]]
        acc = jnp.zeros((tm, LANES), jnp.float32)
        head_of_lane = 2 * hp + (lane >> 6)
        for w, o in zip(ws, o_pats):
            acc = acc + jnp.take_along_axis(w, head_of_lane, axis=1) * o
        pairs.append(acc.astype(jnp.bfloat16))
    mixed = jnp.concatenate(pairs + [og_ref[0]], axis=1)
    y = jnp.dot(mixed, wout_ref[...], preferred_element_type=jnp.float32)
    gate1 = mod_ref[0, 2:3, :]
    shift2 = mod_ref[0, 3:4, :]
    scale2 = mod_ref[0, 4:5, :]
    x1 = x_ref[0] + gate1 * _rms(y, gpost_ref[...])
    x1_ref[0] = x1
    h2 = _rms(x1, gpre_ref[...]) * (1.0 + scale2) + shift2
    h2_ref[0] = _pack_bf16_pairs(h2)
    bf = jnp.bfloat16
    h_hi = h2.astype(bf)
    h_lo = (h2 - h_hi.astype(jnp.float32)).astype(bf)
    rw = rw_ref[...]
    w_hi = rw.astype(bf)
    w_lo = (rw - w_hi.astype(jnp.float32)).astype(bf)
    logits = (jnp.dot(h_hi, w_hi, preferred_element_type=jnp.float32)
              + jnp.dot(h_lo, w_hi, preferred_element_type=jnp.float32)
              + jnp.dot(h_hi, w_lo, preferred_element_type=jnp.float32)) + rb_ref[...]
    lane_f = lane.astype(jnp.float32)
    vals, idxs = [], []
    for _ in range(TOP_K):
        mk = jnp.max(logits, axis=1, keepdims=True)
        ik = jnp.min(jnp.where(logits == mk, lane_f, float(LANES)), axis=1, keepdims=True)
        logits = jnp.where(lane_f == ik, -jnp.inf, logits)
        vals.append(mk)
        idxs.append(ik)
    ex = [jnp.exp(v - vals[0]) for v in vals]
    den = ex[0] + ex[1] + ex[2] + ex[3]
    ti = jnp.zeros((tm, LANES), jnp.float32)
    tg = jnp.zeros((tm, LANES), jnp.float32)
    for kk in range(TOP_K):
        ti = jnp.where(lane == kk, idxs[kk], ti)
        tg = jnp.where(lane == kk, ex[kk] / den, tg)
    ti_ref[0] = ti.astype(jnp.int32)
    tg_ref[0] = tg


def _out_proj(o_pats, st_pats, og, x, mod3, w_out, g_post, g_pre, rw, rb):
    B, S, D = x.shape
    tok = lambda w: pl.BlockSpec((1, TM_OUT, w), lambda b, i: (b, i, 0))
    const = lambda shape: pl.BlockSpec(shape, lambda b, i: tuple(0 for _ in shape))
    out_shapes = [jax.ShapeDtypeStruct((B, S, D), jnp.float32),
                  jax.ShapeDtypeStruct((B, S, D // 2), jnp.uint32),
                  jax.ShapeDtypeStruct((B, S, LANES), jnp.int32),
                  jax.ShapeDtypeStruct((B, S, LANES), jnp.float32)]
    return pl.pallas_call(
        _out_proj_kernel,
        name="out_proj",
        grid=(B, S // TM_OUT),
        in_specs=[pl.BlockSpec((1, TM_OUT // d, d * ATT_WIDTH), lambda b, i: (b, i, 0))
                  for d in DILATIONS]
                 + [pl.BlockSpec((1, TM_OUT // d, d * LANES), lambda b, i: (b, i, 0))
                    for d in DILATIONS]
                 + [tok(GLA_VAL_WIDTH), tok(D),
                  pl.BlockSpec((1, N_MOD, D), lambda b, i: (b, 0, 0)),
                  const((D, D)), const((1, D)), const((1, D)),
                  const((D, LANES)), const((1, LANES))],
        out_specs=[tok(D), tok(D // 2), tok(LANES), tok(LANES)],
        out_shape=out_shapes,
        scratch_shapes=[pltpu.VMEM((2, ATT_WIDTH // LANES, TM_OUT, LANES), jnp.float32),
                        pltpu.VMEM((2, TM_OUT, LANES), jnp.float32)],
        compiler_params=pltpu.CompilerParams(dimension_semantics=("arbitrary", "arbitrary"),
                                             vmem_limit_bytes=VMEM_LIMIT),
    )(*o_pats, *st_pats, og, x, mod3, w_out, g_post, g_pre, rw, rb)


def _split_gate_up_kernel(w_ref, wg_ref, wl_ref):
    group = 2 * LANES
    src = lax.broadcasted_iota(jnp.int32, (group, group), 0)
    dst = lax.broadcasted_iota(jnp.int32, (group, group), 1)
    want = jnp.where(dst < LANES, 2 * dst, 2 * (dst - LANES) + 1)
    perm = jnp.where(src == want, 1.0, 0.0).astype(jnp.bfloat16)
    for j in range(w_ref.shape[1] // group):
        t = jnp.dot(w_ref[:, j * group:(j + 1) * group].astype(jnp.bfloat16), perm,
                    preferred_element_type=jnp.float32)
        wg_ref[:, j * LANES:(j + 1) * LANES] = t[:, :LANES].astype(wg_ref.dtype)
        wl_ref[:, j * LANES:(j + 1) * LANES] = t[:, LANES:].astype(wl_ref.dtype)


def _split_gate_up(w_gate_up):
    E, K, N2 = w_gate_up.shape
    out = jax.ShapeDtypeStruct((E, K, N2 // 2), jnp.bfloat16)
    ospec = pl.BlockSpec((None, TK_SPLIT, N2 // 2), lambda e, i: (e, i, 0))
    return pl.pallas_call(
        _split_gate_up_kernel,
        name="split_gate_up",
        grid=(E, K // TK_SPLIT),
        in_specs=[pl.BlockSpec((None, TK_SPLIT, N2), lambda e, i: (e, i, 0))],
        out_specs=[ospec, ospec],
        out_shape=[out, out],
        compiler_params=pltpu.CompilerParams(dimension_semantics=("arbitrary", "arbitrary"),
                                             vmem_limit_bytes=VMEM_LIMIT),
    )(w_gate_up)


def _moe_kernel(be_ref, nu_ref, x_ref, wg_ref, wl_ref, bg_ref, bl_ref, wd_ref, bd_ref, o_ref):
    i = pl.program_id(0)

    @pl.when(i < nu_ref[0])
    def _():
        lo, hi = _unpack_bf16_pairs(x_ref[...])
        x = jnp.concatenate([lo, hi], axis=1).astype(jnp.bfloat16)
        xg = jnp.dot(x, wg_ref[...], preferred_element_type=jnp.float32) + bg_ref[...]
        xl = jnp.dot(x, wl_ref[...], preferred_element_type=jnp.float32) + bl_ref[...]
        xg = jnp.minimum(xg, SWIGLU_LIMIT)
        xl = jnp.clip(xl, -SWIGLU_LIMIT, SWIGLU_LIMIT)
        act = xg * (1.0 / (1.0 + jnp.exp(-SWIGLU_ALPHA * xg))) * (xl + 1.0)
        out = jnp.dot(act.astype(jnp.bfloat16), wd_ref[...],
                      preferred_element_type=jnp.float32) + bd_ref[...]
        o_ref[...] = _pack_bf16_pairs(out)

    @pl.when(i >= nu_ref[0])
    def _():
        o_ref[...] = jnp.zeros_like(o_ref)


def _moe(xs, blk_e, n_used, wg, wl, bg, bl, wd, bd):
    P, half = xs.shape
    D = 2 * half
    n_blocks = P // TM_MOE
    wspec = lambda k, n: pl.BlockSpec((None, k, n), lambda i, be, nu: (be[i], 0, 0))
    grid_spec = pltpu.PrefetchScalarGridSpec(
        num_scalar_prefetch=2,
        grid=(n_blocks,),
        in_specs=[pl.BlockSpec((TM_MOE, half), lambda i, be, nu: (i, 0)),
                  wspec(D, D_FF), wspec(D, D_FF), wspec(1, D_FF), wspec(1, D_FF),
                  wspec(D_FF, D), wspec(1, D)],
        out_specs=pl.BlockSpec((TM_MOE, half), lambda i, be, nu: (i, 0)),
    )
    return pl.pallas_call(
        _moe_kernel,
        name="moe",
        grid_spec=grid_spec,
        out_shape=jax.ShapeDtypeStruct((P, half), jnp.uint32),
        compiler_params=pltpu.CompilerParams(dimension_semantics=("arbitrary",),
                                             vmem_limit_bytes=VMEM_LIMIT),
    )(blk_e, n_used, xs, wg, wl, bg, bl, wd, bd)


def _final_kernel(x1_ref, y0_ref, y1_ref, y2_ref, y3_ref, tg_ref, mod_ref, g_ref, o_ref):
    gates = tg_ref[...]
    half = y0_ref.shape[1]
    y_lo = jnp.zeros((x1_ref.shape[0], half), jnp.float32)
    y_hi = jnp.zeros((x1_ref.shape[0], half), jnp.float32)
    for kk, yk_ref in enumerate((y0_ref, y1_ref, y2_ref, y3_ref)):
        lo, hi = _unpack_bf16_pairs(yk_ref[...])
        y_lo = y_lo + gates[:, kk:kk + 1] * lo
        y_hi = y_hi + gates[:, kk:kk + 1] * hi
    y = jnp.concatenate([y_lo, y_hi], axis=1)
    gate2 = mod_ref[0, 5:6, :]
    o_ref[...] = x1_ref[...] + gate2 * _rms(y, g_ref[...])


def _final_part_kernel(x1_ref, y0_ref, y1_ref, y2_ref, y3_ref, tg_ref, mod_ref, g_ref, prev_ref,
                       o_ref):
    del prev_ref
    _final_kernel(x1_ref, y0_ref, y1_ref, y2_ref, y3_ref, tg_ref, mod_ref, g_ref, o_ref)


def _final(x1, yk, tg, mod3, g_post, seq_len, part, prev):
    T, D = x1.shape
    nt = seq_len // TM_FIN
    t0 = part * nt
    yspec = lambda kk: pl.BlockSpec((TM_FIN, D // 2), lambda i: (kk * nt + i, 0))
    in_specs = ([pl.BlockSpec((TM_FIN, D), lambda i: (t0 + i, 0))]
                + [yspec(kk) for kk in range(TOP_K)]
                + [pl.BlockSpec((TM_FIN, LANES), lambda i: (t0 + i, 0)),
                   pl.BlockSpec((1, N_MOD, D), lambda i: (part, 0, 0)),
                   pl.BlockSpec((1, D), lambda i: (0, 0))])
    args = [x1, yk, yk, yk, yk, tg, mod3, g_post]
    body, aliases = _final_kernel, {}
    if prev is not None:
        in_specs.append(pl.BlockSpec(memory_space=pl.ANY))
        args.append(prev)
        body, aliases = _final_part_kernel, {len(args) - 1: 0}
    return pl.pallas_call(
        body,
        name="final",
        grid=(nt,),
        in_specs=in_specs,
        out_specs=pl.BlockSpec((TM_FIN, D), lambda i: (t0 + i, 0)),
        out_shape=jax.ShapeDtypeStruct((T, D), jnp.float32),
        input_output_aliases=aliases,
        compiler_params=pltpu.CompilerParams(dimension_semantics=("arbitrary",),
                                             vmem_limit_bytes=VMEM_LIMIT),
    )(*args)


def _pack_w_in(w_in):
    n = np.arange(LANES)
    src = np.where(n < 32, n, np.where(n < 64, n + 32, np.where(n < 96, n - 32, n)))
    perm = (np.arange(ATT_WIDTH // LANES)[:, None] * LANES + src[None, :]).reshape(-1)
    lr = jnp.pad(w_in[:, 3072:3072 + GLA_GATE_RANK], ((0, 0), (0, LANES - GLA_GATE_RANK)))
    packed = jnp.concatenate([w_in[:, perm], w_in[:, ATT_WIDTH + perm], w_in[:, 1024:3072], lr],
                             axis=1)
    return packed.astype(jnp.bfloat16)


def _rope_tables(seq_len):
    half = ATT_HEAD_DIM // 2
    inv_freq = ROPE_THETA ** (-jnp.arange(half, dtype=jnp.float32) / half)
    ang = jnp.arange(seq_len, dtype=jnp.float32)[:, None] * inv_freq[None, :]
    cos = jnp.tile(jnp.cos(ang), (1, LANES // half))
    sin = jnp.tile(jnp.sin(ang), (1, LANES // half))
    sign = jnp.where(jnp.arange(LANES) < LANES // 2, -1.0, 1.0)
    return cos, sin * sign


def _route(top_i, n_tokens):
    A = n_tokens * TOP_K
    e_flat = top_i.reshape(A)
    onehot = (e_flat[:, None] == jnp.arange(N_EXPERTS, dtype=jnp.int32)[None, :]).astype(jnp.int32)
    csum = jnp.cumsum(onehot, axis=0)
    rank = jnp.sum(onehot * csum, axis=1) - 1
    counts = csum[-1]
    padded = (counts + TM_MOE - 1) // TM_MOE * TM_MOE
    pend = jnp.cumsum(padded)
    pstart = pend - padded
    pos = pstart[e_flat] + rank
    n_blocks = -(-(A + N_EXPERTS * (TM_MOE - 1)) // TM_MOE)
    blk_start = jnp.arange(n_blocks, dtype=jnp.int32) * TM_MOE
    blk_e = jnp.minimum(jnp.sum((pend[None, :] <= blk_start[:, None]).astype(jnp.int32), axis=1),
                        N_EXPERTS - 1)
    n_used = (pend[-1:] // TM_MOE).astype(jnp.int32)
    pos_kmajor = pos.reshape(n_tokens, TOP_K).T
    return pos_kmajor, n_blocks * TM_MOE, blk_e, n_used


def kernel(x, c, w_mod, b_mod, g_pre_mix, w_in, w_gate_lr, b_gate, g_gla, w_out, g_post_mix,
           g_pre_ffn, router_w, router_b, w_gate_up, b_gate_up, w_down, b_down, g_post_ffn):
    B, S, D = x.shape
    T = B * S
    bf = jnp.bfloat16
    cos_t, sin_t = _rope_tables(S)
    for l in range(w_mod.shape[0]):
        wg, wl = _split_gate_up(w_gate_up[l])
        wd = w_down[l].astype(bf)
        mod3 = _mod(c, w_mod[l], b_mod[l]).reshape(B, N_MOD, D)
        wlr = jnp.pad(w_gate_lr[l], ((0, LANES - GLA_GATE_RANK), (0, 0))).astype(bf)
        proj = _in_proj(x, mod3, g_pre_mix[l][None], _pack_w_in(w_in[l]), cos_t, sin_t, wlr,
                        b_gate[l][None])
        n_pat = len(DILATIONS)
        qa, ka, va = proj[:n_pat], proj[n_pat:2 * n_pat], proj[2 * n_pat:3 * n_pat]
        qg, kg, vg, sr, la = proj[3 * n_pat:]
        pats = [_attn_pattern(qa[j], ka[j], va[j], d) for j, d in enumerate(DILATIONS)]
        og = _gla(qg, kg, vg, la, sr, g_gla[l][None])
        rw = jnp.pad(router_w[l], ((0, 0), (0, LANES - N_EXPERTS)))
        rb = jnp.pad(router_b[l], (0, LANES - N_EXPERTS), constant_values=NEG_BIG)[None]
        x1, h2, ti, tg = _out_proj([p[0] for p in pats], [p[1] for p in pats], og, x, mod3,
                                   w_out[l].astype(bf), g_post_mix[l][None], g_pre_ffn[l][None],
                                   rw, rb)
        top_i = ti.reshape(T, LANES)[:, :TOP_K]
        pos_kmajor, n_slots, blk_e, n_used = _route(top_i, T)
        xs = _scatter_rows(h2.reshape(T, D // 2), pos_kmajor, n_slots)
        out_buf = _moe(xs, blk_e, n_used, wg, wl,
                       b_gate_up[l][:, None, 0::2], b_gate_up[l][:, None, 1::2],
                       wd, b_down[l][:, None, :])
        out = None
        for b in range(B):
            yk = _gather_rows(out_buf, pos_kmajor[:, b * S:(b + 1) * S].reshape(-1))
            out = _final(x1.reshape(T, D), yk, tg.reshape(T, LANES), mod3, g_post_ffn[l][None],
                         S, b, out)
        x = out.reshape(B, S, D)
    return x
```

```python
import functools

import numpy as np
import jax
import jax.numpy as jnp
from jax import lax
from jax.experimental import pallas as pl
from jax.experimental.pallas import tpu as pltpu
from jax.experimental.pallas import tpu_sc as plsc

D_MODEL = 1024
ATT_HEADS = 8
ATT_HEAD_DIM = 64
ATT_WIDTH = ATT_HEADS * ATT_HEAD_DIM
DILATIONS = (1, 4, 16)
ATT_SPAN = 128
ROPE_THETA = 10000.0
GLA_HEADS = 4
GLA_KEY_DIM = 64
GLA_VAL_DIM = 128
GLA_KEY_WIDTH = GLA_HEADS * GLA_KEY_DIM
GLA_VAL_WIDTH = GLA_HEADS * GLA_VAL_DIM
GLA_GATE_RANK = 16
GLA_GATE_NORMALIZER = 16.0
GLA_CHUNK = 64
N_EXPERTS = 32
TOP_K = 4
D_FF = D_MODEL
SWIGLU_LIMIT = 7.0
SWIGLU_ALPHA = 1.702
NORM_EPS = 1e-6
N_MOD = 6

LANES = 128
NEG_BIG = -1e30

TM_PROJ = 512
TQ_ATT = 128
TG_GLA = 512
TM_OUT = 512
TM_MOE = 512
TM_FIN = 256
TN_MOD = 512
TK_SPLIT = 512
TM_ROUTE = 512
SC_INDEX_WINDOW = 128
SC_GATHER_WINDOW = 64
VMEM_LIMIT = 48 * 1024 * 1024

_C_QA, _C_KA, _C_VA = 0, 512, 1024
_C_QG, _C_KG, _C_VG, _C_RG, _C_LR = 1536, 1792, 2048, 2560, 3072
D_IN_PACKED = 3200


def _rms(x, g):
    return x * lax.rsqrt(jnp.mean(x * x, axis=-1, keepdims=True) + NORM_EPS) * g


def _pack_bf16_pairs(x):
    n = x.shape[1] // 2
    u = lax.bitcast_convert_type(x.astype(jnp.bfloat16).astype(jnp.float32), jnp.uint32)
    return (u[:, :n] >> 16) | (u[:, n:] & jnp.uint32(0xFFFF0000))


def _unpack_bf16_pairs(w):
    lo = lax.bitcast_convert_type(w << 16, jnp.float32)
    hi = lax.bitcast_convert_type(w & jnp.uint32(0xFFFF0000), jnp.float32)
    return lo, hi


def _gather_rows(data, idx):
    n_rows = idx.shape[0]
    width = data.shape[1]
    mesh = plsc.VectorSubcoreMesh(core_axis_name="core", subcore_axis_name="subcore")
    n_workers = mesh.num_cores * mesh.num_subcores
    per_worker = n_rows // n_workers
    assert per_worker * n_workers == n_rows and per_worker % SC_INDEX_WINDOW == 0
    halves = SC_INDEX_WINDOW // SC_GATHER_WINDOW

    @pl.kernel(out_type=jax.ShapeDtypeStruct((n_rows, width), data.dtype), mesh=mesh,
               name="gather_rows",
               scratch_types=[pltpu.VMEM((1, SC_INDEX_WINDOW), jnp.int32),
                              pltpu.VMEM((SC_GATHER_WINDOW, width), data.dtype)])
    def gather(x_hbm, i_hbm, o_hbm, idx_vmem, rows_vmem):
        worker = lax.axis_index("core") * mesh.num_subcores + lax.axis_index("subcore")
        base = worker * per_worker

        @pl.loop(0, per_worker // SC_INDEX_WINDOW)
        def _(j):
            off = base + j * SC_INDEX_WINDOW
            pltpu.sync_copy(i_hbm.at[:, pl.ds(off, SC_INDEX_WINDOW)], idx_vmem)
            for h in range(halves):
                part = idx_vmem.at[0, pl.ds(h * SC_GATHER_WINDOW, SC_GATHER_WINDOW)]
                pltpu.sync_copy(x_hbm.at[part], rows_vmem)
                pltpu.sync_copy(rows_vmem,
                                o_hbm.at[pl.ds(off + h * SC_GATHER_WINDOW, SC_GATHER_WINDOW)])

    return gather(data, idx.reshape(1, n_rows))


def _scatter_rows(data, idx, n_out):
    n_copies, n_rows = idx.shape
    width = data.shape[1]
    mesh = plsc.VectorSubcoreMesh(core_axis_name="core", subcore_axis_name="subcore")
    n_workers = mesh.num_cores * mesh.num_subcores
    per_worker = n_rows // n_workers
    assert per_worker * n_workers == n_rows and per_worker % SC_INDEX_WINDOW == 0
    halves = SC_INDEX_WINDOW // SC_GATHER_WINDOW

    @pl.kernel(out_type=jax.ShapeDtypeStruct((n_out, width), data.dtype), mesh=mesh,
               name="scatter_rows",
               scratch_types=[pltpu.VMEM((n_copies, SC_INDEX_WINDOW), jnp.int32),
                              pltpu.VMEM((SC_GATHER_WINDOW, width), data.dtype)])
    def scatter(x_hbm, i_hbm, o_hbm, idx_vmem, rows_vmem):
        worker = lax.axis_index("core") * mesh.num_subcores + lax.axis_index("subcore")
        base = worker * per_worker

        @pl.loop(0, per_worker // SC_INDEX_WINDOW)
        def _(j):
            off = base + j * SC_INDEX_WINDOW
            pltpu.sync_copy(i_hbm.at[:, pl.ds(off, SC_INDEX_WINDOW)], idx_vmem)
            for h in range(halves):
                pltpu.sync_copy(x_hbm.at[pl.ds(off + h * SC_GATHER_WINDOW, SC_GATHER_WINDOW)],
                                rows_vmem)
                for k in range(n_copies):
                    part = idx_vmem.at[k, pl.ds(h * SC_GATHER_WINDOW, SC_GATHER_WINDOW)]
                    pltpu.sync_copy(rows_vmem, o_hbm.at[part])

    return scatter(data, idx)


def _mod_kernel(ct_ref, w_ref, b_ref, o_ref):
    ct = ct_ref[...]
    s = ct * (1.0 / (1.0 + jnp.exp(-ct)))
    w = w_ref[...]
    rows = [jnp.sum(s[:, b:b + 1] * w, axis=0, keepdims=True) for b in range(ct.shape[1])]
    o_ref[...] = jnp.concatenate(rows, axis=0) + b_ref[...]


def _mod(c, w_mod, b_mod):
    B = c.shape[0]
    n = w_mod.shape[1]
    return pl.pallas_call(
        _mod_kernel,
        name="mod",
        grid=(n // TN_MOD,),
        in_specs=[pl.BlockSpec((D_MODEL, B), lambda j: (0, 0)),
                  pl.BlockSpec((D_MODEL, TN_MOD), lambda j: (0, j)),
                  pl.BlockSpec((1, TN_MOD), lambda j: (0, j))],
        out_specs=pl.BlockSpec((B, TN_MOD), lambda j: (0, j)),
        out_shape=jax.ShapeDtypeStruct((B, n), jnp.float32),
        compiler_params=pltpu.CompilerParams(dimension_semantics=("arbitrary",),
                                             vmem_limit_bytes=VMEM_LIMIT),
    )(c.T, w_mod, b_mod.reshape(1, n))


def _store_residue_views(scr_ref, out_refs):
    ns, tm, _ = scr_ref.shape
    w = ns * LANES
    for d, ref in zip(DILATIONS, out_refs):
        for r in range(d):
            for s in range(ns):
                c0 = r * w + s * LANES
                ref[0, :, c0:c0 + LANES] = scr_ref[s, pl.ds(r, tm // d, stride=d), :].astype(ref.dtype)


def _in_proj_kernel(x_ref, mod_ref, g_ref, w_ref, cos_ref, sin_ref, wlr_ref, bg_ref,
                    qa1_ref, qa4_ref, qa16_ref, ka1_ref, ka4_ref, ka16_ref,
                    va1_ref, va4_ref, va16_ref, qg_ref, kg_ref, vg_ref, sr_ref, la_ref,
                    qs_ref, ks_ref, vs_ref):
    x = x_ref[0]
    shift = mod_ref[0, 0:1, :]
    scale = mod_ref[0, 1:2, :]
    h = (_rms(x, g_ref[...]) * (1.0 + scale) + shift).astype(jnp.bfloat16)

    def proj(c0, width):
        return jnp.dot(h, w_ref[:, c0:c0 + width], preferred_element_type=jnp.float32)

    cos = cos_ref[...]
    sin = sin_ref[...]

    def rope(c0, scr_ref, mult):
        p = proj(c0, ATT_WIDTH)
        for s in range(ATT_WIDTH // LANES):
            t = p[:, s * LANES:(s + 1) * LANES]
            r = t * cos + pltpu.roll(t, LANES // 2, axis=1) * sin
            if mult != 1.0:
                r = r * mult
            scr_ref[s] = r

    rope(_C_QA, qs_ref, ATT_HEAD_DIM ** -0.5)
    _store_residue_views(qs_ref, (qa1_ref, qa4_ref, qa16_ref))
    rope(_C_KA, ks_ref, 1.0)
    _store_residue_views(ks_ref, (ka1_ref, ka4_ref, ka16_ref))
    pv = proj(_C_VA, ATT_WIDTH)
    for s in range(ATT_WIDTH // LANES):
        vs_ref[s] = pv[:, s * LANES:(s + 1) * LANES]
    _store_residue_views(vs_ref, (va1_ref, va4_ref, va16_ref))
    qg_ref[0] = proj(_C_QG, GLA_KEY_WIDTH).astype(qg_ref.dtype)
    kg_ref[0] = proj(_C_KG, GLA_KEY_WIDTH).astype(kg_ref.dtype)
    vg_ref[0] = proj(_C_VG, GLA_VAL_WIDTH).astype(vg_ref.dtype)
    r = proj(_C_RG, GLA_VAL_WIDTH)
    sr_ref[0] = (r * (1.0 / (1.0 + jnp.exp(-r)))).astype(sr_ref.dtype)
    lr = proj(_C_LR, LANES).astype(jnp.bfloat16)
    z = jnp.dot(lr, wlr_ref[...], preferred_element_type=jnp.float32) + bg_ref[...]
    log_sig = jnp.minimum(z, 0.0) - jnp.log(1.0 + jnp.exp(-jnp.abs(z)))
    la_ref[0] = log_sig * (1.0 / GLA_GATE_NORMALIZER)


def _in_proj(x, mod3, g_pre, w_packed, cos_t, sin_t, wlr, bg):
    B, S, D = x.shape
    nb = S // TM_PROJ
    tok = lambda w: pl.BlockSpec((1, TM_PROJ, w), lambda b, i: (b, i, 0))
    const = lambda shape: pl.BlockSpec(shape, lambda b, i: tuple(0 for _ in shape))
    bf = jnp.bfloat16
    att_shapes = [jax.ShapeDtypeStruct((B, S // d, d * ATT_WIDTH), bf) for d in DILATIONS] * 3
    att_specs = [pl.BlockSpec((1, TM_PROJ // d, d * ATT_WIDTH), lambda b, i: (b, i, 0))
                 for d in DILATIONS] * 3
    gla_shapes = [jax.ShapeDtypeStruct((B, S, w), dt) for w, dt in
                  [(GLA_KEY_WIDTH, bf), (GLA_KEY_WIDTH, bf),
                   (GLA_VAL_WIDTH, bf), (GLA_VAL_WIDTH, bf), (GLA_KEY_WIDTH, jnp.float32)]]
    return pl.pallas_call(
        _in_proj_kernel,
        name="in_proj",
        grid=(B, nb),
        in_specs=[tok(D),
                  pl.BlockSpec((1, N_MOD, D), lambda b, i: (b, 0, 0)),
                  const((1, D)),
                  const((D, D_IN_PACKED)),
                  pl.BlockSpec((TM_PROJ, LANES), lambda b, i: (i, 0)),
                  pl.BlockSpec((TM_PROJ, LANES), lambda b, i: (i, 0)),
                  const((LANES, GLA_KEY_WIDTH)),
                  const((1, GLA_KEY_WIDTH))],
        out_specs=att_specs + [tok(s.shape[-1]) for s in gla_shapes],
        out_shape=att_shapes + gla_shapes,
        scratch_shapes=[pltpu.VMEM((ATT_WIDTH // LANES, TM_PROJ, LANES), jnp.float32)] * 3,
        compiler_params=pltpu.CompilerParams(dimension_semantics=("arbitrary", "arbitrary"),
                                             vmem_limit_bytes=VMEM_LIMIT),
    )(x, mod3, g_pre, w_packed, cos_t, sin_t, wlr, bg)


def _attn_kernel(q_ref, kp_ref, kc_ref, vp_ref, vc_ref, o_ref, st_ref):
    i = pl.program_id(2)
    tq = q_ref.shape[1]
    tp = kp_ref.shape[1]
    row = lax.broadcasted_iota(jnp.int32, (tq, tp + tq), 0)
    col = lax.broadcasted_iota(jnp.int32, (tq, tp + tq), 1)
    valid = (col >= row) & (col <= row + ATT_SPAN) & ((col >= tp) | (i > 0))
    bias = jnp.where(valid, 0.0, NEG_BIG)
    lane = lax.broadcasted_iota(jnp.int32, (tq, LANES), 1)
    q_head = (lane >> 5) & 1
    stats = jnp.zeros((tq, LANES), jnp.float32)
    for hp in range(ATT_HEADS // 2):
        sl = slice(hp * LANES, (hp + 1) * LANES)
        q2 = q_ref[0, :, sl]
        kcat = jnp.concatenate([kp_ref[0, :, sl], kc_ref[0, :, sl]], axis=0)
        vcat = jnp.concatenate([vp_ref[0, :, sl], vc_ref[0, :, sl]], axis=0)
        outs = []
        for hh in range(2):
            qm = jnp.where(q_head == hh, q2, jnp.zeros_like(q2))
            s = lax.dot_general(qm, kcat, (((1,), (1,)), ((), ())),
                                preferred_element_type=jnp.float32) + bias
            m = jnp.max(s, axis=1, keepdims=True)
            p = jnp.exp(s - m)
            l = jnp.sum(p, axis=1, keepdims=True)
            o = jnp.dot(p.astype(vcat.dtype), vcat, preferred_element_type=jnp.float32)
            outs.append(o / l)
            stats = jnp.where(lane == 2 * hp + hh, m + jnp.log(l), stats)
        o_ref[0, :, sl] = jnp.where(lane < ATT_HEAD_DIM, outs[0], outs[1]).astype(o_ref.dtype)
    st_ref[0] = stats


def _attn_pattern(qv, kv, vv, dil):
    B, L, _ = qv.shape
    W = ATT_WIDTH
    nq = L // TQ_ATT
    cur = pl.BlockSpec((1, TQ_ATT, W), lambda b, r, i: (b, i, r))
    back = TQ_ATT // ATT_SPAN
    prev = pl.BlockSpec((1, ATT_SPAN, W), lambda b, r, i: (b, jnp.maximum(i * back - 1, 0), r))
    o, st = pl.pallas_call(
        _attn_kernel,
        name=f"attn_d{dil}",
        grid=(B, dil, nq),
        in_specs=[cur, prev, cur, prev, cur],
        out_specs=[cur, pl.BlockSpec((1, TQ_ATT, LANES), lambda b, r, i: (b, i, r))],
        out_shape=[jax.ShapeDtypeStruct((B, L, dil * W), jnp.bfloat16),
                   jax.ShapeDtypeStruct((B, L, dil * LANES), jnp.float32)],
        compiler_params=pltpu.CompilerParams(
            dimension_semantics=("arbitrary", "arbitrary", "arbitrary"),
            vmem_limit_bytes=VMEM_LIMIT),
    )(qv, kv, kv, vv, vv)
    return o, st


def _cumsum_rows(x):
    n = x.shape[0]
    row = lax.broadcasted_iota(jnp.int32, x.shape, 0)
    s = 1
    while s < n:
        x = x + jnp.where(row >= s, pltpu.roll(x, s, axis=0), 0.0)
        s *= 2
    return x


def _gla_kernel(q_ref, k_ref, v_ref, la_ref, sr_ref, g_ref, o_ref, st_ref):
    @pl.when(pl.program_id(1) == 0)
    def _():
        st_ref[...] = jnp.zeros_like(st_ref)

    C = GLA_CHUNK
    n_chunks = q_ref.shape[1] // C
    lane = lax.broadcasted_iota(jnp.int32, (C, LANES), 1)
    lane_sq = lax.broadcasted_iota(jnp.int32, (LANES, LANES), 1)
    tril = (lax.broadcasted_iota(jnp.int32, (C, C), 0)
            >= lax.broadcasted_iota(jnp.int32, (C, C), 1))
    g = g_ref[...]
    bf = jnp.bfloat16

    def chunk(c, carry):
        r0 = pl.multiple_of(c * C, C)
        rows = pl.ds(r0, C)
        for p in range(GLA_HEADS // 2):
            ksl = slice(p * LANES, (p + 1) * LANES)
            b = _cumsum_rows(la_ref[0, rows, ksl])
            b_last = b[C - 1:C, :]
            q = q_ref[0, rows, ksl].astype(jnp.float32) * (GLA_KEY_DIM ** -0.5)
            k = k_ref[0, rows, ksl].astype(jnp.float32)
            q_dec = q * jnp.exp(b)
            k_inv = (k * jnp.exp(-b)).astype(bf)
            k_dec = (k * jnp.exp(b_last - b)).astype(bf)
            st = st_ref[p]
            st_b = st.astype(bf)
            ut = []
            for hh in range(2):
                h = 2 * p + hh
                vsl = slice(h * GLA_VAL_DIM, (h + 1) * GLA_VAL_DIM)
                v = v_ref[0, rows, vsl]
                own = (lane >= GLA_KEY_DIM) if hh else (lane < GLA_KEY_DIM)
                qm = jnp.where(own, q_dec, 0.0).astype(bf)
                att = lax.dot_general(qm, k_inv, (((1,), (1,)), ((), ())),
                                      preferred_element_type=jnp.float32)
                att = jnp.where(tril, att, 0.0).astype(bf)
                o = jnp.dot(att, v, preferred_element_type=jnp.float32)
                o = o + lax.dot_general(qm, st_b, (((1,), (1,)), ((), ())),
                                        preferred_element_type=jnp.float32)
                o = _rms(o, g) * sr_ref[0, rows, vsl].astype(jnp.float32)
                o_ref[0, rows, vsl] = o.astype(o_ref.dtype)
                ut.append(lax.dot_general(v, k_dec, (((0,), (0,)), ((), ())),
                                          preferred_element_type=jnp.float32))
            st_ref[p] = st * jnp.exp(b_last) + jnp.where(lane_sq < GLA_KEY_DIM, ut[0], ut[1])
        return carry

    lax.fori_loop(0, n_chunks, chunk, 0, unroll=8)


def _gla(qg, kg, vg, la, sr, g_gla):
    B, S, _ = qg.shape
    tok = lambda w: pl.BlockSpec((1, TG_GLA, w), lambda b, i: (b, i, 0))
    return pl.pallas_call(
        _gla_kernel,
        name="gla",
        grid=(B, S // TG_GLA),
        in_specs=[tok(GLA_KEY_WIDTH), tok(GLA_KEY_WIDTH), tok(GLA_VAL_WIDTH),
                  tok(GLA_KEY_WIDTH), tok(GLA_VAL_WIDTH),
                  pl.BlockSpec((1, GLA_VAL_DIM), lambda b, i: (0, 0))],
        out_specs=tok(GLA_VAL_WIDTH),
        out_shape=jax.ShapeDtypeStruct((B, S, GLA_VAL_WIDTH), jnp.bfloat16),
        scratch_shapes=[pltpu.VMEM((GLA_HEADS // 2, GLA_VAL_DIM, LANES), jnp.float32)],
        compiler_params=pltpu.CompilerParams(dimension_semantics=("arbitrary", "arbitrary"),
                                             vmem_limit_bytes=VMEM_LIMIT),
    )(qg, kg, vg, la, sr, g_gla)


def _out_proj_kernel(o1_ref, o2_ref, o3_ref, s1_ref, s2_ref, s3_ref, og_ref, x_ref, mod_ref,
                     wout_ref, gpost_ref, gpre_ref, rw_ref, rb_ref,
                     x1_ref, h2_ref, ti_ref, tg_ref, oscr_ref, sscr_ref):
    tm = x_ref.shape[1]
    lane = lax.broadcasted_iota(jnp.int32, (tm, LANES), 1)
    for j, (d, o_ref, s_ref) in enumerate(((DILATIONS[1], o2_ref, s2_ref),
                                           (DILATIONS[2], o3_ref, s3_ref))):
        for r in range(d):
            rows = pl.ds(r, tm // d, stride=d)
            for s in range(ATT_WIDTH // LANES):
                c0 = r * ATT_WIDTH + s * LANES
                oscr_ref[j, s, rows, :] = o_ref[0, :, c0:c0 + LANES].astype(jnp.float32)
            sscr_ref[j, rows, :] = s_ref[0, :, r * LANES:(r + 1) * LANES]
    lses = [s1_ref[0], sscr_ref[0], sscr_ref[1]]
    m = jnp.maximum(jnp.maximum(lses[0], lses[1]), lses[2])
    es = [jnp.exp(t - m) for t in lses]
    inv = 1.0 / (es[0] + es[1] + es[2])
    ws = [e * inv for e in es]
    pairs = []
    for hp in range(ATT_HEADS // 2):
        sl = slice(hp * LANES, (hp + 1) * LANES)
        o_pats = [o1_ref[0, :, sl].astype(jnp.float32), oscr_ref[0, hp], oscr_ref[1, hp]]
        acc = jnp.zeros((tm, LANES), jnp.float32)
        head_of_lane = 2 * hp + (lane >> 6)
        for w, o in zip(ws, o_pats):
            acc = acc + jnp.take_along_axis(w, head_of_lane, axis=1) * o
        pairs.append(acc.astype(jnp.bfloat16))
    mixed = jnp.concatenate(pairs + [og_ref[0]], axis=1)
    y = jnp.dot(mixed, wout_ref[...], preferred_element_type=jnp.float32)
    gate1 = mod_ref[0, 2:3, :]
    shift2 = mod_ref[0, 3:4, :]
    scale2 = mod_ref[0, 4:5, :]
    x1 = x_ref[0] + gate1 * _rms(y, gpost_ref[...])
    x1_ref[0] = x1
    h2 = _rms(x1, gpre_ref[...]) * (1.0 + scale2) + shift2
    h2_ref[0] = _pack_bf16_pairs(h2)
    bf = jnp.bfloat16
    h_hi = h2.astype(bf)
    h_lo = (h2 - h_hi.astype(jnp.float32)).astype(bf)
    rw = rw_ref[...]
    w_hi = rw.astype(bf)
    w_lo = (rw - w_hi.astype(jnp.float32)).astype(bf)
    logits = (jnp.dot(h_hi, w_hi, preferred_element_type=jnp.float32)
              + jnp.dot(h_lo, w_hi, preferred_element_type=jnp.float32)
              + jnp.dot(h_hi, w_lo, preferred_element_type=jnp.float32)) + rb_ref[...]
    lane_f = lane.astype(jnp.float32)
    vals, idxs = [], []
    for _ in range(TOP_K):
        mk = jnp.max(logits, axis=1, keepdims=True)
        ik = jnp.min(jnp.where(logits == mk, lane_f, float(LANES)), axis=1, keepdims=True)
        logits = jnp.where(lane_f == ik, -jnp.inf, logits)
        vals.append(mk)
        idxs.append(ik)
    ex = [jnp.exp(v - vals[0]) for v in vals]
    den = ex[0] + ex[1] + ex[2] + ex[3]
    ti = jnp.zeros((tm, LANES), jnp.float32)
    tg = jnp.zeros((tm, LANES), jnp.float32)
    for kk in range(TOP_K):
        ti = jnp.where(lane == kk, idxs[kk], ti)
        tg = jnp.where(lane == kk, ex[kk] / den, tg)
    ti_ref[0] = ti.astype(jnp.int32)
    tg_ref[0] = tg


def _out_proj(o_pats, st_pats, og, x, mod3, w_out, g_post, g_pre, rw, rb):
    B, S, D = x.shape
    tok = lambda w: pl.BlockSpec((1, TM_OUT, w), lambda b, i: (b, i, 0))
    const = lambda shape: pl.BlockSpec(shape, lambda b, i: tuple(0 for _ in shape))
    out_shapes = [jax.ShapeDtypeStruct((B, S, D), jnp.float32),
                  jax.ShapeDtypeStruct((B, S, D // 2), jnp.uint32),
                  jax.ShapeDtypeStruct((B, S, LANES), jnp.int32),
                  jax.ShapeDtypeStruct((B, S, LANES), jnp.float32)]
    return pl.pallas_call(
        _out_proj_kernel,
        name="out_proj",
        grid=(B, S // TM_OUT),
        in_specs=[pl.BlockSpec((1, TM_OUT // d, d * ATT_WIDTH), lambda b, i: (b, i, 0))
                  for d in DILATIONS]
                 + [pl.BlockSpec((1, TM_OUT // d, d * LANES), lambda b, i: (b, i, 0))
                    for d in DILATIONS]
                 + [tok(GLA_VAL_WIDTH), tok(D),
                  pl.BlockSpec((1, N_MOD, D), lambda b, i: (b, 0, 0)),
                  const((D, D)), const((1, D)), const((1, D)),
                  const((D, LANES)), const((1, LANES))],
        out_specs=[tok(D), tok(D // 2), tok(LANES), tok(LANES)],
        out_shape=out_shapes,
        scratch_shapes=[pltpu.VMEM((2, ATT_WIDTH // LANES, TM_OUT, LANES), jnp.float32),
                        pltpu.VMEM((2, TM_OUT, LANES), jnp.float32)],
        compiler_params=pltpu.CompilerParams(dimension_semantics=("arbitrary", "arbitrary"),
                                             vmem_limit_bytes=VMEM_LIMIT),
    )(*o_pats, *st_pats, og, x, mod3, w_out, g_post, g_pre, rw, rb)


def _split_gate_up_kernel(w_ref, wg_ref, wl_ref):
    group = 2 * LANES
    src = lax.broadcasted_iota(jnp.int32, (group, group), 0)
    dst = lax.broadcasted_iota(jnp.int32, (group, group), 1)
    want = jnp.where(dst < LANES, 2 * dst, 2 * (dst - LANES) + 1)
    perm = jnp.where(src == want, 1.0, 0.0).astype(jnp.bfloat16)
    for j in range(w_ref.shape[1] // group):
        t = jnp.dot(w_ref[:, j * group:(j + 1) * group].astype(jnp.bfloat16), perm,
                    preferred_element_type=jnp.float32)
        wg_ref[:, j * LANES:(j + 1) * LANES] = t[:, :LANES].astype(wg_ref.dtype)
        wl_ref[:, j * LANES:(j + 1) * LANES] = t[:, LANES:].astype(wl_ref.dtype)


def _split_gate_up(w_gate_up):
    E, K, N2 = w_gate_up.shape
    out = jax.ShapeDtypeStruct((E, K, N2 // 2), jnp.bfloat16)
    ospec = pl.BlockSpec((None, TK_SPLIT, N2 // 2), lambda e, i: (e, i, 0))
    return pl.pallas_call(
        _split_gate_up_kernel,
        name="split_gate_up",
        grid=(E, K // TK_SPLIT),
        in_specs=[pl.BlockSpec((None, TK_SPLIT, N2), lambda e, i: (e, i, 0))],
        out_specs=[ospec, ospec],
        out_shape=[out, out],
        compiler_params=pltpu.CompilerParams(dimension_semantics=("arbitrary", "arbitrary"),
                                             vmem_limit_bytes=VMEM_LIMIT),
    )(w_gate_up)


def _moe_kernel(be_ref, nu_ref, x_ref, wg_ref, wl_ref, bg_ref, bl_ref, wd_ref, bd_ref, o_ref):
    i = pl.program_id(0)

    @pl.when(i < nu_ref[0])
    def _():
        lo, hi = _unpack_bf16_pairs(x_ref[...])
        x = jnp.concatenate([lo, hi], axis=1).astype(jnp.bfloat16)
        xg = jnp.dot(x, wg_ref[...], preferred_element_type=jnp.float32) + bg_ref[...]
        xl = jnp.dot(x, wl_ref[...], preferred_element_type=jnp.float32) + bl_ref[...]
        xg = jnp.minimum(xg, SWIGLU_LIMIT)
        xl = jnp.clip(xl, -SWIGLU_LIMIT, SWIGLU_LIMIT)
        act = xg * (1.0 / (1.0 + jnp.exp(-SWIGLU_ALPHA * xg))) * (xl + 1.0)
        out = jnp.dot(act.astype(jnp.bfloat16), wd_ref[...],
                      preferred_element_type=jnp.float32) + bd_ref[...]
        o_ref[...] = _pack_bf16_pairs(out)

    @pl.when(i >= nu_ref[0])
    def _():
        o_ref[...] = jnp.zeros_like(o_ref)


def _moe(xs, blk_e, n_used, wg, wl, bg, bl, wd, bd):
    P, half = xs.shape
    D = 2 * half
    n_blocks = P // TM_MOE
    wspec = lambda k, n: pl.BlockSpec((None, k, n), lambda i, be, nu: (be[i], 0, 0))
    grid_spec = pltpu.PrefetchScalarGridSpec(
        num_scalar_prefetch=2,
        grid=(n_blocks,),
        in_specs=[pl.BlockSpec((TM_MOE, half), lambda i, be, nu: (i, 0)),
                  wspec(D, D_FF), wspec(D, D_FF), wspec(1, D_FF), wspec(1, D_FF),
                  wspec(D_FF, D), wspec(1, D)],
        out_specs=pl.BlockSpec((TM_MOE, half), lambda i, be, nu: (i, 0)),
    )
    return pl.pallas_call(
        _moe_kernel,
        name="moe",
        grid_spec=grid_spec,
        out_shape=jax.ShapeDtypeStruct((P, half), jnp.uint32),
        compiler_params=pltpu.CompilerParams(dimension_semantics=("arbitrary",),
                                             vmem_limit_bytes=VMEM_LIMIT),
    )(blk_e, n_used, xs, wg, wl, bg, bl, wd, bd)


def _final_kernel(x1_ref, y0_ref, y1_ref, y2_ref, y3_ref, tg_ref, mod_ref, g_ref, o_ref):
    gates = tg_ref[...]
    half = y0_ref.shape[1]
    y_lo = jnp.zeros((x1_ref.shape[0], half), jnp.float32)
    y_hi = jnp.zeros((x1_ref.shape[0], half), jnp.float32)
    for kk, yk_ref in enumerate((y0_ref, y1_ref, y2_ref, y3_ref)):
        lo, hi = _unpack_bf16_pairs(yk_ref[...])
        y_lo = y_lo + gates[:, kk:kk + 1] * lo
        y_hi = y_hi + gates[:, kk:kk + 1] * hi
    y = jnp.concatenate([y_lo, y_hi], axis=1)
    gate2 = mod_ref[0, 5:6, :]
    o_ref[...] = x1_ref[...] + gate2 * _rms(y, g_ref[...])


def _final_part_kernel(x1_ref, y0_ref, y1_ref, y2_ref, y3_ref, tg_ref, mod_ref, g_ref, prev_ref,
                       o_ref):
    del prev_ref
    _final_kernel(x1_ref, y0_ref, y1_ref, y2_ref, y3_ref, tg_ref, mod_ref, g_ref, o_ref)


def _final(x1, yk, tg, mod3, g_post, seq_len, part, prev):
    T, D = x1.shape
    nt = seq_len // TM_FIN
    t0 = part * nt
    yspec = lambda kk: pl.BlockSpec((TM_FIN, D // 2), lambda i: (kk * nt + i, 0))
    in_specs = ([pl.BlockSpec((TM_FIN, D), lambda i: (t0 + i, 0))]
                + [yspec(kk) for kk in range(TOP_K)]
                + [pl.BlockSpec((TM_FIN, LANES), lambda i: (t0 + i, 0)),
                   pl.BlockSpec((1, N_MOD, D), lambda i: (part, 0, 0)),
                   pl.BlockSpec((1, D), lambda i: (0, 0))])
    args = [x1, yk, yk, yk, yk, tg, mod3, g_post]
    body, aliases = _final_kernel, {}
    if prev is not None:
        in_specs.append(pl.BlockSpec(memory_space=pl.ANY))
        args.append(prev)
        body, aliases = _final_part_kernel, {len(args) - 1: 0}
    return pl.pallas_call(
        body,
        name="final",
        grid=(nt,),
        in_specs=in_specs,
        out_specs=pl.BlockSpec((TM_FIN, D), lambda i: (t0 + i, 0)),
        out_shape=jax.ShapeDtypeStruct((T, D), jnp.float32),
        input_output_aliases=aliases,
        compiler_params=pltpu.CompilerParams(dimension_semantics=("arbitrary",),
                                             vmem_limit_bytes=VMEM_LIMIT),
    )(*args)


def _pack_w_in(w_in):
    n = np.arange(LANES)
    src = np.where(n < 32, n, np.where(n < 64, n + 32, np.where(n < 96, n - 32, n)))
    perm = (np.arange(ATT_WIDTH // LANES)[:, None] * LANES + src[None, :]).reshape(-1)
    lr = jnp.pad(w_in[:, 3072:3072 + GLA_GATE_RANK], ((0, 0), (0, LANES - GLA_GATE_RANK)))
    packed = jnp.concatenate([w_in[:, perm], w_in[:, ATT_WIDTH + perm], w_in[:, 1024:3072], lr],
                             axis=1)
    return packed.astype(jnp.bfloat16)


def _rope_tables(seq_len):
    half = ATT_HEAD_DIM // 2
    inv_freq = ROPE_THETA ** (-jnp.arange(half, dtype=jnp.float32) / half)
    ang = jnp.arange(seq_len, dtype=jnp.float32)[:, None] * inv_freq[None, :]
    cos = jnp.tile(jnp.cos(ang), (1, LANES // half))
    sin = jnp.tile(jnp.sin(ang), (1, LANES // half))
    sign = jnp.where(jnp.arange(LANES) < LANES // 2, -1.0, 1.0)
    return cos, sin * sign


def _route_kernel(ti_ref, pos_ref, blk_ref, nu_ref, cnt_ref, base_ref, tri_ref):
    phase = pl.program_id(0)
    j = pl.program_id(1)
    tm = ti_ref.shape[0]
    lane = lax.broadcasted_iota(jnp.int32, (tm, LANES), 1)
    ti = ti_ref[...]
    hots = [lane == ti[:, k:k + 1] for k in range(TOP_K)]
    hot_all = jnp.zeros((tm, LANES), jnp.float32)
    for h in hots:
        hot_all = hot_all + jnp.where(h, 1.0, 0.0)
    tile_cnt = jnp.sum(hot_all, axis=0, keepdims=True)

    @pl.when((phase == 0) & (j == 0))
    def _():
        cnt_ref[...] = jnp.zeros_like(cnt_ref)
        r = lax.broadcasted_iota(jnp.int32, (tm, tm), 0)
        c = lax.broadcasted_iota(jnp.int32, (tm, tm), 1)
        tri_ref[...] = jnp.where(c < r, 1.0, 0.0).astype(tri_ref.dtype)

    @pl.when(phase == 0)
    def _():
        cnt_ref[...] += tile_cnt

    @pl.when((phase == 1) & (j == 0))
    def _():
        shift = TM_MOE.bit_length() - 1
        lane8 = lax.broadcasted_iota(jnp.int32, cnt_ref.shape, 1)
        cnt = cnt_ref[...].astype(jnp.int32)
        padded = ((cnt + (TM_MOE - 1)) >> shift) << shift
        pend = padded
        s = 1
        while s < N_EXPERTS:
            pend = pend + jnp.where(lane8 >= s, pltpu.roll(pend, s, axis=1), 0)
            s *= 2
        base_ref[...] = (pend - padded).astype(jnp.float32)
        nb = blk_ref.shape[0]
        blk_start = lax.broadcasted_iota(jnp.int32, (nb, LANES), 0) * TM_MOE
        lane_b = lax.broadcasted_iota(jnp.int32, (nb, LANES), 1)
        done = jnp.where((pend[0:1, :] <= blk_start) & (lane_b < N_EXPERTS), 1.0, 0.0)
        blk = jnp.minimum(jnp.sum(done, axis=1, keepdims=True), float(N_EXPERTS - 1))
        blk_ref[...] = jnp.broadcast_to(blk, blk_ref.shape).astype(jnp.int32)
        total = jnp.max(jnp.where(lane8 < N_EXPERTS, pend, 0).astype(jnp.float32),
                        axis=1, keepdims=True)
        nu_ref[...] = jnp.broadcast_to(total.astype(jnp.int32) >> shift, nu_ref.shape)

    @pl.when(phase == 1)
    def _():
        before = jnp.dot(tri_ref[...], hot_all.astype(tri_ref.dtype),
                         preferred_element_type=jnp.float32)
        slot = before + base_ref[0:1, :]
        pos = jnp.zeros((tm, LANES), jnp.float32)
        for k, h in enumerate(hots):
            pk = jnp.sum(jnp.where(h, slot, 0.0), axis=1, keepdims=True)
            pos = jnp.where(lane == k, pk, pos)
        pos_ref[...] = pos.astype(jnp.int32)
        base_ref[...] += tile_cnt


def _route(ti, n_tokens):
    A = n_tokens * TOP_K
    n_blocks = -(-(A + N_EXPERTS * (TM_MOE - 1)) // TM_MOE)
    nb_pad = -(-n_blocks // 8) * 8
    nt = n_tokens // TM_ROUTE
    pos, blk, nu = pl.pallas_call(
        _route_kernel,
        name="route",
        grid=(2, nt),
        in_specs=[pl.BlockSpec((TM_ROUTE, LANES), lambda p, j: (j, 0))],
        out_specs=[pl.BlockSpec((TM_ROUTE, LANES), lambda p, j: (j * p, 0)),
                   pl.BlockSpec((nb_pad, LANES), lambda p, j: (0, 0)),
                   pl.BlockSpec((8, LANES), lambda p, j: (0, 0))],
        out_shape=[jax.ShapeDtypeStruct((n_tokens, LANES), jnp.int32),
                   jax.ShapeDtypeStruct((nb_pad, LANES), jnp.int32),
                   jax.ShapeDtypeStruct((8, LANES), jnp.int32)],
        scratch_shapes=[pltpu.VMEM((8, LANES), jnp.float32),
                        pltpu.VMEM((8, LANES), jnp.float32),
                        pltpu.VMEM((TM_ROUTE, TM_ROUTE), jnp.bfloat16)],
        compiler_params=pltpu.CompilerParams(dimension_semantics=("arbitrary", "arbitrary"),
                                             vmem_limit_bytes=VMEM_LIMIT),
    )(ti)
    pos_kmajor = pos[:, :TOP_K].T
    return pos_kmajor, n_blocks * TM_MOE, blk[:n_blocks, 0], nu[0, :1]


def kernel(x, c, w_mod, b_mod, g_pre_mix, w_in, w_gate_lr, b_gate, g_gla, w_out, g_post_mix,
           g_pre_ffn, router_w, router_b, w_gate_up, b_gate_up, w_down, b_down, g_post_ffn):
    B, S, D = x.shape
    T = B * S
    bf = jnp.bfloat16
    cos_t, sin_t = _rope_tables(S)
    for l in range(w_mod.shape[0]):
        wg, wl = _split_gate_up(w_gate_up[l])
        wd = w_down[l].astype(bf)
        mod3 = _mod(c, w_mod[l], b_mod[l]).reshape(B, N_MOD, D)
        wlr = jnp.pad(w_gate_lr[l], ((0, LANES - GLA_GATE_RANK), (0, 0))).astype(bf)
        proj = _in_proj(x, mod3, g_pre_mix[l][None], _pack_w_in(w_in[l]), cos_t, sin_t, wlr,
                        b_gate[l][None])
        n_pat = len(DILATIONS)
        qa, ka, va = proj[:n_pat], proj[n_pat:2 * n_pat], proj[2 * n_pat:3 * n_pat]
        qg, kg, vg, sr, la = proj[3 * n_pat:]
        pats = [_attn_pattern(qa[j], ka[j], va[j], d) for j, d in enumerate(DILATIONS)]
        og = _gla(qg, kg, vg, la, sr, g_gla[l][None])
        rw = jnp.pad(router_w[l], ((0, 0), (0, LANES - N_EXPERTS)))
        rb = jnp.pad(router_b[l], (0, LANES - N_EXPERTS), constant_values=NEG_BIG)[None]
        x1, h2, ti, tg = _out_proj([p[0] for p in pats], [p[1] for p in pats], og, x, mod3,
                                   w_out[l].astype(bf), g_post_mix[l][None], g_pre_ffn[l][None],
                                   rw, rb)
        pos_kmajor, n_slots, blk_e, n_used = _route(ti.reshape(T, LANES), T)
        xs = _scatter_rows(h2.reshape(T, D // 2), pos_kmajor, n_slots)
        out_buf = _moe(xs, blk_e, n_used, wg, wl,
                       b_gate_up[l][:, None, 0::2], b_gate_up[l][:, None, 1::2],
                       wd, b_down[l][:, None, :])
        out = None
        for b in range(B):
            yk = _gather_rows(out_buf, pos_kmajor[:, b * S:(b + 1) * S].reshape(-1))
            out = _final(x1.reshape(T, D), yk, tg.reshape(T, LANES), mod3, g_post_ffn[l][None],
                         S, b, out)
        x = out.reshape(B, S, D)
    return x
```

```python
import functools

import numpy as np
import jax
import jax.numpy as jnp
from jax import lax
from jax.experimental import pallas as pl
from jax.experimental.pallas import tpu as pltpu
from jax.experimental.pallas import tpu_sc as plsc

D_MODEL = 1024
ATT_HEADS = 8
ATT_HEAD_DIM = 64
ATT_WIDTH = ATT_HEADS * ATT_HEAD_DIM
DILATIONS = (1, 4, 16)
ATT_SPAN = 128
ROPE_THETA = 10000.0
GLA_HEADS = 4
GLA_KEY_DIM = 64
GLA_VAL_DIM = 128
GLA_KEY_WIDTH = GLA_HEADS * GLA_KEY_DIM
GLA_VAL_WIDTH = GLA_HEADS * GLA_VAL_DIM
GLA_GATE_RANK = 16
GLA_GATE_NORMALIZER = 16.0
GLA_CHUNK = 64
N_EXPERTS = 32
TOP_K = 4
D_FF = D_MODEL
SWIGLU_LIMIT = 7.0
SWIGLU_ALPHA = 1.702
NORM_EPS = 1e-6
N_MOD = 6

LANES = 128
NEG_BIG = -1e30

TM_PROJ = 512
TQ_ATT = 512
TG_GLA = 512
TM_OUT = 512
TM_MOE = 512
TM_FIN = 256
TN_MOD = 512
TK_SPLIT = 512
TM_ROUTE = 512
SC_INDEX_WINDOW = 128
SC_GATHER_WINDOW = 64
VMEM_LIMIT = 48 * 1024 * 1024

_C_QA, _C_KA, _C_VA = 0, 512, 1024
_C_QG, _C_KG, _C_VG, _C_RG, _C_LR = 1536, 1792, 2048, 2560, 3072
D_IN_PACKED = 3200


def _rms(x, g):
    return x * lax.rsqrt(jnp.mean(x * x, axis=-1, keepdims=True) + NORM_EPS) * g


def _pack_bf16_pairs(x):
    n = x.shape[1] // 2
    u = lax.bitcast_convert_type(x.astype(jnp.bfloat16).astype(jnp.float32), jnp.uint32)
    return (u[:, :n] >> 16) | (u[:, n:] & jnp.uint32(0xFFFF0000))


def _unpack_bf16_pairs(w):
    lo = lax.bitcast_convert_type(w << 16, jnp.float32)
    hi = lax.bitcast_convert_type(w & jnp.uint32(0xFFFF0000), jnp.float32)
    return lo, hi


def _gather_rows(data, idx):
    n_rows = idx.shape[0]
    width = data.shape[1]
    mesh = plsc.VectorSubcoreMesh(core_axis_name="core", subcore_axis_name="subcore")
    n_workers = mesh.num_cores * mesh.num_subcores
    per_worker = n_rows // n_workers
    assert per_worker * n_workers == n_rows and per_worker % SC_INDEX_WINDOW == 0
    halves = SC_INDEX_WINDOW // SC_GATHER_WINDOW

    @pl.kernel(out_type=jax.ShapeDtypeStruct((n_rows, width), data.dtype), mesh=mesh,
               name="gather_rows",
               scratch_types=[pltpu.VMEM((1, SC_INDEX_WINDOW), jnp.int32),
                              pltpu.VMEM((SC_GATHER_WINDOW, width), data.dtype)])
    def gather(x_hbm, i_hbm, o_hbm, idx_vmem, rows_vmem):
        worker = lax.axis_index("core") * mesh.num_subcores + lax.axis_index("subcore")
        base = worker * per_worker

        @pl.loop(0, per_worker // SC_INDEX_WINDOW)
        def _(j):
            off = base + j * SC_INDEX_WINDOW
            pltpu.sync_copy(i_hbm.at[:, pl.ds(off, SC_INDEX_WINDOW)], idx_vmem)
            for h in range(halves):
                part = idx_vmem.at[0, pl.ds(h * SC_GATHER_WINDOW, SC_GATHER_WINDOW)]
                pltpu.sync_copy(x_hbm.at[part], rows_vmem)
                pltpu.sync_copy(rows_vmem,
                                o_hbm.at[pl.ds(off + h * SC_GATHER_WINDOW, SC_GATHER_WINDOW)])

    return gather(data, idx.reshape(1, n_rows))


def _scatter_rows(data, idx, n_out):
    n_copies, n_rows = idx.shape
    width = data.shape[1]
    mesh = plsc.VectorSubcoreMesh(core_axis_name="core", subcore_axis_name="subcore")
    n_workers = mesh.num_cores * mesh.num_subcores
    per_worker = n_rows // n_workers
    assert per_worker * n_workers == n_rows and per_worker % SC_INDEX_WINDOW == 0
    halves = SC_INDEX_WINDOW // SC_GATHER_WINDOW

    @pl.kernel(out_type=jax.ShapeDtypeStruct((n_out, width), data.dtype), mesh=mesh,
               name="scatter_rows",
               scratch_types=[pltpu.VMEM((n_copies, SC_INDEX_WINDOW), jnp.int32),
                              pltpu.VMEM((SC_GATHER_WINDOW, width), data.dtype)])
    def scatter(x_hbm, i_hbm, o_hbm, idx_vmem, rows_vmem):
        worker = lax.axis_index("core") * mesh.num_subcores + lax.axis_index("subcore")
        base = worker * per_worker

        @pl.loop(0, per_worker // SC_INDEX_WINDOW)
        def _(j):
            off = base + j * SC_INDEX_WINDOW
            pltpu.sync_copy(i_hbm.at[:, pl.ds(off, SC_INDEX_WINDOW)], idx_vmem)
            for h in range(halves):
                pltpu.sync_copy(x_hbm.at[pl.ds(off + h * SC_GATHER_WINDOW, SC_GATHER_WINDOW)],
                                rows_vmem)
                for k in range(n_copies):
                    part = idx_vmem.at[k, pl.ds(h * SC_GATHER_WINDOW, SC_GATHER_WINDOW)]
                    pltpu.sync_copy(rows_vmem, o_hbm.at[part])

    return scatter(data, idx)


def _mod_kernel(ct_ref, w_ref, b_ref, o_ref):
    ct = ct_ref[...]
    s = ct * (1.0 / (1.0 + jnp.exp(-ct)))
    w = w_ref[...]
    rows = [jnp.sum(s[:, b:b + 1] * w, axis=0, keepdims=True) for b in range(ct.shape[1])]
    o_ref[...] = jnp.concatenate(rows, axis=0) + b_ref[...]


def _mod(c, w_mod, b_mod):
    B = c.shape[0]
    n = w_mod.shape[1]
    return pl.pallas_call(
        _mod_kernel,
        name="mod",
        grid=(n // TN_MOD,),
        in_specs=[pl.BlockSpec((D_MODEL, B), lambda j: (0, 0)),
                  pl.BlockSpec((D_MODEL, TN_MOD), lambda j: (0, j)),
                  pl.BlockSpec((1, TN_MOD), lambda j: (0, j))],
        out_specs=pl.BlockSpec((B, TN_MOD), lambda j: (0, j)),
        out_shape=jax.ShapeDtypeStruct((B, n), jnp.float32),
        compiler_params=pltpu.CompilerParams(dimension_semantics=("arbitrary",),
                                             vmem_limit_bytes=VMEM_LIMIT),
    )(c.T, w_mod, b_mod.reshape(1, n))


def _store_residue_views(scr_ref, out_refs):
    ns, tm, _ = scr_ref.shape
    w = ns * LANES
    for d, ref in zip(DILATIONS, out_refs):
        for r in range(d):
            for s in range(ns):
                c0 = r * w + s * LANES
                ref[0, :, c0:c0 + LANES] = scr_ref[s, pl.ds(r, tm // d, stride=d), :].astype(ref.dtype)


def _in_proj_kernel(x_ref, mod_ref, g_ref, w_ref, cos_ref, sin_ref, wlr_ref, bg_ref,
                    qa1_ref, qa4_ref, qa16_ref, ka1_ref, ka4_ref, ka16_ref,
                    va1_ref, va4_ref, va16_ref, qg_ref, kg_ref, vg_ref, sr_ref, la_ref,
                    qs_ref, ks_ref, vs_ref):
    x = x_ref[0]
    shift = mod_ref[0, 0:1, :]
    scale = mod_ref[0, 1:2, :]
    h = (_rms(x, g_ref[...]) * (1.0 + scale) + shift).astype(jnp.bfloat16)

    def proj(c0, width):
        return jnp.dot(h, w_ref[:, c0:c0 + width], preferred_element_type=jnp.float32)

    cos = cos_ref[...]
    sin = sin_ref[...]

    def rope(c0, scr_ref, mult):
        p = proj(c0, ATT_WIDTH)
        for s in range(ATT_WIDTH // LANES):
            t = p[:, s * LANES:(s + 1) * LANES]
            r = t * cos + pltpu.roll(t, LANES // 2, axis=1) * sin
            if mult != 1.0:
                r = r * mult
            scr_ref[s] = r

    rope(_C_QA, qs_ref, ATT_HEAD_DIM ** -0.5)
    _store_residue_views(qs_ref, (qa1_ref, qa4_ref, qa16_ref))
    rope(_C_KA, ks_ref, 1.0)
    _store_residue_views(ks_ref, (ka1_ref, ka4_ref, ka16_ref))
    pv = proj(_C_VA, ATT_WIDTH)
    for s in range(ATT_WIDTH // LANES):
        vs_ref[s] = pv[:, s * LANES:(s + 1) * LANES]
    _store_residue_views(vs_ref, (va1_ref, va4_ref, va16_ref))
    qg_ref[0] = proj(_C_QG, GLA_KEY_WIDTH).astype(qg_ref.dtype)
    kg_ref[0] = proj(_C_KG, GLA_KEY_WIDTH).astype(kg_ref.dtype)
    vg_ref[0] = proj(_C_VG, GLA_VAL_WIDTH).astype(vg_ref.dtype)
    r = proj(_C_RG, GLA_VAL_WIDTH)
    sr_ref[0] = (r * (1.0 / (1.0 + jnp.exp(-r)))).astype(sr_ref.dtype)
    lr = proj(_C_LR, LANES).astype(jnp.bfloat16)
    z = jnp.dot(lr, wlr_ref[...], preferred_element_type=jnp.float32) + bg_ref[...]
    log_sig = jnp.minimum(z, 0.0) - jnp.log(1.0 + jnp.exp(-jnp.abs(z)))
    la_ref[0] = log_sig * (1.0 / GLA_GATE_NORMALIZER)


def _in_proj(x, mod3, g_pre, w_packed, cos_t, sin_t, wlr, bg):
    B, S, D = x.shape
    nb = S // TM_PROJ
    tok = lambda w: pl.BlockSpec((1, TM_PROJ, w), lambda b, i: (b, i, 0))
    const = lambda shape: pl.BlockSpec(shape, lambda b, i: tuple(0 for _ in shape))
    bf = jnp.bfloat16
    att_shapes = [jax.ShapeDtypeStruct((B, S // d, d * ATT_WIDTH), bf) for d in DILATIONS] * 3
    att_specs = [pl.BlockSpec((1, TM_PROJ // d, d * ATT_WIDTH), lambda b, i: (b, i, 0))
                 for d in DILATIONS] * 3
    gla_shapes = [jax.ShapeDtypeStruct((B, S, w), dt) for w, dt in
                  [(GLA_KEY_WIDTH, bf), (GLA_KEY_WIDTH, bf),
                   (GLA_VAL_WIDTH, bf), (GLA_VAL_WIDTH, bf), (GLA_KEY_WIDTH, jnp.float32)]]
    return pl.pallas_call(
        _in_proj_kernel,
        name="in_proj",
        grid=(B, nb),
        in_specs=[tok(D),
                  pl.BlockSpec((1, N_MOD, D), lambda b, i: (b, 0, 0)),
                  const((1, D)),
                  const((D, D_IN_PACKED)),
                  pl.BlockSpec((TM_PROJ, LANES), lambda b, i: (i, 0)),
                  pl.BlockSpec((TM_PROJ, LANES), lambda b, i: (i, 0)),
                  const((LANES, GLA_KEY_WIDTH)),
                  const((1, GLA_KEY_WIDTH))],
        out_specs=att_specs + [tok(s.shape[-1]) for s in gla_shapes],
        out_shape=att_shapes + gla_shapes,
        scratch_shapes=[pltpu.VMEM((ATT_WIDTH // LANES, TM_PROJ, LANES), jnp.float32)] * 3,
        compiler_params=pltpu.CompilerParams(dimension_semantics=("arbitrary", "arbitrary"),
                                             vmem_limit_bytes=VMEM_LIMIT),
    )(x, mod3, g_pre, w_packed, cos_t, sin_t, wlr, bg)


def _attn_kernel(q_ref, kp_ref, kc_ref, vp_ref, vc_ref, o_ref, st_ref):
    i = pl.program_id(2)
    tp = kp_ref.shape[1]
    tq = tp
    n_sub = q_ref.shape[1] // tq
    row = lax.broadcasted_iota(jnp.int32, (tq, tp + tq), 0)
    col = lax.broadcasted_iota(jnp.int32, (tq, tp + tq), 1)
    band = (col >= row) & (col <= row + ATT_SPAN)
    bias_inner = jnp.where(band, 0.0, NEG_BIG)
    bias_first = jnp.where(band & ((col >= tp) | (i > 0)), 0.0, NEG_BIG)
    lane = lax.broadcasted_iota(jnp.int32, (tq, LANES), 1)
    q_head = (lane >> 5) & 1
    for sb in range(n_sub):
        rows = slice(sb * tq, (sb + 1) * tq)
        bias = bias_first if sb == 0 else bias_inner
        stats = jnp.zeros((tq, LANES), jnp.float32)
        for hp in range(ATT_HEADS // 2):
            sl = slice(hp * LANES, (hp + 1) * LANES)
            q2 = q_ref[0, rows, sl]
            if sb == 0:
                kcat = jnp.concatenate([kp_ref[0, :, sl], kc_ref[0, rows, sl]], axis=0)
                vcat = jnp.concatenate([vp_ref[0, :, sl], vc_ref[0, rows, sl]], axis=0)
            else:
                kcat = kc_ref[0, (sb - 1) * tq:(sb + 1) * tq, sl]
                vcat = vc_ref[0, (sb - 1) * tq:(sb + 1) * tq, sl]
            outs = []
            for hh in range(2):
                qm = jnp.where(q_head == hh, q2, jnp.zeros_like(q2))
                s = lax.dot_general(qm, kcat, (((1,), (1,)), ((), ())),
                                    preferred_element_type=jnp.float32) + bias
                m = jnp.max(s, axis=1, keepdims=True)
                p = jnp.exp(s - m)
                l = jnp.sum(p, axis=1, keepdims=True)
                o = jnp.dot(p.astype(vcat.dtype), vcat, preferred_element_type=jnp.float32)
                outs.append(o / l)
                stats = jnp.where(lane == 2 * hp + hh, m + jnp.log(l), stats)
            o_ref[0, rows, sl] = jnp.where(lane < ATT_HEAD_DIM, outs[0], outs[1]).astype(o_ref.dtype)
        st_ref[0, rows, :] = stats


def _attn_pattern(qv, kv, vv, dil):
    B, L, _ = qv.shape
    W = ATT_WIDTH
    nq = L // TQ_ATT
    cur = pl.BlockSpec((1, TQ_ATT, W), lambda b, r, i: (b, i, r))
    back = TQ_ATT // ATT_SPAN
    prev = pl.BlockSpec((1, ATT_SPAN, W), lambda b, r, i: (b, jnp.maximum(i * back - 1, 0), r))
    o, st = pl.pallas_call(
        _attn_kernel,
        name=f"attn_d{dil}",
        grid=(B, dil, nq),
        in_specs=[cur, prev, cur, prev, cur],
        out_specs=[cur, pl.BlockSpec((1, TQ_ATT, LANES), lambda b, r, i: (b, i, r))],
        out_shape=[jax.ShapeDtypeStruct((B, L, dil * W), jnp.bfloat16),
                   jax.ShapeDtypeStruct((B, L, dil * LANES), jnp.float32)],
        compiler_params=pltpu.CompilerParams(
            dimension_semantics=("arbitrary", "arbitrary", "arbitrary"),
            vmem_limit_bytes=VMEM_LIMIT),
    )(qv, kv, kv, vv, vv)
    return o, st


def _cumsum_rows(x):
    n = x.shape[0]
    row = lax.broadcasted_iota(jnp.int32, x.shape, 0)
    s = 1
    while s < n:
        x = x + jnp.where(row >= s, pltpu.roll(x, s, axis=0), 0.0)
        s *= 2
    return x


def _gla_kernel(q_ref, k_ref, v_ref, la_ref, sr_ref, g_ref, o_ref, st_ref):
    @pl.when(pl.program_id(1) == 0)
    def _():
        st_ref[...] = jnp.zeros_like(st_ref)

    C = GLA_CHUNK
    n_chunks = q_ref.shape[1] // C
    lane = lax.broadcasted_iota(jnp.int32, (C, LANES), 1)
    lane_sq = lax.broadcasted_iota(jnp.int32, (LANES, LANES), 1)
    tril = (lax.broadcasted_iota(jnp.int32, (C, C), 0)
            >= lax.broadcasted_iota(jnp.int32, (C, C), 1))
    g = g_ref[...]
    bf = jnp.bfloat16

    def chunk(c, carry):
        r0 = pl.multiple_of(c * C, C)
        rows = pl.ds(r0, C)
        for p in range(GLA_HEADS // 2):
            ksl = slice(p * LANES, (p + 1) * LANES)
            b = _cumsum_rows(la_ref[0, rows, ksl])
            b_last = b[C - 1:C, :]
            q = q_ref[0, rows, ksl].astype(jnp.float32) * (GLA_KEY_DIM ** -0.5)
            k = k_ref[0, rows, ksl].astype(jnp.float32)
            q_dec = q * jnp.exp(b)
            k_inv = (k * jnp.exp(-b)).astype(bf)
            k_dec = (k * jnp.exp(b_last - b)).astype(bf)
            st = st_ref[p]
            st_b = st.astype(bf)
            ut = []
            for hh in range(2):
                h = 2 * p + hh
                vsl = slice(h * GLA_VAL_DIM, (h + 1) * GLA_VAL_DIM)
                v = v_ref[0, rows, vsl]
                own = (lane >= GLA_KEY_DIM) if hh else (lane < GLA_KEY_DIM)
                qm = jnp.where(own, q_dec, 0.0).astype(bf)
                att = lax.dot_general(qm, k_inv, (((1,), (1,)), ((), ())),
                                      preferred_element_type=jnp.float32)
                att = jnp.where(tril, att, 0.0).astype(bf)
                o = jnp.dot(att, v, preferred_element_type=jnp.float32)
                o = o + lax.dot_general(qm, st_b, (((1,), (1,)), ((), ())),
                                        preferred_element_type=jnp.float32)
                o = _rms(o, g) * sr_ref[0, rows, vsl].astype(jnp.float32)
                o_ref[0, rows, vsl] = o.astype(o_ref.dtype)
                ut.append(lax.dot_general(v, k_dec, (((0,), (0,)), ((), ())),
                                          preferred_element_type=jnp.float32))
            st_ref[p] = st * jnp.exp(b_last) + jnp.where(lane_sq < GLA_KEY_DIM, ut[0], ut[1])
        return carry

    lax.fori_loop(0, n_chunks, chunk, 0, unroll=8)


def _gla(qg, kg, vg, la, sr, g_gla):
    B, S, _ = qg.shape
    tok = lambda w: pl.BlockSpec((1, TG_GLA, w), lambda b, i: (b, i, 0))
    return pl.pallas_call(
        _gla_kernel,
        name="gla",
        grid=(B, S // TG_GLA),
        in_specs=[tok(GLA_KEY_WIDTH), tok(GLA_KEY_WIDTH), tok(GLA_VAL_WIDTH),
                  tok(GLA_KEY_WIDTH), tok(GLA_VAL_WIDTH),
                  pl.BlockSpec((1, GLA_VAL_DIM), lambda b, i: (0, 0))],
        out_specs=tok(GLA_VAL_WIDTH),
        out_shape=jax.ShapeDtypeStruct((B, S, GLA_VAL_WIDTH), jnp.bfloat16),
        scratch_shapes=[pltpu.VMEM((GLA_HEADS // 2, GLA_VAL_DIM, LANES), jnp.float32)],
        compiler_params=pltpu.CompilerParams(dimension_semantics=("arbitrary", "arbitrary"),
                                             vmem_limit_bytes=VMEM_LIMIT),
    )(qg, kg, vg, la, sr, g_gla)


def _out_proj_kernel(o1_ref, o2_ref, o3_ref, s1_ref, s2_ref, s3_ref, og_ref, x_ref, mod_ref,
                     wout_ref, gpost_ref, gpre_ref, rw_ref, rb_ref,
                     x1_ref, h2_ref, ti_ref, tg_ref, oscr_ref, sscr_ref):
    tm = x_ref.shape[1]
    lane = lax.broadcasted_iota(jnp.int32, (tm, LANES), 1)
    for j, (d, o_ref, s_ref) in enumerate(((DILATIONS[1], o2_ref, s2_ref),
                                           (DILATIONS[2], o3_ref, s3_ref))):
        for r in range(d):
            rows = pl.ds(r, tm // d, stride=d)
            for s in range(ATT_WIDTH // LANES):
                c0 = r * ATT_WIDTH + s * LANES
                oscr_ref[j, s, rows, :] = o_ref[0, :, c0:c0 + LANES].astype(jnp.float32)
            sscr_ref[j, rows, :] = s_ref[0, :, r * LANES:(r + 1) * LANES]
    lses = [s1_ref[0], sscr_ref[0], sscr_ref[1]]
    m = jnp.maximum(jnp.maximum(lses[0], lses[1]), lses[2])
    es = [jnp.exp(t - m) for t in lses]
    inv = 1.0 / (es[0] + es[1] + es[2])
    ws = [e * inv for e in es]
    pairs = []
    for hp in range(ATT_HEADS // 2):
        sl = slice(hp * LANES, (hp + 1) * LANES)
        o_pats = [o1_ref[0, :, sl].astype(jnp.float32), oscr_ref[0, hp], oscr_ref[1, hp]]
        acc = jnp.zeros((tm, LANES), jnp.float32)
        head_of_lane = 2 * hp + (lane >> 6)
        for w, o in zip(ws, o_pats):
            acc = acc + jnp.take_along_axis(w, head_of_lane, axis=1) * o
        pairs.append(acc.astype(jnp.bfloat16))
    mixed = jnp.concatenate(pairs + [og_ref[0]], axis=1)
    y = jnp.dot(mixed, wout_ref[...], preferred_element_type=jnp.float32)
    gate1 = mod_ref[0, 2:3, :]
    shift2 = mod_ref[0, 3:4, :]
    scale2 = mod_ref[0, 4:5, :]
    x1 = x_ref[0] + gate1 * _rms(y, gpost_ref[...])
    x1_ref[0] = x1
    h2 = _rms(x1, gpre_ref[...]) * (1.0 + scale2) + shift2
    h2_ref[0] = _pack_bf16_pairs(h2)
    bf = jnp.bfloat16
    h_hi = h2.astype(bf)
    h_lo = (h2 - h_hi.astype(jnp.float32)).astype(bf)
    rw = rw_ref[...]
    w_hi = rw.astype(bf)
    w_lo = (rw - w_hi.astype(jnp.float32)).astype(bf)
    logits = (jnp.dot(h_hi, w_hi, preferred_element_type=jnp.float32)
              + jnp.dot(h_lo, w_hi, preferred_element_type=jnp.float32)
              + jnp.dot(h_hi, w_lo, preferred_element_type=jnp.float32)) + rb_ref[...]
    lane_f = lane.astype(jnp.float32)
    vals, idxs = [], []
    for _ in range(TOP_K):
        mk = jnp.max(logits, axis=1, keepdims=True)
        ik = jnp.min(jnp.where(logits == mk, lane_f, float(LANES)), axis=1, keepdims=True)
        logits = jnp.where(lane_f == ik, -jnp.inf, logits)
        vals.append(mk)
        idxs.append(ik)
    ex = [jnp.exp(v - vals[0]) for v in vals]
    den = ex[0] + ex[1] + ex[2] + ex[3]
    ti = jnp.zeros((tm, LANES), jnp.float32)
    tg = jnp.zeros((tm, LANES), jnp.float32)
    for kk in range(TOP_K):
        ti = jnp.where(lane == kk, idxs[kk], ti)
        tg = jnp.where(lane == kk, ex[kk] / den, tg)
    ti_ref[0] = ti.astype(jnp.int32)
    tg_ref[0] = tg


def _out_proj(o_pats, st_pats, og, x, mod3, w_out, g_post, g_pre, rw, rb):
    B, S, D = x.shape
    tok = lambda w: pl.BlockSpec((1, TM_OUT, w), lambda b, i: (b, i, 0))
    const = lambda shape: pl.BlockSpec(shape, lambda b, i: tuple(0 for _ in shape))
    out_shapes = [jax.ShapeDtypeStruct((B, S, D), jnp.float32),
                  jax.ShapeDtypeStruct((B, S, D // 2), jnp.uint32),
                  jax.ShapeDtypeStruct((B, S, LANES), jnp.int32),
                  jax.ShapeDtypeStruct((B, S, LANES), jnp.float32)]
    return pl.pallas_call(
        _out_proj_kernel,
        name="out_proj",
        grid=(B, S // TM_OUT),
        in_specs=[pl.BlockSpec((1, TM_OUT // d, d * ATT_WIDTH), lambda b, i: (b, i, 0))
                  for d in DILATIONS]
                 + [pl.BlockSpec((1, TM_OUT // d, d * LANES), lambda b, i: (b, i, 0))
                    for d in DILATIONS]
                 + [tok(GLA_VAL_WIDTH), tok(D),
                  pl.BlockSpec((1, N_MOD, D), lambda b, i: (b, 0, 0)),
                  const((D, D)), const((1, D)), const((1, D)),
                  const((D, LANES)), const((1, LANES))],
        out_specs=[tok(D), tok(D // 2), tok(LANES), tok(LANES)],
        out_shape=out_shapes,
        scratch_shapes=[pltpu.VMEM((2, ATT_WIDTH // LANES, TM_OUT, LANES), jnp.float32),
                        pltpu.VMEM((2, TM_OUT, LANES), jnp.float32)],
        compiler_params=pltpu.CompilerParams(dimension_semantics=("arbitrary", "arbitrary"),
                                             vmem_limit_bytes=VMEM_LIMIT),
    )(*o_pats, *st_pats, og, x, mod3, w_out, g_post, g_pre, rw, rb)


def _split_gate_up_kernel(w_ref, wg_ref, wl_ref):
    group = 2 * LANES
    src = lax.broadcasted_iota(jnp.int32, (group, group), 0)
    dst = lax.broadcasted_iota(jnp.int32, (group, group), 1)
    want = jnp.where(dst < LANES, 2 * dst, 2 * (dst - LANES) + 1)
    perm = jnp.where(src == want, 1.0, 0.0).astype(jnp.bfloat16)
    for j in range(w_ref.shape[1] // group):
        t = jnp.dot(w_ref[:, j * group:(j + 1) * group].astype(jnp.bfloat16), perm,
                    preferred_element_type=jnp.float32)
        wg_ref[:, j * LANES:(j + 1) * LANES] = t[:, :LANES].astype(wg_ref.dtype)
        wl_ref[:, j * LANES:(j + 1) * LANES] = t[:, LANES:].astype(wl_ref.dtype)


def _split_gate_up(w_gate_up):
    E, K, N2 = w_gate_up.shape
    out = jax.ShapeDtypeStruct((E, K, N2 // 2), jnp.bfloat16)
    ospec = pl.BlockSpec((None, TK_SPLIT, N2 // 2), lambda e, i: (e, i, 0))
    return pl.pallas_call(
        _split_gate_up_kernel,
        name="split_gate_up",
        grid=(E, K // TK_SPLIT),
        in_specs=[pl.BlockSpec((None, TK_SPLIT, N2), lambda e, i: (e, i, 0))],
        out_specs=[ospec, ospec],
        out_shape=[out, out],
        compiler_params=pltpu.CompilerParams(dimension_semantics=("arbitrary", "arbitrary"),
                                             vmem_limit_bytes=VMEM_LIMIT),
    )(w_gate_up)


def _moe_kernel(be_ref, nu_ref, x_ref, wg_ref, wl_ref, bg_ref, bl_ref, wd_ref, bd_ref, o_ref):
    i = pl.program_id(0)

    @pl.when(i < nu_ref[0])
    def _():
        lo, hi = _unpack_bf16_pairs(x_ref[...])
        x = jnp.concatenate([lo, hi], axis=1).astype(jnp.bfloat16)
        xg = jnp.dot(x, wg_ref[...], preferred_element_type=jnp.float32) + bg_ref[...]
        xl = jnp.dot(x, wl_ref[...], preferred_element_type=jnp.float32) + bl_ref[...]
        xg = jnp.minimum(xg, SWIGLU_LIMIT)
        xl = jnp.clip(xl, -SWIGLU_LIMIT, SWIGLU_LIMIT)
        act = xg * (1.0 / (1.0 + jnp.exp(-SWIGLU_ALPHA * xg))) * (xl + 1.0)
        out = jnp.dot(act.astype(jnp.bfloat16), wd_ref[...],
                      preferred_element_type=jnp.float32) + bd_ref[...]
        o_ref[...] = _pack_bf16_pairs(out)

    @pl.when(i >= nu_ref[0])
    def _():
        o_ref[...] = jnp.zeros_like(o_ref)


def _moe(xs, blk_e, n_used, wg, wl, bg, bl, wd, bd):
    P, half = xs.shape
    D = 2 * half
    n_blocks = P // TM_MOE
    wspec = lambda k, n: pl.BlockSpec((None, k, n), lambda i, be, nu: (be[i], 0, 0))
    grid_spec = pltpu.PrefetchScalarGridSpec(
        num_scalar_prefetch=2,
        grid=(n_blocks,),
        in_specs=[pl.BlockSpec((TM_MOE, half), lambda i, be, nu: (i, 0)),
                  wspec(D, D_FF), wspec(D, D_FF), wspec(1, D_FF), wspec(1, D_FF),
                  wspec(D_FF, D), wspec(1, D)],
        out_specs=pl.BlockSpec((TM_MOE, half), lambda i, be, nu: (i, 0)),
    )
    return pl.pallas_call(
        _moe_kernel,
        name="moe",
        grid_spec=grid_spec,
        out_shape=jax.ShapeDtypeStruct((P, half), jnp.uint32),
        compiler_params=pltpu.CompilerParams(dimension_semantics=("arbitrary",),
                                             vmem_limit_bytes=VMEM_LIMIT),
    )(blk_e, n_used, xs, wg, wl, bg, bl, wd, bd)


def _final_kernel(x1_ref, y0_ref, y1_ref, y2_ref, y3_ref, tg_ref, mod_ref, g_ref, o_ref):
    gates = tg_ref[...]
    half = y0_ref.shape[1]
    y_lo = jnp.zeros((x1_ref.shape[0], half), jnp.float32)
    y_hi = jnp.zeros((x1_ref.shape[0], half), jnp.float32)
    for kk, yk_ref in enumerate((y0_ref, y1_ref, y2_ref, y3_ref)):
        lo, hi = _unpack_bf16_pairs(yk_ref[...])
        y_lo = y_lo + gates[:, kk:kk + 1] * lo
        y_hi = y_hi + gates[:, kk:kk + 1] * hi
    y = jnp.concatenate([y_lo, y_hi], axis=1)
    gate2 = mod_ref[0, 5:6, :]
    o_ref[...] = x1_ref[...] + gate2 * _rms(y, g_ref[...])


def _final_part_kernel(x1_ref, y0_ref, y1_ref, y2_ref, y3_ref, tg_ref, mod_ref, g_ref, prev_ref,
                       o_ref):
    del prev_ref
    _final_kernel(x1_ref, y0_ref, y1_ref, y2_ref, y3_ref, tg_ref, mod_ref, g_ref, o_ref)


def _final(x1, yk, tg, mod3, g_post, seq_len, part, prev):
    T, D = x1.shape
    nt = seq_len // TM_FIN
    t0 = part * nt
    yspec = lambda kk: pl.BlockSpec((TM_FIN, D // 2), lambda i: (kk * nt + i, 0))
    in_specs = ([pl.BlockSpec((TM_FIN, D), lambda i: (t0 + i, 0))]
                + [yspec(kk) for kk in range(TOP_K)]
                + [pl.BlockSpec((TM_FIN, LANES), lambda i: (t0 + i, 0)),
                   pl.BlockSpec((1, N_MOD, D), lambda i: (part, 0, 0)),
                   pl.BlockSpec((1, D), lambda i: (0, 0))])
    args = [x1, yk, yk, yk, yk, tg, mod3, g_post]
    body, aliases = _final_kernel, {}
    if prev is not None:
        in_specs.append(pl.BlockSpec(memory_space=pl.ANY))
        args.append(prev)
        body, aliases = _final_part_kernel, {len(args) - 1: 0}
    return pl.pallas_call(
        body,
        name="final",
        grid=(nt,),
        in_specs=in_specs,
        out_specs=pl.BlockSpec((TM_FIN, D), lambda i: (t0 + i, 0)),
        out_shape=jax.ShapeDtypeStruct((T, D), jnp.float32),
        input_output_aliases=aliases,
        compiler_params=pltpu.CompilerParams(dimension_semantics=("arbitrary",),
                                             vmem_limit_bytes=VMEM_LIMIT),
    )(*args)


def _pack_w_in(w_in):
    n = np.arange(LANES)
    src = np.where(n < 32, n, np.where(n < 64, n + 32, np.where(n < 96, n - 32, n)))
    perm = (np.arange(ATT_WIDTH // LANES)[:, None] * LANES + src[None, :]).reshape(-1)
    lr = jnp.pad(w_in[:, 3072:3072 + GLA_GATE_RANK], ((0, 0), (0, LANES - GLA_GATE_RANK)))
    packed = jnp.concatenate([w_in[:, perm], w_in[:, ATT_WIDTH + perm], w_in[:, 1024:3072], lr],
                             axis=1)
    return packed.astype(jnp.bfloat16)


def _rope_tables(seq_len):
    half = ATT_HEAD_DIM // 2
    inv_freq = ROPE_THETA ** (-jnp.arange(half, dtype=jnp.float32) / half)
    ang = jnp.arange(seq_len, dtype=jnp.float32)[:, None] * inv_freq[None, :]
    cos = jnp.tile(jnp.cos(ang), (1, LANES // half))
    sin = jnp.tile(jnp.sin(ang), (1, LANES // half))
    sign = jnp.where(jnp.arange(LANES) < LANES // 2, -1.0, 1.0)
    return cos, sin * sign


def _route_kernel(ti_ref, pos_ref, blk_ref, nu_ref, cnt_ref, base_ref, tri_ref):
    phase = pl.program_id(0)
    j = pl.program_id(1)
    tm = ti_ref.shape[0]
    lane = lax.broadcasted_iota(jnp.int32, (tm, LANES), 1)
    ti = ti_ref[...]
    hots = [lane == ti[:, k:k + 1] for k in range(TOP_K)]
    hot_all = jnp.zeros((tm, LANES), jnp.float32)
    for h in hots:
        hot_all = hot_all + jnp.where(h, 1.0, 0.0)
    tile_cnt = jnp.sum(hot_all, axis=0, keepdims=True)

    @pl.when((phase == 0) & (j == 0))
    def _():
        cnt_ref[...] = jnp.zeros_like(cnt_ref)
        r = lax.broadcasted_iota(jnp.int32, (tm, tm), 0)
        c = lax.broadcasted_iota(jnp.int32, (tm, tm), 1)
        tri_ref[...] = jnp.where(c < r, 1.0, 0.0).astype(tri_ref.dtype)

    @pl.when(phase == 0)
    def _():
        cnt_ref[...] += tile_cnt

    @pl.when((phase == 1) & (j == 0))
    def _():
        shift = TM_MOE.bit_length() - 1
        lane8 = lax.broadcasted_iota(jnp.int32, cnt_ref.shape, 1)
        cnt = cnt_ref[...].astype(jnp.int32)
        padded = ((cnt + (TM_MOE - 1)) >> shift) << shift
        pend = padded
        s = 1
        while s < N_EXPERTS:
            pend = pend + jnp.where(lane8 >= s, pltpu.roll(pend, s, axis=1), 0)
            s *= 2
        base_ref[...] = (pend - padded).astype(jnp.float32)
        nb = blk_ref.shape[0]
        blk_start = lax.broadcasted_iota(jnp.int32, (nb, LANES), 0) * TM_MOE
        lane_b = lax.broadcasted_iota(jnp.int32, (nb, LANES), 1)
        done = jnp.where((pend[0:1, :] <= blk_start) & (lane_b < N_EXPERTS), 1.0, 0.0)
        blk = jnp.minimum(jnp.sum(done, axis=1, keepdims=True), float(N_EXPERTS - 1))
        blk_ref[...] = jnp.broadcast_to(blk, blk_ref.shape).astype(jnp.int32)
        total = jnp.max(jnp.where(lane8 < N_EXPERTS, pend, 0).astype(jnp.float32),
                        axis=1, keepdims=True)
        nu_ref[...] = jnp.broadcast_to(total.astype(jnp.int32) >> shift, nu_ref.shape)

    @pl.when(phase == 1)
    def _():
        before = jnp.dot(tri_ref[...], hot_all.astype(tri_ref.dtype),
                         preferred_element_type=jnp.float32)
        slot = before + base_ref[0:1, :]
        pos = jnp.zeros((tm, LANES), jnp.float32)
        for k, h in enumerate(hots):
            pk = jnp.sum(jnp.where(h, slot, 0.0), axis=1, keepdims=True)
            pos = jnp.where(lane == k, pk, pos)
        pos_ref[...] = pos.astype(jnp.int32)
        base_ref[...] += tile_cnt


def _route(ti, n_tokens):
    A = n_tokens * TOP_K
    n_blocks = -(-(A + N_EXPERTS * (TM_MOE - 1)) // TM_MOE)
    nb_pad = -(-n_blocks // 8) * 8
    nt = n_tokens // TM_ROUTE
    pos, blk, nu = pl.pallas_call(
        _route_kernel,
        name="route",
        grid=(2, nt),
        in_specs=[pl.BlockSpec((TM_ROUTE, LANES), lambda p, j: (j, 0))],
        out_specs=[pl.BlockSpec((TM_ROUTE, LANES), lambda p, j: (j * p, 0)),
                   pl.BlockSpec((nb_pad, LANES), lambda p, j: (0, 0)),
                   pl.BlockSpec((8, LANES), lambda p, j: (0, 0))],
        out_shape=[jax.ShapeDtypeStruct((n_tokens, LANES), jnp.int32),
                   jax.ShapeDtypeStruct((nb_pad, LANES), jnp.int32),
                   jax.ShapeDtypeStruct((8, LANES), jnp.int32)],
        scratch_shapes=[pltpu.VMEM((8, LANES), jnp.float32),
                        pltpu.VMEM((8, LANES), jnp.float32),
                        pltpu.VMEM((TM_ROUTE, TM_ROUTE), jnp.bfloat16)],
        compiler_params=pltpu.CompilerParams(dimension_semantics=("arbitrary", "arbitrary"),
                                             vmem_limit_bytes=VMEM_LIMIT),
    )(ti)
    pos_kmajor = pos[:, :TOP_K].T
    return pos_kmajor, n_blocks * TM_MOE, blk[:n_blocks, 0], nu[0, :1]


def kernel(x, c, w_mod, b_mod, g_pre_mix, w_in, w_gate_lr, b_gate, g_gla, w_out, g_post_mix,
           g_pre_ffn, router_w, router_b, w_gate_up, b_gate_up, w_down, b_down, g_post_ffn):
    B, S, D = x.shape
    T = B * S
    bf = jnp.bfloat16
    cos_t, sin_t = _rope_tables(S)
    for l in range(w_mod.shape[0]):
        wg, wl = _split_gate_up(w_gate_up[l])
        wd = w_down[l].astype(bf)
        mod3 = _mod(c, w_mod[l], b_mod[l]).reshape(B, N_MOD, D)
        wlr = jnp.pad(w_gate_lr[l], ((0, LANES - GLA_GATE_RANK), (0, 0))).astype(bf)
        proj = _in_proj(x, mod3, g_pre_mix[l][None], _pack_w_in(w_in[l]), cos_t, sin_t, wlr,
                        b_gate[l][None])
        n_pat = len(DILATIONS)
        qa, ka, va = proj[:n_pat], proj[n_pat:2 * n_pat], proj[2 * n_pat:3 * n_pat]
        qg, kg, vg, sr, la = proj[3 * n_pat:]
        pats = [_attn_pattern(qa[j], ka[j], va[j], d) for j, d in enumerate(DILATIONS)]
        og = _gla(qg, kg, vg, la, sr, g_gla[l][None])
        rw = jnp.pad(router_w[l], ((0, 0), (0, LANES - N_EXPERTS)))
        rb = jnp.pad(router_b[l], (0, LANES - N_EXPERTS), constant_values=NEG_BIG)[None]
        x1, h2, ti, tg = _out_proj([p[0] for p in pats], [p[1] for p in pats], og, x, mod3,
                                   w_out[l].astype(bf), g_post_mix[l][None], g_pre_ffn[l][None],
                                   rw, rb)
        pos_kmajor, n_slots, blk_e, n_used = _route(ti.reshape(T, LANES), T)
        xs = _scatter_rows(h2.reshape(T, D // 2), pos_kmajor, n_slots)
        out_buf = _moe(xs, blk_e, n_used, wg, wl,
                       b_gate_up[l][:, None, 0::2], b_gate_up[l][:, None, 1::2],
                       wd, b_down[l][:, None, :])
        out = None
        for b in range(B):
            yk = _gather_rows(out_buf, pos_kmajor[:, b * S:(b + 1) * S].reshape(-1))
            out = _final(x1.reshape(T, D), yk, tg.reshape(T, LANES), mod3, g_post_ffn[l][None],
                         S, b, out)
        x = out.reshape(B, S, D)
    return x
```

```python
import functools

import numpy as np
import jax
import jax.numpy as jnp
from jax import lax
from jax.experimental import pallas as pl
from jax.experimental.pallas import tpu as pltpu
from jax.experimental.pallas import tpu_sc as plsc

D_MODEL = 1024
ATT_HEADS = 8
ATT_HEAD_DIM = 64
ATT_WIDTH = ATT_HEADS * ATT_HEAD_DIM
DILATIONS = (1, 4, 16)
ATT_SPAN = 128
ROPE_THETA = 10000.0
GLA_HEADS = 4
GLA_KEY_DIM = 64
GLA_VAL_DIM = 128
GLA_KEY_WIDTH = GLA_HEADS * GLA_KEY_DIM
GLA_VAL_WIDTH = GLA_HEADS * GLA_VAL_DIM
GLA_GATE_RANK = 16
GLA_GATE_NORMALIZER = 16.0
GLA_CHUNK = 64
N_EXPERTS = 32
TOP_K = 4
D_FF = D_MODEL
SWIGLU_LIMIT = 7.0
SWIGLU_ALPHA = 1.702
NORM_EPS = 1e-6
N_MOD = 6

LANES = 128
NEG_BIG = -1e30

TM_PROJ = 512
TQ_ATT = 512
TG_GLA = 512
TM_OUT = 512
TM_MOE = 512
TM_FIN = 256
TN_MOD = 512
TK_SPLIT = 512
TM_ROUTE = 1024
SC_INDEX_WINDOW = 128
SC_GATHER_WINDOW = 64
VMEM_LIMIT = 48 * 1024 * 1024

_C_QA, _C_KA, _C_VA = 0, 512, 1024
_C_QG, _C_KG, _C_VG, _C_RG, _C_LR = 1536, 1792, 2048, 2560, 3072
D_IN_PACKED = 3200


def _rms(x, g):
    return x * lax.rsqrt(jnp.mean(x * x, axis=-1, keepdims=True) + NORM_EPS) * g


def _pack_bf16_pairs(x):
    n = x.shape[1] // 2
    u = lax.bitcast_convert_type(x.astype(jnp.bfloat16).astype(jnp.float32), jnp.uint32)
    return (u[:, :n] >> 16) | (u[:, n:] & jnp.uint32(0xFFFF0000))


def _unpack_bf16_pairs(w):
    lo = lax.bitcast_convert_type(w << 16, jnp.float32)
    hi = lax.bitcast_convert_type(w & jnp.uint32(0xFFFF0000), jnp.float32)
    return lo, hi


def _gather_rows(data, idx):
    n_rows = idx.shape[0]
    width = data.shape[1]
    mesh = plsc.VectorSubcoreMesh(core_axis_name="core", subcore_axis_name="subcore")
    n_workers = mesh.num_cores * mesh.num_subcores
    per_worker = n_rows // n_workers
    assert per_worker * n_workers == n_rows and per_worker % SC_INDEX_WINDOW == 0
    halves = SC_INDEX_WINDOW // SC_GATHER_WINDOW

    @pl.kernel(out_type=jax.ShapeDtypeStruct((n_rows, width), data.dtype), mesh=mesh,
               name="gather_rows",
               scratch_types=[pltpu.VMEM((1, SC_INDEX_WINDOW), jnp.int32),
                              pltpu.VMEM((SC_GATHER_WINDOW, width), data.dtype)])
    def gather(x_hbm, i_hbm, o_hbm, idx_vmem, rows_vmem):
        worker = lax.axis_index("core") * mesh.num_subcores + lax.axis_index("subcore")
        base = worker * per_worker

        @pl.loop(0, per_worker // SC_INDEX_WINDOW)
        def _(j):
            off = base + j * SC_INDEX_WINDOW
            pltpu.sync_copy(i_hbm.at[:, pl.ds(off, SC_INDEX_WINDOW)], idx_vmem)
            for h in range(halves):
                part = idx_vmem.at[0, pl.ds(h * SC_GATHER_WINDOW, SC_GATHER_WINDOW)]
                pltpu.sync_copy(x_hbm.at[part], rows_vmem)
                pltpu.sync_copy(rows_vmem,
                                o_hbm.at[pl.ds(off + h * SC_GATHER_WINDOW, SC_GATHER_WINDOW)])

    return gather(data, idx.reshape(1, n_rows))


def _scatter_rows(data, idx, n_out):
    n_copies, n_rows = idx.shape
    width = data.shape[1]
    mesh = plsc.VectorSubcoreMesh(core_axis_name="core", subcore_axis_name="subcore")
    n_workers = mesh.num_cores * mesh.num_subcores
    per_worker = n_rows // n_workers
    assert per_worker * n_workers == n_rows and per_worker % SC_INDEX_WINDOW == 0
    halves = SC_INDEX_WINDOW // SC_GATHER_WINDOW

    @pl.kernel(out_type=jax.ShapeDtypeStruct((n_out, width), data.dtype), mesh=mesh,
               name="scatter_rows",
               scratch_types=[pltpu.VMEM((n_copies, SC_INDEX_WINDOW), jnp.int32),
                              pltpu.VMEM((SC_GATHER_WINDOW, width), data.dtype)])
    def scatter(x_hbm, i_hbm, o_hbm, idx_vmem, rows_vmem):
        worker = lax.axis_index("core") * mesh.num_subcores + lax.axis_index("subcore")
        base = worker * per_worker

        @pl.loop(0, per_worker // SC_INDEX_WINDOW)
        def _(j):
            off = base + j * SC_INDEX_WINDOW
            pltpu.sync_copy(i_hbm.at[:, pl.ds(off, SC_INDEX_WINDOW)], idx_vmem)
            for h in range(halves):
                pltpu.sync_copy(x_hbm.at[pl.ds(off + h * SC_GATHER_WINDOW, SC_GATHER_WINDOW)],
                                rows_vmem)
                for k in range(n_copies):
                    part = idx_vmem.at[k, pl.ds(h * SC_GATHER_WINDOW, SC_GATHER_WINDOW)]
                    pltpu.sync_copy(rows_vmem, o_hbm.at[part])

    return scatter(data, idx)


def _mod_kernel(ct_ref, w_ref, b_ref, o_ref):
    ct = ct_ref[...]
    s = ct * (1.0 / (1.0 + jnp.exp(-ct)))
    w = w_ref[...]
    rows = [jnp.sum(s[:, b:b + 1] * w, axis=0, keepdims=True) for b in range(ct.shape[1])]
    o_ref[...] = jnp.concatenate(rows, axis=0) + b_ref[...]


def _mod(c, w_mod, b_mod):
    B = c.shape[0]
    n = w_mod.shape[1]
    return pl.pallas_call(
        _mod_kernel,
        name="mod",
        grid=(n // TN_MOD,),
        in_specs=[pl.BlockSpec((D_MODEL, B), lambda j: (0, 0)),
                  pl.BlockSpec((D_MODEL, TN_MOD), lambda j: (0, j)),
                  pl.BlockSpec((1, TN_MOD), lambda j: (0, j))],
        out_specs=pl.BlockSpec((B, TN_MOD), lambda j: (0, j)),
        out_shape=jax.ShapeDtypeStruct((B, n), jnp.float32),
        compiler_params=pltpu.CompilerParams(dimension_semantics=("arbitrary",),
                                             vmem_limit_bytes=VMEM_LIMIT),
    )(c.T, w_mod, b_mod.reshape(1, n))


def _store_residue_views(scr_ref, out_refs):
    ns, tm, _ = scr_ref.shape
    w = ns * LANES
    for d, ref in zip(DILATIONS, out_refs):
        for r in range(d):
            for s in range(ns):
                c0 = r * w + s * LANES
                ref[0, :, c0:c0 + LANES] = scr_ref[s, pl.ds(r, tm // d, stride=d), :].astype(ref.dtype)


def _in_proj_kernel(x_ref, mod_ref, g_ref, w_ref, cos_ref, sin_ref, wlr_ref, bg_ref,
                    qa1_ref, qa4_ref, qa16_ref, ka1_ref, ka4_ref, ka16_ref,
                    va1_ref, va4_ref, va16_ref, qg_ref, kg_ref, vg_ref, sr_ref, la_ref,
                    qs_ref, ks_ref, vs_ref):
    x = x_ref[0]
    shift = mod_ref[0, 0:1, :]
    scale = mod_ref[0, 1:2, :]
    h = (_rms(x, g_ref[...]) * (1.0 + scale) + shift).astype(jnp.bfloat16)

    def proj(c0, width):
        return jnp.dot(h, w_ref[:, c0:c0 + width], preferred_element_type=jnp.float32)

    cos = cos_ref[...]
    sin = sin_ref[...]

    def rope(c0, scr_ref, mult):
        p = proj(c0, ATT_WIDTH)
        for s in range(ATT_WIDTH // LANES):
            t = p[:, s * LANES:(s + 1) * LANES]
            r = t * cos + pltpu.roll(t, LANES // 2, axis=1) * sin
            if mult != 1.0:
                r = r * mult
            scr_ref[s] = r

    rope(_C_QA, qs_ref, ATT_HEAD_DIM ** -0.5)
    _store_residue_views(qs_ref, (qa1_ref, qa4_ref, qa16_ref))
    rope(_C_KA, ks_ref, 1.0)
    _store_residue_views(ks_ref, (ka1_ref, ka4_ref, ka16_ref))
    pv = proj(_C_VA, ATT_WIDTH)
    for s in range(ATT_WIDTH // LANES):
        vs_ref[s] = pv[:, s * LANES:(s + 1) * LANES]
    _store_residue_views(vs_ref, (va1_ref, va4_ref, va16_ref))
    qg_ref[0] = proj(_C_QG, GLA_KEY_WIDTH).astype(qg_ref.dtype)
    kg_ref[0] = proj(_C_KG, GLA_KEY_WIDTH).astype(kg_ref.dtype)
    vg_ref[0] = proj(_C_VG, GLA_VAL_WIDTH).astype(vg_ref.dtype)
    r = proj(_C_RG, GLA_VAL_WIDTH)
    sr_ref[0] = (r * (1.0 / (1.0 + jnp.exp(-r)))).astype(sr_ref.dtype)
    lr = proj(_C_LR, LANES).astype(jnp.bfloat16)
    z = jnp.dot(lr, wlr_ref[...], preferred_element_type=jnp.float32) + bg_ref[...]
    log_sig = jnp.minimum(z, 0.0) - jnp.log(1.0 + jnp.exp(-jnp.abs(z)))
    la_ref[0] = log_sig * (1.0 / GLA_GATE_NORMALIZER)


def _in_proj(x, mod3, g_pre, w_packed, cos_t, sin_t, wlr, bg):
    B, S, D = x.shape
    nb = S // TM_PROJ
    tok = lambda w: pl.BlockSpec((1, TM_PROJ, w), lambda b, i: (b, i, 0))
    const = lambda shape: pl.BlockSpec(shape, lambda b, i: tuple(0 for _ in shape))
    bf = jnp.bfloat16
    att_shapes = [jax.ShapeDtypeStruct((B, S // d, d * ATT_WIDTH), bf) for d in DILATIONS] * 3
    att_specs = [pl.BlockSpec((1, TM_PROJ // d, d * ATT_WIDTH), lambda b, i: (b, i, 0))
                 for d in DILATIONS] * 3
    gla_shapes = [jax.ShapeDtypeStruct((B, S, w), dt) for w, dt in
                  [(GLA_KEY_WIDTH, bf), (GLA_KEY_WIDTH, bf),
                   (GLA_VAL_WIDTH, bf), (GLA_VAL_WIDTH, bf), (GLA_KEY_WIDTH, jnp.float32)]]
    return pl.pallas_call(
        _in_proj_kernel,
        name="in_proj",
        grid=(B, nb),
        in_specs=[tok(D),
                  pl.BlockSpec((1, N_MOD, D), lambda b, i: (b, 0, 0)),
                  const((1, D)),
                  const((D, D_IN_PACKED)),
                  pl.BlockSpec((TM_PROJ, LANES), lambda b, i: (i, 0)),
                  pl.BlockSpec((TM_PROJ, LANES), lambda b, i: (i, 0)),
                  const((LANES, GLA_KEY_WIDTH)),
                  const((1, GLA_KEY_WIDTH))],
        out_specs=att_specs + [tok(s.shape[-1]) for s in gla_shapes],
        out_shape=att_shapes + gla_shapes,
        scratch_shapes=[pltpu.VMEM((ATT_WIDTH // LANES, TM_PROJ, LANES), jnp.float32)] * 3,
        compiler_params=pltpu.CompilerParams(dimension_semantics=("arbitrary", "arbitrary"),
                                             vmem_limit_bytes=VMEM_LIMIT),
    )(x, mod3, g_pre, w_packed, cos_t, sin_t, wlr, bg)


def _attn_kernel(q_ref, kp_ref, kc_ref, vp_ref, vc_ref, o_ref, st_ref):
    i = pl.program_id(2)
    tp = kp_ref.shape[1]
    tq = tp
    n_sub = q_ref.shape[1] // tq
    row = lax.broadcasted_iota(jnp.int32, (tq, tp + tq), 0)
    col = lax.broadcasted_iota(jnp.int32, (tq, tp + tq), 1)
    band = (col >= row) & (col <= row + ATT_SPAN)
    bias_inner = jnp.where(band, 0.0, NEG_BIG)
    bias_first = jnp.where(band & ((col >= tp) | (i > 0)), 0.0, NEG_BIG)
    lane = lax.broadcasted_iota(jnp.int32, (tq, LANES), 1)
    q_head = (lane >> 5) & 1
    for sb in range(n_sub):
        rows = slice(sb * tq, (sb + 1) * tq)
        bias = bias_first if sb == 0 else bias_inner
        stats = jnp.zeros((tq, LANES), jnp.float32)
        for hp in range(ATT_HEADS // 2):
            sl = slice(hp * LANES, (hp + 1) * LANES)
            q2 = q_ref[0, rows, sl]
            if sb == 0:
                kcat = jnp.concatenate([kp_ref[0, :, sl], kc_ref[0, rows, sl]], axis=0)
                vcat = jnp.concatenate([vp_ref[0, :, sl], vc_ref[0, rows, sl]], axis=0)
            else:
                kcat = kc_ref[0, (sb - 1) * tq:(sb + 1) * tq, sl]
                vcat = vc_ref[0, (sb - 1) * tq:(sb + 1) * tq, sl]
            outs = []
            for hh in range(2):
                qm = jnp.where(q_head == hh, q2, jnp.zeros_like(q2))
                s = lax.dot_general(qm, kcat, (((1,), (1,)), ((), ())),
                                    preferred_element_type=jnp.float32) + bias
                m = jnp.max(s, axis=1, keepdims=True)
                p = jnp.exp(s - m)
                l = jnp.sum(p, axis=1, keepdims=True)
                o = jnp.dot(p.astype(vcat.dtype), vcat, preferred_element_type=jnp.float32)
                outs.append(o / l)
                stats = jnp.where(lane == 2 * hp + hh, m + jnp.log(l), stats)
            o_ref[0, rows, sl] = jnp.where(lane < ATT_HEAD_DIM, outs[0], outs[1]).astype(o_ref.dtype)
        st_ref[0, rows, :] = stats


def _attn_pattern(qv, kv, vv, dil):
    B, L, _ = qv.shape
    W = ATT_WIDTH
    nq = L // TQ_ATT
    cur = pl.BlockSpec((1, TQ_ATT, W), lambda b, r, i: (b, i, r))
    back = TQ_ATT // ATT_SPAN
    prev = pl.BlockSpec((1, ATT_SPAN, W), lambda b, r, i: (b, jnp.maximum(i * back - 1, 0), r))
    o, st = pl.pallas_call(
        _attn_kernel,
        name=f"attn_d{dil}",
        grid=(B, dil, nq),
        in_specs=[cur, prev, cur, prev, cur],
        out_specs=[cur, pl.BlockSpec((1, TQ_ATT, LANES), lambda b, r, i: (b, i, r))],
        out_shape=[jax.ShapeDtypeStruct((B, L, dil * W), jnp.bfloat16),
                   jax.ShapeDtypeStruct((B, L, dil * LANES), jnp.float32)],
        compiler_params=pltpu.CompilerParams(
            dimension_semantics=("arbitrary", "arbitrary", "arbitrary"),
            vmem_limit_bytes=VMEM_LIMIT),
    )(qv, kv, kv, vv, vv)
    return o, st


def _cumsum_rows(x):
    n = x.shape[0]
    row = lax.broadcasted_iota(jnp.int32, x.shape, 0)
    s = 1
    while s < n:
        x = x + jnp.where(row >= s, pltpu.roll(x, s, axis=0), 0.0)
        s *= 2
    return x


def _gla_kernel(q_ref, k_ref, v_ref, la_ref, sr_ref, g_ref, o_ref, st_ref):
    @pl.when(pl.program_id(1) == 0)
    def _():
        st_ref[...] = jnp.zeros_like(st_ref)

    C = GLA_CHUNK
    n_chunks = q_ref.shape[1] // C
    lane = lax.broadcasted_iota(jnp.int32, (C, LANES), 1)
    lane_sq = lax.broadcasted_iota(jnp.int32, (LANES, LANES), 1)
    tril = (lax.broadcasted_iota(jnp.int32, (C, C), 0)
            >= lax.broadcasted_iota(jnp.int32, (C, C), 1))
    g = g_ref[...]
    bf = jnp.bfloat16

    def chunk(c, carry):
        r0 = pl.multiple_of(c * C, C)
        rows = pl.ds(r0, C)
        for p in range(GLA_HEADS // 2):
            ksl = slice(p * LANES, (p + 1) * LANES)
            b = _cumsum_rows(la_ref[0, rows, ksl])
            b_last = b[C - 1:C, :]
            q = q_ref[0, rows, ksl].astype(jnp.float32) * (GLA_KEY_DIM ** -0.5)
            k = k_ref[0, rows, ksl].astype(jnp.float32)
            q_dec = q * jnp.exp(b)
            k_inv = (k * jnp.exp(-b)).astype(bf)
            k_dec = (k * jnp.exp(b_last - b)).astype(bf)
            st = st_ref[p]
            st_b = st.astype(bf)
            ut = []
            for hh in range(2):
                h = 2 * p + hh
                vsl = slice(h * GLA_VAL_DIM, (h + 1) * GLA_VAL_DIM)
                v = v_ref[0, rows, vsl]
                own = (lane >= GLA_KEY_DIM) if hh else (lane < GLA_KEY_DIM)
                qm = jnp.where(own, q_dec, 0.0).astype(bf)
                att = lax.dot_general(qm, k_inv, (((1,), (1,)), ((), ())),
                                      preferred_element_type=jnp.float32)
                att = jnp.where(tril, att, 0.0).astype(bf)
                o = jnp.dot(att, v, preferred_element_type=jnp.float32)
                o = o + lax.dot_general(qm, st_b, (((1,), (1,)), ((), ())),
                                        preferred_element_type=jnp.float32)
                o = _rms(o, g) * sr_ref[0, rows, vsl].astype(jnp.float32)
                o_ref[0, rows, vsl] = o.astype(o_ref.dtype)
                ut.append(lax.dot_general(v, k_dec, (((0,), (0,)), ((), ())),
                                          preferred_element_type=jnp.float32))
            st_ref[p] = st * jnp.exp(b_last) + jnp.where(lane_sq < GLA_KEY_DIM, ut[0], ut[1])
        return carry

    lax.fori_loop(0, n_chunks, chunk, 0, unroll=8)


def _gla(qg, kg, vg, la, sr, g_gla):
    B, S, _ = qg.shape
    tok = lambda w: pl.BlockSpec((1, TG_GLA, w), lambda b, i: (b, i, 0))
    return pl.pallas_call(
        _gla_kernel,
        name="gla",
        grid=(B, S // TG_GLA),
        in_specs=[tok(GLA_KEY_WIDTH), tok(GLA_KEY_WIDTH), tok(GLA_VAL_WIDTH),
                  tok(GLA_KEY_WIDTH), tok(GLA_VAL_WIDTH),
                  pl.BlockSpec((1, GLA_VAL_DIM), lambda b, i: (0, 0))],
        out_specs=tok(GLA_VAL_WIDTH),
        out_shape=jax.ShapeDtypeStruct((B, S, GLA_VAL_WIDTH), jnp.bfloat16),
        scratch_shapes=[pltpu.VMEM((GLA_HEADS // 2, GLA_VAL_DIM, LANES), jnp.float32)],
        compiler_params=pltpu.CompilerParams(dimension_semantics=("arbitrary", "arbitrary"),
                                             vmem_limit_bytes=VMEM_LIMIT),
    )(qg, kg, vg, la, sr, g_gla)


def _out_proj_kernel(o1_ref, o2_ref, o3_ref, s1_ref, s2_ref, s3_ref, og_ref, x_ref, mod_ref,
                     wout_ref, gpost_ref, gpre_ref, rw_ref, rb_ref,
                     x1_ref, h2_ref, ti_ref, tg_ref, oscr_ref, sscr_ref):
    tm = x_ref.shape[1]
    lane = lax.broadcasted_iota(jnp.int32, (tm, LANES), 1)
    for j, (d, o_ref, s_ref) in enumerate(((DILATIONS[1], o2_ref, s2_ref),
                                           (DILATIONS[2], o3_ref, s3_ref))):
        for r in range(d):
            rows = pl.ds(r, tm // d, stride=d)
            for s in range(ATT_WIDTH // LANES):
                c0 = r * ATT_WIDTH + s * LANES
                oscr_ref[j, s, rows, :] = o_ref[0, :, c0:c0 + LANES].astype(jnp.float32)
            sscr_ref[j, rows, :] = s_ref[0, :, r * LANES:(r + 1) * LANES]
    lses = [s1_ref[0], sscr_ref[0], sscr_ref[1]]
    m = jnp.maximum(jnp.maximum(lses[0], lses[1]), lses[2])
    es = [jnp.exp(t - m) for t in lses]
    inv = 1.0 / (es[0] + es[1] + es[2])
    ws = [e * inv for e in es]
    pairs = []
    for hp in range(ATT_HEADS // 2):
        sl = slice(hp * LANES, (hp + 1) * LANES)
        o_pats = [o1_ref[0, :, sl].astype(jnp.float32), oscr_ref[0, hp], oscr_ref[1, hp]]
        acc = jnp.zeros((tm, LANES), jnp.float32)
        head_of_lane = 2 * hp + (lane >> 6)
        for w, o in zip(ws, o_pats):
            acc = acc + jnp.take_along_axis(w, head_of_lane, axis=1) * o
        pairs.append(acc.astype(jnp.bfloat16))
    mixed = jnp.concatenate(pairs + [og_ref[0]], axis=1)
    y = jnp.dot(mixed, wout_ref[...], preferred_element_type=jnp.float32)
    gate1 = mod_ref[0, 2:3, :]
    shift2 = mod_ref[0, 3:4, :]
    scale2 = mod_ref[0, 4:5, :]
    x1 = x_ref[0] + gate1 * _rms(y, gpost_ref[...])
    x1_ref[0] = x1
    h2 = _rms(x1, gpre_ref[...]) * (1.0 + scale2) + shift2
    h2_ref[0] = _pack_bf16_pairs(h2)
    bf = jnp.bfloat16
    h_hi = h2.astype(bf)
    h_lo = (h2 - h_hi.astype(jnp.float32)).astype(bf)
    rw = rw_ref[...]
    w_hi = rw.astype(bf)
    w_lo = (rw - w_hi.astype(jnp.float32)).astype(bf)
    logits = (jnp.dot(h_hi, w_hi, preferred_element_type=jnp.float32)
              + jnp.dot(h_lo, w_hi, preferred_element_type=jnp.float32)
              + jnp.dot(h_hi, w_lo, preferred_element_type=jnp.float32)) + rb_ref[...]
    lane_f = lane.astype(jnp.float32)
    vals, idxs = [], []
    for _ in range(TOP_K):
        mk = jnp.max(logits, axis=1, keepdims=True)
        ik = jnp.min(jnp.where(logits == mk, lane_f, float(LANES)), axis=1, keepdims=True)
        logits = jnp.where(lane_f == ik, -jnp.inf, logits)
        vals.append(mk)
        idxs.append(ik)
    ex = [jnp.exp(v - vals[0]) for v in vals]
    den = ex[0] + ex[1] + ex[2] + ex[3]
    ti = jnp.zeros((tm, LANES), jnp.float32)
    tg = jnp.zeros((tm, LANES), jnp.float32)
    for kk in range(TOP_K):
        ti = jnp.where(lane == kk, idxs[kk], ti)
        tg = jnp.where(lane == kk, ex[kk] / den, tg)
    ti_ref[0] = ti.astype(jnp.int32)
    tg_ref[0] = tg


def _out_proj(o_pats, st_pats, og, x, mod3, w_out, g_post, g_pre, rw, rb):
    B, S, D = x.shape
    tok = lambda w: pl.BlockSpec((1, TM_OUT, w), lambda b, i: (b, i, 0))
    const = lambda shape: pl.BlockSpec(shape, lambda b, i: tuple(0 for _ in shape))
    out_shapes = [jax.ShapeDtypeStruct((B, S, D), jnp.float32),
                  jax.ShapeDtypeStruct((B, S, D // 2), jnp.uint32),
                  jax.ShapeDtypeStruct((B, S, LANES), jnp.int32),
                  jax.ShapeDtypeStruct((B, S, LANES), jnp.float32)]
    return pl.pallas_call(
        _out_proj_kernel,
        name="out_proj",
        grid=(B, S // TM_OUT),
        in_specs=[pl.BlockSpec((1, TM_OUT // d, d * ATT_WIDTH), lambda b, i: (b, i, 0))
                  for d in DILATIONS]
                 + [pl.BlockSpec((1, TM_OUT // d, d * LANES), lambda b, i: (b, i, 0))
                    for d in DILATIONS]
                 + [tok(GLA_VAL_WIDTH), tok(D),
                  pl.BlockSpec((1, N_MOD, D), lambda b, i: (b, 0, 0)),
                  const((D, D)), const((1, D)), const((1, D)),
                  const((D, LANES)), const((1, LANES))],
        out_specs=[tok(D), tok(D // 2), tok(LANES), tok(LANES)],
        out_shape=out_shapes,
        scratch_shapes=[pltpu.VMEM((2, ATT_WIDTH // LANES, TM_OUT, LANES), jnp.float32),
                        pltpu.VMEM((2, TM_OUT, LANES), jnp.float32)],
        compiler_params=pltpu.CompilerParams(dimension_semantics=("arbitrary", "arbitrary"),
                                             vmem_limit_bytes=VMEM_LIMIT),
    )(*o_pats, *st_pats, og, x, mod3, w_out, g_post, g_pre, rw, rb)


def _split_gate_up_kernel(w_ref, wg_ref, wl_ref):
    group = 2 * LANES
    src = lax.broadcasted_iota(jnp.int32, (group, group), 0)
    dst = lax.broadcasted_iota(jnp.int32, (group, group), 1)
    want = jnp.where(dst < LANES, 2 * dst, 2 * (dst - LANES) + 1)
    perm = jnp.where(src == want, 1.0, 0.0).astype(jnp.bfloat16)
    for j in range(w_ref.shape[1] // group):
        t = jnp.dot(w_ref[:, j * group:(j + 1) * group].astype(jnp.bfloat16), perm,
                    preferred_element_type=jnp.float32)
        wg_ref[:, j * LANES:(j + 1) * LANES] = t[:, :LANES].astype(wg_ref.dtype)
        wl_ref[:, j * LANES:(j + 1) * LANES] = t[:, LANES:].astype(wl_ref.dtype)


def _split_gate_up(w_gate_up):
    E, K, N2 = w_gate_up.shape
    out = jax.ShapeDtypeStruct((E, K, N2 // 2), jnp.bfloat16)
    ospec = pl.BlockSpec((None, TK_SPLIT, N2 // 2), lambda e, i: (e, i, 0))
    return pl.pallas_call(
        _split_gate_up_kernel,
        name="split_gate_up",
        grid=(E, K // TK_SPLIT),
        in_specs=[pl.BlockSpec((None, TK_SPLIT, N2), lambda e, i: (e, i, 0))],
        out_specs=[ospec, ospec],
        out_shape=[out, out],
        compiler_params=pltpu.CompilerParams(dimension_semantics=("arbitrary", "arbitrary"),
                                             vmem_limit_bytes=VMEM_LIMIT),
    )(w_gate_up)


def _moe_kernel(be_ref, nu_ref, x_ref, wg_ref, wl_ref, bg_ref, bl_ref, wd_ref, bd_ref, o_ref,
                wd_bf_ref):
    i = pl.program_id(0)
    new_expert = (i == 0) | (be_ref[i] != be_ref[jnp.maximum(i - 1, 0)])

    @pl.when(new_expert & (i < nu_ref[0]))
    def _():
        wd_bf_ref[...] = wd_ref[...].astype(wd_bf_ref.dtype)

    @pl.when(i < nu_ref[0])
    def _():
        lo, hi = _unpack_bf16_pairs(x_ref[...])
        x = jnp.concatenate([lo, hi], axis=1).astype(jnp.bfloat16)
        xg = jnp.dot(x, wg_ref[...], preferred_element_type=jnp.float32) + bg_ref[...]
        xl = jnp.dot(x, wl_ref[...], preferred_element_type=jnp.float32) + bl_ref[...]
        xg = jnp.minimum(xg, SWIGLU_LIMIT)
        xl = jnp.clip(xl, -SWIGLU_LIMIT, SWIGLU_LIMIT)
        act = xg * (1.0 / (1.0 + jnp.exp(-SWIGLU_ALPHA * xg))) * (xl + 1.0)
        out = jnp.dot(act.astype(jnp.bfloat16), wd_bf_ref[...],
                      preferred_element_type=jnp.float32) + bd_ref[...]
        o_ref[...] = _pack_bf16_pairs(out)

    @pl.when(i >= nu_ref[0])
    def _():
        o_ref[...] = jnp.zeros_like(o_ref)


def _moe(xs, blk_e, n_used, wg, wl, bg, bl, wd, bd):
    P, half = xs.shape
    D = 2 * half
    n_blocks = P // TM_MOE
    wspec = lambda k, n: pl.BlockSpec((None, k, n), lambda i, be, nu: (be[i], 0, 0))
    grid_spec = pltpu.PrefetchScalarGridSpec(
        num_scalar_prefetch=2,
        grid=(n_blocks,),
        in_specs=[pl.BlockSpec((TM_MOE, half), lambda i, be, nu: (i, 0)),
                  wspec(D, D_FF), wspec(D, D_FF), wspec(1, D_FF), wspec(1, D_FF),
                  wspec(D_FF, D), wspec(1, D)],
        out_specs=pl.BlockSpec((TM_MOE, half), lambda i, be, nu: (i, 0)),
        scratch_shapes=[pltpu.VMEM((D_FF, D), jnp.bfloat16)],
    )
    return pl.pallas_call(
        _moe_kernel,
        name="moe",
        grid_spec=grid_spec,
        out_shape=jax.ShapeDtypeStruct((P, half), jnp.uint32),
        compiler_params=pltpu.CompilerParams(dimension_semantics=("arbitrary",),
                                             vmem_limit_bytes=VMEM_LIMIT),
    )(blk_e, n_used, xs, wg, wl, bg, bl, wd, bd)


def _final_kernel(x1_ref, y0_ref, y1_ref, y2_ref, y3_ref, tg_ref, mod_ref, g_ref, o_ref):
    gates = tg_ref[...]
    half = y0_ref.shape[1]
    y_lo = jnp.zeros((x1_ref.shape[0], half), jnp.float32)
    y_hi = jnp.zeros((x1_ref.shape[0], half), jnp.float32)
    for kk, yk_ref in enumerate((y0_ref, y1_ref, y2_ref, y3_ref)):
        lo, hi = _unpack_bf16_pairs(yk_ref[...])
        y_lo = y_lo + gates[:, kk:kk + 1] * lo
        y_hi = y_hi + gates[:, kk:kk + 1] * hi
    y = jnp.concatenate([y_lo, y_hi], axis=1)
    gate2 = mod_ref[0, 5:6, :]
    o_ref[...] = x1_ref[...] + gate2 * _rms(y, g_ref[...])


def _final_part_kernel(x1_ref, y0_ref, y1_ref, y2_ref, y3_ref, tg_ref, mod_ref, g_ref, prev_ref,
                       o_ref):
    del prev_ref
    _final_kernel(x1_ref, y0_ref, y1_ref, y2_ref, y3_ref, tg_ref, mod_ref, g_ref, o_ref)


def _final(x1, yk, tg, mod3, g_post, seq_len, part, prev):
    T, D = x1.shape
    nt = seq_len // TM_FIN
    t0 = part * nt
    yspec = lambda kk: pl.BlockSpec((TM_FIN, D // 2), lambda i: (kk * nt + i, 0))
    in_specs = ([pl.BlockSpec((TM_FIN, D), lambda i: (t0 + i, 0))]
                + [yspec(kk) for kk in range(TOP_K)]
                + [pl.BlockSpec((TM_FIN, LANES), lambda i: (t0 + i, 0)),
                   pl.BlockSpec((1, N_MOD, D), lambda i: (part, 0, 0)),
                   pl.BlockSpec((1, D), lambda i: (0, 0))])
    args = [x1, yk, yk, yk, yk, tg, mod3, g_post]
    body, aliases = _final_kernel, {}
    if prev is not None:
        in_specs.append(pl.BlockSpec(memory_space=pl.ANY))
        args.append(prev)
        body, aliases = _final_part_kernel, {len(args) - 1: 0}
    return pl.pallas_call(
        body,
        name="final",
        grid=(nt,),
        in_specs=in_specs,
        out_specs=pl.BlockSpec((TM_FIN, D), lambda i: (t0 + i, 0)),
        out_shape=jax.ShapeDtypeStruct((T, D), jnp.float32),
        input_output_aliases=aliases,
        compiler_params=pltpu.CompilerParams(dimension_semantics=("arbitrary",),
                                             vmem_limit_bytes=VMEM_LIMIT),
    )(*args)


def _pack_w_in(w_in):
    half = ATT_HEAD_DIM // 2

    def pair_rotary_layout(w):
        k = w.shape[0]
        w = w.reshape(k, ATT_HEADS // 2, 2, 2, half)
        return w.transpose(0, 1, 3, 2, 4).reshape(k, ATT_WIDTH)

    w_in = w_in.astype(jnp.bfloat16)
    lr = jnp.pad(w_in[:, 3072:3072 + GLA_GATE_RANK], ((0, 0), (0, LANES - GLA_GATE_RANK)))
    return jnp.concatenate([pair_rotary_layout(w_in[:, :ATT_WIDTH]),
                            pair_rotary_layout(w_in[:, ATT_WIDTH:2 * ATT_WIDTH]),
                            w_in[:, 1024:3072], lr], axis=1)


def _rope_tables(seq_len):
    half = ATT_HEAD_DIM // 2
    inv_freq = ROPE_THETA ** (-jnp.arange(half, dtype=jnp.float32) / half)
    ang = jnp.arange(seq_len, dtype=jnp.float32)[:, None] * inv_freq[None, :]
    cos = jnp.tile(jnp.cos(ang), (1, LANES // half))
    sin = jnp.tile(jnp.sin(ang), (1, LANES // half))
    sign = jnp.where(jnp.arange(LANES) < LANES // 2, -1.0, 1.0)
    return cos, sin * sign


def _route_kernel(ti_ref, pos_ref, blk_ref, nu_ref, cnt_ref, base_ref, tri_ref):
    phase = pl.program_id(0)
    j = pl.program_id(1)
    tm = ti_ref.shape[0]
    lane = lax.broadcasted_iota(jnp.int32, (tm, LANES), 1)
    ti = ti_ref[...]
    hots = [lane == ti[:, k:k + 1] for k in range(TOP_K)]
    hot_all = jnp.zeros((tm, LANES), jnp.float32)
    for h in hots:
        hot_all = hot_all + jnp.where(h, 1.0, 0.0)
    tile_cnt = jnp.sum(hot_all, axis=0, keepdims=True)

    @pl.when((phase == 0) & (j == 0))
    def _():
        cnt_ref[...] = jnp.zeros_like(cnt_ref)
        r = lax.broadcasted_iota(jnp.int32, (tm, tm), 0)
        c = lax.broadcasted_iota(jnp.int32, (tm, tm), 1)
        tri_ref[...] = jnp.where(c < r, 1.0, 0.0).astype(tri_ref.dtype)

    @pl.when(phase == 0)
    def _():
        cnt_ref[...] += tile_cnt

    @pl.when((phase == 1) & (j == 0))
    def _():
        shift = TM_MOE.bit_length() - 1
        lane8 = lax.broadcasted_iota(jnp.int32, cnt_ref.shape, 1)
        cnt = cnt_ref[...].astype(jnp.int32)
        padded = ((cnt + (TM_MOE - 1)) >> shift) << shift
        pend = padded
        s = 1
        while s < N_EXPERTS:
            pend = pend + jnp.where(lane8 >= s, pltpu.roll(pend, s, axis=1), 0)
            s *= 2
        base_ref[...] = (pend - padded).astype(jnp.float32)
        nb = blk_ref.shape[0]
        blk_start = lax.broadcasted_iota(jnp.int32, (nb, LANES), 0) * TM_MOE
        lane_b = lax.broadcasted_iota(jnp.int32, (nb, LANES), 1)
        done = jnp.where((pend[0:1, :] <= blk_start) & (lane_b < N_EXPERTS), 1.0, 0.0)
        blk = jnp.minimum(jnp.sum(done, axis=1, keepdims=True), float(N_EXPERTS - 1))
        blk_ref[...] = jnp.broadcast_to(blk, blk_ref.shape).astype(jnp.int32)
        total = jnp.max(jnp.where(lane8 < N_EXPERTS, pend, 0).astype(jnp.float32),
                        axis=1, keepdims=True)
        nu_ref[...] = jnp.broadcast_to(total.astype(jnp.int32) >> shift, nu_ref.shape)

    @pl.when(phase == 1)
    def _():
        before = jnp.dot(tri_ref[...], hot_all.astype(tri_ref.dtype),
                         preferred_element_type=jnp.float32)
        slot = before + base_ref[0:1, :]
        pos = jnp.zeros((tm, LANES), jnp.float32)
        for k, h in enumerate(hots):
            pk = jnp.sum(jnp.where(h, slot, 0.0), axis=1, keepdims=True)
            pos = jnp.where(lane == k, pk, pos)
        pos_ref[...] = pos.astype(jnp.int32)
        base_ref[...] += tile_cnt


def _route(ti, n_tokens):
    A = n_tokens * TOP_K
    n_blocks = -(-(A + N_EXPERTS * (TM_MOE - 1)) // TM_MOE)
    nb_pad = -(-n_blocks // 8) * 8
    nt = n_tokens // TM_ROUTE
    pos, blk, nu = pl.pallas_call(
        _route_kernel,
        name="route",
        grid=(2, nt),
        in_specs=[pl.BlockSpec((TM_ROUTE, LANES), lambda p, j: (j, 0))],
        out_specs=[pl.BlockSpec((TM_ROUTE, LANES), lambda p, j: (j * p, 0)),
                   pl.BlockSpec((nb_pad, LANES), lambda p, j: (0, 0)),
                   pl.BlockSpec((8, LANES), lambda p, j: (0, 0))],
        out_shape=[jax.ShapeDtypeStruct((n_tokens, LANES), jnp.int32),
                   jax.ShapeDtypeStruct((nb_pad, LANES), jnp.int32),
                   jax.ShapeDtypeStruct((8, LANES), jnp.int32)],
        scratch_shapes=[pltpu.VMEM((8, LANES), jnp.float32),
                        pltpu.VMEM((8, LANES), jnp.float32),
                        pltpu.VMEM((TM_ROUTE, TM_ROUTE), jnp.bfloat16)],
        compiler_params=pltpu.CompilerParams(dimension_semantics=("arbitrary", "arbitrary"),
                                             vmem_limit_bytes=VMEM_LIMIT),
    )(ti)
    pos_kmajor = pos[:, :TOP_K].T
    return pos_kmajor, n_blocks * TM_MOE, blk[:n_blocks, 0], nu[0, :1]


def kernel(x, c, w_mod, b_mod, g_pre_mix, w_in, w_gate_lr, b_gate, g_gla, w_out, g_post_mix,
           g_pre_ffn, router_w, router_b, w_gate_up, b_gate_up, w_down, b_down, g_post_ffn):
    B, S, D = x.shape
    T = B * S
    bf = jnp.bfloat16
    cos_t, sin_t = _rope_tables(S)
    for l in range(w_mod.shape[0]):
        wg, wl = _split_gate_up(w_gate_up[l])
        wd = w_down[l]
        mod3 = _mod(c, w_mod[l], b_mod[l]).reshape(B, N_MOD, D)
        wlr = jnp.pad(w_gate_lr[l], ((0, LANES - GLA_GATE_RANK), (0, 0))).astype(bf)
        proj = _in_proj(x, mod3, g_pre_mix[l][None], _pack_w_in(w_in[l]), cos_t, sin_t, wlr,
                        b_gate[l][None])
        n_pat = len(DILATIONS)
        qa, ka, va = proj[:n_pat], proj[n_pat:2 * n_pat], proj[2 * n_pat:3 * n_pat]
        qg, kg, vg, sr, la = proj[3 * n_pat:]
        pats = [_attn_pattern(qa[j], ka[j], va[j], d) for j, d in enumerate(DILATIONS)]
        og = _gla(qg, kg, vg, la, sr, g_gla[l][None])
        rw = jnp.pad(router_w[l], ((0, 0), (0, LANES - N_EXPERTS)))
        rb = jnp.pad(router_b[l], (0, LANES - N_EXPERTS), constant_values=NEG_BIG)[None]
        x1, h2, ti, tg = _out_proj([p[0] for p in pats], [p[1] for p in pats], og, x, mod3,
                                   w_out[l].astype(bf), g_post_mix[l][None], g_pre_ffn[l][None],
                                   rw, rb)
        pos_kmajor, n_slots, blk_e, n_used = _route(ti.reshape(T, LANES), T)
        xs = _scatter_rows(h2.reshape(T, D // 2), pos_kmajor, n_slots)
        out_buf = _moe(xs, blk_e, n_used, wg, wl,
                       b_gate_up[l][:, None, 0::2], b_gate_up[l][:, None, 1::2],
                       wd, b_down[l][:, None, :])
        out = None
        for b in range(B):
            yk = _gather_rows(out_buf, pos_kmajor[:, b * S:(b + 1) * S].reshape(-1))
            out = _final(x1.reshape(T, D), yk, tg.reshape(T, LANES), mod3, g_post_ffn[l][None],
                         S, b, out)
        x = out.reshape(B, S, D)
    return x
```

```python
import functools

import numpy as np
import jax
import jax.numpy as jnp
from jax import lax
from jax.experimental import pallas as pl
from jax.experimental.pallas import tpu as pltpu
from jax.experimental.pallas import tpu_sc as plsc

D_MODEL = 1024
ATT_HEADS = 8
ATT_HEAD_DIM = 64
ATT_WIDTH = ATT_HEADS * ATT_HEAD_DIM
DILATIONS = (1, 4, 16)
ATT_SPAN = 128
ROPE_THETA = 10000.0
GLA_HEADS = 4
GLA_KEY_DIM = 64
GLA_VAL_DIM = 128
GLA_KEY_WIDTH = GLA_HEADS * GLA_KEY_DIM
GLA_VAL_WIDTH = GLA_HEADS * GLA_VAL_DIM
GLA_GATE_RANK = 16
GLA_GATE_NORMALIZER = 16.0
GLA_CHUNK = 64
N_EXPERTS = 32
TOP_K = 4
D_FF = D_MODEL
SWIGLU_LIMIT = 7.0
SWIGLU_ALPHA = 1.702
NORM_EPS = 1e-6
N_MOD = 6

LANES = 128
NEG_BIG = -1e30

TM_PROJ = 512
TQ_ATT = 512
TG_GLA = 512
TM_OUT = 512
TM_MOE = 512
TM_FIN = 256
TN_MOD = 512
TK_SPLIT = 512
TM_ROUTE = 1024
FINAL_PARTS = 4
SC_INDEX_WINDOW = 128
SC_GATHER_WINDOW = 64
VMEM_LIMIT = 48 * 1024 * 1024

_C_QA, _C_KA, _C_VA = 0, 512, 1024
_C_QG, _C_KG, _C_VG, _C_RG, _C_LR = 1536, 1792, 2048, 2560, 3072
D_IN_PACKED = 3200


def _rms(x, g):
    return x * lax.rsqrt(jnp.mean(x * x, axis=-1, keepdims=True) + NORM_EPS) * g


def _pack_bf16_pairs(x):
    n = x.shape[1] // 2
    u = lax.bitcast_convert_type(x.astype(jnp.bfloat16).astype(jnp.float32), jnp.uint32)
    return (u[:, :n] >> 16) | (u[:, n:] & jnp.uint32(0xFFFF0000))


def _unpack_bf16_pairs(w):
    lo = lax.bitcast_convert_type(w << 16, jnp.float32)
    hi = lax.bitcast_convert_type(w & jnp.uint32(0xFFFF0000), jnp.float32)
    return lo, hi


def _gather_rows(data, idx):
    n_rows = idx.shape[0]
    width = data.shape[1]
    mesh = plsc.VectorSubcoreMesh(core_axis_name="core", subcore_axis_name="subcore")
    n_workers = mesh.num_cores * mesh.num_subcores
    per_worker = n_rows // n_workers
    assert per_worker * n_workers == n_rows and per_worker % SC_INDEX_WINDOW == 0
    halves = SC_INDEX_WINDOW // SC_GATHER_WINDOW

    @pl.kernel(out_type=jax.ShapeDtypeStruct((n_rows, width), data.dtype), mesh=mesh,
               name="gather_rows",
               scratch_types=[pltpu.VMEM((1, SC_INDEX_WINDOW), jnp.int32),
                              pltpu.VMEM((SC_GATHER_WINDOW, width), data.dtype)])
    def gather(x_hbm, i_hbm, o_hbm, idx_vmem, rows_vmem):
        worker = lax.axis_index("core") * mesh.num_subcores + lax.axis_index("subcore")
        base = worker * per_worker

        @pl.loop(0, per_worker // SC_INDEX_WINDOW)
        def _(j):
            off = base + j * SC_INDEX_WINDOW
            pltpu.sync_copy(i_hbm.at[:, pl.ds(off, SC_INDEX_WINDOW)], idx_vmem)
            for h in range(halves):
                part = idx_vmem.at[0, pl.ds(h * SC_GATHER_WINDOW, SC_GATHER_WINDOW)]
                pltpu.sync_copy(x_hbm.at[part], rows_vmem)
                pltpu.sync_copy(rows_vmem,
                                o_hbm.at[pl.ds(off + h * SC_GATHER_WINDOW, SC_GATHER_WINDOW)])

    return gather(data, idx.reshape(1, n_rows))


def _scatter_rows(data, idx, n_out):
    n_copies, n_rows = idx.shape
    width = data.shape[1]
    mesh = plsc.VectorSubcoreMesh(core_axis_name="core", subcore_axis_name="subcore")
    n_workers = mesh.num_cores * mesh.num_subcores
    per_worker = n_rows // n_workers
    assert per_worker * n_workers == n_rows and per_worker % SC_INDEX_WINDOW == 0
    halves = SC_INDEX_WINDOW // SC_GATHER_WINDOW

    @pl.kernel(out_type=jax.ShapeDtypeStruct((n_out, width), data.dtype), mesh=mesh,
               name="scatter_rows",
               scratch_types=[pltpu.VMEM((n_copies, SC_INDEX_WINDOW), jnp.int32),
                              pltpu.VMEM((SC_GATHER_WINDOW, width), data.dtype)])
    def scatter(x_hbm, i_hbm, o_hbm, idx_vmem, rows_vmem):
        worker = lax.axis_index("core") * mesh.num_subcores + lax.axis_index("subcore")
        base = worker * per_worker

        @pl.loop(0, per_worker // SC_INDEX_WINDOW)
        def _(j):
            off = base + j * SC_INDEX_WINDOW
            pltpu.sync_copy(i_hbm.at[:, pl.ds(off, SC_INDEX_WINDOW)], idx_vmem)
            for h in range(halves):
                pltpu.sync_copy(x_hbm.at[pl.ds(off + h * SC_GATHER_WINDOW, SC_GATHER_WINDOW)],
                                rows_vmem)
                for k in range(n_copies):
                    part = idx_vmem.at[k, pl.ds(h * SC_GATHER_WINDOW, SC_GATHER_WINDOW)]
                    pltpu.sync_copy(rows_vmem, o_hbm.at[part])

    return scatter(data, idx)


def _mod_kernel(ct_ref, w_ref, b_ref, o_ref):
    ct = ct_ref[...]
    s = ct * (1.0 / (1.0 + jnp.exp(-ct)))
    w = w_ref[...]
    rows = [jnp.sum(s[:, b:b + 1] * w, axis=0, keepdims=True) for b in range(ct.shape[1])]
    o_ref[...] = jnp.concatenate(rows, axis=0) + b_ref[...]


def _mod(c, w_mod, b_mod):
    B = c.shape[0]
    n = w_mod.shape[1]
    return pl.pallas_call(
        _mod_kernel,
        name="mod",
        grid=(n // TN_MOD,),
        in_specs=[pl.BlockSpec((D_MODEL, B), lambda j: (0, 0)),
                  pl.BlockSpec((D_MODEL, TN_MOD), lambda j: (0, j)),
                  pl.BlockSpec((1, TN_MOD), lambda j: (0, j))],
        out_specs=pl.BlockSpec((B, TN_MOD), lambda j: (0, j)),
        out_shape=jax.ShapeDtypeStruct((B, n), jnp.float32),
        compiler_params=pltpu.CompilerParams(dimension_semantics=("arbitrary",),
                                             vmem_limit_bytes=VMEM_LIMIT),
    )(c.T, w_mod, b_mod.reshape(1, n))


def _store_residue_views(scr_ref, out_refs):
    ns, tm, _ = scr_ref.shape
    w = ns * LANES
    for d, ref in zip(DILATIONS, out_refs):
        for r in range(d):
            for s in range(ns):
                c0 = r * w + s * LANES
                ref[0, :, c0:c0 + LANES] = scr_ref[s, pl.ds(r, tm // d, stride=d), :].astype(ref.dtype)


def _in_proj_kernel(x_ref, mod_ref, g_ref, w_ref, cos_ref, sin_ref, wlr_ref, bg_ref,
                    qa1_ref, qa4_ref, qa16_ref, ka1_ref, ka4_ref, ka16_ref,
                    va1_ref, va4_ref, va16_ref, qg_ref, kg_ref, vg_ref, sr_ref, la_ref,
                    qs_ref, ks_ref, vs_ref):
    x = x_ref[0]
    shift = mod_ref[0, 0:1, :]
    scale = mod_ref[0, 1:2, :]
    h = (_rms(x, g_ref[...]) * (1.0 + scale) + shift).astype(jnp.bfloat16)

    def proj(c0, width):
        return jnp.dot(h, w_ref[:, c0:c0 + width], preferred_element_type=jnp.float32)

    cos = cos_ref[...]
    sin = sin_ref[...]

    def rope(c0, scr_ref, mult):
        p = proj(c0, ATT_WIDTH)
        for s in range(ATT_WIDTH // LANES):
            t = p[:, s * LANES:(s + 1) * LANES]
            r = t * cos + pltpu.roll(t, LANES // 2, axis=1) * sin
            if mult != 1.0:
                r = r * mult
            scr_ref[s] = r

    rope(_C_QA, qs_ref, ATT_HEAD_DIM ** -0.5)
    _store_residue_views(qs_ref, (qa1_ref, qa4_ref, qa16_ref))
    rope(_C_KA, ks_ref, 1.0)
    _store_residue_views(ks_ref, (ka1_ref, ka4_ref, ka16_ref))
    pv = proj(_C_VA, ATT_WIDTH)
    for s in range(ATT_WIDTH // LANES):
        vs_ref[s] = pv[:, s * LANES:(s + 1) * LANES]
    _store_residue_views(vs_ref, (va1_ref, va4_ref, va16_ref))
    qg_ref[0] = proj(_C_QG, GLA_KEY_WIDTH).astype(qg_ref.dtype)
    kg_ref[0] = proj(_C_KG, GLA_KEY_WIDTH).astype(kg_ref.dtype)
    vg_ref[0] = proj(_C_VG, GLA_VAL_WIDTH).astype(vg_ref.dtype)
    r = proj(_C_RG, GLA_VAL_WIDTH)
    sr_ref[0] = (r * (1.0 / (1.0 + jnp.exp(-r)))).astype(sr_ref.dtype)
    lr = proj(_C_LR, LANES).astype(jnp.bfloat16)
    z = jnp.dot(lr, wlr_ref[...], preferred_element_type=jnp.float32) + bg_ref[...]
    log_sig = jnp.minimum(z, 0.0) - jnp.log(1.0 + jnp.exp(-jnp.abs(z)))
    la_ref[0] = log_sig * (1.0 / GLA_GATE_NORMALIZER)


def _in_proj(x, mod3, g_pre, w_packed, cos_t, sin_t, wlr, bg):
    B, S, D = x.shape
    nb = S // TM_PROJ
    tok = lambda w: pl.BlockSpec((1, TM_PROJ, w), lambda b, i: (b, i, 0))
    const = lambda shape: pl.BlockSpec(shape, lambda b, i: tuple(0 for _ in shape))
    bf = jnp.bfloat16
    att_shapes = [jax.ShapeDtypeStruct((B, S // d, d * ATT_WIDTH), bf) for d in DILATIONS] * 3
    att_specs = [pl.BlockSpec((1, TM_PROJ // d, d * ATT_WIDTH), lambda b, i: (b, i, 0))
                 for d in DILATIONS] * 3
    gla_shapes = [jax.ShapeDtypeStruct((B, S, w), dt) for w, dt in
                  [(GLA_KEY_WIDTH, bf), (GLA_KEY_WIDTH, bf),
                   (GLA_VAL_WIDTH, bf), (GLA_VAL_WIDTH, bf), (GLA_KEY_WIDTH, jnp.float32)]]
    return pl.pallas_call(
        _in_proj_kernel,
        name="in_proj",
        grid=(B, nb),
        in_specs=[tok(D),
                  pl.BlockSpec((1, N_MOD, D), lambda b, i: (b, 0, 0)),
                  const((1, D)),
                  const((D, D_IN_PACKED)),
                  pl.BlockSpec((TM_PROJ, LANES), lambda b, i: (i, 0)),
                  pl.BlockSpec((TM_PROJ, LANES), lambda b, i: (i, 0)),
                  const((LANES, GLA_KEY_WIDTH)),
                  const((1, GLA_KEY_WIDTH))],
        out_specs=att_specs + [tok(s.shape[-1]) for s in gla_shapes],
        out_shape=att_shapes + gla_shapes,
        scratch_shapes=[pltpu.VMEM((ATT_WIDTH // LANES, TM_PROJ, LANES), jnp.float32)] * 3,
        compiler_params=pltpu.CompilerParams(dimension_semantics=("arbitrary", "arbitrary"),
                                             vmem_limit_bytes=VMEM_LIMIT),
    )(x, mod3, g_pre, w_packed, cos_t, sin_t, wlr, bg)


def _attn_kernel(q_ref, kp_ref, kc_ref, vp_ref, vc_ref, o_ref, st_ref):
    i = pl.program_id(2)
    tp = kp_ref.shape[1]
    tq = tp
    n_sub = q_ref.shape[1] // tq
    row = lax.broadcasted_iota(jnp.int32, (tq, tp + tq), 0)
    col = lax.broadcasted_iota(jnp.int32, (tq, tp + tq), 1)
    band = (col >= row) & (col <= row + ATT_SPAN)
    bias_inner = jnp.where(band, 0.0, NEG_BIG)
    bias_first = jnp.where(band & ((col >= tp) | (i > 0)), 0.0, NEG_BIG)
    lane = lax.broadcasted_iota(jnp.int32, (tq, LANES), 1)
    q_head = (lane >> 5) & 1
    for sb in range(n_sub):
        rows = slice(sb * tq, (sb + 1) * tq)
        bias = bias_first if sb == 0 else bias_inner
        stats = jnp.zeros((tq, LANES), jnp.float32)
        for hp in range(ATT_HEADS // 2):
            sl = slice(hp * LANES, (hp + 1) * LANES)
            q2 = q_ref[0, rows, sl]
            if sb == 0:
                kcat = jnp.concatenate([kp_ref[0, :, sl], kc_ref[0, rows, sl]], axis=0)
                vcat = jnp.concatenate([vp_ref[0, :, sl], vc_ref[0, rows, sl]], axis=0)
            else:
                kcat = kc_ref[0, (sb - 1) * tq:(sb + 1) * tq, sl]
                vcat = vc_ref[0, (sb - 1) * tq:(sb + 1) * tq, sl]
            outs = []
            for hh in range(2):
                qm = jnp.where(q_head == hh, q2, jnp.zeros_like(q2))
                s = lax.dot_general(qm, kcat, (((1,), (1,)), ((), ())),
                                    preferred_element_type=jnp.float32) + bias
                m = jnp.max(s, axis=1, keepdims=True)
                p = jnp.exp(s - m)
                l = jnp.sum(p, axis=1, keepdims=True)
                o = jnp.dot(p.astype(vcat.dtype), vcat, preferred_element_type=jnp.float32)
                outs.append(o / l)
                stats = jnp.where(lane == 2 * hp + hh, m + jnp.log(l), stats)
            o_ref[0, rows, sl] = jnp.where(lane < ATT_HEAD_DIM, outs[0], outs[1]).astype(o_ref.dtype)
        st_ref[0, rows, :] = stats


def _attn_pattern(qv, kv, vv, dil):
    B, L, _ = qv.shape
    W = ATT_WIDTH
    nq = L // TQ_ATT
    cur = pl.BlockSpec((1, TQ_ATT, W), lambda b, r, i: (b, i, r))
    back = TQ_ATT // ATT_SPAN
    prev = pl.BlockSpec((1, ATT_SPAN, W), lambda b, r, i: (b, jnp.maximum(i * back - 1, 0), r))
    o, st = pl.pallas_call(
        _attn_kernel,
        name=f"attn_d{dil}",
        grid=(B, dil, nq),
        in_specs=[cur, prev, cur, prev, cur],
        out_specs=[cur, pl.BlockSpec((1, TQ_ATT, LANES), lambda b, r, i: (b, i, r))],
        out_shape=[jax.ShapeDtypeStruct((B, L, dil * W), jnp.bfloat16),
                   jax.ShapeDtypeStruct((B, L, dil * LANES), jnp.float32)],
        compiler_params=pltpu.CompilerParams(
            dimension_semantics=("arbitrary", "arbitrary", "arbitrary"),
            vmem_limit_bytes=VMEM_LIMIT),
    )(qv, kv, kv, vv, vv)
    return o, st


def _cumsum_rows(x):
    n = x.shape[0]
    row = lax.broadcasted_iota(jnp.int32, x.shape, 0)
    s = 1
    while s < n:
        x = x + jnp.where(row >= s, pltpu.roll(x, s, axis=0), 0.0)
        s *= 2
    return x


def _gla_kernel(q_ref, k_ref, v_ref, la_ref, sr_ref, g_ref, o_ref, st_ref):
    @pl.when(pl.program_id(1) == 0)
    def _():
        st_ref[...] = jnp.zeros_like(st_ref)

    C = GLA_CHUNK
    n_chunks = q_ref.shape[1] // C
    lane = lax.broadcasted_iota(jnp.int32, (C, LANES), 1)
    lane_sq = lax.broadcasted_iota(jnp.int32, (LANES, LANES), 1)
    tril = (lax.broadcasted_iota(jnp.int32, (C, C), 0)
            >= lax.broadcasted_iota(jnp.int32, (C, C), 1))
    g = g_ref[...]
    bf = jnp.bfloat16

    def chunk(c, carry):
        r0 = pl.multiple_of(c * C, C)
        rows = pl.ds(r0, C)
        for p in range(GLA_HEADS // 2):
            ksl = slice(p * LANES, (p + 1) * LANES)
            b = _cumsum_rows(la_ref[0, rows, ksl])
            b_last = b[C - 1:C, :]
            q = q_ref[0, rows, ksl].astype(jnp.float32) * (GLA_KEY_DIM ** -0.5)
            k = k_ref[0, rows, ksl].astype(jnp.float32)
            q_dec = q * jnp.exp(b)
            k_inv = (k * jnp.exp(-b)).astype(bf)
            k_dec = (k * jnp.exp(b_last - b)).astype(bf)
            st = st_ref[p]
            st_b = st.astype(bf)
            ut = []
            for hh in range(2):
                h = 2 * p + hh
                vsl = slice(h * GLA_VAL_DIM, (h + 1) * GLA_VAL_DIM)
                v = v_ref[0, rows, vsl]
                own = (lane >= GLA_KEY_DIM) if hh else (lane < GLA_KEY_DIM)
                qm = jnp.where(own, q_dec, 0.0).astype(bf)
                att = lax.dot_general(qm, k_inv, (((1,), (1,)), ((), ())),
                                      preferred_element_type=jnp.float32)
                att = jnp.where(tril, att, 0.0).astype(bf)
                o = jnp.dot(att, v, preferred_element_type=jnp.float32)
                o = o + lax.dot_general(qm, st_b, (((1,), (1,)), ((), ())),
                                        preferred_element_type=jnp.float32)
                o = _rms(o, g) * sr_ref[0, rows, vsl].astype(jnp.float32)
                o_ref[0, rows, vsl] = o.astype(o_ref.dtype)
                ut.append(lax.dot_general(v, k_dec, (((0,), (0,)), ((), ())),
                                          preferred_element_type=jnp.float32))
            st_ref[p] = st * jnp.exp(b_last) + jnp.where(lane_sq < GLA_KEY_DIM, ut[0], ut[1])
        return carry

    lax.fori_loop(0, n_chunks, chunk, 0, unroll=8)


def _gla(qg, kg, vg, la, sr, g_gla):
    B, S, _ = qg.shape
    tok = lambda w: pl.BlockSpec((1, TG_GLA, w), lambda b, i: (b, i, 0))
    return pl.pallas_call(
        _gla_kernel,
        name="gla",
        grid=(B, S // TG_GLA),
        in_specs=[tok(GLA_KEY_WIDTH), tok(GLA_KEY_WIDTH), tok(GLA_VAL_WIDTH),
                  tok(GLA_KEY_WIDTH), tok(GLA_VAL_WIDTH),
                  pl.BlockSpec((1, GLA_VAL_DIM), lambda b, i: (0, 0))],
        out_specs=tok(GLA_VAL_WIDTH),
        out_shape=jax.ShapeDtypeStruct((B, S, GLA_VAL_WIDTH), jnp.bfloat16),
        scratch_shapes=[pltpu.VMEM((GLA_HEADS // 2, GLA_VAL_DIM, LANES), jnp.float32)],
        compiler_params=pltpu.CompilerParams(dimension_semantics=("arbitrary", "arbitrary"),
                                             vmem_limit_bytes=VMEM_LIMIT),
    )(qg, kg, vg, la, sr, g_gla)


def _out_proj_kernel(o1_ref, o2_ref, o3_ref, s1_ref, s2_ref, s3_ref, og_ref, x_ref, mod_ref,
                     wout_ref, gpost_ref, gpre_ref, rw_ref, rb_ref,
                     x1_ref, h2_ref, ti_ref, tg_ref, oscr_ref, sscr_ref):
    tm = x_ref.shape[1]
    lane = lax.broadcasted_iota(jnp.int32, (tm, LANES), 1)
    for j, (d, o_ref, s_ref) in enumerate(((DILATIONS[1], o2_ref, s2_ref),
                                           (DILATIONS[2], o3_ref, s3_ref))):
        for r in range(d):
            rows = pl.ds(r, tm // d, stride=d)
            for s in range(ATT_WIDTH // LANES):
                c0 = r * ATT_WIDTH + s * LANES
                oscr_ref[j, s, rows, :] = o_ref[0, :, c0:c0 + LANES].astype(jnp.float32)
            sscr_ref[j, rows, :] = s_ref[0, :, r * LANES:(r + 1) * LANES]
    lses = [s1_ref[0], sscr_ref[0], sscr_ref[1]]
    m = jnp.maximum(jnp.maximum(lses[0], lses[1]), lses[2])
    es = [jnp.exp(t - m) for t in lses]
    inv = 1.0 / (es[0] + es[1] + es[2])
    ws = [e * inv for e in es]
    pairs = []
    for hp in range(ATT_HEADS // 2):
        sl = slice(hp * LANES, (hp + 1) * LANES)
        o_pats = [o1_ref[0, :, sl].astype(jnp.float32), oscr_ref[0, hp], oscr_ref[1, hp]]
        acc = jnp.zeros((tm, LANES), jnp.float32)
        head_of_lane = 2 * hp + (lane >> 6)
        for w, o in zip(ws, o_pats):
            acc = acc + jnp.take_along_axis(w, head_of_lane, axis=1) * o
        pairs.append(acc.astype(jnp.bfloat16))
    mixed = jnp.concatenate(pairs + [og_ref[0]], axis=1)
    y = jnp.dot(mixed, wout_ref[...], preferred_element_type=jnp.float32)
    gate1 = mod_ref[0, 2:3, :]
    shift2 = mod_ref[0, 3:4, :]
    scale2 = mod_ref[0, 4:5, :]
    x1 = x_ref[0] + gate1 * _rms(y, gpost_ref[...])
    x1_ref[0] = x1
    h2 = _rms(x1, gpre_ref[...]) * (1.0 + scale2) + shift2
    h2_ref[0] = _pack_bf16_pairs(h2)
    bf = jnp.bfloat16
    h_hi = h2.astype(bf)
    h_lo = (h2 - h_hi.astype(jnp.float32)).astype(bf)
    rw = rw_ref[...]
    w_hi = rw.astype(bf)
    w_lo = (rw - w_hi.astype(jnp.float32)).astype(bf)
    logits = (jnp.dot(h_hi, w_hi, preferred_element_type=jnp.float32)
              + jnp.dot(h_lo, w_hi, preferred_element_type=jnp.float32)
              + jnp.dot(h_hi, w_lo, preferred_element_type=jnp.float32)) + rb_ref[...]
    lane_f = lane.astype(jnp.float32)
    vals, idxs = [], []
    for _ in range(TOP_K):
        mk = jnp.max(logits, axis=1, keepdims=True)
        ik = jnp.min(jnp.where(logits == mk, lane_f, float(LANES)), axis=1, keepdims=True)
        logits = jnp.where(lane_f == ik, -jnp.inf, logits)
        vals.append(mk)
        idxs.append(ik)
    ex = [jnp.exp(v - vals[0]) for v in vals]
    den = ex[0] + ex[1] + ex[2] + ex[3]
    ti = jnp.zeros((tm, LANES), jnp.float32)
    tg = jnp.zeros((tm, LANES), jnp.float32)
    for kk in range(TOP_K):
        ti = jnp.where(lane == kk, idxs[kk], ti)
        tg = jnp.where(lane == kk, ex[kk] / den, tg)
    ti_ref[0] = ti.astype(jnp.int32)
    tg_ref[0] = tg


def _out_proj(o_pats, st_pats, og, x, mod3, w_out, g_post, g_pre, rw, rb):
    B, S, D = x.shape
    tok = lambda w: pl.BlockSpec((1, TM_OUT, w), lambda b, i: (b, i, 0))
    const = lambda shape: pl.BlockSpec(shape, lambda b, i: tuple(0 for _ in shape))
    out_shapes = [jax.ShapeDtypeStruct((B, S, D), jnp.float32),
                  jax.ShapeDtypeStruct((B, S, D // 2), jnp.uint32),
                  jax.ShapeDtypeStruct((B, S, LANES), jnp.int32),
                  jax.ShapeDtypeStruct((B, S, LANES), jnp.float32)]
    return pl.pallas_call(
        _out_proj_kernel,
        name="out_proj",
        grid=(B, S // TM_OUT),
        in_specs=[pl.BlockSpec((1, TM_OUT // d, d * ATT_WIDTH), lambda b, i: (b, i, 0))
                  for d in DILATIONS]
                 + [pl.BlockSpec((1, TM_OUT // d, d * LANES), lambda b, i: (b, i, 0))
                    for d in DILATIONS]
                 + [tok(GLA_VAL_WIDTH), tok(D),
                  pl.BlockSpec((1, N_MOD, D), lambda b, i: (b, 0, 0)),
                  const((D, D)), const((1, D)), const((1, D)),
                  const((D, LANES)), const((1, LANES))],
        out_specs=[tok(D), tok(D // 2), tok(LANES), tok(LANES)],
        out_shape=out_shapes,
        scratch_shapes=[pltpu.VMEM((2, ATT_WIDTH // LANES, TM_OUT, LANES), jnp.float32),
                        pltpu.VMEM((2, TM_OUT, LANES), jnp.float32)],
        compiler_params=pltpu.CompilerParams(dimension_semantics=("arbitrary", "arbitrary"),
                                             vmem_limit_bytes=VMEM_LIMIT),
    )(*o_pats, *st_pats, og, x, mod3, w_out, g_post, g_pre, rw, rb)


def _split_gate_up_kernel(w_ref, wg_ref, wl_ref):
    group = 2 * LANES
    src = lax.broadcasted_iota(jnp.int32, (group, group), 0)
    dst = lax.broadcasted_iota(jnp.int32, (group, group), 1)
    want = jnp.where(dst < LANES, 2 * dst, 2 * (dst - LANES) + 1)
    perm = jnp.where(src == want, 1.0, 0.0).astype(jnp.bfloat16)
    for j in range(w_ref.shape[1] // group):
        t = jnp.dot(w_ref[:, j * group:(j + 1) * group].astype(jnp.bfloat16), perm,
                    preferred_element_type=jnp.float32)
        wg_ref[:, j * LANES:(j + 1) * LANES] = t[:, :LANES].astype(wg_ref.dtype)
        wl_ref[:, j * LANES:(j + 1) * LANES] = t[:, LANES:].astype(wl_ref.dtype)


def _split_gate_up(w_gate_up):
    E, K, N2 = w_gate_up.shape
    out = jax.ShapeDtypeStruct((E, K, N2 // 2), jnp.bfloat16)
    ospec = pl.BlockSpec((None, TK_SPLIT, N2 // 2), lambda e, i: (e, i, 0))
    return pl.pallas_call(
        _split_gate_up_kernel,
        name="split_gate_up",
        grid=(E, K // TK_SPLIT),
        in_specs=[pl.BlockSpec((None, TK_SPLIT, N2), lambda e, i: (e, i, 0))],
        out_specs=[ospec, ospec],
        out_shape=[out, out],
        compiler_params=pltpu.CompilerParams(dimension_semantics=("arbitrary", "arbitrary"),
                                             vmem_limit_bytes=VMEM_LIMIT),
    )(w_gate_up)


def _moe_kernel(be_ref, nu_ref, x_ref, wg_ref, wl_ref, bg_ref, bl_ref, wd_ref, bd_ref, o_ref,
                wd_bf_ref):
    i = pl.program_id(0)
    new_expert = (i == 0) | (be_ref[i] != be_ref[jnp.maximum(i - 1, 0)])

    @pl.when(new_expert & (i < nu_ref[0]))
    def _():
        wd_bf_ref[...] = wd_ref[...].astype(wd_bf_ref.dtype)

    @pl.when(i < nu_ref[0])
    def _():
        lo, hi = _unpack_bf16_pairs(x_ref[...])
        x = jnp.concatenate([lo, hi], axis=1).astype(jnp.bfloat16)
        xg = jnp.dot(x, wg_ref[...], preferred_element_type=jnp.float32) + bg_ref[...]
        xl = jnp.dot(x, wl_ref[...], preferred_element_type=jnp.float32) + bl_ref[...]
        xg = jnp.minimum(xg, SWIGLU_LIMIT)
        xl = jnp.clip(xl, -SWIGLU_LIMIT, SWIGLU_LIMIT)
        act = xg * (1.0 / (1.0 + jnp.exp(-SWIGLU_ALPHA * xg))) * (xl + 1.0)
        out = jnp.dot(act.astype(jnp.bfloat16), wd_bf_ref[...],
                      preferred_element_type=jnp.float32) + bd_ref[...]
        o_ref[...] = _pack_bf16_pairs(out)

    @pl.when(i >= nu_ref[0])
    def _():
        o_ref[...] = jnp.zeros_like(o_ref)


def _moe(xs, blk_e, n_used, wg, wl, bg, bl, wd, bd):
    P, half = xs.shape
    D = 2 * half
    n_blocks = P // TM_MOE
    wspec = lambda k, n: pl.BlockSpec((None, k, n), lambda i, be, nu: (be[i], 0, 0))
    grid_spec = pltpu.PrefetchScalarGridSpec(
        num_scalar_prefetch=2,
        grid=(n_blocks,),
        in_specs=[pl.BlockSpec((TM_MOE, half), lambda i, be, nu: (i, 0)),
                  wspec(D, D_FF), wspec(D, D_FF), wspec(1, D_FF), wspec(1, D_FF),
                  wspec(D_FF, D), wspec(1, D)],
        out_specs=pl.BlockSpec((TM_MOE, half), lambda i, be, nu: (i, 0)),
        scratch_shapes=[pltpu.VMEM((D_FF, D), jnp.bfloat16)],
    )
    return pl.pallas_call(
        _moe_kernel,
        name="moe",
        grid_spec=grid_spec,
        out_shape=jax.ShapeDtypeStruct((P, half), jnp.uint32),
        compiler_params=pltpu.CompilerParams(dimension_semantics=("arbitrary",),
                                             vmem_limit_bytes=VMEM_LIMIT),
    )(blk_e, n_used, xs, wg, wl, bg, bl, wd, bd)


def _final_kernel(x1_ref, y0_ref, y1_ref, y2_ref, y3_ref, tg_ref, mod_ref, g_ref, o_ref):
    gates = tg_ref[...]
    half = y0_ref.shape[1]
    y_lo = jnp.zeros((x1_ref.shape[0], half), jnp.float32)
    y_hi = jnp.zeros((x1_ref.shape[0], half), jnp.float32)
    for kk, yk_ref in enumerate((y0_ref, y1_ref, y2_ref, y3_ref)):
        lo, hi = _unpack_bf16_pairs(yk_ref[...])
        y_lo = y_lo + gates[:, kk:kk + 1] * lo
        y_hi = y_hi + gates[:, kk:kk + 1] * hi
    y = jnp.concatenate([y_lo, y_hi], axis=1)
    gate2 = mod_ref[0, 5:6, :]
    o_ref[...] = x1_ref[...] + gate2 * _rms(y, g_ref[...])


def _final_part_kernel(x1_ref, y0_ref, y1_ref, y2_ref, y3_ref, tg_ref, mod_ref, g_ref, prev_ref,
                       o_ref):
    del prev_ref
    _final_kernel(x1_ref, y0_ref, y1_ref, y2_ref, y3_ref, tg_ref, mod_ref, g_ref, o_ref)


def _final(x1, yk, tg, mod3, g_post, seq_len, part, prev):
    T, D = x1.shape
    tp = yk.shape[0] // TOP_K
    nt = tp // TM_FIN
    t0 = part * nt
    batch = (part * tp) // seq_len
    yspec = lambda kk: pl.BlockSpec((TM_FIN, D // 2), lambda i: (kk * nt + i, 0))
    in_specs = ([pl.BlockSpec((TM_FIN, D), lambda i: (t0 + i, 0))]
                + [yspec(kk) for kk in range(TOP_K)]
                + [pl.BlockSpec((TM_FIN, LANES), lambda i: (t0 + i, 0)),
                   pl.BlockSpec((1, N_MOD, D), lambda i: (batch, 0, 0)),
                   pl.BlockSpec((1, D), lambda i: (0, 0))])
    args = [x1, yk, yk, yk, yk, tg, mod3, g_post]
    body, aliases = _final_kernel, {}
    if prev is not None:
        in_specs.append(pl.BlockSpec(memory_space=pl.ANY))
        args.append(prev)
        body, aliases = _final_part_kernel, {len(args) - 1: 0}
    return pl.pallas_call(
        body,
        name="final",
        grid=(nt,),
        in_specs=in_specs,
        out_specs=pl.BlockSpec((TM_FIN, D), lambda i: (t0 + i, 0)),
        out_shape=jax.ShapeDtypeStruct((T, D), jnp.float32),
        input_output_aliases=aliases,
        compiler_params=pltpu.CompilerParams(dimension_semantics=("arbitrary",),
                                             vmem_limit_bytes=VMEM_LIMIT),
    )(*args)


def _pack_w_in(w_in):
    half = ATT_HEAD_DIM // 2

    def pair_rotary_layout(w):
        k = w.shape[0]
        w = w.reshape(k, ATT_HEADS // 2, 2, 2, half)
        return w.transpose(0, 1, 3, 2, 4).reshape(k, ATT_WIDTH)

    w_in = w_in.astype(jnp.bfloat16)
    lr = jnp.pad(w_in[:, 3072:3072 + GLA_GATE_RANK], ((0, 0), (0, LANES - GLA_GATE_RANK)))
    return jnp.concatenate([pair_rotary_layout(w_in[:, :ATT_WIDTH]),
                            pair_rotary_layout(w_in[:, ATT_WIDTH:2 * ATT_WIDTH]),
                            w_in[:, 1024:3072], lr], axis=1)


def _rope_tables(seq_len):
    half = ATT_HEAD_DIM // 2
    inv_freq = ROPE_THETA ** (-jnp.arange(half, dtype=jnp.float32) / half)
    ang = jnp.arange(seq_len, dtype=jnp.float32)[:, None] * inv_freq[None, :]
    cos = jnp.tile(jnp.cos(ang), (1, LANES // half))
    sin = jnp.tile(jnp.sin(ang), (1, LANES // half))
    sign = jnp.where(jnp.arange(LANES) < LANES // 2, -1.0, 1.0)
    return cos, sin * sign


def _route_kernel(ti_ref, pos_ref, blk_ref, nu_ref, cnt_ref, base_ref, tri_ref):
    phase = pl.program_id(0)
    j = pl.program_id(1)
    tm = ti_ref.shape[0]
    lane = lax.broadcasted_iota(jnp.int32, (tm, LANES), 1)
    ti = ti_ref[...]
    hots = [lane == ti[:, k:k + 1] for k in range(TOP_K)]
    hot_all = jnp.zeros((tm, LANES), jnp.float32)
    for h in hots:
        hot_all = hot_all + jnp.where(h, 1.0, 0.0)
    tile_cnt = jnp.sum(hot_all, axis=0, keepdims=True)

    @pl.when((phase == 0) & (j == 0))
    def _():
        cnt_ref[...] = jnp.zeros_like(cnt_ref)
        r = lax.broadcasted_iota(jnp.int32, (tm, tm), 0)
        c = lax.broadcasted_iota(jnp.int32, (tm, tm), 1)
        tri_ref[...] = jnp.where(c < r, 1.0, 0.0).astype(tri_ref.dtype)

    @pl.when(phase == 0)
    def _():
        cnt_ref[...] += tile_cnt

    @pl.when((phase == 1) & (j == 0))
    def _():
        shift = TM_MOE.bit_length() - 1
        lane8 = lax.broadcasted_iota(jnp.int32, cnt_ref.shape, 1)
        cnt = cnt_ref[...].astype(jnp.int32)
        padded = ((cnt + (TM_MOE - 1)) >> shift) << shift
        pend = padded
        s = 1
        while s < N_EXPERTS:
            pend = pend + jnp.where(lane8 >= s, pltpu.roll(pend, s, axis=1), 0)
            s *= 2
        base_ref[...] = (pend - padded).astype(jnp.float32)
        nb = blk_ref.shape[0]
        blk_start = lax.broadcasted_iota(jnp.int32, (nb, LANES), 0) * TM_MOE
        lane_b = lax.broadcasted_iota(jnp.int32, (nb, LANES), 1)
        done = jnp.where((pend[0:1, :] <= blk_start) & (lane_b < N_EXPERTS), 1.0, 0.0)
        blk = jnp.minimum(jnp.sum(done, axis=1, keepdims=True), float(N_EXPERTS - 1))
        blk_ref[...] = jnp.broadcast_to(blk, blk_ref.shape).astype(jnp.int32)
        total = jnp.max(jnp.where(lane8 < N_EXPERTS, pend, 0).astype(jnp.float32),
                        axis=1, keepdims=True)
        nu_ref[...] = jnp.broadcast_to(total.astype(jnp.int32) >> shift, nu_ref.shape)

    @pl.when(phase == 1)
    def _():
        before = jnp.dot(tri_ref[...], hot_all.astype(tri_ref.dtype),
                         preferred_element_type=jnp.float32)
        slot = before + base_ref[0:1, :]
        pos = jnp.zeros((tm, LANES), jnp.float32)
        for k, h in enumerate(hots):
            pk = jnp.sum(jnp.where(h, slot, 0.0), axis=1, keepdims=True)
            pos = jnp.where(lane == k, pk, pos)
        pos_ref[...] = pos.astype(jnp.int32)
        base_ref[...] += tile_cnt


def _route(ti, n_tokens):
    A = n_tokens * TOP_K
    n_blocks = -(-(A + N_EXPERTS * (TM_MOE - 1)) // TM_MOE)
    nb_pad = -(-n_blocks // 8) * 8
    nt = n_tokens // TM_ROUTE
    pos, blk, nu = pl.pallas_call(
        _route_kernel,
        name="route",
        grid=(2, nt),
        in_specs=[pl.BlockSpec((TM_ROUTE, LANES), lambda p, j: (j, 0))],
        out_specs=[pl.BlockSpec((TM_ROUTE, LANES), lambda p, j: (j * p, 0)),
                   pl.BlockSpec((nb_pad, LANES), lambda p, j: (0, 0)),
                   pl.BlockSpec((8, LANES), lambda p, j: (0, 0))],
        out_shape=[jax.ShapeDtypeStruct((n_tokens, LANES), jnp.int32),
                   jax.ShapeDtypeStruct((nb_pad, LANES), jnp.int32),
                   jax.ShapeDtypeStruct((8, LANES), jnp.int32)],
        scratch_shapes=[pltpu.VMEM((8, LANES), jnp.float32),
                        pltpu.VMEM((8, LANES), jnp.float32),
                        pltpu.VMEM((TM_ROUTE, TM_ROUTE), jnp.bfloat16)],
        compiler_params=pltpu.CompilerParams(dimension_semantics=("arbitrary", "arbitrary"),
                                             vmem_limit_bytes=VMEM_LIMIT),
    )(ti)
    pos_kmajor = pos[:, :TOP_K].T
    return pos_kmajor, n_blocks * TM_MOE, blk[:n_blocks, 0], nu[0, :1]


def kernel(x, c, w_mod, b_mod, g_pre_mix, w_in, w_gate_lr, b_gate, g_gla, w_out, g_post_mix,
           g_pre_ffn, router_w, router_b, w_gate_up, b_gate_up, w_down, b_down, g_post_ffn):
    B, S, D = x.shape
    T = B * S
    bf = jnp.bfloat16
    cos_t, sin_t = _rope_tables(S)
    for l in range(w_mod.shape[0]):
        wg, wl = _split_gate_up(w_gate_up[l])
        wd = w_down[l]
        mod3 = _mod(c, w_mod[l], b_mod[l]).reshape(B, N_MOD, D)
        wlr = jnp.pad(w_gate_lr[l], ((0, LANES - GLA_GATE_RANK), (0, 0))).astype(bf)
        proj = _in_proj(x, mod3, g_pre_mix[l][None], _pack_w_in(w_in[l]), cos_t, sin_t, wlr,
                        b_gate[l][None])
        n_pat = len(DILATIONS)
        qa, ka, va = proj[:n_pat], proj[n_pat:2 * n_pat], proj[2 * n_pat:3 * n_pat]
        qg, kg, vg, sr, la = proj[3 * n_pat:]
        pats = [_attn_pattern(qa[j], ka[j], va[j], d) for j, d in enumerate(DILATIONS)]
        og = _gla(qg, kg, vg, la, sr, g_gla[l][None])
        rw = jnp.pad(router_w[l], ((0, 0), (0, LANES - N_EXPERTS)))
        rb = jnp.pad(router_b[l], (0, LANES - N_EXPERTS), constant_values=NEG_BIG)[None]
        x1, h2, ti, tg = _out_proj([p[0] for p in pats], [p[1] for p in pats], og, x, mod3,
                                   w_out[l].astype(bf), g_post_mix[l][None], g_pre_ffn[l][None],
                                   rw, rb)
        pos_kmajor, n_slots, blk_e, n_used = _route(ti.reshape(T, LANES), T)
        xs = _scatter_rows(h2.reshape(T, D // 2), pos_kmajor, n_slots)
        out_buf = _moe(xs, blk_e, n_used, wg, wl,
                       b_gate_up[l][:, None, 0::2], b_gate_up[l][:, None, 1::2],
                       wd, b_down[l][:, None, :])
        out = None
        tp = T // FINAL_PARTS
        for part in range(FINAL_PARTS):
            yk = _gather_rows(out_buf, pos_kmajor[:, part * tp:(part + 1) * tp].reshape(-1))
            out = _final(x1.reshape(T, D), yk, tg.reshape(T, LANES), mod3, g_post_ffn[l][None],
                         S, part, out)
        x = out.reshape(B, S, D)
    return x
```

```python
import functools

import numpy as np
import jax
import jax.numpy as jnp
from jax import lax
from jax.experimental import pallas as pl
from jax.experimental.pallas import tpu as pltpu
from jax.experimental.pallas import tpu_sc as plsc

D_MODEL = 1024
ATT_HEADS = 8
ATT_HEAD_DIM = 64
ATT_WIDTH = ATT_HEADS * ATT_HEAD_DIM
DILATIONS = (1, 4, 16)
ATT_SPAN = 128
ROPE_THETA = 10000.0
GLA_HEADS = 4
GLA_KEY_DIM = 64
GLA_VAL_DIM = 128
GLA_KEY_WIDTH = GLA_HEADS * GLA_KEY_DIM
GLA_VAL_WIDTH = GLA_HEADS * GLA_VAL_DIM
GLA_GATE_RANK = 16
GLA_GATE_NORMALIZER = 16.0
GLA_CHUNK = 64
N_EXPERTS = 32
TOP_K = 4
D_FF = D_MODEL
SWIGLU_LIMIT = 7.0
SWIGLU_ALPHA = 1.702
NORM_EPS = 1e-6
N_MOD = 6

LANES = 128
NEG_BIG = -1e30

TM_PROJ = 512
TQ_ATT = 1024
TG_GLA = 512
TM_OUT = 512
OUT_SUBTILES = 1
TM_MOE = 512
TM_FIN = 256
TN_MOD = 512
TK_SPLIT = 512
TM_ROUTE = 1024
FINAL_PARTS = 4
SC_INDEX_WINDOW = 128
SC_GATHER_WINDOW = 64
VMEM_LIMIT = 48 * 1024 * 1024

_C_QA, _C_KA, _C_VA = 0, 512, 1024
_C_QG, _C_KG, _C_VG, _C_RG, _C_LR = 1536, 1792, 2048, 2560, 3072
D_IN_PACKED = 3200


def _rms(x, g):
    return x * lax.rsqrt(jnp.mean(x * x, axis=-1, keepdims=True) + NORM_EPS) * g


def _pack_bf16_pairs(x):
    n = x.shape[1] // 2
    u = lax.bitcast_convert_type(x.astype(jnp.bfloat16).astype(jnp.float32), jnp.uint32)
    return (u[:, :n] >> 16) | (u[:, n:] & jnp.uint32(0xFFFF0000))


def _unpack_bf16_pairs(w):
    lo = lax.bitcast_convert_type(w << 16, jnp.float32)
    hi = lax.bitcast_convert_type(w & jnp.uint32(0xFFFF0000), jnp.float32)
    return lo, hi


def _gather_rows(data, idx):
    n_rows = idx.shape[0]
    width = data.shape[1]
    mesh = plsc.VectorSubcoreMesh(core_axis_name="core", subcore_axis_name="subcore")
    n_workers = mesh.num_cores * mesh.num_subcores
    per_worker = n_rows // n_workers
    assert per_worker * n_workers == n_rows and per_worker % SC_INDEX_WINDOW == 0
    halves = SC_INDEX_WINDOW // SC_GATHER_WINDOW

    @pl.kernel(out_type=jax.ShapeDtypeStruct((n_rows, width), data.dtype), mesh=mesh,
               name="gather_rows",
               scratch_types=[pltpu.VMEM((1, SC_INDEX_WINDOW), jnp.int32),
                              pltpu.VMEM((SC_GATHER_WINDOW, width), data.dtype)])
    def gather(x_hbm, i_hbm, o_hbm, idx_vmem, rows_vmem):
        worker = lax.axis_index("core") * mesh.num_subcores + lax.axis_index("subcore")
        base = worker * per_worker

        @pl.loop(0, per_worker // SC_INDEX_WINDOW)
        def _(j):
            off = base + j * SC_INDEX_WINDOW
            pltpu.sync_copy(i_hbm.at[:, pl.ds(off, SC_INDEX_WINDOW)], idx_vmem)
            for h in range(halves):
                part = idx_vmem.at[0, pl.ds(h * SC_GATHER_WINDOW, SC_GATHER_WINDOW)]
                pltpu.sync_copy(x_hbm.at[part], rows_vmem)
                pltpu.sync_copy(rows_vmem,
                                o_hbm.at[pl.ds(off + h * SC_GATHER_WINDOW, SC_GATHER_WINDOW)])

    return gather(data, idx.reshape(1, n_rows))


def _scatter_rows(data, idx, n_out):
    n_copies, n_rows = idx.shape
    width = data.shape[1]
    mesh = plsc.VectorSubcoreMesh(core_axis_name="core", subcore_axis_name="subcore")
    n_workers = mesh.num_cores * mesh.num_subcores
    per_worker = n_rows // n_workers
    assert per_worker * n_workers == n_rows and per_worker % SC_INDEX_WINDOW == 0
    halves = SC_INDEX_WINDOW // SC_GATHER_WINDOW

    @pl.kernel(out_type=jax.ShapeDtypeStruct((n_out, width), data.dtype), mesh=mesh,
               name="scatter_rows",
               scratch_types=[pltpu.VMEM((n_copies, SC_INDEX_WINDOW), jnp.int32),
                              pltpu.VMEM((SC_GATHER_WINDOW, width), data.dtype)])
    def scatter(x_hbm, i_hbm, o_hbm, idx_vmem, rows_vmem):
        worker = lax.axis_index("core") * mesh.num_subcores + lax.axis_index("subcore")
        base = worker * per_worker

        @pl.loop(0, per_worker // SC_INDEX_WINDOW)
        def _(j):
            off = base + j * SC_INDEX_WINDOW
            pltpu.sync_copy(i_hbm.at[:, pl.ds(off, SC_INDEX_WINDOW)], idx_vmem)
            for h in range(halves):
                pltpu.sync_copy(x_hbm.at[pl.ds(off + h * SC_GATHER_WINDOW, SC_GATHER_WINDOW)],
                                rows_vmem)
                for k in range(n_copies):
                    part = idx_vmem.at[k, pl.ds(h * SC_GATHER_WINDOW, SC_GATHER_WINDOW)]
                    pltpu.sync_copy(rows_vmem, o_hbm.at[part])

    return scatter(data, idx)


def _mod_kernel(ct_ref, w_ref, b_ref, o_ref):
    ct = ct_ref[...]
    s = ct * (1.0 / (1.0 + jnp.exp(-ct)))
    w = w_ref[...]
    rows = [jnp.sum(s[:, b:b + 1] * w, axis=0, keepdims=True) for b in range(ct.shape[1])]
    o_ref[...] = jnp.concatenate(rows, axis=0) + b_ref[...]


def _mod(c, w_mod, b_mod):
    B = c.shape[0]
    n = w_mod.shape[1]
    return pl.pallas_call(
        _mod_kernel,
        name="mod",
        grid=(n // TN_MOD,),
        in_specs=[pl.BlockSpec((D_MODEL, B), lambda j: (0, 0)),
                  pl.BlockSpec((D_MODEL, TN_MOD), lambda j: (0, j)),
                  pl.BlockSpec((1, TN_MOD), lambda j: (0, j))],
        out_specs=pl.BlockSpec((B, TN_MOD), lambda j: (0, j)),
        out_shape=jax.ShapeDtypeStruct((B, n), jnp.float32),
        compiler_params=pltpu.CompilerParams(dimension_semantics=("arbitrary",),
                                             vmem_limit_bytes=VMEM_LIMIT),
    )(c.T, w_mod, b_mod.reshape(1, n))


def _store_residue_views(scr_ref, out_refs):
    ns, tm, _ = scr_ref.shape
    w = ns * LANES
    for d, ref in zip(DILATIONS, out_refs):
        for r in range(d):
            for s in range(ns):
                c0 = r * w + s * LANES
                ref[0, :, c0:c0 + LANES] = scr_ref[s, pl.ds(r, tm // d, stride=d), :].astype(ref.dtype)


def _in_proj_kernel(x_ref, mod_ref, g_ref, w_ref, cos_ref, sin_ref, wlr_ref, bg_ref,
                    qa1_ref, qa4_ref, qa16_ref, ka1_ref, ka4_ref, ka16_ref,
                    va1_ref, va4_ref, va16_ref, qg_ref, kg_ref, vg_ref, sr_ref, la_ref,
                    qs_ref, ks_ref, vs_ref):
    x = x_ref[0]
    shift = mod_ref[0, 0:1, :]
    scale = mod_ref[0, 1:2, :]
    h = (_rms(x, g_ref[...]) * (1.0 + scale) + shift).astype(jnp.bfloat16)

    def proj(c0, width):
        return jnp.dot(h, w_ref[:, c0:c0 + width], preferred_element_type=jnp.float32)

    cos = cos_ref[...]
    sin = sin_ref[...]

    def rope(c0, scr_ref, mult):
        p = proj(c0, ATT_WIDTH)
        for s in range(ATT_WIDTH // LANES):
            t = p[:, s * LANES:(s + 1) * LANES]
            r = t * cos + pltpu.roll(t, LANES // 2, axis=1) * sin
            if mult != 1.0:
                r = r * mult
            scr_ref[s] = r

    rope(_C_QA, qs_ref, ATT_HEAD_DIM ** -0.5)
    _store_residue_views(qs_ref, (qa1_ref, qa4_ref, qa16_ref))
    rope(_C_KA, ks_ref, 1.0)
    _store_residue_views(ks_ref, (ka1_ref, ka4_ref, ka16_ref))
    pv = proj(_C_VA, ATT_WIDTH)
    for s in range(ATT_WIDTH // LANES):
        vs_ref[s] = pv[:, s * LANES:(s + 1) * LANES]
    _store_residue_views(vs_ref, (va1_ref, va4_ref, va16_ref))
    qg_ref[0] = proj(_C_QG, GLA_KEY_WIDTH).astype(qg_ref.dtype)
    kg_ref[0] = proj(_C_KG, GLA_KEY_WIDTH).astype(kg_ref.dtype)
    vg_ref[0] = proj(_C_VG, GLA_VAL_WIDTH).astype(vg_ref.dtype)
    r = proj(_C_RG, GLA_VAL_WIDTH)
    sr_ref[0] = (r * (1.0 / (1.0 + jnp.exp(-r)))).astype(sr_ref.dtype)
    lr = proj(_C_LR, LANES).astype(jnp.bfloat16)
    z = jnp.dot(lr, wlr_ref[...], preferred_element_type=jnp.float32) + bg_ref[...]
    log_sig = jnp.minimum(z, 0.0) - jnp.log(1.0 + jnp.exp(-jnp.abs(z)))
    la_ref[0] = log_sig * (1.0 / GLA_GATE_NORMALIZER)


def _in_proj(x, mod3, g_pre, w_packed, cos_t, sin_t, wlr, bg):
    B, S, D = x.shape
    nb = S // TM_PROJ
    tok = lambda w: pl.BlockSpec((1, TM_PROJ, w), lambda b, i: (b, i, 0))
    const = lambda shape: pl.BlockSpec(shape, lambda b, i: tuple(0 for _ in shape))
    bf = jnp.bfloat16
    att_shapes = [jax.ShapeDtypeStruct((B, S // d, d * ATT_WIDTH), bf) for d in DILATIONS] * 3
    att_specs = [pl.BlockSpec((1, TM_PROJ // d, d * ATT_WIDTH), lambda b, i: (b, i, 0))
                 for d in DILATIONS] * 3
    gla_shapes = [jax.ShapeDtypeStruct((B, S, w), dt) for w, dt in
                  [(GLA_KEY_WIDTH, bf), (GLA_KEY_WIDTH, bf),
                   (GLA_VAL_WIDTH, bf), (GLA_VAL_WIDTH, bf), (GLA_KEY_WIDTH, jnp.float32)]]
    return pl.pallas_call(
        _in_proj_kernel,
        name="in_proj",
        grid=(B, nb),
        in_specs=[tok(D),
                  pl.BlockSpec((1, N_MOD, D), lambda b, i: (b, 0, 0)),
                  const((1, D)),
                  const((D, D_IN_PACKED)),
                  pl.BlockSpec((TM_PROJ, LANES), lambda b, i: (i, 0)),
                  pl.BlockSpec((TM_PROJ, LANES), lambda b, i: (i, 0)),
                  const((LANES, GLA_KEY_WIDTH)),
                  const((1, GLA_KEY_WIDTH))],
        out_specs=att_specs + [tok(s.shape[-1]) for s in gla_shapes],
        out_shape=att_shapes + gla_shapes,
        scratch_shapes=[pltpu.VMEM((ATT_WIDTH // LANES, TM_PROJ, LANES), jnp.float32)] * 3,
        compiler_params=pltpu.CompilerParams(dimension_semantics=("arbitrary", "arbitrary"),
                                             vmem_limit_bytes=VMEM_LIMIT),
    )(x, mod3, g_pre, w_packed, cos_t, sin_t, wlr, bg)


def _attn_kernel(q_ref, kp_ref, kc_ref, vp_ref, vc_ref, o_ref, st_ref):
    i = pl.program_id(2)
    tp = kp_ref.shape[1]
    tq = tp
    n_sub = q_ref.shape[1] // tq
    row = lax.broadcasted_iota(jnp.int32, (tq, tp + tq), 0)
    col = lax.broadcasted_iota(jnp.int32, (tq, tp + tq), 1)
    band = (col >= row) & (col <= row + ATT_SPAN)
    bias_inner = jnp.where(band, 0.0, NEG_BIG)
    bias_first = jnp.where(band & ((col >= tp) | (i > 0)), 0.0, NEG_BIG)
    lane = lax.broadcasted_iota(jnp.int32, (tq, LANES), 1)
    q_head = (lane >> 5) & 1
    for sb in range(n_sub):
        rows = slice(sb * tq, (sb + 1) * tq)
        bias = bias_first if sb == 0 else bias_inner
        stats = jnp.zeros((tq, LANES), jnp.float32)
        for hp in range(ATT_HEADS // 2):
            sl = slice(hp * LANES, (hp + 1) * LANES)
            q2 = q_ref[0, rows, sl]
            if sb == 0:
                kcat = jnp.concatenate([kp_ref[0, :, sl], kc_ref[0, rows, sl]], axis=0)
                vcat = jnp.concatenate([vp_ref[0, :, sl], vc_ref[0, rows, sl]], axis=0)
            else:
                kcat = kc_ref[0, (sb - 1) * tq:(sb + 1) * tq, sl]
                vcat = vc_ref[0, (sb - 1) * tq:(sb + 1) * tq, sl]
            outs = []
            for hh in range(2):
                qm = jnp.where(q_head == hh, q2, jnp.zeros_like(q2))
                s = lax.dot_general(qm, kcat, (((1,), (1,)), ((), ())),
                                    preferred_element_type=jnp.float32) + bias
                m = jnp.max(s, axis=1, keepdims=True)
                p = jnp.exp(s - m)
                l = jnp.sum(p, axis=1, keepdims=True)
                o = jnp.dot(p.astype(vcat.dtype), vcat, preferred_element_type=jnp.float32)
                outs.append(o / l)
                stats = jnp.where(lane == 2 * hp + hh, m + jnp.log(l), stats)
            o_ref[0, rows, sl] = jnp.where(lane < ATT_HEAD_DIM, outs[0], outs[1]).astype(o_ref.dtype)
        st_ref[0, rows, :] = stats


def _attn_pattern(qv, kv, vv, dil):
    B, L, _ = qv.shape
    W = ATT_WIDTH
    tq = min(TQ_ATT, L)
    nq = L // tq
    cur = pl.BlockSpec((1, tq, W), lambda b, r, i: (b, i, r))
    back = tq // ATT_SPAN
    prev = pl.BlockSpec((1, ATT_SPAN, W), lambda b, r, i: (b, jnp.maximum(i * back - 1, 0), r))
    o, st = pl.pallas_call(
        _attn_kernel,
        name=f"attn_d{dil}",
        grid=(B, dil, nq),
        in_specs=[cur, prev, cur, prev, cur],
        out_specs=[cur, pl.BlockSpec((1, tq, LANES), lambda b, r, i: (b, i, r))],
        out_shape=[jax.ShapeDtypeStruct((B, L, dil * W), jnp.bfloat16),
                   jax.ShapeDtypeStruct((B, L, dil * LANES), jnp.float32)],
        compiler_params=pltpu.CompilerParams(
            dimension_semantics=("arbitrary", "arbitrary", "arbitrary"),
            vmem_limit_bytes=VMEM_LIMIT),
    )(qv, kv, kv, vv, vv)
    return o, st


def _cumsum_rows(x):
    n = x.shape[0]
    row = lax.broadcasted_iota(jnp.int32, x.shape, 0)
    s = 1
    while s < n:
        x = x + jnp.where(row >= s, pltpu.roll(x, s, axis=0), 0.0)
        s *= 2
    return x


def _gla_kernel(q_ref, k_ref, v_ref, la_ref, sr_ref, g_ref, o_ref, st_ref):
    @pl.when(pl.program_id(1) == 0)
    def _():
        st_ref[...] = jnp.zeros_like(st_ref)

    C = GLA_CHUNK
    n_chunks = q_ref.shape[1] // C
    lane = lax.broadcasted_iota(jnp.int32, (C, LANES), 1)
    lane_sq = lax.broadcasted_iota(jnp.int32, (LANES, LANES), 1)
    tril = (lax.broadcasted_iota(jnp.int32, (C, C), 0)
            >= lax.broadcasted_iota(jnp.int32, (C, C), 1))
    g = g_ref[...]
    bf = jnp.bfloat16

    def chunk(c, carry):
        r0 = pl.multiple_of(c * C, C)
        rows = pl.ds(r0, C)
        for p in range(GLA_HEADS // 2):
            ksl = slice(p * LANES, (p + 1) * LANES)
            b = _cumsum_rows(la_ref[0, rows, ksl])
            b_last = b[C - 1:C, :]
            q = q_ref[0, rows, ksl].astype(jnp.float32) * (GLA_KEY_DIM ** -0.5)
            k = k_ref[0, rows, ksl].astype(jnp.float32)
            q_dec = q * jnp.exp(b)
            k_inv = (k * jnp.exp(-b)).astype(bf)
            k_dec = (k * jnp.exp(b_last - b)).astype(bf)
            st = st_ref[p]
            st_b = st.astype(bf)
            ut = []
            for hh in range(2):
                h = 2 * p + hh
                vsl = slice(h * GLA_VAL_DIM, (h + 1) * GLA_VAL_DIM)
                v = v_ref[0, rows, vsl]
                own = (lane >= GLA_KEY_DIM) if hh else (lane < GLA_KEY_DIM)
                qm = jnp.where(own, q_dec, 0.0).astype(bf)
                att = lax.dot_general(qm, k_inv, (((1,), (1,)), ((), ())),
                                      preferred_element_type=jnp.float32)
                att = jnp.where(tril, att, 0.0).astype(bf)
                o = jnp.dot(att, v, preferred_element_type=jnp.float32)
                o = o + lax.dot_general(qm, st_b, (((1,), (1,)), ((), ())),
                                        preferred_element_type=jnp.float32)
                o = _rms(o, g) * sr_ref[0, rows, vsl].astype(jnp.float32)
                o_ref[0, rows, vsl] = o.astype(o_ref.dtype)
                ut.append(lax.dot_general(v, k_dec, (((0,), (0,)), ((), ())),
                                          preferred_element_type=jnp.float32))
            st_ref[p] = st * jnp.exp(b_last) + jnp.where(lane_sq < GLA_KEY_DIM, ut[0], ut[1])
        return carry

    lax.fori_loop(0, n_chunks, chunk, 0, unroll=8)


def _gla(qg, kg, vg, la, sr, g_gla):
    B, S, _ = qg.shape
    tok = lambda w: pl.BlockSpec((1, TG_GLA, w), lambda b, i: (b, i, 0))
    return pl.pallas_call(
        _gla_kernel,
        name="gla",
        grid=(B, S // TG_GLA),
        in_specs=[tok(GLA_KEY_WIDTH), tok(GLA_KEY_WIDTH), tok(GLA_VAL_WIDTH),
                  tok(GLA_KEY_WIDTH), tok(GLA_VAL_WIDTH),
                  pl.BlockSpec((1, GLA_VAL_DIM), lambda b, i: (0, 0))],
        out_specs=tok(GLA_VAL_WIDTH),
        out_shape=jax.ShapeDtypeStruct((B, S, GLA_VAL_WIDTH), jnp.bfloat16),
        scratch_shapes=[pltpu.VMEM((GLA_HEADS // 2, GLA_VAL_DIM, LANES), jnp.float32)],
        compiler_params=pltpu.CompilerParams(dimension_semantics=("arbitrary", "arbitrary"),
                                             vmem_limit_bytes=VMEM_LIMIT),
    )(qg, kg, vg, la, sr, g_gla)


def _out_proj_kernel(o1_ref, o2_ref, o3_ref, s1_ref, s2_ref, s3_ref, og_ref, x_ref, mod_ref,
                     wout_ref, gpost_ref, gpre_ref, rw_ref, rb_ref,
                     x1_ref, h2_ref, ti_ref, tg_ref, oscr_ref, sscr_ref):
    tm = x_ref.shape[1]
    for j, (d, o_ref, s_ref) in enumerate(((DILATIONS[1], o2_ref, s2_ref),
                                           (DILATIONS[2], o3_ref, s3_ref))):
        for r in range(d):
            rows = pl.ds(r, tm // d, stride=d)
            for s in range(ATT_WIDTH // LANES):
                c0 = r * ATT_WIDTH + s * LANES
                oscr_ref[j, s, rows, :] = o_ref[0, :, c0:c0 + LANES].astype(jnp.float32)
            sscr_ref[j, rows, :] = s_ref[0, :, r * LANES:(r + 1) * LANES]
    gate1 = mod_ref[0, 2:3, :]
    shift2 = mod_ref[0, 3:4, :]
    scale2 = mod_ref[0, 4:5, :]
    bf = jnp.bfloat16
    rw = rw_ref[...]
    w_hi = rw.astype(bf)
    w_lo = (rw - w_hi.astype(jnp.float32)).astype(bf)
    ts = tm // OUT_SUBTILES
    lane = lax.broadcasted_iota(jnp.int32, (ts, LANES), 1)
    lane_f = lane.astype(jnp.float32)
    for sub in range(OUT_SUBTILES):
        rs = slice(sub * ts, (sub + 1) * ts)
        lses = [s1_ref[0, rs, :], sscr_ref[0, rs, :], sscr_ref[1, rs, :]]
        m = jnp.maximum(jnp.maximum(lses[0], lses[1]), lses[2])
        es = [jnp.exp(t - m) for t in lses]
        inv = 1.0 / (es[0] + es[1] + es[2])
        ws = [e * inv for e in es]
        pairs = []
        for hp in range(ATT_HEADS // 2):
            sl = slice(hp * LANES, (hp + 1) * LANES)
            o_pats = [o1_ref[0, rs, sl].astype(jnp.float32), oscr_ref[0, hp, rs, :],
                      oscr_ref[1, hp, rs, :]]
            acc = jnp.zeros((ts, LANES), jnp.float32)
            head_of_lane = 2 * hp + (lane >> 6)
            for w, o in zip(ws, o_pats):
                acc = acc + jnp.take_along_axis(w, head_of_lane, axis=1) * o
            pairs.append(acc.astype(bf))
        mixed = jnp.concatenate(pairs + [og_ref[0, rs, :]], axis=1)
        y = jnp.dot(mixed, wout_ref[...], preferred_element_type=jnp.float32)
        x1 = x_ref[0, rs, :] + gate1 * _rms(y, gpost_ref[...])
        x1_ref[0, rs, :] = x1
        h2 = _rms(x1, gpre_ref[...]) * (1.0 + scale2) + shift2
        h2_ref[0, rs, :] = _pack_bf16_pairs(h2)
        h_hi = h2.astype(bf)
        h_lo = (h2 - h_hi.astype(jnp.float32)).astype(bf)
        logits = (jnp.dot(h_hi, w_hi, preferred_element_type=jnp.float32)
                  + jnp.dot(h_lo, w_hi, preferred_element_type=jnp.float32)
                  + jnp.dot(h_hi, w_lo, preferred_element_type=jnp.float32)) + rb_ref[...]
        vals, idxs = [], []
        for _ in range(TOP_K):
            mk = jnp.max(logits, axis=1, keepdims=True)
            ik = jnp.min(jnp.where(logits == mk, lane_f, float(LANES)), axis=1, keepdims=True)
            logits = jnp.where(lane_f == ik, -jnp.inf, logits)
            vals.append(mk)
            idxs.append(ik)
        ex = [jnp.exp(v - vals[0]) for v in vals]
        den = ex[0] + ex[1] + ex[2] + ex[3]
        ti = jnp.zeros((ts, LANES), jnp.float32)
        tg = jnp.zeros((ts, LANES), jnp.float32)
        for kk in range(TOP_K):
            ti = jnp.where(lane == kk, idxs[kk], ti)
            tg = jnp.where(lane == kk, ex[kk] / den, tg)
        ti_ref[0, rs, :] = ti.astype(jnp.int32)
        tg_ref[0, rs, :] = tg


def _out_proj(o_pats, st_pats, og, x, mod3, w_out, g_post, g_pre, rw, rb):
    B, S, D = x.shape
    tok = lambda w: pl.BlockSpec((1, TM_OUT, w), lambda b, i: (b, i, 0))
    const = lambda shape: pl.BlockSpec(shape, lambda b, i: tuple(0 for _ in shape))
    out_shapes = [jax.ShapeDtypeStruct((B, S, D), jnp.float32),
                  jax.ShapeDtypeStruct((B, S, D // 2), jnp.uint32),
                  jax.ShapeDtypeStruct((B, S, LANES), jnp.int32),
                  jax.ShapeDtypeStruct((B, S, LANES), jnp.float32)]
    return pl.pallas_call(
        _out_proj_kernel,
        name="out_proj",
        grid=(B, S // TM_OUT),
        in_specs=[pl.BlockSpec((1, TM_OUT // d, d * ATT_WIDTH), lambda b, i: (b, i, 0))
                  for d in DILATIONS]
                 + [pl.BlockSpec((1, TM_OUT // d, d * LANES), lambda b, i: (b, i, 0))
                    for d in DILATIONS]
                 + [tok(GLA_VAL_WIDTH), tok(D),
                  pl.BlockSpec((1, N_MOD, D), lambda b, i: (b, 0, 0)),
                  const((D, D)), const((1, D)), const((1, D)),
                  const((D, LANES)), const((1, LANES))],
        out_specs=[tok(D), tok(D // 2), tok(LANES), tok(LANES)],
        out_shape=out_shapes,
        scratch_shapes=[pltpu.VMEM((2, ATT_WIDTH // LANES, TM_OUT, LANES), jnp.float32),
                        pltpu.VMEM((2, TM_OUT, LANES), jnp.float32)],
        compiler_params=pltpu.CompilerParams(dimension_semantics=("arbitrary", "arbitrary"),
                                             vmem_limit_bytes=VMEM_LIMIT),
    )(*o_pats, *st_pats, og, x, mod3, w_out, g_post, g_pre, rw, rb)


def _split_gate_up_kernel(w_ref, wg_ref, wl_ref):
    group = 2 * LANES
    src = lax.broadcasted_iota(jnp.int32, (group, group), 0)
    dst = lax.broadcasted_iota(jnp.int32, (group, group), 1)
    want = jnp.where(dst < LANES, 2 * dst, 2 * (dst - LANES) + 1)
    perm = jnp.where(src == want, 1.0, 0.0).astype(jnp.bfloat16)
    for j in range(w_ref.shape[1] // group):
        t = jnp.dot(w_ref[:, j * group:(j + 1) * group].astype(jnp.bfloat16), perm,
                    preferred_element_type=jnp.float32)
        wg_ref[:, j * LANES:(j + 1) * LANES] = t[:, :LANES].astype(wg_ref.dtype)
        wl_ref[:, j * LANES:(j + 1) * LANES] = t[:, LANES:].astype(wl_ref.dtype)


def _split_gate_up(w_gate_up):
    E, K, N2 = w_gate_up.shape
    out = jax.ShapeDtypeStruct((E, K, N2 // 2), jnp.bfloat16)
    ospec = pl.BlockSpec((None, TK_SPLIT, N2 // 2), lambda e, i: (e, i, 0))
    return pl.pallas_call(
        _split_gate_up_kernel,
        name="split_gate_up",
        grid=(E, K // TK_SPLIT),
        in_specs=[pl.BlockSpec((None, TK_SPLIT, N2), lambda e, i: (e, i, 0))],
        out_specs=[ospec, ospec],
        out_shape=[out, out],
        compiler_params=pltpu.CompilerParams(dimension_semantics=("arbitrary", "arbitrary"),
                                             vmem_limit_bytes=VMEM_LIMIT),
    )(w_gate_up)


def _moe_kernel(be_ref, nu_ref, x_ref, wg_ref, wl_ref, bg_ref, bl_ref, wd_ref, bd_ref, o_ref,
                wd_bf_ref):
    i = pl.program_id(0)
    new_expert = (i == 0) | (be_ref[i] != be_ref[jnp.maximum(i - 1, 0)])

    @pl.when(new_expert & (i < nu_ref[0]))
    def _():
        wd_bf_ref[...] = wd_ref[...].astype(wd_bf_ref.dtype)

    @pl.when(i < nu_ref[0])
    def _():
        lo, hi = _unpack_bf16_pairs(x_ref[...])
        x = jnp.concatenate([lo, hi], axis=1).astype(jnp.bfloat16)
        xg = jnp.dot(x, wg_ref[...], preferred_element_type=jnp.float32) + bg_ref[...]
        xl = jnp.dot(x, wl_ref[...], preferred_element_type=jnp.float32) + bl_ref[...]
        xg = jnp.minimum(xg, SWIGLU_LIMIT)
        xl = jnp.clip(xl, -SWIGLU_LIMIT, SWIGLU_LIMIT)
        act = xg * (1.0 / (1.0 + jnp.exp(-SWIGLU_ALPHA * xg))) * (xl + 1.0)
        out = jnp.dot(act.astype(jnp.bfloat16), wd_bf_ref[...],
                      preferred_element_type=jnp.float32) + bd_ref[...]
        o_ref[...] = _pack_bf16_pairs(out)

    @pl.when(i >= nu_ref[0])
    def _():
        o_ref[...] = jnp.zeros_like(o_ref)


def _moe(xs, blk_e, n_used, wg, wl, bg, bl, wd, bd):
    P, half = xs.shape
    D = 2 * half
    n_blocks = P // TM_MOE
    wspec = lambda k, n: pl.BlockSpec((None, k, n), lambda i, be, nu: (be[i], 0, 0))
    grid_spec = pltpu.PrefetchScalarGridSpec(
        num_scalar_prefetch=2,
        grid=(n_blocks,),
        in_specs=[pl.BlockSpec((TM_MOE, half), lambda i, be, nu: (i, 0)),
                  wspec(D, D_FF), wspec(D, D_FF), wspec(1, D_FF), wspec(1, D_FF),
                  wspec(D_FF, D), wspec(1, D)],
        out_specs=pl.BlockSpec((TM_MOE, half), lambda i, be, nu: (i, 0)),
        scratch_shapes=[pltpu.VMEM((D_FF, D), jnp.bfloat16)],
    )
    return pl.pallas_call(
        _moe_kernel,
        name="moe",
        grid_spec=grid_spec,
        out_shape=jax.ShapeDtypeStruct((P, half), jnp.uint32),
        compiler_params=pltpu.CompilerParams(dimension_semantics=("arbitrary",),
                                             vmem_limit_bytes=VMEM_LIMIT),
    )(blk_e, n_used, xs, wg, wl, bg, bl, wd, bd)


def _final_kernel(x1_ref, y0_ref, y1_ref, y2_ref, y3_ref, tg_ref, mod_ref, g_ref, o_ref):
    gates = tg_ref[...]
    half = y0_ref.shape[1]
    y_lo = jnp.zeros((x1_ref.shape[0], half), jnp.float32)
    y_hi = jnp.zeros((x1_ref.shape[0], half), jnp.float32)
    for kk, yk_ref in enumerate((y0_ref, y1_ref, y2_ref, y3_ref)):
        lo, hi = _unpack_bf16_pairs(yk_ref[...])
        y_lo = y_lo + gates[:, kk:kk + 1] * lo
        y_hi = y_hi + gates[:, kk:kk + 1] * hi
    y = jnp.concatenate([y_lo, y_hi], axis=1)
    gate2 = mod_ref[0, 5:6, :]
    o_ref[...] = x1_ref[...] + gate2 * _rms(y, g_ref[...])


def _final_part_kernel(x1_ref, y0_ref, y1_ref, y2_ref, y3_ref, tg_ref, mod_ref, g_ref, prev_ref,
                       o_ref):
    del prev_ref
    _final_kernel(x1_ref, y0_ref, y1_ref, y2_ref, y3_ref, tg_ref, mod_ref, g_ref, o_ref)


def _final(x1, yk, tg, mod3, g_post, seq_len, part, prev):
    T, D = x1.shape
    tp = yk.shape[0] // TOP_K
    nt = tp // TM_FIN
    t0 = part * nt
    batch = (part * tp) // seq_len
    yspec = lambda kk: pl.BlockSpec((TM_FIN, D // 2), lambda i: (kk * nt + i, 0))
    in_specs = ([pl.BlockSpec((TM_FIN, D), lambda i: (t0 + i, 0))]
                + [yspec(kk) for kk in range(TOP_K)]
                + [pl.BlockSpec((TM_FIN, LANES), lambda i: (t0 + i, 0)),
                   pl.BlockSpec((1, N_MOD, D), lambda i: (batch, 0, 0)),
                   pl.BlockSpec((1, D), lambda i: (0, 0))])
    args = [x1, yk, yk, yk, yk, tg, mod3, g_post]
    body, aliases = _final_kernel, {}
    if prev is not None:
        in_specs.append(pl.BlockSpec(memory_space=pl.ANY))
        args.append(prev)
        body, aliases = _final_part_kernel, {len(args) - 1: 0}
    return pl.pallas_call(
        body,
        name="final",
        grid=(nt,),
        in_specs=in_specs,
        out_specs=pl.BlockSpec((TM_FIN, D), lambda i: (t0 + i, 0)),
        out_shape=jax.ShapeDtypeStruct((T, D), jnp.float32),
        input_output_aliases=aliases,
        compiler_params=pltpu.CompilerParams(dimension_semantics=("arbitrary",),
                                             vmem_limit_bytes=VMEM_LIMIT),
    )(*args)


def _pack_w_in(w_in):
    half = ATT_HEAD_DIM // 2

    def pair_rotary_layout(w):
        k = w.shape[0]
        w = w.reshape(k, ATT_HEADS // 2, 2, 2, half)
        return w.transpose(0, 1, 3, 2, 4).reshape(k, ATT_WIDTH)

    w_in = w_in.astype(jnp.bfloat16)
    lr = jnp.pad(w_in[:, 3072:3072 + GLA_GATE_RANK], ((0, 0), (0, LANES - GLA_GATE_RANK)))
    return jnp.concatenate([pair_rotary_layout(w_in[:, :ATT_WIDTH]),
                            pair_rotary_layout(w_in[:, ATT_WIDTH:2 * ATT_WIDTH]),
                            w_in[:, 1024:3072], lr], axis=1)


def _rope_tables(seq_len):
    half = ATT_HEAD_DIM // 2
    inv_freq = ROPE_THETA ** (-jnp.arange(half, dtype=jnp.float32) / half)
    ang = jnp.arange(seq_len, dtype=jnp.float32)[:, None] * inv_freq[None, :]
    cos = jnp.tile(jnp.cos(ang), (1, LANES // half))
    sin = jnp.tile(jnp.sin(ang), (1, LANES // half))
    sign = jnp.where(jnp.arange(LANES) < LANES // 2, -1.0, 1.0)
    return cos, sin * sign


def _route_kernel(ti_ref, pos_ref, blk_ref, nu_ref, cnt_ref, base_ref, tri_ref):
    phase = pl.program_id(0)
    j = pl.program_id(1)
    tm = ti_ref.shape[0]
    lane = lax.broadcasted_iota(jnp.int32, (tm, LANES), 1)
    ti = ti_ref[...]
    hots = [lane == ti[:, k:k + 1] for k in range(TOP_K)]
    hot_all = jnp.zeros((tm, LANES), jnp.float32)
    for h in hots:
        hot_all = hot_all + jnp.where(h, 1.0, 0.0)
    tile_cnt = jnp.sum(hot_all, axis=0, keepdims=True)

    @pl.when((phase == 0) & (j == 0))
    def _():
        cnt_ref[...] = jnp.zeros_like(cnt_ref)
        r = lax.broadcasted_iota(jnp.int32, (tm, tm), 0)
        c = lax.broadcasted_iota(jnp.int32, (tm, tm), 1)
        tri_ref[...] = jnp.where(c < r, 1.0, 0.0).astype(tri_ref.dtype)

    @pl.when(phase == 0)
    def _():
        cnt_ref[...] += tile_cnt

    @pl.when((phase == 1) & (j == 0))
    def _():
        shift = TM_MOE.bit_length() - 1
        lane8 = lax.broadcasted_iota(jnp.int32, cnt_ref.shape, 1)
        cnt = cnt_ref[...].astype(jnp.int32)
        padded = ((cnt + (TM_MOE - 1)) >> shift) << shift
        pend = padded
        s = 1
        while s < N_EXPERTS:
            pend = pend + jnp.where(lane8 >= s, pltpu.roll(pend, s, axis=1), 0)
            s *= 2
        base_ref[...] = (pend - padded).astype(jnp.float32)
        nb = blk_ref.shape[0]
        blk_start = lax.broadcasted_iota(jnp.int32, (nb, LANES), 0) * TM_MOE
        lane_b = lax.broadcasted_iota(jnp.int32, (nb, LANES), 1)
        done = jnp.where((pend[0:1, :] <= blk_start) & (lane_b < N_EXPERTS), 1.0, 0.0)
        blk = jnp.minimum(jnp.sum(done, axis=1, keepdims=True), float(N_EXPERTS - 1))
        blk_ref[...] = jnp.broadcast_to(blk, blk_ref.shape).astype(jnp.int32)
        total = jnp.max(jnp.where(lane8 < N_EXPERTS, pend, 0).astype(jnp.float32),
                        axis=1, keepdims=True)
        nu_ref[...] = jnp.broadcast_to(total.astype(jnp.int32) >> shift, nu_ref.shape)

    @pl.when(phase == 1)
    def _():
        before = jnp.dot(tri_ref[...], hot_all.astype(tri_ref.dtype),
                         preferred_element_type=jnp.float32)
        slot = before + base_ref[0:1, :]
        pos = jnp.zeros((tm, LANES), jnp.float32)
        for k, h in enumerate(hots):
            pk = jnp.sum(jnp.where(h, slot, 0.0), axis=1, keepdims=True)
            pos = jnp.where(lane == k, pk, pos)
        pos_ref[...] = pos.astype(jnp.int32)
        base_ref[...] += tile_cnt


def _route(ti, n_tokens):
    A = n_tokens * TOP_K
    n_blocks = -(-(A + N_EXPERTS * (TM_MOE - 1)) // TM_MOE)
    nb_pad = -(-n_blocks // 8) * 8
    nt = n_tokens // TM_ROUTE
    pos, blk, nu = pl.pallas_call(
        _route_kernel,
        name="route",
        grid=(2, nt),
        in_specs=[pl.BlockSpec((TM_ROUTE, LANES), lambda p, j: (j, 0))],
        out_specs=[pl.BlockSpec((TM_ROUTE, LANES), lambda p, j: (j * p, 0)),
                   pl.BlockSpec((nb_pad, LANES), lambda p, j: (0, 0)),
                   pl.BlockSpec((8, LANES), lambda p, j: (0, 0))],
        out_shape=[jax.ShapeDtypeStruct((n_tokens, LANES), jnp.int32),
                   jax.ShapeDtypeStruct((nb_pad, LANES), jnp.int32),
                   jax.ShapeDtypeStruct((8, LANES), jnp.int32)],
        scratch_shapes=[pltpu.VMEM((8, LANES), jnp.float32),
                        pltpu.VMEM((8, LANES), jnp.float32),
                        pltpu.VMEM((TM_ROUTE, TM_ROUTE), jnp.bfloat16)],
        compiler_params=pltpu.CompilerParams(dimension_semantics=("arbitrary", "arbitrary"),
                                             vmem_limit_bytes=VMEM_LIMIT),
    )(ti)
    pos_kmajor = pos[:, :TOP_K].T
    return pos_kmajor, n_blocks * TM_MOE, blk[:n_blocks, 0], nu[0, :1]


def kernel(x, c, w_mod, b_mod, g_pre_mix, w_in, w_gate_lr, b_gate, g_gla, w_out, g_post_mix,
           g_pre_ffn, router_w, router_b, w_gate_up, b_gate_up, w_down, b_down, g_post_ffn):
    B, S, D = x.shape
    T = B * S
    bf = jnp.bfloat16
    cos_t, sin_t = _rope_tables(S)
    for l in range(w_mod.shape[0]):
        wg, wl = _split_gate_up(w_gate_up[l])
        wd = w_down[l]
        mod3 = _mod(c, w_mod[l], b_mod[l]).reshape(B, N_MOD, D)
        wlr = jnp.pad(w_gate_lr[l], ((0, LANES - GLA_GATE_RANK), (0, 0))).astype(bf)
        proj = _in_proj(x, mod3, g_pre_mix[l][None], _pack_w_in(w_in[l]), cos_t, sin_t, wlr,
                        b_gate[l][None])
        n_pat = len(DILATIONS)
        qa, ka, va = proj[:n_pat], proj[n_pat:2 * n_pat], proj[2 * n_pat:3 * n_pat]
        qg, kg, vg, sr, la = proj[3 * n_pat:]
        pats = [_attn_pattern(qa[j], ka[j], va[j], d) for j, d in enumerate(DILATIONS)]
        og = _gla(qg, kg, vg, la, sr, g_gla[l][None])
        rw = jnp.pad(router_w[l], ((0, 0), (0, LANES - N_EXPERTS)))
        rb = jnp.pad(router_b[l], (0, LANES - N_EXPERTS), constant_values=NEG_BIG)[None]
        x1, h2, ti, tg = _out_proj([p[0] for p in pats], [p[1] for p in pats], og, x, mod3,
                                   w_out[l].astype(bf), g_post_mix[l][None], g_pre_ffn[l][None],
                                   rw, rb)
        pos_kmajor, n_slots, blk_e, n_used = _route(ti.reshape(T, LANES), T)
        xs = _scatter_rows(h2.reshape(T, D // 2), pos_kmajor, n_slots)
        out_buf = _moe(xs, blk_e, n_used, wg, wl,
                       b_gate_up[l][:, None, 0::2], b_gate_up[l][:, None, 1::2],
                       wd, b_down[l][:, None, :])
        out = None
        tp = T // FINAL_PARTS
        for part in range(FINAL_PARTS):
            yk = _gather_rows(out_buf, pos_kmajor[:, part * tp:(part + 1) * tp].reshape(-1))
            out = _final(x1.reshape(T, D), yk, tg.reshape(T, LANES), mod3, g_post_ffn[l][None],
                         S, part, out)
        x = out.reshape(B, S, D)
    return x
```

```python
import functools

import numpy as np
import jax
import jax.numpy as jnp
from jax import lax
from jax.experimental import pallas as pl
from jax.experimental.pallas import tpu as pltpu
from jax.experimental.pallas import tpu_sc as plsc

D_MODEL = 1024
ATT_HEADS = 8
ATT_HEAD_DIM = 64
ATT_WIDTH = ATT_HEADS * ATT_HEAD_DIM
DILATIONS = (1, 4, 16)
ATT_SPAN = 128
ROPE_THETA = 10000.0
GLA_HEADS = 4
GLA_KEY_DIM = 64
GLA_VAL_DIM = 128
GLA_KEY_WIDTH = GLA_HEADS * GLA_KEY_DIM
GLA_VAL_WIDTH = GLA_HEADS * GLA_VAL_DIM
GLA_GATE_RANK = 16
GLA_GATE_NORMALIZER = 16.0
GLA_CHUNK = 64
N_EXPERTS = 32
TOP_K = 4
D_FF = D_MODEL
SWIGLU_LIMIT = 7.0
SWIGLU_ALPHA = 1.702
NORM_EPS = 1e-6
N_MOD = 6

LANES = 128
NEG_BIG = -1e30

TM_PROJ = 512
TQ_ATT = 1024
TG_GLA = 512
GLA_UNROLL = 8
TM_OUT = 512
OUT_SUBTILES = 1
TM_MOE = 512
TM_FIN = 256
TN_MOD = 512
TK_SPLIT = 512
TM_ROUTE = 1024
FINAL_PARTS = 4
SC_INDEX_WINDOW = 128
SC_GATHER_WINDOW = 64
VMEM_LIMIT = 48 * 1024 * 1024

_C_QA, _C_KA, _C_VA = 0, 512, 1024
_C_QG, _C_KG, _C_VG, _C_RG, _C_LR = 1536, 1792, 2048, 2560, 3072
D_IN_PACKED = 3200


def _rms(x, g):
    return x * lax.rsqrt(jnp.mean(x * x, axis=-1, keepdims=True) + NORM_EPS) * g


def _pack_bf16_pairs(x):
    n = x.shape[1] // 2
    u = lax.bitcast_convert_type(x.astype(jnp.bfloat16).astype(jnp.float32), jnp.uint32)
    return (u[:, :n] >> 16) | (u[:, n:] & jnp.uint32(0xFFFF0000))


def _unpack_bf16_pairs(w):
    lo = lax.bitcast_convert_type(w << 16, jnp.float32)
    hi = lax.bitcast_convert_type(w & jnp.uint32(0xFFFF0000), jnp.float32)
    return lo, hi


def _gather_rows(data, idx):
    n_rows = idx.shape[0]
    width = data.shape[1]
    mesh = plsc.VectorSubcoreMesh(core_axis_name="core", subcore_axis_name="subcore")
    n_workers = mesh.num_cores * mesh.num_subcores
    per_worker = n_rows // n_workers
    assert per_worker * n_workers == n_rows and per_worker % SC_INDEX_WINDOW == 0
    halves = SC_INDEX_WINDOW // SC_GATHER_WINDOW

    @pl.kernel(out_type=jax.ShapeDtypeStruct((n_rows, width), data.dtype), mesh=mesh,
               name="gather_rows",
               scratch_types=[pltpu.VMEM((1, SC_INDEX_WINDOW), jnp.int32),
                              pltpu.VMEM((SC_GATHER_WINDOW, width), data.dtype)])
    def gather(x_hbm, i_hbm, o_hbm, idx_vmem, rows_vmem):
        worker = lax.axis_index("core") * mesh.num_subcores + lax.axis_index("subcore")
        base = worker * per_worker

        @pl.loop(0, per_worker // SC_INDEX_WINDOW)
        def _(j):
            off = base + j * SC_INDEX_WINDOW
            pltpu.sync_copy(i_hbm.at[:, pl.ds(off, SC_INDEX_WINDOW)], idx_vmem)
            for h in range(halves):
                part = idx_vmem.at[0, pl.ds(h * SC_GATHER_WINDOW, SC_GATHER_WINDOW)]
                pltpu.sync_copy(x_hbm.at[part], rows_vmem)
                pltpu.sync_copy(rows_vmem,
                                o_hbm.at[pl.ds(off + h * SC_GATHER_WINDOW, SC_GATHER_WINDOW)])

    return gather(data, idx.reshape(1, n_rows))


def _scatter_rows(data, idx, n_out):
    n_copies, n_rows = idx.shape
    width = data.shape[1]
    mesh = plsc.VectorSubcoreMesh(core_axis_name="core", subcore_axis_name="subcore")
    n_workers = mesh.num_cores * mesh.num_subcores
    per_worker = n_rows // n_workers
    assert per_worker * n_workers == n_rows and per_worker % SC_INDEX_WINDOW == 0
    halves = SC_INDEX_WINDOW // SC_GATHER_WINDOW

    @pl.kernel(out_type=jax.ShapeDtypeStruct((n_out, width), data.dtype), mesh=mesh,
               name="scatter_rows",
               scratch_types=[pltpu.VMEM((n_copies, SC_INDEX_WINDOW), jnp.int32),
                              pltpu.VMEM((SC_GATHER_WINDOW, width), data.dtype)])
    def scatter(x_hbm, i_hbm, o_hbm, idx_vmem, rows_vmem):
        worker = lax.axis_index("core") * mesh.num_subcores + lax.axis_index("subcore")
        base = worker * per_worker

        @pl.loop(0, per_worker // SC_INDEX_WINDOW)
        def _(j):
            off = base + j * SC_INDEX_WINDOW
            pltpu.sync_copy(i_hbm.at[:, pl.ds(off, SC_INDEX_WINDOW)], idx_vmem)
            for h in range(halves):
                pltpu.sync_copy(x_hbm.at[pl.ds(off + h * SC_GATHER_WINDOW, SC_GATHER_WINDOW)],
                                rows_vmem)
                for k in range(n_copies):
                    part = idx_vmem.at[k, pl.ds(h * SC_GATHER_WINDOW, SC_GATHER_WINDOW)]
                    pltpu.sync_copy(rows_vmem, o_hbm.at[part])

    return scatter(data, idx)


def _mod_kernel(ct_ref, w_ref, b_ref, o_ref):
    ct = ct_ref[...]
    s = ct * (1.0 / (1.0 + jnp.exp(-ct)))
    w = w_ref[...]
    rows = [jnp.sum(s[:, b:b + 1] * w, axis=0, keepdims=True) for b in range(ct.shape[1])]
    o_ref[...] = jnp.concatenate(rows, axis=0) + b_ref[...]


def _mod(c, w_mod, b_mod):
    B = c.shape[0]
    n = w_mod.shape[1]
    return pl.pallas_call(
        _mod_kernel,
        name="mod",
        grid=(n // TN_MOD,),
        in_specs=[pl.BlockSpec((D_MODEL, B), lambda j: (0, 0)),
                  pl.BlockSpec((D_MODEL, TN_MOD), lambda j: (0, j)),
                  pl.BlockSpec((1, TN_MOD), lambda j: (0, j))],
        out_specs=pl.BlockSpec((B, TN_MOD), lambda j: (0, j)),
        out_shape=jax.ShapeDtypeStruct((B, n), jnp.float32),
        compiler_params=pltpu.CompilerParams(dimension_semantics=("arbitrary",),
                                             vmem_limit_bytes=VMEM_LIMIT),
    )(c.T, w_mod, b_mod.reshape(1, n))


def _store_residue_views(scr_ref, out_refs):
    ns, tm, _ = scr_ref.shape
    w = ns * LANES
    for d, ref in zip(DILATIONS, out_refs):
        for r in range(d):
            for s in range(ns):
                c0 = r * w + s * LANES
                rows = slice(None) if d == 1 else pl.ds(r, tm // d, stride=d)
                ref[0, :, c0:c0 + LANES] = scr_ref[s, rows, :].astype(ref.dtype)


def _in_proj_kernel(x_ref, mod_ref, g_ref, w_ref, cos_ref, sin_ref, wlr_ref, bg_ref,
                    qa1_ref, qa4_ref, qa16_ref, ka1_ref, ka4_ref, ka16_ref,
                    va1_ref, va4_ref, va16_ref, qg_ref, kg_ref, vg_ref, sr_ref, la_ref,
                    qs_ref, ks_ref, vs_ref):
    x = x_ref[0]
    shift = mod_ref[0, 0:1, :]
    scale = mod_ref[0, 1:2, :]
    h = (_rms(x, g_ref[...]) * (1.0 + scale) + shift).astype(jnp.bfloat16)

    def proj(c0, width):
        return jnp.dot(h, w_ref[:, c0:c0 + width], preferred_element_type=jnp.float32)

    cos = cos_ref[...]
    sin = sin_ref[...]

    def rope(c0, scr_ref, mult):
        p = proj(c0, ATT_WIDTH)
        for s in range(ATT_WIDTH // LANES):
            t = p[:, s * LANES:(s + 1) * LANES]
            r = t * cos + pltpu.roll(t, LANES // 2, axis=1) * sin
            if mult != 1.0:
                r = r * mult
            scr_ref[s] = r

    rope(_C_QA, qs_ref, ATT_HEAD_DIM ** -0.5)
    _store_residue_views(qs_ref, (qa1_ref, qa4_ref, qa16_ref))
    rope(_C_KA, ks_ref, 1.0)
    _store_residue_views(ks_ref, (ka1_ref, ka4_ref, ka16_ref))
    pv = proj(_C_VA, ATT_WIDTH)
    for s in range(ATT_WIDTH // LANES):
        vs_ref[s] = pv[:, s * LANES:(s + 1) * LANES]
    _store_residue_views(vs_ref, (va1_ref, va4_ref, va16_ref))
    qg_ref[0] = proj(_C_QG, GLA_KEY_WIDTH).astype(qg_ref.dtype)
    kg_ref[0] = proj(_C_KG, GLA_KEY_WIDTH).astype(kg_ref.dtype)
    vg_ref[0] = proj(_C_VG, GLA_VAL_WIDTH).astype(vg_ref.dtype)
    r = proj(_C_RG, GLA_VAL_WIDTH)
    sr_ref[0] = (r * (1.0 / (1.0 + jnp.exp(-r)))).astype(sr_ref.dtype)
    lr = proj(_C_LR, LANES).astype(jnp.bfloat16)
    z = jnp.dot(lr, wlr_ref[...], preferred_element_type=jnp.float32) + bg_ref[...]
    log_sig = jnp.minimum(z, 0.0) - jnp.log(1.0 + jnp.exp(-jnp.abs(z)))
    la_ref[0] = log_sig * (1.0 / GLA_GATE_NORMALIZER)


def _in_proj(x, mod3, g_pre, w_packed, cos_t, sin_t, wlr, bg):
    B, S, D = x.shape
    nb = S // TM_PROJ
    tok = lambda w: pl.BlockSpec((1, TM_PROJ, w), lambda b, i: (b, i, 0))
    const = lambda shape: pl.BlockSpec(shape, lambda b, i: tuple(0 for _ in shape))
    bf = jnp.bfloat16
    att_shapes = [jax.ShapeDtypeStruct((B, S // d, d * ATT_WIDTH), bf) for d in DILATIONS] * 3
    att_specs = [pl.BlockSpec((1, TM_PROJ // d, d * ATT_WIDTH), lambda b, i: (b, i, 0))
                 for d in DILATIONS] * 3
    gla_shapes = [jax.ShapeDtypeStruct((B, S, w), dt) for w, dt in
                  [(GLA_KEY_WIDTH, bf), (GLA_KEY_WIDTH, bf),
                   (GLA_VAL_WIDTH, bf), (GLA_VAL_WIDTH, bf), (GLA_KEY_WIDTH, jnp.float32)]]
    return pl.pallas_call(
        _in_proj_kernel,
        name="in_proj",
        grid=(B, nb),
        in_specs=[tok(D),
                  pl.BlockSpec((1, N_MOD, D), lambda b, i: (b, 0, 0)),
                  const((1, D)),
                  const((D, D_IN_PACKED)),
                  pl.BlockSpec((TM_PROJ, LANES), lambda b, i: (i, 0)),
                  pl.BlockSpec((TM_PROJ, LANES), lambda b, i: (i, 0)),
                  const((LANES, GLA_KEY_WIDTH)),
                  const((1, GLA_KEY_WIDTH))],
        out_specs=att_specs + [tok(s.shape[-1]) for s in gla_shapes],
        out_shape=att_shapes + gla_shapes,
        scratch_shapes=[pltpu.VMEM((ATT_WIDTH // LANES, TM_PROJ, LANES), jnp.float32)] * 3,
        compiler_params=pltpu.CompilerParams(dimension_semantics=("arbitrary", "arbitrary"),
                                             vmem_limit_bytes=VMEM_LIMIT),
    )(x, mod3, g_pre, w_packed, cos_t, sin_t, wlr, bg)


def _attn_kernel(q_ref, kp_ref, kc_ref, vp_ref, vc_ref, o_ref, st_ref):
    i = pl.program_id(2)
    tp = kp_ref.shape[1]
    tq = tp
    n_sub = q_ref.shape[1] // tq
    row = lax.broadcasted_iota(jnp.int32, (tq, tp + tq), 0)
    col = lax.broadcasted_iota(jnp.int32, (tq, tp + tq), 1)
    band = (col >= row) & (col <= row + ATT_SPAN)
    bias_inner = jnp.where(band, 0.0, NEG_BIG)
    bias_first = jnp.where(band & ((col >= tp) | (i > 0)), 0.0, NEG_BIG)
    lane = lax.broadcasted_iota(jnp.int32, (tq, LANES), 1)
    q_head = (lane >> 5) & 1
    for sb in range(n_sub):
        rows = slice(sb * tq, (sb + 1) * tq)
        bias = bias_first if sb == 0 else bias_inner
        stats = jnp.zeros((tq, LANES), jnp.float32)
        for hp in range(ATT_HEADS // 2):
            sl = slice(hp * LANES, (hp + 1) * LANES)
            q2 = q_ref[0, rows, sl]
            if sb == 0:
                kcat = jnp.concatenate([kp_ref[0, :, sl], kc_ref[0, rows, sl]], axis=0)
                vcat = jnp.concatenate([vp_ref[0, :, sl], vc_ref[0, rows, sl]], axis=0)
            else:
                kcat = kc_ref[0, (sb - 1) * tq:(sb + 1) * tq, sl]
                vcat = vc_ref[0, (sb - 1) * tq:(sb + 1) * tq, sl]
            outs = []
            for hh in range(2):
                qm = jnp.where(q_head == hh, q2, jnp.zeros_like(q2))
                s = lax.dot_general(qm, kcat, (((1,), (1,)), ((), ())),
                                    preferred_element_type=jnp.float32) + bias
                m = jnp.max(s, axis=1, keepdims=True)
                p = jnp.exp(s - m)
                l = jnp.sum(p, axis=1, keepdims=True)
                o = jnp.dot(p.astype(vcat.dtype), vcat, preferred_element_type=jnp.float32)
                outs.append(o / l)
                stats = jnp.where(lane == 2 * hp + hh, m + jnp.log(l), stats)
            o_ref[0, rows, sl] = jnp.where(lane < ATT_HEAD_DIM, outs[0], outs[1]).astype(o_ref.dtype)
        st_ref[0, rows, :] = stats


def _attn_pattern(qv, kv, vv, dil):
    B, L, _ = qv.shape
    W = ATT_WIDTH
    tq = min(TQ_ATT, L)
    nq = L // tq
    cur = pl.BlockSpec((1, tq, W), lambda b, r, i: (b, i, r))
    back = tq // ATT_SPAN
    prev = pl.BlockSpec((1, ATT_SPAN, W), lambda b, r, i: (b, jnp.maximum(i * back - 1, 0), r))
    o, st = pl.pallas_call(
        _attn_kernel,
        name=f"attn_d{dil}",
        grid=(B, dil, nq),
        in_specs=[cur, prev, cur, prev, cur],
        out_specs=[cur, pl.BlockSpec((1, tq, LANES), lambda b, r, i: (b, i, r))],
        out_shape=[jax.ShapeDtypeStruct((B, L, dil * W), jnp.bfloat16),
                   jax.ShapeDtypeStruct((B, L, dil * LANES), jnp.float32)],
        compiler_params=pltpu.CompilerParams(
            dimension_semantics=("arbitrary", "arbitrary", "arbitrary"),
            vmem_limit_bytes=VMEM_LIMIT),
    )(qv, kv, kv, vv, vv)
    return o, st


def _cumsum_rows(x):
    n = x.shape[0]
    row = lax.broadcasted_iota(jnp.int32, x.shape, 0)
    s = 1
    while s < n:
        x = x + jnp.where(row >= s, pltpu.roll(x, s, axis=0), 0.0)
        s *= 2
    return x


def _gla_kernel(q_ref, k_ref, v_ref, la_ref, sr_ref, g_ref, o_ref, st_ref):
    @pl.when(pl.program_id(0) == 0)
    def _():
        st_ref[...] = jnp.zeros_like(st_ref)

    C = GLA_CHUNK
    n_batch = q_ref.shape[0]
    n_chunks = q_ref.shape[1] // C
    lane = lax.broadcasted_iota(jnp.int32, (C, LANES), 1)
    lane_sq = lax.broadcasted_iota(jnp.int32, (LANES, LANES), 1)
    tril = (lax.broadcasted_iota(jnp.int32, (C, C), 0)
            >= lax.broadcasted_iota(jnp.int32, (C, C), 1))
    g = g_ref[...]
    bf = jnp.bfloat16

    def chunk(c, carry):
        r0 = pl.multiple_of(c * C, C)
        rows = pl.ds(r0, C)
        for n in range(n_batch):
            for p in range(GLA_HEADS // 2):
                ksl = slice(p * LANES, (p + 1) * LANES)
                b = _cumsum_rows(la_ref[n, rows, ksl])
                b_last = b[C - 1:C, :]
                q = q_ref[n, rows, ksl].astype(jnp.float32) * (GLA_KEY_DIM ** -0.5)
                k = k_ref[n, rows, ksl].astype(jnp.float32)
                q_dec = q * jnp.exp(b)
                k_inv = (k * jnp.exp(-b)).astype(bf)
                k_dec = (k * jnp.exp(b_last - b)).astype(bf)
                st = st_ref[n, p]
                st_b = st.astype(bf)
                ut = []
                for hh in range(2):
                    h = 2 * p + hh
                    vsl = slice(h * GLA_VAL_DIM, (h + 1) * GLA_VAL_DIM)
                    v = v_ref[n, rows, vsl]
                    own = (lane >= GLA_KEY_DIM) if hh else (lane < GLA_KEY_DIM)
                    qm = jnp.where(own, q_dec, 0.0).astype(bf)
                    att = lax.dot_general(qm, k_inv, (((1,), (1,)), ((), ())),
                                          preferred_element_type=jnp.float32)
                    att = jnp.where(tril, att, 0.0).astype(bf)
                    o = jnp.dot(att, v, preferred_element_type=jnp.float32)
                    o = o + lax.dot_general(qm, st_b, (((1,), (1,)), ((), ())),
                                            preferred_element_type=jnp.float32)
                    o = _rms(o, g) * sr_ref[n, rows, vsl].astype(jnp.float32)
                    o_ref[n, rows, vsl] = o.astype(o_ref.dtype)
                    ut.append(lax.dot_general(v, k_dec, (((0,), (0,)), ((), ())),
                                              preferred_element_type=jnp.float32))
                st_ref[n, p] = (st * jnp.exp(b_last)
                                + jnp.where(lane_sq < GLA_KEY_DIM, ut[0], ut[1]))
        return carry

    lax.fori_loop(0, n_chunks, chunk, 0, unroll=GLA_UNROLL)


def _gla(qg, kg, vg, la, sr, g_gla):
    B, S, _ = qg.shape
    tok = lambda w: pl.BlockSpec((B, TG_GLA, w), lambda i: (0, i, 0))
    return pl.pallas_call(
        _gla_kernel,
        name="gla",
        grid=(S // TG_GLA,),
        in_specs=[tok(GLA_KEY_WIDTH), tok(GLA_KEY_WIDTH), tok(GLA_VAL_WIDTH),
                  tok(GLA_KEY_WIDTH), tok(GLA_VAL_WIDTH),
                  pl.BlockSpec((1, GLA_VAL_DIM), lambda i: (0, 0))],
        out_specs=tok(GLA_VAL_WIDTH),
        out_shape=jax.ShapeDtypeStruct((B, S, GLA_VAL_WIDTH), jnp.bfloat16),
        scratch_shapes=[pltpu.VMEM((B, GLA_HEADS // 2, GLA_VAL_DIM, LANES), jnp.float32)],
        compiler_params=pltpu.CompilerParams(dimension_semantics=("arbitrary",),
                                             vmem_limit_bytes=VMEM_LIMIT),
    )(qg, kg, vg, la, sr, g_gla)


def _out_proj_kernel(o1_ref, o2_ref, o3_ref, s1_ref, s2_ref, s3_ref, og_ref, x_ref, mod_ref,
                     wout_ref, gpost_ref, gpre_ref, rw_ref, rb_ref,
                     x1_ref, h2_ref, ti_ref, tg_ref, oscr_ref, sscr_ref):
    tm = x_ref.shape[1]
    for j, (d, o_ref, s_ref) in enumerate(((DILATIONS[1], o2_ref, s2_ref),
                                           (DILATIONS[2], o3_ref, s3_ref))):
        for r in range(d):
            rows = pl.ds(r, tm // d, stride=d)
            for s in range(ATT_WIDTH // LANES):
                c0 = r * ATT_WIDTH + s * LANES
                oscr_ref[j, s, rows, :] = o_ref[0, :, c0:c0 + LANES].astype(jnp.float32)
            sscr_ref[j, rows, :] = s_ref[0, :, r * LANES:(r + 1) * LANES]
    gate1 = mod_ref[0, 2:3, :]
    shift2 = mod_ref[0, 3:4, :]
    scale2 = mod_ref[0, 4:5, :]
    bf = jnp.bfloat16
    rw = rw_ref[...]
    w_hi = rw.astype(bf)
    w_lo = (rw - w_hi.astype(jnp.float32)).astype(bf)
    ts = tm // OUT_SUBTILES
    lane = lax.broadcasted_iota(jnp.int32, (ts, LANES), 1)
    lane_f = lane.astype(jnp.float32)
    for sub in range(OUT_SUBTILES):
        rs = slice(sub * ts, (sub + 1) * ts)
        lses = [s1_ref[0, rs, :], sscr_ref[0, rs, :], sscr_ref[1, rs, :]]
        m = jnp.maximum(jnp.maximum(lses[0], lses[1]), lses[2])
        es = [jnp.exp(t - m) for t in lses]
        inv = 1.0 / (es[0] + es[1] + es[2])
        ws = [e * inv for e in es]
        pairs = []
        for hp in range(ATT_HEADS // 2):
            sl = slice(hp * LANES, (hp + 1) * LANES)
            o_pats = [o1_ref[0, rs, sl].astype(jnp.float32), oscr_ref[0, hp, rs, :],
                      oscr_ref[1, hp, rs, :]]
            acc = jnp.zeros((ts, LANES), jnp.float32)
            head_of_lane = 2 * hp + (lane >> 6)
            for w, o in zip(ws, o_pats):
                acc = acc + jnp.take_along_axis(w, head_of_lane, axis=1) * o
            pairs.append(acc.astype(bf))
        mixed = jnp.concatenate(pairs + [og_ref[0, rs, :]], axis=1)
        y = jnp.dot(mixed, wout_ref[...], preferred_element_type=jnp.float32)
        x1 = x_ref[0, rs, :] + gate1 * _rms(y, gpost_ref[...])
        x1_ref[0, rs, :] = x1
        h2 = _rms(x1, gpre_ref[...]) * (1.0 + scale2) + shift2
        h2_ref[0, rs, :] = _pack_bf16_pairs(h2)
        h_hi = h2.astype(bf)
        h_lo = (h2 - h_hi.astype(jnp.float32)).astype(bf)
        logits = (jnp.dot(h_hi, w_hi, preferred_element_type=jnp.float32)
                  + jnp.dot(h_lo, w_hi, preferred_element_type=jnp.float32)
                  + jnp.dot(h_hi, w_lo, preferred_element_type=jnp.float32)) + rb_ref[...]
        vals, idxs = [], []
        for _ in range(TOP_K):
            mk = jnp.max(logits, axis=1, keepdims=True)
            ik = jnp.min(jnp.where(logits == mk, lane_f, float(LANES)), axis=1, keepdims=True)
            logits = jnp.where(lane_f == ik, -jnp.inf, logits)
            vals.append(mk)
            idxs.append(ik)
        ex = [jnp.exp(v - vals[0]) for v in vals]
        den = ex[0] + ex[1] + ex[2] + ex[3]
        ti = jnp.zeros((ts, LANES), jnp.float32)
        tg = jnp.zeros((ts, LANES), jnp.float32)
        for kk in range(TOP_K):
            ti = jnp.where(lane == kk, idxs[kk], ti)
            tg = jnp.where(lane == kk, ex[kk] / den, tg)
        ti_ref[0, rs, :] = ti.astype(jnp.int32)
        tg_ref[0, rs, :] = tg


def _out_proj(o_pats, st_pats, og, x, mod3, w_out, g_post, g_pre, rw, rb):
    B, S, D = x.shape
    tok = lambda w: pl.BlockSpec((1, TM_OUT, w), lambda b, i: (b, i, 0))
    const = lambda shape: pl.BlockSpec(shape, lambda b, i: tuple(0 for _ in shape))
    out_shapes = [jax.ShapeDtypeStruct((B, S, D), jnp.float32),
                  jax.ShapeDtypeStruct((B, S, D // 2), jnp.uint32),
                  jax.ShapeDtypeStruct((B, S, LANES), jnp.int32),
                  jax.ShapeDtypeStruct((B, S, LANES), jnp.float32)]
    return pl.pallas_call(
        _out_proj_kernel,
        name="out_proj",
        grid=(B, S // TM_OUT),
        in_specs=[pl.BlockSpec((1, TM_OUT // d, d * ATT_WIDTH), lambda b, i: (b, i, 0))
                  for d in DILATIONS]
                 + [pl.BlockSpec((1, TM_OUT // d, d * LANES), lambda b, i: (b, i, 0))
                    for d in DILATIONS]
                 + [tok(GLA_VAL_WIDTH), tok(D),
                  pl.BlockSpec((1, N_MOD, D), lambda b, i: (b, 0, 0)),
                  const((D, D)), const((1, D)), const((1, D)),
                  const((D, LANES)), const((1, LANES))],
        out_specs=[tok(D), tok(D // 2), tok(LANES), tok(LANES)],
        out_shape=out_shapes,
        scratch_shapes=[pltpu.VMEM((2, ATT_WIDTH // LANES, TM_OUT, LANES), jnp.float32),
                        pltpu.VMEM((2, TM_OUT, LANES), jnp.float32)],
        compiler_params=pltpu.CompilerParams(dimension_semantics=("arbitrary", "arbitrary"),
                                             vmem_limit_bytes=VMEM_LIMIT),
    )(*o_pats, *st_pats, og, x, mod3, w_out, g_post, g_pre, rw, rb)


def _split_gate_up_kernel(w_ref, wg_ref, wl_ref):
    group = 2 * LANES
    src = lax.broadcasted_iota(jnp.int32, (group, group), 0)
    dst = lax.broadcasted_iota(jnp.int32, (group, group), 1)
    want = jnp.where(dst < LANES, 2 * dst, 2 * (dst - LANES) + 1)
    perm = jnp.where(src == want, 1.0, 0.0).astype(jnp.bfloat16)
    for j in range(w_ref.shape[1] // group):
        t = jnp.dot(w_ref[:, j * group:(j + 1) * group].astype(jnp.bfloat16), perm,
                    preferred_element_type=jnp.float32)
        wg_ref[:, j * LANES:(j + 1) * LANES] = t[:, :LANES].astype(wg_ref.dtype)
        wl_ref[:, j * LANES:(j + 1) * LANES] = t[:, LANES:].astype(wl_ref.dtype)


def _split_gate_up(w_gate_up):
    E, K, N2 = w_gate_up.shape
    out = jax.ShapeDtypeStruct((E, K, N2 // 2), jnp.bfloat16)
    ospec = pl.BlockSpec((None, TK_SPLIT, N2 // 2), lambda e, i: (e, i, 0))
    return pl.pallas_call(
        _split_gate_up_kernel,
        name="split_gate_up",
        grid=(E, K // TK_SPLIT),
        in_specs=[pl.BlockSpec((None, TK_SPLIT, N2), lambda e, i: (e, i, 0))],
        out_specs=[ospec, ospec],
        out_shape=[out, out],
        compiler_params=pltpu.CompilerParams(dimension_semantics=("arbitrary", "arbitrary"),
                                             vmem_limit_bytes=VMEM_LIMIT),
    )(w_gate_up)


def _moe_kernel(be_ref, nu_ref, x_ref, wg_ref, wl_ref, bg_ref, bl_ref, wd_ref, bd_ref, o_ref,
                wd_bf_ref):
    i = pl.program_id(0)
    new_expert = (i == 0) | (be_ref[i] != be_ref[jnp.maximum(i - 1, 0)])

    @pl.when(new_expert & (i < nu_ref[0]))
    def _():
        wd_bf_ref[...] = wd_ref[...].astype(wd_bf_ref.dtype)

    @pl.when(i < nu_ref[0])
    def _():
        lo, hi = _unpack_bf16_pairs(x_ref[...])
        x = jnp.concatenate([lo, hi], axis=1).astype(jnp.bfloat16)
        xg = jnp.dot(x, wg_ref[...], preferred_element_type=jnp.float32) + bg_ref[...]
        xl = jnp.dot(x, wl_ref[...], preferred_element_type=jnp.float32) + bl_ref[...]
        xg = jnp.minimum(xg, SWIGLU_LIMIT)
        xl = jnp.clip(xl, -SWIGLU_LIMIT, SWIGLU_LIMIT)
        act = xg * (1.0 / (1.0 + jnp.exp(-SWIGLU_ALPHA * xg))) * (xl + 1.0)
        out = jnp.dot(act.astype(jnp.bfloat16), wd_bf_ref[...],
                      preferred_element_type=jnp.float32) + bd_ref[...]
        o_ref[...] = _pack_bf16_pairs(out)

    @pl.when(i >= nu_ref[0])
    def _():
        o_ref[...] = jnp.zeros_like(o_ref)


def _moe(xs, blk_e, n_used, wg, wl, bg, bl, wd, bd):
    P, half = xs.shape
    D = 2 * half
    n_blocks = P // TM_MOE
    wspec = lambda k, n: pl.BlockSpec((None, k, n), lambda i, be, nu: (be[i], 0, 0))
    grid_spec = pltpu.PrefetchScalarGridSpec(
        num_scalar_prefetch=2,
        grid=(n_blocks,),
        in_specs=[pl.BlockSpec((TM_MOE, half), lambda i, be, nu: (i, 0)),
                  wspec(D, D_FF), wspec(D, D_FF), wspec(1, D_FF), wspec(1, D_FF),
                  wspec(D_FF, D), wspec(1, D)],
        out_specs=pl.BlockSpec((TM_MOE, half), lambda i, be, nu: (i, 0)),
        scratch_shapes=[pltpu.VMEM((D_FF, D), jnp.bfloat16)],
    )
    return pl.pallas_call(
        _moe_kernel,
        name="moe",
        grid_spec=grid_spec,
        out_shape=jax.ShapeDtypeStruct((P, half), jnp.uint32),
        compiler_params=pltpu.CompilerParams(dimension_semantics=("arbitrary",),
                                             vmem_limit_bytes=VMEM_LIMIT),
    )(blk_e, n_used, xs, wg, wl, bg, bl, wd, bd)


def _final_kernel(x1_ref, y0_ref, y1_ref, y2_ref, y3_ref, tg_ref, mod_ref, g_ref, o_ref):
    gates = tg_ref[...]
    half = y0_ref.shape[1]
    y_lo = jnp.zeros((x1_ref.shape[0], half), jnp.float32)
    y_hi = jnp.zeros((x1_ref.shape[0], half), jnp.float32)
    for kk, yk_ref in enumerate((y0_ref, y1_ref, y2_ref, y3_ref)):
        lo, hi = _unpack_bf16_pairs(yk_ref[...])
        y_lo = y_lo + gates[:, kk:kk + 1] * lo
        y_hi = y_hi + gates[:, kk:kk + 1] * hi
    y = jnp.concatenate([y_lo, y_hi], axis=1)
    gate2 = mod_ref[0, 5:6, :]
    o_ref[...] = x1_ref[...] + gate2 * _rms(y, g_ref[...])


def _final_part_kernel(x1_ref, y0_ref, y1_ref, y2_ref, y3_ref, tg_ref, mod_ref, g_ref, prev_ref,
                       o_ref):
    del prev_ref
    _final_kernel(x1_ref, y0_ref, y1_ref, y2_ref, y3_ref, tg_ref, mod_ref, g_ref, o_ref)


def _final(x1, yk, tg, mod3, g_post, seq_len, part, prev):
    T, D = x1.shape
    tp = yk.shape[0] // TOP_K
    nt = tp // TM_FIN
    t0 = part * nt
    batch = (part * tp) // seq_len
    yspec = lambda kk: pl.BlockSpec((TM_FIN, D // 2), lambda i: (kk * nt + i, 0))
    in_specs = ([pl.BlockSpec((TM_FIN, D), lambda i: (t0 + i, 0))]
                + [yspec(kk) for kk in range(TOP_K)]
                + [pl.BlockSpec((TM_FIN, LANES), lambda i: (t0 + i, 0)),
                   pl.BlockSpec((1, N_MOD, D), lambda i: (batch, 0, 0)),
                   pl.BlockSpec((1, D), lambda i: (0, 0))])
    args = [x1, yk, yk, yk, yk, tg, mod3, g_post]
    body, aliases = _final_kernel, {}
    if prev is not None:
        in_specs.append(pl.BlockSpec(memory_space=pl.ANY))
        args.append(prev)
        body, aliases = _final_part_kernel, {len(args) - 1: 0}
    return pl.pallas_call(
        body,
        name="final",
        grid=(nt,),
        in_specs=in_specs,
        out_specs=pl.BlockSpec((TM_FIN, D), lambda i: (t0 + i, 0)),
        out_shape=jax.ShapeDtypeStruct((T, D), jnp.float32),
        input_output_aliases=aliases,
        compiler_params=pltpu.CompilerParams(dimension_semantics=("arbitrary",),
                                             vmem_limit_bytes=VMEM_LIMIT),
    )(*args)


def _pack_w_in(w_in):
    half = ATT_HEAD_DIM // 2

    def pair_rotary_layout(w):
        k = w.shape[0]
        w = w.reshape(k, ATT_HEADS // 2, 2, 2, half)
        return w.transpose(0, 1, 3, 2, 4).reshape(k, ATT_WIDTH)

    w_in = w_in.astype(jnp.bfloat16)
    lr = jnp.pad(w_in[:, 3072:3072 + GLA_GATE_RANK], ((0, 0), (0, LANES - GLA_GATE_RANK)))
    return jnp.concatenate([pair_rotary_layout(w_in[:, :ATT_WIDTH]),
                            pair_rotary_layout(w_in[:, ATT_WIDTH:2 * ATT_WIDTH]),
                            w_in[:, 1024:3072], lr], axis=1)


def _rope_tables(seq_len):
    half = ATT_HEAD_DIM // 2
    inv_freq = ROPE_THETA ** (-jnp.arange(half, dtype=jnp.float32) / half)
    ang = jnp.arange(seq_len, dtype=jnp.float32)[:, None] * inv_freq[None, :]
    cos = jnp.tile(jnp.cos(ang), (1, LANES // half))
    sin = jnp.tile(jnp.sin(ang), (1, LANES // half))
    sign = jnp.where(jnp.arange(LANES) < LANES // 2, -1.0, 1.0)
    return cos, sin * sign


def _route_kernel(ti_ref, pos_ref, blk_ref, nu_ref, cnt_ref, base_ref, tri_ref):
    phase = pl.program_id(0)
    j = pl.program_id(1)
    tm = ti_ref.shape[0]
    lane = lax.broadcasted_iota(jnp.int32, (tm, LANES), 1)
    ti = ti_ref[...]
    hots = [lane == ti[:, k:k + 1] for k in range(TOP_K)]
    hot_all = jnp.zeros((tm, LANES), jnp.float32)
    for h in hots:
        hot_all = hot_all + jnp.where(h, 1.0, 0.0)
    tile_cnt = jnp.sum(hot_all, axis=0, keepdims=True)

    @pl.when((phase == 0) & (j == 0))
    def _():
        cnt_ref[...] = jnp.zeros_like(cnt_ref)
        r = lax.broadcasted_iota(jnp.int32, (tm, tm), 0)
        c = lax.broadcasted_iota(jnp.int32, (tm, tm), 1)
        tri_ref[...] = jnp.where(c < r, 1.0, 0.0).astype(tri_ref.dtype)

    @pl.when(phase == 0)
    def _():
        cnt_ref[...] += tile_cnt

    @pl.when((phase == 1) & (j == 0))
    def _():
        shift = TM_MOE.bit_length() - 1
        lane8 = lax.broadcasted_iota(jnp.int32, cnt_ref.shape, 1)
        cnt = cnt_ref[...].astype(jnp.int32)
        padded = ((cnt + (TM_MOE - 1)) >> shift) << shift
        pend = padded
        s = 1
        while s < N_EXPERTS:
            pend = pend + jnp.where(lane8 >= s, pltpu.roll(pend, s, axis=1), 0)
            s *= 2
        base_ref[...] = (pend - padded).astype(jnp.float32)
        nb = blk_ref.shape[0]
        blk_start = lax.broadcasted_iota(jnp.int32, (nb, LANES), 0) * TM_MOE
        lane_b = lax.broadcasted_iota(jnp.int32, (nb, LANES), 1)
        done = jnp.where((pend[0:1, :] <= blk_start) & (lane_b < N_EXPERTS), 1.0, 0.0)
        blk = jnp.minimum(jnp.sum(done, axis=1, keepdims=True), float(N_EXPERTS - 1))
        blk_ref[...] = jnp.broadcast_to(blk, blk_ref.shape).astype(jnp.int32)
        total = jnp.max(jnp.where(lane8 < N_EXPERTS, pend, 0).astype(jnp.float32),
                        axis=1, keepdims=True)
        nu_ref[...] = jnp.broadcast_to(total.astype(jnp.int32) >> shift, nu_ref.shape)

    @pl.when(phase == 1)
    def _():
        before = jnp.dot(tri_ref[...], hot_all.astype(tri_ref.dtype),
                         preferred_element_type=jnp.float32)
        slot = before + base_ref[0:1, :]
        pos = jnp.zeros((tm, LANES), jnp.float32)
        for k, h in enumerate(hots):
            pk = jnp.sum(jnp.where(h, slot, 0.0), axis=1, keepdims=True)
            pos = jnp.where(lane == k, pk, pos)
        pos_ref[...] = pos.astype(jnp.int32)
        base_ref[...] += tile_cnt


def _route(ti, n_tokens):
    A = n_tokens * TOP_K
    n_blocks = -(-(A + N_EXPERTS * (TM_MOE - 1)) // TM_MOE)
    nb_pad = -(-n_blocks // 8) * 8
    nt = n_tokens // TM_ROUTE
    pos, blk, nu = pl.pallas_call(
        _route_kernel,
        name="route",
        grid=(2, nt),
        in_specs=[pl.BlockSpec((TM_ROUTE, LANES), lambda p, j: (j, 0))],
        out_specs=[pl.BlockSpec((TM_ROUTE, LANES), lambda p, j: (j * p, 0)),
                   pl.BlockSpec((nb_pad, LANES), lambda p, j: (0, 0)),
                   pl.BlockSpec((8, LANES), lambda p, j: (0, 0))],
        out_shape=[jax.ShapeDtypeStruct((n_tokens, LANES), jnp.int32),
                   jax.ShapeDtypeStruct((nb_pad, LANES), jnp.int32),
                   jax.ShapeDtypeStruct((8, LANES), jnp.int32)],
        scratch_shapes=[pltpu.VMEM((8, LANES), jnp.float32),
                        pltpu.VMEM((8, LANES), jnp.float32),
                        pltpu.VMEM((TM_ROUTE, TM_ROUTE), jnp.bfloat16)],
        compiler_params=pltpu.CompilerParams(dimension_semantics=("arbitrary", "arbitrary"),
                                             vmem_limit_bytes=VMEM_LIMIT),
    )(ti)
    pos_kmajor = pos[:, :TOP_K].T
    return pos_kmajor, n_blocks * TM_MOE, blk[:n_blocks, 0], nu[0, :1]


def kernel(x, c, w_mod, b_mod, g_pre_mix, w_in, w_gate_lr, b_gate, g_gla, w_out, g_post_mix,
           g_pre_ffn, router_w, router_b, w_gate_up, b_gate_up, w_down, b_down, g_post_ffn):
    B, S, D = x.shape
    T = B * S
    bf = jnp.bfloat16
    cos_t, sin_t = _rope_tables(S)
    for l in range(w_mod.shape[0]):
        wg, wl = _split_gate_up(w_gate_up[l])
        wd = w_down[l]
        mod3 = _mod(c, w_mod[l], b_mod[l]).reshape(B, N_MOD, D)
        wlr = jnp.pad(w_gate_lr[l], ((0, LANES - GLA_GATE_RANK), (0, 0))).astype(bf)
        proj = _in_proj(x, mod3, g_pre_mix[l][None], _pack_w_in(w_in[l]), cos_t, sin_t, wlr,
                        b_gate[l][None])
        n_pat = len(DILATIONS)
        qa, ka, va = proj[:n_pat], proj[n_pat:2 * n_pat], proj[2 * n_pat:3 * n_pat]
        qg, kg, vg, sr, la = proj[3 * n_pat:]
        pats = [_attn_pattern(qa[j], ka[j], va[j], d) for j, d in enumerate(DILATIONS)]
        og = _gla(qg, kg, vg, la, sr, g_gla[l][None])
        rw = jnp.pad(router_w[l], ((0, 0), (0, LANES - N_EXPERTS)))
        rb = jnp.pad(router_b[l], (0, LANES - N_EXPERTS), constant_values=NEG_BIG)[None]
        x1, h2, ti, tg = _out_proj([p[0] for p in pats], [p[1] for p in pats], og, x, mod3,
                                   w_out[l].astype(bf), g_post_mix[l][None], g_pre_ffn[l][None],
                                   rw, rb)
        pos_kmajor, n_slots, blk_e, n_used = _route(ti.reshape(T, LANES), T)
        xs = _scatter_rows(h2.reshape(T, D // 2), pos_kmajor, n_slots)
        out_buf = _moe(xs, blk_e, n_used, wg, wl,
                       b_gate_up[l][:, None, 0::2], b_gate_up[l][:, None, 1::2],
                       wd, b_down[l][:, None, :])
        out = None
        tp = T // FINAL_PARTS
        for part in range(FINAL_PARTS):
            yk = _gather_rows(out_buf, pos_kmajor[:, part * tp:(part + 1) * tp].reshape(-1))
            out = _final(x1.reshape(T, D), yk, tg.reshape(T, LANES), mod3, g_post_ffn[l][None],
                         S, part, out)
        x = out.reshape(B, S, D)
    return x
```

```python
import functools

import numpy as np
import jax
import jax.numpy as jnp
from jax import lax
from jax.experimental import pallas as pl
from jax.experimental.pallas import tpu as pltpu
from jax.experimental.pallas import tpu_sc as plsc

D_MODEL = 1024
ATT_HEADS = 8
ATT_HEAD_DIM = 64
ATT_WIDTH = ATT_HEADS * ATT_HEAD_DIM
DILATIONS = (1, 4, 16)
ATT_SPAN = 128
ROPE_THETA = 10000.0
GLA_HEADS = 4
GLA_KEY_DIM = 64
GLA_VAL_DIM = 128
GLA_KEY_WIDTH = GLA_HEADS * GLA_KEY_DIM
GLA_VAL_WIDTH = GLA_HEADS * GLA_VAL_DIM
GLA_GATE_RANK = 16
GLA_GATE_NORMALIZER = 16.0
GLA_CHUNK = 64
N_EXPERTS = 32
TOP_K = 4
D_FF = D_MODEL
SWIGLU_LIMIT = 7.0
SWIGLU_ALPHA = 1.702
NORM_EPS = 1e-6
N_MOD = 6

LANES = 128
NEG_BIG = -1e30

TM_PROJ = 512
TQ_ATT = 1024
TG_GLA = 512
GLA_UNROLL = 8
TM_OUT = 512
OUT_SUBTILES = 1
TM_MOE = 512
TM_FIN = 512
TN_MOD = 512
TK_SPLIT = 512
TM_ROUTE = 1024
FINAL_PARTS = 4
SC_INDEX_WINDOW = 128
SC_GATHER_WINDOW = 64
VMEM_LIMIT = 48 * 1024 * 1024

_C_QA, _C_KA, _C_VA = 0, 512, 1024
_C_QG, _C_KG, _C_VG, _C_RG, _C_LR = 1536, 1792, 2048, 2560, 3072
D_IN_PACKED = 3200


def _rms(x, g):
    return x * lax.rsqrt(jnp.mean(x * x, axis=-1, keepdims=True) + NORM_EPS) * g


def _pack_bf16_pairs(x):
    n = x.shape[1] // 2
    u = lax.bitcast_convert_type(x.astype(jnp.bfloat16).astype(jnp.float32), jnp.uint32)
    return (u[:, :n] >> 16) | (u[:, n:] & jnp.uint32(0xFFFF0000))


def _unpack_bf16_pairs(w):
    lo = lax.bitcast_convert_type(w << 16, jnp.float32)
    hi = lax.bitcast_convert_type(w & jnp.uint32(0xFFFF0000), jnp.float32)
    return lo, hi


def _gather_rows(data, idx):
    n_rows = idx.shape[0]
    width = data.shape[1]
    mesh = plsc.VectorSubcoreMesh(core_axis_name="core", subcore_axis_name="subcore")
    n_workers = mesh.num_cores * mesh.num_subcores
    per_worker = n_rows // n_workers
    assert per_worker * n_workers == n_rows and per_worker % SC_INDEX_WINDOW == 0
    halves = SC_INDEX_WINDOW // SC_GATHER_WINDOW

    @pl.kernel(out_type=jax.ShapeDtypeStruct((n_rows, width), data.dtype), mesh=mesh,
               name="gather_rows",
               scratch_types=[pltpu.VMEM((1, SC_INDEX_WINDOW), jnp.int32),
                              pltpu.VMEM((SC_GATHER_WINDOW, width), data.dtype)])
    def gather(x_hbm, i_hbm, o_hbm, idx_vmem, rows_vmem):
        worker = lax.axis_index("core") * mesh.num_subcores + lax.axis_index("subcore")
        base = worker * per_worker

        @pl.loop(0, per_worker // SC_INDEX_WINDOW)
        def _(j):
            off = base + j * SC_INDEX_WINDOW
            pltpu.sync_copy(i_hbm.at[:, pl.ds(off, SC_INDEX_WINDOW)], idx_vmem)
            for h in range(halves):
                part = idx_vmem.at[0, pl.ds(h * SC_GATHER_WINDOW, SC_GATHER_WINDOW)]
                pltpu.sync_copy(x_hbm.at[part], rows_vmem)
                pltpu.sync_copy(rows_vmem,
                                o_hbm.at[pl.ds(off + h * SC_GATHER_WINDOW, SC_GATHER_WINDOW)])

    return gather(data, idx.reshape(1, n_rows))


def _scatter_rows(data, idx, n_out):
    n_copies, n_rows = idx.shape
    width = data.shape[1]
    mesh = plsc.VectorSubcoreMesh(core_axis_name="core", subcore_axis_name="subcore")
    n_workers = mesh.num_cores * mesh.num_subcores
    per_worker = n_rows // n_workers
    assert per_worker * n_workers == n_rows and per_worker % SC_INDEX_WINDOW == 0
    halves = SC_INDEX_WINDOW // SC_GATHER_WINDOW

    @pl.kernel(out_type=jax.ShapeDtypeStruct((n_out, width), data.dtype), mesh=mesh,
               name="scatter_rows",
               scratch_types=[pltpu.VMEM((n_copies, SC_INDEX_WINDOW), jnp.int32),
                              pltpu.VMEM((SC_GATHER_WINDOW, width), data.dtype)])
    def scatter(x_hbm, i_hbm, o_hbm, idx_vmem, rows_vmem):
        worker = lax.axis_index("core") * mesh.num_subcores + lax.axis_index("subcore")
        base = worker * per_worker

        @pl.loop(0, per_worker // SC_INDEX_WINDOW)
        def _(j):
            off = base + j * SC_INDEX_WINDOW
            pltpu.sync_copy(i_hbm.at[:, pl.ds(off, SC_INDEX_WINDOW)], idx_vmem)
            for h in range(halves):
                pltpu.sync_copy(x_hbm.at[pl.ds(off + h * SC_GATHER_WINDOW, SC_GATHER_WINDOW)],
                                rows_vmem)
                for k in range(n_copies):
                    part = idx_vmem.at[k, pl.ds(h * SC_GATHER_WINDOW, SC_GATHER_WINDOW)]
                    pltpu.sync_copy(rows_vmem, o_hbm.at[part])

    return scatter(data, idx)


def _mod_kernel(ct_ref, w_ref, b_ref, o_ref):
    ct = ct_ref[...]
    s = ct * (1.0 / (1.0 + jnp.exp(-ct)))
    w = w_ref[...]
    rows = [jnp.sum(s[:, b:b + 1] * w, axis=0, keepdims=True) for b in range(ct.shape[1])]
    o_ref[...] = jnp.concatenate(rows, axis=0) + b_ref[...]


def _mod(c, w_mod, b_mod):
    B = c.shape[0]
    n = w_mod.shape[1]
    return pl.pallas_call(
        _mod_kernel,
        name="mod",
        grid=(n // TN_MOD,),
        in_specs=[pl.BlockSpec((D_MODEL, B), lambda j: (0, 0)),
                  pl.BlockSpec((D_MODEL, TN_MOD), lambda j: (0, j)),
                  pl.BlockSpec((1, TN_MOD), lambda j: (0, j))],
        out_specs=pl.BlockSpec((B, TN_MOD), lambda j: (0, j)),
        out_shape=jax.ShapeDtypeStruct((B, n), jnp.float32),
        compiler_params=pltpu.CompilerParams(dimension_semantics=("arbitrary",),
                                             vmem_limit_bytes=VMEM_LIMIT),
    )(c.T, w_mod, b_mod.reshape(1, n))


def _store_residue_views(scr_ref, out_refs):
    ns, tm, _ = scr_ref.shape
    w = ns * LANES
    for d, ref in zip(DILATIONS, out_refs):
        for r in range(d):
            for s in range(ns):
                c0 = r * w + s * LANES
                rows = slice(None) if d == 1 else pl.ds(r, tm // d, stride=d)
                ref[0, :, c0:c0 + LANES] = scr_ref[s, rows, :].astype(ref.dtype)


def _in_proj_kernel(x_ref, mod_ref, g_ref, w_ref, cos_ref, sin_ref, wlr_ref, bg_ref,
                    qa1_ref, qa4_ref, qa16_ref, ka1_ref, ka4_ref, ka16_ref,
                    va1_ref, va4_ref, va16_ref, qg_ref, kg_ref, vg_ref, sr_ref, la_ref,
                    qs_ref, ks_ref, vs_ref):
    x = x_ref[0]
    shift = mod_ref[0, 0:1, :]
    scale = mod_ref[0, 1:2, :]
    h = (_rms(x, g_ref[...]) * (1.0 + scale) + shift).astype(jnp.bfloat16)

    def proj(c0, width):
        return jnp.dot(h, w_ref[:, c0:c0 + width], preferred_element_type=jnp.float32)

    cos = cos_ref[...]
    sin = sin_ref[...]

    def rope(c0, scr_ref, mult):
        p = proj(c0, ATT_WIDTH)
        for s in range(ATT_WIDTH // LANES):
            t = p[:, s * LANES:(s + 1) * LANES]
            r = t * cos + pltpu.roll(t, LANES // 2, axis=1) * sin
            if mult != 1.0:
                r = r * mult
            scr_ref[s] = r

    rope(_C_QA, qs_ref, ATT_HEAD_DIM ** -0.5)
    _store_residue_views(qs_ref, (qa1_ref, qa4_ref, qa16_ref))
    rope(_C_KA, ks_ref, 1.0)
    _store_residue_views(ks_ref, (ka1_ref, ka4_ref, ka16_ref))
    pv = proj(_C_VA, ATT_WIDTH)
    for s in range(ATT_WIDTH // LANES):
        vs_ref[s] = pv[:, s * LANES:(s + 1) * LANES]
    _store_residue_views(vs_ref, (va1_ref, va4_ref, va16_ref))
    qg_ref[0] = proj(_C_QG, GLA_KEY_WIDTH).astype(qg_ref.dtype)
    kg_ref[0] = proj(_C_KG, GLA_KEY_WIDTH).astype(kg_ref.dtype)
    vg_ref[0] = proj(_C_VG, GLA_VAL_WIDTH).astype(vg_ref.dtype)
    r = proj(_C_RG, GLA_VAL_WIDTH)
    sr_ref[0] = (r * (1.0 / (1.0 + jnp.exp(-r)))).astype(sr_ref.dtype)
    lr = proj(_C_LR, LANES).astype(jnp.bfloat16)
    z = jnp.dot(lr, wlr_ref[...], preferred_element_type=jnp.float32) + bg_ref[...]
    log_sig = jnp.minimum(z, 0.0) - jnp.log(1.0 + jnp.exp(-jnp.abs(z)))
    la_ref[0] = log_sig * (1.0 / GLA_GATE_NORMALIZER)


def _in_proj(x, mod3, g_pre, w_packed, cos_t, sin_t, wlr, bg):
    B, S, D = x.shape
    nb = S // TM_PROJ
    tok = lambda w: pl.BlockSpec((1, TM_PROJ, w), lambda b, i: (b, i, 0))
    const = lambda shape: pl.BlockSpec(shape, lambda b, i: tuple(0 for _ in shape))
    bf = jnp.bfloat16
    att_shapes = [jax.ShapeDtypeStruct((B, S // d, d * ATT_WIDTH), bf) for d in DILATIONS] * 3
    att_specs = [pl.BlockSpec((1, TM_PROJ // d, d * ATT_WIDTH), lambda b, i: (b, i, 0))
                 for d in DILATIONS] * 3
    gla_shapes = [jax.ShapeDtypeStruct((B, S, w), dt) for w, dt in
                  [(GLA_KEY_WIDTH, bf), (GLA_KEY_WIDTH, bf),
                   (GLA_VAL_WIDTH, bf), (GLA_VAL_WIDTH, bf), (GLA_KEY_WIDTH, jnp.float32)]]
    return pl.pallas_call(
        _in_proj_kernel,
        name="in_proj",
        grid=(B, nb),
        in_specs=[tok(D),
                  pl.BlockSpec((1, N_MOD, D), lambda b, i: (b, 0, 0)),
                  const((1, D)),
                  const((D, D_IN_PACKED)),
                  pl.BlockSpec((TM_PROJ, LANES), lambda b, i: (i, 0)),
                  pl.BlockSpec((TM_PROJ, LANES), lambda b, i: (i, 0)),
                  const((LANES, GLA_KEY_WIDTH)),
                  const((1, GLA_KEY_WIDTH))],
        out_specs=att_specs + [tok(s.shape[-1]) for s in gla_shapes],
        out_shape=att_shapes + gla_shapes,
        scratch_shapes=[pltpu.VMEM((ATT_WIDTH // LANES, TM_PROJ, LANES), jnp.float32)] * 3,
        compiler_params=pltpu.CompilerParams(dimension_semantics=("arbitrary", "arbitrary"),
                                             vmem_limit_bytes=VMEM_LIMIT),
    )(x, mod3, g_pre, w_packed, cos_t, sin_t, wlr, bg)


def _attn_kernel(q_ref, kp_ref, kc_ref, vp_ref, vc_ref, o_ref, st_ref):
    i = pl.program_id(2)
    tp = kp_ref.shape[1]
    tq = tp
    n_sub = q_ref.shape[1] // tq
    row = lax.broadcasted_iota(jnp.int32, (tq, tp + tq), 0)
    col = lax.broadcasted_iota(jnp.int32, (tq, tp + tq), 1)
    band = (col >= row) & (col <= row + ATT_SPAN)
    bias_inner = jnp.where(band, 0.0, NEG_BIG)
    bias_first = jnp.where(band & ((col >= tp) | (i > 0)), 0.0, NEG_BIG)
    lane = lax.broadcasted_iota(jnp.int32, (tq, LANES), 1)
    q_head = (lane >> 5) & 1
    for sb in range(n_sub):
        rows = slice(sb * tq, (sb + 1) * tq)
        bias = bias_first if sb == 0 else bias_inner
        stats = jnp.zeros((tq, LANES), jnp.float32)
        for hp in range(ATT_HEADS // 2):
            sl = slice(hp * LANES, (hp + 1) * LANES)
            q2 = q_ref[0, rows, sl]
            if sb == 0:
                kcat = jnp.concatenate([kp_ref[0, :, sl], kc_ref[0, rows, sl]], axis=0)
                vcat = jnp.concatenate([vp_ref[0, :, sl], vc_ref[0, rows, sl]], axis=0)
            else:
                kcat = kc_ref[0, (sb - 1) * tq:(sb + 1) * tq, sl]
                vcat = vc_ref[0, (sb - 1) * tq:(sb + 1) * tq, sl]
            outs = []
            for hh in range(2):
                qm = jnp.where(q_head == hh, q2, jnp.zeros_like(q2))
                s = lax.dot_general(qm, kcat, (((1,), (1,)), ((), ())),
                                    preferred_element_type=jnp.float32) + bias
                m = jnp.max(s, axis=1, keepdims=True)
                p = jnp.exp(s - m)
                l = jnp.sum(p, axis=1, keepdims=True)
                o = jnp.dot(p.astype(vcat.dtype), vcat, preferred_element_type=jnp.float32)
                outs.append(o / l)
                stats = jnp.where(lane == 2 * hp + hh, m + jnp.log(l), stats)
            o_ref[0, rows, sl] = jnp.where(lane < ATT_HEAD_DIM, outs[0], outs[1]).astype(o_ref.dtype)
        st_ref[0, rows, :] = stats


def _attn_pattern(qv, kv, vv, dil):
    B, L, _ = qv.shape
    W = ATT_WIDTH
    tq = min(TQ_ATT, L)
    nq = L // tq
    cur = pl.BlockSpec((1, tq, W), lambda b, r, i: (b, i, r))
    back = tq // ATT_SPAN
    prev = pl.BlockSpec((1, ATT_SPAN, W), lambda b, r, i: (b, jnp.maximum(i * back - 1, 0), r))
    o, st = pl.pallas_call(
        _attn_kernel,
        name=f"attn_d{dil}",
        grid=(B, dil, nq),
        in_specs=[cur, prev, cur, prev, cur],
        out_specs=[cur, pl.BlockSpec((1, tq, LANES), lambda b, r, i: (b, i, r))],
        out_shape=[jax.ShapeDtypeStruct((B, L, dil * W), jnp.bfloat16),
                   jax.ShapeDtypeStruct((B, L, dil * LANES), jnp.float32)],
        compiler_params=pltpu.CompilerParams(
            dimension_semantics=("arbitrary", "arbitrary", "arbitrary"),
            vmem_limit_bytes=VMEM_LIMIT),
    )(qv, kv, kv, vv, vv)
    return o, st


def _cumsum_rows(x):
    n = x.shape[0]
    row = lax.broadcasted_iota(jnp.int32, x.shape, 0)
    s = 1
    while s < n:
        x = x + jnp.where(row >= s, pltpu.roll(x, s, axis=0), 0.0)
        s *= 2
    return x


def _gla_kernel(q_ref, k_ref, v_ref, la_ref, sr_ref, g_ref, o_ref, st_ref):
    @pl.when(pl.program_id(0) == 0)
    def _():
        st_ref[...] = jnp.zeros_like(st_ref)

    C = GLA_CHUNK
    n_batch = q_ref.shape[0]
    n_chunks = q_ref.shape[1] // C
    lane = lax.broadcasted_iota(jnp.int32, (C, LANES), 1)
    lane_sq = lax.broadcasted_iota(jnp.int32, (LANES, LANES), 1)
    tril = (lax.broadcasted_iota(jnp.int32, (C, C), 0)
            >= lax.broadcasted_iota(jnp.int32, (C, C), 1))
    g = g_ref[...]
    bf = jnp.bfloat16

    def chunk(c, carry):
        r0 = pl.multiple_of(c * C, C)
        rows = pl.ds(r0, C)
        for n in range(n_batch):
            for p in range(GLA_HEADS // 2):
                ksl = slice(p * LANES, (p + 1) * LANES)
                b = _cumsum_rows(la_ref[n, rows, ksl])
                b_last = b[C - 1:C, :]
                q = q_ref[n, rows, ksl].astype(jnp.float32) * (GLA_KEY_DIM ** -0.5)
                k = k_ref[n, rows, ksl].astype(jnp.float32)
                q_dec = q * jnp.exp(b)
                k_inv = (k * jnp.exp(-b)).astype(bf)
                k_dec = (k * jnp.exp(b_last - b)).astype(bf)
                st = st_ref[n, p]
                st_b = st.astype(bf)
                ut = []
                for hh in range(2):
                    h = 2 * p + hh
                    vsl = slice(h * GLA_VAL_DIM, (h + 1) * GLA_VAL_DIM)
                    v = v_ref[n, rows, vsl]
                    own = (lane >= GLA_KEY_DIM) if hh else (lane < GLA_KEY_DIM)
                    qm = jnp.where(own, q_dec, 0.0).astype(bf)
                    att = lax.dot_general(qm, k_inv, (((1,), (1,)), ((), ())),
                                          preferred_element_type=jnp.float32)
                    att = jnp.where(tril, att, 0.0).astype(bf)
                    o = jnp.dot(att, v, preferred_element_type=jnp.float32)
                    o = o + lax.dot_general(qm, st_b, (((1,), (1,)), ((), ())),
                                            preferred_element_type=jnp.float32)
                    o = _rms(o, g) * sr_ref[n, rows, vsl].astype(jnp.float32)
                    o_ref[n, rows, vsl] = o.astype(o_ref.dtype)
                    ut.append(lax.dot_general(v, k_dec, (((0,), (0,)), ((), ())),
                                              preferred_element_type=jnp.float32))
                st_ref[n, p] = (st * jnp.exp(b_last)
                                + jnp.where(lane_sq < GLA_KEY_DIM, ut[0], ut[1]))
        return carry

    lax.fori_loop(0, n_chunks, chunk, 0, unroll=GLA_UNROLL)


def _gla(qg, kg, vg, la, sr, g_gla):
    B, S, _ = qg.shape
    tok = lambda w: pl.BlockSpec((B, TG_GLA, w), lambda i: (0, i, 0))
    return pl.pallas_call(
        _gla_kernel,
        name="gla",
        grid=(S // TG_GLA,),
        in_specs=[tok(GLA_KEY_WIDTH), tok(GLA_KEY_WIDTH), tok(GLA_VAL_WIDTH),
                  tok(GLA_KEY_WIDTH), tok(GLA_VAL_WIDTH),
                  pl.BlockSpec((1, GLA_VAL_DIM), lambda i: (0, 0))],
        out_specs=tok(GLA_VAL_WIDTH),
        out_shape=jax.ShapeDtypeStruct((B, S, GLA_VAL_WIDTH), jnp.bfloat16),
        scratch_shapes=[pltpu.VMEM((B, GLA_HEADS // 2, GLA_VAL_DIM, LANES), jnp.float32)],
        compiler_params=pltpu.CompilerParams(dimension_semantics=("arbitrary",),
                                             vmem_limit_bytes=VMEM_LIMIT),
    )(qg, kg, vg, la, sr, g_gla)


def _out_proj_kernel(o1_ref, o2_ref, o3_ref, s1_ref, s2_ref, s3_ref, og_ref, x_ref, mod_ref,
                     wout_ref, gpost_ref, gpre_ref, rw_ref, rb_ref,
                     x1_ref, h2_ref, ti_ref, tg_ref, cnt_ref, oscr_ref, sscr_ref):
    tm = x_ref.shape[1]

    @pl.when((pl.program_id(0) == 0) & (pl.program_id(1) == 0))
    def _():
        cnt_ref[...] = jnp.zeros_like(cnt_ref)

    for j, (d, o_ref, s_ref) in enumerate(((DILATIONS[1], o2_ref, s2_ref),
                                           (DILATIONS[2], o3_ref, s3_ref))):
        for r in range(d):
            rows = pl.ds(r, tm // d, stride=d)
            for s in range(ATT_WIDTH // LANES):
                c0 = r * ATT_WIDTH + s * LANES
                oscr_ref[j, s, rows, :] = o_ref[0, :, c0:c0 + LANES].astype(jnp.float32)
            sscr_ref[j, rows, :] = s_ref[0, :, r * LANES:(r + 1) * LANES]
    gate1 = mod_ref[0, 2:3, :]
    shift2 = mod_ref[0, 3:4, :]
    scale2 = mod_ref[0, 4:5, :]
    bf = jnp.bfloat16
    rw = rw_ref[...]
    w_hi = rw.astype(bf)
    w_lo = (rw - w_hi.astype(jnp.float32)).astype(bf)
    ts = tm // OUT_SUBTILES
    lane = lax.broadcasted_iota(jnp.int32, (ts, LANES), 1)
    lane_f = lane.astype(jnp.float32)
    for sub in range(OUT_SUBTILES):
        rs = slice(sub * ts, (sub + 1) * ts)
        lses = [s1_ref[0, rs, :], sscr_ref[0, rs, :], sscr_ref[1, rs, :]]
        m = jnp.maximum(jnp.maximum(lses[0], lses[1]), lses[2])
        es = [jnp.exp(t - m) for t in lses]
        inv = 1.0 / (es[0] + es[1] + es[2])
        ws = [e * inv for e in es]
        pairs = []
        for hp in range(ATT_HEADS // 2):
            sl = slice(hp * LANES, (hp + 1) * LANES)
            o_pats = [o1_ref[0, rs, sl].astype(jnp.float32), oscr_ref[0, hp, rs, :],
                      oscr_ref[1, hp, rs, :]]
            acc = jnp.zeros((ts, LANES), jnp.float32)
            head_of_lane = 2 * hp + (lane >> 6)
            for w, o in zip(ws, o_pats):
                acc = acc + jnp.take_along_axis(w, head_of_lane, axis=1) * o
            pairs.append(acc.astype(bf))
        mixed = jnp.concatenate(pairs + [og_ref[0, rs, :]], axis=1)
        y = jnp.dot(mixed, wout_ref[...], preferred_element_type=jnp.float32)
        x1 = x_ref[0, rs, :] + gate1 * _rms(y, gpost_ref[...])
        x1_ref[0, rs, :] = x1
        h2 = _rms(x1, gpre_ref[...]) * (1.0 + scale2) + shift2
        h2_ref[0, rs, :] = _pack_bf16_pairs(h2)
        h_hi = h2.astype(bf)
        h_lo = (h2 - h_hi.astype(jnp.float32)).astype(bf)
        logits = (jnp.dot(h_hi, w_hi, preferred_element_type=jnp.float32)
                  + jnp.dot(h_lo, w_hi, preferred_element_type=jnp.float32)
                  + jnp.dot(h_hi, w_lo, preferred_element_type=jnp.float32)) + rb_ref[...]
        vals, idxs = [], []
        for _ in range(TOP_K):
            mk = jnp.max(logits, axis=1, keepdims=True)
            ik = jnp.min(jnp.where(logits == mk, lane_f, float(LANES)), axis=1, keepdims=True)
            logits = jnp.where(lane_f == ik, -jnp.inf, logits)
            vals.append(mk)
            idxs.append(ik)
        ex = [jnp.exp(v - vals[0]) for v in vals]
        den = ex[0] + ex[1] + ex[2] + ex[3]
        ti = jnp.zeros((ts, LANES), jnp.float32)
        tg = jnp.zeros((ts, LANES), jnp.float32)
        for kk in range(TOP_K):
            ti = jnp.where(lane == kk, idxs[kk], ti)
            tg = jnp.where(lane == kk, ex[kk] / den, tg)
        ti_ref[0, rs, :] = ti.astype(jnp.int32)
        tg_ref[0, rs, :] = tg
        chosen = jnp.zeros((ts, LANES), jnp.float32)
        for ik in idxs:
            chosen = chosen + jnp.where(lane_f == ik, 1.0, 0.0)
        cnt_ref[...] += jnp.sum(chosen, axis=0, keepdims=True)


def _out_proj(o_pats, st_pats, og, x, mod3, w_out, g_post, g_pre, rw, rb):
    B, S, D = x.shape
    tok = lambda w: pl.BlockSpec((1, TM_OUT, w), lambda b, i: (b, i, 0))
    const = lambda shape: pl.BlockSpec(shape, lambda b, i: tuple(0 for _ in shape))
    out_shapes = [jax.ShapeDtypeStruct((B, S, D), jnp.float32),
                  jax.ShapeDtypeStruct((B, S, D // 2), jnp.uint32),
                  jax.ShapeDtypeStruct((B, S, LANES), jnp.int32),
                  jax.ShapeDtypeStruct((B, S, LANES), jnp.float32),
                  jax.ShapeDtypeStruct((8, LANES), jnp.float32)]
    return pl.pallas_call(
        _out_proj_kernel,
        name="out_proj",
        grid=(B, S // TM_OUT),
        in_specs=[pl.BlockSpec((1, TM_OUT // d, d * ATT_WIDTH), lambda b, i: (b, i, 0))
                  for d in DILATIONS]
                 + [pl.BlockSpec((1, TM_OUT // d, d * LANES), lambda b, i: (b, i, 0))
                    for d in DILATIONS]
                 + [tok(GLA_VAL_WIDTH), tok(D),
                  pl.BlockSpec((1, N_MOD, D), lambda b, i: (b, 0, 0)),
                  const((D, D)), const((1, D)), const((1, D)),
                  const((D, LANES)), const((1, LANES))],
        out_specs=[tok(D), tok(D // 2), tok(LANES), tok(LANES), const((8, LANES))],
        out_shape=out_shapes,
        scratch_shapes=[pltpu.VMEM((2, ATT_WIDTH // LANES, TM_OUT, LANES), jnp.float32),
                        pltpu.VMEM((2, TM_OUT, LANES), jnp.float32)],
        compiler_params=pltpu.CompilerParams(dimension_semantics=("arbitrary", "arbitrary"),
                                             vmem_limit_bytes=VMEM_LIMIT),
    )(*o_pats, *st_pats, og, x, mod3, w_out, g_post, g_pre, rw, rb)


def _split_gate_up_kernel(w_ref, wg_ref, wl_ref):
    group = 2 * LANES
    src = lax.broadcasted_iota(jnp.int32, (group, group), 0)
    dst = lax.broadcasted_iota(jnp.int32, (group, group), 1)
    want = jnp.where(dst < LANES, 2 * dst, 2 * (dst - LANES) + 1)
    perm = jnp.where(src == want, 1.0, 0.0).astype(jnp.bfloat16)
    for j in range(w_ref.shape[1] // group):
        t = jnp.dot(w_ref[:, j * group:(j + 1) * group].astype(jnp.bfloat16), perm,
                    preferred_element_type=jnp.float32)
        wg_ref[:, j * LANES:(j + 1) * LANES] = t[:, :LANES].astype(wg_ref.dtype)
        wl_ref[:, j * LANES:(j + 1) * LANES] = t[:, LANES:].astype(wl_ref.dtype)


def _split_gate_up(w_gate_up):
    E, K, N2 = w_gate_up.shape
    out = jax.ShapeDtypeStruct((E, K, N2 // 2), jnp.bfloat16)
    ospec = pl.BlockSpec((None, TK_SPLIT, N2 // 2), lambda e, i: (e, i, 0))
    return pl.pallas_call(
        _split_gate_up_kernel,
        name="split_gate_up",
        grid=(E, K // TK_SPLIT),
        in_specs=[pl.BlockSpec((None, TK_SPLIT, N2), lambda e, i: (e, i, 0))],
        out_specs=[ospec, ospec],
        out_shape=[out, out],
        compiler_params=pltpu.CompilerParams(dimension_semantics=("arbitrary", "arbitrary"),
                                             vmem_limit_bytes=VMEM_LIMIT),
    )(w_gate_up)


def _moe_kernel(be_ref, nu_ref, x_ref, wg_ref, wl_ref, bg_ref, bl_ref, wd_ref, bd_ref, o_ref,
                wd_bf_ref):
    i = pl.program_id(0)
    new_expert = (i == 0) | (be_ref[i] != be_ref[jnp.maximum(i - 1, 0)])

    @pl.when(new_expert & (i < nu_ref[0]))
    def _():
        wd_bf_ref[...] = wd_ref[...].astype(wd_bf_ref.dtype)

    @pl.when(i < nu_ref[0])
    def _():
        lo, hi = _unpack_bf16_pairs(x_ref[...])
        x = jnp.concatenate([lo, hi], axis=1).astype(jnp.bfloat16)
        xg = jnp.dot(x, wg_ref[...], preferred_element_type=jnp.float32) + bg_ref[...]
        xl = jnp.dot(x, wl_ref[...], preferred_element_type=jnp.float32) + bl_ref[...]
        xg = jnp.minimum(xg, SWIGLU_LIMIT)
        xl = jnp.clip(xl, -SWIGLU_LIMIT, SWIGLU_LIMIT)
        act = xg * (1.0 / (1.0 + jnp.exp(-SWIGLU_ALPHA * xg))) * (xl + 1.0)
        out = jnp.dot(act.astype(jnp.bfloat16), wd_bf_ref[...],
                      preferred_element_type=jnp.float32) + bd_ref[...]
        o_ref[...] = _pack_bf16_pairs(out)

    @pl.when(i >= nu_ref[0])
    def _():
        o_ref[...] = jnp.zeros_like(o_ref)


def _moe(xs, blk_e, n_used, wg, wl, bg, bl, wd, bd):
    P, half = xs.shape
    D = 2 * half
    n_blocks = P // TM_MOE
    wspec = lambda k, n: pl.BlockSpec((None, k, n), lambda i, be, nu: (be[i], 0, 0))
    grid_spec = pltpu.PrefetchScalarGridSpec(
        num_scalar_prefetch=2,
        grid=(n_blocks,),
        in_specs=[pl.BlockSpec((TM_MOE, half), lambda i, be, nu: (i, 0)),
                  wspec(D, D_FF), wspec(D, D_FF), wspec(1, D_FF), wspec(1, D_FF),
                  wspec(D_FF, D), wspec(1, D)],
        out_specs=pl.BlockSpec((TM_MOE, half), lambda i, be, nu: (i, 0)),
        scratch_shapes=[pltpu.VMEM((D_FF, D), jnp.bfloat16)],
    )
    return pl.pallas_call(
        _moe_kernel,
        name="moe",
        grid_spec=grid_spec,
        out_shape=jax.ShapeDtypeStruct((P, half), jnp.uint32),
        compiler_params=pltpu.CompilerParams(dimension_semantics=("arbitrary",),
                                             vmem_limit_bytes=VMEM_LIMIT),
    )(blk_e, n_used, xs, wg, wl, bg, bl, wd, bd)


def _final_kernel(x1_ref, y0_ref, y1_ref, y2_ref, y3_ref, tg_ref, mod_ref, g_ref, o_ref):
    gates = tg_ref[...]
    half = y0_ref.shape[1]
    y_lo = jnp.zeros((x1_ref.shape[0], half), jnp.float32)
    y_hi = jnp.zeros((x1_ref.shape[0], half), jnp.float32)
    for kk, yk_ref in enumerate((y0_ref, y1_ref, y2_ref, y3_ref)):
        lo, hi = _unpack_bf16_pairs(yk_ref[...])
        y_lo = y_lo + gates[:, kk:kk + 1] * lo
        y_hi = y_hi + gates[:, kk:kk + 1] * hi
    y = jnp.concatenate([y_lo, y_hi], axis=1)
    gate2 = mod_ref[0, 5:6, :]
    o_ref[...] = x1_ref[...] + gate2 * _rms(y, g_ref[...])


def _final_part_kernel(x1_ref, y0_ref, y1_ref, y2_ref, y3_ref, tg_ref, mod_ref, g_ref, prev_ref,
                       o_ref):
    del prev_ref
    _final_kernel(x1_ref, y0_ref, y1_ref, y2_ref, y3_ref, tg_ref, mod_ref, g_ref, o_ref)


def _final(x1, yk, tg, mod3, g_post, seq_len, part, prev):
    T, D = x1.shape
    tp = yk.shape[0] // TOP_K
    nt = tp // TM_FIN
    t0 = part * nt
    batch = (part * tp) // seq_len
    yspec = lambda kk: pl.BlockSpec((TM_FIN, D // 2), lambda i: (kk * nt + i, 0))
    in_specs = ([pl.BlockSpec((TM_FIN, D), lambda i: (t0 + i, 0))]
                + [yspec(kk) for kk in range(TOP_K)]
                + [pl.BlockSpec((TM_FIN, LANES), lambda i: (t0 + i, 0)),
                   pl.BlockSpec((1, N_MOD, D), lambda i: (batch, 0, 0)),
                   pl.BlockSpec((1, D), lambda i: (0, 0))])
    args = [x1, yk, yk, yk, yk, tg, mod3, g_post]
    body, aliases = _final_kernel, {}
    if prev is not None:
        in_specs.append(pl.BlockSpec(memory_space=pl.ANY))
        args.append(prev)
        body, aliases = _final_part_kernel, {len(args) - 1: 0}
    return pl.pallas_call(
        body,
        name="final",
        grid=(nt,),
        in_specs=in_specs,
        out_specs=pl.BlockSpec((TM_FIN, D), lambda i: (t0 + i, 0)),
        out_shape=jax.ShapeDtypeStruct((T, D), jnp.float32),
        input_output_aliases=aliases,
        compiler_params=pltpu.CompilerParams(dimension_semantics=("arbitrary",),
                                             vmem_limit_bytes=VMEM_LIMIT),
    )(*args)


def _pack_w_in(w_in):
    half = ATT_HEAD_DIM // 2

    def pair_rotary_layout(w):
        k = w.shape[0]
        w = w.reshape(k, ATT_HEADS // 2, 2, 2, half)
        return w.transpose(0, 1, 3, 2, 4).reshape(k, ATT_WIDTH)

    w_in = w_in.astype(jnp.bfloat16)
    lr = jnp.pad(w_in[:, 3072:3072 + GLA_GATE_RANK], ((0, 0), (0, LANES - GLA_GATE_RANK)))
    return jnp.concatenate([pair_rotary_layout(w_in[:, :ATT_WIDTH]),
                            pair_rotary_layout(w_in[:, ATT_WIDTH:2 * ATT_WIDTH]),
                            w_in[:, 1024:3072], lr], axis=1)


def _rope_tables(seq_len):
    half = ATT_HEAD_DIM // 2
    inv_freq = ROPE_THETA ** (-jnp.arange(half, dtype=jnp.float32) / half)
    ang = jnp.arange(seq_len, dtype=jnp.float32)[:, None] * inv_freq[None, :]
    cos = jnp.tile(jnp.cos(ang), (1, LANES // half))
    sin = jnp.tile(jnp.sin(ang), (1, LANES // half))
    sign = jnp.where(jnp.arange(LANES) < LANES // 2, -1.0, 1.0)
    return cos, sin * sign


def _route_kernel(ti_ref, cnt_ref, pos_ref, blk_ref, nu_ref, base_ref, tri_ref):
    j = pl.program_id(0)
    tm = ti_ref.shape[0]
    lane = lax.broadcasted_iota(jnp.int32, (tm, LANES), 1)
    ti = ti_ref[...]
    hots = [lane == ti[:, k:k + 1] for k in range(TOP_K)]
    hot_all = jnp.zeros((tm, LANES), jnp.float32)
    for h in hots:
        hot_all = hot_all + jnp.where(h, 1.0, 0.0)
    tile_cnt = jnp.sum(hot_all, axis=0, keepdims=True)

    @pl.when(j == 0)
    def _():
        r = lax.broadcasted_iota(jnp.int32, (tm, tm), 0)
        c = lax.broadcasted_iota(jnp.int32, (tm, tm), 1)
        tri_ref[...] = jnp.where(c < r, 1.0, 0.0).astype(tri_ref.dtype)
        shift = TM_MOE.bit_length() - 1
        lane8 = lax.broadcasted_iota(jnp.int32, cnt_ref.shape, 1)
        cnt = cnt_ref[...].astype(jnp.int32)
        padded = ((cnt + (TM_MOE - 1)) >> shift) << shift
        pend = padded
        s = 1
        while s < N_EXPERTS:
            pend = pend + jnp.where(lane8 >= s, pltpu.roll(pend, s, axis=1), 0)
            s *= 2
        base_ref[...] = (pend - padded).astype(jnp.float32)
        nb = blk_ref.shape[0]
        blk_start = lax.broadcasted_iota(jnp.int32, (nb, LANES), 0) * TM_MOE
        lane_b = lax.broadcasted_iota(jnp.int32, (nb, LANES), 1)
        done = jnp.where((pend[0:1, :] <= blk_start) & (lane_b < N_EXPERTS), 1.0, 0.0)
        blk = jnp.minimum(jnp.sum(done, axis=1, keepdims=True), float(N_EXPERTS - 1))
        blk_ref[...] = jnp.broadcast_to(blk, blk_ref.shape).astype(jnp.int32)
        total = jnp.max(jnp.where(lane8 < N_EXPERTS, pend, 0).astype(jnp.float32),
                        axis=1, keepdims=True)
        nu_ref[...] = jnp.broadcast_to(total.astype(jnp.int32) >> shift, nu_ref.shape)

    before = jnp.dot(tri_ref[...], hot_all.astype(tri_ref.dtype),
                     preferred_element_type=jnp.float32)
    slot = before + base_ref[0:1, :]
    pos = jnp.zeros((tm, LANES), jnp.float32)
    for k, h in enumerate(hots):
        pk = jnp.sum(jnp.where(h, slot, 0.0), axis=1, keepdims=True)
        pos = jnp.where(lane == k, pk, pos)
    pos_ref[...] = pos.astype(jnp.int32)
    base_ref[...] += tile_cnt


def _route(ti, cnt, n_tokens):
    A = n_tokens * TOP_K
    n_blocks = -(-(A + N_EXPERTS * (TM_MOE - 1)) // TM_MOE)
    nb_pad = -(-n_blocks // 8) * 8
    nt = n_tokens // TM_ROUTE
    pos, blk, nu = pl.pallas_call(
        _route_kernel,
        name="route",
        grid=(nt,),
        in_specs=[pl.BlockSpec((TM_ROUTE, LANES), lambda j: (j, 0)),
                  pl.BlockSpec((8, LANES), lambda j: (0, 0))],
        out_specs=[pl.BlockSpec((TM_ROUTE, LANES), lambda j: (j, 0)),
                   pl.BlockSpec((nb_pad, LANES), lambda j: (0, 0)),
                   pl.BlockSpec((8, LANES), lambda j: (0, 0))],
        out_shape=[jax.ShapeDtypeStruct((n_tokens, LANES), jnp.int32),
                   jax.ShapeDtypeStruct((nb_pad, LANES), jnp.int32),
                   jax.ShapeDtypeStruct((8, LANES), jnp.int32)],
        scratch_shapes=[pltpu.VMEM((8, LANES), jnp.float32),
                        pltpu.VMEM((TM_ROUTE, TM_ROUTE), jnp.bfloat16)],
        compiler_params=pltpu.CompilerParams(dimension_semantics=("arbitrary",),
                                             vmem_limit_bytes=VMEM_LIMIT),
    )(ti, cnt)
    pos_kmajor = pos[:, :TOP_K].T
    return pos_kmajor, n_blocks * TM_MOE, blk[:n_blocks, 0], nu[0, :1]


def kernel(x, c, w_mod, b_mod, g_pre_mix, w_in, w_gate_lr, b_gate, g_gla, w_out, g_post_mix,
           g_pre_ffn, router_w, router_b, w_gate_up, b_gate_up, w_down, b_down, g_post_ffn):
    B, S, D = x.shape
    T = B * S
    bf = jnp.bfloat16
    cos_t, sin_t = _rope_tables(S)
    for l in range(w_mod.shape[0]):
        wg, wl = _split_gate_up(w_gate_up[l])
        wd = w_down[l]
        mod3 = _mod(c, w_mod[l], b_mod[l]).reshape(B, N_MOD, D)
        wlr = jnp.pad(w_gate_lr[l], ((0, LANES - GLA_GATE_RANK), (0, 0))).astype(bf)
        proj = _in_proj(x, mod3, g_pre_mix[l][None], _pack_w_in(w_in[l]), cos_t, sin_t, wlr,
                        b_gate[l][None])
        n_pat = len(DILATIONS)
        qa, ka, va = proj[:n_pat], proj[n_pat:2 * n_pat], proj[2 * n_pat:3 * n_pat]
        qg, kg, vg, sr, la = proj[3 * n_pat:]
        pats = [_attn_pattern(qa[j], ka[j], va[j], d) for j, d in enumerate(DILATIONS)]
        og = _gla(qg, kg, vg, la, sr, g_gla[l][None])
        rw = jnp.pad(router_w[l], ((0, 0), (0, LANES - N_EXPERTS)))
        rb = jnp.pad(router_b[l], (0, LANES - N_EXPERTS), constant_values=NEG_BIG)[None]
        x1, h2, ti, tg, cnt = _out_proj([p[0] for p in pats], [p[1] for p in pats], og, x, mod3,
                                        w_out[l].astype(bf), g_post_mix[l][None],
                                        g_pre_ffn[l][None], rw, rb)
        pos_kmajor, n_slots, blk_e, n_used = _route(ti.reshape(T, LANES), cnt, T)
        xs = _scatter_rows(h2.reshape(T, D // 2), pos_kmajor, n_slots)
        out_buf = _moe(xs, blk_e, n_used, wg, wl,
                       b_gate_up[l][:, None, 0::2], b_gate_up[l][:, None, 1::2],
                       wd, b_down[l][:, None, :])
        out = None
        tp = T // FINAL_PARTS
        for part in range(FINAL_PARTS):
            yk = _gather_rows(out_buf, pos_kmajor[:, part * tp:(part + 1) * tp].reshape(-1))
            out = _final(x1.reshape(T, D), yk, tg.reshape(T, LANES), mod3, g_post_ffn[l][None],
                         S, part, out)
        x = out.reshape(B, S, D)
    return x
```

```python
import functools

import numpy as np
import jax
import jax.numpy as jnp
from jax import lax
from jax.experimental import pallas as pl
from jax.experimental.pallas import tpu as pltpu
from jax.experimental.pallas import tpu_sc as plsc

D_MODEL = 1024
ATT_HEADS = 8
ATT_HEAD_DIM = 64
ATT_WIDTH = ATT_HEADS * ATT_HEAD_DIM
DILATIONS = (1, 4, 16)
ATT_SPAN = 128
ROPE_THETA = 10000.0
GLA_HEADS = 4
GLA_KEY_DIM = 64
GLA_VAL_DIM = 128
GLA_KEY_WIDTH = GLA_HEADS * GLA_KEY_DIM
GLA_VAL_WIDTH = GLA_HEADS * GLA_VAL_DIM
GLA_GATE_RANK = 16
GLA_GATE_NORMALIZER = 16.0
GLA_CHUNK = 64
N_EXPERTS = 32
TOP_K = 4
D_FF = D_MODEL
SWIGLU_LIMIT = 7.0
SWIGLU_ALPHA = 1.702
NORM_EPS = 1e-6
N_MOD = 6

LANES = 128
NEG_BIG = -1e30

TM_PROJ = 512
TQ_ATT = 1024
TG_GLA = 512
GLA_UNROLL = 8
TM_OUT = 512
OUT_SUBTILES = 1
TM_MOE = 512
TM_FIN = 512
TN_MOD = 512
TK_SPLIT = 512
TM_ROUTE = 1024
FINAL_PARTS = 4
SC_INDEX_WINDOW = 128
SC_GATHER_WINDOW = 64
VMEM_LIMIT = 48 * 1024 * 1024

_C_QA, _C_KA, _C_VA = 0, 512, 1024
_C_QG, _C_KG, _C_VG, _C_RG, _C_LR = 1536, 1792, 2048, 2560, 3072
D_IN_PACKED = 3200


def _rms(x, g):
    return x * lax.rsqrt(jnp.mean(x * x, axis=-1, keepdims=True) + NORM_EPS) * g


def _pack_bf16_pairs(x):
    n = x.shape[1] // 2
    u = lax.bitcast_convert_type(x.astype(jnp.bfloat16).astype(jnp.float32), jnp.uint32)
    return (u[:, :n] >> 16) | (u[:, n:] & jnp.uint32(0xFFFF0000))


def _unpack_bf16_pairs(w):
    lo = lax.bitcast_convert_type(w << 16, jnp.float32)
    hi = lax.bitcast_convert_type(w & jnp.uint32(0xFFFF0000), jnp.float32)
    return lo, hi


def _gather_rows(data, idx):
    n_rows = idx.shape[0]
    width = data.shape[1]
    mesh = plsc.VectorSubcoreMesh(core_axis_name="core", subcore_axis_name="subcore")
    n_workers = mesh.num_cores * mesh.num_subcores
    per_worker = n_rows // n_workers
    assert per_worker * n_workers == n_rows and per_worker % SC_INDEX_WINDOW == 0
    halves = SC_INDEX_WINDOW // SC_GATHER_WINDOW

    @pl.kernel(out_type=jax.ShapeDtypeStruct((n_rows, width), data.dtype), mesh=mesh,
               name="gather_rows",
               scratch_types=[pltpu.VMEM((1, SC_INDEX_WINDOW), jnp.int32),
                              pltpu.VMEM((SC_GATHER_WINDOW, width), data.dtype)])
    def gather(x_hbm, i_hbm, o_hbm, idx_vmem, rows_vmem):
        worker = lax.axis_index("core") * mesh.num_subcores + lax.axis_index("subcore")
        base = worker * per_worker

        @pl.loop(0, per_worker // SC_INDEX_WINDOW)
        def _(j):
            off = base + j * SC_INDEX_WINDOW
            pltpu.sync_copy(i_hbm.at[:, pl.ds(off, SC_INDEX_WINDOW)], idx_vmem)
            for h in range(halves):
                part = idx_vmem.at[0, pl.ds(h * SC_GATHER_WINDOW, SC_GATHER_WINDOW)]
                pltpu.sync_copy(x_hbm.at[part], rows_vmem)
                pltpu.sync_copy(rows_vmem,
                                o_hbm.at[pl.ds(off + h * SC_GATHER_WINDOW, SC_GATHER_WINDOW)])

    return gather(data, idx.reshape(1, n_rows))


def _scatter_rows(data, idx, n_out):
    n_copies, n_rows = idx.shape
    width = data.shape[1]
    mesh = plsc.VectorSubcoreMesh(core_axis_name="core", subcore_axis_name="subcore")
    n_workers = mesh.num_cores * mesh.num_subcores
    per_worker = n_rows // n_workers
    assert per_worker * n_workers == n_rows and per_worker % SC_INDEX_WINDOW == 0
    halves = SC_INDEX_WINDOW // SC_GATHER_WINDOW

    @pl.kernel(out_type=jax.ShapeDtypeStruct((n_out, width), data.dtype), mesh=mesh,
               name="scatter_rows",
               scratch_types=[pltpu.VMEM((n_copies, SC_INDEX_WINDOW), jnp.int32),
                              pltpu.VMEM((SC_GATHER_WINDOW, width), data.dtype)])
    def scatter(x_hbm, i_hbm, o_hbm, idx_vmem, rows_vmem):
        worker = lax.axis_index("core") * mesh.num_subcores + lax.axis_index("subcore")
        base = worker * per_worker

        @pl.loop(0, per_worker // SC_INDEX_WINDOW)
        def _(j):
            off = base + j * SC_INDEX_WINDOW
            pltpu.sync_copy(i_hbm.at[:, pl.ds(off, SC_INDEX_WINDOW)], idx_vmem)
            for h in range(halves):
                pltpu.sync_copy(x_hbm.at[pl.ds(off + h * SC_GATHER_WINDOW, SC_GATHER_WINDOW)],
                                rows_vmem)
                for k in range(n_copies):
                    part = idx_vmem.at[k, pl.ds(h * SC_GATHER_WINDOW, SC_GATHER_WINDOW)]
                    pltpu.sync_copy(rows_vmem, o_hbm.at[part])

    return scatter(data, idx)


def _mod_kernel(ct_ref, w_ref, b_ref, o_ref):
    ct = ct_ref[...]
    s = ct * (1.0 / (1.0 + jnp.exp(-ct)))
    w = w_ref[...]
    rows = [jnp.sum(s[:, b:b + 1] * w, axis=0, keepdims=True) for b in range(ct.shape[1])]
    o_ref[...] = jnp.concatenate(rows, axis=0) + b_ref[...]


def _mod(c, w_mod, b_mod):
    B = c.shape[0]
    n = w_mod.shape[1]
    return pl.pallas_call(
        _mod_kernel,
        name="mod",
        grid=(n // TN_MOD,),
        in_specs=[pl.BlockSpec((D_MODEL, B), lambda j: (0, 0)),
                  pl.BlockSpec((D_MODEL, TN_MOD), lambda j: (0, j)),
                  pl.BlockSpec((1, TN_MOD), lambda j: (0, j))],
        out_specs=pl.BlockSpec((B, TN_MOD), lambda j: (0, j)),
        out_shape=jax.ShapeDtypeStruct((B, n), jnp.float32),
        compiler_params=pltpu.CompilerParams(dimension_semantics=("arbitrary",),
                                             vmem_limit_bytes=VMEM_LIMIT),
    )(c.T, w_mod, b_mod.reshape(1, n))


def _store_residue_views(scr_ref, scr4_ref, out_refs):
    ns, tm, _ = scr_ref.shape
    w = ns * LANES
    o1_ref, o4_ref, o16_ref = out_refs
    assert DILATIONS == (1, 4, 16)
    q4 = tm // 4
    for s in range(ns):
        o1_ref[0, :, s * LANES:(s + 1) * LANES] = scr_ref[s].astype(o1_ref.dtype)
    for r in range(4):
        for s in range(ns):
            vals = scr_ref[s, pl.ds(r, q4, stride=4), :]
            scr4_ref[s, r * q4:(r + 1) * q4, :] = vals
            c0 = r * w + s * LANES
            o4_ref[0, :, c0:c0 + LANES] = vals.astype(o4_ref.dtype)
    for r in range(16):
        lo, hi = r % 4, r // 4
        for s in range(ns):
            vals = scr4_ref[s, pl.ds(lo * q4 + hi, tm // 16, stride=4), :]
            c0 = r * w + s * LANES
            o16_ref[0, :, c0:c0 + LANES] = vals.astype(o16_ref.dtype)


def _in_proj_kernel(x_ref, mod_ref, g_ref, w_ref, cos_ref, sin_ref, wlr_ref, bg_ref,
                    qa1_ref, qa4_ref, qa16_ref, ka1_ref, ka4_ref, ka16_ref,
                    va1_ref, va4_ref, va16_ref, qg_ref, kg_ref, vg_ref, sr_ref, la_ref,
                    qs_ref, ks_ref, vs_ref, q4_ref, k4_ref, v4_ref):
    x = x_ref[0]
    shift = mod_ref[0, 0:1, :]
    scale = mod_ref[0, 1:2, :]
    h = (_rms(x, g_ref[...]) * (1.0 + scale) + shift).astype(jnp.bfloat16)

    def proj(c0, width):
        return jnp.dot(h, w_ref[:, c0:c0 + width], preferred_element_type=jnp.float32)

    cos = cos_ref[...]
    sin = sin_ref[...]

    def rope(c0, scr_ref, mult):
        p = proj(c0, ATT_WIDTH)
        for s in range(ATT_WIDTH // LANES):
            t = p[:, s * LANES:(s + 1) * LANES]
            r = t * cos + pltpu.roll(t, LANES // 2, axis=1) * sin
            if mult != 1.0:
                r = r * mult
            scr_ref[s] = r

    rope(_C_QA, qs_ref, ATT_HEAD_DIM ** -0.5)
    _store_residue_views(qs_ref, q4_ref, (qa1_ref, qa4_ref, qa16_ref))
    rope(_C_KA, ks_ref, 1.0)
    _store_residue_views(ks_ref, k4_ref, (ka1_ref, ka4_ref, ka16_ref))
    pv = proj(_C_VA, ATT_WIDTH)
    for s in range(ATT_WIDTH // LANES):
        vs_ref[s] = pv[:, s * LANES:(s + 1) * LANES]
    _store_residue_views(vs_ref, v4_ref, (va1_ref, va4_ref, va16_ref))
    qg_ref[0] = proj(_C_QG, GLA_KEY_WIDTH).astype(qg_ref.dtype)
    kg_ref[0] = proj(_C_KG, GLA_KEY_WIDTH).astype(kg_ref.dtype)
    vg_ref[0] = proj(_C_VG, GLA_VAL_WIDTH).astype(vg_ref.dtype)
    r = proj(_C_RG, GLA_VAL_WIDTH)
    sr_ref[0] = (r * (1.0 / (1.0 + jnp.exp(-r)))).astype(sr_ref.dtype)
    lr = proj(_C_LR, LANES).astype(jnp.bfloat16)
    z = jnp.dot(lr, wlr_ref[...], preferred_element_type=jnp.float32) + bg_ref[...]
    log_sig = jnp.minimum(z, 0.0) - jnp.log(1.0 + jnp.exp(-jnp.abs(z)))
    la_ref[0] = log_sig * (1.0 / GLA_GATE_NORMALIZER)


def _in_proj(x, mod3, g_pre, w_packed, cos_t, sin_t, wlr, bg):
    B, S, D = x.shape
    nb = S // TM_PROJ
    tok = lambda w: pl.BlockSpec((1, TM_PROJ, w), lambda b, i: (b, i, 0))
    const = lambda shape: pl.BlockSpec(shape, lambda b, i: tuple(0 for _ in shape))
    bf = jnp.bfloat16
    att_shapes = [jax.ShapeDtypeStruct((B, S // d, d * ATT_WIDTH), bf) for d in DILATIONS] * 3
    att_specs = [pl.BlockSpec((1, TM_PROJ // d, d * ATT_WIDTH), lambda b, i: (b, i, 0))
                 for d in DILATIONS] * 3
    gla_shapes = [jax.ShapeDtypeStruct((B, S, w), dt) for w, dt in
                  [(GLA_KEY_WIDTH, bf), (GLA_KEY_WIDTH, bf),
                   (GLA_VAL_WIDTH, bf), (GLA_VAL_WIDTH, bf), (GLA_KEY_WIDTH, jnp.float32)]]
    return pl.pallas_call(
        _in_proj_kernel,
        name="in_proj",
        grid=(B, nb),
        in_specs=[tok(D),
                  pl.BlockSpec((1, N_MOD, D), lambda b, i: (b, 0, 0)),
                  const((1, D)),
                  const((D, D_IN_PACKED)),
                  pl.BlockSpec((TM_PROJ, LANES), lambda b, i: (i, 0)),
                  pl.BlockSpec((TM_PROJ, LANES), lambda b, i: (i, 0)),
                  const((LANES, GLA_KEY_WIDTH)),
                  const((1, GLA_KEY_WIDTH))],
        out_specs=att_specs + [tok(s.shape[-1]) for s in gla_shapes],
        out_shape=att_shapes + gla_shapes,
        scratch_shapes=[pltpu.VMEM((ATT_WIDTH // LANES, TM_PROJ, LANES), jnp.float32)] * 6,
        compiler_params=pltpu.CompilerParams(dimension_semantics=("arbitrary", "arbitrary"),
                                             vmem_limit_bytes=VMEM_LIMIT),
    )(x, mod3, g_pre, w_packed, cos_t, sin_t, wlr, bg)


def _attn_kernel(q_ref, kp_ref, kc_ref, vp_ref, vc_ref, o_ref, st_ref):
    i = pl.program_id(2)
    tp = kp_ref.shape[1]
    tq = tp
    n_sub = q_ref.shape[1] // tq
    row = lax.broadcasted_iota(jnp.int32, (tq, tp + tq), 0)
    col = lax.broadcasted_iota(jnp.int32, (tq, tp + tq), 1)
    band = (col >= row) & (col <= row + ATT_SPAN)
    bias_inner = jnp.where(band, 0.0, NEG_BIG)
    bias_first = jnp.where(band & ((col >= tp) | (i > 0)), 0.0, NEG_BIG)
    lane = lax.broadcasted_iota(jnp.int32, (tq, LANES), 1)
    q_head = (lane >> 5) & 1
    for sb in range(n_sub):
        rows = slice(sb * tq, (sb + 1) * tq)
        bias = bias_first if sb == 0 else bias_inner
        stats = jnp.zeros((tq, LANES), jnp.float32)
        for hp in range(ATT_HEADS // 2):
            sl = slice(hp * LANES, (hp + 1) * LANES)
            q2 = q_ref[0, rows, sl]
            if sb == 0:
                kcat = jnp.concatenate([kp_ref[0, :, sl], kc_ref[0, rows, sl]], axis=0)
                vcat = jnp.concatenate([vp_ref[0, :, sl], vc_ref[0, rows, sl]], axis=0)
            else:
                kcat = kc_ref[0, (sb - 1) * tq:(sb + 1) * tq, sl]
                vcat = vc_ref[0, (sb - 1) * tq:(sb + 1) * tq, sl]
            outs = []
            for hh in range(2):
                qm = jnp.where(q_head == hh, q2, jnp.zeros_like(q2))
                s = lax.dot_general(qm, kcat, (((1,), (1,)), ((), ())),
                                    preferred_element_type=jnp.float32) + bias
                m = jnp.max(s, axis=1, keepdims=True)
                p = jnp.exp(s - m)
                l = jnp.sum(p, axis=1, keepdims=True)
                o = jnp.dot(p.astype(vcat.dtype), vcat, preferred_element_type=jnp.float32)
                outs.append(o / l)
                stats = jnp.where(lane == 2 * hp + hh, m + jnp.log(l), stats)
            o_ref[0, rows, sl] = jnp.where(lane < ATT_HEAD_DIM, outs[0], outs[1]).astype(o_ref.dtype)
        st_ref[0, rows, :] = stats


def _attn_pattern(qv, kv, vv, dil):
    B, L, _ = qv.shape
    W = ATT_WIDTH
    tq = min(TQ_ATT, L)
    nq = L // tq
    cur = pl.BlockSpec((1, tq, W), lambda b, r, i: (b, i, r))
    back = tq // ATT_SPAN
    prev = pl.BlockSpec((1, ATT_SPAN, W), lambda b, r, i: (b, jnp.maximum(i * back - 1, 0), r))
    o, st = pl.pallas_call(
        _attn_kernel,
        name=f"attn_d{dil}",
        grid=(B, dil, nq),
        in_specs=[cur, prev, cur, prev, cur],
        out_specs=[cur, pl.BlockSpec((1, tq, LANES), lambda b, r, i: (b, i, r))],
        out_shape=[jax.ShapeDtypeStruct((B, L, dil * W), jnp.bfloat16),
                   jax.ShapeDtypeStruct((B, L, dil * LANES), jnp.float32)],
        compiler_params=pltpu.CompilerParams(
            dimension_semantics=("arbitrary", "arbitrary", "arbitrary"),
            vmem_limit_bytes=VMEM_LIMIT),
    )(qv, kv, kv, vv, vv)
    return o, st


def _cumsum_rows(x):
    n = x.shape[0]
    row = lax.broadcasted_iota(jnp.int32, x.shape, 0)
    s = 1
    while s < n:
        x = x + jnp.where(row >= s, pltpu.roll(x, s, axis=0), 0.0)
        s *= 2
    return x


def _gla_kernel(q_ref, k_ref, v_ref, la_ref, sr_ref, g_ref, o_ref, st_ref):
    @pl.when(pl.program_id(0) == 0)
    def _():
        st_ref[...] = jnp.zeros_like(st_ref)

    C = GLA_CHUNK
    n_batch = q_ref.shape[0]
    n_chunks = q_ref.shape[1] // C
    lane = lax.broadcasted_iota(jnp.int32, (C, LANES), 1)
    lane_sq = lax.broadcasted_iota(jnp.int32, (LANES, LANES), 1)
    tril = (lax.broadcasted_iota(jnp.int32, (C, C), 0)
            >= lax.broadcasted_iota(jnp.int32, (C, C), 1))
    g = g_ref[...]
    bf = jnp.bfloat16

    def chunk(c, carry):
        r0 = pl.multiple_of(c * C, C)
        rows = pl.ds(r0, C)
        for n in range(n_batch):
            for p in range(GLA_HEADS // 2):
                ksl = slice(p * LANES, (p + 1) * LANES)
                b = _cumsum_rows(la_ref[n, rows, ksl])
                b_last = b[C - 1:C, :]
                q = q_ref[n, rows, ksl].astype(jnp.float32) * (GLA_KEY_DIM ** -0.5)
                k = k_ref[n, rows, ksl].astype(jnp.float32)
                q_dec = q * jnp.exp(b)
                k_inv = (k * jnp.exp(-b)).astype(bf)
                k_dec = (k * jnp.exp(b_last - b)).astype(bf)
                st = st_ref[n, p]
                st_b = st.astype(bf)
                ut = []
                for hh in range(2):
                    h = 2 * p + hh
                    vsl = slice(h * GLA_VAL_DIM, (h + 1) * GLA_VAL_DIM)
                    v = v_ref[n, rows, vsl]
                    own = (lane >= GLA_KEY_DIM) if hh else (lane < GLA_KEY_DIM)
                    qm = jnp.where(own, q_dec, 0.0).astype(bf)
                    att = lax.dot_general(qm, k_inv, (((1,), (1,)), ((), ())),
                                          preferred_element_type=jnp.float32)
                    att = jnp.where(tril, att, 0.0).astype(bf)
                    o = jnp.dot(att, v, preferred_element_type=jnp.float32)
                    o = o + lax.dot_general(qm, st_b, (((1,), (1,)), ((), ())),
                                            preferred_element_type=jnp.float32)
                    o = _rms(o, g) * sr_ref[n, rows, vsl].astype(jnp.float32)
                    o_ref[n, rows, vsl] = o.astype(o_ref.dtype)
                    ut.append(lax.dot_general(v, k_dec, (((0,), (0,)), ((), ())),
                                              preferred_element_type=jnp.float32))
                st_ref[n, p] = (st * jnp.exp(b_last)
                                + jnp.where(lane_sq < GLA_KEY_DIM, ut[0], ut[1]))
        return carry

    lax.fori_loop(0, n_chunks, chunk, 0, unroll=GLA_UNROLL)


def _gla(qg, kg, vg, la, sr, g_gla):
    B, S, _ = qg.shape
    tok = lambda w: pl.BlockSpec((B, TG_GLA, w), lambda i: (0, i, 0))
    return pl.pallas_call(
        _gla_kernel,
        name="gla",
        grid=(S // TG_GLA,),
        in_specs=[tok(GLA_KEY_WIDTH), tok(GLA_KEY_WIDTH), tok(GLA_VAL_WIDTH),
                  tok(GLA_KEY_WIDTH), tok(GLA_VAL_WIDTH),
                  pl.BlockSpec((1, GLA_VAL_DIM), lambda i: (0, 0))],
        out_specs=tok(GLA_VAL_WIDTH),
        out_shape=jax.ShapeDtypeStruct((B, S, GLA_VAL_WIDTH), jnp.bfloat16),
        scratch_shapes=[pltpu.VMEM((B, GLA_HEADS // 2, GLA_VAL_DIM, LANES), jnp.float32)],
        compiler_params=pltpu.CompilerParams(dimension_semantics=("arbitrary",),
                                             vmem_limit_bytes=VMEM_LIMIT),
    )(qg, kg, vg, la, sr, g_gla)


def _out_proj_kernel(o1_ref, o2_ref, o3_ref, s1_ref, s2_ref, s3_ref, og_ref, x_ref, mod_ref,
                     wout_ref, gpost_ref, gpre_ref, rw_ref, rb_ref,
                     x1_ref, h2_ref, ti_ref, tg_ref, cnt_ref, oscr_ref, sscr_ref, otmp_ref,
                     stmp_ref):
    tm = x_ref.shape[1]

    @pl.when((pl.program_id(0) == 0) & (pl.program_id(1) == 0))
    def _():
        cnt_ref[...] = jnp.zeros_like(cnt_ref)

    ns = ATT_WIDTH // LANES
    q4 = tm // 4
    for r in range(4):
        rows = pl.ds(r, q4, stride=4)
        for s in range(ns):
            c0 = r * ATT_WIDTH + s * LANES
            oscr_ref[0, s, rows, :] = o2_ref[0, :, c0:c0 + LANES].astype(jnp.float32)
        sscr_ref[0, rows, :] = s2_ref[0, :, r * LANES:(r + 1) * LANES]
    for r in range(16):
        lo, hi = r % 4, r // 4
        rows = pl.ds(lo * q4 + hi, tm // 16, stride=4)
        for s in range(ns):
            c0 = r * ATT_WIDTH + s * LANES
            otmp_ref[s, rows, :] = o3_ref[0, :, c0:c0 + LANES].astype(jnp.float32)
        stmp_ref[rows, :] = s3_ref[0, :, r * LANES:(r + 1) * LANES]
    for r in range(4):
        rows = pl.ds(r, q4, stride=4)
        for s in range(ns):
            oscr_ref[1, s, rows, :] = otmp_ref[s, r * q4:(r + 1) * q4, :]
        sscr_ref[1, rows, :] = stmp_ref[r * q4:(r + 1) * q4, :]
    gate1 = mod_ref[0, 2:3, :]
    shift2 = mod_ref[0, 3:4, :]
    scale2 = mod_ref[0, 4:5, :]
    bf = jnp.bfloat16
    rw = rw_ref[...]
    w_hi = rw.astype(bf)
    w_lo = (rw - w_hi.astype(jnp.float32)).astype(bf)
    ts = tm // OUT_SUBTILES
    lane = lax.broadcasted_iota(jnp.int32, (ts, LANES), 1)
    lane_f = lane.astype(jnp.float32)
    for sub in range(OUT_SUBTILES):
        rs = slice(sub * ts, (sub + 1) * ts)
        lses = [s1_ref[0, rs, :], sscr_ref[0, rs, :], sscr_ref[1, rs, :]]
        m = jnp.maximum(jnp.maximum(lses[0], lses[1]), lses[2])
        es = [jnp.exp(t - m) for t in lses]
        inv = 1.0 / (es[0] + es[1] + es[2])
        ws = [e * inv for e in es]
        pairs = []
        for hp in range(ATT_HEADS // 2):
            sl = slice(hp * LANES, (hp + 1) * LANES)
            o_pats = [o1_ref[0, rs, sl].astype(jnp.float32), oscr_ref[0, hp, rs, :],
                      oscr_ref[1, hp, rs, :]]
            acc = jnp.zeros((ts, LANES), jnp.float32)
            head_of_lane = 2 * hp + (lane >> 6)
            for w, o in zip(ws, o_pats):
                acc = acc + jnp.take_along_axis(w, head_of_lane, axis=1) * o
            pairs.append(acc.astype(bf))
        mixed = jnp.concatenate(pairs + [og_ref[0, rs, :]], axis=1)
        y = jnp.dot(mixed, wout_ref[...], preferred_element_type=jnp.float32)
        x1 = x_ref[0, rs, :] + gate1 * _rms(y, gpost_ref[...])
        x1_ref[0, rs, :] = x1
        h2 = _rms(x1, gpre_ref[...]) * (1.0 + scale2) + shift2
        h2_ref[0, rs, :] = _pack_bf16_pairs(h2)
        h_hi = h2.astype(bf)
        h_lo = (h2 - h_hi.astype(jnp.float32)).astype(bf)
        logits = (jnp.dot(h_hi, w_hi, preferred_element_type=jnp.float32)
                  + jnp.dot(h_lo, w_hi, preferred_element_type=jnp.float32)
                  + jnp.dot(h_hi, w_lo, preferred_element_type=jnp.float32)) + rb_ref[...]
        vals, idxs = [], []
        for _ in range(TOP_K):
            mk = jnp.max(logits, axis=1, keepdims=True)
            ik = jnp.min(jnp.where(logits == mk, lane_f, float(LANES)), axis=1, keepdims=True)
            logits = jnp.where(lane_f == ik, -jnp.inf, logits)
            vals.append(mk)
            idxs.append(ik)
        ex = [jnp.exp(v - vals[0]) for v in vals]
        den = ex[0] + ex[1] + ex[2] + ex[3]
        ti = jnp.zeros((ts, LANES), jnp.float32)
        tg = jnp.zeros((ts, LANES), jnp.float32)
        for kk in range(TOP_K):
            ti = jnp.where(lane == kk, idxs[kk], ti)
            tg = jnp.where(lane == kk, ex[kk] / den, tg)
        ti_ref[0, rs, :] = ti.astype(jnp.int32)
        tg_ref[0, rs, :] = tg
        chosen = jnp.zeros((ts, LANES), jnp.float32)
        for ik in idxs:
            chosen = chosen + jnp.where(lane_f == ik, 1.0, 0.0)
        cnt_ref[...] += jnp.sum(chosen, axis=0, keepdims=True)


def _out_proj(o_pats, st_pats, og, x, mod3, w_out, g_post, g_pre, rw, rb):
    B, S, D = x.shape
    tok = lambda w: pl.BlockSpec((1, TM_OUT, w), lambda b, i: (b, i, 0))
    const = lambda shape: pl.BlockSpec(shape, lambda b, i: tuple(0 for _ in shape))
    out_shapes = [jax.ShapeDtypeStruct((B, S, D), jnp.float32),
                  jax.ShapeDtypeStruct((B, S, D // 2), jnp.uint32),
                  jax.ShapeDtypeStruct((B, S, LANES), jnp.int32),
                  jax.ShapeDtypeStruct((B, S, LANES), jnp.float32),
                  jax.ShapeDtypeStruct((8, LANES), jnp.float32)]
    return pl.pallas_call(
        _out_proj_kernel,
        name="out_proj",
        grid=(B, S // TM_OUT),
        in_specs=[pl.BlockSpec((1, TM_OUT // d, d * ATT_WIDTH), lambda b, i: (b, i, 0))
                  for d in DILATIONS]
                 + [pl.BlockSpec((1, TM_OUT // d, d * LANES), lambda b, i: (b, i, 0))
                    for d in DILATIONS]
                 + [tok(GLA_VAL_WIDTH), tok(D),
                  pl.BlockSpec((1, N_MOD, D), lambda b, i: (b, 0, 0)),
                  const((D, D)), const((1, D)), const((1, D)),
                  const((D, LANES)), const((1, LANES))],
        out_specs=[tok(D), tok(D // 2), tok(LANES), tok(LANES), const((8, LANES))],
        out_shape=out_shapes,
        scratch_shapes=[pltpu.VMEM((2, ATT_WIDTH // LANES, TM_OUT, LANES), jnp.float32),
                        pltpu.VMEM((2, TM_OUT, LANES), jnp.float32),
                        pltpu.VMEM((ATT_WIDTH // LANES, TM_OUT, LANES), jnp.float32),
                        pltpu.VMEM((TM_OUT, LANES), jnp.float32)],
        compiler_params=pltpu.CompilerParams(dimension_semantics=("arbitrary", "arbitrary"),
                                             vmem_limit_bytes=VMEM_LIMIT),
    )(*o_pats, *st_pats, og, x, mod3, w_out, g_post, g_pre, rw, rb)


def _split_gate_up_kernel(w_ref, wg_ref, wl_ref):
    group = 2 * LANES
    src = lax.broadcasted_iota(jnp.int32, (group, group), 0)
    dst = lax.broadcasted_iota(jnp.int32, (group, group), 1)
    want = jnp.where(dst < LANES, 2 * dst, 2 * (dst - LANES) + 1)
    perm = jnp.where(src == want, 1.0, 0.0).astype(jnp.bfloat16)
    for j in range(w_ref.shape[1] // group):
        t = jnp.dot(w_ref[:, j * group:(j + 1) * group].astype(jnp.bfloat16), perm,
                    preferred_element_type=jnp.float32)
        wg_ref[:, j * LANES:(j + 1) * LANES] = t[:, :LANES].astype(wg_ref.dtype)
        wl_ref[:, j * LANES:(j + 1) * LANES] = t[:, LANES:].astype(wl_ref.dtype)


def _split_gate_up(w_gate_up):
    E, K, N2 = w_gate_up.shape
    out = jax.ShapeDtypeStruct((E, K, N2 // 2), jnp.bfloat16)
    ospec = pl.BlockSpec((None, TK_SPLIT, N2 // 2), lambda e, i: (e, i, 0))
    return pl.pallas_call(
        _split_gate_up_kernel,
        name="split_gate_up",
        grid=(E, K // TK_SPLIT),
        in_specs=[pl.BlockSpec((None, TK_SPLIT, N2), lambda e, i: (e, i, 0))],
        out_specs=[ospec, ospec],
        out_shape=[out, out],
        compiler_params=pltpu.CompilerParams(dimension_semantics=("arbitrary", "arbitrary"),
                                             vmem_limit_bytes=VMEM_LIMIT),
    )(w_gate_up)


def _moe_kernel(be_ref, nu_ref, x_ref, wg_ref, wl_ref, bg_ref, bl_ref, wd_ref, bd_ref, o_ref,
                wd_bf_ref):
    i = pl.program_id(0)
    new_expert = (i == 0) | (be_ref[i] != be_ref[jnp.maximum(i - 1, 0)])

    @pl.when(new_expert & (i < nu_ref[0]))
    def _():
        wd_bf_ref[...] = wd_ref[...].astype(wd_bf_ref.dtype)

    @pl.when(i < nu_ref[0])
    def _():
        lo, hi = _unpack_bf16_pairs(x_ref[...])
        x = jnp.concatenate([lo, hi], axis=1).astype(jnp.bfloat16)
        xg = jnp.dot(x, wg_ref[...], preferred_element_type=jnp.float32) + bg_ref[...]
        xl = jnp.dot(x, wl_ref[...], preferred_element_type=jnp.float32) + bl_ref[...]
        xg = jnp.minimum(xg, SWIGLU_LIMIT)
        xl = jnp.clip(xl, -SWIGLU_LIMIT, SWIGLU_LIMIT)
        act = xg * (1.0 / (1.0 + jnp.exp(-SWIGLU_ALPHA * xg))) * (xl + 1.0)
        out = jnp.dot(act.astype(jnp.bfloat16), wd_bf_ref[...],
                      preferred_element_type=jnp.float32) + bd_ref[...]
        o_ref[...] = _pack_bf16_pairs(out)

    @pl.when(i >= nu_ref[0])
    def _():
        o_ref[...] = jnp.zeros_like(o_ref)


def _moe(xs, blk_e, n_used, wg, wl, bg, bl, wd, bd):
    P, half = xs.shape
    D = 2 * half
    n_blocks = P // TM_MOE
    wspec = lambda k, n: pl.BlockSpec((None, k, n), lambda i, be, nu: (be[i], 0, 0))
    grid_spec = pltpu.PrefetchScalarGridSpec(
        num_scalar_prefetch=2,
        grid=(n_blocks,),
        in_specs=[pl.BlockSpec((TM_MOE, half), lambda i, be, nu: (i, 0)),
                  wspec(D, D_FF), wspec(D, D_FF), wspec(1, D_FF), wspec(1, D_FF),
                  wspec(D_FF, D), wspec(1, D)],
        out_specs=pl.BlockSpec((TM_MOE, half), lambda i, be, nu: (i, 0)),
        scratch_shapes=[pltpu.VMEM((D_FF, D), jnp.bfloat16)],
    )
    return pl.pallas_call(
        _moe_kernel,
        name="moe",
        grid_spec=grid_spec,
        out_shape=jax.ShapeDtypeStruct((P, half), jnp.uint32),
        compiler_params=pltpu.CompilerParams(dimension_semantics=("arbitrary",),
                                             vmem_limit_bytes=VMEM_LIMIT),
    )(blk_e, n_used, xs, wg, wl, bg, bl, wd, bd)


def _final_kernel(x1_ref, y0_ref, y1_ref, y2_ref, y3_ref, tg_ref, mod_ref, g_ref, o_ref):
    gates = tg_ref[...]
    half = y0_ref.shape[1]
    y_lo = jnp.zeros((x1_ref.shape[0], half), jnp.float32)
    y_hi = jnp.zeros((x1_ref.shape[0], half), jnp.float32)
    for kk, yk_ref in enumerate((y0_ref, y1_ref, y2_ref, y3_ref)):
        lo, hi = _unpack_bf16_pairs(yk_ref[...])
        y_lo = y_lo + gates[:, kk:kk + 1] * lo
        y_hi = y_hi + gates[:, kk:kk + 1] * hi
    y = jnp.concatenate([y_lo, y_hi], axis=1)
    gate2 = mod_ref[0, 5:6, :]
    o_ref[...] = x1_ref[...] + gate2 * _rms(y, g_ref[...])


def _final_part_kernel(x1_ref, y0_ref, y1_ref, y2_ref, y3_ref, tg_ref, mod_ref, g_ref, prev_ref,
                       o_ref):
    del prev_ref
    _final_kernel(x1_ref, y0_ref, y1_ref, y2_ref, y3_ref, tg_ref, mod_ref, g_ref, o_ref)


def _final(x1, yk, tg, mod3, g_post, seq_len, part, prev):
    T, D = x1.shape
    tp = yk.shape[0] // TOP_K
    nt = tp // TM_FIN
    t0 = part * nt
    batch = (part * tp) // seq_len
    yspec = lambda kk: pl.BlockSpec((TM_FIN, D // 2), lambda i: (kk * nt + i, 0))
    in_specs = ([pl.BlockSpec((TM_FIN, D), lambda i: (t0 + i, 0))]
                + [yspec(kk) for kk in range(TOP_K)]
                + [pl.BlockSpec((TM_FIN, LANES), lambda i: (t0 + i, 0)),
                   pl.BlockSpec((1, N_MOD, D), lambda i: (batch, 0, 0)),
                   pl.BlockSpec((1, D), lambda i: (0, 0))])
    args = [x1, yk, yk, yk, yk, tg, mod3, g_post]
    body, aliases = _final_kernel, {}
    if prev is not None:
        in_specs.append(pl.BlockSpec(memory_space=pl.ANY))
        args.append(prev)
        body, aliases = _final_part_kernel, {len(args) - 1: 0}
    return pl.pallas_call(
        body,
        name="final",
        grid=(nt,),
        in_specs=in_specs,
        out_specs=pl.BlockSpec((TM_FIN, D), lambda i: (t0 + i, 0)),
        out_shape=jax.ShapeDtypeStruct((T, D), jnp.float32),
        input_output_aliases=aliases,
        compiler_params=pltpu.CompilerParams(dimension_semantics=("arbitrary",),
                                             vmem_limit_bytes=VMEM_LIMIT),
    )(*args)


def _pack_w_in(w_in):
    half = ATT_HEAD_DIM // 2

    def pair_rotary_layout(w):
        k = w.shape[0]
        w = w.reshape(k, ATT_HEADS // 2, 2, 2, half)
        return w.transpose(0, 1, 3, 2, 4).reshape(k, ATT_WIDTH)

    w_in = w_in.astype(jnp.bfloat16)
    lr = jnp.pad(w_in[:, 3072:3072 + GLA_GATE_RANK], ((0, 0), (0, LANES - GLA_GATE_RANK)))
    return jnp.concatenate([pair_rotary_layout(w_in[:, :ATT_WIDTH]),
                            pair_rotary_layout(w_in[:, ATT_WIDTH:2 * ATT_WIDTH]),
                            w_in[:, 1024:3072], lr], axis=1)


def _rope_tables(seq_len):
    half = ATT_HEAD_DIM // 2
    inv_freq = ROPE_THETA ** (-jnp.arange(half, dtype=jnp.float32) / half)
    ang = jnp.arange(seq_len, dtype=jnp.float32)[:, None] * inv_freq[None, :]
    cos = jnp.tile(jnp.cos(ang), (1, LANES // half))
    sin = jnp.tile(jnp.sin(ang), (1, LANES // half))
    sign = jnp.where(jnp.arange(LANES) < LANES // 2, -1.0, 1.0)
    return cos, sin * sign


def _route_kernel(ti_ref, cnt_ref, pos_ref, blk_ref, nu_ref, base_ref, tri_ref):
    j = pl.program_id(0)
    tm = ti_ref.shape[0]
    lane = lax.broadcasted_iota(jnp.int32, (tm, LANES), 1)
    ti = ti_ref[...]
    hots = [lane == ti[:, k:k + 1] for k in range(TOP_K)]
    hot_all = jnp.zeros((tm, LANES), jnp.float32)
    for h in hots:
        hot_all = hot_all + jnp.where(h, 1.0, 0.0)
    tile_cnt = jnp.sum(hot_all, axis=0, keepdims=True)

    @pl.when(j == 0)
    def _():
        r = lax.broadcasted_iota(jnp.int32, (tm, tm), 0)
        c = lax.broadcasted_iota(jnp.int32, (tm, tm), 1)
        tri_ref[...] = jnp.where(c < r, 1.0, 0.0).astype(tri_ref.dtype)
        shift = TM_MOE.bit_length() - 1
        lane8 = lax.broadcasted_iota(jnp.int32, cnt_ref.shape, 1)
        cnt = cnt_ref[...].astype(jnp.int32)
        padded = ((cnt + (TM_MOE - 1)) >> shift) << shift
        pend = padded
        s = 1
        while s < N_EXPERTS:
            pend = pend + jnp.where(lane8 >= s, pltpu.roll(pend, s, axis=1), 0)
            s *= 2
        base_ref[...] = (pend - padded).astype(jnp.float32)
        nb = blk_ref.shape[0]
        blk_start = lax.broadcasted_iota(jnp.int32, (nb, LANES), 0) * TM_MOE
        lane_b = lax.broadcasted_iota(jnp.int32, (nb, LANES), 1)
        done = jnp.where((pend[0:1, :] <= blk_start) & (lane_b < N_EXPERTS), 1.0, 0.0)
        blk = jnp.minimum(jnp.sum(done, axis=1, keepdims=True), float(N_EXPERTS - 1))
        blk_ref[...] = jnp.broadcast_to(blk, blk_ref.shape).astype(jnp.int32)
        total = jnp.max(jnp.where(lane8 < N_EXPERTS, pend, 0).astype(jnp.float32),
                        axis=1, keepdims=True)
        nu_ref[...] = jnp.broadcast_to(total.astype(jnp.int32) >> shift, nu_ref.shape)

    before = jnp.dot(tri_ref[...], hot_all.astype(tri_ref.dtype),
                     preferred_element_type=jnp.float32)
    slot = before + base_ref[0:1, :]
    pos = jnp.zeros((tm, LANES), jnp.float32)
    for k, h in enumerate(hots):
        pk = jnp.sum(jnp.where(h, slot, 0.0), axis=1, keepdims=True)
        pos = jnp.where(lane == k, pk, pos)
    pos_ref[...] = pos.astype(jnp.int32)
    base_ref[...] += tile_cnt


def _route(ti, cnt, n_tokens):
    A = n_tokens * TOP_K
    n_blocks = -(-(A + N_EXPERTS * (TM_MOE - 1)) // TM_MOE)
    nb_pad = -(-n_blocks // 8) * 8
    nt = n_tokens // TM_ROUTE
    pos, blk, nu = pl.pallas_call(
        _route_kernel,
        name="route",
        grid=(nt,),
        in_specs=[pl.BlockSpec((TM_ROUTE, LANES), lambda j: (j, 0)),
                  pl.BlockSpec((8, LANES), lambda j: (0, 0))],
        out_specs=[pl.BlockSpec((TM_ROUTE, LANES), lambda j: (j, 0)),
                   pl.BlockSpec((nb_pad, LANES), lambda j: (0, 0)),
                   pl.BlockSpec((8, LANES), lambda j: (0, 0))],
        out_shape=[jax.ShapeDtypeStruct((n_tokens, LANES), jnp.int32),
                   jax.ShapeDtypeStruct((nb_pad, LANES), jnp.int32),
                   jax.ShapeDtypeStruct((8, LANES), jnp.int32)],
        scratch_shapes=[pltpu.VMEM((8, LANES), jnp.float32),
                        pltpu.VMEM((TM_ROUTE, TM_ROUTE), jnp.bfloat16)],
        compiler_params=pltpu.CompilerParams(dimension_semantics=("arbitrary",),
                                             vmem_limit_bytes=VMEM_LIMIT),
    )(ti, cnt)
    pos_kmajor = pos[:, :TOP_K].T
    return pos_kmajor, n_blocks * TM_MOE, blk[:n_blocks, 0], nu[0, :1]


def kernel(x, c, w_mod, b_mod, g_pre_mix, w_in, w_gate_lr, b_gate, g_gla, w_out, g_post_mix,
           g_pre_ffn, router_w, router_b, w_gate_up, b_gate_up, w_down, b_down, g_post_ffn):
    B, S, D = x.shape
    T = B * S
    bf = jnp.bfloat16
    cos_t, sin_t = _rope_tables(S)
    for l in range(w_mod.shape[0]):
        wg, wl = _split_gate_up(w_gate_up[l])
        wd = w_down[l]
        mod3 = _mod(c, w_mod[l], b_mod[l]).reshape(B, N_MOD, D)
        wlr = jnp.pad(w_gate_lr[l], ((0, LANES - GLA_GATE_RANK), (0, 0))).astype(bf)
        proj = _in_proj(x, mod3, g_pre_mix[l][None], _pack_w_in(w_in[l]), cos_t, sin_t, wlr,
                        b_gate[l][None])
        n_pat = len(DILATIONS)
        qa, ka, va = proj[:n_pat], proj[n_pat:2 * n_pat], proj[2 * n_pat:3 * n_pat]
        qg, kg, vg, sr, la = proj[3 * n_pat:]
        pats = [_attn_pattern(qa[j], ka[j], va[j], d) for j, d in enumerate(DILATIONS)]
        og = _gla(qg, kg, vg, la, sr, g_gla[l][None])
        rw = jnp.pad(router_w[l], ((0, 0), (0, LANES - N_EXPERTS)))
        rb = jnp.pad(router_b[l], (0, LANES - N_EXPERTS), constant_values=NEG_BIG)[None]
        x1, h2, ti, tg, cnt = _out_proj([p[0] for p in pats], [p[1] for p in pats], og, x, mod3,
                                        w_out[l].astype(bf), g_post_mix[l][None],
                                        g_pre_ffn[l][None], rw, rb)
        pos_kmajor, n_slots, blk_e, n_used = _route(ti.reshape(T, LANES), cnt, T)
        xs = _scatter_rows(h2.reshape(T, D // 2), pos_kmajor, n_slots)
        out_buf = _moe(xs, blk_e, n_used, wg, wl,
                       b_gate_up[l][:, None, 0::2], b_gate_up[l][:, None, 1::2],
                       wd, b_down[l][:, None, :])
        out = None
        tp = T // FINAL_PARTS
        for part in range(FINAL_PARTS):
            yk = _gather_rows(out_buf, pos_kmajor[:, part * tp:(part + 1) * tp].reshape(-1))
            out = _final(x1.reshape(T, D), yk, tg.reshape(T, LANES), mod3, g_post_ffn[l][None],
                         S, part, out)
        x = out.reshape(B, S, D)
    return x
```

```python
import functools

import numpy as np
import jax
import jax.numpy as jnp
from jax import lax
from jax.experimental import pallas as pl
from jax.experimental.pallas import tpu as pltpu
from jax.experimental.pallas import tpu_sc as plsc

D_MODEL = 1024
ATT_HEADS = 8
ATT_HEAD_DIM = 64
ATT_WIDTH = ATT_HEADS * ATT_HEAD_DIM
DILATIONS = (1, 4, 16)
ATT_SPAN = 128
ROPE_THETA = 10000.0
GLA_HEADS = 4
GLA_KEY_DIM = 64
GLA_VAL_DIM = 128
GLA_KEY_WIDTH = GLA_HEADS * GLA_KEY_DIM
GLA_VAL_WIDTH = GLA_HEADS * GLA_VAL_DIM
GLA_GATE_RANK = 16
GLA_GATE_NORMALIZER = 16.0
GLA_CHUNK = 64
N_EXPERTS = 32
TOP_K = 4
D_FF = D_MODEL
SWIGLU_LIMIT = 7.0
SWIGLU_ALPHA = 1.702
NORM_EPS = 1e-6
N_MOD = 6

LANES = 128
NEG_BIG = -1e30

TM_PROJ = 512
TQ_ATT = 1024
TG_GLA = 512
GLA_UNROLL = 8
TM_OUT = 512
OUT_SUBTILES = 1
TM_MOE = 512
TM_FIN = 512
TN_MOD = 512
TK_SPLIT = 512
TM_ROUTE = 1024
FINAL_PARTS = 4
SC_INDEX_WINDOW = 128
SC_GATHER_WINDOW = 64
VMEM_LIMIT = 48 * 1024 * 1024

_C_QA, _C_KA, _C_VA = 0, 512, 1024
_C_QG, _C_KG, _C_VG, _C_RG, _C_LR = 1536, 1792, 2048, 2560, 3072
D_IN_PACKED = 3200


def _rms(x, g):
    return x * lax.rsqrt(jnp.mean(x * x, axis=-1, keepdims=True) + NORM_EPS) * g


def _pack_bf16_pairs(x):
    n = x.shape[1] // 2
    u = lax.bitcast_convert_type(x.astype(jnp.bfloat16).astype(jnp.float32), jnp.uint32)
    return (u[:, :n] >> 16) | (u[:, n:] & jnp.uint32(0xFFFF0000))


def _unpack_bf16_pairs(w):
    lo = lax.bitcast_convert_type(w << 16, jnp.float32)
    hi = lax.bitcast_convert_type(w & jnp.uint32(0xFFFF0000), jnp.float32)
    return lo, hi


def _gather_rows(data, idx):
    n_rows = idx.shape[0]
    width = data.shape[1]
    mesh = plsc.VectorSubcoreMesh(core_axis_name="core", subcore_axis_name="subcore")
    n_workers = mesh.num_cores * mesh.num_subcores
    per_worker = n_rows // n_workers
    assert per_worker * n_workers == n_rows and per_worker % SC_INDEX_WINDOW == 0
    halves = SC_INDEX_WINDOW // SC_GATHER_WINDOW

    @pl.kernel(out_type=jax.ShapeDtypeStruct((n_rows, width), data.dtype), mesh=mesh,
               name="gather_rows",
               scratch_types=[pltpu.VMEM((1, SC_INDEX_WINDOW), jnp.int32),
                              pltpu.VMEM((SC_GATHER_WINDOW, width), data.dtype)])
    def gather(x_hbm, i_hbm, o_hbm, idx_vmem, rows_vmem):
        worker = lax.axis_index("core") * mesh.num_subcores + lax.axis_index("subcore")
        base = worker * per_worker

        @pl.loop(0, per_worker // SC_INDEX_WINDOW)
        def _(j):
            off = base + j * SC_INDEX_WINDOW
            pltpu.sync_copy(i_hbm.at[:, pl.ds(off, SC_INDEX_WINDOW)], idx_vmem)
            for h in range(halves):
                part = idx_vmem.at[0, pl.ds(h * SC_GATHER_WINDOW, SC_GATHER_WINDOW)]
                pltpu.sync_copy(x_hbm.at[part], rows_vmem)
                pltpu.sync_copy(rows_vmem,
                                o_hbm.at[pl.ds(off + h * SC_GATHER_WINDOW, SC_GATHER_WINDOW)])

    return gather(data, idx.reshape(1, n_rows))


def _scatter_rows(data, idx, n_out):
    n_copies, n_rows = idx.shape
    width = data.shape[1]
    mesh = plsc.VectorSubcoreMesh(core_axis_name="core", subcore_axis_name="subcore")
    n_workers = mesh.num_cores * mesh.num_subcores
    per_worker = n_rows // n_workers
    assert per_worker * n_workers == n_rows and per_worker % SC_INDEX_WINDOW == 0
    halves = SC_INDEX_WINDOW // SC_GATHER_WINDOW

    @pl.kernel(out_type=jax.ShapeDtypeStruct((n_out, width), data.dtype), mesh=mesh,
               name="scatter_rows",
               scratch_types=[pltpu.VMEM((n_copies, SC_INDEX_WINDOW), jnp.int32),
                              pltpu.VMEM((SC_GATHER_WINDOW, width), data.dtype)])
    def scatter(x_hbm, i_hbm, o_hbm, idx_vmem, rows_vmem):
        worker = lax.axis_index("core") * mesh.num_subcores + lax.axis_index("subcore")
        base = worker * per_worker

        @pl.loop(0, per_worker // SC_INDEX_WINDOW)
        def _(j):
            off = base + j * SC_INDEX_WINDOW
            pltpu.sync_copy(i_hbm.at[:, pl.ds(off, SC_INDEX_WINDOW)], idx_vmem)
            for h in range(halves):
                pltpu.sync_copy(x_hbm.at[pl.ds(off + h * SC_GATHER_WINDOW, SC_GATHER_WINDOW)],
                                rows_vmem)
                for k in range(n_copies):
                    part = idx_vmem.at[k, pl.ds(h * SC_GATHER_WINDOW, SC_GATHER_WINDOW)]
                    pltpu.sync_copy(rows_vmem, o_hbm.at[part])

    return scatter(data, idx)


def _mod_kernel(ct_ref, w_ref, b_ref, o_ref):
    ct = ct_ref[...]
    s = ct * (1.0 / (1.0 + jnp.exp(-ct)))
    w = w_ref[...]
    rows = [jnp.sum(s[:, b:b + 1] * w, axis=0, keepdims=True) for b in range(ct.shape[1])]
    o_ref[...] = jnp.concatenate(rows, axis=0) + b_ref[...]


def _mod(c, w_mod, b_mod):
    B = c.shape[0]
    n = w_mod.shape[1]
    return pl.pallas_call(
        _mod_kernel,
        name="mod",
        grid=(n // TN_MOD,),
        in_specs=[pl.BlockSpec((D_MODEL, B), lambda j: (0, 0)),
                  pl.BlockSpec((D_MODEL, TN_MOD), lambda j: (0, j)),
                  pl.BlockSpec((1, TN_MOD), lambda j: (0, j))],
        out_specs=pl.BlockSpec((B, TN_MOD), lambda j: (0, j)),
        out_shape=jax.ShapeDtypeStruct((B, n), jnp.float32),
        compiler_params=pltpu.CompilerParams(dimension_semantics=("arbitrary",),
                                             vmem_limit_bytes=VMEM_LIMIT),
    )(c.T, w_mod, b_mod.reshape(1, n))


def _store_residue_views(scr_ref, scr4_ref, out_refs):
    ns, tm, _ = scr_ref.shape
    w = ns * LANES
    o1_ref, o4_ref, o16_ref = out_refs
    assert DILATIONS == (1, 4, 16)
    q4 = tm // 4
    for s in range(ns):
        o1_ref[0, :, s * LANES:(s + 1) * LANES] = scr_ref[s].astype(o1_ref.dtype)
    for r in range(4):
        for s in range(ns):
            vals = scr_ref[s, pl.ds(r, q4, stride=4), :]
            scr4_ref[s, r * q4:(r + 1) * q4, :] = vals
            c0 = r * w + s * LANES
            o4_ref[0, :, c0:c0 + LANES] = vals.astype(o4_ref.dtype)
    for r in range(16):
        lo, hi = r % 4, r // 4
        for s in range(ns):
            vals = scr4_ref[s, pl.ds(lo * q4 + hi, tm // 16, stride=4), :]
            c0 = r * w + s * LANES
            o16_ref[0, :, c0:c0 + LANES] = vals.astype(o16_ref.dtype)


def _in_proj_kernel(x_ref, mod_ref, g_ref, w_ref, cos_ref, sin_ref, wlr_ref, bg_ref,
                    qa1_ref, qa4_ref, qa16_ref, ka1_ref, ka4_ref, ka16_ref,
                    va1_ref, va4_ref, va16_ref, qg_ref, kg_ref, vg_ref, sr_ref, la_ref,
                    qs_ref, ks_ref, vs_ref, q4_ref, k4_ref, v4_ref):
    x = x_ref[0]
    shift = mod_ref[0, 0:1, :]
    scale = mod_ref[0, 1:2, :]
    h = (_rms(x, g_ref[...] * (1.0 + scale)) + shift).astype(jnp.bfloat16)

    def proj(c0, width):
        return jnp.dot(h, w_ref[:, c0:c0 + width], preferred_element_type=jnp.float32)

    cos = cos_ref[...]
    sin = sin_ref[...]

    def rope(c0, scr_ref, mult):
        p = proj(c0, ATT_WIDTH)
        for s in range(ATT_WIDTH // LANES):
            t = p[:, s * LANES:(s + 1) * LANES]
            r = t * cos + pltpu.roll(t, LANES // 2, axis=1) * sin
            if mult != 1.0:
                r = r * mult
            scr_ref[s] = r

    rope(_C_QA, qs_ref, ATT_HEAD_DIM ** -0.5)
    _store_residue_views(qs_ref, q4_ref, (qa1_ref, qa4_ref, qa16_ref))
    rope(_C_KA, ks_ref, 1.0)
    _store_residue_views(ks_ref, k4_ref, (ka1_ref, ka4_ref, ka16_ref))
    pv = proj(_C_VA, ATT_WIDTH)
    for s in range(ATT_WIDTH // LANES):
        vs_ref[s] = pv[:, s * LANES:(s + 1) * LANES]
    _store_residue_views(vs_ref, v4_ref, (va1_ref, va4_ref, va16_ref))
    qg_ref[0] = proj(_C_QG, GLA_KEY_WIDTH).astype(qg_ref.dtype)
    kg_ref[0] = proj(_C_KG, GLA_KEY_WIDTH).astype(kg_ref.dtype)
    vg_ref[0] = proj(_C_VG, GLA_VAL_WIDTH).astype(vg_ref.dtype)
    r = proj(_C_RG, GLA_VAL_WIDTH)
    sr_ref[0] = (r * (1.0 / (1.0 + jnp.exp(-r)))).astype(sr_ref.dtype)
    lr = proj(_C_LR, LANES).astype(jnp.bfloat16)
    z = jnp.dot(lr, wlr_ref[...], preferred_element_type=jnp.float32) + bg_ref[...]
    log_sig = jnp.minimum(z, 0.0) - jnp.log(1.0 + jnp.exp(-jnp.abs(z)))
    la_ref[0] = log_sig * (1.0 / GLA_GATE_NORMALIZER)


def _in_proj(x, mod3, g_pre, w_packed, cos_t, sin_t, wlr, bg):
    B, S, D = x.shape
    nb = S // TM_PROJ
    tok = lambda w: pl.BlockSpec((1, TM_PROJ, w), lambda b, i: (b, i, 0))
    const = lambda shape: pl.BlockSpec(shape, lambda b, i: tuple(0 for _ in shape))
    bf = jnp.bfloat16
    att_shapes = [jax.ShapeDtypeStruct((B, S // d, d * ATT_WIDTH), bf) for d in DILATIONS] * 3
    att_specs = [pl.BlockSpec((1, TM_PROJ // d, d * ATT_WIDTH), lambda b, i: (b, i, 0))
                 for d in DILATIONS] * 3
    gla_shapes = [jax.ShapeDtypeStruct((B, S, w), dt) for w, dt in
                  [(GLA_KEY_WIDTH, bf), (GLA_KEY_WIDTH, bf),
                   (GLA_VAL_WIDTH, bf), (GLA_VAL_WIDTH, bf), (GLA_KEY_WIDTH, jnp.float32)]]
    return pl.pallas_call(
        _in_proj_kernel,
        name="in_proj",
        grid=(B, nb),
        in_specs=[tok(D),
                  pl.BlockSpec((1, N_MOD, D), lambda b, i: (b, 0, 0)),
                  const((1, D)),
                  const((D, D_IN_PACKED)),
                  pl.BlockSpec((TM_PROJ, LANES), lambda b, i: (i, 0)),
                  pl.BlockSpec((TM_PROJ, LANES), lambda b, i: (i, 0)),
                  const((LANES, GLA_KEY_WIDTH)),
                  const((1, GLA_KEY_WIDTH))],
        out_specs=att_specs + [tok(s.shape[-1]) for s in gla_shapes],
        out_shape=att_shapes + gla_shapes,
        scratch_shapes=[pltpu.VMEM((ATT_WIDTH // LANES, TM_PROJ, LANES), jnp.float32)] * 6,
        compiler_params=pltpu.CompilerParams(dimension_semantics=("arbitrary", "arbitrary"),
                                             vmem_limit_bytes=VMEM_LIMIT),
    )(x, mod3, g_pre, w_packed, cos_t, sin_t, wlr, bg)


def _attn_kernel(q_ref, kp_ref, kc_ref, vp_ref, vc_ref, o_ref, st_ref):
    i = pl.program_id(2)
    tp = kp_ref.shape[1]
    tq = tp
    n_sub = q_ref.shape[1] // tq
    row = lax.broadcasted_iota(jnp.int32, (tq, tp + tq), 0)
    col = lax.broadcasted_iota(jnp.int32, (tq, tp + tq), 1)
    band = (col >= row) & (col <= row + ATT_SPAN)
    bias_inner = jnp.where(band, 0.0, NEG_BIG)
    bias_first = jnp.where(band & ((col >= tp) | (i > 0)), 0.0, NEG_BIG)
    lane = lax.broadcasted_iota(jnp.int32, (tq, LANES), 1)
    q_head = (lane >> 5) & 1
    for sb in range(n_sub):
        rows = slice(sb * tq, (sb + 1) * tq)
        bias = bias_first if sb == 0 else bias_inner
        stats = jnp.zeros((tq, LANES), jnp.float32)
        for hp in range(ATT_HEADS // 2):
            sl = slice(hp * LANES, (hp + 1) * LANES)
            q2 = q_ref[0, rows, sl]
            if sb == 0:
                kcat = jnp.concatenate([kp_ref[0, :, sl], kc_ref[0, rows, sl]], axis=0)
                vcat = jnp.concatenate([vp_ref[0, :, sl], vc_ref[0, rows, sl]], axis=0)
            else:
                kcat = kc_ref[0, (sb - 1) * tq:(sb + 1) * tq, sl]
                vcat = vc_ref[0, (sb - 1) * tq:(sb + 1) * tq, sl]
            outs = []
            for hh in range(2):
                qm = jnp.where(q_head == hh, q2, jnp.zeros_like(q2))
                s = lax.dot_general(qm, kcat, (((1,), (1,)), ((), ())),
                                    preferred_element_type=jnp.float32) + bias
                m = jnp.max(s, axis=1, keepdims=True)
                p = jnp.exp(s - m)
                l = jnp.sum(p, axis=1, keepdims=True)
                o = jnp.dot(p.astype(vcat.dtype), vcat, preferred_element_type=jnp.float32)
                outs.append(o / l)
                stats = jnp.where(lane == 2 * hp + hh, m + jnp.log(l), stats)
            o_ref[0, rows, sl] = jnp.where(lane < ATT_HEAD_DIM, outs[0], outs[1]).astype(o_ref.dtype)
        st_ref[0, rows, :] = stats


def _attn_pattern(qv, kv, vv, dil):
    B, L, _ = qv.shape
    W = ATT_WIDTH
    tq = min(TQ_ATT, L)
    nq = L // tq
    cur = pl.BlockSpec((1, tq, W), lambda b, r, i: (b, i, r))
    back = tq // ATT_SPAN
    prev = pl.BlockSpec((1, ATT_SPAN, W), lambda b, r, i: (b, jnp.maximum(i * back - 1, 0), r))
    o, st = pl.pallas_call(
        _attn_kernel,
        name=f"attn_d{dil}",
        grid=(B, dil, nq),
        in_specs=[cur, prev, cur, prev, cur],
        out_specs=[cur, pl.BlockSpec((1, tq, LANES), lambda b, r, i: (b, i, r))],
        out_shape=[jax.ShapeDtypeStruct((B, L, dil * W), jnp.bfloat16),
                   jax.ShapeDtypeStruct((B, L, dil * LANES), jnp.float32)],
        compiler_params=pltpu.CompilerParams(
            dimension_semantics=("arbitrary", "arbitrary", "arbitrary"),
            vmem_limit_bytes=VMEM_LIMIT),
    )(qv, kv, kv, vv, vv)
    return o, st


def _cumsum_rows(x):
    n = x.shape[0]
    row = lax.broadcasted_iota(jnp.int32, x.shape, 0)
    s = 1
    while s < n:
        x = x + jnp.where(row >= s, pltpu.roll(x, s, axis=0), 0.0)
        s *= 2
    return x


def _gla_kernel(q_ref, k_ref, v_ref, la_ref, sr_ref, g_ref, o_ref, st_ref):
    @pl.when(pl.program_id(0) == 0)
    def _():
        st_ref[...] = jnp.zeros_like(st_ref)

    C = GLA_CHUNK
    n_batch = q_ref.shape[0]
    n_chunks = q_ref.shape[1] // C
    lane = lax.broadcasted_iota(jnp.int32, (C, LANES), 1)
    lane_sq = lax.broadcasted_iota(jnp.int32, (LANES, LANES), 1)
    tril = (lax.broadcasted_iota(jnp.int32, (C, C), 0)
            >= lax.broadcasted_iota(jnp.int32, (C, C), 1))
    g = g_ref[...]
    bf = jnp.bfloat16

    def chunk(c, carry):
        r0 = pl.multiple_of(c * C, C)
        rows = pl.ds(r0, C)
        for n in range(n_batch):
            for p in range(GLA_HEADS // 2):
                ksl = slice(p * LANES, (p + 1) * LANES)
                b = _cumsum_rows(la_ref[n, rows, ksl])
                b_last = b[C - 1:C, :]
                q = q_ref[n, rows, ksl].astype(jnp.float32) * (GLA_KEY_DIM ** -0.5)
                k = k_ref[n, rows, ksl].astype(jnp.float32)
                q_dec = q * jnp.exp(b)
                k_inv = (k * jnp.exp(-b)).astype(bf)
                k_dec = (k * jnp.exp(b_last - b)).astype(bf)
                st = st_ref[n, p]
                st_b = st.astype(bf)
                ut = []
                for hh in range(2):
                    h = 2 * p + hh
                    vsl = slice(h * GLA_VAL_DIM, (h + 1) * GLA_VAL_DIM)
                    v = v_ref[n, rows, vsl]
                    own = (lane >= GLA_KEY_DIM) if hh else (lane < GLA_KEY_DIM)
                    qm = jnp.where(own, q_dec, 0.0).astype(bf)
                    att = lax.dot_general(qm, k_inv, (((1,), (1,)), ((), ())),
                                          preferred_element_type=jnp.float32)
                    att = jnp.where(tril, att, 0.0).astype(bf)
                    o = jnp.dot(att, v, preferred_element_type=jnp.float32)
                    o = o + lax.dot_general(qm, st_b, (((1,), (1,)), ((), ())),
                                            preferred_element_type=jnp.float32)
                    o = _rms(o, g) * sr_ref[n, rows, vsl].astype(jnp.float32)
                    o_ref[n, rows, vsl] = o.astype(o_ref.dtype)
                    ut.append(lax.dot_general(v, k_dec, (((0,), (0,)), ((), ())),
                                              preferred_element_type=jnp.float32))
                st_ref[n, p] = (st * jnp.exp(b_last)
                                + jnp.where(lane_sq < GLA_KEY_DIM, ut[0], ut[1]))
        return carry

    lax.fori_loop(0, n_chunks, chunk, 0, unroll=GLA_UNROLL)


def _gla(qg, kg, vg, la, sr, g_gla):
    B, S, _ = qg.shape
    tok = lambda w: pl.BlockSpec((B, TG_GLA, w), lambda i: (0, i, 0))
    return pl.pallas_call(
        _gla_kernel,
        name="gla",
        grid=(S // TG_GLA,),
        in_specs=[tok(GLA_KEY_WIDTH), tok(GLA_KEY_WIDTH), tok(GLA_VAL_WIDTH),
                  tok(GLA_KEY_WIDTH), tok(GLA_VAL_WIDTH),
                  pl.BlockSpec((1, GLA_VAL_DIM), lambda i: (0, 0))],
        out_specs=tok(GLA_VAL_WIDTH),
        out_shape=jax.ShapeDtypeStruct((B, S, GLA_VAL_WIDTH), jnp.bfloat16),
        scratch_shapes=[pltpu.VMEM((B, GLA_HEADS // 2, GLA_VAL_DIM, LANES), jnp.float32)],
        compiler_params=pltpu.CompilerParams(dimension_semantics=("arbitrary",),
                                             vmem_limit_bytes=VMEM_LIMIT),
    )(qg, kg, vg, la, sr, g_gla)


def _out_proj_kernel(o1_ref, o2_ref, o3_ref, s1_ref, s2_ref, s3_ref, og_ref, x_ref, mod_ref,
                     wout_ref, gpost_ref, gpre_ref, rw_ref, rb_ref,
                     x1_ref, h2_ref, ti_ref, tg_ref, cnt_ref, oscr_ref, sscr_ref, otmp_ref,
                     stmp_ref):
    tm = x_ref.shape[1]

    @pl.when((pl.program_id(0) == 0) & (pl.program_id(1) == 0))
    def _():
        cnt_ref[...] = jnp.zeros_like(cnt_ref)

    ns = ATT_WIDTH // LANES
    q4 = tm // 4
    for r in range(4):
        rows = pl.ds(r, q4, stride=4)
        for s in range(ns):
            c0 = r * ATT_WIDTH + s * LANES
            oscr_ref[0, s, rows, :] = o2_ref[0, :, c0:c0 + LANES].astype(jnp.float32)
        sscr_ref[0, rows, :] = s2_ref[0, :, r * LANES:(r + 1) * LANES]
    for r in range(16):
        lo, hi = r % 4, r // 4
        rows = pl.ds(lo * q4 + hi, tm // 16, stride=4)
        for s in range(ns):
            c0 = r * ATT_WIDTH + s * LANES
            otmp_ref[s, rows, :] = o3_ref[0, :, c0:c0 + LANES].astype(jnp.float32)
        stmp_ref[rows, :] = s3_ref[0, :, r * LANES:(r + 1) * LANES]
    for r in range(4):
        rows = pl.ds(r, q4, stride=4)
        for s in range(ns):
            oscr_ref[1, s, rows, :] = otmp_ref[s, r * q4:(r + 1) * q4, :]
        sscr_ref[1, rows, :] = stmp_ref[r * q4:(r + 1) * q4, :]
    gate1 = mod_ref[0, 2:3, :]
    shift2 = mod_ref[0, 3:4, :]
    scale2 = mod_ref[0, 4:5, :]
    bf = jnp.bfloat16
    rw = rw_ref[...]
    w_hi = rw.astype(bf)
    w_lo = (rw - w_hi.astype(jnp.float32)).astype(bf)
    ts = tm // OUT_SUBTILES
    lane = lax.broadcasted_iota(jnp.int32, (ts, LANES), 1)
    lane_f = lane.astype(jnp.float32)
    for sub in range(OUT_SUBTILES):
        rs = slice(sub * ts, (sub + 1) * ts)
        lses = [s1_ref[0, rs, :], sscr_ref[0, rs, :], sscr_ref[1, rs, :]]
        m = jnp.maximum(jnp.maximum(lses[0], lses[1]), lses[2])
        es = [jnp.exp(t - m) for t in lses]
        inv = 1.0 / (es[0] + es[1] + es[2])
        ws = [e * inv for e in es]
        pairs = []
        for hp in range(ATT_HEADS // 2):
            sl = slice(hp * LANES, (hp + 1) * LANES)
            o_pats = [o1_ref[0, rs, sl].astype(jnp.float32), oscr_ref[0, hp, rs, :],
                      oscr_ref[1, hp, rs, :]]
            acc = jnp.zeros((ts, LANES), jnp.float32)
            head_of_lane = 2 * hp + (lane >> 6)
            for w, o in zip(ws, o_pats):
                acc = acc + jnp.take_along_axis(w, head_of_lane, axis=1) * o
            pairs.append(acc.astype(bf))
        mixed = jnp.concatenate(pairs + [og_ref[0, rs, :]], axis=1)
        y = jnp.dot(mixed, wout_ref[...], preferred_element_type=jnp.float32)
        x1 = x_ref[0, rs, :] + _rms(y, gate1 * gpost_ref[...])
        x1_ref[0, rs, :] = x1
        h2 = _rms(x1, gpre_ref[...] * (1.0 + scale2)) + shift2
        h2_ref[0, rs, :] = _pack_bf16_pairs(h2)
        h_hi = h2.astype(bf)
        h_lo = (h2 - h_hi.astype(jnp.float32)).astype(bf)
        logits = (jnp.dot(h_hi, w_hi, preferred_element_type=jnp.float32)
                  + jnp.dot(h_lo, w_hi, preferred_element_type=jnp.float32)
                  + jnp.dot(h_hi, w_lo, preferred_element_type=jnp.float32)) + rb_ref[...]
        vals, idxs = [], []
        for _ in range(TOP_K):
            mk = jnp.max(logits, axis=1, keepdims=True)
            ik = jnp.min(jnp.where(logits == mk, lane_f, float(LANES)), axis=1, keepdims=True)
            logits = jnp.where(lane_f == ik, -jnp.inf, logits)
            vals.append(mk)
            idxs.append(ik)
        ex = [jnp.exp(v - vals[0]) for v in vals]
        den = ex[0] + ex[1] + ex[2] + ex[3]
        ti = jnp.zeros((ts, LANES), jnp.float32)
        tg = jnp.zeros((ts, LANES), jnp.float32)
        for kk in range(TOP_K):
            ti = jnp.where(lane == kk, idxs[kk], ti)
            tg = jnp.where(lane == kk, ex[kk] / den, tg)
        ti_ref[0, rs, :] = ti.astype(jnp.int32)
        tg_ref[0, rs, :] = tg
        chosen = jnp.zeros((ts, LANES), jnp.float32)
        for ik in idxs:
            chosen = chosen + jnp.where(lane_f == ik, 1.0, 0.0)
        cnt_ref[...] += jnp.sum(chosen, axis=0, keepdims=True)


def _out_proj(o_pats, st_pats, og, x, mod3, w_out, g_post, g_pre, rw, rb):
    B, S, D = x.shape
    tok = lambda w: pl.BlockSpec((1, TM_OUT, w), lambda b, i: (b, i, 0))
    const = lambda shape: pl.BlockSpec(shape, lambda b, i: tuple(0 for _ in shape))
    out_shapes = [jax.ShapeDtypeStruct((B, S, D), jnp.float32),
                  jax.ShapeDtypeStruct((B, S, D // 2), jnp.uint32),
                  jax.ShapeDtypeStruct((B, S, LANES), jnp.int32),
                  jax.ShapeDtypeStruct((B, S, LANES), jnp.float32),
                  jax.ShapeDtypeStruct((8, LANES), jnp.float32)]
    return pl.pallas_call(
        _out_proj_kernel,
        name="out_proj",
        grid=(B, S // TM_OUT),
        in_specs=[pl.BlockSpec((1, TM_OUT // d, d * ATT_WIDTH), lambda b, i: (b, i, 0))
                  for d in DILATIONS]
                 + [pl.BlockSpec((1, TM_OUT // d, d * LANES), lambda b, i: (b, i, 0))
                    for d in DILATIONS]
                 + [tok(GLA_VAL_WIDTH), tok(D),
                  pl.BlockSpec((1, N_MOD, D), lambda b, i: (b, 0, 0)),
                  const((D, D)), const((1, D)), const((1, D)),
                  const((D, LANES)), const((1, LANES))],
        out_specs=[tok(D), tok(D // 2), tok(LANES), tok(LANES), const((8, LANES))],
        out_shape=out_shapes,
        scratch_shapes=[pltpu.VMEM((2, ATT_WIDTH // LANES, TM_OUT, LANES), jnp.float32),
                        pltpu.VMEM((2, TM_OUT, LANES), jnp.float32),
                        pltpu.VMEM((ATT_WIDTH // LANES, TM_OUT, LANES), jnp.float32),
                        pltpu.VMEM((TM_OUT, LANES), jnp.float32)],
        compiler_params=pltpu.CompilerParams(dimension_semantics=("arbitrary", "arbitrary"),
                                             vmem_limit_bytes=VMEM_LIMIT),
    )(*o_pats, *st_pats, og, x, mod3, w_out, g_post, g_pre, rw, rb)


def _split_gate_up_kernel(w_ref, wg_ref, wl_ref):
    group = 2 * LANES
    src = lax.broadcasted_iota(jnp.int32, (group, group), 0)
    dst = lax.broadcasted_iota(jnp.int32, (group, group), 1)
    want = jnp.where(dst < LANES, 2 * dst, 2 * (dst - LANES) + 1)
    perm = jnp.where(src == want, 1.0, 0.0).astype(jnp.bfloat16)
    for j in range(w_ref.shape[1] // group):
        t = jnp.dot(w_ref[:, j * group:(j + 1) * group].astype(jnp.bfloat16), perm,
                    preferred_element_type=jnp.float32)
        wg_ref[:, j * LANES:(j + 1) * LANES] = t[:, :LANES].astype(wg_ref.dtype)
        wl_ref[:, j * LANES:(j + 1) * LANES] = t[:, LANES:].astype(wl_ref.dtype)


def _split_gate_up(w_gate_up):
    E, K, N2 = w_gate_up.shape
    out = jax.ShapeDtypeStruct((E, K, N2 // 2), jnp.bfloat16)
    ospec = pl.BlockSpec((None, TK_SPLIT, N2 // 2), lambda e, i: (e, i, 0))
    return pl.pallas_call(
        _split_gate_up_kernel,
        name="split_gate_up",
        grid=(E, K // TK_SPLIT),
        in_specs=[pl.BlockSpec((None, TK_SPLIT, N2), lambda e, i: (e, i, 0))],
        out_specs=[ospec, ospec],
        out_shape=[out, out],
        compiler_params=pltpu.CompilerParams(dimension_semantics=("arbitrary", "arbitrary"),
                                             vmem_limit_bytes=VMEM_LIMIT),
    )(w_gate_up)


def _moe_kernel(be_ref, nu_ref, x_ref, wg_ref, wl_ref, bg_ref, bl_ref, wd_ref, bd_ref, o_ref,
                wd_bf_ref):
    i = pl.program_id(0)
    new_expert = (i == 0) | (be_ref[i] != be_ref[jnp.maximum(i - 1, 0)])

    @pl.when(new_expert & (i < nu_ref[0]))
    def _():
        wd_bf_ref[...] = wd_ref[...].astype(wd_bf_ref.dtype)

    @pl.when(i < nu_ref[0])
    def _():
        lo, hi = _unpack_bf16_pairs(x_ref[...])
        x = jnp.concatenate([lo, hi], axis=1).astype(jnp.bfloat16)
        xg = jnp.dot(x, wg_ref[...], preferred_element_type=jnp.float32) + bg_ref[...]
        xl = jnp.dot(x, wl_ref[...], preferred_element_type=jnp.float32) + bl_ref[...]
        xg = jnp.minimum(xg, SWIGLU_LIMIT)
        xl = jnp.clip(xl, -SWIGLU_LIMIT, SWIGLU_LIMIT)
        act = xg * (1.0 / (1.0 + jnp.exp(-SWIGLU_ALPHA * xg))) * (xl + 1.0)
        out = jnp.dot(act.astype(jnp.bfloat16), wd_bf_ref[...],
                      preferred_element_type=jnp.float32) + bd_ref[...]
        o_ref[...] = _pack_bf16_pairs(out)

    @pl.when(i >= nu_ref[0])
    def _():
        o_ref[...] = jnp.zeros_like(o_ref)


def _moe(xs, blk_e, n_used, wg, wl, bg, bl, wd, bd):
    P, half = xs.shape
    D = 2 * half
    n_blocks = P // TM_MOE
    wspec = lambda k, n: pl.BlockSpec((None, k, n), lambda i, be, nu: (be[i], 0, 0))
    grid_spec = pltpu.PrefetchScalarGridSpec(
        num_scalar_prefetch=2,
        grid=(n_blocks,),
        in_specs=[pl.BlockSpec((TM_MOE, half), lambda i, be, nu: (i, 0)),
                  wspec(D, D_FF), wspec(D, D_FF), wspec(1, D_FF), wspec(1, D_FF),
                  wspec(D_FF, D), wspec(1, D)],
        out_specs=pl.BlockSpec((TM_MOE, half), lambda i, be, nu: (i, 0)),
        scratch_shapes=[pltpu.VMEM((D_FF, D), jnp.bfloat16)],
    )
    return pl.pallas_call(
        _moe_kernel,
        name="moe",
        grid_spec=grid_spec,
        out_shape=jax.ShapeDtypeStruct((P, half), jnp.uint32),
        compiler_params=pltpu.CompilerParams(dimension_semantics=("arbitrary",),
                                             vmem_limit_bytes=VMEM_LIMIT),
    )(blk_e, n_used, xs, wg, wl, bg, bl, wd, bd)


def _final_kernel(x1_ref, y0_ref, y1_ref, y2_ref, y3_ref, tg_ref, mod_ref, g_ref, o_ref):
    gates = tg_ref[...]
    half = y0_ref.shape[1]
    y_lo = jnp.zeros((x1_ref.shape[0], half), jnp.float32)
    y_hi = jnp.zeros((x1_ref.shape[0], half), jnp.float32)
    for kk, yk_ref in enumerate((y0_ref, y1_ref, y2_ref, y3_ref)):
        lo, hi = _unpack_bf16_pairs(yk_ref[...])
        y_lo = y_lo + gates[:, kk:kk + 1] * lo
        y_hi = y_hi + gates[:, kk:kk + 1] * hi
    y = jnp.concatenate([y_lo, y_hi], axis=1)
    gate2 = mod_ref[0, 5:6, :]
    o_ref[...] = x1_ref[...] + _rms(y, gate2 * g_ref[...])


def _final_part_kernel(x1_ref, y0_ref, y1_ref, y2_ref, y3_ref, tg_ref, mod_ref, g_ref, prev_ref,
                       o_ref):
    del prev_ref
    _final_kernel(x1_ref, y0_ref, y1_ref, y2_ref, y3_ref, tg_ref, mod_ref, g_ref, o_ref)


def _final(x1, yk, tg, mod3, g_post, seq_len, part, prev):
    T, D = x1.shape
    tp = yk.shape[0] // TOP_K
    nt = tp // TM_FIN
    t0 = part * nt
    batch = (part * tp) // seq_len
    yspec = lambda kk: pl.BlockSpec((TM_FIN, D // 2), lambda i: (kk * nt + i, 0))
    in_specs = ([pl.BlockSpec((TM_FIN, D), lambda i: (t0 + i, 0))]
                + [yspec(kk) for kk in range(TOP_K)]
                + [pl.BlockSpec((TM_FIN, LANES), lambda i: (t0 + i, 0)),
                   pl.BlockSpec((1, N_MOD, D), lambda i: (batch, 0, 0)),
                   pl.BlockSpec((1, D), lambda i: (0, 0))])
    args = [x1, yk, yk, yk, yk, tg, mod3, g_post]
    body, aliases = _final_kernel, {}
    if prev is not None:
        in_specs.append(pl.BlockSpec(memory_space=pl.ANY))
        args.append(prev)
        body, aliases = _final_part_kernel, {len(args) - 1: 0}
    return pl.pallas_call(
        body,
        name="final",
        grid=(nt,),
        in_specs=in_specs,
        out_specs=pl.BlockSpec((TM_FIN, D), lambda i: (t0 + i, 0)),
        out_shape=jax.ShapeDtypeStruct((T, D), jnp.float32),
        input_output_aliases=aliases,
        compiler_params=pltpu.CompilerParams(dimension_semantics=("arbitrary",),
                                             vmem_limit_bytes=VMEM_LIMIT),
    )(*args)


def _pack_w_in(w_in):
    half = ATT_HEAD_DIM // 2

    def pair_rotary_layout(w):
        k = w.shape[0]
        w = w.reshape(k, ATT_HEADS // 2, 2, 2, half)
        return w.transpose(0, 1, 3, 2, 4).reshape(k, ATT_WIDTH)

    w_in = w_in.astype(jnp.bfloat16)
    lr = jnp.pad(w_in[:, 3072:3072 + GLA_GATE_RANK], ((0, 0), (0, LANES - GLA_GATE_RANK)))
    return jnp.concatenate([pair_rotary_layout(w_in[:, :ATT_WIDTH]),
                            pair_rotary_layout(w_in[:, ATT_WIDTH:2 * ATT_WIDTH]),
                            w_in[:, 1024:3072], lr], axis=1)


def _rope_tables(seq_len):
    half = ATT_HEAD_DIM // 2
    inv_freq = ROPE_THETA ** (-jnp.arange(half, dtype=jnp.float32) / half)
    ang = jnp.arange(seq_len, dtype=jnp.float32)[:, None] * inv_freq[None, :]
    cos = jnp.tile(jnp.cos(ang), (1, LANES // half))
    sin = jnp.tile(jnp.sin(ang), (1, LANES // half))
    sign = jnp.where(jnp.arange(LANES) < LANES // 2, -1.0, 1.0)
    return cos, sin * sign


def _route_kernel(ti_ref, cnt_ref, pos_ref, blk_ref, nu_ref, base_ref, tri_ref):
    j = pl.program_id(0)
    tm = ti_ref.shape[0]
    lane = lax.broadcasted_iota(jnp.int32, (tm, LANES), 1)
    ti = ti_ref[...]
    hots = [lane == ti[:, k:k + 1] for k in range(TOP_K)]
    hot_all = jnp.zeros((tm, LANES), jnp.float32)
    for h in hots:
        hot_all = hot_all + jnp.where(h, 1.0, 0.0)
    tile_cnt = jnp.sum(hot_all, axis=0, keepdims=True)

    @pl.when(j == 0)
    def _():
        r = lax.broadcasted_iota(jnp.int32, (tm, tm), 0)
        c = lax.broadcasted_iota(jnp.int32, (tm, tm), 1)
        tri_ref[...] = jnp.where(c < r, 1.0, 0.0).astype(tri_ref.dtype)
        shift = TM_MOE.bit_length() - 1
        lane8 = lax.broadcasted_iota(jnp.int32, cnt_ref.shape, 1)
        cnt = cnt_ref[...].astype(jnp.int32)
        padded = ((cnt + (TM_MOE - 1)) >> shift) << shift
        pend = padded
        s = 1
        while s < N_EXPERTS:
            pend = pend + jnp.where(lane8 >= s, pltpu.roll(pend, s, axis=1), 0)
            s *= 2
        base_ref[...] = (pend - padded).astype(jnp.float32)
        nb = blk_ref.shape[0]
        blk_start = lax.broadcasted_iota(jnp.int32, (nb, LANES), 0) * TM_MOE
        lane_b = lax.broadcasted_iota(jnp.int32, (nb, LANES), 1)
        done = jnp.where((pend[0:1, :] <= blk_start) & (lane_b < N_EXPERTS), 1.0, 0.0)
        blk = jnp.minimum(jnp.sum(done, axis=1, keepdims=True), float(N_EXPERTS - 1))
        blk_ref[...] = jnp.broadcast_to(blk, blk_ref.shape).astype(jnp.int32)
        total = jnp.max(jnp.where(lane8 < N_EXPERTS, pend, 0).astype(jnp.float32),
                        axis=1, keepdims=True)
        nu_ref[...] = jnp.broadcast_to(total.astype(jnp.int32) >> shift, nu_ref.shape)

    before = jnp.dot(tri_ref[...], hot_all.astype(tri_ref.dtype),
                     preferred_element_type=jnp.float32)
    slot = before + base_ref[0:1, :]
    pos = jnp.zeros((tm, LANES), jnp.float32)
    for k, h in enumerate(hots):
        pk = jnp.sum(jnp.where(h, slot, 0.0), axis=1, keepdims=True)
        pos = jnp.where(lane == k, pk, pos)
    pos_ref[...] = pos.astype(jnp.int32)
    base_ref[...] += tile_cnt


def _route(ti, cnt, n_tokens):
    A = n_tokens * TOP_K
    n_blocks = -(-(A + N_EXPERTS * (TM_MOE - 1)) // TM_MOE)
    nb_pad = -(-n_blocks // 8) * 8
    nt = n_tokens // TM_ROUTE
    pos, blk, nu = pl.pallas_call(
        _route_kernel,
        name="route",
        grid=(nt,),
        in_specs=[pl.BlockSpec((TM_ROUTE, LANES), lambda j: (j, 0)),
                  pl.BlockSpec((8, LANES), lambda j: (0, 0))],
        out_specs=[pl.BlockSpec((TM_ROUTE, LANES), lambda j: (j, 0)),
                   pl.BlockSpec((nb_pad, LANES), lambda j: (0, 0)),
                   pl.BlockSpec((8, LANES), lambda j: (0, 0))],
        out_shape=[jax.ShapeDtypeStruct((n_tokens, LANES), jnp.int32),
                   jax.ShapeDtypeStruct((nb_pad, LANES), jnp.int32),
                   jax.ShapeDtypeStruct((8, LANES), jnp.int32)],
        scratch_shapes=[pltpu.VMEM((8, LANES), jnp.float32),
                        pltpu.VMEM((TM_ROUTE, TM_ROUTE), jnp.bfloat16)],
        compiler_params=pltpu.CompilerParams(dimension_semantics=("arbitrary",),
                                             vmem_limit_bytes=VMEM_LIMIT),
    )(ti, cnt)
    pos_kmajor = pos[:, :TOP_K].T
    return pos_kmajor, n_blocks * TM_MOE, blk[:n_blocks, 0], nu[0, :1]


def kernel(x, c, w_mod, b_mod, g_pre_mix, w_in, w_gate_lr, b_gate, g_gla, w_out, g_post_mix,
           g_pre_ffn, router_w, router_b, w_gate_up, b_gate_up, w_down, b_down, g_post_ffn):
    B, S, D = x.shape
    T = B * S
    bf = jnp.bfloat16
    cos_t, sin_t = _rope_tables(S)
    for l in range(w_mod.shape[0]):
        wg, wl = _split_gate_up(w_gate_up[l])
        wd = w_down[l]
        mod3 = _mod(c, w_mod[l], b_mod[l]).reshape(B, N_MOD, D)
        wlr = jnp.pad(w_gate_lr[l], ((0, LANES - GLA_GATE_RANK), (0, 0))).astype(bf)
        proj = _in_proj(x, mod3, g_pre_mix[l][None], _pack_w_in(w_in[l]), cos_t, sin_t, wlr,
                        b_gate[l][None])
        n_pat = len(DILATIONS)
        qa, ka, va = proj[:n_pat], proj[n_pat:2 * n_pat], proj[2 * n_pat:3 * n_pat]
        qg, kg, vg, sr, la = proj[3 * n_pat:]
        pats = [_attn_pattern(qa[j], ka[j], va[j], d) for j, d in enumerate(DILATIONS)]
        og = _gla(qg, kg, vg, la, sr, g_gla[l][None])
        rw = jnp.pad(router_w[l], ((0, 0), (0, LANES - N_EXPERTS)))
        rb = jnp.pad(router_b[l], (0, LANES - N_EXPERTS), constant_values=NEG_BIG)[None]
        x1, h2, ti, tg, cnt = _out_proj([p[0] for p in pats], [p[1] for p in pats], og, x, mod3,
                                        w_out[l].astype(bf), g_post_mix[l][None],
                                        g_pre_ffn[l][None], rw, rb)
        pos_kmajor, n_slots, blk_e, n_used = _route(ti.reshape(T, LANES), cnt, T)
        xs = _scatter_rows(h2.reshape(T, D // 2), pos_kmajor, n_slots)
        out_buf = _moe(xs, blk_e, n_used, wg, wl,
                       b_gate_up[l][:, None, 0::2], b_gate_up[l][:, None, 1::2],
                       wd, b_down[l][:, None, :])
        out = None
        tp = T // FINAL_PARTS
        for part in range(FINAL_PARTS):
            yk = _gather_rows(out_buf, pos_kmajor[:, part * tp:(part + 1) * tp].reshape(-1))
            out = _final(x1.reshape(T, D), yk, tg.reshape(T, LANES), mod3, g_post_ffn[l][None],
                         S, part, out)
        x = out.reshape(B, S, D)
    return x
```

```python
import functools

import numpy as np
import jax
import jax.numpy as jnp
from jax import lax
from jax.experimental import pallas as pl
from jax.experimental.pallas import tpu as pltpu
from jax.experimental.pallas import tpu_sc as plsc

D_MODEL = 1024
ATT_HEADS = 8
ATT_HEAD_DIM = 64
ATT_WIDTH = ATT_HEADS * ATT_HEAD_DIM
DILATIONS = (1, 4, 16)
ATT_SPAN = 128
ROPE_THETA = 10000.0
GLA_HEADS = 4
GLA_KEY_DIM = 64
GLA_VAL_DIM = 128
GLA_KEY_WIDTH = GLA_HEADS * GLA_KEY_DIM
GLA_VAL_WIDTH = GLA_HEADS * GLA_VAL_DIM
GLA_GATE_RANK = 16
GLA_GATE_NORMALIZER = 16.0
GLA_CHUNK = 64
N_EXPERTS = 32
TOP_K = 4
D_FF = D_MODEL
SWIGLU_LIMIT = 7.0
SWIGLU_ALPHA = 1.702
NORM_EPS = 1e-6
N_MOD = 6

LANES = 128
NEG_BIG = -1e30

TM_PROJ = 512
TQ_ATT = 1024
TG_GLA = 512
GLA_UNROLL = 8
TM_OUT = 512
OUT_SUBTILES = 1
TM_MOE = 512
TM_FIN = 512
TN_MOD = 512
TK_SPLIT = 512
TM_ROUTE = 1024
FINAL_PARTS = 4
SC_INDEX_WINDOW = 128
SC_GATHER_WINDOW = 64
VMEM_LIMIT = 48 * 1024 * 1024

_C_QA, _C_KA, _C_VA = 0, 512, 1024
_C_QG, _C_KG, _C_VG, _C_RG, _C_LR = 1536, 1792, 2048, 2560, 3072
D_IN_PACKED = 3200


def _rms(x, g):
    return x * lax.rsqrt(jnp.mean(x * x, axis=-1, keepdims=True) + NORM_EPS) * g


def _pack_bf16_pairs(x):
    n = x.shape[1] // 2
    u = lax.bitcast_convert_type(x.astype(jnp.bfloat16).astype(jnp.float32), jnp.uint32)
    return (u[:, :n] >> 16) | (u[:, n:] & jnp.uint32(0xFFFF0000))


def _unpack_bf16_pairs(w):
    lo = lax.bitcast_convert_type(w << 16, jnp.float32)
    hi = lax.bitcast_convert_type(w & jnp.uint32(0xFFFF0000), jnp.float32)
    return lo, hi


def _gather_rows(data, idx):
    n_rows = idx.shape[0]
    width = data.shape[1]
    mesh = plsc.VectorSubcoreMesh(core_axis_name="core", subcore_axis_name="subcore")
    n_workers = mesh.num_cores * mesh.num_subcores
    per_worker = n_rows // n_workers
    assert per_worker * n_workers == n_rows and per_worker % SC_INDEX_WINDOW == 0
    halves = SC_INDEX_WINDOW // SC_GATHER_WINDOW

    @pl.kernel(out_type=jax.ShapeDtypeStruct((n_rows, width), data.dtype), mesh=mesh,
               name="gather_rows",
               scratch_types=[pltpu.VMEM((1, SC_INDEX_WINDOW), jnp.int32),
                              pltpu.VMEM((SC_GATHER_WINDOW, width), data.dtype)])
    def gather(x_hbm, i_hbm, o_hbm, idx_vmem, rows_vmem):
        worker = lax.axis_index("core") * mesh.num_subcores + lax.axis_index("subcore")
        base = worker * per_worker

        @pl.loop(0, per_worker // SC_INDEX_WINDOW)
        def _(j):
            off = base + j * SC_INDEX_WINDOW
            pltpu.sync_copy(i_hbm.at[:, pl.ds(off, SC_INDEX_WINDOW)], idx_vmem)
            for h in range(halves):
                part = idx_vmem.at[0, pl.ds(h * SC_GATHER_WINDOW, SC_GATHER_WINDOW)]
                pltpu.sync_copy(x_hbm.at[part], rows_vmem)
                pltpu.sync_copy(rows_vmem,
                                o_hbm.at[pl.ds(off + h * SC_GATHER_WINDOW, SC_GATHER_WINDOW)])

    return gather(data, idx.reshape(1, n_rows))


def _scatter_rows(data, idx, n_out):
    n_copies, n_rows = idx.shape
    width = data.shape[1]
    mesh = plsc.VectorSubcoreMesh(core_axis_name="core", subcore_axis_name="subcore")
    n_workers = mesh.num_cores * mesh.num_subcores
    per_worker = n_rows // n_workers
    assert per_worker * n_workers == n_rows and per_worker % SC_INDEX_WINDOW == 0
    halves = SC_INDEX_WINDOW // SC_GATHER_WINDOW

    @pl.kernel(out_type=jax.ShapeDtypeStruct((n_out, width), data.dtype), mesh=mesh,
               name="scatter_rows",
               scratch_types=[pltpu.VMEM((n_copies, SC_INDEX_WINDOW), jnp.int32),
                              pltpu.VMEM((SC_GATHER_WINDOW, width), data.dtype)])
    def scatter(x_hbm, i_hbm, o_hbm, idx_vmem, rows_vmem):
        worker = lax.axis_index("core") * mesh.num_subcores + lax.axis_index("subcore")
        base = worker * per_worker

        @pl.loop(0, per_worker // SC_INDEX_WINDOW)
        def _(j):
            off = base + j * SC_INDEX_WINDOW
            pltpu.sync_copy(i_hbm.at[:, pl.ds(off, SC_INDEX_WINDOW)], idx_vmem)
            for h in range(halves):
                pltpu.sync_copy(x_hbm.at[pl.ds(off + h * SC_GATHER_WINDOW, SC_GATHER_WINDOW)],
                                rows_vmem)
                for k in range(n_copies):
                    part = idx_vmem.at[k, pl.ds(h * SC_GATHER_WINDOW, SC_GATHER_WINDOW)]
                    pltpu.sync_copy(rows_vmem, o_hbm.at[part])

    return scatter(data, idx)


def _mod_kernel(ct_ref, w_ref, b_ref, o_ref):
    ct = ct_ref[...]
    s = ct * (1.0 / (1.0 + jnp.exp(-ct)))
    w = w_ref[...]
    rows = [jnp.sum(s[:, b:b + 1] * w, axis=0, keepdims=True) for b in range(ct.shape[1])]
    o_ref[...] = jnp.concatenate(rows, axis=0) + b_ref[...]


def _mod(c, w_mod, b_mod):
    B = c.shape[0]
    n = w_mod.shape[1]
    return pl.pallas_call(
        _mod_kernel,
        name="mod",
        grid=(n // TN_MOD,),
        in_specs=[pl.BlockSpec((D_MODEL, B), lambda j: (0, 0)),
                  pl.BlockSpec((D_MODEL, TN_MOD), lambda j: (0, j)),
                  pl.BlockSpec((1, TN_MOD), lambda j: (0, j))],
        out_specs=pl.BlockSpec((B, TN_MOD), lambda j: (0, j)),
        out_shape=jax.ShapeDtypeStruct((B, n), jnp.float32),
        compiler_params=pltpu.CompilerParams(dimension_semantics=("arbitrary",),
                                             vmem_limit_bytes=VMEM_LIMIT),
    )(c.T, w_mod, b_mod.reshape(1, n))


def _store_residue_views(scr_ref, scr4_ref, out_refs):
    ns, tm, _ = scr_ref.shape
    w = ns * LANES
    o1_ref, o4_ref, o16_ref = out_refs
    assert DILATIONS == (1, 4, 16)
    q4 = tm // 4
    for s in range(ns):
        o1_ref[0, :, s * LANES:(s + 1) * LANES] = scr_ref[s].astype(o1_ref.dtype)
    for r in range(4):
        for s in range(ns):
            vals = scr_ref[s, pl.ds(r, q4, stride=4), :]
            scr4_ref[s, r * q4:(r + 1) * q4, :] = vals
            c0 = r * w + s * LANES
            o4_ref[0, :, c0:c0 + LANES] = vals.astype(o4_ref.dtype)
    for r in range(16):
        lo, hi = r % 4, r // 4
        for s in range(ns):
            vals = scr4_ref[s, pl.ds(lo * q4 + hi, tm // 16, stride=4), :]
            c0 = r * w + s * LANES
            o16_ref[0, :, c0:c0 + LANES] = vals.astype(o16_ref.dtype)


def _in_proj_kernel(x_ref, mod_ref, g_ref, w_ref, cos_ref, sin_ref, wlr_ref, bg_ref,
                    qa1_ref, qa4_ref, qa16_ref, ka1_ref, ka4_ref, ka16_ref,
                    va1_ref, va4_ref, va16_ref, qg_ref, kg_ref, vg_ref, sr_ref, la_ref,
                    qs_ref, ks_ref, vs_ref, q4_ref, k4_ref, v4_ref):
    x = x_ref[0]
    shift = mod_ref[0, 0:1, :]
    scale = mod_ref[0, 1:2, :]
    h = (_rms(x, g_ref[...] * (1.0 + scale)) + shift).astype(jnp.bfloat16)

    def proj(c0, width):
        return jnp.dot(h, w_ref[:, c0:c0 + width], preferred_element_type=jnp.float32)

    cos = cos_ref[...]
    sin = sin_ref[...]

    def rope(c0, scr_ref, mult):
        p = proj(c0, ATT_WIDTH)
        for s in range(ATT_WIDTH // LANES):
            t = p[:, s * LANES:(s + 1) * LANES]
            r = t * cos + pltpu.roll(t, LANES // 2, axis=1) * sin
            if mult != 1.0:
                r = r * mult
            scr_ref[s] = r

    rope(_C_QA, qs_ref, ATT_HEAD_DIM ** -0.5)
    _store_residue_views(qs_ref, q4_ref, (qa1_ref, qa4_ref, qa16_ref))
    rope(_C_KA, ks_ref, 1.0)
    _store_residue_views(ks_ref, k4_ref, (ka1_ref, ka4_ref, ka16_ref))
    pv = proj(_C_VA, ATT_WIDTH)
    for s in range(ATT_WIDTH // LANES):
        vs_ref[s] = pv[:, s * LANES:(s + 1) * LANES]
    _store_residue_views(vs_ref, v4_ref, (va1_ref, va4_ref, va16_ref))
    qg_ref[0] = proj(_C_QG, GLA_KEY_WIDTH).astype(qg_ref.dtype)
    kg_ref[0] = proj(_C_KG, GLA_KEY_WIDTH).astype(kg_ref.dtype)
    vg_ref[0] = proj(_C_VG, GLA_VAL_WIDTH).astype(vg_ref.dtype)
    r = proj(_C_RG, GLA_VAL_WIDTH)
    sr_ref[0] = (r * (1.0 / (1.0 + jnp.exp(-r)))).astype(sr_ref.dtype)
    lr = proj(_C_LR, LANES).astype(jnp.bfloat16)
    z = jnp.dot(lr, wlr_ref[...], preferred_element_type=jnp.float32) + bg_ref[...]
    log_sig = jnp.minimum(z, 0.0) - jnp.log(1.0 + jnp.exp(-jnp.abs(z)))
    la_ref[0] = log_sig * (1.0 / GLA_GATE_NORMALIZER)


def _in_proj(x, mod3, g_pre, w_packed, cos_t, sin_t, wlr, bg):
    B, S, D = x.shape
    nb = S // TM_PROJ
    tok = lambda w: pl.BlockSpec((1, TM_PROJ, w), lambda b, i: (b, i, 0))
    const = lambda shape: pl.BlockSpec(shape, lambda b, i: tuple(0 for _ in shape))
    bf = jnp.bfloat16
    att_shapes = [jax.ShapeDtypeStruct((B, S // d, d * ATT_WIDTH), bf) for d in DILATIONS] * 3
    att_specs = [pl.BlockSpec((1, TM_PROJ // d, d * ATT_WIDTH), lambda b, i: (b, i, 0))
                 for d in DILATIONS] * 3
    gla_shapes = [jax.ShapeDtypeStruct((B, S, w), dt) for w, dt in
                  [(GLA_KEY_WIDTH, bf), (GLA_KEY_WIDTH, bf),
                   (GLA_VAL_WIDTH, bf), (GLA_VAL_WIDTH, bf), (GLA_KEY_WIDTH, jnp.float32)]]
    return pl.pallas_call(
        _in_proj_kernel,
        name="in_proj",
        grid=(B, nb),
        in_specs=[tok(D),
                  pl.BlockSpec((1, N_MOD, D), lambda b, i: (b, 0, 0)),
                  const((1, D)),
                  const((D, D_IN_PACKED)),
                  pl.BlockSpec((TM_PROJ, LANES), lambda b, i: (i, 0)),
                  pl.BlockSpec((TM_PROJ, LANES), lambda b, i: (i, 0)),
                  const((LANES, GLA_KEY_WIDTH)),
                  const((1, GLA_KEY_WIDTH))],
        out_specs=att_specs + [tok(s.shape[-1]) for s in gla_shapes],
        out_shape=att_shapes + gla_shapes,
        scratch_shapes=[pltpu.VMEM((ATT_WIDTH // LANES, TM_PROJ, LANES), jnp.float32)] * 6,
        compiler_params=pltpu.CompilerParams(dimension_semantics=("arbitrary", "arbitrary"),
                                             vmem_limit_bytes=VMEM_LIMIT),
    )(x, mod3, g_pre, w_packed, cos_t, sin_t, wlr, bg)


def _attn_kernel(q_ref, kp_ref, kc_ref, vp_ref, vc_ref, o_ref, st_ref):
    i = pl.program_id(2)
    tp = kp_ref.shape[1]
    tq = tp
    n_sub = q_ref.shape[1] // tq
    row = lax.broadcasted_iota(jnp.int32, (tq, tp + tq), 0)
    col = lax.broadcasted_iota(jnp.int32, (tq, tp + tq), 1)
    band = (col >= row) & (col <= row + ATT_SPAN)
    bias_inner = jnp.where(band, 0.0, NEG_BIG)
    bias_first = jnp.where(band & ((col >= tp) | (i > 0)), 0.0, NEG_BIG)
    lane = lax.broadcasted_iota(jnp.int32, (tq, LANES), 1)
    q_head = (lane >> 5) & 1
    for sb in range(n_sub):
        rows = slice(sb * tq, (sb + 1) * tq)
        bias = bias_first if sb == 0 else bias_inner
        stats = jnp.zeros((tq, LANES), jnp.float32)
        for hp in range(ATT_HEADS // 2):
            sl = slice(hp * LANES, (hp + 1) * LANES)
            q2 = q_ref[0, rows, sl]
            if sb == 0:
                kcat = jnp.concatenate([kp_ref[0, :, sl], kc_ref[0, rows, sl]], axis=0)
                vcat = jnp.concatenate([vp_ref[0, :, sl], vc_ref[0, rows, sl]], axis=0)
            else:
                kcat = kc_ref[0, (sb - 1) * tq:(sb + 1) * tq, sl]
                vcat = vc_ref[0, (sb - 1) * tq:(sb + 1) * tq, sl]
            outs = []
            for hh in range(2):
                qm = jnp.where(q_head == hh, q2, jnp.zeros_like(q2))
                s = lax.dot_general(qm, kcat, (((1,), (1,)), ((), ())),
                                    preferred_element_type=jnp.float32) + bias
                m = jnp.max(s, axis=1, keepdims=True)
                p = jnp.exp(s - m)
                l = jnp.sum(p, axis=1, keepdims=True)
                o = jnp.dot(p.astype(vcat.dtype), vcat, preferred_element_type=jnp.float32)
                outs.append(o / l)
                stats = jnp.where(lane == 2 * hp + hh, m + jnp.log(l), stats)
            o_ref[0, rows, sl] = jnp.where(lane < ATT_HEAD_DIM, outs[0], outs[1]).astype(o_ref.dtype)
        st_ref[0, rows, :] = stats


def _attn_pattern(qv, kv, vv, dil):
    B, L, _ = qv.shape
    W = ATT_WIDTH
    tq = min(TQ_ATT, L)
    nq = L // tq
    cur = pl.BlockSpec((1, tq, W), lambda b, r, i: (b, i, r))
    back = tq // ATT_SPAN
    prev = pl.BlockSpec((1, ATT_SPAN, W), lambda b, r, i: (b, jnp.maximum(i * back - 1, 0), r))
    o, st = pl.pallas_call(
        _attn_kernel,
        name=f"attn_d{dil}",
        grid=(B, dil, nq),
        in_specs=[cur, prev, cur, prev, cur],
        out_specs=[cur, pl.BlockSpec((1, tq, LANES), lambda b, r, i: (b, i, r))],
        out_shape=[jax.ShapeDtypeStruct((B, L, dil * W), jnp.bfloat16),
                   jax.ShapeDtypeStruct((B, L, dil * LANES), jnp.float32)],
        compiler_params=pltpu.CompilerParams(
            dimension_semantics=("arbitrary", "arbitrary", "arbitrary"),
            vmem_limit_bytes=VMEM_LIMIT),
    )(qv, kv, kv, vv, vv)
    return o, st


def _cumsum_rows(x):
    n = x.shape[0]
    row = lax.broadcasted_iota(jnp.int32, x.shape, 0)
    s = 1
    while s < n:
        x = x + jnp.where(row >= s, pltpu.roll(x, s, axis=0), 0.0)
        s *= 2
    return x


def _gla_kernel(q_ref, k_ref, v_ref, la_ref, sr_ref, g_ref, o_ref, st_ref):
    @pl.when(pl.program_id(0) == 0)
    def _():
        st_ref[...] = jnp.zeros_like(st_ref)

    C = GLA_CHUNK
    n_batch = q_ref.shape[0]
    n_chunks = q_ref.shape[1] // C
    lane = lax.broadcasted_iota(jnp.int32, (C, LANES), 1)
    lane_sq = lax.broadcasted_iota(jnp.int32, (LANES, LANES), 1)
    tril = (lax.broadcasted_iota(jnp.int32, (C, C), 0)
            >= lax.broadcasted_iota(jnp.int32, (C, C), 1))
    g = g_ref[...]
    bf = jnp.bfloat16

    def chunk(c, carry):
        r0 = pl.multiple_of(c * C, C)
        rows = pl.ds(r0, C)
        for n in range(n_batch):
            for p in range(GLA_HEADS // 2):
                ksl = slice(p * LANES, (p + 1) * LANES)
                b = _cumsum_rows(la_ref[n, rows, ksl])
                b_last = b[C - 1:C, :]
                q = q_ref[n, rows, ksl].astype(jnp.float32) * (GLA_KEY_DIM ** -0.5)
                k = k_ref[n, rows, ksl].astype(jnp.float32)
                q_dec = q * jnp.exp(b)
                k_inv = (k * jnp.exp(-b)).astype(bf)
                k_dec = (k * jnp.exp(b_last - b)).astype(bf)
                st = st_ref[n, p]
                st_b = st.astype(bf)
                ut = []
                for hh in range(2):
                    h = 2 * p + hh
                    vsl = slice(h * GLA_VAL_DIM, (h + 1) * GLA_VAL_DIM)
                    v = v_ref[n, rows, vsl]
                    own = (lane >= GLA_KEY_DIM) if hh else (lane < GLA_KEY_DIM)
                    qm = jnp.where(own, q_dec, 0.0).astype(bf)
                    att = lax.dot_general(qm, k_inv, (((1,), (1,)), ((), ())),
                                          preferred_element_type=jnp.float32)
                    att = jnp.where(tril, att, 0.0).astype(bf)
                    o = jnp.dot(att, v, preferred_element_type=jnp.float32)
                    o = o + lax.dot_general(qm, st_b, (((1,), (1,)), ((), ())),
                                            preferred_element_type=jnp.float32)
                    o = _rms(o, g) * sr_ref[n, rows, vsl].astype(jnp.float32)
                    o_ref[n, rows, vsl] = o.astype(o_ref.dtype)
                    ut.append(lax.dot_general(v, k_dec, (((0,), (0,)), ((), ())),
                                              preferred_element_type=jnp.float32))
                st_ref[n, p] = (st * jnp.exp(b_last)
                                + jnp.where(lane_sq < GLA_KEY_DIM, ut[0], ut[1]))
        return carry

    lax.fori_loop(0, n_chunks, chunk, 0, unroll=GLA_UNROLL)


def _gla(qg, kg, vg, la, sr, g_gla):
    B, S, _ = qg.shape
    tok = lambda w: pl.BlockSpec((B, TG_GLA, w), lambda i: (0, i, 0))
    return pl.pallas_call(
        _gla_kernel,
        name="gla",
        grid=(S // TG_GLA,),
        in_specs=[tok(GLA_KEY_WIDTH), tok(GLA_KEY_WIDTH), tok(GLA_VAL_WIDTH),
                  tok(GLA_KEY_WIDTH), tok(GLA_VAL_WIDTH),
                  pl.BlockSpec((1, GLA_VAL_DIM), lambda i: (0, 0))],
        out_specs=tok(GLA_VAL_WIDTH),
        out_shape=jax.ShapeDtypeStruct((B, S, GLA_VAL_WIDTH), jnp.bfloat16),
        scratch_shapes=[pltpu.VMEM((B, GLA_HEADS // 2, GLA_VAL_DIM, LANES), jnp.float32)],
        compiler_params=pltpu.CompilerParams(dimension_semantics=("arbitrary",),
                                             vmem_limit_bytes=VMEM_LIMIT),
    )(qg, kg, vg, la, sr, g_gla)


def _out_proj_kernel(o1_ref, o2_ref, o3_ref, s1_ref, s2_ref, s3_ref, og_ref, x_ref, mod_ref,
                     wout_ref, gpost_ref, gpre_ref, rw_ref, rb_ref,
                     x1_ref, h2_ref, ti_ref, tg_ref, cnt_ref, oscr_ref, sscr_ref, otmp_ref,
                     stmp_ref):
    tm = x_ref.shape[1]

    @pl.when((pl.program_id(0) == 0) & (pl.program_id(1) == 0))
    def _():
        cnt_ref[...] = jnp.zeros_like(cnt_ref)

    ns = ATT_WIDTH // LANES
    q4 = tm // 4
    for r in range(4):
        rows = pl.ds(r, q4, stride=4)
        for s in range(ns):
            c0 = r * ATT_WIDTH + s * LANES
            oscr_ref[0, s, rows, :] = o2_ref[0, :, c0:c0 + LANES].astype(jnp.float32)
        sscr_ref[0, rows, :] = s2_ref[0, :, r * LANES:(r + 1) * LANES]
    for r in range(16):
        lo, hi = r % 4, r // 4
        rows = pl.ds(lo * q4 + hi, tm // 16, stride=4)
        for s in range(ns):
            c0 = r * ATT_WIDTH + s * LANES
            otmp_ref[s, rows, :] = o3_ref[0, :, c0:c0 + LANES].astype(jnp.float32)
        stmp_ref[rows, :] = s3_ref[0, :, r * LANES:(r + 1) * LANES]
    for r in range(4):
        rows = pl.ds(r, q4, stride=4)
        for s in range(ns):
            oscr_ref[1, s, rows, :] = otmp_ref[s, r * q4:(r + 1) * q4, :]
        sscr_ref[1, rows, :] = stmp_ref[r * q4:(r + 1) * q4, :]
    gate1 = mod_ref[0, 2:3, :]
    shift2 = mod_ref[0, 3:4, :]
    scale2 = mod_ref[0, 4:5, :]
    bf = jnp.bfloat16
    rw = rw_ref[...]
    w_hi = rw.astype(bf)
    w_lo = (rw - w_hi.astype(jnp.float32)).astype(bf)
    ts = tm // OUT_SUBTILES
    lane = lax.broadcasted_iota(jnp.int32, (ts, LANES), 1)
    lane_f = lane.astype(jnp.float32)
    for sub in range(OUT_SUBTILES):
        rs = slice(sub * ts, (sub + 1) * ts)
        lses = [s1_ref[0, rs, :], sscr_ref[0, rs, :], sscr_ref[1, rs, :]]
        m = jnp.maximum(jnp.maximum(lses[0], lses[1]), lses[2])
        es = [jnp.exp(t - m) for t in lses]
        inv = 1.0 / (es[0] + es[1] + es[2])
        ws = [e * inv for e in es]
        pairs = []
        for hp in range(ATT_HEADS // 2):
            sl = slice(hp * LANES, (hp + 1) * LANES)
            o_pats = [o1_ref[0, rs, sl].astype(jnp.float32), oscr_ref[0, hp, rs, :],
                      oscr_ref[1, hp, rs, :]]
            acc = jnp.zeros((ts, LANES), jnp.float32)
            head_of_lane = 2 * hp + (lane >> 6)
            for w, o in zip(ws, o_pats):
                acc = acc + jnp.take_along_axis(w, head_of_lane, axis=1) * o
            pairs.append(acc.astype(bf))
        mixed = jnp.concatenate(pairs + [og_ref[0, rs, :]], axis=1)
        y = jnp.dot(mixed, wout_ref[...], preferred_element_type=jnp.float32)
        x1 = x_ref[0, rs, :] + _rms(y, gate1 * gpost_ref[...])
        x1_ref[0, rs, :] = x1
        h2 = _rms(x1, gpre_ref[...] * (1.0 + scale2)) + shift2
        h2_ref[0, rs, :] = _pack_bf16_pairs(h2)
        h_hi = h2.astype(bf)
        h_lo = (h2 - h_hi.astype(jnp.float32)).astype(bf)
        logits = (jnp.dot(h_hi, w_hi, preferred_element_type=jnp.float32)
                  + jnp.dot(h_lo, w_hi, preferred_element_type=jnp.float32)
                  + jnp.dot(h_hi, w_lo, preferred_element_type=jnp.float32)) + rb_ref[...]
        vals, idxs = [], []
        for _ in range(TOP_K):
            mk = jnp.max(logits, axis=1, keepdims=True)
            ik = jnp.min(jnp.where(logits == mk, lane_f, float(LANES)), axis=1, keepdims=True)
            logits = jnp.where(lane_f == ik, -jnp.inf, logits)
            vals.append(mk)
            idxs.append(ik)
        ex = [jnp.exp(v - vals[0]) for v in vals]
        den = ex[0] + ex[1] + ex[2] + ex[3]
        ti = jnp.zeros((ts, LANES), jnp.float32)
        tg = jnp.zeros((ts, LANES), jnp.float32)
        for kk in range(TOP_K):
            ti = jnp.where(lane == kk, idxs[kk], ti)
            tg = jnp.where(lane == kk, ex[kk] / den, tg)
        ti_ref[0, rs, :] = ti.astype(jnp.int32)
        tg_ref[0, rs, :] = tg
        chosen = jnp.zeros((ts, LANES), jnp.float32)
        for ik in idxs:
            chosen = chosen + jnp.where(lane_f == ik, 1.0, 0.0)
        cnt_ref[...] += jnp.sum(chosen, axis=0, keepdims=True)


def _out_proj(o_pats, st_pats, og, x, mod3, w_out, g_post, g_pre, rw, rb):
    B, S, D = x.shape
    tok = lambda w: pl.BlockSpec((1, TM_OUT, w), lambda b, i: (b, i, 0))
    const = lambda shape: pl.BlockSpec(shape, lambda b, i: tuple(0 for _ in shape))
    out_shapes = [jax.ShapeDtypeStruct((B, S, D), jnp.float32),
                  jax.ShapeDtypeStruct((B, S, D // 2), jnp.uint32),
                  jax.ShapeDtypeStruct((B, S, LANES), jnp.int32),
                  jax.ShapeDtypeStruct((B, S, LANES), jnp.float32),
                  jax.ShapeDtypeStruct((8, LANES), jnp.float32)]
    return pl.pallas_call(
        _out_proj_kernel,
        name="out_proj",
        grid=(B, S // TM_OUT),
        in_specs=[pl.BlockSpec((1, TM_OUT // d, d * ATT_WIDTH), lambda b, i: (b, i, 0))
                  for d in DILATIONS]
                 + [pl.BlockSpec((1, TM_OUT // d, d * LANES), lambda b, i: (b, i, 0))
                    for d in DILATIONS]
                 + [tok(GLA_VAL_WIDTH), tok(D),
                  pl.BlockSpec((1, N_MOD, D), lambda b, i: (b, 0, 0)),
                  const((D, D)), const((1, D)), const((1, D)),
                  const((D, LANES)), const((1, LANES))],
        out_specs=[tok(D), tok(D // 2), tok(LANES), tok(LANES), const((8, LANES))],
        out_shape=out_shapes,
        scratch_shapes=[pltpu.VMEM((2, ATT_WIDTH // LANES, TM_OUT, LANES), jnp.float32),
                        pltpu.VMEM((2, TM_OUT, LANES), jnp.float32),
                        pltpu.VMEM((ATT_WIDTH // LANES, TM_OUT, LANES), jnp.float32),
                        pltpu.VMEM((TM_OUT, LANES), jnp.float32)],
        compiler_params=pltpu.CompilerParams(dimension_semantics=("arbitrary", "arbitrary"),
                                             vmem_limit_bytes=VMEM_LIMIT),
    )(*o_pats, *st_pats, og, x, mod3, w_out, g_post, g_pre, rw, rb)


def _split_gate_up_kernel(w_ref, wg_ref, wl_ref):
    group = 2 * LANES
    src = lax.broadcasted_iota(jnp.int32, (group, group), 0)
    dst = lax.broadcasted_iota(jnp.int32, (group, group), 1)
    want = jnp.where(dst < LANES, 2 * dst, 2 * (dst - LANES) + 1)
    perm = jnp.where(src == want, 1.0, 0.0).astype(jnp.bfloat16)
    for j in range(w_ref.shape[1] // group):
        t = jnp.dot(w_ref[:, j * group:(j + 1) * group].astype(jnp.bfloat16), perm,
                    preferred_element_type=jnp.float32)
        wg_ref[:, j * LANES:(j + 1) * LANES] = t[:, :LANES].astype(wg_ref.dtype)
        wl_ref[:, j * LANES:(j + 1) * LANES] = t[:, LANES:].astype(wl_ref.dtype)


def _split_gate_up(w_gate_up):
    E, K, N2 = w_gate_up.shape
    out = jax.ShapeDtypeStruct((E, K, N2 // 2), jnp.bfloat16)
    ospec = pl.BlockSpec((None, TK_SPLIT, N2 // 2), lambda e, i: (e, i, 0))
    return pl.pallas_call(
        _split_gate_up_kernel,
        name="split_gate_up",
        grid=(E, K // TK_SPLIT),
        in_specs=[pl.BlockSpec((None, TK_SPLIT, N2), lambda e, i: (e, i, 0))],
        out_specs=[ospec, ospec],
        out_shape=[out, out],
        compiler_params=pltpu.CompilerParams(dimension_semantics=("arbitrary", "arbitrary"),
                                             vmem_limit_bytes=VMEM_LIMIT),
    )(w_gate_up)


def _moe_kernel(be_ref, nu_ref, x_ref, wg_ref, wl_ref, bg_ref, bl_ref, wd_ref, bd_ref, o_ref,
                wd_bf_ref):
    i = pl.program_id(0)
    new_expert = (i == 0) | (be_ref[i] != be_ref[jnp.maximum(i - 1, 0)])

    @pl.when(new_expert & (i < nu_ref[0]))
    def _():
        wd_bf_ref[...] = wd_ref[...].astype(wd_bf_ref.dtype)

    @pl.when(i < nu_ref[0])
    def _():
        lo, hi = _unpack_bf16_pairs(x_ref[...])
        x = jnp.concatenate([lo, hi], axis=1).astype(jnp.bfloat16)
        xg = jnp.dot(x, wg_ref[...], preferred_element_type=jnp.float32) + bg_ref[...]
        xl = jnp.dot(x, wl_ref[...], preferred_element_type=jnp.float32) + bl_ref[...]
        xg = jnp.minimum(xg, SWIGLU_LIMIT)
        xl = jnp.clip(xl, -SWIGLU_LIMIT, SWIGLU_LIMIT)
        act = xg * (1.0 / (1.0 + jnp.exp(-SWIGLU_ALPHA * xg))) * (xl + 1.0)
        out = jnp.dot(act.astype(jnp.bfloat16), wd_bf_ref[...],
                      preferred_element_type=jnp.float32) + bd_ref[...]
        o_ref[...] = _pack_bf16_pairs(out)

    @pl.when(i >= nu_ref[0])
    def _():
        o_ref[...] = jnp.zeros_like(o_ref)


def _moe(xs, blk_e, n_used, wg, wl, bg, bl, wd, bd):
    P, half = xs.shape
    D = 2 * half
    n_blocks = P // TM_MOE
    wspec = lambda k, n: pl.BlockSpec((None, k, n), lambda i, be, nu: (be[i], 0, 0))
    grid_spec = pltpu.PrefetchScalarGridSpec(
        num_scalar_prefetch=2,
        grid=(n_blocks,),
        in_specs=[pl.BlockSpec((TM_MOE, half), lambda i, be, nu: (i, 0)),
                  wspec(D, D_FF), wspec(D, D_FF), wspec(1, D_FF), wspec(1, D_FF),
                  wspec(D_FF, D), wspec(1, D)],
        out_specs=pl.BlockSpec((TM_MOE, half), lambda i, be, nu: (i, 0)),
        scratch_shapes=[pltpu.VMEM((D_FF, D), jnp.bfloat16)],
    )
    return pl.pallas_call(
        _moe_kernel,
        name="moe",
        grid_spec=grid_spec,
        out_shape=jax.ShapeDtypeStruct((P, half), jnp.uint32),
        compiler_params=pltpu.CompilerParams(dimension_semantics=("arbitrary",),
                                             vmem_limit_bytes=VMEM_LIMIT),
    )(blk_e, n_used, xs, wg, wl, bg, bl, wd, bd)


def _final_kernel(x1_ref, y0_ref, y1_ref, y2_ref, y3_ref, tg_ref, mod_ref, g_ref, o_ref):
    gates = tg_ref[...]
    half = y0_ref.shape[1]
    y_lo = jnp.zeros((x1_ref.shape[0], half), jnp.float32)
    y_hi = jnp.zeros((x1_ref.shape[0], half), jnp.float32)
    for kk, yk_ref in enumerate((y0_ref, y1_ref, y2_ref, y3_ref)):
        lo, hi = _unpack_bf16_pairs(yk_ref[...])
        y_lo = y_lo + gates[:, kk:kk + 1] * lo
        y_hi = y_hi + gates[:, kk:kk + 1] * hi
    y = jnp.concatenate([y_lo, y_hi], axis=1)
    gate2 = mod_ref[0, 5:6, :]
    o_ref[...] = x1_ref[...] + _rms(y, gate2 * g_ref[...])


def _final_part_kernel(x1_ref, y0_ref, y1_ref, y2_ref, y3_ref, tg_ref, mod_ref, g_ref, prev_ref,
                       o_ref):
    del prev_ref
    _final_kernel(x1_ref, y0_ref, y1_ref, y2_ref, y3_ref, tg_ref, mod_ref, g_ref, o_ref)


def _final(x1, yk, tg, mod3, g_post, seq_len, part, prev):
    T, D = x1.shape
    tp = yk.shape[0] // TOP_K
    nt = tp // TM_FIN
    t0 = part * nt
    batch = (part * tp) // seq_len
    yspec = lambda kk: pl.BlockSpec((TM_FIN, D // 2), lambda i: (kk * nt + i, 0))
    in_specs = ([pl.BlockSpec((TM_FIN, D), lambda i: (t0 + i, 0))]
                + [yspec(kk) for kk in range(TOP_K)]
                + [pl.BlockSpec((TM_FIN, LANES), lambda i: (t0 + i, 0)),
                   pl.BlockSpec((1, N_MOD, D), lambda i: (batch, 0, 0)),
                   pl.BlockSpec((1, D), lambda i: (0, 0))])
    args = [x1, yk, yk, yk, yk, tg, mod3, g_post]
    body, aliases = _final_kernel, {}
    if prev is not None:
        in_specs.append(pl.BlockSpec(memory_space=pl.ANY))
        args.append(prev)
        body, aliases = _final_part_kernel, {len(args) - 1: 0}
    return pl.pallas_call(
        body,
        name="final",
        grid=(nt,),
        in_specs=in_specs,
        out_specs=pl.BlockSpec((TM_FIN, D), lambda i: (t0 + i, 0)),
        out_shape=jax.ShapeDtypeStruct((T, D), jnp.float32),
        input_output_aliases=aliases,
        compiler_params=pltpu.CompilerParams(dimension_semantics=("arbitrary",),
                                             vmem_limit_bytes=VMEM_LIMIT),
    )(*args)


def _pack_w_in(w_in):
    half = ATT_HEAD_DIM // 2

    def pair_rotary_layout(w):
        k = w.shape[0]
        w = w.reshape(k, ATT_HEADS // 2, 2, 2, half)
        return w.transpose(0, 1, 3, 2, 4).reshape(k, ATT_WIDTH)

    lr = jnp.pad(w_in[:, 3072:3072 + GLA_GATE_RANK], ((0, 0), (0, LANES - GLA_GATE_RANK)))
    return jnp.concatenate([pair_rotary_layout(w_in[:, :ATT_WIDTH]),
                            pair_rotary_layout(w_in[:, ATT_WIDTH:2 * ATT_WIDTH]),
                            w_in[:, 1024:3072], lr], axis=1).astype(jnp.bfloat16)


def _rope_tables(seq_len):
    half = ATT_HEAD_DIM // 2
    inv_freq = ROPE_THETA ** (-jnp.arange(half, dtype=jnp.float32) / half)
    ang = jnp.arange(seq_len, dtype=jnp.float32)[:, None] * inv_freq[None, :]
    cos = jnp.tile(jnp.cos(ang), (1, LANES // half))
    sin = jnp.tile(jnp.sin(ang), (1, LANES // half))
    sign = jnp.where(jnp.arange(LANES) < LANES // 2, -1.0, 1.0)
    return cos, sin * sign


def _route_kernel(ti_ref, cnt_ref, pos_ref, blk_ref, nu_ref, base_ref, tri_ref):
    j = pl.program_id(0)
    tm = ti_ref.shape[0]
    lane = lax.broadcasted_iota(jnp.int32, (tm, LANES), 1)
    ti = ti_ref[...]
    hots = [lane == ti[:, k:k + 1] for k in range(TOP_K)]
    hot_all = jnp.zeros((tm, LANES), jnp.float32)
    for h in hots:
        hot_all = hot_all + jnp.where(h, 1.0, 0.0)
    tile_cnt = jnp.sum(hot_all, axis=0, keepdims=True)

    @pl.when(j == 0)
    def _():
        r = lax.broadcasted_iota(jnp.int32, (tm, tm), 0)
        c = lax.broadcasted_iota(jnp.int32, (tm, tm), 1)
        tri_ref[...] = jnp.where(c < r, 1.0, 0.0).astype(tri_ref.dtype)
        shift = TM_MOE.bit_length() - 1
        lane8 = lax.broadcasted_iota(jnp.int32, cnt_ref.shape, 1)
        cnt = cnt_ref[...].astype(jnp.int32)
        padded = ((cnt + (TM_MOE - 1)) >> shift) << shift
        pend = padded
        s = 1
        while s < N_EXPERTS:
            pend = pend + jnp.where(lane8 >= s, pltpu.roll(pend, s, axis=1), 0)
            s *= 2
        base_ref[...] = (pend - padded).astype(jnp.float32)
        nb = blk_ref.shape[0]
        blk_start = lax.broadcasted_iota(jnp.int32, (nb, LANES), 0) * TM_MOE
        lane_b = lax.broadcasted_iota(jnp.int32, (nb, LANES), 1)
        done = jnp.where((pend[0:1, :] <= blk_start) & (lane_b < N_EXPERTS), 1.0, 0.0)
        blk = jnp.minimum(jnp.sum(done, axis=1, keepdims=True), float(N_EXPERTS - 1))
        blk_ref[...] = jnp.broadcast_to(blk, blk_ref.shape).astype(jnp.int32)
        total = jnp.max(jnp.where(lane8 < N_EXPERTS, pend, 0).astype(jnp.float32),
                        axis=1, keepdims=True)
        nu_ref[...] = jnp.broadcast_to(total.astype(jnp.int32) >> shift, nu_ref.shape)

    before = jnp.dot(tri_ref[...], hot_all.astype(tri_ref.dtype),
                     preferred_element_type=jnp.float32)
    slot = before + base_ref[0:1, :]
    pos = jnp.zeros((tm, LANES), jnp.float32)
    for k, h in enumerate(hots):
        pk = jnp.sum(jnp.where(h, slot, 0.0), axis=1, keepdims=True)
        pos = jnp.where(lane == k, pk, pos)
    pos_ref[...] = pos.astype(jnp.int32)
    base_ref[...] += tile_cnt


def _route(ti, cnt, n_tokens):
    A = n_tokens * TOP_K
    n_blocks = -(-(A + N_EXPERTS * (TM_MOE - 1)) // TM_MOE)
    nb_pad = -(-n_blocks // 8) * 8
    nt = n_tokens // TM_ROUTE
    pos, blk, nu = pl.pallas_call(
        _route_kernel,
        name="route",
        grid=(nt,),
        in_specs=[pl.BlockSpec((TM_ROUTE, LANES), lambda j: (j, 0)),
                  pl.BlockSpec((8, LANES), lambda j: (0, 0))],
        out_specs=[pl.BlockSpec((TM_ROUTE, LANES), lambda j: (j, 0)),
                   pl.BlockSpec((nb_pad, LANES), lambda j: (0, 0)),
                   pl.BlockSpec((8, LANES), lambda j: (0, 0))],
        out_shape=[jax.ShapeDtypeStruct((n_tokens, LANES), jnp.int32),
                   jax.ShapeDtypeStruct((nb_pad, LANES), jnp.int32),
                   jax.ShapeDtypeStruct((8, LANES), jnp.int32)],
        scratch_shapes=[pltpu.VMEM((8, LANES), jnp.float32),
                        pltpu.VMEM((TM_ROUTE, TM_ROUTE), jnp.bfloat16)],
        compiler_params=pltpu.CompilerParams(dimension_semantics=("arbitrary",),
                                             vmem_limit_bytes=VMEM_LIMIT),
    )(ti, cnt)
    pos_kmajor = pos[:, :TOP_K].T
    return pos_kmajor, n_blocks * TM_MOE, blk[:n_blocks, 0], nu[0, :1]


def kernel(x, c, w_mod, b_mod, g_pre_mix, w_in, w_gate_lr, b_gate, g_gla, w_out, g_post_mix,
           g_pre_ffn, router_w, router_b, w_gate_up, b_gate_up, w_down, b_down, g_post_ffn):
    B, S, D = x.shape
    T = B * S
    bf = jnp.bfloat16
    cos_t, sin_t = _rope_tables(S)
    for l in range(w_mod.shape[0]):
        wg, wl = _split_gate_up(w_gate_up[l])
        wd = w_down[l]
        mod3 = _mod(c, w_mod[l], b_mod[l]).reshape(B, N_MOD, D)
        wlr = jnp.pad(w_gate_lr[l], ((0, LANES - GLA_GATE_RANK), (0, 0))).astype(bf)
        proj = _in_proj(x, mod3, g_pre_mix[l:l + 1], _pack_w_in(w_in[l]), cos_t, sin_t, wlr,
                        b_gate[l:l + 1])
        n_pat = len(DILATIONS)
        qa, ka, va = proj[:n_pat], proj[n_pat:2 * n_pat], proj[2 * n_pat:3 * n_pat]
        qg, kg, vg, sr, la = proj[3 * n_pat:]
        pats = [_attn_pattern(qa[j], ka[j], va[j], d) for j, d in enumerate(DILATIONS)]
        og = _gla(qg, kg, vg, la, sr, g_gla[l:l + 1])
        rw = jnp.pad(router_w[l], ((0, 0), (0, LANES - N_EXPERTS)))
        rb = jnp.pad(router_b[l], (0, LANES - N_EXPERTS), constant_values=NEG_BIG)[None]
        x1, h2, ti, tg, cnt = _out_proj([p[0] for p in pats], [p[1] for p in pats], og, x, mod3,
                                        w_out[l].astype(bf), g_post_mix[l:l + 1],
                                        g_pre_ffn[l:l + 1], rw, rb)
        pos_kmajor, n_slots, blk_e, n_used = _route(ti.reshape(T, LANES), cnt, T)
        xs = _scatter_rows(h2.reshape(T, D // 2), pos_kmajor, n_slots)
        out_buf = _moe(xs, blk_e, n_used, wg, wl,
                       b_gate_up[l][:, None, 0::2], b_gate_up[l][:, None, 1::2],
                       wd, b_down[l][:, None, :])
        out = None
        tp = T // FINAL_PARTS
        for part in range(FINAL_PARTS):
            yk = _gather_rows(out_buf, pos_kmajor[:, part * tp:(part + 1) * tp].reshape(-1))
            out = _final(x1.reshape(T, D), yk, tg.reshape(T, LANES), mod3, g_post_ffn[l:l + 1],
                         S, part, out)
        x = out.reshape(B, S, D)
    return x
```

```python
import functools

import numpy as np
import jax
import jax.numpy as jnp
from jax import lax
from jax.experimental import pallas as pl
from jax.experimental.pallas import tpu as pltpu
from jax.experimental.pallas import tpu_sc as plsc

D_MODEL = 1024
ATT_HEADS = 8
ATT_HEAD_DIM = 64
ATT_WIDTH = ATT_HEADS * ATT_HEAD_DIM
DILATIONS = (1, 4, 16)
ATT_SPAN = 128
ROPE_THETA = 10000.0
GLA_HEADS = 4
GLA_KEY_DIM = 64
GLA_VAL_DIM = 128
GLA_KEY_WIDTH = GLA_HEADS * GLA_KEY_DIM
GLA_VAL_WIDTH = GLA_HEADS * GLA_VAL_DIM
GLA_GATE_RANK = 16
GLA_GATE_NORMALIZER = 16.0
GLA_CHUNK = 64
N_EXPERTS = 32
TOP_K = 4
D_FF = D_MODEL
SWIGLU_LIMIT = 7.0
SWIGLU_ALPHA = 1.702
NORM_EPS = 1e-6
N_MOD = 6

LANES = 128
NEG_BIG = -1e30

TM_PROJ = 512
TQ_ATT = 1024
TG_GLA = 512
GLA_UNROLL = 8
TM_OUT = 512
OUT_SUBTILES = 1
TM_MOE = 512
TM_FIN = 1024
TN_MOD = 1024
TK_SPLIT = 512
TM_ROUTE = 1024
FINAL_PARTS = 4
SC_INDEX_WINDOW = 128
SC_GATHER_WINDOW = 64
VMEM_LIMIT = 48 * 1024 * 1024

_C_QA, _C_KA, _C_VA = 0, 512, 1024
_C_QG, _C_KG, _C_VG, _C_RG, _C_LR = 1536, 1792, 2048, 2560, 3072
D_IN_PACKED = 3200


def _rms(x, g):
    return x * lax.rsqrt(jnp.mean(x * x, axis=-1, keepdims=True) + NORM_EPS) * g


def _pack_bf16_pairs(x):
    n = x.shape[1] // 2
    u = lax.bitcast_convert_type(x.astype(jnp.bfloat16).astype(jnp.float32), jnp.uint32)
    return (u[:, :n] >> 16) | (u[:, n:] & jnp.uint32(0xFFFF0000))


def _unpack_bf16_pairs(w):
    lo = lax.bitcast_convert_type(w << 16, jnp.float32)
    hi = lax.bitcast_convert_type(w & jnp.uint32(0xFFFF0000), jnp.float32)
    return lo, hi


def _gather_rows(data, idx):
    n_rows = idx.shape[0]
    width = data.shape[1]
    mesh = plsc.VectorSubcoreMesh(core_axis_name="core", subcore_axis_name="subcore")
    n_workers = mesh.num_cores * mesh.num_subcores
    per_worker = n_rows // n_workers
    assert per_worker * n_workers == n_rows and per_worker % SC_INDEX_WINDOW == 0
    halves = SC_INDEX_WINDOW // SC_GATHER_WINDOW

    @pl.kernel(out_type=jax.ShapeDtypeStruct((n_rows, width), data.dtype), mesh=mesh,
               name="gather_rows",
               scratch_types=[pltpu.VMEM((1, SC_INDEX_WINDOW), jnp.int32),
                              pltpu.VMEM((SC_GATHER_WINDOW, width), data.dtype)])
    def gather(x_hbm, i_hbm, o_hbm, idx_vmem, rows_vmem):
        worker = lax.axis_index("core") * mesh.num_subcores + lax.axis_index("subcore")
        base = worker * per_worker

        @pl.loop(0, per_worker // SC_INDEX_WINDOW)
        def _(j):
            off = base + j * SC_INDEX_WINDOW
            pltpu.sync_copy(i_hbm.at[:, pl.ds(off, SC_INDEX_WINDOW)], idx_vmem)
            for h in range(halves):
                part = idx_vmem.at[0, pl.ds(h * SC_GATHER_WINDOW, SC_GATHER_WINDOW)]
                pltpu.sync_copy(x_hbm.at[part], rows_vmem)
                pltpu.sync_copy(rows_vmem,
                                o_hbm.at[pl.ds(off + h * SC_GATHER_WINDOW, SC_GATHER_WINDOW)])

    return gather(data, idx.reshape(1, n_rows))


def _scatter_rows(data, idx, n_out):
    n_copies, n_rows = idx.shape
    width = data.shape[1]
    mesh = plsc.VectorSubcoreMesh(core_axis_name="core", subcore_axis_name="subcore")
    n_workers = mesh.num_cores * mesh.num_subcores
    per_worker = n_rows // n_workers
    assert per_worker * n_workers == n_rows and per_worker % SC_INDEX_WINDOW == 0
    halves = SC_INDEX_WINDOW // SC_GATHER_WINDOW

    @pl.kernel(out_type=jax.ShapeDtypeStruct((n_out, width), data.dtype), mesh=mesh,
               name="scatter_rows",
               scratch_types=[pltpu.VMEM((n_copies, SC_INDEX_WINDOW), jnp.int32),
                              pltpu.VMEM((SC_GATHER_WINDOW, width), data.dtype)])
    def scatter(x_hbm, i_hbm, o_hbm, idx_vmem, rows_vmem):
        worker = lax.axis_index("core") * mesh.num_subcores + lax.axis_index("subcore")
        base = worker * per_worker

        @pl.loop(0, per_worker // SC_INDEX_WINDOW)
        def _(j):
            off = base + j * SC_INDEX_WINDOW
            pltpu.sync_copy(i_hbm.at[:, pl.ds(off, SC_INDEX_WINDOW)], idx_vmem)
            for h in range(halves):
                pltpu.sync_copy(x_hbm.at[pl.ds(off + h * SC_GATHER_WINDOW, SC_GATHER_WINDOW)],
                                rows_vmem)
                for k in range(n_copies):
                    part = idx_vmem.at[k, pl.ds(h * SC_GATHER_WINDOW, SC_GATHER_WINDOW)]
                    pltpu.sync_copy(rows_vmem, o_hbm.at[part])

    return scatter(data, idx)


def _mod_kernel(ct_ref, w_ref, b_ref, o_ref):
    ct = ct_ref[...]
    s = ct * (1.0 / (1.0 + jnp.exp(-ct)))
    w = w_ref[...]
    rows = [jnp.sum(s[:, b:b + 1] * w, axis=0, keepdims=True) for b in range(ct.shape[1])]
    o_ref[...] = jnp.concatenate(rows, axis=0) + b_ref[...]


def _mod(c, w_mod, b_mod):
    B = c.shape[0]
    n = w_mod.shape[1]
    return pl.pallas_call(
        _mod_kernel,
        name="mod",
        grid=(n // TN_MOD,),
        in_specs=[pl.BlockSpec((D_MODEL, B), lambda j: (0, 0)),
                  pl.BlockSpec((D_MODEL, TN_MOD), lambda j: (0, j)),
                  pl.BlockSpec((1, TN_MOD), lambda j: (0, j))],
        out_specs=pl.BlockSpec((B, TN_MOD), lambda j: (0, j)),
        out_shape=jax.ShapeDtypeStruct((B, n), jnp.float32),
        compiler_params=pltpu.CompilerParams(dimension_semantics=("arbitrary",),
                                             vmem_limit_bytes=VMEM_LIMIT),
    )(c.T, w_mod, b_mod.reshape(1, n))


def _store_residue_views(scr_ref, scr4_ref, out_refs):
    ns, tm, _ = scr_ref.shape
    w = ns * LANES
    o1_ref, o4_ref, o16_ref = out_refs
    assert DILATIONS == (1, 4, 16)
    q4 = tm // 4
    for s in range(ns):
        o1_ref[0, :, s * LANES:(s + 1) * LANES] = scr_ref[s].astype(o1_ref.dtype)
    for r in range(4):
        for s in range(ns):
            vals = scr_ref[s, pl.ds(r, q4, stride=4), :]
            scr4_ref[s, r * q4:(r + 1) * q4, :] = vals
            c0 = r * w + s * LANES
            o4_ref[0, :, c0:c0 + LANES] = vals.astype(o4_ref.dtype)
    for r in range(16):
        lo, hi = r % 4, r // 4
        for s in range(ns):
            vals = scr4_ref[s, pl.ds(lo * q4 + hi, tm // 16, stride=4), :]
            c0 = r * w + s * LANES
            o16_ref[0, :, c0:c0 + LANES] = vals.astype(o16_ref.dtype)


def _in_proj_kernel(x_ref, mod_ref, g_ref, w_ref, cos_ref, sin_ref, wlr_ref, bg_ref,
                    qa1_ref, qa4_ref, qa16_ref, ka1_ref, ka4_ref, ka16_ref,
                    va1_ref, va4_ref, va16_ref, qg_ref, kg_ref, vg_ref, sr_ref, la_ref,
                    qs_ref, ks_ref, vs_ref, q4_ref, k4_ref, v4_ref):
    x = x_ref[0]
    shift = mod_ref[0, 0:1, :]
    scale = mod_ref[0, 1:2, :]
    h = (_rms(x, g_ref[...] * (1.0 + scale)) + shift).astype(jnp.bfloat16)

    def proj(c0, width):
        return jnp.dot(h, w_ref[:, c0:c0 + width], preferred_element_type=jnp.float32)

    cos = cos_ref[...]
    sin = sin_ref[...]

    def rope(c0, scr_ref, mult):
        p = proj(c0, ATT_WIDTH)
        for s in range(ATT_WIDTH // LANES):
            t = p[:, s * LANES:(s + 1) * LANES]
            r = t * cos + pltpu.roll(t, LANES // 2, axis=1) * sin
            if mult != 1.0:
                r = r * mult
            scr_ref[s] = r

    rope(_C_QA, qs_ref, ATT_HEAD_DIM ** -0.5)
    _store_residue_views(qs_ref, q4_ref, (qa1_ref, qa4_ref, qa16_ref))
    rope(_C_KA, ks_ref, 1.0)
    _store_residue_views(ks_ref, k4_ref, (ka1_ref, ka4_ref, ka16_ref))
    pv = proj(_C_VA, ATT_WIDTH)
    for s in range(ATT_WIDTH // LANES):
        vs_ref[s] = pv[:, s * LANES:(s + 1) * LANES]
    _store_residue_views(vs_ref, v4_ref, (va1_ref, va4_ref, va16_ref))
    qg_ref[0] = proj(_C_QG, GLA_KEY_WIDTH).astype(qg_ref.dtype)
    kg_ref[0] = proj(_C_KG, GLA_KEY_WIDTH).astype(kg_ref.dtype)
    vg_ref[0] = proj(_C_VG, GLA_VAL_WIDTH).astype(vg_ref.dtype)
    r = proj(_C_RG, GLA_VAL_WIDTH)
    sr_ref[0] = (r * (1.0 / (1.0 + jnp.exp(-r)))).astype(sr_ref.dtype)
    lr = proj(_C_LR, LANES).astype(jnp.bfloat16)
    z = jnp.dot(lr, wlr_ref[...], preferred_element_type=jnp.float32) + bg_ref[...]
    log_sig = jnp.minimum(z, 0.0) - jnp.log(1.0 + jnp.exp(-jnp.abs(z)))
    la_ref[0] = log_sig * (1.0 / GLA_GATE_NORMALIZER)


def _in_proj(x, mod3, g_pre, w_packed, cos_t, sin_t, wlr, bg):
    B, S, D = x.shape
    nb = S // TM_PROJ
    tok = lambda w: pl.BlockSpec((1, TM_PROJ, w), lambda b, i: (b, i, 0))
    const = lambda shape: pl.BlockSpec(shape, lambda b, i: tuple(0 for _ in shape))
    bf = jnp.bfloat16
    att_shapes = [jax.ShapeDtypeStruct((B, S // d, d * ATT_WIDTH), bf) for d in DILATIONS] * 3
    att_specs = [pl.BlockSpec((1, TM_PROJ // d, d * ATT_WIDTH), lambda b, i: (b, i, 0))
                 for d in DILATIONS] * 3
    gla_shapes = [jax.ShapeDtypeStruct((B, S, w), dt) for w, dt in
                  [(GLA_KEY_WIDTH, bf), (GLA_KEY_WIDTH, bf),
                   (GLA_VAL_WIDTH, bf), (GLA_VAL_WIDTH, bf), (GLA_KEY_WIDTH, jnp.float32)]]
    return pl.pallas_call(
        _in_proj_kernel,
        name="in_proj",
        grid=(B, nb),
        in_specs=[tok(D),
                  pl.BlockSpec((1, N_MOD, D), lambda b, i: (b, 0, 0)),
                  const((1, D)),
                  const((D, D_IN_PACKED)),
                  pl.BlockSpec((TM_PROJ, LANES), lambda b, i: (i, 0)),
                  pl.BlockSpec((TM_PROJ, LANES), lambda b, i: (i, 0)),
                  const((LANES, GLA_KEY_WIDTH)),
                  const((1, GLA_KEY_WIDTH))],
        out_specs=att_specs + [tok(s.shape[-1]) for s in gla_shapes],
        out_shape=att_shapes + gla_shapes,
        scratch_shapes=[pltpu.VMEM((ATT_WIDTH // LANES, TM_PROJ, LANES), jnp.float32)] * 6,
        compiler_params=pltpu.CompilerParams(dimension_semantics=("arbitrary", "arbitrary"),
                                             vmem_limit_bytes=VMEM_LIMIT),
    )(x, mod3, g_pre, w_packed, cos_t, sin_t, wlr, bg)


def _attn_kernel(q_ref, kp_ref, kc_ref, vp_ref, vc_ref, o_ref, st_ref):
    i = pl.program_id(2)
    tp = kp_ref.shape[1]
    tq = tp
    n_sub = q_ref.shape[1] // tq
    row = lax.broadcasted_iota(jnp.int32, (tq, tp + tq), 0)
    col = lax.broadcasted_iota(jnp.int32, (tq, tp + tq), 1)
    band = (col >= row) & (col <= row + ATT_SPAN)
    bias_inner = jnp.where(band, 0.0, NEG_BIG)
    bias_first = jnp.where(band & ((col >= tp) | (i > 0)), 0.0, NEG_BIG)
    lane = lax.broadcasted_iota(jnp.int32, (tq, LANES), 1)
    q_head = (lane >> 5) & 1
    for sb in range(n_sub):
        rows = slice(sb * tq, (sb + 1) * tq)
        bias = bias_first if sb == 0 else bias_inner
        stat_m = jnp.zeros((tq, LANES), jnp.float32)
        stat_l = jnp.ones((tq, LANES), jnp.float32)
        for hp in range(ATT_HEADS // 2):
            sl = slice(hp * LANES, (hp + 1) * LANES)
            q2 = q_ref[0, rows, sl]
            if sb == 0:
                kcat = jnp.concatenate([kp_ref[0, :, sl], kc_ref[0, rows, sl]], axis=0)
                vcat = jnp.concatenate([vp_ref[0, :, sl], vc_ref[0, rows, sl]], axis=0)
            else:
                kcat = kc_ref[0, (sb - 1) * tq:(sb + 1) * tq, sl]
                vcat = vc_ref[0, (sb - 1) * tq:(sb + 1) * tq, sl]
            outs, dens = [], []
            for hh in range(2):
                qm = jnp.where(q_head == hh, q2, jnp.zeros_like(q2))
                s = lax.dot_general(qm, kcat, (((1,), (1,)), ((), ())),
                                    preferred_element_type=jnp.float32) + bias
                m = jnp.max(s, axis=1, keepdims=True)
                p = jnp.exp(s - m)
                l = jnp.sum(p, axis=1, keepdims=True)
                outs.append(jnp.dot(p.astype(vcat.dtype), vcat, preferred_element_type=jnp.float32))
                dens.append(l)
                stat_m = jnp.where(lane == 2 * hp + hh, m, stat_m)
                stat_l = jnp.where(lane == 2 * hp + hh, l, stat_l)
            first = lane < ATT_HEAD_DIM
            o_pair = jnp.where(first, outs[0], outs[1]) / jnp.where(first, dens[0], dens[1])
            o_ref[0, rows, sl] = o_pair.astype(o_ref.dtype)
        st_ref[0, rows, :] = stat_m + jnp.log(stat_l)


def _attn_pattern(qv, kv, vv, dil):
    B, L, _ = qv.shape
    W = ATT_WIDTH
    tq = min(TQ_ATT, L)
    nq = L // tq
    cur = pl.BlockSpec((1, tq, W), lambda b, r, i: (b, i, r))
    back = tq // ATT_SPAN
    prev = pl.BlockSpec((1, ATT_SPAN, W), lambda b, r, i: (b, jnp.maximum(i * back - 1, 0), r))
    o, st = pl.pallas_call(
        _attn_kernel,
        name=f"attn_d{dil}",
        grid=(B, dil, nq),
        in_specs=[cur, prev, cur, prev, cur],
        out_specs=[cur, pl.BlockSpec((1, tq, LANES), lambda b, r, i: (b, i, r))],
        out_shape=[jax.ShapeDtypeStruct((B, L, dil * W), jnp.bfloat16),
                   jax.ShapeDtypeStruct((B, L, dil * LANES), jnp.float32)],
        compiler_params=pltpu.CompilerParams(
            dimension_semantics=("arbitrary", "arbitrary", "arbitrary"),
            vmem_limit_bytes=VMEM_LIMIT),
    )(qv, kv, kv, vv, vv)
    return o, st


def _cumsum_rows(x):
    n = x.shape[0]
    row = lax.broadcasted_iota(jnp.int32, x.shape, 0)
    s = 1
    while s < n:
        x = x + jnp.where(row >= s, pltpu.roll(x, s, axis=0), 0.0)
        s *= 2
    return x


def _gla_kernel(q_ref, k_ref, v_ref, la_ref, sr_ref, g_ref, o_ref, st_ref):
    @pl.when(pl.program_id(0) == 0)
    def _():
        st_ref[...] = jnp.zeros_like(st_ref)

    C = GLA_CHUNK
    n_batch = q_ref.shape[0]
    n_chunks = q_ref.shape[1] // C
    lane = lax.broadcasted_iota(jnp.int32, (C, LANES), 1)
    lane_sq = lax.broadcasted_iota(jnp.int32, (LANES, LANES), 1)
    tril = (lax.broadcasted_iota(jnp.int32, (C, C), 0)
            >= lax.broadcasted_iota(jnp.int32, (C, C), 1))
    g = g_ref[...]
    bf = jnp.bfloat16

    def chunk(c, carry):
        r0 = pl.multiple_of(c * C, C)
        rows = pl.ds(r0, C)
        for n in range(n_batch):
            for p in range(GLA_HEADS // 2):
                ksl = slice(p * LANES, (p + 1) * LANES)
                b = _cumsum_rows(la_ref[n, rows, ksl])
                b_last = b[C - 1:C, :]
                q = q_ref[n, rows, ksl].astype(jnp.float32) * (GLA_KEY_DIM ** -0.5)
                k = k_ref[n, rows, ksl].astype(jnp.float32)
                q_dec = q * jnp.exp(b)
                k_inv = (k * jnp.exp(-b)).astype(bf)
                k_dec = (k * jnp.exp(b_last - b)).astype(bf)
                st = st_ref[n, p]
                st_b = st.astype(bf)
                ut = []
                for hh in range(2):
                    h = 2 * p + hh
                    vsl = slice(h * GLA_VAL_DIM, (h + 1) * GLA_VAL_DIM)
                    v = v_ref[n, rows, vsl]
                    own = (lane >= GLA_KEY_DIM) if hh else (lane < GLA_KEY_DIM)
                    qm = jnp.where(own, q_dec, 0.0).astype(bf)
                    att = lax.dot_general(qm, k_inv, (((1,), (1,)), ((), ())),
                                          preferred_element_type=jnp.float32)
                    att = jnp.where(tril, att, 0.0).astype(bf)
                    o = jnp.dot(att, v, preferred_element_type=jnp.float32)
                    o = o + lax.dot_general(qm, st_b, (((1,), (1,)), ((), ())),
                                            preferred_element_type=jnp.float32)
                    o = _rms(o, g) * sr_ref[n, rows, vsl].astype(jnp.float32)
                    o_ref[n, rows, vsl] = o.astype(o_ref.dtype)
                    ut.append(lax.dot_general(v, k_dec, (((0,), (0,)), ((), ())),
                                              preferred_element_type=jnp.float32))
                st_ref[n, p] = (st * jnp.exp(b_last)
                                + jnp.where(lane_sq < GLA_KEY_DIM, ut[0], ut[1]))
        return carry

    lax.fori_loop(0, n_chunks, chunk, 0, unroll=GLA_UNROLL)


def _gla(qg, kg, vg, la, sr, g_gla):
    B, S, _ = qg.shape
    tok = lambda w: pl.BlockSpec((B, TG_GLA, w), lambda i: (0, i, 0))
    return pl.pallas_call(
        _gla_kernel,
        name="gla",
        grid=(S // TG_GLA,),
        in_specs=[tok(GLA_KEY_WIDTH), tok(GLA_KEY_WIDTH), tok(GLA_VAL_WIDTH),
                  tok(GLA_KEY_WIDTH), tok(GLA_VAL_WIDTH),
                  pl.BlockSpec((1, GLA_VAL_DIM), lambda i: (0, 0))],
        out_specs=tok(GLA_VAL_WIDTH),
        out_shape=jax.ShapeDtypeStruct((B, S, GLA_VAL_WIDTH), jnp.bfloat16),
        scratch_shapes=[pltpu.VMEM((B, GLA_HEADS // 2, GLA_VAL_DIM, LANES), jnp.float32)],
        compiler_params=pltpu.CompilerParams(dimension_semantics=("arbitrary",),
                                             vmem_limit_bytes=VMEM_LIMIT),
    )(qg, kg, vg, la, sr, g_gla)


def _out_proj_kernel(o1_ref, o2_ref, o3_ref, s1_ref, s2_ref, s3_ref, og_ref, x_ref, mod_ref,
                     wout_ref, gpost_ref, gpre_ref, rw_ref, rb_ref,
                     x1_ref, h2_ref, ti_ref, tg_ref, cnt_ref, oscr_ref, sscr_ref, otmp_ref,
                     stmp_ref):
    tm = x_ref.shape[1]

    @pl.when((pl.program_id(0) == 0) & (pl.program_id(1) == 0))
    def _():
        cnt_ref[...] = jnp.zeros_like(cnt_ref)

    ns = ATT_WIDTH // LANES
    q4 = tm // 4
    for r in range(4):
        rows = pl.ds(r, q4, stride=4)
        for s in range(ns):
            c0 = r * ATT_WIDTH + s * LANES
            oscr_ref[0, s, rows, :] = o2_ref[0, :, c0:c0 + LANES].astype(jnp.float32)
        sscr_ref[0, rows, :] = s2_ref[0, :, r * LANES:(r + 1) * LANES]
    for r in range(16):
        lo, hi = r % 4, r // 4
        rows = pl.ds(lo * q4 + hi, tm // 16, stride=4)
        for s in range(ns):
            c0 = r * ATT_WIDTH + s * LANES
            otmp_ref[s, rows, :] = o3_ref[0, :, c0:c0 + LANES].astype(jnp.float32)
        stmp_ref[rows, :] = s3_ref[0, :, r * LANES:(r + 1) * LANES]
    for r in range(4):
        rows = pl.ds(r, q4, stride=4)
        for s in range(ns):
            oscr_ref[1, s, rows, :] = otmp_ref[s, r * q4:(r + 1) * q4, :]
        sscr_ref[1, rows, :] = stmp_ref[r * q4:(r + 1) * q4, :]
    gate1 = mod_ref[0, 2:3, :]
    shift2 = mod_ref[0, 3:4, :]
    scale2 = mod_ref[0, 4:5, :]
    bf = jnp.bfloat16
    rw = rw_ref[...]
    w_hi = rw.astype(bf)
    w_lo = (rw - w_hi.astype(jnp.float32)).astype(bf)
    ts = tm // OUT_SUBTILES
    lane = lax.broadcasted_iota(jnp.int32, (ts, LANES), 1)
    lane_f = lane.astype(jnp.float32)
    for sub in range(OUT_SUBTILES):
        rs = slice(sub * ts, (sub + 1) * ts)
        lses = [s1_ref[0, rs, :], sscr_ref[0, rs, :], sscr_ref[1, rs, :]]
        m = jnp.maximum(jnp.maximum(lses[0], lses[1]), lses[2])
        es = [jnp.exp(t - m) for t in lses]
        inv = 1.0 / (es[0] + es[1] + es[2])
        ws = [e * inv for e in es]
        pairs = []
        for hp in range(ATT_HEADS // 2):
            sl = slice(hp * LANES, (hp + 1) * LANES)
            o_pats = [o1_ref[0, rs, sl].astype(jnp.float32), oscr_ref[0, hp, rs, :],
                      oscr_ref[1, hp, rs, :]]
            acc = jnp.zeros((ts, LANES), jnp.float32)
            head_of_lane = 2 * hp + (lane >> 6)
            for w, o in zip(ws, o_pats):
                acc = acc + jnp.take_along_axis(w, head_of_lane, axis=1) * o
            pairs.append(acc.astype(bf))
        mixed = jnp.concatenate(pairs + [og_ref[0, rs, :]], axis=1)
        y = jnp.dot(mixed, wout_ref[...], preferred_element_type=jnp.float32)
        x1 = x_ref[0, rs, :] + _rms(y, gate1 * gpost_ref[...])
        x1_ref[0, rs, :] = x1
        h2 = _rms(x1, gpre_ref[...] * (1.0 + scale2)) + shift2
        h2_ref[0, rs, :] = _pack_bf16_pairs(h2)
        h_hi = h2.astype(bf)
        h_lo = (h2 - h_hi.astype(jnp.float32)).astype(bf)
        logits = (jnp.dot(h_hi, w_hi, preferred_element_type=jnp.float32)
                  + jnp.dot(h_lo, w_hi, preferred_element_type=jnp.float32)
                  + jnp.dot(h_hi, w_lo, preferred_element_type=jnp.float32)) + rb_ref[...]
        vals, idxs = [], []
        for _ in range(TOP_K):
            mk = jnp.max(logits, axis=1, keepdims=True)
            ik = jnp.min(jnp.where(logits == mk, lane_f, float(LANES)), axis=1, keepdims=True)
            logits = jnp.where(lane_f == ik, -jnp.inf, logits)
            vals.append(mk)
            idxs.append(ik)
        ex = [jnp.exp(v - vals[0]) for v in vals]
        den = ex[0] + ex[1] + ex[2] + ex[3]
        ti = jnp.zeros((ts, LANES), jnp.float32)
        tg = jnp.zeros((ts, LANES), jnp.float32)
        for kk in range(TOP_K):
            ti = jnp.where(lane == kk, idxs[kk], ti)
            tg = jnp.where(lane == kk, ex[kk] / den, tg)
        ti_ref[0, rs, :] = ti.astype(jnp.int32)
        tg_ref[0, rs, :] = tg
        chosen = jnp.zeros((ts, LANES), jnp.float32)
        for ik in idxs:
            chosen = chosen + jnp.where(lane_f == ik, 1.0, 0.0)
        cnt_ref[...] += jnp.sum(chosen, axis=0, keepdims=True)


def _out_proj(o_pats, st_pats, og, x, mod3, w_out, g_post, g_pre, rw, rb):
    B, S, D = x.shape
    tok = lambda w: pl.BlockSpec((1, TM_OUT, w), lambda b, i: (b, i, 0))
    const = lambda shape: pl.BlockSpec(shape, lambda b, i: tuple(0 for _ in shape))
    out_shapes = [jax.ShapeDtypeStruct((B, S, D), jnp.float32),
                  jax.ShapeDtypeStruct((B, S, D // 2), jnp.uint32),
                  jax.ShapeDtypeStruct((B, S, LANES), jnp.int32),
                  jax.ShapeDtypeStruct((B, S, LANES), jnp.float32),
                  jax.ShapeDtypeStruct((8, LANES), jnp.float32)]
    return pl.pallas_call(
        _out_proj_kernel,
        name="out_proj",
        grid=(B, S // TM_OUT),
        in_specs=[pl.BlockSpec((1, TM_OUT // d, d * ATT_WIDTH), lambda b, i: (b, i, 0))
                  for d in DILATIONS]
                 + [pl.BlockSpec((1, TM_OUT // d, d * LANES), lambda b, i: (b, i, 0))
                    for d in DILATIONS]
                 + [tok(GLA_VAL_WIDTH), tok(D),
                  pl.BlockSpec((1, N_MOD, D), lambda b, i: (b, 0, 0)),
                  const((D, D)), const((1, D)), const((1, D)),
                  const((D, LANES)), const((1, LANES))],
        out_specs=[tok(D), tok(D // 2), tok(LANES), tok(LANES), const((8, LANES))],
        out_shape=out_shapes,
        scratch_shapes=[pltpu.VMEM((2, ATT_WIDTH // LANES, TM_OUT, LANES), jnp.float32),
                        pltpu.VMEM((2, TM_OUT, LANES), jnp.float32),
                        pltpu.VMEM((ATT_WIDTH // LANES, TM_OUT, LANES), jnp.float32),
                        pltpu.VMEM((TM_OUT, LANES), jnp.float32)],
        compiler_params=pltpu.CompilerParams(dimension_semantics=("arbitrary", "arbitrary"),
                                             vmem_limit_bytes=VMEM_LIMIT),
    )(*o_pats, *st_pats, og, x, mod3, w_out, g_post, g_pre, rw, rb)


def _split_gate_up_kernel(w_ref, wg_ref, wl_ref):
    group = 2 * LANES
    src = lax.broadcasted_iota(jnp.int32, (group, group), 0)
    dst = lax.broadcasted_iota(jnp.int32, (group, group), 1)
    want = jnp.where(dst < LANES, 2 * dst, 2 * (dst - LANES) + 1)
    perm = jnp.where(src == want, 1.0, 0.0).astype(jnp.bfloat16)
    for j in range(w_ref.shape[1] // group):
        t = jnp.dot(w_ref[:, j * group:(j + 1) * group].astype(jnp.bfloat16), perm,
                    preferred_element_type=jnp.float32)
        wg_ref[:, j * LANES:(j + 1) * LANES] = t[:, :LANES].astype(wg_ref.dtype)
        wl_ref[:, j * LANES:(j + 1) * LANES] = t[:, LANES:].astype(wl_ref.dtype)


def _split_gate_up(w_gate_up):
    E, K, N2 = w_gate_up.shape
    out = jax.ShapeDtypeStruct((E, K, N2 // 2), jnp.bfloat16)
    ospec = pl.BlockSpec((None, TK_SPLIT, N2 // 2), lambda e, i: (e, i, 0))
    return pl.pallas_call(
        _split_gate_up_kernel,
        name="split_gate_up",
        grid=(E, K // TK_SPLIT),
        in_specs=[pl.BlockSpec((None, TK_SPLIT, N2), lambda e, i: (e, i, 0))],
        out_specs=[ospec, ospec],
        out_shape=[out, out],
        compiler_params=pltpu.CompilerParams(dimension_semantics=("arbitrary", "arbitrary"),
                                             vmem_limit_bytes=VMEM_LIMIT),
    )(w_gate_up)


def _moe_kernel(be_ref, nu_ref, x_ref, wg_ref, wl_ref, bg_ref, bl_ref, wd_ref, bd_ref, o_ref,
                wd_bf_ref):
    i = pl.program_id(0)
    new_expert = (i == 0) | (be_ref[i] != be_ref[jnp.maximum(i - 1, 0)])

    @pl.when(new_expert & (i < nu_ref[0]))
    def _():
        wd_bf_ref[...] = wd_ref[...].astype(wd_bf_ref.dtype)

    @pl.when(i < nu_ref[0])
    def _():
        lo, hi = _unpack_bf16_pairs(x_ref[...])
        x = jnp.concatenate([lo, hi], axis=1).astype(jnp.bfloat16)
        xg = jnp.dot(x, wg_ref[...], preferred_element_type=jnp.float32) + bg_ref[...]
        xl = jnp.dot(x, wl_ref[...], preferred_element_type=jnp.float32) + bl_ref[...]
        xg = jnp.minimum(xg, SWIGLU_LIMIT)
        xl = jnp.clip(xl, -SWIGLU_LIMIT, SWIGLU_LIMIT)
        act = xg * (1.0 / (1.0 + jnp.exp(-SWIGLU_ALPHA * xg))) * (xl + 1.0)
        out = jnp.dot(act.astype(jnp.bfloat16), wd_bf_ref[...],
                      preferred_element_type=jnp.float32) + bd_ref[...]
        o_ref[...] = _pack_bf16_pairs(out)

    @pl.when(i >= nu_ref[0])
    def _():
        o_ref[...] = jnp.zeros_like(o_ref)


def _moe(xs, blk_e, n_used, wg, wl, bg, bl, wd, bd):
    P, half = xs.shape
    D = 2 * half
    n_blocks = P // TM_MOE
    wspec = lambda k, n: pl.BlockSpec((None, k, n), lambda i, be, nu: (be[i], 0, 0))
    grid_spec = pltpu.PrefetchScalarGridSpec(
        num_scalar_prefetch=2,
        grid=(n_blocks,),
        in_specs=[pl.BlockSpec((TM_MOE, half), lambda i, be, nu: (i, 0)),
                  wspec(D, D_FF), wspec(D, D_FF), wspec(1, D_FF), wspec(1, D_FF),
                  wspec(D_FF, D), wspec(1, D)],
        out_specs=pl.BlockSpec((TM_MOE, half), lambda i, be, nu: (i, 0)),
        scratch_shapes=[pltpu.VMEM((D_FF, D), jnp.bfloat16)],
    )
    return pl.pallas_call(
        _moe_kernel,
        name="moe",
        grid_spec=grid_spec,
        out_shape=jax.ShapeDtypeStruct((P, half), jnp.uint32),
        compiler_params=pltpu.CompilerParams(dimension_semantics=("arbitrary",),
                                             vmem_limit_bytes=VMEM_LIMIT),
    )(blk_e, n_used, xs, wg, wl, bg, bl, wd, bd)


def _final_kernel(x1_ref, y0_ref, y1_ref, y2_ref, y3_ref, tg_ref, mod_ref, g_ref, o_ref):
    gates = tg_ref[...]
    half = y0_ref.shape[1]
    y_lo = jnp.zeros((x1_ref.shape[0], half), jnp.float32)
    y_hi = jnp.zeros((x1_ref.shape[0], half), jnp.float32)
    for kk, yk_ref in enumerate((y0_ref, y1_ref, y2_ref, y3_ref)):
        lo, hi = _unpack_bf16_pairs(yk_ref[...])
        y_lo = y_lo + gates[:, kk:kk + 1] * lo
        y_hi = y_hi + gates[:, kk:kk + 1] * hi
    y = jnp.concatenate([y_lo, y_hi], axis=1)
    gate2 = mod_ref[0, 5:6, :]
    o_ref[...] = x1_ref[...] + _rms(y, gate2 * g_ref[...])


def _final_part_kernel(x1_ref, y0_ref, y1_ref, y2_ref, y3_ref, tg_ref, mod_ref, g_ref, prev_ref,
                       o_ref):
    del prev_ref
    _final_kernel(x1_ref, y0_ref, y1_ref, y2_ref, y3_ref, tg_ref, mod_ref, g_ref, o_ref)


def _final(x1, yk, tg, mod3, g_post, seq_len, part, prev):
    T, D = x1.shape
    tp = yk.shape[0] // TOP_K
    nt = tp // TM_FIN
    t0 = part * nt
    batch = (part * tp) // seq_len
    yspec = lambda kk: pl.BlockSpec((TM_FIN, D // 2), lambda i: (kk * nt + i, 0))
    in_specs = ([pl.BlockSpec((TM_FIN, D), lambda i: (t0 + i, 0))]
                + [yspec(kk) for kk in range(TOP_K)]
                + [pl.BlockSpec((TM_FIN, LANES), lambda i: (t0 + i, 0)),
                   pl.BlockSpec((1, N_MOD, D), lambda i: (batch, 0, 0)),
                   pl.BlockSpec((1, D), lambda i: (0, 0))])
    args = [x1, yk, yk, yk, yk, tg, mod3, g_post]
    body, aliases = _final_kernel, {}
    if prev is not None:
        in_specs.append(pl.BlockSpec(memory_space=pl.ANY))
        args.append(prev)
        body, aliases = _final_part_kernel, {len(args) - 1: 0}
    return pl.pallas_call(
        body,
        name="final",
        grid=(nt,),
        in_specs=in_specs,
        out_specs=pl.BlockSpec((TM_FIN, D), lambda i: (t0 + i, 0)),
        out_shape=jax.ShapeDtypeStruct((T, D), jnp.float32),
        input_output_aliases=aliases,
        compiler_params=pltpu.CompilerParams(dimension_semantics=("arbitrary",),
                                             vmem_limit_bytes=VMEM_LIMIT),
    )(*args)


def _pack_w_in(w_in):
    half = ATT_HEAD_DIM // 2

    def pair_rotary_layout(w):
        k = w.shape[0]
        w = w.reshape(k, ATT_HEADS // 2, 2, 2, half)
        return w.transpose(0, 1, 3, 2, 4).reshape(k, ATT_WIDTH)

    lr = jnp.pad(w_in[:, 3072:3072 + GLA_GATE_RANK], ((0, 0), (0, LANES - GLA_GATE_RANK)))
    return jnp.concatenate([pair_rotary_layout(w_in[:, :ATT_WIDTH]),
                            pair_rotary_layout(w_in[:, ATT_WIDTH:2 * ATT_WIDTH]),
                            w_in[:, 1024:3072], lr], axis=1).astype(jnp.bfloat16)


def _rope_tables(seq_len):
    half = ATT_HEAD_DIM // 2
    inv_freq = ROPE_THETA ** (-jnp.arange(half, dtype=jnp.float32) / half)
    ang = jnp.arange(seq_len, dtype=jnp.float32)[:, None] * inv_freq[None, :]
    cos = jnp.tile(jnp.cos(ang), (1, LANES // half))
    sin = jnp.tile(jnp.sin(ang), (1, LANES // half))
    sign = jnp.where(jnp.arange(LANES) < LANES // 2, -1.0, 1.0)
    return cos, sin * sign


def _route_kernel(ti_ref, cnt_ref, pos_ref, blk_ref, nu_ref, base_ref, tri_ref):
    j = pl.program_id(0)
    tm = ti_ref.shape[0]
    lane = lax.broadcasted_iota(jnp.int32, (tm, LANES), 1)
    ti = ti_ref[...]
    hots = [lane == ti[:, k:k + 1] for k in range(TOP_K)]
    hot_all = jnp.zeros((tm, LANES), jnp.float32)
    for h in hots:
        hot_all = hot_all + jnp.where(h, 1.0, 0.0)
    tile_cnt = jnp.sum(hot_all, axis=0, keepdims=True)

    @pl.when(j == 0)
    def _():
        r = lax.broadcasted_iota(jnp.int32, (tm, tm), 0)
        c = lax.broadcasted_iota(jnp.int32, (tm, tm), 1)
        tri_ref[...] = jnp.where(c < r, 1.0, 0.0).astype(tri_ref.dtype)
        shift = TM_MOE.bit_length() - 1
        lane8 = lax.broadcasted_iota(jnp.int32, cnt_ref.shape, 1)
        cnt = cnt_ref[...].astype(jnp.int32)
        padded = ((cnt + (TM_MOE - 1)) >> shift) << shift
        pend = padded
        s = 1
        while s < N_EXPERTS:
            pend = pend + jnp.where(lane8 >= s, pltpu.roll(pend, s, axis=1), 0)
            s *= 2
        base_ref[...] = (pend - padded).astype(jnp.float32)
        nb = blk_ref.shape[0]
        blk_start = lax.broadcasted_iota(jnp.int32, (nb, LANES), 0) * TM_MOE
        lane_b = lax.broadcasted_iota(jnp.int32, (nb, LANES), 1)
        done = jnp.where((pend[0:1, :] <= blk_start) & (lane_b < N_EXPERTS), 1.0, 0.0)
        blk = jnp.minimum(jnp.sum(done, axis=1, keepdims=True), float(N_EXPERTS - 1))
        blk_ref[...] = jnp.broadcast_to(blk, blk_ref.shape).astype(jnp.int32)
        total = jnp.max(jnp.where(lane8 < N_EXPERTS, pend, 0).astype(jnp.float32),
                        axis=1, keepdims=True)
        nu_ref[...] = jnp.broadcast_to(total.astype(jnp.int32) >> shift, nu_ref.shape)

    before = jnp.dot(tri_ref[...], hot_all.astype(tri_ref.dtype),
                     preferred_element_type=jnp.float32)
    slot = before + base_ref[0:1, :]
    pos = jnp.zeros((tm, LANES), jnp.float32)
    for k, h in enumerate(hots):
        pk = jnp.sum(jnp.where(h, slot, 0.0), axis=1, keepdims=True)
        pos = jnp.where(lane == k, pk, pos)
    pos_ref[...] = pos.astype(jnp.int32)
    base_ref[...] += tile_cnt


def _route(ti, cnt, n_tokens):
    A = n_tokens * TOP_K
    n_blocks = -(-(A + N_EXPERTS * (TM_MOE - 1)) // TM_MOE)
    nb_pad = -(-n_blocks // 8) * 8
    nt = n_tokens // TM_ROUTE
    pos, blk, nu = pl.pallas_call(
        _route_kernel,
        name="route",
        grid=(nt,),
        in_specs=[pl.BlockSpec((TM_ROUTE, LANES), lambda j: (j, 0)),
                  pl.BlockSpec((8, LANES), lambda j: (0, 0))],
        out_specs=[pl.BlockSpec((TM_ROUTE, LANES), lambda j: (j, 0)),
                   pl.BlockSpec((nb_pad, LANES), lambda j: (0, 0)),
                   pl.BlockSpec((8, LANES), lambda j: (0, 0))],
        out_shape=[jax.ShapeDtypeStruct((n_tokens, LANES), jnp.int32),
                   jax.ShapeDtypeStruct((nb_pad, LANES), jnp.int32),
                   jax.ShapeDtypeStruct((8, LANES), jnp.int32)],
        scratch_shapes=[pltpu.VMEM((8, LANES), jnp.float32),
                        pltpu.VMEM((TM_ROUTE, TM_ROUTE), jnp.bfloat16)],
        compiler_params=pltpu.CompilerParams(dimension_semantics=("arbitrary",),
                                             vmem_limit_bytes=VMEM_LIMIT),
    )(ti, cnt)
    pos_kmajor = pos[:, :TOP_K].T
    return pos_kmajor, n_blocks * TM_MOE, blk[:n_blocks, 0], nu[0, :1]


def kernel(x, c, w_mod, b_mod, g_pre_mix, w_in, w_gate_lr, b_gate, g_gla, w_out, g_post_mix,
           g_pre_ffn, router_w, router_b, w_gate_up, b_gate_up, w_down, b_down, g_post_ffn):
    B, S, D = x.shape
    T = B * S
    bf = jnp.bfloat16
    cos_t, sin_t = _rope_tables(S)
    for l in range(w_mod.shape[0]):
        wg, wl = _split_gate_up(w_gate_up[l])
        wd = w_down[l]
        mod3 = _mod(c, w_mod[l], b_mod[l]).reshape(B, N_MOD, D)
        wlr = jnp.pad(w_gate_lr[l], ((0, LANES - GLA_GATE_RANK), (0, 0))).astype(bf)
        proj = _in_proj(x, mod3, g_pre_mix[l:l + 1], _pack_w_in(w_in[l]), cos_t, sin_t, wlr,
                        b_gate[l:l + 1])
        n_pat = len(DILATIONS)
        qa, ka, va = proj[:n_pat], proj[n_pat:2 * n_pat], proj[2 * n_pat:3 * n_pat]
        qg, kg, vg, sr, la = proj[3 * n_pat:]
        pats = [_attn_pattern(qa[j], ka[j], va[j], d) for j, d in enumerate(DILATIONS)]
        og = _gla(qg, kg, vg, la, sr, g_gla[l:l + 1])
        rw = jnp.pad(router_w[l], ((0, 0), (0, LANES - N_EXPERTS)))
        rb = jnp.pad(router_b[l], (0, LANES - N_EXPERTS), constant_values=NEG_BIG)[None]
        x1, h2, ti, tg, cnt = _out_proj([p[0] for p in pats], [p[1] for p in pats], og, x, mod3,
                                        w_out[l].astype(bf), g_post_mix[l:l + 1],
                                        g_pre_ffn[l:l + 1], rw, rb)
        pos_kmajor, n_slots, blk_e, n_used = _route(ti.reshape(T, LANES), cnt, T)
        xs = _scatter_rows(h2.reshape(T, D // 2), pos_kmajor, n_slots)
        out_buf = _moe(xs, blk_e, n_used, wg, wl,
                       b_gate_up[l][:, None, 0::2], b_gate_up[l][:, None, 1::2],
                       wd, b_down[l][:, None, :])
        out = None
        tp = T // FINAL_PARTS
        for part in range(FINAL_PARTS):
            yk = _gather_rows(out_buf, pos_kmajor[:, part * tp:(part + 1) * tp].reshape(-1))
            out = _final(x1.reshape(T, D), yk, tg.reshape(T, LANES), mod3, g_post_ffn[l:l + 1],
                         S, part, out)
        x = out.reshape(B, S, D)
    return x
```

```python
import functools

import numpy as np
import jax
import jax.numpy as jnp
from jax import lax
from jax.experimental import pallas as pl
from jax.experimental.pallas import tpu as pltpu
from jax.experimental.pallas import tpu_sc as plsc

D_MODEL = 1024
ATT_HEADS = 8
ATT_HEAD_DIM = 64
ATT_WIDTH = ATT_HEADS * ATT_HEAD_DIM
DILATIONS = (1, 4, 16)
ATT_SPAN = 128
ROPE_THETA = 10000.0
GLA_HEADS = 4
GLA_KEY_DIM = 64
GLA_VAL_DIM = 128
GLA_KEY_WIDTH = GLA_HEADS * GLA_KEY_DIM
GLA_VAL_WIDTH = GLA_HEADS * GLA_VAL_DIM
GLA_GATE_RANK = 16
GLA_GATE_NORMALIZER = 16.0
GLA_CHUNK = 64
N_EXPERTS = 32
TOP_K = 4
D_FF = D_MODEL
SWIGLU_LIMIT = 7.0
SWIGLU_ALPHA = 1.702
NORM_EPS = 1e-6
N_MOD = 6

LANES = 128
NEG_BIG = -1e30

TM_PROJ = 512
TQ_ATT = 1024
TG_GLA = 1024
GLA_UNROLL = 8
TM_OUT = 512
OUT_SUBTILES = 1
TM_MOE = 512
TM_FIN = 1024
TN_MOD = 1024
TK_SPLIT = 1024
TM_ROUTE = 1024
FINAL_PARTS = 4
SC_INDEX_WINDOW = 128
SC_GATHER_WINDOW = 64
VMEM_LIMIT = 48 * 1024 * 1024

_C_QA, _C_KA, _C_VA = 0, 512, 1024
_C_QG, _C_KG, _C_VG, _C_RG, _C_LR = 1536, 1792, 2048, 2560, 3072
D_IN_PACKED = 3200


def _rms(x, g):
    return x * lax.rsqrt(jnp.mean(x * x, axis=-1, keepdims=True) + NORM_EPS) * g


def _pack_bf16_pairs(x):
    n = x.shape[1] // 2
    u = lax.bitcast_convert_type(x.astype(jnp.bfloat16).astype(jnp.float32), jnp.uint32)
    return (u[:, :n] >> 16) | (u[:, n:] & jnp.uint32(0xFFFF0000))


def _unpack_bf16_pairs(w):
    lo = lax.bitcast_convert_type(w << 16, jnp.float32)
    hi = lax.bitcast_convert_type(w & jnp.uint32(0xFFFF0000), jnp.float32)
    return lo, hi


def _gather_rows(data, idx):
    n_rows = idx.shape[0]
    width = data.shape[1]
    mesh = plsc.VectorSubcoreMesh(core_axis_name="core", subcore_axis_name="subcore")
    n_workers = mesh.num_cores * mesh.num_subcores
    per_worker = n_rows // n_workers
    assert per_worker * n_workers == n_rows and per_worker % SC_INDEX_WINDOW == 0
    halves = SC_INDEX_WINDOW // SC_GATHER_WINDOW

    @pl.kernel(out_type=jax.ShapeDtypeStruct((n_rows, width), data.dtype), mesh=mesh,
               name="gather_rows",
               scratch_types=[pltpu.VMEM((1, SC_INDEX_WINDOW), jnp.int32),
                              pltpu.VMEM((SC_GATHER_WINDOW, width), data.dtype)])
    def gather(x_hbm, i_hbm, o_hbm, idx_vmem, rows_vmem):
        worker = lax.axis_index("core") * mesh.num_subcores + lax.axis_index("subcore")
        base = worker * per_worker

        @pl.loop(0, per_worker // SC_INDEX_WINDOW)
        def _(j):
            off = base + j * SC_INDEX_WINDOW
            pltpu.sync_copy(i_hbm.at[:, pl.ds(off, SC_INDEX_WINDOW)], idx_vmem)
            for h in range(halves):
                part = idx_vmem.at[0, pl.ds(h * SC_GATHER_WINDOW, SC_GATHER_WINDOW)]
                pltpu.sync_copy(x_hbm.at[part], rows_vmem)
                pltpu.sync_copy(rows_vmem,
                                o_hbm.at[pl.ds(off + h * SC_GATHER_WINDOW, SC_GATHER_WINDOW)])

    return gather(data, idx.reshape(1, n_rows))


def _scatter_rows(data, idx, n_out):
    n_copies, n_rows = idx.shape
    width = data.shape[1]
    mesh = plsc.VectorSubcoreMesh(core_axis_name="core", subcore_axis_name="subcore")
    n_workers = mesh.num_cores * mesh.num_subcores
    per_worker = n_rows // n_workers
    assert per_worker * n_workers == n_rows and per_worker % SC_INDEX_WINDOW == 0
    halves = SC_INDEX_WINDOW // SC_GATHER_WINDOW

    @pl.kernel(out_type=jax.ShapeDtypeStruct((n_out, width), data.dtype), mesh=mesh,
               name="scatter_rows",
               scratch_types=[pltpu.VMEM((n_copies, SC_INDEX_WINDOW), jnp.int32),
                              pltpu.VMEM((SC_GATHER_WINDOW, width), data.dtype)])
    def scatter(x_hbm, i_hbm, o_hbm, idx_vmem, rows_vmem):
        worker = lax.axis_index("core") * mesh.num_subcores + lax.axis_index("subcore")
        base = worker * per_worker

        @pl.loop(0, per_worker // SC_INDEX_WINDOW)
        def _(j):
            off = base + j * SC_INDEX_WINDOW
            pltpu.sync_copy(i_hbm.at[:, pl.ds(off, SC_INDEX_WINDOW)], idx_vmem)
            for h in range(halves):
                pltpu.sync_copy(x_hbm.at[pl.ds(off + h * SC_GATHER_WINDOW, SC_GATHER_WINDOW)],
                                rows_vmem)
                for k in range(n_copies):
                    part = idx_vmem.at[k, pl.ds(h * SC_GATHER_WINDOW, SC_GATHER_WINDOW)]
                    pltpu.sync_copy(rows_vmem, o_hbm.at[part])

    return scatter(data, idx)


def _mod_kernel(ct_ref, w_ref, b_ref, o_ref):
    ct = ct_ref[...]
    s = ct * (1.0 / (1.0 + jnp.exp(-ct)))
    w = w_ref[...]
    rows = [jnp.sum(s[:, b:b + 1] * w, axis=0, keepdims=True) for b in range(ct.shape[1])]
    o_ref[...] = jnp.concatenate(rows, axis=0) + b_ref[...]


def _mod(c, w_mod, b_mod):
    B = c.shape[0]
    n = w_mod.shape[1]
    return pl.pallas_call(
        _mod_kernel,
        name="mod",
        grid=(n // TN_MOD,),
        in_specs=[pl.BlockSpec((D_MODEL, B), lambda j: (0, 0)),
                  pl.BlockSpec((D_MODEL, TN_MOD), lambda j: (0, j)),
                  pl.BlockSpec((1, TN_MOD), lambda j: (0, j))],
        out_specs=pl.BlockSpec((B, TN_MOD), lambda j: (0, j)),
        out_shape=jax.ShapeDtypeStruct((B, n), jnp.float32),
        compiler_params=pltpu.CompilerParams(dimension_semantics=("arbitrary",),
                                             vmem_limit_bytes=VMEM_LIMIT),
    )(c.T, w_mod, b_mod.reshape(1, n))


def _store_residue_views(scr_ref, scr4_ref, out_refs):
    ns, tm, _ = scr_ref.shape
    w = ns * LANES
    o1_ref, o4_ref, o16_ref = out_refs
    assert DILATIONS == (1, 4, 16)
    q4 = tm // 4
    for s in range(ns):
        o1_ref[0, :, s * LANES:(s + 1) * LANES] = scr_ref[s].astype(o1_ref.dtype)
    for r in range(4):
        for s in range(ns):
            vals = scr_ref[s, pl.ds(r, q4, stride=4), :]
            scr4_ref[s, r * q4:(r + 1) * q4, :] = vals
            c0 = r * w + s * LANES
            o4_ref[0, :, c0:c0 + LANES] = vals.astype(o4_ref.dtype)
    for r in range(16):
        lo, hi = r % 4, r // 4
        for s in range(ns):
            vals = scr4_ref[s, pl.ds(lo * q4 + hi, tm // 16, stride=4), :]
            c0 = r * w + s * LANES
            o16_ref[0, :, c0:c0 + LANES] = vals.astype(o16_ref.dtype)


def _in_proj_kernel(x_ref, mod_ref, g_ref, w_ref, cos_ref, sin_ref, wlr_ref, bg_ref,
                    qa1_ref, qa4_ref, qa16_ref, ka1_ref, ka4_ref, ka16_ref,
                    va1_ref, va4_ref, va16_ref, qg_ref, kg_ref, vg_ref, sr_ref, la_ref,
                    qs_ref, ks_ref, vs_ref, q4_ref, k4_ref, v4_ref):
    x = x_ref[0]
    shift = mod_ref[0, 0:1, :]
    scale = mod_ref[0, 1:2, :]
    h = (_rms(x, g_ref[...] * (1.0 + scale)) + shift).astype(jnp.bfloat16)

    def proj(c0, width):
        return jnp.dot(h, w_ref[:, c0:c0 + width], preferred_element_type=jnp.float32)

    cos = cos_ref[...]
    sin = sin_ref[...]

    def rope(c0, scr_ref, mult):
        p = proj(c0, ATT_WIDTH)
        for s in range(ATT_WIDTH // LANES):
            t = p[:, s * LANES:(s + 1) * LANES]
            r = t * cos + pltpu.roll(t, LANES // 2, axis=1) * sin
            if mult != 1.0:
                r = r * mult
            scr_ref[s] = r

    rope(_C_QA, qs_ref, ATT_HEAD_DIM ** -0.5)
    _store_residue_views(qs_ref, q4_ref, (qa1_ref, qa4_ref, qa16_ref))
    rope(_C_KA, ks_ref, 1.0)
    _store_residue_views(ks_ref, k4_ref, (ka1_ref, ka4_ref, ka16_ref))
    pv = proj(_C_VA, ATT_WIDTH)
    for s in range(ATT_WIDTH // LANES):
        vs_ref[s] = pv[:, s * LANES:(s + 1) * LANES]
    _store_residue_views(vs_ref, v4_ref, (va1_ref, va4_ref, va16_ref))
    qg_ref[0] = proj(_C_QG, GLA_KEY_WIDTH).astype(qg_ref.dtype)
    kg_ref[0] = proj(_C_KG, GLA_KEY_WIDTH).astype(kg_ref.dtype)
    vg_ref[0] = proj(_C_VG, GLA_VAL_WIDTH).astype(vg_ref.dtype)
    r = proj(_C_RG, GLA_VAL_WIDTH)
    sr_ref[0] = (r * (1.0 / (1.0 + jnp.exp(-r)))).astype(sr_ref.dtype)
    lr = proj(_C_LR, LANES).astype(jnp.bfloat16)
    z = jnp.dot(lr, wlr_ref[...], preferred_element_type=jnp.float32) + bg_ref[...]
    log_sig = jnp.minimum(z, 0.0) - jnp.log(1.0 + jnp.exp(-jnp.abs(z)))
    la_ref[0] = log_sig * (1.0 / GLA_GATE_NORMALIZER)


def _in_proj(x, mod3, g_pre, w_packed, cos_t, sin_t, wlr, bg):
    B, S, D = x.shape
    nb = S // TM_PROJ
    tok = lambda w: pl.BlockSpec((1, TM_PROJ, w), lambda b, i: (b, i, 0))
    const = lambda shape: pl.BlockSpec(shape, lambda b, i: tuple(0 for _ in shape))
    bf = jnp.bfloat16
    att_shapes = [jax.ShapeDtypeStruct((B, S // d, d * ATT_WIDTH), bf) for d in DILATIONS] * 3
    att_specs = [pl.BlockSpec((1, TM_PROJ // d, d * ATT_WIDTH), lambda b, i: (b, i, 0))
                 for d in DILATIONS] * 3
    gla_shapes = [jax.ShapeDtypeStruct((B, S, w), dt) for w, dt in
                  [(GLA_KEY_WIDTH, bf), (GLA_KEY_WIDTH, bf),
                   (GLA_VAL_WIDTH, bf), (GLA_VAL_WIDTH, bf), (GLA_KEY_WIDTH, jnp.float32)]]
    return pl.pallas_call(
        _in_proj_kernel,
        name="in_proj",
        grid=(B, nb),
        in_specs=[tok(D),
                  pl.BlockSpec((1, N_MOD, D), lambda b, i: (b, 0, 0)),
                  const((1, D)),
                  const((D, D_IN_PACKED)),
                  pl.BlockSpec((TM_PROJ, LANES), lambda b, i: (i, 0)),
                  pl.BlockSpec((TM_PROJ, LANES), lambda b, i: (i, 0)),
                  const((LANES, GLA_KEY_WIDTH)),
                  const((1, GLA_KEY_WIDTH))],
        out_specs=att_specs + [tok(s.shape[-1]) for s in gla_shapes],
        out_shape=att_shapes + gla_shapes,
        scratch_shapes=[pltpu.VMEM((ATT_WIDTH // LANES, TM_PROJ, LANES), jnp.float32)] * 6,
        compiler_params=pltpu.CompilerParams(dimension_semantics=("arbitrary", "arbitrary"),
                                             vmem_limit_bytes=VMEM_LIMIT),
    )(x, mod3, g_pre, w_packed, cos_t, sin_t, wlr, bg)


def _attn_kernel(q_ref, kp_ref, kc_ref, vp_ref, vc_ref, o_ref, st_ref):
    i = pl.program_id(2)
    tp = kp_ref.shape[1]
    tq = tp
    n_sub = q_ref.shape[1] // tq
    row = lax.broadcasted_iota(jnp.int32, (tq, tp + tq), 0)
    col = lax.broadcasted_iota(jnp.int32, (tq, tp + tq), 1)
    band = (col >= row) & (col <= row + ATT_SPAN)
    bias_inner = jnp.where(band, 0.0, NEG_BIG)
    bias_first = jnp.where(band & ((col >= tp) | (i > 0)), 0.0, NEG_BIG)
    lane = lax.broadcasted_iota(jnp.int32, (tq, LANES), 1)
    q_head = (lane >> 5) & 1
    for sb in range(n_sub):
        rows = slice(sb * tq, (sb + 1) * tq)
        bias = bias_first if sb == 0 else bias_inner
        stat_m = jnp.zeros((tq, LANES), jnp.float32)
        stat_l = jnp.ones((tq, LANES), jnp.float32)
        for hp in range(ATT_HEADS // 2):
            sl = slice(hp * LANES, (hp + 1) * LANES)
            q2 = q_ref[0, rows, sl]
            if sb == 0:
                kcat = jnp.concatenate([kp_ref[0, :, sl], kc_ref[0, rows, sl]], axis=0)
                vcat = jnp.concatenate([vp_ref[0, :, sl], vc_ref[0, rows, sl]], axis=0)
            else:
                kcat = kc_ref[0, (sb - 1) * tq:(sb + 1) * tq, sl]
                vcat = vc_ref[0, (sb - 1) * tq:(sb + 1) * tq, sl]
            outs, dens = [], []
            for hh in range(2):
                qm = jnp.where(q_head == hh, q2, jnp.zeros_like(q2))
                s = lax.dot_general(qm, kcat, (((1,), (1,)), ((), ())),
                                    preferred_element_type=jnp.float32) + bias
                m = jnp.max(s, axis=1, keepdims=True)
                p = jnp.exp(s - m)
                l = jnp.sum(p, axis=1, keepdims=True)
                outs.append(jnp.dot(p.astype(vcat.dtype), vcat, preferred_element_type=jnp.float32))
                dens.append(l)
                stat_m = jnp.where(lane == 2 * hp + hh, m, stat_m)
                stat_l = jnp.where(lane == 2 * hp + hh, l, stat_l)
            first = lane < ATT_HEAD_DIM
            o_pair = jnp.where(first, outs[0], outs[1]) / jnp.where(first, dens[0], dens[1])
            o_ref[0, rows, sl] = o_pair.astype(o_ref.dtype)
        st_ref[0, rows, :] = stat_m + jnp.log(stat_l)


def _attn_pattern(qv, kv, vv, dil):
    B, L, _ = qv.shape
    W = ATT_WIDTH
    tq = min(TQ_ATT, L)
    nq = L // tq
    cur = pl.BlockSpec((1, tq, W), lambda b, r, i: (b, i, r))
    back = tq // ATT_SPAN
    prev = pl.BlockSpec((1, ATT_SPAN, W), lambda b, r, i: (b, jnp.maximum(i * back - 1, 0), r))
    o, st = pl.pallas_call(
        _attn_kernel,
        name=f"attn_d{dil}",
        grid=(B, dil, nq),
        in_specs=[cur, prev, cur, prev, cur],
        out_specs=[cur, pl.BlockSpec((1, tq, LANES), lambda b, r, i: (b, i, r))],
        out_shape=[jax.ShapeDtypeStruct((B, L, dil * W), jnp.bfloat16),
                   jax.ShapeDtypeStruct((B, L, dil * LANES), jnp.float32)],
        compiler_params=pltpu.CompilerParams(
            dimension_semantics=("arbitrary", "arbitrary", "arbitrary"),
            vmem_limit_bytes=VMEM_LIMIT),
    )(qv, kv, kv, vv, vv)
    return o, st


def _cumsum_rows(x):
    n = x.shape[0]
    row = lax.broadcasted_iota(jnp.int32, x.shape, 0)
    s = 1
    while s < n:
        x = x + jnp.where(row >= s, pltpu.roll(x, s, axis=0), 0.0)
        s *= 2
    return x


def _gla_kernel(q_ref, k_ref, v_ref, la_ref, sr_ref, g_ref, o_ref, st_ref):
    @pl.when(pl.program_id(0) == 0)
    def _():
        st_ref[...] = jnp.zeros_like(st_ref)

    C = GLA_CHUNK
    n_batch = q_ref.shape[0]
    n_chunks = q_ref.shape[1] // C
    lane = lax.broadcasted_iota(jnp.int32, (C, LANES), 1)
    lane_sq = lax.broadcasted_iota(jnp.int32, (LANES, LANES), 1)
    tril = (lax.broadcasted_iota(jnp.int32, (C, C), 0)
            >= lax.broadcasted_iota(jnp.int32, (C, C), 1))
    g = g_ref[...]
    bf = jnp.bfloat16

    def chunk(c, carry):
        r0 = pl.multiple_of(c * C, C)
        rows = pl.ds(r0, C)
        for n in range(n_batch):
            for p in range(GLA_HEADS // 2):
                ksl = slice(p * LANES, (p + 1) * LANES)
                b = _cumsum_rows(la_ref[n, rows, ksl])
                b_last = b[C - 1:C, :]
                q = q_ref[n, rows, ksl].astype(jnp.float32) * (GLA_KEY_DIM ** -0.5)
                k = k_ref[n, rows, ksl].astype(jnp.float32)
                q_dec = q * jnp.exp(b)
                k_inv = (k * jnp.exp(-b)).astype(bf)
                k_dec = (k * jnp.exp(b_last - b)).astype(bf)
                st = st_ref[n, p]
                st_b = st.astype(bf)
                ut = []
                for hh in range(2):
                    h = 2 * p + hh
                    vsl = slice(h * GLA_VAL_DIM, (h + 1) * GLA_VAL_DIM)
                    v = v_ref[n, rows, vsl]
                    own = (lane >= GLA_KEY_DIM) if hh else (lane < GLA_KEY_DIM)
                    qm = jnp.where(own, q_dec, 0.0).astype(bf)
                    att = lax.dot_general(qm, k_inv, (((1,), (1,)), ((), ())),
                                          preferred_element_type=jnp.float32)
                    att = jnp.where(tril, att, 0.0).astype(bf)
                    o = jnp.dot(att, v, preferred_element_type=jnp.float32)
                    o = o + lax.dot_general(qm, st_b, (((1,), (1,)), ((), ())),
                                            preferred_element_type=jnp.float32)
                    o = _rms(o, g) * sr_ref[n, rows, vsl].astype(jnp.float32)
                    o_ref[n, rows, vsl] = o.astype(o_ref.dtype)
                    ut.append(lax.dot_general(v, k_dec, (((0,), (0,)), ((), ())),
                                              preferred_element_type=jnp.float32))
                st_ref[n, p] = (st * jnp.exp(b_last)
                                + jnp.where(lane_sq < GLA_KEY_DIM, ut[0], ut[1]))
        return carry

    lax.fori_loop(0, n_chunks, chunk, 0, unroll=GLA_UNROLL)


def _gla(qg, kg, vg, la, sr, g_gla):
    B, S, _ = qg.shape
    tok = lambda w: pl.BlockSpec((B, TG_GLA, w), lambda i: (0, i, 0))
    return pl.pallas_call(
        _gla_kernel,
        name="gla",
        grid=(S // TG_GLA,),
        in_specs=[tok(GLA_KEY_WIDTH), tok(GLA_KEY_WIDTH), tok(GLA_VAL_WIDTH),
                  tok(GLA_KEY_WIDTH), tok(GLA_VAL_WIDTH),
                  pl.BlockSpec((1, GLA_VAL_DIM), lambda i: (0, 0))],
        out_specs=tok(GLA_VAL_WIDTH),
        out_shape=jax.ShapeDtypeStruct((B, S, GLA_VAL_WIDTH), jnp.bfloat16),
        scratch_shapes=[pltpu.VMEM((B, GLA_HEADS // 2, GLA_VAL_DIM, LANES), jnp.float32)],
        compiler_params=pltpu.CompilerParams(dimension_semantics=("arbitrary",),
                                             vmem_limit_bytes=VMEM_LIMIT),
    )(qg, kg, vg, la, sr, g_gla)


def _out_proj_kernel(o1_ref, o2_ref, o3_ref, s1_ref, s2_ref, s3_ref, og_ref, x_ref, mod_ref,
                     wout_ref, gpost_ref, gpre_ref, rw_ref, rb_ref,
                     x1_ref, h2_ref, ti_ref, tg_ref, cnt_ref, oscr_ref, sscr_ref, otmp_ref,
                     stmp_ref):
    tm = x_ref.shape[1]

    @pl.when((pl.program_id(0) == 0) & (pl.program_id(1) == 0))
    def _():
        cnt_ref[...] = jnp.zeros_like(cnt_ref)

    ns = ATT_WIDTH // LANES
    q4 = tm // 4
    for r in range(4):
        rows = pl.ds(r, q4, stride=4)
        for s in range(ns):
            c0 = r * ATT_WIDTH + s * LANES
            oscr_ref[0, s, rows, :] = o2_ref[0, :, c0:c0 + LANES].astype(jnp.float32)
        sscr_ref[0, rows, :] = s2_ref[0, :, r * LANES:(r + 1) * LANES]
    for r in range(16):
        lo, hi = r % 4, r // 4
        rows = pl.ds(lo * q4 + hi, tm // 16, stride=4)
        for s in range(ns):
            c0 = r * ATT_WIDTH + s * LANES
            otmp_ref[s, rows, :] = o3_ref[0, :, c0:c0 + LANES].astype(jnp.float32)
        stmp_ref[rows, :] = s3_ref[0, :, r * LANES:(r + 1) * LANES]
    for r in range(4):
        rows = pl.ds(r, q4, stride=4)
        for s in range(ns):
            oscr_ref[1, s, rows, :] = otmp_ref[s, r * q4:(r + 1) * q4, :]
        sscr_ref[1, rows, :] = stmp_ref[r * q4:(r + 1) * q4, :]
    gate1 = mod_ref[0, 2:3, :]
    shift2 = mod_ref[0, 3:4, :]
    scale2 = mod_ref[0, 4:5, :]
    bf = jnp.bfloat16
    rw = rw_ref[...]
    w_hi = rw.astype(bf)
    w_lo = (rw - w_hi.astype(jnp.float32)).astype(bf)
    ts = tm // OUT_SUBTILES
    lane = lax.broadcasted_iota(jnp.int32, (ts, LANES), 1)
    lane_f = lane.astype(jnp.float32)
    for sub in range(OUT_SUBTILES):
        rs = slice(sub * ts, (sub + 1) * ts)
        lses = [s1_ref[0, rs, :], sscr_ref[0, rs, :], sscr_ref[1, rs, :]]
        m = jnp.maximum(jnp.maximum(lses[0], lses[1]), lses[2])
        es = [jnp.exp(t - m) for t in lses]
        inv = 1.0 / (es[0] + es[1] + es[2])
        ws = [e * inv for e in es]
        pairs = []
        for hp in range(ATT_HEADS // 2):
            sl = slice(hp * LANES, (hp + 1) * LANES)
            o_pats = [o1_ref[0, rs, sl].astype(jnp.float32), oscr_ref[0, hp, rs, :],
                      oscr_ref[1, hp, rs, :]]
            acc = jnp.zeros((ts, LANES), jnp.float32)
            head_of_lane = 2 * hp + (lane >> 6)
            for w, o in zip(ws, o_pats):
                acc = acc + jnp.take_along_axis(w, head_of_lane, axis=1) * o
            pairs.append(acc.astype(bf))
        mixed = jnp.concatenate(pairs + [og_ref[0, rs, :]], axis=1)
        y = jnp.dot(mixed, wout_ref[...], preferred_element_type=jnp.float32)
        x1 = x_ref[0, rs, :] + _rms(y, gate1 * gpost_ref[...])
        x1_ref[0, rs, :] = x1
        h2 = _rms(x1, gpre_ref[...] * (1.0 + scale2)) + shift2
        h2_ref[0, rs, :] = _pack_bf16_pairs(h2)
        h_hi = h2.astype(bf)
        h_lo = (h2 - h_hi.astype(jnp.float32)).astype(bf)
        logits = (jnp.dot(h_hi, w_hi, preferred_element_type=jnp.float32)
                  + jnp.dot(h_lo, w_hi, preferred_element_type=jnp.float32)
                  + jnp.dot(h_hi, w_lo, preferred_element_type=jnp.float32)) + rb_ref[...]
        vals, idxs = [], []
        for _ in range(TOP_K):
            mk = jnp.max(logits, axis=1, keepdims=True)
            ik = jnp.min(jnp.where(logits == mk, lane_f, float(LANES)), axis=1, keepdims=True)
            logits = jnp.where(lane_f == ik, -jnp.inf, logits)
            vals.append(mk)
            idxs.append(ik)
        ex = [jnp.exp(v - vals[0]) for v in vals]
        den = ex[0] + ex[1] + ex[2] + ex[3]
        ti = jnp.zeros((ts, LANES), jnp.float32)
        tg = jnp.zeros((ts, LANES), jnp.float32)
        for kk in range(TOP_K):
            ti = jnp.where(lane == kk, idxs[kk], ti)
            tg = jnp.where(lane == kk, ex[kk] / den, tg)
        ti_ref[0, rs, :] = ti.astype(jnp.int32)
        tg_ref[0, rs, :] = tg
        chosen = jnp.zeros((ts, LANES), jnp.float32)
        for ik in idxs:
            chosen = chosen + jnp.where(lane_f == ik, 1.0, 0.0)
        cnt_ref[...] += jnp.sum(chosen, axis=0, keepdims=True)


def _out_proj(o_pats, st_pats, og, x, mod3, w_out, g_post, g_pre, rw, rb):
    B, S, D = x.shape
    tok = lambda w: pl.BlockSpec((1, TM_OUT, w), lambda b, i: (b, i, 0))
    const = lambda shape: pl.BlockSpec(shape, lambda b, i: tuple(0 for _ in shape))
    out_shapes = [jax.ShapeDtypeStruct((B, S, D), jnp.float32),
                  jax.ShapeDtypeStruct((B, S, D // 2), jnp.uint32),
                  jax.ShapeDtypeStruct((B, S, LANES), jnp.int32),
                  jax.ShapeDtypeStruct((B, S, LANES), jnp.float32),
                  jax.ShapeDtypeStruct((8, LANES), jnp.float32)]
    return pl.pallas_call(
        _out_proj_kernel,
        name="out_proj",
        grid=(B, S // TM_OUT),
        in_specs=[pl.BlockSpec((1, TM_OUT // d, d * ATT_WIDTH), lambda b, i: (b, i, 0))
                  for d in DILATIONS]
                 + [pl.BlockSpec((1, TM_OUT // d, d * LANES), lambda b, i: (b, i, 0))
                    for d in DILATIONS]
                 + [tok(GLA_VAL_WIDTH), tok(D),
                  pl.BlockSpec((1, N_MOD, D), lambda b, i: (b, 0, 0)),
                  const((D, D)), const((1, D)), const((1, D)),
                  const((D, LANES)), const((1, LANES))],
        out_specs=[tok(D), tok(D // 2), tok(LANES), tok(LANES), const((8, LANES))],
        out_shape=out_shapes,
        scratch_shapes=[pltpu.VMEM((2, ATT_WIDTH // LANES, TM_OUT, LANES), jnp.float32),
                        pltpu.VMEM((2, TM_OUT, LANES), jnp.float32),
                        pltpu.VMEM((ATT_WIDTH // LANES, TM_OUT, LANES), jnp.float32),
                        pltpu.VMEM((TM_OUT, LANES), jnp.float32)],
        compiler_params=pltpu.CompilerParams(dimension_semantics=("arbitrary", "arbitrary"),
                                             vmem_limit_bytes=VMEM_LIMIT),
    )(*o_pats, *st_pats, og, x, mod3, w_out, g_post, g_pre, rw, rb)


def _split_gate_up_kernel(w_ref, wg_ref, wl_ref):
    group = 2 * LANES
    src = lax.broadcasted_iota(jnp.int32, (group, group), 0)
    dst = lax.broadcasted_iota(jnp.int32, (group, group), 1)
    want = jnp.where(dst < LANES, 2 * dst, 2 * (dst - LANES) + 1)
    perm = jnp.where(src == want, 1.0, 0.0).astype(jnp.bfloat16)
    for j in range(w_ref.shape[1] // group):
        t = jnp.dot(w_ref[:, j * group:(j + 1) * group].astype(jnp.bfloat16), perm,
                    preferred_element_type=jnp.float32)
        wg_ref[:, j * LANES:(j + 1) * LANES] = t[:, :LANES].astype(wg_ref.dtype)
        wl_ref[:, j * LANES:(j + 1) * LANES] = t[:, LANES:].astype(wl_ref.dtype)


def _split_gate_up(w_gate_up):
    E, K, N2 = w_gate_up.shape
    out = jax.ShapeDtypeStruct((E, K, N2 // 2), jnp.bfloat16)
    ospec = pl.BlockSpec((None, TK_SPLIT, N2 // 2), lambda e, i: (e, i, 0))
    return pl.pallas_call(
        _split_gate_up_kernel,
        name="split_gate_up",
        grid=(E, K // TK_SPLIT),
        in_specs=[pl.BlockSpec((None, TK_SPLIT, N2), lambda e, i: (e, i, 0))],
        out_specs=[ospec, ospec],
        out_shape=[out, out],
        compiler_params=pltpu.CompilerParams(dimension_semantics=("arbitrary", "arbitrary"),
                                             vmem_limit_bytes=VMEM_LIMIT),
    )(w_gate_up)


def _moe_kernel(be_ref, nu_ref, x_ref, wg_ref, wl_ref, bg_ref, bl_ref, wd_ref, bd_ref, o_ref,
                wd_bf_ref):
    i = pl.program_id(0)
    new_expert = (i == 0) | (be_ref[i] != be_ref[jnp.maximum(i - 1, 0)])

    @pl.when(new_expert & (i < nu_ref[0]))
    def _():
        wd_bf_ref[...] = wd_ref[...].astype(wd_bf_ref.dtype)

    @pl.when(i < nu_ref[0])
    def _():
        lo, hi = _unpack_bf16_pairs(x_ref[...])
        x = jnp.concatenate([lo, hi], axis=1).astype(jnp.bfloat16)
        xg = jnp.dot(x, wg_ref[...], preferred_element_type=jnp.float32) + bg_ref[...]
        xl = jnp.dot(x, wl_ref[...], preferred_element_type=jnp.float32) + bl_ref[...]
        xg = jnp.minimum(xg, SWIGLU_LIMIT)
        xl = jnp.clip(xl, -SWIGLU_LIMIT, SWIGLU_LIMIT)
        act = xg * (1.0 / (1.0 + jnp.exp(-SWIGLU_ALPHA * xg))) * (xl + 1.0)
        out = jnp.dot(act.astype(jnp.bfloat16), wd_bf_ref[...],
                      preferred_element_type=jnp.float32) + bd_ref[...]
        o_ref[...] = _pack_bf16_pairs(out)

    @pl.when(i >= nu_ref[0])
    def _():
        o_ref[...] = jnp.zeros_like(o_ref)


def _moe(xs, blk_e, n_used, wg, wl, bg, bl, wd, bd):
    P, half = xs.shape
    D = 2 * half
    n_blocks = P // TM_MOE
    wspec = lambda k, n: pl.BlockSpec((None, k, n), lambda i, be, nu: (be[i], 0, 0))
    grid_spec = pltpu.PrefetchScalarGridSpec(
        num_scalar_prefetch=2,
        grid=(n_blocks,),
        in_specs=[pl.BlockSpec((TM_MOE, half), lambda i, be, nu: (i, 0)),
                  wspec(D, D_FF), wspec(D, D_FF), wspec(1, D_FF), wspec(1, D_FF),
                  wspec(D_FF, D), wspec(1, D)],
        out_specs=pl.BlockSpec((TM_MOE, half), lambda i, be, nu: (i, 0)),
        scratch_shapes=[pltpu.VMEM((D_FF, D), jnp.bfloat16)],
    )
    return pl.pallas_call(
        _moe_kernel,
        name="moe",
        grid_spec=grid_spec,
        out_shape=jax.ShapeDtypeStruct((P, half), jnp.uint32),
        compiler_params=pltpu.CompilerParams(dimension_semantics=("arbitrary",),
                                             vmem_limit_bytes=VMEM_LIMIT),
    )(blk_e, n_used, xs, wg, wl, bg, bl, wd, bd)


def _final_kernel(x1_ref, y0_ref, y1_ref, y2_ref, y3_ref, tg_ref, mod_ref, g_ref, o_ref):
    gates = tg_ref[...]
    half = y0_ref.shape[1]
    y_lo = jnp.zeros((x1_ref.shape[0], half), jnp.float32)
    y_hi = jnp.zeros((x1_ref.shape[0], half), jnp.float32)
    for kk, yk_ref in enumerate((y0_ref, y1_ref, y2_ref, y3_ref)):
        lo, hi = _unpack_bf16_pairs(yk_ref[...])
        y_lo = y_lo + gates[:, kk:kk + 1] * lo
        y_hi = y_hi + gates[:, kk:kk + 1] * hi
    y = jnp.concatenate([y_lo, y_hi], axis=1)
    gate2 = mod_ref[0, 5:6, :]
    o_ref[...] = x1_ref[...] + _rms(y, gate2 * g_ref[...])


def _final_part_kernel(x1_ref, y0_ref, y1_ref, y2_ref, y3_ref, tg_ref, mod_ref, g_ref, prev_ref,
                       o_ref):
    del prev_ref
    _final_kernel(x1_ref, y0_ref, y1_ref, y2_ref, y3_ref, tg_ref, mod_ref, g_ref, o_ref)


def _final(x1, yk, tg, mod3, g_post, seq_len, part, prev):
    T, D = x1.shape
    tp = yk.shape[0] // TOP_K
    nt = tp // TM_FIN
    t0 = part * nt
    batch = (part * tp) // seq_len
    yspec = lambda kk: pl.BlockSpec((TM_FIN, D // 2), lambda i: (kk * nt + i, 0))
    in_specs = ([pl.BlockSpec((TM_FIN, D), lambda i: (t0 + i, 0))]
                + [yspec(kk) for kk in range(TOP_K)]
                + [pl.BlockSpec((TM_FIN, LANES), lambda i: (t0 + i, 0)),
                   pl.BlockSpec((1, N_MOD, D), lambda i: (batch, 0, 0)),
                   pl.BlockSpec((1, D), lambda i: (0, 0))])
    args = [x1, yk, yk, yk, yk, tg, mod3, g_post]
    body, aliases = _final_kernel, {}
    if prev is not None:
        in_specs.append(pl.BlockSpec(memory_space=pl.ANY))
        args.append(prev)
        body, aliases = _final_part_kernel, {len(args) - 1: 0}
    return pl.pallas_call(
        body,
        name="final",
        grid=(nt,),
        in_specs=in_specs,
        out_specs=pl.BlockSpec((TM_FIN, D), lambda i: (t0 + i, 0)),
        out_shape=jax.ShapeDtypeStruct((T, D), jnp.float32),
        input_output_aliases=aliases,
        compiler_params=pltpu.CompilerParams(dimension_semantics=("arbitrary",),
                                             vmem_limit_bytes=VMEM_LIMIT),
    )(*args)


def _pack_w_in(w_in):
    half = ATT_HEAD_DIM // 2

    def pair_rotary_layout(w):
        k = w.shape[0]
        w = w.reshape(k, ATT_HEADS // 2, 2, 2, half)
        return w.transpose(0, 1, 3, 2, 4).reshape(k, ATT_WIDTH)

    lr = jnp.pad(w_in[:, 3072:3072 + GLA_GATE_RANK], ((0, 0), (0, LANES - GLA_GATE_RANK)))
    return jnp.concatenate([pair_rotary_layout(w_in[:, :ATT_WIDTH]),
                            pair_rotary_layout(w_in[:, ATT_WIDTH:2 * ATT_WIDTH]),
                            w_in[:, 1024:3072], lr], axis=1).astype(jnp.bfloat16)


def _rope_tables(seq_len):
    half = ATT_HEAD_DIM // 2
    inv_freq = ROPE_THETA ** (-jnp.arange(half, dtype=jnp.float32) / half)
    ang = jnp.arange(seq_len, dtype=jnp.float32)[:, None] * inv_freq[None, :]
    cos = jnp.tile(jnp.cos(ang), (1, LANES // half))
    sin = jnp.tile(jnp.sin(ang), (1, LANES // half))
    sign = jnp.where(jnp.arange(LANES) < LANES // 2, -1.0, 1.0)
    return cos, sin * sign


def _route_kernel(ti_ref, cnt_ref, pos_ref, blk_ref, nu_ref, base_ref, tri_ref):
    j = pl.program_id(0)
    tm = ti_ref.shape[0]
    lane = lax.broadcasted_iota(jnp.int32, (tm, LANES), 1)
    ti = ti_ref[...]
    hots = [lane == ti[:, k:k + 1] for k in range(TOP_K)]
    hot_all = jnp.zeros((tm, LANES), jnp.float32)
    for h in hots:
        hot_all = hot_all + jnp.where(h, 1.0, 0.0)
    tile_cnt = jnp.sum(hot_all, axis=0, keepdims=True)

    @pl.when(j == 0)
    def _():
        r = lax.broadcasted_iota(jnp.int32, (tm, tm), 0)
        c = lax.broadcasted_iota(jnp.int32, (tm, tm), 1)
        tri_ref[...] = jnp.where(c < r, 1.0, 0.0).astype(tri_ref.dtype)
        shift = TM_MOE.bit_length() - 1
        lane8 = lax.broadcasted_iota(jnp.int32, cnt_ref.shape, 1)
        cnt = cnt_ref[...].astype(jnp.int32)
        padded = ((cnt + (TM_MOE - 1)) >> shift) << shift
        pend = padded
        s = 1
        while s < N_EXPERTS:
            pend = pend + jnp.where(lane8 >= s, pltpu.roll(pend, s, axis=1), 0)
            s *= 2
        base_ref[...] = (pend - padded).astype(jnp.float32)
        nb = blk_ref.shape[0]
        blk_start = lax.broadcasted_iota(jnp.int32, (nb, LANES), 0) * TM_MOE
        lane_b = lax.broadcasted_iota(jnp.int32, (nb, LANES), 1)
        done = jnp.where((pend[0:1, :] <= blk_start) & (lane_b < N_EXPERTS), 1.0, 0.0)
        blk = jnp.minimum(jnp.sum(done, axis=1, keepdims=True), float(N_EXPERTS - 1))
        blk_ref[...] = jnp.broadcast_to(blk, blk_ref.shape).astype(jnp.int32)
        total = jnp.max(jnp.where(lane8 < N_EXPERTS, pend, 0).astype(jnp.float32),
                        axis=1, keepdims=True)
        nu_ref[...] = jnp.broadcast_to(total.astype(jnp.int32) >> shift, nu_ref.shape)

    before = jnp.dot(tri_ref[...], hot_all.astype(tri_ref.dtype),
                     preferred_element_type=jnp.float32)
    slot = before + base_ref[0:1, :]
    pos = jnp.zeros((tm, LANES), jnp.float32)
    for k, h in enumerate(hots):
        pk = jnp.sum(jnp.where(h, slot, 0.0), axis=1, keepdims=True)
        pos = jnp.where(lane == k, pk, pos)
    pos_ref[...] = pos.astype(jnp.int32)
    base_ref[...] += tile_cnt


def _route(ti, cnt, n_tokens):
    A = n_tokens * TOP_K
    n_blocks = -(-(A + N_EXPERTS * (TM_MOE - 1)) // TM_MOE)
    nb_pad = -(-n_blocks // 8) * 8
    nt = n_tokens // TM_ROUTE
    pos, blk, nu = pl.pallas_call(
        _route_kernel,
        name="route",
        grid=(nt,),
        in_specs=[pl.BlockSpec((TM_ROUTE, LANES), lambda j: (j, 0)),
                  pl.BlockSpec((8, LANES), lambda j: (0, 0))],
        out_specs=[pl.BlockSpec((TM_ROUTE, LANES), lambda j: (j, 0)),
                   pl.BlockSpec((nb_pad, LANES), lambda j: (0, 0)),
                   pl.BlockSpec((8, LANES), lambda j: (0, 0))],
        out_shape=[jax.ShapeDtypeStruct((n_tokens, LANES), jnp.int32),
                   jax.ShapeDtypeStruct((nb_pad, LANES), jnp.int32),
                   jax.ShapeDtypeStruct((8, LANES), jnp.int32)],
        scratch_shapes=[pltpu.VMEM((8, LANES), jnp.float32),
                        pltpu.VMEM((TM_ROUTE, TM_ROUTE), jnp.bfloat16)],
        compiler_params=pltpu.CompilerParams(dimension_semantics=("arbitrary",),
                                             vmem_limit_bytes=VMEM_LIMIT),
    )(ti, cnt)
    pos_kmajor = pos[:, :TOP_K].T
    return pos_kmajor, n_blocks * TM_MOE, blk[:n_blocks, 0], nu[0, :1]


def kernel(x, c, w_mod, b_mod, g_pre_mix, w_in, w_gate_lr, b_gate, g_gla, w_out, g_post_mix,
           g_pre_ffn, router_w, router_b, w_gate_up, b_gate_up, w_down, b_down, g_post_ffn):
    B, S, D = x.shape
    T = B * S
    bf = jnp.bfloat16
    cos_t, sin_t = _rope_tables(S)
    for l in range(w_mod.shape[0]):
        wg, wl = _split_gate_up(w_gate_up[l])
        wd = w_down[l]
        mod3 = _mod(c, w_mod[l], b_mod[l]).reshape(B, N_MOD, D)
        wlr = jnp.pad(w_gate_lr[l], ((0, LANES - GLA_GATE_RANK), (0, 0))).astype(bf)
        proj = _in_proj(x, mod3, g_pre_mix[l:l + 1], _pack_w_in(w_in[l]), cos_t, sin_t, wlr,
                        b_gate[l:l + 1])
        n_pat = len(DILATIONS)
        qa, ka, va = proj[:n_pat], proj[n_pat:2 * n_pat], proj[2 * n_pat:3 * n_pat]
        qg, kg, vg, sr, la = proj[3 * n_pat:]
        pats = [_attn_pattern(qa[j], ka[j], va[j], d) for j, d in enumerate(DILATIONS)]
        og = _gla(qg, kg, vg, la, sr, g_gla[l:l + 1])
        rw = jnp.pad(router_w[l], ((0, 0), (0, LANES - N_EXPERTS)))
        rb = jnp.pad(router_b[l], (0, LANES - N_EXPERTS), constant_values=NEG_BIG)[None]
        x1, h2, ti, tg, cnt = _out_proj([p[0] for p in pats], [p[1] for p in pats], og, x, mod3,
                                        w_out[l].astype(bf), g_post_mix[l:l + 1],
                                        g_pre_ffn[l:l + 1], rw, rb)
        pos_kmajor, n_slots, blk_e, n_used = _route(ti.reshape(T, LANES), cnt, T)
        xs = _scatter_rows(h2.reshape(T, D // 2), pos_kmajor, n_slots)
        out_buf = _moe(xs, blk_e, n_used, wg, wl,
                       b_gate_up[l][:, None, 0::2], b_gate_up[l][:, None, 1::2],
                       wd, b_down[l][:, None, :])
        out = None
        tp = T // FINAL_PARTS
        for part in range(FINAL_PARTS):
            yk = _gather_rows(out_buf, pos_kmajor[:, part * tp:(part + 1) * tp].reshape(-1))
            out = _final(x1.reshape(T, D), yk, tg.reshape(T, LANES), mod3, g_post_ffn[l:l + 1],
                         S, part, out)
        x = out.reshape(B, S, D)
    return x
```

```python
import functools

import numpy as np
import jax
import jax.numpy as jnp
from jax import lax
from jax.experimental import pallas as pl
from jax.experimental.pallas import tpu as pltpu
from jax.experimental.pallas import tpu_sc as plsc

D_MODEL = 1024
ATT_HEADS = 8
ATT_HEAD_DIM = 64
ATT_WIDTH = ATT_HEADS * ATT_HEAD_DIM
DILATIONS = (1, 4, 16)
ATT_SPAN = 128
ROPE_THETA = 10000.0
GLA_HEADS = 4
GLA_KEY_DIM = 64
GLA_VAL_DIM = 128
GLA_KEY_WIDTH = GLA_HEADS * GLA_KEY_DIM
GLA_VAL_WIDTH = GLA_HEADS * GLA_VAL_DIM
GLA_GATE_RANK = 16
GLA_GATE_NORMALIZER = 16.0
GLA_CHUNK = 64
N_EXPERTS = 32
TOP_K = 4
D_FF = D_MODEL
SWIGLU_LIMIT = 7.0
SWIGLU_ALPHA = 1.702
NORM_EPS = 1e-6
N_MOD = 6

LANES = 128
NEG_BIG = -1e30

TM_PROJ = 512
TQ_ATT = 1024
TG_GLA = 512
GLA_UNROLL = 8
TM_OUT = 512
OUT_SUBTILES = 1
TM_MOE = 512
TM_FIN = 1024
TN_MOD = 1024
TK_SPLIT = 512
TM_ROUTE = 1024
FINAL_PARTS = 4
SC_INDEX_WINDOW = 128
SC_GATHER_WINDOW = 64
VMEM_LIMIT = 48 * 1024 * 1024

_C_QA, _C_KA, _C_VA = 0, 512, 1024
_C_QG, _C_KG, _C_VG, _C_RG, _C_LR = 1536, 1792, 2048, 2560, 3072
D_IN_PACKED = 3200


def _rms(x, g):
    return x * lax.rsqrt(jnp.mean(x * x, axis=-1, keepdims=True) + NORM_EPS) * g


def _pack_bf16_pairs(x):
    n = x.shape[1] // 2
    u = lax.bitcast_convert_type(x.astype(jnp.bfloat16).astype(jnp.float32), jnp.uint32)
    return (u[:, :n] >> 16) | (u[:, n:] & jnp.uint32(0xFFFF0000))


def _unpack_bf16_pairs(w):
    lo = lax.bitcast_convert_type(w << 16, jnp.float32)
    hi = lax.bitcast_convert_type(w & jnp.uint32(0xFFFF0000), jnp.float32)
    return lo, hi


def _gather_rows(data, idx):
    n_rows = idx.shape[0]
    width = data.shape[1]
    mesh = plsc.VectorSubcoreMesh(core_axis_name="core", subcore_axis_name="subcore")
    n_workers = mesh.num_cores * mesh.num_subcores
    per_worker = n_rows // n_workers
    assert per_worker * n_workers == n_rows and per_worker % SC_INDEX_WINDOW == 0
    halves = SC_INDEX_WINDOW // SC_GATHER_WINDOW

    @pl.kernel(out_type=jax.ShapeDtypeStruct((n_rows, width), data.dtype), mesh=mesh,
               name="gather_rows",
               scratch_types=[pltpu.VMEM((1, SC_INDEX_WINDOW), jnp.int32),
                              pltpu.VMEM((SC_GATHER_WINDOW, width), data.dtype)])
    def gather(x_hbm, i_hbm, o_hbm, idx_vmem, rows_vmem):
        worker = lax.axis_index("core") * mesh.num_subcores + lax.axis_index("subcore")
        base = worker * per_worker

        @pl.loop(0, per_worker // SC_INDEX_WINDOW)
        def _(j):
            off = base + j * SC_INDEX_WINDOW
            pltpu.sync_copy(i_hbm.at[:, pl.ds(off, SC_INDEX_WINDOW)], idx_vmem)
            for h in range(halves):
                part = idx_vmem.at[0, pl.ds(h * SC_GATHER_WINDOW, SC_GATHER_WINDOW)]
                pltpu.sync_copy(x_hbm.at[part], rows_vmem)
                pltpu.sync_copy(rows_vmem,
                                o_hbm.at[pl.ds(off + h * SC_GATHER_WINDOW, SC_GATHER_WINDOW)])

    return gather(data, idx.reshape(1, n_rows))


def _scatter_rows(data, idx, n_out):
    n_copies, n_rows = idx.shape
    width = data.shape[1]
    mesh = plsc.VectorSubcoreMesh(core_axis_name="core", subcore_axis_name="subcore")
    n_workers = mesh.num_cores * mesh.num_subcores
    per_worker = n_rows // n_workers
    assert per_worker * n_workers == n_rows and per_worker % SC_INDEX_WINDOW == 0
    halves = SC_INDEX_WINDOW // SC_GATHER_WINDOW

    @pl.kernel(out_type=jax.ShapeDtypeStruct((n_out, width), data.dtype), mesh=mesh,
               name="scatter_rows",
               scratch_types=[pltpu.VMEM((n_copies, SC_INDEX_WINDOW), jnp.int32),
                              pltpu.VMEM((SC_GATHER_WINDOW, width), data.dtype)])
    def scatter(x_hbm, i_hbm, o_hbm, idx_vmem, rows_vmem):
        worker = lax.axis_index("core") * mesh.num_subcores + lax.axis_index("subcore")
        base = worker * per_worker

        @pl.loop(0, per_worker // SC_INDEX_WINDOW)
        def _(j):
            off = base + j * SC_INDEX_WINDOW
            pltpu.sync_copy(i_hbm.at[:, pl.ds(off, SC_INDEX_WINDOW)], idx_vmem)
            for h in range(halves):
                pltpu.sync_copy(x_hbm.at[pl.ds(off + h * SC_GATHER_WINDOW, SC_GATHER_WINDOW)],
                                rows_vmem)
                for k in range(n_copies):
                    part = idx_vmem.at[k, pl.ds(h * SC_GATHER_WINDOW, SC_GATHER_WINDOW)]
                    pltpu.sync_copy(rows_vmem, o_hbm.at[part])

    return scatter(data, idx)


def _mod_kernel(ct_ref, w_ref, b_ref, o_ref):
    ct = ct_ref[...]
    s = ct * (1.0 / (1.0 + jnp.exp(-ct)))
    w = w_ref[...]
    rows = [jnp.sum(s[:, b:b + 1] * w, axis=0, keepdims=True) for b in range(ct.shape[1])]
    o_ref[...] = jnp.concatenate(rows, axis=0) + b_ref[...]


def _mod(c, w_mod, b_mod):
    B = c.shape[0]
    n = w_mod.shape[1]
    return pl.pallas_call(
        _mod_kernel,
        name="mod",
        grid=(n // TN_MOD,),
        in_specs=[pl.BlockSpec((D_MODEL, B), lambda j: (0, 0)),
                  pl.BlockSpec((D_MODEL, TN_MOD), lambda j: (0, j)),
                  pl.BlockSpec((1, TN_MOD), lambda j: (0, j))],
        out_specs=pl.BlockSpec((B, TN_MOD), lambda j: (0, j)),
        out_shape=jax.ShapeDtypeStruct((B, n), jnp.float32),
        compiler_params=pltpu.CompilerParams(dimension_semantics=("arbitrary",),
                                             vmem_limit_bytes=VMEM_LIMIT),
    )(c.T, w_mod, b_mod.reshape(1, n))


def _store_residue_views(scr_ref, scr4_ref, out_refs):
    ns, tm, _ = scr_ref.shape
    w = ns * LANES
    o1_ref, o4_ref, o16_ref = out_refs
    assert DILATIONS == (1, 4, 16)
    q4 = tm // 4
    for s in range(ns):
        o1_ref[0, :, s * LANES:(s + 1) * LANES] = scr_ref[s].astype(o1_ref.dtype)
    for r in range(4):
        for s in range(ns):
            vals = scr_ref[s, pl.ds(r, q4, stride=4), :]
            scr4_ref[s, r * q4:(r + 1) * q4, :] = vals
            c0 = r * w + s * LANES
            o4_ref[0, :, c0:c0 + LANES] = vals.astype(o4_ref.dtype)
    for r in range(16):
        lo, hi = r % 4, r // 4
        for s in range(ns):
            vals = scr4_ref[s, pl.ds(lo * q4 + hi, tm // 16, stride=4), :]
            c0 = r * w + s * LANES
            o16_ref[0, :, c0:c0 + LANES] = vals.astype(o16_ref.dtype)


def _in_proj_kernel(x_ref, mod_ref, g_ref, w_ref, cos_ref, sin_ref, wlr_ref, bg_ref,
                    qa1_ref, qa4_ref, qa16_ref, ka1_ref, ka4_ref, ka16_ref,
                    va1_ref, va4_ref, va16_ref, qg_ref, kg_ref, vg_ref, sr_ref, la_ref,
                    qs_ref, ks_ref, vs_ref, q4_ref, k4_ref, v4_ref):
    x = x_ref[0]
    shift = mod_ref[0, 0:1, :]
    scale = mod_ref[0, 1:2, :]
    h = (_rms(x, g_ref[...] * (1.0 + scale)) + shift).astype(jnp.bfloat16)

    def proj(c0, width):
        return jnp.dot(h, w_ref[:, c0:c0 + width], preferred_element_type=jnp.float32)

    cos = cos_ref[...]
    sin = sin_ref[...]

    def rope(c0, scr_ref, mult):
        p = proj(c0, ATT_WIDTH)
        for s in range(ATT_WIDTH // LANES):
            t = p[:, s * LANES:(s + 1) * LANES]
            r = t * cos + pltpu.roll(t, LANES // 2, axis=1) * sin
            if mult != 1.0:
                r = r * mult
            scr_ref[s] = r

    rope(_C_QA, qs_ref, ATT_HEAD_DIM ** -0.5)
    _store_residue_views(qs_ref, q4_ref, (qa1_ref, qa4_ref, qa16_ref))
    rope(_C_KA, ks_ref, 1.0)
    _store_residue_views(ks_ref, k4_ref, (ka1_ref, ka4_ref, ka16_ref))
    pv = proj(_C_VA, ATT_WIDTH)
    for s in range(ATT_WIDTH // LANES):
        vs_ref[s] = pv[:, s * LANES:(s + 1) * LANES]
    _store_residue_views(vs_ref, v4_ref, (va1_ref, va4_ref, va16_ref))
    qg_ref[0] = proj(_C_QG, GLA_KEY_WIDTH).astype(qg_ref.dtype)
    kg_ref[0] = proj(_C_KG, GLA_KEY_WIDTH).astype(kg_ref.dtype)
    vg_ref[0] = proj(_C_VG, GLA_VAL_WIDTH).astype(vg_ref.dtype)
    r = proj(_C_RG, GLA_VAL_WIDTH)
    sr_ref[0] = (r * (1.0 / (1.0 + jnp.exp(-r)))).astype(sr_ref.dtype)
    lr = proj(_C_LR, LANES).astype(jnp.bfloat16)
    z = jnp.dot(lr, wlr_ref[...], preferred_element_type=jnp.float32) + bg_ref[...]
    log_sig = jnp.minimum(z, 0.0) - jnp.log(1.0 + jnp.exp(-jnp.abs(z)))
    la_ref[0] = log_sig * (1.0 / GLA_GATE_NORMALIZER)


def _in_proj(x, mod3, g_pre, w_packed, cos_t, sin_t, wlr, bg):
    B, S, D = x.shape
    nb = S // TM_PROJ
    tok = lambda w: pl.BlockSpec((1, TM_PROJ, w), lambda b, i: (b, i, 0))
    const = lambda shape: pl.BlockSpec(shape, lambda b, i: tuple(0 for _ in shape))
    bf = jnp.bfloat16
    att_shapes = [jax.ShapeDtypeStruct((B, S // d, d * ATT_WIDTH), bf) for d in DILATIONS] * 3
    att_specs = [pl.BlockSpec((1, TM_PROJ // d, d * ATT_WIDTH), lambda b, i: (b, i, 0))
                 for d in DILATIONS] * 3
    gla_shapes = [jax.ShapeDtypeStruct((B, S, w), dt) for w, dt in
                  [(GLA_KEY_WIDTH, bf), (GLA_KEY_WIDTH, bf),
                   (GLA_VAL_WIDTH, bf), (GLA_VAL_WIDTH, bf), (GLA_KEY_WIDTH, jnp.float32)]]
    return pl.pallas_call(
        _in_proj_kernel,
        name="in_proj",
        grid=(B, nb),
        in_specs=[tok(D),
                  pl.BlockSpec((1, N_MOD, D), lambda b, i: (b, 0, 0)),
                  const((1, D)),
                  const((D, D_IN_PACKED)),
                  pl.BlockSpec((TM_PROJ, LANES), lambda b, i: (i, 0)),
                  pl.BlockSpec((TM_PROJ, LANES), lambda b, i: (i, 0)),
                  const((LANES, GLA_KEY_WIDTH)),
                  const((1, GLA_KEY_WIDTH))],
        out_specs=att_specs + [tok(s.shape[-1]) for s in gla_shapes],
        out_shape=att_shapes + gla_shapes,
        scratch_shapes=[pltpu.VMEM((ATT_WIDTH // LANES, TM_PROJ, LANES), jnp.float32)] * 6,
        compiler_params=pltpu.CompilerParams(dimension_semantics=("arbitrary", "arbitrary"),
                                             vmem_limit_bytes=VMEM_LIMIT),
    )(x, mod3, g_pre, w_packed, cos_t, sin_t, wlr, bg)


def _attn_kernel(q_ref, kp_ref, kc_ref, vp_ref, vc_ref, o_ref, st_ref):
    i = pl.program_id(2)
    tp = kp_ref.shape[1]
    tq = tp
    n_sub = q_ref.shape[1] // tq
    row = lax.broadcasted_iota(jnp.int32, (tq, tp + tq), 0)
    col = lax.broadcasted_iota(jnp.int32, (tq, tp + tq), 1)
    band = (col >= row) & (col <= row + ATT_SPAN)
    bias_inner = jnp.where(band, 0.0, NEG_BIG)
    bias_first = jnp.where(band & ((col >= tp) | (i > 0)), 0.0, NEG_BIG)
    lane = lax.broadcasted_iota(jnp.int32, (tq, LANES), 1)
    q_head = (lane >> 5) & 1
    for sb in range(n_sub):
        rows = slice(sb * tq, (sb + 1) * tq)
        bias = bias_first if sb == 0 else bias_inner
        stat_m = jnp.zeros((tq, LANES), jnp.float32)
        stat_l = jnp.ones((tq, LANES), jnp.float32)
        for hp in range(ATT_HEADS // 2):
            sl = slice(hp * LANES, (hp + 1) * LANES)
            q2 = q_ref[0, rows, sl]
            if sb == 0:
                kcat = jnp.concatenate([kp_ref[0, :, sl], kc_ref[0, rows, sl]], axis=0)
                vcat = jnp.concatenate([vp_ref[0, :, sl], vc_ref[0, rows, sl]], axis=0)
            else:
                kcat = kc_ref[0, (sb - 1) * tq:(sb + 1) * tq, sl]
                vcat = vc_ref[0, (sb - 1) * tq:(sb + 1) * tq, sl]
            outs, dens = [], []
            for hh in range(2):
                qm = jnp.where(q_head == hh, q2, jnp.zeros_like(q2))
                s = lax.dot_general(qm, kcat, (((1,), (1,)), ((), ())),
                                    preferred_element_type=jnp.float32) + bias
                m = jnp.max(s, axis=1, keepdims=True)
                p = jnp.exp(s - m)
                l = jnp.sum(p, axis=1, keepdims=True)
                outs.append(jnp.dot(p.astype(vcat.dtype), vcat, preferred_element_type=jnp.float32))
                dens.append(l)
                stat_m = jnp.where(lane == 2 * hp + hh, m, stat_m)
                stat_l = jnp.where(lane == 2 * hp + hh, l, stat_l)
            first = lane < ATT_HEAD_DIM
            o_pair = jnp.where(first, outs[0], outs[1]) / jnp.where(first, dens[0], dens[1])
            o_ref[0, rows, sl] = o_pair.astype(o_ref.dtype)
        st_ref[0, rows, :] = stat_m + jnp.log(stat_l)


def _attn_pattern(qv, kv, vv, dil):
    B, L, _ = qv.shape
    W = ATT_WIDTH
    tq = min(TQ_ATT, L)
    nq = L // tq
    cur = pl.BlockSpec((1, tq, W), lambda b, r, i: (b, i, r))
    back = tq // ATT_SPAN
    prev = pl.BlockSpec((1, ATT_SPAN, W), lambda b, r, i: (b, jnp.maximum(i * back - 1, 0), r))
    o, st = pl.pallas_call(
        _attn_kernel,
        name=f"attn_d{dil}",
        grid=(B, dil, nq),
        in_specs=[cur, prev, cur, prev, cur],
        out_specs=[cur, pl.BlockSpec((1, tq, LANES), lambda b, r, i: (b, i, r))],
        out_shape=[jax.ShapeDtypeStruct((B, L, dil * W), jnp.bfloat16),
                   jax.ShapeDtypeStruct((B, L, dil * LANES), jnp.float32)],
        compiler_params=pltpu.CompilerParams(
            dimension_semantics=("arbitrary", "arbitrary", "arbitrary"),
            vmem_limit_bytes=VMEM_LIMIT),
    )(qv, kv, kv, vv, vv)
    return o, st


def _cumsum_rows(x):
    n = x.shape[0]
    row = lax.broadcasted_iota(jnp.int32, x.shape, 0)
    s = 1
    while s < n:
        x = x + jnp.where(row >= s, pltpu.roll(x, s, axis=0), 0.0)
        s *= 2
    return x


def _gla_kernel(q_ref, k_ref, v_ref, la_ref, sr_ref, g_ref, o_ref, st_ref):
    @pl.when(pl.program_id(0) == 0)
    def _():
        st_ref[...] = jnp.zeros_like(st_ref)

    C = GLA_CHUNK
    n_batch = q_ref.shape[0]
    n_chunks = q_ref.shape[1] // C
    lane = lax.broadcasted_iota(jnp.int32, (C, LANES), 1)
    lane_sq = lax.broadcasted_iota(jnp.int32, (LANES, LANES), 1)
    tril = (lax.broadcasted_iota(jnp.int32, (C, C), 0)
            >= lax.broadcasted_iota(jnp.int32, (C, C), 1))
    g = g_ref[...]
    bf = jnp.bfloat16

    def chunk(c, carry):
        r0 = pl.multiple_of(c * C, C)
        rows = pl.ds(r0, C)
        for n in range(n_batch):
            for p in range(GLA_HEADS // 2):
                ksl = slice(p * LANES, (p + 1) * LANES)
                b = _cumsum_rows(la_ref[n, rows, ksl])
                b_last = b[C - 1:C, :]
                q = q_ref[n, rows, ksl].astype(jnp.float32) * (GLA_KEY_DIM ** -0.5)
                k = k_ref[n, rows, ksl].astype(jnp.float32)
                q_dec = q * jnp.exp(b)
                k_inv = (k * jnp.exp(-b)).astype(bf)
                k_dec = (k * jnp.exp(b_last - b)).astype(bf)
                st = st_ref[n, p]
                st_b = st.astype(bf)
                ut = []
                for hh in range(2):
                    h = 2 * p + hh
                    vsl = slice(h * GLA_VAL_DIM, (h + 1) * GLA_VAL_DIM)
                    v = v_ref[n, rows, vsl]
                    own = (lane >= GLA_KEY_DIM) if hh else (lane < GLA_KEY_DIM)
                    qm = jnp.where(own, q_dec, 0.0).astype(bf)
                    att = lax.dot_general(qm, k_inv, (((1,), (1,)), ((), ())),
                                          preferred_element_type=jnp.float32)
                    att = jnp.where(tril, att, 0.0).astype(bf)
                    o = jnp.dot(att, v, preferred_element_type=jnp.float32)
                    o = o + lax.dot_general(qm, st_b, (((1,), (1,)), ((), ())),
                                            preferred_element_type=jnp.float32)
                    o = _rms(o, g) * sr_ref[n, rows, vsl].astype(jnp.float32)
                    o_ref[n, rows, vsl] = o.astype(o_ref.dtype)
                    ut.append(lax.dot_general(v, k_dec, (((0,), (0,)), ((), ())),
                                              preferred_element_type=jnp.float32))
                st_ref[n, p] = (st * jnp.exp(b_last)
                                + jnp.where(lane_sq < GLA_KEY_DIM, ut[0], ut[1]))
        return carry

    lax.fori_loop(0, n_chunks, chunk, 0, unroll=GLA_UNROLL)


def _gla(qg, kg, vg, la, sr, g_gla):
    B, S, _ = qg.shape
    tok = lambda w: pl.BlockSpec((B, TG_GLA, w), lambda i: (0, i, 0))
    return pl.pallas_call(
        _gla_kernel,
        name="gla",
        grid=(S // TG_GLA,),
        in_specs=[tok(GLA_KEY_WIDTH), tok(GLA_KEY_WIDTH), tok(GLA_VAL_WIDTH),
                  tok(GLA_KEY_WIDTH), tok(GLA_VAL_WIDTH),
                  pl.BlockSpec((1, GLA_VAL_DIM), lambda i: (0, 0))],
        out_specs=tok(GLA_VAL_WIDTH),
        out_shape=jax.ShapeDtypeStruct((B, S, GLA_VAL_WIDTH), jnp.bfloat16),
        scratch_shapes=[pltpu.VMEM((B, GLA_HEADS // 2, GLA_VAL_DIM, LANES), jnp.float32)],
        compiler_params=pltpu.CompilerParams(dimension_semantics=("arbitrary",),
                                             vmem_limit_bytes=VMEM_LIMIT),
    )(qg, kg, vg, la, sr, g_gla)


def _out_proj_kernel(o1_ref, o2_ref, o3_ref, s1_ref, s2_ref, s3_ref, og_ref, x_ref, mod_ref,
                     wout_ref, gpost_ref, gpre_ref, rw_ref, rb_ref,
                     x1_ref, h2_ref, ti_ref, tg_ref, cnt_ref, oscr_ref, sscr_ref, otmp_ref,
                     stmp_ref):
    tm = x_ref.shape[1]

    @pl.when((pl.program_id(0) == 0) & (pl.program_id(1) == 0))
    def _():
        cnt_ref[...] = jnp.zeros_like(cnt_ref)

    ns = ATT_WIDTH // LANES
    q4 = tm // 4
    for r in range(4):
        rows = pl.ds(r, q4, stride=4)
        for s in range(ns):
            c0 = r * ATT_WIDTH + s * LANES
            oscr_ref[0, s, rows, :] = o2_ref[0, :, c0:c0 + LANES].astype(jnp.float32)
        sscr_ref[0, rows, :] = s2_ref[0, :, r * LANES:(r + 1) * LANES]
    for r in range(16):
        lo, hi = r % 4, r // 4
        rows = pl.ds(lo * q4 + hi, tm // 16, stride=4)
        for s in range(ns):
            c0 = r * ATT_WIDTH + s * LANES
            otmp_ref[s, rows, :] = o3_ref[0, :, c0:c0 + LANES].astype(jnp.float32)
        stmp_ref[rows, :] = s3_ref[0, :, r * LANES:(r + 1) * LANES]
    for r in range(4):
        rows = pl.ds(r, q4, stride=4)
        for s in range(ns):
            oscr_ref[1, s, rows, :] = otmp_ref[s, r * q4:(r + 1) * q4, :]
        sscr_ref[1, rows, :] = stmp_ref[r * q4:(r + 1) * q4, :]
    gate1 = mod_ref[0, 2:3, :]
    shift2 = mod_ref[0, 3:4, :]
    scale2 = mod_ref[0, 4:5, :]
    bf = jnp.bfloat16
    rw = rw_ref[...]
    w_hi = rw.astype(bf)
    w_lo = (rw - w_hi.astype(jnp.float32)).astype(bf)
    ts = tm // OUT_SUBTILES
    lane = lax.broadcasted_iota(jnp.int32, (ts, LANES), 1)
    lane_f = lane.astype(jnp.float32)
    for sub in range(OUT_SUBTILES):
        rs = slice(sub * ts, (sub + 1) * ts)
        lses = [s1_ref[0, rs, :], sscr_ref[0, rs, :], sscr_ref[1, rs, :]]
        m = jnp.maximum(jnp.maximum(lses[0], lses[1]), lses[2])
        es = [jnp.exp(t - m) for t in lses]
        inv = 1.0 / (es[0] + es[1] + es[2])
        ws = [e * inv for e in es]
        pairs = []
        for hp in range(ATT_HEADS // 2):
            sl = slice(hp * LANES, (hp + 1) * LANES)
            o_pats = [o1_ref[0, rs, sl].astype(jnp.float32), oscr_ref[0, hp, rs, :],
                      oscr_ref[1, hp, rs, :]]
            acc = jnp.zeros((ts, LANES), jnp.float32)
            head_of_lane = 2 * hp + (lane >> 6)
            for w, o in zip(ws, o_pats):
                acc = acc + jnp.take_along_axis(w, head_of_lane, axis=1) * o
            pairs.append(acc.astype(bf))
        mixed = jnp.concatenate(pairs + [og_ref[0, rs, :]], axis=1)
        y = jnp.dot(mixed, wout_ref[...], preferred_element_type=jnp.float32)
        x1 = x_ref[0, rs, :] + _rms(y, gate1 * gpost_ref[...])
        x1_ref[0, rs, :] = x1
        h2 = _rms(x1, gpre_ref[...] * (1.0 + scale2)) + shift2
        h2_ref[0, rs, :] = _pack_bf16_pairs(h2)
        h_hi = h2.astype(bf)
        h_lo = (h2 - h_hi.astype(jnp.float32)).astype(bf)
        logits = (jnp.dot(h_hi, w_hi, preferred_element_type=jnp.float32)
                  + jnp.dot(h_lo, w_hi, preferred_element_type=jnp.float32)
                  + jnp.dot(h_hi, w_lo, preferred_element_type=jnp.float32)) + rb_ref[...]
        vals, idxs = [], []
        for _ in range(TOP_K):
            mk = jnp.max(logits, axis=1, keepdims=True)
            ik = jnp.min(jnp.where(logits == mk, lane_f, float(LANES)), axis=1, keepdims=True)
            logits = jnp.where(lane_f == ik, -jnp.inf, logits)
            vals.append(mk)
            idxs.append(ik)
        ex = [jnp.exp(v - vals[0]) for v in vals]
        den = ex[0] + ex[1] + ex[2] + ex[3]
        ti = jnp.zeros((ts, LANES), jnp.float32)
        tg = jnp.zeros((ts, LANES), jnp.float32)
        for kk in range(TOP_K):
            ti = jnp.where(lane == kk, idxs[kk], ti)
            tg = jnp.where(lane == kk, ex[kk] / den, tg)
        ti_ref[0, rs, :] = ti.astype(jnp.int32)
        tg_ref[0, rs, :] = tg
        chosen = jnp.zeros((ts, LANES), jnp.float32)
        for ik in idxs:
            chosen = chosen + jnp.where(lane_f == ik, 1.0, 0.0)
        cnt_ref[...] += jnp.sum(chosen, axis=0, keepdims=True)


def _out_proj(o_pats, st_pats, og, x, mod3, w_out, g_post, g_pre, rw, rb):
    B, S, D = x.shape
    tok = lambda w: pl.BlockSpec((1, TM_OUT, w), lambda b, i: (b, i, 0))
    const = lambda shape: pl.BlockSpec(shape, lambda b, i: tuple(0 for _ in shape))
    out_shapes = [jax.ShapeDtypeStruct((B, S, D), jnp.float32),
                  jax.ShapeDtypeStruct((B, S, D // 2), jnp.uint32),
                  jax.ShapeDtypeStruct((B, S, LANES), jnp.int32),
                  jax.ShapeDtypeStruct((B, S, LANES), jnp.float32),
                  jax.ShapeDtypeStruct((8, LANES), jnp.float32)]
    return pl.pallas_call(
        _out_proj_kernel,
        name="out_proj",
        grid=(B, S // TM_OUT),
        in_specs=[pl.BlockSpec((1, TM_OUT // d, d * ATT_WIDTH), lambda b, i: (b, i, 0))
                  for d in DILATIONS]
                 + [pl.BlockSpec((1, TM_OUT // d, d * LANES), lambda b, i: (b, i, 0))
                    for d in DILATIONS]
                 + [tok(GLA_VAL_WIDTH), tok(D),
                  pl.BlockSpec((1, N_MOD, D), lambda b, i: (b, 0, 0)),
                  const((D, D)), const((1, D)), const((1, D)),
                  const((D, LANES)), const((1, LANES))],
        out_specs=[tok(D), tok(D // 2), tok(LANES), tok(LANES), const((8, LANES))],
        out_shape=out_shapes,
        scratch_shapes=[pltpu.VMEM((2, ATT_WIDTH // LANES, TM_OUT, LANES), jnp.float32),
                        pltpu.VMEM((2, TM_OUT, LANES), jnp.float32),
                        pltpu.VMEM((ATT_WIDTH // LANES, TM_OUT, LANES), jnp.float32),
                        pltpu.VMEM((TM_OUT, LANES), jnp.float32)],
        compiler_params=pltpu.CompilerParams(dimension_semantics=("arbitrary", "arbitrary"),
                                             vmem_limit_bytes=VMEM_LIMIT),
    )(*o_pats, *st_pats, og, x, mod3, w_out, g_post, g_pre, rw, rb)


def _split_gate_up_kernel(w_ref, wg_ref, wl_ref):
    group = 2 * LANES
    src = lax.broadcasted_iota(jnp.int32, (group, group), 0)
    dst = lax.broadcasted_iota(jnp.int32, (group, group), 1)
    want = jnp.where(dst < LANES, 2 * dst, 2 * (dst - LANES) + 1)
    perm = jnp.where(src == want, 1.0, 0.0).astype(jnp.bfloat16)
    for j in range(w_ref.shape[1] // group):
        t = jnp.dot(w_ref[:, j * group:(j + 1) * group].astype(jnp.bfloat16), perm,
                    preferred_element_type=jnp.float32)
        wg_ref[:, j * LANES:(j + 1) * LANES] = t[:, :LANES].astype(wg_ref.dtype)
        wl_ref[:, j * LANES:(j + 1) * LANES] = t[:, LANES:].astype(wl_ref.dtype)


def _split_gate_up(w_gate_up):
    E, K, N2 = w_gate_up.shape
    out = jax.ShapeDtypeStruct((E, K, N2 // 2), jnp.bfloat16)
    ospec = pl.BlockSpec((None, TK_SPLIT, N2 // 2), lambda e, i: (e, i, 0))
    return pl.pallas_call(
        _split_gate_up_kernel,
        name="split_gate_up",
        grid=(E, K // TK_SPLIT),
        in_specs=[pl.BlockSpec((None, TK_SPLIT, N2), lambda e, i: (e, i, 0))],
        out_specs=[ospec, ospec],
        out_shape=[out, out],
        compiler_params=pltpu.CompilerParams(dimension_semantics=("arbitrary", "arbitrary"),
                                             vmem_limit_bytes=VMEM_LIMIT),
    )(w_gate_up)


def _moe_kernel(be_ref, nu_ref, x_ref, wg_ref, wl_ref, bg_ref, bl_ref, wd_ref, bd_ref, o_ref,
                wd_bf_ref):
    i = pl.program_id(0)
    new_expert = (i == 0) | (be_ref[i] != be_ref[jnp.maximum(i - 1, 0)])

    @pl.when(new_expert & (i < nu_ref[0]))
    def _():
        wd_bf_ref[...] = wd_ref[...].astype(wd_bf_ref.dtype)

    @pl.when(i < nu_ref[0])
    def _():
        lo, hi = _unpack_bf16_pairs(x_ref[...])
        x = jnp.concatenate([lo, hi], axis=1).astype(jnp.bfloat16)
        xg = jnp.dot(x, wg_ref[...], preferred_element_type=jnp.float32) + bg_ref[...]
        xl = jnp.dot(x, wl_ref[...], preferred_element_type=jnp.float32) + bl_ref[...]
        xg = jnp.minimum(xg, SWIGLU_LIMIT)
        xl = jnp.clip(xl, -SWIGLU_LIMIT, SWIGLU_LIMIT)
        act = xg * (1.0 / (1.0 + jnp.exp(-SWIGLU_ALPHA * xg))) * (xl + 1.0)
        out = jnp.dot(act.astype(jnp.bfloat16), wd_bf_ref[...],
                      preferred_element_type=jnp.float32) + bd_ref[...]
        o_ref[...] = _pack_bf16_pairs(out)


def _moe(xs, blk_e, n_used, wg, wl, bg, bl, wd, bd):
    P, half = xs.shape
    D = 2 * half
    n_blocks = P // TM_MOE
    used = lambda i, nu: jnp.minimum(i, nu[0] - 1)
    wspec = lambda k, n: pl.BlockSpec((None, k, n), lambda i, be, nu: (be[used(i, nu)], 0, 0))
    grid_spec = pltpu.PrefetchScalarGridSpec(
        num_scalar_prefetch=2,
        grid=(n_blocks,),
        in_specs=[pl.BlockSpec((TM_MOE, half), lambda i, be, nu: (used(i, nu), 0)),
                  wspec(D, D_FF), wspec(D, D_FF), wspec(1, D_FF), wspec(1, D_FF),
                  wspec(D_FF, D), wspec(1, D)],
        out_specs=pl.BlockSpec((TM_MOE, half), lambda i, be, nu: (used(i, nu), 0)),
        scratch_shapes=[pltpu.VMEM((D_FF, D), jnp.bfloat16)],
    )
    return pl.pallas_call(
        _moe_kernel,
        name="moe",
        grid_spec=grid_spec,
        out_shape=jax.ShapeDtypeStruct((P, half), jnp.uint32),
        compiler_params=pltpu.CompilerParams(dimension_semantics=("arbitrary",),
                                             vmem_limit_bytes=VMEM_LIMIT),
    )(blk_e, n_used, xs, wg, wl, bg, bl, wd, bd)


def _final_kernel(x1_ref, y0_ref, y1_ref, y2_ref, y3_ref, tg_ref, mod_ref, g_ref, o_ref):
    gates = tg_ref[...]
    half = y0_ref.shape[1]
    y_lo = jnp.zeros((x1_ref.shape[0], half), jnp.float32)
    y_hi = jnp.zeros((x1_ref.shape[0], half), jnp.float32)
    for kk, yk_ref in enumerate((y0_ref, y1_ref, y2_ref, y3_ref)):
        lo, hi = _unpack_bf16_pairs(yk_ref[...])
        y_lo = y_lo + gates[:, kk:kk + 1] * lo
        y_hi = y_hi + gates[:, kk:kk + 1] * hi
    y = jnp.concatenate([y_lo, y_hi], axis=1)
    gate2 = mod_ref[0, 5:6, :]
    o_ref[...] = x1_ref[...] + _rms(y, gate2 * g_ref[...])


def _final_part_kernel(x1_ref, y0_ref, y1_ref, y2_ref, y3_ref, tg_ref, mod_ref, g_ref, prev_ref,
                       o_ref):
    del prev_ref
    _final_kernel(x1_ref, y0_ref, y1_ref, y2_ref, y3_ref, tg_ref, mod_ref, g_ref, o_ref)


def _final(x1, yk, tg, mod3, g_post, seq_len, part, prev):
    T, D = x1.shape
    tp = yk.shape[0] // TOP_K
    nt = tp // TM_FIN
    t0 = part * nt
    batch = (part * tp) // seq_len
    yspec = lambda kk: pl.BlockSpec((TM_FIN, D // 2), lambda i: (kk * nt + i, 0))
    in_specs = ([pl.BlockSpec((TM_FIN, D), lambda i: (t0 + i, 0))]
                + [yspec(kk) for kk in range(TOP_K)]
                + [pl.BlockSpec((TM_FIN, LANES), lambda i: (t0 + i, 0)),
                   pl.BlockSpec((1, N_MOD, D), lambda i: (batch, 0, 0)),
                   pl.BlockSpec((1, D), lambda i: (0, 0))])
    args = [x1, yk, yk, yk, yk, tg, mod3, g_post]
    body, aliases = _final_kernel, {}
    if prev is not None:
        in_specs.append(pl.BlockSpec(memory_space=pl.ANY))
        args.append(prev)
        body, aliases = _final_part_kernel, {len(args) - 1: 0}
    return pl.pallas_call(
        body,
        name="final",
        grid=(nt,),
        in_specs=in_specs,
        out_specs=pl.BlockSpec((TM_FIN, D), lambda i: (t0 + i, 0)),
        out_shape=jax.ShapeDtypeStruct((T, D), jnp.float32),
        input_output_aliases=aliases,
        compiler_params=pltpu.CompilerParams(dimension_semantics=("arbitrary",),
                                             vmem_limit_bytes=VMEM_LIMIT),
    )(*args)


def _pack_w_in(w_in):
    half = ATT_HEAD_DIM // 2

    def pair_rotary_layout(w):
        k = w.shape[0]
        w = w.reshape(k, ATT_HEADS // 2, 2, 2, half)
        return w.transpose(0, 1, 3, 2, 4).reshape(k, ATT_WIDTH)

    lr = jnp.pad(w_in[:, 3072:3072 + GLA_GATE_RANK], ((0, 0), (0, LANES - GLA_GATE_RANK)))
    return jnp.concatenate([pair_rotary_layout(w_in[:, :ATT_WIDTH]),
                            pair_rotary_layout(w_in[:, ATT_WIDTH:2 * ATT_WIDTH]),
                            w_in[:, 1024:3072], lr], axis=1).astype(jnp.bfloat16)


def _rope_tables(seq_len):
    half = ATT_HEAD_DIM // 2
    inv_freq = ROPE_THETA ** (-jnp.arange(half, dtype=jnp.float32) / half)
    ang = jnp.arange(seq_len, dtype=jnp.float32)[:, None] * inv_freq[None, :]
    cos = jnp.tile(jnp.cos(ang), (1, LANES // half))
    sin = jnp.tile(jnp.sin(ang), (1, LANES // half))
    sign = jnp.where(jnp.arange(LANES) < LANES // 2, -1.0, 1.0)
    return cos, sin * sign


def _route_kernel(ti_ref, cnt_ref, pos_ref, blk_ref, nu_ref, base_ref, tri_ref):
    j = pl.program_id(0)
    tm = ti_ref.shape[0]
    lane = lax.broadcasted_iota(jnp.int32, (tm, LANES), 1)
    ti = ti_ref[...]
    hots = [lane == ti[:, k:k + 1] for k in range(TOP_K)]
    hot_all = jnp.zeros((tm, LANES), jnp.float32)
    for h in hots:
        hot_all = hot_all + jnp.where(h, 1.0, 0.0)
    tile_cnt = jnp.sum(hot_all, axis=0, keepdims=True)

    @pl.when(j == 0)
    def _():
        r = lax.broadcasted_iota(jnp.int32, (tm, tm), 0)
        c = lax.broadcasted_iota(jnp.int32, (tm, tm), 1)
        tri_ref[...] = jnp.where(c < r, 1.0, 0.0).astype(tri_ref.dtype)
        shift = TM_MOE.bit_length() - 1
        lane8 = lax.broadcasted_iota(jnp.int32, cnt_ref.shape, 1)
        cnt = cnt_ref[...].astype(jnp.int32)
        padded = ((cnt + (TM_MOE - 1)) >> shift) << shift
        pend = padded
        s = 1
        while s < N_EXPERTS:
            pend = pend + jnp.where(lane8 >= s, pltpu.roll(pend, s, axis=1), 0)
            s *= 2
        base_ref[...] = (pend - padded).astype(jnp.float32)
        nb = blk_ref.shape[0]
        blk_start = lax.broadcasted_iota(jnp.int32, (nb, LANES), 0) * TM_MOE
        lane_b = lax.broadcasted_iota(jnp.int32, (nb, LANES), 1)
        done = jnp.where((pend[0:1, :] <= blk_start) & (lane_b < N_EXPERTS), 1.0, 0.0)
        blk = jnp.minimum(jnp.sum(done, axis=1, keepdims=True), float(N_EXPERTS - 1))
        blk_ref[...] = jnp.broadcast_to(blk, blk_ref.shape).astype(jnp.int32)
        total = jnp.max(jnp.where(lane8 < N_EXPERTS, pend, 0).astype(jnp.float32),
                        axis=1, keepdims=True)
        nu_ref[...] = jnp.broadcast_to(total.astype(jnp.int32) >> shift, nu_ref.shape)

    before = jnp.dot(tri_ref[...], hot_all.astype(tri_ref.dtype),
                     preferred_element_type=jnp.float32)
    slot = before + base_ref[0:1, :]
    pos = jnp.zeros((tm, LANES), jnp.float32)
    for k, h in enumerate(hots):
        pk = jnp.sum(jnp.where(h, slot, 0.0), axis=1, keepdims=True)
        pos = jnp.where(lane == k, pk, pos)
    pos_ref[...] = pos.astype(jnp.int32)
    base_ref[...] += tile_cnt


def _route(ti, cnt, n_tokens):
    A = n_tokens * TOP_K
    n_blocks = -(-(A + N_EXPERTS * (TM_MOE - 1)) // TM_MOE)
    nb_pad = -(-n_blocks // 8) * 8
    nt = n_tokens // TM_ROUTE
    pos, blk, nu = pl.pallas_call(
        _route_kernel,
        name="route",
        grid=(nt,),
        in_specs=[pl.BlockSpec((TM_ROUTE, LANES), lambda j: (j, 0)),
                  pl.BlockSpec((8, LANES), lambda j: (0, 0))],
        out_specs=[pl.BlockSpec((TM_ROUTE, LANES), lambda j: (j, 0)),
                   pl.BlockSpec((nb_pad, LANES), lambda j: (0, 0)),
                   pl.BlockSpec((8, LANES), lambda j: (0, 0))],
        out_shape=[jax.ShapeDtypeStruct((n_tokens, LANES), jnp.int32),
                   jax.ShapeDtypeStruct((nb_pad, LANES), jnp.int32),
                   jax.ShapeDtypeStruct((8, LANES), jnp.int32)],
        scratch_shapes=[pltpu.VMEM((8, LANES), jnp.float32),
                        pltpu.VMEM((TM_ROUTE, TM_ROUTE), jnp.bfloat16)],
        compiler_params=pltpu.CompilerParams(dimension_semantics=("arbitrary",),
                                             vmem_limit_bytes=VMEM_LIMIT),
    )(ti, cnt)
    pos_kmajor = pos[:, :TOP_K].T
    return pos_kmajor, n_blocks * TM_MOE, blk[:n_blocks, 0], nu[0, :1]


def kernel(x, c, w_mod, b_mod, g_pre_mix, w_in, w_gate_lr, b_gate, g_gla, w_out, g_post_mix,
           g_pre_ffn, router_w, router_b, w_gate_up, b_gate_up, w_down, b_down, g_post_ffn):
    B, S, D = x.shape
    T = B * S
    bf = jnp.bfloat16
    cos_t, sin_t = _rope_tables(S)
    for l in range(w_mod.shape[0]):
        wg, wl = _split_gate_up(w_gate_up[l])
        wd = w_down[l]
        mod3 = _mod(c, w_mod[l], b_mod[l]).reshape(B, N_MOD, D)
        wlr = jnp.pad(w_gate_lr[l], ((0, LANES - GLA_GATE_RANK), (0, 0))).astype(bf)
        proj = _in_proj(x, mod3, g_pre_mix[l:l + 1], _pack_w_in(w_in[l]), cos_t, sin_t, wlr,
                        b_gate[l:l + 1])
        n_pat = len(DILATIONS)
        qa, ka, va = proj[:n_pat], proj[n_pat:2 * n_pat], proj[2 * n_pat:3 * n_pat]
        qg, kg, vg, sr, la = proj[3 * n_pat:]
        pats = [_attn_pattern(qa[j], ka[j], va[j], d) for j, d in enumerate(DILATIONS)]
        og = _gla(qg, kg, vg, la, sr, g_gla[l:l + 1])
        rw = jnp.pad(router_w[l], ((0, 0), (0, LANES - N_EXPERTS)))
        rb = jnp.pad(router_b[l], (0, LANES - N_EXPERTS), constant_values=NEG_BIG)[None]
        x1, h2, ti, tg, cnt = _out_proj([p[0] for p in pats], [p[1] for p in pats], og, x, mod3,
                                        w_out[l].astype(bf), g_post_mix[l:l + 1],
                                        g_pre_ffn[l:l + 1], rw, rb)
        pos_kmajor, n_slots, blk_e, n_used = _route(ti.reshape(T, LANES), cnt, T)
        xs = _scatter_rows(h2.reshape(T, D // 2), pos_kmajor, n_slots)
        out_buf = _moe(xs, blk_e, n_used, wg, wl,
                       b_gate_up[l][:, None, 0::2], b_gate_up[l][:, None, 1::2],
                       wd, b_down[l][:, None, :])
        out = None
        tp = T // FINAL_PARTS
        for part in range(FINAL_PARTS):
            yk = _gather_rows(out_buf, pos_kmajor[:, part * tp:(part + 1) * tp].reshape(-1))
            out = _final(x1.reshape(T, D), yk, tg.reshape(T, LANES), mod3, g_post_ffn[l:l + 1],
                         S, part, out)
        x = out.reshape(B, S, D)
    return x
```

```python
import functools

import numpy as np
import jax
import jax.numpy as jnp
from jax import lax
from jax.experimental import pallas as pl
from jax.experimental.pallas import tpu as pltpu
from jax.experimental.pallas import tpu_sc as plsc

D_MODEL = 1024
ATT_HEADS = 8
ATT_HEAD_DIM = 64
ATT_WIDTH = ATT_HEADS * ATT_HEAD_DIM
DILATIONS = (1, 4, 16)
ATT_SPAN = 128
ROPE_THETA = 10000.0
GLA_HEADS = 4
GLA_KEY_DIM = 64
GLA_VAL_DIM = 128
GLA_KEY_WIDTH = GLA_HEADS * GLA_KEY_DIM
GLA_VAL_WIDTH = GLA_HEADS * GLA_VAL_DIM
GLA_GATE_RANK = 16
GLA_GATE_NORMALIZER = 16.0
GLA_CHUNK = 64
N_EXPERTS = 32
TOP_K = 4
D_FF = D_MODEL
SWIGLU_LIMIT = 7.0
SWIGLU_ALPHA = 1.702
NORM_EPS = 1e-6
N_MOD = 6

LANES = 128
NEG_BIG = -1e30

TM_PROJ = 512
TQ_ATT = 1024
TG_GLA = 512
GLA_UNROLL = 8
TM_OUT = 512
OUT_SUBTILES = 1
TM_MOE = 512
TM_FIN = 1024
TN_MOD = 1024
TK_SPLIT = 512
TM_ROUTE = 1024
FINAL_PARTS = 8
SC_INDEX_WINDOW = 128
SC_GATHER_WINDOW = 64
VMEM_LIMIT = 48 * 1024 * 1024

_C_QA, _C_KA, _C_VA = 0, 512, 1024
_C_QG, _C_KG, _C_VG, _C_RG, _C_LR = 1536, 1792, 2048, 2560, 3072
D_IN_PACKED = 3200


def _rms(x, g):
    return x * lax.rsqrt(jnp.mean(x * x, axis=-1, keepdims=True) + NORM_EPS) * g


def _pack_bf16_pairs(x):
    n = x.shape[1] // 2
    u = lax.bitcast_convert_type(x.astype(jnp.bfloat16).astype(jnp.float32), jnp.uint32)
    return (u[:, :n] >> 16) | (u[:, n:] & jnp.uint32(0xFFFF0000))


def _unpack_bf16_pairs(w):
    lo = lax.bitcast_convert_type(w << 16, jnp.float32)
    hi = lax.bitcast_convert_type(w & jnp.uint32(0xFFFF0000), jnp.float32)
    return lo, hi


def _gather_rows(data, idx):
    n_rows = idx.shape[0]
    width = data.shape[1]
    mesh = plsc.VectorSubcoreMesh(core_axis_name="core", subcore_axis_name="subcore")
    n_workers = mesh.num_cores * mesh.num_subcores
    per_worker = n_rows // n_workers
    assert per_worker * n_workers == n_rows and per_worker % SC_INDEX_WINDOW == 0
    halves = SC_INDEX_WINDOW // SC_GATHER_WINDOW

    @pl.kernel(out_type=jax.ShapeDtypeStruct((n_rows, width), data.dtype), mesh=mesh,
               name="gather_rows",
               scratch_types=[pltpu.VMEM((1, SC_INDEX_WINDOW), jnp.int32),
                              pltpu.VMEM((SC_GATHER_WINDOW, width), data.dtype)])
    def gather(x_hbm, i_hbm, o_hbm, idx_vmem, rows_vmem):
        worker = lax.axis_index("core") * mesh.num_subcores + lax.axis_index("subcore")
        base = worker * per_worker

        @pl.loop(0, per_worker // SC_INDEX_WINDOW)
        def _(j):
            off = base + j * SC_INDEX_WINDOW
            pltpu.sync_copy(i_hbm.at[:, pl.ds(off, SC_INDEX_WINDOW)], idx_vmem)
            for h in range(halves):
                part = idx_vmem.at[0, pl.ds(h * SC_GATHER_WINDOW, SC_GATHER_WINDOW)]
                pltpu.sync_copy(x_hbm.at[part], rows_vmem)
                pltpu.sync_copy(rows_vmem,
                                o_hbm.at[pl.ds(off + h * SC_GATHER_WINDOW, SC_GATHER_WINDOW)])

    return gather(data, idx.reshape(1, n_rows))


def _scatter_rows(data, idx, n_out):
    n_copies, n_rows = idx.shape
    width = data.shape[1]
    mesh = plsc.VectorSubcoreMesh(core_axis_name="core", subcore_axis_name="subcore")
    n_workers = mesh.num_cores * mesh.num_subcores
    per_worker = n_rows // n_workers
    assert per_worker * n_workers == n_rows and per_worker % SC_INDEX_WINDOW == 0
    halves = SC_INDEX_WINDOW // SC_GATHER_WINDOW

    @pl.kernel(out_type=jax.ShapeDtypeStruct((n_out, width), data.dtype), mesh=mesh,
               name="scatter_rows",
               scratch_types=[pltpu.VMEM((n_copies, SC_INDEX_WINDOW), jnp.int32),
                              pltpu.VMEM((SC_GATHER_WINDOW, width), data.dtype)])
    def scatter(x_hbm, i_hbm, o_hbm, idx_vmem, rows_vmem):
        worker = lax.axis_index("core") * mesh.num_subcores + lax.axis_index("subcore")
        base = worker * per_worker

        @pl.loop(0, per_worker // SC_INDEX_WINDOW)
        def _(j):
            off = base + j * SC_INDEX_WINDOW
            pltpu.sync_copy(i_hbm.at[:, pl.ds(off, SC_INDEX_WINDOW)], idx_vmem)
            for h in range(halves):
                pltpu.sync_copy(x_hbm.at[pl.ds(off + h * SC_GATHER_WINDOW, SC_GATHER_WINDOW)],
                                rows_vmem)
                for k in range(n_copies):
                    part = idx_vmem.at[k, pl.ds(h * SC_GATHER_WINDOW, SC_GATHER_WINDOW)]
                    pltpu.sync_copy(rows_vmem, o_hbm.at[part])

    return scatter(data, idx)


def _mod_kernel(ct_ref, w_ref, b_ref, o_ref):
    ct = ct_ref[...]
    s = ct * (1.0 / (1.0 + jnp.exp(-ct)))
    w = w_ref[...]
    rows = [jnp.sum(s[:, b:b + 1] * w, axis=0, keepdims=True) for b in range(ct.shape[1])]
    o_ref[...] = jnp.concatenate(rows, axis=0) + b_ref[...]


def _mod(c, w_mod, b_mod):
    B = c.shape[0]
    n = w_mod.shape[1]
    return pl.pallas_call(
        _mod_kernel,
        name="mod",
        grid=(n // TN_MOD,),
        in_specs=[pl.BlockSpec((D_MODEL, B), lambda j: (0, 0)),
                  pl.BlockSpec((D_MODEL, TN_MOD), lambda j: (0, j)),
                  pl.BlockSpec((1, TN_MOD), lambda j: (0, j))],
        out_specs=pl.BlockSpec((B, TN_MOD), lambda j: (0, j)),
        out_shape=jax.ShapeDtypeStruct((B, n), jnp.float32),
        compiler_params=pltpu.CompilerParams(dimension_semantics=("arbitrary",),
                                             vmem_limit_bytes=VMEM_LIMIT),
    )(c.T, w_mod, b_mod.reshape(1, n))


def _store_residue_views(scr_ref, scr4_ref, out_refs):
    ns, tm, _ = scr_ref.shape
    w = ns * LANES
    o1_ref, o4_ref, o16_ref = out_refs
    assert DILATIONS == (1, 4, 16)
    q4 = tm // 4
    for s in range(ns):
        o1_ref[0, :, s * LANES:(s + 1) * LANES] = scr_ref[s].astype(o1_ref.dtype)
    for r in range(4):
        for s in range(ns):
            vals = scr_ref[s, pl.ds(r, q4, stride=4), :]
            scr4_ref[s, r * q4:(r + 1) * q4, :] = vals
            c0 = r * w + s * LANES
            o4_ref[0, :, c0:c0 + LANES] = vals.astype(o4_ref.dtype)
    for r in range(16):
        lo, hi = r % 4, r // 4
        for s in range(ns):
            vals = scr4_ref[s, pl.ds(lo * q4 + hi, tm // 16, stride=4), :]
            c0 = r * w + s * LANES
            o16_ref[0, :, c0:c0 + LANES] = vals.astype(o16_ref.dtype)


def _in_proj_kernel(x_ref, mod_ref, g_ref, w_ref, cos_ref, sin_ref, wlr_ref, bg_ref,
                    qa1_ref, qa4_ref, qa16_ref, ka1_ref, ka4_ref, ka16_ref,
                    va1_ref, va4_ref, va16_ref, qg_ref, kg_ref, vg_ref, sr_ref, la_ref,
                    qs_ref, ks_ref, vs_ref, q4_ref, k4_ref, v4_ref):
    x = x_ref[0]
    shift = mod_ref[0, 0:1, :]
    scale = mod_ref[0, 1:2, :]
    h = (_rms(x, g_ref[...] * (1.0 + scale)) + shift).astype(jnp.bfloat16)

    def proj(c0, width):
        return jnp.dot(h, w_ref[:, c0:c0 + width], preferred_element_type=jnp.float32)

    cos = cos_ref[...]
    sin = sin_ref[...]

    def rope(c0, scr_ref, mult):
        p = proj(c0, ATT_WIDTH)
        for s in range(ATT_WIDTH // LANES):
            t = p[:, s * LANES:(s + 1) * LANES]
            r = t * cos + pltpu.roll(t, LANES // 2, axis=1) * sin
            if mult != 1.0:
                r = r * mult
            scr_ref[s] = r

    rope(_C_QA, qs_ref, ATT_HEAD_DIM ** -0.5)
    _store_residue_views(qs_ref, q4_ref, (qa1_ref, qa4_ref, qa16_ref))
    rope(_C_KA, ks_ref, 1.0)
    _store_residue_views(ks_ref, k4_ref, (ka1_ref, ka4_ref, ka16_ref))
    pv = proj(_C_VA, ATT_WIDTH)
    for s in range(ATT_WIDTH // LANES):
        vs_ref[s] = pv[:, s * LANES:(s + 1) * LANES]
    _store_residue_views(vs_ref, v4_ref, (va1_ref, va4_ref, va16_ref))
    qg_ref[0] = proj(_C_QG, GLA_KEY_WIDTH).astype(qg_ref.dtype)
    kg_ref[0] = proj(_C_KG, GLA_KEY_WIDTH).astype(kg_ref.dtype)
    vg_ref[0] = proj(_C_VG, GLA_VAL_WIDTH).astype(vg_ref.dtype)
    r = proj(_C_RG, GLA_VAL_WIDTH)
    sr_ref[0] = (r * (1.0 / (1.0 + jnp.exp(-r)))).astype(sr_ref.dtype)
    lr = proj(_C_LR, LANES).astype(jnp.bfloat16)
    z = jnp.dot(lr, wlr_ref[...], preferred_element_type=jnp.float32) + bg_ref[...]
    log_sig = jnp.minimum(z, 0.0) - jnp.log(1.0 + jnp.exp(-jnp.abs(z)))
    la_ref[0] = log_sig * (1.0 / GLA_GATE_NORMALIZER)


def _in_proj(x, mod3, g_pre, w_packed, cos_t, sin_t, wlr, bg):
    B, S, D = x.shape
    nb = S // TM_PROJ
    tok = lambda w: pl.BlockSpec((1, TM_PROJ, w), lambda b, i: (b, i, 0))
    const = lambda shape: pl.BlockSpec(shape, lambda b, i: tuple(0 for _ in shape))
    bf = jnp.bfloat16
    att_shapes = [jax.ShapeDtypeStruct((B, S // d, d * ATT_WIDTH), bf) for d in DILATIONS] * 3
    att_specs = [pl.BlockSpec((1, TM_PROJ // d, d * ATT_WIDTH), lambda b, i: (b, i, 0))
                 for d in DILATIONS] * 3
    gla_shapes = [jax.ShapeDtypeStruct((B, S, w), dt) for w, dt in
                  [(GLA_KEY_WIDTH, bf), (GLA_KEY_WIDTH, bf),
                   (GLA_VAL_WIDTH, bf), (GLA_VAL_WIDTH, bf), (GLA_KEY_WIDTH, jnp.float32)]]
    return pl.pallas_call(
        _in_proj_kernel,
        name="in_proj",
        grid=(B, nb),
        in_specs=[tok(D),
                  pl.BlockSpec((1, N_MOD, D), lambda b, i: (b, 0, 0)),
                  const((1, D)),
                  const((D, D_IN_PACKED)),
                  pl.BlockSpec((TM_PROJ, LANES), lambda b, i: (i, 0)),
                  pl.BlockSpec((TM_PROJ, LANES), lambda b, i: (i, 0)),
                  const((LANES, GLA_KEY_WIDTH)),
                  const((1, GLA_KEY_WIDTH))],
        out_specs=att_specs + [tok(s.shape[-1]) for s in gla_shapes],
        out_shape=att_shapes + gla_shapes,
        scratch_shapes=[pltpu.VMEM((ATT_WIDTH // LANES, TM_PROJ, LANES), jnp.float32)] * 6,
        compiler_params=pltpu.CompilerParams(dimension_semantics=("arbitrary", "arbitrary"),
                                             vmem_limit_bytes=VMEM_LIMIT),
    )(x, mod3, g_pre, w_packed, cos_t, sin_t, wlr, bg)


def _attn_kernel(q_ref, kp_ref, kc_ref, vp_ref, vc_ref, o_ref, st_ref):
    i = pl.program_id(2)
    tp = kp_ref.shape[1]
    tq = tp
    n_sub = q_ref.shape[1] // tq
    row = lax.broadcasted_iota(jnp.int32, (tq, tp + tq), 0)
    col = lax.broadcasted_iota(jnp.int32, (tq, tp + tq), 1)
    band = (col >= row) & (col <= row + ATT_SPAN)
    bias_inner = jnp.where(band, 0.0, NEG_BIG)
    bias_first = jnp.where(band & ((col >= tp) | (i > 0)), 0.0, NEG_BIG)
    lane = lax.broadcasted_iota(jnp.int32, (tq, LANES), 1)
    q_head = (lane >> 5) & 1
    for sb in range(n_sub):
        rows = slice(sb * tq, (sb + 1) * tq)
        bias = bias_first if sb == 0 else bias_inner
        stat_m = jnp.zeros((tq, LANES), jnp.float32)
        stat_l = jnp.ones((tq, LANES), jnp.float32)
        for hp in range(ATT_HEADS // 2):
            sl = slice(hp * LANES, (hp + 1) * LANES)
            q2 = q_ref[0, rows, sl]
            if sb == 0:
                kcat = jnp.concatenate([kp_ref[0, :, sl], kc_ref[0, rows, sl]], axis=0)
                vcat = jnp.concatenate([vp_ref[0, :, sl], vc_ref[0, rows, sl]], axis=0)
            else:
                kcat = kc_ref[0, (sb - 1) * tq:(sb + 1) * tq, sl]
                vcat = vc_ref[0, (sb - 1) * tq:(sb + 1) * tq, sl]
            outs, dens = [], []
            for hh in range(2):
                qm = jnp.where(q_head == hh, q2, jnp.zeros_like(q2))
                s = lax.dot_general(qm, kcat, (((1,), (1,)), ((), ())),
                                    preferred_element_type=jnp.float32) + bias
                m = jnp.max(s, axis=1, keepdims=True)
                p = jnp.exp(s - m)
                l = jnp.sum(p, axis=1, keepdims=True)
                outs.append(jnp.dot(p.astype(vcat.dtype), vcat, preferred_element_type=jnp.float32))
                dens.append(l)
                stat_m = jnp.where(lane == 2 * hp + hh, m, stat_m)
                stat_l = jnp.where(lane == 2 * hp + hh, l, stat_l)
            first = lane < ATT_HEAD_DIM
            o_pair = jnp.where(first, outs[0], outs[1]) / jnp.where(first, dens[0], dens[1])
            o_ref[0, rows, sl] = o_pair.astype(o_ref.dtype)
        st_ref[0, rows, :] = stat_m + jnp.log(stat_l)


def _attn_pattern(qv, kv, vv, dil):
    B, L, _ = qv.shape
    W = ATT_WIDTH
    tq = min(TQ_ATT, L)
    nq = L // tq
    cur = pl.BlockSpec((1, tq, W), lambda b, r, i: (b, i, r))
    back = tq // ATT_SPAN
    prev = pl.BlockSpec((1, ATT_SPAN, W), lambda b, r, i: (b, jnp.maximum(i * back - 1, 0), r))
    o, st = pl.pallas_call(
        _attn_kernel,
        name=f"attn_d{dil}",
        grid=(B, dil, nq),
        in_specs=[cur, prev, cur, prev, cur],
        out_specs=[cur, pl.BlockSpec((1, tq, LANES), lambda b, r, i: (b, i, r))],
        out_shape=[jax.ShapeDtypeStruct((B, L, dil * W), jnp.bfloat16),
                   jax.ShapeDtypeStruct((B, L, dil * LANES), jnp.float32)],
        compiler_params=pltpu.CompilerParams(
            dimension_semantics=("arbitrary", "arbitrary", "arbitrary"),
            vmem_limit_bytes=VMEM_LIMIT),
    )(qv, kv, kv, vv, vv)
    return o, st


def _cumsum_rows(x):
    n = x.shape[0]
    row = lax.broadcasted_iota(jnp.int32, x.shape, 0)
    s = 1
    while s < n:
        x = x + jnp.where(row >= s, pltpu.roll(x, s, axis=0), 0.0)
        s *= 2
    return x


def _gla_kernel(q_ref, k_ref, v_ref, la_ref, sr_ref, g_ref, o_ref, st_ref):
    @pl.when(pl.program_id(0) == 0)
    def _():
        st_ref[...] = jnp.zeros_like(st_ref)

    C = GLA_CHUNK
    n_batch = q_ref.shape[0]
    n_chunks = q_ref.shape[1] // C
    lane = lax.broadcasted_iota(jnp.int32, (C, LANES), 1)
    lane_sq = lax.broadcasted_iota(jnp.int32, (LANES, LANES), 1)
    tril = (lax.broadcasted_iota(jnp.int32, (C, C), 0)
            >= lax.broadcasted_iota(jnp.int32, (C, C), 1))
    g = g_ref[...]
    bf = jnp.bfloat16

    def chunk(c, carry):
        r0 = pl.multiple_of(c * C, C)
        rows = pl.ds(r0, C)
        for n in range(n_batch):
            for p in range(GLA_HEADS // 2):
                ksl = slice(p * LANES, (p + 1) * LANES)
                b = _cumsum_rows(la_ref[n, rows, ksl])
                b_last = b[C - 1:C, :]
                q = q_ref[n, rows, ksl].astype(jnp.float32) * (GLA_KEY_DIM ** -0.5)
                k = k_ref[n, rows, ksl].astype(jnp.float32)
                q_dec = q * jnp.exp(b)
                k_inv = (k * jnp.exp(-b)).astype(bf)
                k_dec = (k * jnp.exp(b_last - b)).astype(bf)
                st = st_ref[n, p]
                st_b = st.astype(bf)
                ut = []
                for hh in range(2):
                    h = 2 * p + hh
                    vsl = slice(h * GLA_VAL_DIM, (h + 1) * GLA_VAL_DIM)
                    v = v_ref[n, rows, vsl]
                    own = (lane >= GLA_KEY_DIM) if hh else (lane < GLA_KEY_DIM)
                    qm = jnp.where(own, q_dec, 0.0).astype(bf)
                    att = lax.dot_general(qm, k_inv, (((1,), (1,)), ((), ())),
                                          preferred_element_type=jnp.float32)
                    att = jnp.where(tril, att, 0.0).astype(bf)
                    o = jnp.dot(att, v, preferred_element_type=jnp.float32)
                    o = o + lax.dot_general(qm, st_b, (((1,), (1,)), ((), ())),
                                            preferred_element_type=jnp.float32)
                    o = _rms(o, g) * sr_ref[n, rows, vsl].astype(jnp.float32)
                    o_ref[n, rows, vsl] = o.astype(o_ref.dtype)
                    ut.append(lax.dot_general(v, k_dec, (((0,), (0,)), ((), ())),
                                              preferred_element_type=jnp.float32))
                st_ref[n, p] = (st * jnp.exp(b_last)
                                + jnp.where(lane_sq < GLA_KEY_DIM, ut[0], ut[1]))
        return carry

    lax.fori_loop(0, n_chunks, chunk, 0, unroll=GLA_UNROLL)


def _gla(qg, kg, vg, la, sr, g_gla):
    B, S, _ = qg.shape
    tok = lambda w: pl.BlockSpec((B, TG_GLA, w), lambda i: (0, i, 0))
    return pl.pallas_call(
        _gla_kernel,
        name="gla",
        grid=(S // TG_GLA,),
        in_specs=[tok(GLA_KEY_WIDTH), tok(GLA_KEY_WIDTH), tok(GLA_VAL_WIDTH),
                  tok(GLA_KEY_WIDTH), tok(GLA_VAL_WIDTH),
                  pl.BlockSpec((1, GLA_VAL_DIM), lambda i: (0, 0))],
        out_specs=tok(GLA_VAL_WIDTH),
        out_shape=jax.ShapeDtypeStruct((B, S, GLA_VAL_WIDTH), jnp.bfloat16),
        scratch_shapes=[pltpu.VMEM((B, GLA_HEADS // 2, GLA_VAL_DIM, LANES), jnp.float32)],
        compiler_params=pltpu.CompilerParams(dimension_semantics=("arbitrary",),
                                             vmem_limit_bytes=VMEM_LIMIT),
    )(qg, kg, vg, la, sr, g_gla)


def _out_proj_kernel(o1_ref, o2_ref, o3_ref, s1_ref, s2_ref, s3_ref, og_ref, x_ref, mod_ref,
                     wout_ref, gpost_ref, gpre_ref, rw_ref, rb_ref,
                     x1_ref, h2_ref, ti_ref, tg_ref, cnt_ref, oscr_ref, sscr_ref, otmp_ref,
                     stmp_ref):
    tm = x_ref.shape[1]

    @pl.when((pl.program_id(0) == 0) & (pl.program_id(1) == 0))
    def _():
        cnt_ref[...] = jnp.zeros_like(cnt_ref)

    ns = ATT_WIDTH // LANES
    q4 = tm // 4
    for r in range(4):
        rows = pl.ds(r, q4, stride=4)
        for s in range(ns):
            c0 = r * ATT_WIDTH + s * LANES
            oscr_ref[0, s, rows, :] = o2_ref[0, :, c0:c0 + LANES].astype(jnp.float32)
        sscr_ref[0, rows, :] = s2_ref[0, :, r * LANES:(r + 1) * LANES]
    for r in range(16):
        lo, hi = r % 4, r // 4
        rows = pl.ds(lo * q4 + hi, tm // 16, stride=4)
        for s in range(ns):
            c0 = r * ATT_WIDTH + s * LANES
            otmp_ref[s, rows, :] = o3_ref[0, :, c0:c0 + LANES].astype(jnp.float32)
        stmp_ref[rows, :] = s3_ref[0, :, r * LANES:(r + 1) * LANES]
    for r in range(4):
        rows = pl.ds(r, q4, stride=4)
        for s in range(ns):
            oscr_ref[1, s, rows, :] = otmp_ref[s, r * q4:(r + 1) * q4, :]
        sscr_ref[1, rows, :] = stmp_ref[r * q4:(r + 1) * q4, :]
    gate1 = mod_ref[0, 2:3, :]
    shift2 = mod_ref[0, 3:4, :]
    scale2 = mod_ref[0, 4:5, :]
    bf = jnp.bfloat16
    rw = rw_ref[...]
    w_hi = rw.astype(bf)
    w_lo = (rw - w_hi.astype(jnp.float32)).astype(bf)
    ts = tm // OUT_SUBTILES
    lane = lax.broadcasted_iota(jnp.int32, (ts, LANES), 1)
    lane_f = lane.astype(jnp.float32)
    for sub in range(OUT_SUBTILES):
        rs = slice(sub * ts, (sub + 1) * ts)
        lses = [s1_ref[0, rs, :], sscr_ref[0, rs, :], sscr_ref[1, rs, :]]
        m = jnp.maximum(jnp.maximum(lses[0], lses[1]), lses[2])
        es = [jnp.exp(t - m) for t in lses]
        inv = 1.0 / (es[0] + es[1] + es[2])
        ws = [e * inv for e in es]
        pairs = []
        for hp in range(ATT_HEADS // 2):
            sl = slice(hp * LANES, (hp + 1) * LANES)
            o_pats = [o1_ref[0, rs, sl].astype(jnp.float32), oscr_ref[0, hp, rs, :],
                      oscr_ref[1, hp, rs, :]]
            acc = jnp.zeros((ts, LANES), jnp.float32)
            head_of_lane = 2 * hp + (lane >> 6)
            for w, o in zip(ws, o_pats):
                acc = acc + jnp.take_along_axis(w, head_of_lane, axis=1) * o
            pairs.append(acc.astype(bf))
        mixed = jnp.concatenate(pairs + [og_ref[0, rs, :]], axis=1)
        y = jnp.dot(mixed, wout_ref[...], preferred_element_type=jnp.float32)
        x1 = x_ref[0, rs, :] + _rms(y, gate1 * gpost_ref[...])
        x1_ref[0, rs, :] = x1
        h2 = _rms(x1, gpre_ref[...] * (1.0 + scale2)) + shift2
        h2_ref[0, rs, :] = _pack_bf16_pairs(h2)
        h_hi = h2.astype(bf)
        h_lo = (h2 - h_hi.astype(jnp.float32)).astype(bf)
        logits = (jnp.dot(h_hi, w_hi, preferred_element_type=jnp.float32)
                  + jnp.dot(h_lo, w_hi, preferred_element_type=jnp.float32)
                  + jnp.dot(h_hi, w_lo, preferred_element_type=jnp.float32)) + rb_ref[...]
        vals, idxs = [], []
        for _ in range(TOP_K):
            mk = jnp.max(logits, axis=1, keepdims=True)
            ik = jnp.min(jnp.where(logits == mk, lane_f, float(LANES)), axis=1, keepdims=True)
            logits = jnp.where(lane_f == ik, -jnp.inf, logits)
            vals.append(mk)
            idxs.append(ik)
        ex = [jnp.exp(v - vals[0]) for v in vals]
        den = ex[0] + ex[1] + ex[2] + ex[3]
        ti = jnp.zeros((ts, LANES), jnp.float32)
        tg = jnp.zeros((ts, LANES), jnp.float32)
        for kk in range(TOP_K):
            ti = jnp.where(lane == kk, idxs[kk], ti)
            tg = jnp.where(lane == kk, ex[kk] / den, tg)
        ti_ref[0, rs, :] = ti.astype(jnp.int32)
        tg_ref[0, rs, :] = tg
        chosen = jnp.zeros((ts, LANES), jnp.float32)
        for ik in idxs:
            chosen = chosen + jnp.where(lane_f == ik, 1.0, 0.0)
        cnt_ref[...] += jnp.sum(chosen, axis=0, keepdims=True)


def _out_proj(o_pats, st_pats, og, x, mod3, w_out, g_post, g_pre, rw, rb):
    B, S, D = x.shape
    tok = lambda w: pl.BlockSpec((1, TM_OUT, w), lambda b, i: (b, i, 0))
    const = lambda shape: pl.BlockSpec(shape, lambda b, i: tuple(0 for _ in shape))
    out_shapes = [jax.ShapeDtypeStruct((B, S, D), jnp.float32),
                  jax.ShapeDtypeStruct((B, S, D // 2), jnp.uint32),
                  jax.ShapeDtypeStruct((B, S, LANES), jnp.int32),
                  jax.ShapeDtypeStruct((B, S, LANES), jnp.float32),
                  jax.ShapeDtypeStruct((8, LANES), jnp.float32)]
    return pl.pallas_call(
        _out_proj_kernel,
        name="out_proj",
        grid=(B, S // TM_OUT),
        in_specs=[pl.BlockSpec((1, TM_OUT // d, d * ATT_WIDTH), lambda b, i: (b, i, 0))
                  for d in DILATIONS]
                 + [pl.BlockSpec((1, TM_OUT // d, d * LANES), lambda b, i: (b, i, 0))
                    for d in DILATIONS]
                 + [tok(GLA_VAL_WIDTH), tok(D),
                  pl.BlockSpec((1, N_MOD, D), lambda b, i: (b, 0, 0)),
                  const((D, D)), const((1, D)), const((1, D)),
                  const((D, LANES)), const((1, LANES))],
        out_specs=[tok(D), tok(D // 2), tok(LANES), tok(LANES), const((8, LANES))],
        out_shape=out_shapes,
        scratch_shapes=[pltpu.VMEM((2, ATT_WIDTH // LANES, TM_OUT, LANES), jnp.float32),
                        pltpu.VMEM((2, TM_OUT, LANES), jnp.float32),
                        pltpu.VMEM((ATT_WIDTH // LANES, TM_OUT, LANES), jnp.float32),
                        pltpu.VMEM((TM_OUT, LANES), jnp.float32)],
        compiler_params=pltpu.CompilerParams(dimension_semantics=("arbitrary", "arbitrary"),
                                             vmem_limit_bytes=VMEM_LIMIT),
    )(*o_pats, *st_pats, og, x, mod3, w_out, g_post, g_pre, rw, rb)


def _split_gate_up_kernel(w_ref, wg_ref, wl_ref):
    group = 2 * LANES
    src = lax.broadcasted_iota(jnp.int32, (group, group), 0)
    dst = lax.broadcasted_iota(jnp.int32, (group, group), 1)
    want = jnp.where(dst < LANES, 2 * dst, 2 * (dst - LANES) + 1)
    perm = jnp.where(src == want, 1.0, 0.0).astype(jnp.bfloat16)
    for j in range(w_ref.shape[1] // group):
        t = jnp.dot(w_ref[:, j * group:(j + 1) * group].astype(jnp.bfloat16), perm,
                    preferred_element_type=jnp.float32)
        wg_ref[:, j * LANES:(j + 1) * LANES] = t[:, :LANES].astype(wg_ref.dtype)
        wl_ref[:, j * LANES:(j + 1) * LANES] = t[:, LANES:].astype(wl_ref.dtype)


def _split_gate_up(w_gate_up):
    E, K, N2 = w_gate_up.shape
    out = jax.ShapeDtypeStruct((E, K, N2 // 2), jnp.bfloat16)
    ospec = pl.BlockSpec((None, TK_SPLIT, N2 // 2), lambda e, i: (e, i, 0))
    return pl.pallas_call(
        _split_gate_up_kernel,
        name="split_gate_up",
        grid=(E, K // TK_SPLIT),
        in_specs=[pl.BlockSpec((None, TK_SPLIT, N2), lambda e, i: (e, i, 0))],
        out_specs=[ospec, ospec],
        out_shape=[out, out],
        compiler_params=pltpu.CompilerParams(dimension_semantics=("arbitrary", "arbitrary"),
                                             vmem_limit_bytes=VMEM_LIMIT),
    )(w_gate_up)


def _moe_kernel(be_ref, nu_ref, x_ref, wg_ref, wl_ref, bg_ref, bl_ref, wd_ref, bd_ref, o_ref,
                wd_bf_ref):
    i = pl.program_id(0)
    new_expert = (i == 0) | (be_ref[i] != be_ref[jnp.maximum(i - 1, 0)])

    @pl.when(new_expert & (i < nu_ref[0]))
    def _():
        wd_bf_ref[...] = wd_ref[...].astype(wd_bf_ref.dtype)

    @pl.when(i < nu_ref[0])
    def _():
        lo, hi = _unpack_bf16_pairs(x_ref[...])
        x = jnp.concatenate([lo, hi], axis=1).astype(jnp.bfloat16)
        xg = jnp.dot(x, wg_ref[...], preferred_element_type=jnp.float32) + bg_ref[...]
        xl = jnp.dot(x, wl_ref[...], preferred_element_type=jnp.float32) + bl_ref[...]
        xg = jnp.minimum(xg, SWIGLU_LIMIT)
        xl = jnp.clip(xl, -SWIGLU_LIMIT, SWIGLU_LIMIT)
        act = xg * (1.0 / (1.0 + jnp.exp(-SWIGLU_ALPHA * xg))) * (xl + 1.0)
        out = jnp.dot(act.astype(jnp.bfloat16), wd_bf_ref[...],
                      preferred_element_type=jnp.float32) + bd_ref[...]
        o_ref[...] = _pack_bf16_pairs(out)


def _moe(xs, blk_e, n_used, wg, wl, bg, bl, wd, bd):
    P, half = xs.shape
    D = 2 * half
    n_blocks = P // TM_MOE
    used = lambda i, nu: jnp.minimum(i, nu[0] - 1)
    wspec = lambda k, n: pl.BlockSpec((None, k, n), lambda i, be, nu: (be[used(i, nu)], 0, 0))
    grid_spec = pltpu.PrefetchScalarGridSpec(
        num_scalar_prefetch=2,
        grid=(n_blocks,),
        in_specs=[pl.BlockSpec((TM_MOE, half), lambda i, be, nu: (used(i, nu), 0)),
                  wspec(D, D_FF), wspec(D, D_FF), wspec(1, D_FF), wspec(1, D_FF),
                  wspec(D_FF, D), wspec(1, D)],
        out_specs=pl.BlockSpec((TM_MOE, half), lambda i, be, nu: (used(i, nu), 0)),
        scratch_shapes=[pltpu.VMEM((D_FF, D), jnp.bfloat16)],
    )
    return pl.pallas_call(
        _moe_kernel,
        name="moe",
        grid_spec=grid_spec,
        out_shape=jax.ShapeDtypeStruct((P, half), jnp.uint32),
        compiler_params=pltpu.CompilerParams(dimension_semantics=("arbitrary",),
                                             vmem_limit_bytes=VMEM_LIMIT),
    )(blk_e, n_used, xs, wg, wl, bg, bl, wd, bd)


def _final_kernel(x1_ref, y0_ref, y1_ref, y2_ref, y3_ref, tg_ref, mod_ref, g_ref, o_ref):
    gates = tg_ref[...]
    half = y0_ref.shape[1]
    y_lo = jnp.zeros((x1_ref.shape[0], half), jnp.float32)
    y_hi = jnp.zeros((x1_ref.shape[0], half), jnp.float32)
    for kk, yk_ref in enumerate((y0_ref, y1_ref, y2_ref, y3_ref)):
        lo, hi = _unpack_bf16_pairs(yk_ref[...])
        y_lo = y_lo + gates[:, kk:kk + 1] * lo
        y_hi = y_hi + gates[:, kk:kk + 1] * hi
    y = jnp.concatenate([y_lo, y_hi], axis=1)
    gate2 = mod_ref[0, 5:6, :]
    o_ref[...] = x1_ref[...] + _rms(y, gate2 * g_ref[...])


def _final_part_kernel(x1_ref, y0_ref, y1_ref, y2_ref, y3_ref, tg_ref, mod_ref, g_ref, prev_ref,
                       o_ref):
    del prev_ref
    _final_kernel(x1_ref, y0_ref, y1_ref, y2_ref, y3_ref, tg_ref, mod_ref, g_ref, o_ref)


def _final(x1, yk, tg, mod3, g_post, seq_len, part, prev):
    T, D = x1.shape
    tp = yk.shape[0] // TOP_K
    nt = tp // TM_FIN
    t0 = part * nt
    batch = (part * tp) // seq_len
    yspec = lambda kk: pl.BlockSpec((TM_FIN, D // 2), lambda i: (kk * nt + i, 0))
    in_specs = ([pl.BlockSpec((TM_FIN, D), lambda i: (t0 + i, 0))]
                + [yspec(kk) for kk in range(TOP_K)]
                + [pl.BlockSpec((TM_FIN, LANES), lambda i: (t0 + i, 0)),
                   pl.BlockSpec((1, N_MOD, D), lambda i: (batch, 0, 0)),
                   pl.BlockSpec((1, D), lambda i: (0, 0))])
    args = [x1, yk, yk, yk, yk, tg, mod3, g_post]
    body, aliases = _final_kernel, {}
    if prev is not None:
        in_specs.append(pl.BlockSpec(memory_space=pl.ANY))
        args.append(prev)
        body, aliases = _final_part_kernel, {len(args) - 1: 0}
    return pl.pallas_call(
        body,
        name="final",
        grid=(nt,),
        in_specs=in_specs,
        out_specs=pl.BlockSpec((TM_FIN, D), lambda i: (t0 + i, 0)),
        out_shape=jax.ShapeDtypeStruct((T, D), jnp.float32),
        input_output_aliases=aliases,
        compiler_params=pltpu.CompilerParams(dimension_semantics=("arbitrary",),
                                             vmem_limit_bytes=VMEM_LIMIT),
    )(*args)


def _pack_w_in(w_in):
    half = ATT_HEAD_DIM // 2

    def pair_rotary_layout(w):
        k = w.shape[0]
        w = w.reshape(k, ATT_HEADS // 2, 2, 2, half)
        return w.transpose(0, 1, 3, 2, 4).reshape(k, ATT_WIDTH)

    lr = jnp.pad(w_in[:, 3072:3072 + GLA_GATE_RANK], ((0, 0), (0, LANES - GLA_GATE_RANK)))
    return jnp.concatenate([pair_rotary_layout(w_in[:, :ATT_WIDTH]),
                            pair_rotary_layout(w_in[:, ATT_WIDTH:2 * ATT_WIDTH]),
                            w_in[:, 1024:3072], lr], axis=1).astype(jnp.bfloat16)


def _rope_tables(seq_len):
    half = ATT_HEAD_DIM // 2
    inv_freq = ROPE_THETA ** (-jnp.arange(half, dtype=jnp.float32) / half)
    ang = jnp.arange(seq_len, dtype=jnp.float32)[:, None] * inv_freq[None, :]
    cos = jnp.tile(jnp.cos(ang), (1, LANES // half))
    sin = jnp.tile(jnp.sin(ang), (1, LANES // half))
    sign = jnp.where(jnp.arange(LANES) < LANES // 2, -1.0, 1.0)
    return cos, sin * sign


def _route_kernel(ti_ref, cnt_ref, pos_ref, blk_ref, nu_ref, base_ref, tri_ref):
    j = pl.program_id(0)
    tm = ti_ref.shape[0]
    lane = lax.broadcasted_iota(jnp.int32, (tm, LANES), 1)
    ti = ti_ref[...]
    hots = [lane == ti[:, k:k + 1] for k in range(TOP_K)]
    hot_all = jnp.zeros((tm, LANES), jnp.float32)
    for h in hots:
        hot_all = hot_all + jnp.where(h, 1.0, 0.0)
    tile_cnt = jnp.sum(hot_all, axis=0, keepdims=True)

    @pl.when(j == 0)
    def _():
        r = lax.broadcasted_iota(jnp.int32, (tm, tm), 0)
        c = lax.broadcasted_iota(jnp.int32, (tm, tm), 1)
        tri_ref[...] = jnp.where(c < r, 1.0, 0.0).astype(tri_ref.dtype)
        shift = TM_MOE.bit_length() - 1
        lane8 = lax.broadcasted_iota(jnp.int32, cnt_ref.shape, 1)
        cnt = cnt_ref[...].astype(jnp.int32)
        padded = ((cnt + (TM_MOE - 1)) >> shift) << shift
        pend = padded
        s = 1
        while s < N_EXPERTS:
            pend = pend + jnp.where(lane8 >= s, pltpu.roll(pend, s, axis=1), 0)
            s *= 2
        base_ref[...] = (pend - padded).astype(jnp.float32)
        nb = blk_ref.shape[0]
        blk_start = lax.broadcasted_iota(jnp.int32, (nb, LANES), 0) * TM_MOE
        lane_b = lax.broadcasted_iota(jnp.int32, (nb, LANES), 1)
        done = jnp.where((pend[0:1, :] <= blk_start) & (lane_b < N_EXPERTS), 1.0, 0.0)
        blk = jnp.minimum(jnp.sum(done, axis=1, keepdims=True), float(N_EXPERTS - 1))
        blk_ref[...] = jnp.broadcast_to(blk, blk_ref.shape).astype(jnp.int32)
        total = jnp.max(jnp.where(lane8 < N_EXPERTS, pend, 0).astype(jnp.float32),
                        axis=1, keepdims=True)
        nu_ref[...] = jnp.broadcast_to(total.astype(jnp.int32) >> shift, nu_ref.shape)

    before = jnp.dot(tri_ref[...], hot_all.astype(tri_ref.dtype),
                     preferred_element_type=jnp.float32)
    slot = before + base_ref[0:1, :]
    pos = jnp.zeros((tm, LANES), jnp.float32)
    for k, h in enumerate(hots):
        pk = jnp.sum(jnp.where(h, slot, 0.0), axis=1, keepdims=True)
        pos = jnp.where(lane == k, pk, pos)
    pos_ref[...] = pos.astype(jnp.int32)
    base_ref[...] += tile_cnt


def _route(ti, cnt, n_tokens):
    A = n_tokens * TOP_K
    n_blocks = -(-(A + N_EXPERTS * (TM_MOE - 1)) // TM_MOE)
    nb_pad = -(-n_blocks // 8) * 8
    nt = n_tokens // TM_ROUTE
    pos, blk, nu = pl.pallas_call(
        _route_kernel,
        name="route",
        grid=(nt,),
        in_specs=[pl.BlockSpec((TM_ROUTE, LANES), lambda j: (j, 0)),
                  pl.BlockSpec((8, LANES), lambda j: (0, 0))],
        out_specs=[pl.BlockSpec((TM_ROUTE, LANES), lambda j: (j, 0)),
                   pl.BlockSpec((nb_pad, LANES), lambda j: (0, 0)),
                   pl.BlockSpec((8, LANES), lambda j: (0, 0))],
        out_shape=[jax.ShapeDtypeStruct((n_tokens, LANES), jnp.int32),
                   jax.ShapeDtypeStruct((nb_pad, LANES), jnp.int32),
                   jax.ShapeDtypeStruct((8, LANES), jnp.int32)],
        scratch_shapes=[pltpu.VMEM((8, LANES), jnp.float32),
                        pltpu.VMEM((TM_ROUTE, TM_ROUTE), jnp.bfloat16)],
        compiler_params=pltpu.CompilerParams(dimension_semantics=("arbitrary",),
                                             vmem_limit_bytes=VMEM_LIMIT),
    )(ti, cnt)
    pos_kmajor = pos[:, :TOP_K].T
    return pos_kmajor, n_blocks * TM_MOE, blk[:n_blocks, 0], nu[0, :1]


def kernel(x, c, w_mod, b_mod, g_pre_mix, w_in, w_gate_lr, b_gate, g_gla, w_out, g_post_mix,
           g_pre_ffn, router_w, router_b, w_gate_up, b_gate_up, w_down, b_down, g_post_ffn):
    B, S, D = x.shape
    T = B * S
    bf = jnp.bfloat16
    cos_t, sin_t = _rope_tables(S)
    for l in range(w_mod.shape[0]):
        wg, wl = _split_gate_up(w_gate_up[l])
        wd = w_down[l]
        mod3 = _mod(c, w_mod[l], b_mod[l]).reshape(B, N_MOD, D)
        wlr = jnp.pad(w_gate_lr[l], ((0, LANES - GLA_GATE_RANK), (0, 0))).astype(bf)
        proj = _in_proj(x, mod3, g_pre_mix[l:l + 1], _pack_w_in(w_in[l]), cos_t, sin_t, wlr,
                        b_gate[l:l + 1])
        n_pat = len(DILATIONS)
        qa, ka, va = proj[:n_pat], proj[n_pat:2 * n_pat], proj[2 * n_pat:3 * n_pat]
        qg, kg, vg, sr, la = proj[3 * n_pat:]
        pats = [_attn_pattern(qa[j], ka[j], va[j], d) for j, d in enumerate(DILATIONS)]
        og = _gla(qg, kg, vg, la, sr, g_gla[l:l + 1])
        rw = jnp.pad(router_w[l], ((0, 0), (0, LANES - N_EXPERTS)))
        rb = jnp.pad(router_b[l], (0, LANES - N_EXPERTS), constant_values=NEG_BIG)[None]
        x1, h2, ti, tg, cnt = _out_proj([p[0] for p in pats], [p[1] for p in pats], og, x, mod3,
                                        w_out[l].astype(bf), g_post_mix[l:l + 1],
                                        g_pre_ffn[l:l + 1], rw, rb)
        pos_kmajor, n_slots, blk_e, n_used = _route(ti.reshape(T, LANES), cnt, T)
        xs = _scatter_rows(h2.reshape(T, D // 2), pos_kmajor, n_slots)
        out_buf = _moe(xs, blk_e, n_used, wg, wl,
                       b_gate_up[l][:, None, 0::2], b_gate_up[l][:, None, 1::2],
                       wd, b_down[l][:, None, :])
        out = None
        tp = T // FINAL_PARTS
        for part in range(FINAL_PARTS):
            yk = _gather_rows(out_buf, pos_kmajor[:, part * tp:(part + 1) * tp].reshape(-1))
            out = _final(x1.reshape(T, D), yk, tg.reshape(T, LANES), mod3, g_post_ffn[l:l + 1],
                         S, part, out)
        x = out.reshape(B, S, D)
    return x
```

```python
import functools

import numpy as np
import jax
import jax.numpy as jnp
from jax import lax
from jax.experimental import pallas as pl
from jax.experimental.pallas import tpu as pltpu
from jax.experimental.pallas import tpu_sc as plsc

D_MODEL = 1024
ATT_HEADS = 8
ATT_HEAD_DIM = 64
ATT_WIDTH = ATT_HEADS * ATT_HEAD_DIM
DILATIONS = (1, 4, 16)
ATT_SPAN = 128
ROPE_THETA = 10000.0
GLA_HEADS = 4
GLA_KEY_DIM = 64
GLA_VAL_DIM = 128
GLA_KEY_WIDTH = GLA_HEADS * GLA_KEY_DIM
GLA_VAL_WIDTH = GLA_HEADS * GLA_VAL_DIM
GLA_GATE_RANK = 16
GLA_GATE_NORMALIZER = 16.0
GLA_CHUNK = 64
N_EXPERTS = 32
TOP_K = 4
D_FF = D_MODEL
SWIGLU_LIMIT = 7.0
SWIGLU_ALPHA = 1.702
NORM_EPS = 1e-6
N_MOD = 6

LANES = 128
NEG_BIG = -1e30

TM_PROJ = 512
TQ_ATT = 2048
TG_GLA = 512
GLA_UNROLL = 8
TM_OUT = 512
OUT_SUBTILES = 1
TM_MOE = 512
TM_FIN = 1024
TN_MOD = 1024
TK_SPLIT = 512
TM_ROUTE = 1024
FINAL_PARTS = 4
SC_INDEX_WINDOW = 128
SC_GATHER_WINDOW = 64
VMEM_LIMIT = 48 * 1024 * 1024

_C_QA, _C_KA, _C_VA = 0, 512, 1024
_C_QG, _C_KG, _C_VG, _C_RG, _C_LR = 1536, 1792, 2048, 2560, 3072
D_IN_PACKED = 3200


def _rms(x, g):
    return x * lax.rsqrt(jnp.mean(x * x, axis=-1, keepdims=True) + NORM_EPS) * g


def _pack_bf16_pairs(x):
    n = x.shape[1] // 2
    u = lax.bitcast_convert_type(x.astype(jnp.bfloat16).astype(jnp.float32), jnp.uint32)
    return (u[:, :n] >> 16) | (u[:, n:] & jnp.uint32(0xFFFF0000))


def _unpack_bf16_pairs(w):
    lo = lax.bitcast_convert_type(w << 16, jnp.float32)
    hi = lax.bitcast_convert_type(w & jnp.uint32(0xFFFF0000), jnp.float32)
    return lo, hi


def _gather_rows(data, idx):
    n_rows = idx.shape[0]
    width = data.shape[1]
    mesh = plsc.VectorSubcoreMesh(core_axis_name="core", subcore_axis_name="subcore")
    n_workers = mesh.num_cores * mesh.num_subcores
    per_worker = n_rows // n_workers
    assert per_worker * n_workers == n_rows and per_worker % SC_INDEX_WINDOW == 0
    halves = SC_INDEX_WINDOW // SC_GATHER_WINDOW

    @pl.kernel(out_type=jax.ShapeDtypeStruct((n_rows, width), data.dtype), mesh=mesh,
               name="gather_rows",
               scratch_types=[pltpu.VMEM((1, SC_INDEX_WINDOW), jnp.int32),
                              pltpu.VMEM((SC_GATHER_WINDOW, width), data.dtype)])
    def gather(x_hbm, i_hbm, o_hbm, idx_vmem, rows_vmem):
        worker = lax.axis_index("core") * mesh.num_subcores + lax.axis_index("subcore")
        base = worker * per_worker

        @pl.loop(0, per_worker // SC_INDEX_WINDOW)
        def _(j):
            off = base + j * SC_INDEX_WINDOW
            pltpu.sync_copy(i_hbm.at[:, pl.ds(off, SC_INDEX_WINDOW)], idx_vmem)
            for h in range(halves):
                part = idx_vmem.at[0, pl.ds(h * SC_GATHER_WINDOW, SC_GATHER_WINDOW)]
                pltpu.sync_copy(x_hbm.at[part], rows_vmem)
                pltpu.sync_copy(rows_vmem,
                                o_hbm.at[pl.ds(off + h * SC_GATHER_WINDOW, SC_GATHER_WINDOW)])

    return gather(data, idx.reshape(1, n_rows))


def _scatter_rows(data, idx, n_out):
    n_copies, n_rows = idx.shape
    width = data.shape[1]
    mesh = plsc.VectorSubcoreMesh(core_axis_name="core", subcore_axis_name="subcore")
    n_workers = mesh.num_cores * mesh.num_subcores
    per_worker = n_rows // n_workers
    assert per_worker * n_workers == n_rows and per_worker % SC_INDEX_WINDOW == 0
    halves = SC_INDEX_WINDOW // SC_GATHER_WINDOW

    @pl.kernel(out_type=jax.ShapeDtypeStruct((n_out, width), data.dtype), mesh=mesh,
               name="scatter_rows",
               scratch_types=[pltpu.VMEM((n_copies, SC_INDEX_WINDOW), jnp.int32),
                              pltpu.VMEM((SC_GATHER_WINDOW, width), data.dtype)])
    def scatter(x_hbm, i_hbm, o_hbm, idx_vmem, rows_vmem):
        worker = lax.axis_index("core") * mesh.num_subcores + lax.axis_index("subcore")
        base = worker * per_worker

        @pl.loop(0, per_worker // SC_INDEX_WINDOW)
        def _(j):
            off = base + j * SC_INDEX_WINDOW
            pltpu.sync_copy(i_hbm.at[:, pl.ds(off, SC_INDEX_WINDOW)], idx_vmem)
            for h in range(halves):
                pltpu.sync_copy(x_hbm.at[pl.ds(off + h * SC_GATHER_WINDOW, SC_GATHER_WINDOW)],
                                rows_vmem)
                for k in range(n_copies):
                    part = idx_vmem.at[k, pl.ds(h * SC_GATHER_WINDOW, SC_GATHER_WINDOW)]
                    pltpu.sync_copy(rows_vmem, o_hbm.at[part])

    return scatter(data, idx)


def _mod_kernel(ct_ref, w_ref, b_ref, o_ref):
    ct = ct_ref[...]
    s = ct * (1.0 / (1.0 + jnp.exp(-ct)))
    w = w_ref[...]
    rows = [jnp.sum(s[:, b:b + 1] * w, axis=0, keepdims=True) for b in range(ct.shape[1])]
    o_ref[...] = jnp.concatenate(rows, axis=0) + b_ref[...]


def _mod(c, w_mod, b_mod):
    B = c.shape[0]
    n = w_mod.shape[1]
    return pl.pallas_call(
        _mod_kernel,
        name="mod",
        grid=(n // TN_MOD,),
        in_specs=[pl.BlockSpec((D_MODEL, B), lambda j: (0, 0)),
                  pl.BlockSpec((D_MODEL, TN_MOD), lambda j: (0, j)),
                  pl.BlockSpec((1, TN_MOD), lambda j: (0, j))],
        out_specs=pl.BlockSpec((B, TN_MOD), lambda j: (0, j)),
        out_shape=jax.ShapeDtypeStruct((B, n), jnp.float32),
        compiler_params=pltpu.CompilerParams(dimension_semantics=("arbitrary",),
                                             vmem_limit_bytes=VMEM_LIMIT),
    )(c.T, w_mod, b_mod.reshape(1, n))


def _store_residue_views(scr_ref, scr4_ref, out_refs):
    ns, tm, _ = scr_ref.shape
    w = ns * LANES
    o1_ref, o4_ref, o16_ref = out_refs
    assert DILATIONS == (1, 4, 16)
    q4 = tm // 4
    for s in range(ns):
        o1_ref[0, :, s * LANES:(s + 1) * LANES] = scr_ref[s].astype(o1_ref.dtype)
    for r in range(4):
        for s in range(ns):
            vals = scr_ref[s, pl.ds(r, q4, stride=4), :]
            scr4_ref[s, r * q4:(r + 1) * q4, :] = vals
            c0 = r * w + s * LANES
            o4_ref[0, :, c0:c0 + LANES] = vals.astype(o4_ref.dtype)
    for r in range(16):
        lo, hi = r % 4, r // 4
        for s in range(ns):
            vals = scr4_ref[s, pl.ds(lo * q4 + hi, tm // 16, stride=4), :]
            c0 = r * w + s * LANES
            o16_ref[0, :, c0:c0 + LANES] = vals.astype(o16_ref.dtype)


def _in_proj_kernel(x_ref, mod_ref, g_ref, w_ref, cos_ref, sin_ref, wlr_ref, bg_ref,
                    qa1_ref, qa4_ref, qa16_ref, ka1_ref, ka4_ref, ka16_ref,
                    va1_ref, va4_ref, va16_ref, qg_ref, kg_ref, vg_ref, sr_ref, la_ref,
                    qs_ref, ks_ref, vs_ref, q4_ref, k4_ref, v4_ref):
    x = x_ref[0]
    shift = mod_ref[0, 0:1, :]
    scale = mod_ref[0, 1:2, :]
    h = (_rms(x, g_ref[...] * (1.0 + scale)) + shift).astype(jnp.bfloat16)

    def proj(c0, width):
        return jnp.dot(h, w_ref[:, c0:c0 + width], preferred_element_type=jnp.float32)

    cos = cos_ref[...]
    sin = sin_ref[...]

    def rope(c0, scr_ref, mult):
        p = proj(c0, ATT_WIDTH)
        for s in range(ATT_WIDTH // LANES):
            t = p[:, s * LANES:(s + 1) * LANES]
            r = t * cos + pltpu.roll(t, LANES // 2, axis=1) * sin
            if mult != 1.0:
                r = r * mult
            scr_ref[s] = r

    rope(_C_QA, qs_ref, ATT_HEAD_DIM ** -0.5)
    _store_residue_views(qs_ref, q4_ref, (qa1_ref, qa4_ref, qa16_ref))
    rope(_C_KA, ks_ref, 1.0)
    _store_residue_views(ks_ref, k4_ref, (ka1_ref, ka4_ref, ka16_ref))
    pv = proj(_C_VA, ATT_WIDTH)
    for s in range(ATT_WIDTH // LANES):
        vs_ref[s] = pv[:, s * LANES:(s + 1) * LANES]
    _store_residue_views(vs_ref, v4_ref, (va1_ref, va4_ref, va16_ref))
    qg_ref[0] = proj(_C_QG, GLA_KEY_WIDTH).astype(qg_ref.dtype)
    kg_ref[0] = proj(_C_KG, GLA_KEY_WIDTH).astype(kg_ref.dtype)
    vg_ref[0] = proj(_C_VG, GLA_VAL_WIDTH).astype(vg_ref.dtype)
    r = proj(_C_RG, GLA_VAL_WIDTH)
    sr_ref[0] = (r * (1.0 / (1.0 + jnp.exp(-r)))).astype(sr_ref.dtype)
    lr = proj(_C_LR, LANES).astype(jnp.bfloat16)
    z = jnp.dot(lr, wlr_ref[...], preferred_element_type=jnp.float32) + bg_ref[...]
    log_sig = jnp.minimum(z, 0.0) - jnp.log(1.0 + jnp.exp(-jnp.abs(z)))
    la_ref[0] = log_sig * (1.0 / GLA_GATE_NORMALIZER)


def _in_proj(x, mod3, g_pre, w_packed, cos_t, sin_t, wlr, bg):
    B, S, D = x.shape
    nb = S // TM_PROJ
    tok = lambda w: pl.BlockSpec((1, TM_PROJ, w), lambda b, i: (b, i, 0))
    const = lambda shape: pl.BlockSpec(shape, lambda b, i: tuple(0 for _ in shape))
    bf = jnp.bfloat16
    att_shapes = [jax.ShapeDtypeStruct((B, S // d, d * ATT_WIDTH), bf) for d in DILATIONS] * 3
    att_specs = [pl.BlockSpec((1, TM_PROJ // d, d * ATT_WIDTH), lambda b, i: (b, i, 0))
                 for d in DILATIONS] * 3
    gla_shapes = [jax.ShapeDtypeStruct((B, S, w), dt) for w, dt in
                  [(GLA_KEY_WIDTH, bf), (GLA_KEY_WIDTH, bf),
                   (GLA_VAL_WIDTH, bf), (GLA_VAL_WIDTH, bf), (GLA_KEY_WIDTH, jnp.float32)]]
    return pl.pallas_call(
        _in_proj_kernel,
        name="in_proj",
        grid=(B, nb),
        in_specs=[tok(D),
                  pl.BlockSpec((1, N_MOD, D), lambda b, i: (b, 0, 0)),
                  const((1, D)),
                  const((D, D_IN_PACKED)),
                  pl.BlockSpec((TM_PROJ, LANES), lambda b, i: (i, 0)),
                  pl.BlockSpec((TM_PROJ, LANES), lambda b, i: (i, 0)),
                  const((LANES, GLA_KEY_WIDTH)),
                  const((1, GLA_KEY_WIDTH))],
        out_specs=att_specs + [tok(s.shape[-1]) for s in gla_shapes],
        out_shape=att_shapes + gla_shapes,
        scratch_shapes=[pltpu.VMEM((ATT_WIDTH // LANES, TM_PROJ, LANES), jnp.float32)] * 6,
        compiler_params=pltpu.CompilerParams(dimension_semantics=("arbitrary", "arbitrary"),
                                             vmem_limit_bytes=VMEM_LIMIT),
    )(x, mod3, g_pre, w_packed, cos_t, sin_t, wlr, bg)


def _attn_kernel(q_ref, kp_ref, kc_ref, vp_ref, vc_ref, o_ref, st_ref):
    i = pl.program_id(2)
    tp = kp_ref.shape[1]
    tq = tp
    n_sub = q_ref.shape[1] // tq
    row = lax.broadcasted_iota(jnp.int32, (tq, tp + tq), 0)
    col = lax.broadcasted_iota(jnp.int32, (tq, tp + tq), 1)
    band = (col >= row) & (col <= row + ATT_SPAN)
    bias_inner = jnp.where(band, 0.0, NEG_BIG)
    bias_first = jnp.where(band & ((col >= tp) | (i > 0)), 0.0, NEG_BIG)
    lane = lax.broadcasted_iota(jnp.int32, (tq, LANES), 1)
    q_head = (lane >> 5) & 1
    for sb in range(n_sub):
        rows = slice(sb * tq, (sb + 1) * tq)
        bias = bias_first if sb == 0 else bias_inner
        stat_m = jnp.zeros((tq, LANES), jnp.float32)
        stat_l = jnp.ones((tq, LANES), jnp.float32)
        for hp in range(ATT_HEADS // 2):
            sl = slice(hp * LANES, (hp + 1) * LANES)
            q2 = q_ref[0, rows, sl]
            if sb == 0:
                kcat = jnp.concatenate([kp_ref[0, :, sl], kc_ref[0, rows, sl]], axis=0)
                vcat = jnp.concatenate([vp_ref[0, :, sl], vc_ref[0, rows, sl]], axis=0)
            else:
                kcat = kc_ref[0, (sb - 1) * tq:(sb + 1) * tq, sl]
                vcat = vc_ref[0, (sb - 1) * tq:(sb + 1) * tq, sl]
            outs, dens = [], []
            for hh in range(2):
                qm = jnp.where(q_head == hh, q2, jnp.zeros_like(q2))
                s = lax.dot_general(qm, kcat, (((1,), (1,)), ((), ())),
                                    preferred_element_type=jnp.float32) + bias
                m = jnp.max(s, axis=1, keepdims=True)
                p = jnp.exp(s - m)
                l = jnp.sum(p, axis=1, keepdims=True)
                outs.append(jnp.dot(p.astype(vcat.dtype), vcat, preferred_element_type=jnp.float32))
                dens.append(l)
                stat_m = jnp.where(lane == 2 * hp + hh, m, stat_m)
                stat_l = jnp.where(lane == 2 * hp + hh, l, stat_l)
            first = lane < ATT_HEAD_DIM
            o_pair = jnp.where(first, outs[0], outs[1]) / jnp.where(first, dens[0], dens[1])
            o_ref[0, rows, sl] = o_pair.astype(o_ref.dtype)
        st_ref[0, rows, :] = stat_m + jnp.log(stat_l)


def _attn_pattern(qv, kv, vv, dil):
    B, L, _ = qv.shape
    W = ATT_WIDTH
    tq = min(TQ_ATT, L)
    nq = L // tq
    cur = pl.BlockSpec((1, tq, W), lambda b, r, i: (b, i, r))
    back = tq // ATT_SPAN
    prev = pl.BlockSpec((1, ATT_SPAN, W), lambda b, r, i: (b, jnp.maximum(i * back - 1, 0), r))
    o, st = pl.pallas_call(
        _attn_kernel,
        name=f"attn_d{dil}",
        grid=(B, dil, nq),
        in_specs=[cur, prev, cur, prev, cur],
        out_specs=[cur, pl.BlockSpec((1, tq, LANES), lambda b, r, i: (b, i, r))],
        out_shape=[jax.ShapeDtypeStruct((B, L, dil * W), jnp.bfloat16),
                   jax.ShapeDtypeStruct((B, L, dil * LANES), jnp.float32)],
        compiler_params=pltpu.CompilerParams(
            dimension_semantics=("arbitrary", "arbitrary", "arbitrary"),
            vmem_limit_bytes=VMEM_LIMIT),
    )(qv, kv, kv, vv, vv)
    return o, st


def _cumsum_rows(x):
    n = x.shape[0]
    row = lax.broadcasted_iota(jnp.int32, x.shape, 0)
    s = 1
    while s < n:
        x = x + jnp.where(row >= s, pltpu.roll(x, s, axis=0), 0.0)
        s *= 2
    return x


def _gla_kernel(q_ref, k_ref, v_ref, la_ref, sr_ref, g_ref, o_ref, st_ref):
    @pl.when(pl.program_id(0) == 0)
    def _():
        st_ref[...] = jnp.zeros_like(st_ref)

    C = GLA_CHUNK
    n_batch = q_ref.shape[0]
    n_chunks = q_ref.shape[1] // C
    lane = lax.broadcasted_iota(jnp.int32, (C, LANES), 1)
    lane_sq = lax.broadcasted_iota(jnp.int32, (LANES, LANES), 1)
    tril = (lax.broadcasted_iota(jnp.int32, (C, C), 0)
            >= lax.broadcasted_iota(jnp.int32, (C, C), 1))
    g = g_ref[...]
    bf = jnp.bfloat16

    def chunk(c, carry):
        r0 = pl.multiple_of(c * C, C)
        rows = pl.ds(r0, C)
        for n in range(n_batch):
            for p in range(GLA_HEADS // 2):
                ksl = slice(p * LANES, (p + 1) * LANES)
                b = _cumsum_rows(la_ref[n, rows, ksl])
                b_last = b[C - 1:C, :]
                q = q_ref[n, rows, ksl].astype(jnp.float32) * (GLA_KEY_DIM ** -0.5)
                k = k_ref[n, rows, ksl].astype(jnp.float32)
                q_dec = q * jnp.exp(b)
                k_inv = (k * jnp.exp(-b)).astype(bf)
                k_dec = (k * jnp.exp(b_last - b)).astype(bf)
                st = st_ref[n, p]
                st_b = st.astype(bf)
                ut = []
                for hh in range(2):
                    h = 2 * p + hh
                    vsl = slice(h * GLA_VAL_DIM, (h + 1) * GLA_VAL_DIM)
                    v = v_ref[n, rows, vsl]
                    own = (lane >= GLA_KEY_DIM) if hh else (lane < GLA_KEY_DIM)
                    qm = jnp.where(own, q_dec, 0.0).astype(bf)
                    att = lax.dot_general(qm, k_inv, (((1,), (1,)), ((), ())),
                                          preferred_element_type=jnp.float32)
                    att = jnp.where(tril, att, 0.0).astype(bf)
                    o = jnp.dot(att, v, preferred_element_type=jnp.float32)
                    o = o + lax.dot_general(qm, st_b, (((1,), (1,)), ((), ())),
                                            preferred_element_type=jnp.float32)
                    o = _rms(o, g) * sr_ref[n, rows, vsl].astype(jnp.float32)
                    o_ref[n, rows, vsl] = o.astype(o_ref.dtype)
                    ut.append(lax.dot_general(v, k_dec, (((0,), (0,)), ((), ())),
                                              preferred_element_type=jnp.float32))
                st_ref[n, p] = (st * jnp.exp(b_last)
                                + jnp.where(lane_sq < GLA_KEY_DIM, ut[0], ut[1]))
        return carry

    lax.fori_loop(0, n_chunks, chunk, 0, unroll=GLA_UNROLL)


def _gla(qg, kg, vg, la, sr, g_gla):
    B, S, _ = qg.shape
    tok = lambda w: pl.BlockSpec((B, TG_GLA, w), lambda i: (0, i, 0))
    return pl.pallas_call(
        _gla_kernel,
        name="gla",
        grid=(S // TG_GLA,),
        in_specs=[tok(GLA_KEY_WIDTH), tok(GLA_KEY_WIDTH), tok(GLA_VAL_WIDTH),
                  tok(GLA_KEY_WIDTH), tok(GLA_VAL_WIDTH),
                  pl.BlockSpec((1, GLA_VAL_DIM), lambda i: (0, 0))],
        out_specs=tok(GLA_VAL_WIDTH),
        out_shape=jax.ShapeDtypeStruct((B, S, GLA_VAL_WIDTH), jnp.bfloat16),
        scratch_shapes=[pltpu.VMEM((B, GLA_HEADS // 2, GLA_VAL_DIM, LANES), jnp.float32)],
        compiler_params=pltpu.CompilerParams(dimension_semantics=("arbitrary",),
                                             vmem_limit_bytes=VMEM_LIMIT),
    )(qg, kg, vg, la, sr, g_gla)


def _out_proj_kernel(o1_ref, o2_ref, o3_ref, s1_ref, s2_ref, s3_ref, og_ref, x_ref, mod_ref,
                     wout_ref, gpost_ref, gpre_ref, rw_ref, rb_ref,
                     x1_ref, h2_ref, ti_ref, tg_ref, cnt_ref, oscr_ref, sscr_ref, otmp_ref,
                     stmp_ref):
    tm = x_ref.shape[1]

    @pl.when((pl.program_id(0) == 0) & (pl.program_id(1) == 0))
    def _():
        cnt_ref[...] = jnp.zeros_like(cnt_ref)

    ns = ATT_WIDTH // LANES
    q4 = tm // 4
    for r in range(4):
        rows = pl.ds(r, q4, stride=4)
        for s in range(ns):
            c0 = r * ATT_WIDTH + s * LANES
            oscr_ref[0, s, rows, :] = o2_ref[0, :, c0:c0 + LANES].astype(jnp.float32)
        sscr_ref[0, rows, :] = s2_ref[0, :, r * LANES:(r + 1) * LANES]
    for r in range(16):
        lo, hi = r % 4, r // 4
        rows = pl.ds(lo * q4 + hi, tm // 16, stride=4)
        for s in range(ns):
            c0 = r * ATT_WIDTH + s * LANES
            otmp_ref[s, rows, :] = o3_ref[0, :, c0:c0 + LANES].astype(jnp.float32)
        stmp_ref[rows, :] = s3_ref[0, :, r * LANES:(r + 1) * LANES]
    for r in range(4):
        rows = pl.ds(r, q4, stride=4)
        for s in range(ns):
            oscr_ref[1, s, rows, :] = otmp_ref[s, r * q4:(r + 1) * q4, :]
        sscr_ref[1, rows, :] = stmp_ref[r * q4:(r + 1) * q4, :]
    gate1 = mod_ref[0, 2:3, :]
    shift2 = mod_ref[0, 3:4, :]
    scale2 = mod_ref[0, 4:5, :]
    bf = jnp.bfloat16
    rw = rw_ref[...]
    w_hi = rw.astype(bf)
    w_lo = (rw - w_hi.astype(jnp.float32)).astype(bf)
    ts = tm // OUT_SUBTILES
    lane = lax.broadcasted_iota(jnp.int32, (ts, LANES), 1)
    lane_f = lane.astype(jnp.float32)
    for sub in range(OUT_SUBTILES):
        rs = slice(sub * ts, (sub + 1) * ts)
        lses = [s1_ref[0, rs, :], sscr_ref[0, rs, :], sscr_ref[1, rs, :]]
        m = jnp.maximum(jnp.maximum(lses[0], lses[1]), lses[2])
        es = [jnp.exp(t - m) for t in lses]
        inv = 1.0 / (es[0] + es[1] + es[2])
        ws = [e * inv for e in es]
        pairs = []
        for hp in range(ATT_HEADS // 2):
            sl = slice(hp * LANES, (hp + 1) * LANES)
            o_pats = [o1_ref[0, rs, sl].astype(jnp.float32), oscr_ref[0, hp, rs, :],
                      oscr_ref[1, hp, rs, :]]
            acc = jnp.zeros((ts, LANES), jnp.float32)
            head_of_lane = 2 * hp + (lane >> 6)
            for w, o in zip(ws, o_pats):
                acc = acc + jnp.take_along_axis(w, head_of_lane, axis=1) * o
            pairs.append(acc.astype(bf))
        mixed = jnp.concatenate(pairs + [og_ref[0, rs, :]], axis=1)
        y = jnp.dot(mixed, wout_ref[...], preferred_element_type=jnp.float32)
        x1 = x_ref[0, rs, :] + _rms(y, gate1 * gpost_ref[...])
        x1_ref[0, rs, :] = x1
        h2 = _rms(x1, gpre_ref[...] * (1.0 + scale2)) + shift2
        h2_ref[0, rs, :] = _pack_bf16_pairs(h2)
        h_hi = h2.astype(bf)
        h_lo = (h2 - h_hi.astype(jnp.float32)).astype(bf)
        logits = (jnp.dot(h_hi, w_hi, preferred_element_type=jnp.float32)
                  + jnp.dot(h_lo, w_hi, preferred_element_type=jnp.float32)
                  + jnp.dot(h_hi, w_lo, preferred_element_type=jnp.float32)) + rb_ref[...]
        vals, idxs = [], []
        for _ in range(TOP_K):
            mk = jnp.max(logits, axis=1, keepdims=True)
            ik = jnp.min(jnp.where(logits == mk, lane_f, float(LANES)), axis=1, keepdims=True)
            logits = jnp.where(lane_f == ik, -jnp.inf, logits)
            vals.append(mk)
            idxs.append(ik)
        ex = [jnp.exp(v - vals[0]) for v in vals]
        den = ex[0] + ex[1] + ex[2] + ex[3]
        ti = jnp.zeros((ts, LANES), jnp.float32)
        tg = jnp.zeros((ts, LANES), jnp.float32)
        for kk in range(TOP_K):
            ti = jnp.where(lane == kk, idxs[kk], ti)
            tg = jnp.where(lane == kk, ex[kk] / den, tg)
        ti_ref[0, rs, :] = ti.astype(jnp.int32)
        tg_ref[0, rs, :] = tg
        chosen = jnp.zeros((ts, LANES), jnp.float32)
        for ik in idxs:
            chosen = chosen + jnp.where(lane_f == ik, 1.0, 0.0)
        cnt_ref[...] += jnp.sum(chosen, axis=0, keepdims=True)


def _out_proj(o_pats, st_pats, og, x, mod3, w_out, g_post, g_pre, rw, rb):
    B, S, D = x.shape
    tok = lambda w: pl.BlockSpec((1, TM_OUT, w), lambda b, i: (b, i, 0))
    const = lambda shape: pl.BlockSpec(shape, lambda b, i: tuple(0 for _ in shape))
    out_shapes = [jax.ShapeDtypeStruct((B, S, D), jnp.float32),
                  jax.ShapeDtypeStruct((B, S, D // 2), jnp.uint32),
                  jax.ShapeDtypeStruct((B, S, LANES), jnp.int32),
                  jax.ShapeDtypeStruct((B, S, LANES), jnp.float32),
                  jax.ShapeDtypeStruct((8, LANES), jnp.float32)]
    return pl.pallas_call(
        _out_proj_kernel,
        name="out_proj",
        grid=(B, S // TM_OUT),
        in_specs=[pl.BlockSpec((1, TM_OUT // d, d * ATT_WIDTH), lambda b, i: (b, i, 0))
                  for d in DILATIONS]
                 + [pl.BlockSpec((1, TM_OUT // d, d * LANES), lambda b, i: (b, i, 0))
                    for d in DILATIONS]
                 + [tok(GLA_VAL_WIDTH), tok(D),
                  pl.BlockSpec((1, N_MOD, D), lambda b, i: (b, 0, 0)),
                  const((D, D)), const((1, D)), const((1, D)),
                  const((D, LANES)), const((1, LANES))],
        out_specs=[tok(D), tok(D // 2), tok(LANES), tok(LANES), const((8, LANES))],
        out_shape=out_shapes,
        scratch_shapes=[pltpu.VMEM((2, ATT_WIDTH // LANES, TM_OUT, LANES), jnp.float32),
                        pltpu.VMEM((2, TM_OUT, LANES), jnp.float32),
                        pltpu.VMEM((ATT_WIDTH // LANES, TM_OUT, LANES), jnp.float32),
                        pltpu.VMEM((TM_OUT, LANES), jnp.float32)],
        compiler_params=pltpu.CompilerParams(dimension_semantics=("arbitrary", "arbitrary"),
                                             vmem_limit_bytes=VMEM_LIMIT),
    )(*o_pats, *st_pats, og, x, mod3, w_out, g_post, g_pre, rw, rb)


def _split_gate_up_kernel(w_ref, wg_ref, wl_ref):
    group = 2 * LANES
    src = lax.broadcasted_iota(jnp.int32, (group, group), 0)
    dst = lax.broadcasted_iota(jnp.int32, (group, group), 1)
    want = jnp.where(dst < LANES, 2 * dst, 2 * (dst - LANES) + 1)
    perm = jnp.where(src == want, 1.0, 0.0).astype(jnp.bfloat16)
    for j in range(w_ref.shape[1] // group):
        t = jnp.dot(w_ref[:, j * group:(j + 1) * group].astype(jnp.bfloat16), perm,
                    preferred_element_type=jnp.float32)
        wg_ref[:, j * LANES:(j + 1) * LANES] = t[:, :LANES].astype(wg_ref.dtype)
        wl_ref[:, j * LANES:(j + 1) * LANES] = t[:, LANES:].astype(wl_ref.dtype)


def _split_gate_up(w_gate_up):
    E, K, N2 = w_gate_up.shape
    out = jax.ShapeDtypeStruct((E, K, N2 // 2), jnp.bfloat16)
    ospec = pl.BlockSpec((None, TK_SPLIT, N2 // 2), lambda e, i: (e, i, 0))
    return pl.pallas_call(
        _split_gate_up_kernel,
        name="split_gate_up",
        grid=(E, K // TK_SPLIT),
        in_specs=[pl.BlockSpec((None, TK_SPLIT, N2), lambda e, i: (e, i, 0))],
        out_specs=[ospec, ospec],
        out_shape=[out, out],
        compiler_params=pltpu.CompilerParams(dimension_semantics=("arbitrary", "arbitrary"),
                                             vmem_limit_bytes=VMEM_LIMIT),
    )(w_gate_up)


def _moe_kernel(be_ref, nu_ref, x_ref, wg_ref, wl_ref, bg_ref, bl_ref, wd_ref, bd_ref, o_ref,
                wd_bf_ref):
    i = pl.program_id(0)
    new_expert = (i == 0) | (be_ref[i] != be_ref[jnp.maximum(i - 1, 0)])

    @pl.when(new_expert & (i < nu_ref[0]))
    def _():
        wd_bf_ref[...] = wd_ref[...].astype(wd_bf_ref.dtype)

    @pl.when(i < nu_ref[0])
    def _():
        lo, hi = _unpack_bf16_pairs(x_ref[...])
        x = jnp.concatenate([lo, hi], axis=1).astype(jnp.bfloat16)
        xg = jnp.dot(x, wg_ref[...], preferred_element_type=jnp.float32) + bg_ref[...]
        xl = jnp.dot(x, wl_ref[...], preferred_element_type=jnp.float32) + bl_ref[...]
        xg = jnp.minimum(xg, SWIGLU_LIMIT)
        xl = jnp.clip(xl, -SWIGLU_LIMIT, SWIGLU_LIMIT)
        act = xg * (1.0 / (1.0 + jnp.exp(-SWIGLU_ALPHA * xg))) * (xl + 1.0)
        out = jnp.dot(act.astype(jnp.bfloat16), wd_bf_ref[...],
                      preferred_element_type=jnp.float32) + bd_ref[...]
        o_ref[...] = _pack_bf16_pairs(out)


def _moe(xs, blk_e, n_used, wg, wl, bg, bl, wd, bd):
    P, half = xs.shape
    D = 2 * half
    n_blocks = P // TM_MOE
    used = lambda i, nu: jnp.minimum(i, nu[0] - 1)
    wspec = lambda k, n: pl.BlockSpec((None, k, n), lambda i, be, nu: (be[used(i, nu)], 0, 0))
    grid_spec = pltpu.PrefetchScalarGridSpec(
        num_scalar_prefetch=2,
        grid=(n_blocks,),
        in_specs=[pl.BlockSpec((TM_MOE, half), lambda i, be, nu: (used(i, nu), 0)),
                  wspec(D, D_FF), wspec(D, D_FF), wspec(1, D_FF), wspec(1, D_FF),
                  wspec(D_FF, D), wspec(1, D)],
        out_specs=pl.BlockSpec((TM_MOE, half), lambda i, be, nu: (used(i, nu), 0)),
        scratch_shapes=[pltpu.VMEM((D_FF, D), jnp.bfloat16)],
    )
    return pl.pallas_call(
        _moe_kernel,
        name="moe",
        grid_spec=grid_spec,
        out_shape=jax.ShapeDtypeStruct((P, half), jnp.uint32),
        compiler_params=pltpu.CompilerParams(dimension_semantics=("arbitrary",),
                                             vmem_limit_bytes=VMEM_LIMIT),
    )(blk_e, n_used, xs, wg, wl, bg, bl, wd, bd)


def _final_kernel(x1_ref, y0_ref, y1_ref, y2_ref, y3_ref, tg_ref, mod_ref, g_ref, o_ref):
    gates = tg_ref[...]
    half = y0_ref.shape[1]
    y_lo = jnp.zeros((x1_ref.shape[0], half), jnp.float32)
    y_hi = jnp.zeros((x1_ref.shape[0], half), jnp.float32)
    for kk, yk_ref in enumerate((y0_ref, y1_ref, y2_ref, y3_ref)):
        lo, hi = _unpack_bf16_pairs(yk_ref[...])
        y_lo = y_lo + gates[:, kk:kk + 1] * lo
        y_hi = y_hi + gates[:, kk:kk + 1] * hi
    y = jnp.concatenate([y_lo, y_hi], axis=1)
    gate2 = mod_ref[0, 5:6, :]
    o_ref[...] = x1_ref[...] + _rms(y, gate2 * g_ref[...])


def _final_part_kernel(x1_ref, y0_ref, y1_ref, y2_ref, y3_ref, tg_ref, mod_ref, g_ref, prev_ref,
                       o_ref):
    del prev_ref
    _final_kernel(x1_ref, y0_ref, y1_ref, y2_ref, y3_ref, tg_ref, mod_ref, g_ref, o_ref)


def _final(x1, yk, tg, mod3, g_post, seq_len, part, prev):
    T, D = x1.shape
    tp = yk.shape[0] // TOP_K
    nt = tp // TM_FIN
    t0 = part * nt
    batch = (part * tp) // seq_len
    yspec = lambda kk: pl.BlockSpec((TM_FIN, D // 2), lambda i: (kk * nt + i, 0))
    in_specs = ([pl.BlockSpec((TM_FIN, D), lambda i: (t0 + i, 0))]
                + [yspec(kk) for kk in range(TOP_K)]
                + [pl.BlockSpec((TM_FIN, LANES), lambda i: (t0 + i, 0)),
                   pl.BlockSpec((1, N_MOD, D), lambda i: (batch, 0, 0)),
                   pl.BlockSpec((1, D), lambda i: (0, 0))])
    args = [x1, yk, yk, yk, yk, tg, mod3, g_post]
    body, aliases = _final_kernel, {}
    if prev is not None:
        in_specs.append(pl.BlockSpec(memory_space=pl.ANY))
        args.append(prev)
        body, aliases = _final_part_kernel, {len(args) - 1: 0}
    return pl.pallas_call(
        body,
        name="final",
        grid=(nt,),
        in_specs=in_specs,
        out_specs=pl.BlockSpec((TM_FIN, D), lambda i: (t0 + i, 0)),
        out_shape=jax.ShapeDtypeStruct((T, D), jnp.float32),
        input_output_aliases=aliases,
        compiler_params=pltpu.CompilerParams(dimension_semantics=("arbitrary",),
                                             vmem_limit_bytes=VMEM_LIMIT),
    )(*args)


def _pack_w_in(w_in):
    half = ATT_HEAD_DIM // 2

    def pair_rotary_layout(w):
        k = w.shape[0]
        w = w.reshape(k, ATT_HEADS // 2, 2, 2, half)
        return w.transpose(0, 1, 3, 2, 4).reshape(k, ATT_WIDTH)

    lr = jnp.pad(w_in[:, 3072:3072 + GLA_GATE_RANK], ((0, 0), (0, LANES - GLA_GATE_RANK)))
    return jnp.concatenate([pair_rotary_layout(w_in[:, :ATT_WIDTH]),
                            pair_rotary_layout(w_in[:, ATT_WIDTH:2 * ATT_WIDTH]),
                            w_in[:, 1024:3072], lr], axis=1).astype(jnp.bfloat16)


def _rope_tables(seq_len):
    half = ATT_HEAD_DIM // 2
    inv_freq = ROPE_THETA ** (-jnp.arange(half, dtype=jnp.float32) / half)
    ang = jnp.arange(seq_len, dtype=jnp.float32)[:, None] * inv_freq[None, :]
    cos = jnp.tile(jnp.cos(ang), (1, LANES // half))
    sin = jnp.tile(jnp.sin(ang), (1, LANES // half))
    sign = jnp.where(jnp.arange(LANES) < LANES // 2, -1.0, 1.0)
    return cos, sin * sign


def _route_kernel(ti_ref, cnt_ref, pos_ref, blk_ref, nu_ref, base_ref, tri_ref):
    j = pl.program_id(0)
    tm = ti_ref.shape[0]
    lane = lax.broadcasted_iota(jnp.int32, (tm, LANES), 1)
    ti = ti_ref[...]
    hots = [lane == ti[:, k:k + 1] for k in range(TOP_K)]
    hot_all = jnp.zeros((tm, LANES), jnp.float32)
    for h in hots:
        hot_all = hot_all + jnp.where(h, 1.0, 0.0)
    tile_cnt = jnp.sum(hot_all, axis=0, keepdims=True)

    @pl.when(j == 0)
    def _():
        r = lax.broadcasted_iota(jnp.int32, (tm, tm), 0)
        c = lax.broadcasted_iota(jnp.int32, (tm, tm), 1)
        tri_ref[...] = jnp.where(c < r, 1.0, 0.0).astype(tri_ref.dtype)
        shift = TM_MOE.bit_length() - 1
        lane8 = lax.broadcasted_iota(jnp.int32, cnt_ref.shape, 1)
        cnt = cnt_ref[...].astype(jnp.int32)
        padded = ((cnt + (TM_MOE - 1)) >> shift) << shift
        pend = padded
        s = 1
        while s < N_EXPERTS:
            pend = pend + jnp.where(lane8 >= s, pltpu.roll(pend, s, axis=1), 0)
            s *= 2
        base_ref[...] = (pend - padded).astype(jnp.float32)
        nb = blk_ref.shape[0]
        blk_start = lax.broadcasted_iota(jnp.int32, (nb, LANES), 0) * TM_MOE
        lane_b = lax.broadcasted_iota(jnp.int32, (nb, LANES), 1)
        done = jnp.where((pend[0:1, :] <= blk_start) & (lane_b < N_EXPERTS), 1.0, 0.0)
        blk = jnp.minimum(jnp.sum(done, axis=1, keepdims=True), float(N_EXPERTS - 1))
        blk_ref[...] = jnp.broadcast_to(blk, blk_ref.shape).astype(jnp.int32)
        total = jnp.max(jnp.where(lane8 < N_EXPERTS, pend, 0).astype(jnp.float32),
                        axis=1, keepdims=True)
        nu_ref[...] = jnp.broadcast_to(total.astype(jnp.int32) >> shift, nu_ref.shape)

    before = jnp.dot(tri_ref[...], hot_all.astype(tri_ref.dtype),
                     preferred_element_type=jnp.float32)
    slot = before + base_ref[0:1, :]
    pos = jnp.zeros((tm, LANES), jnp.float32)
    for k, h in enumerate(hots):
        pk = jnp.sum(jnp.where(h, slot, 0.0), axis=1, keepdims=True)
        pos = jnp.where(lane == k, pk, pos)
    pos_ref[...] = pos.astype(jnp.int32)
    base_ref[...] += tile_cnt


def _route(ti, cnt, n_tokens):
    A = n_tokens * TOP_K
    n_blocks = -(-(A + N_EXPERTS * (TM_MOE - 1)) // TM_MOE)
    nb_pad = -(-n_blocks // 8) * 8
    nt = n_tokens // TM_ROUTE
    pos, blk, nu = pl.pallas_call(
        _route_kernel,
        name="route",
        grid=(nt,),
        in_specs=[pl.BlockSpec((TM_ROUTE, LANES), lambda j: (j, 0)),
                  pl.BlockSpec((8, LANES), lambda j: (0, 0))],
        out_specs=[pl.BlockSpec((TM_ROUTE, LANES), lambda j: (j, 0)),
                   pl.BlockSpec((nb_pad, LANES), lambda j: (0, 0)),
                   pl.BlockSpec((8, LANES), lambda j: (0, 0))],
        out_shape=[jax.ShapeDtypeStruct((n_tokens, LANES), jnp.int32),
                   jax.ShapeDtypeStruct((nb_pad, LANES), jnp.int32),
                   jax.ShapeDtypeStruct((8, LANES), jnp.int32)],
        scratch_shapes=[pltpu.VMEM((8, LANES), jnp.float32),
                        pltpu.VMEM((TM_ROUTE, TM_ROUTE), jnp.bfloat16)],
        compiler_params=pltpu.CompilerParams(dimension_semantics=("arbitrary",),
                                             vmem_limit_bytes=VMEM_LIMIT),
    )(ti, cnt)
    pos_kmajor = pos[:, :TOP_K].T
    return pos_kmajor, n_blocks * TM_MOE, blk[:n_blocks, 0], nu[0, :1]


def kernel(x, c, w_mod, b_mod, g_pre_mix, w_in, w_gate_lr, b_gate, g_gla, w_out, g_post_mix,
           g_pre_ffn, router_w, router_b, w_gate_up, b_gate_up, w_down, b_down, g_post_ffn):
    B, S, D = x.shape
    T = B * S
    bf = jnp.bfloat16
    cos_t, sin_t = _rope_tables(S)
    for l in range(w_mod.shape[0]):
        wg, wl = _split_gate_up(w_gate_up[l])
        wd = w_down[l]
        mod3 = _mod(c, w_mod[l], b_mod[l]).reshape(B, N_MOD, D)
        wlr = jnp.pad(w_gate_lr[l], ((0, LANES - GLA_GATE_RANK), (0, 0))).astype(bf)
        proj = _in_proj(x, mod3, g_pre_mix[l:l + 1], _pack_w_in(w_in[l]), cos_t, sin_t, wlr,
                        b_gate[l:l + 1])
        n_pat = len(DILATIONS)
        qa, ka, va = proj[:n_pat], proj[n_pat:2 * n_pat], proj[2 * n_pat:3 * n_pat]
        qg, kg, vg, sr, la = proj[3 * n_pat:]
        pats = [_attn_pattern(qa[j], ka[j], va[j], d) for j, d in enumerate(DILATIONS)]
        og = _gla(qg, kg, vg, la, sr, g_gla[l:l + 1])
        rw = jnp.pad(router_w[l], ((0, 0), (0, LANES - N_EXPERTS)))
        rb = jnp.pad(router_b[l], (0, LANES - N_EXPERTS), constant_values=NEG_BIG)[None]
        x1, h2, ti, tg, cnt = _out_proj([p[0] for p in pats], [p[1] for p in pats], og, x, mod3,
                                        w_out[l].astype(bf), g_post_mix[l:l + 1],
                                        g_pre_ffn[l:l + 1], rw, rb)
        pos_kmajor, n_slots, blk_e, n_used = _route(ti.reshape(T, LANES), cnt, T)
        xs = _scatter_rows(h2.reshape(T, D // 2), pos_kmajor, n_slots)
        out_buf = _moe(xs, blk_e, n_used, wg, wl,
                       b_gate_up[l][:, None, 0::2], b_gate_up[l][:, None, 1::2],
                       wd, b_down[l][:, None, :])
        out = None
        tp = T // FINAL_PARTS
        for part in range(FINAL_PARTS):
            yk = _gather_rows(out_buf, pos_kmajor[:, part * tp:(part + 1) * tp].reshape(-1))
            out = _final(x1.reshape(T, D), yk, tg.reshape(T, LANES), mod3, g_post_ffn[l:l + 1],
                         S, part, out)
        x = out.reshape(B, S, D)
    return x
```
